```python
import math
import jax, jax.numpy as jnp
from jax import lax
import numpy as np

D_MODEL = 1024
BATCH = 8
SEQ = 4096
DEPTH = 2

HEAD_DIM = 64
SB_HEADS = 8
SW_HEADS = 8
SW_KV_HEADS = 2
SW_GROUP = SW_HEADS // SW_KV_HEADS
WINDOW = 128
BLOCK = 128
D_FF = 4 * D_MODEL
PLE_DIM = 256
N_BUCKETS = 32
MAX_DISTANCE = 128
EPS = 1e-6

SB_W = SB_HEADS * HEAD_DIM
SW_QW = SW_HEADS * HEAD_DIM
SW_KVW = SW_KV_HEADS * HEAD_DIM
IN_COLS = 3 * SB_W + SW_QW + 2 * SW_KVW + 2 * D_MODEL

kernel_name = "stick_breaking_swa_sink_hybrid_block"


def rmsnorm(x, g):
    xf = x.astype(jnp.float32)
    r = lax.rsqrt(jnp.mean(xf * xf, axis=-1, keepdims=True) + EPS)
    return (xf * r).astype(x.dtype) * g


def t5_causal_bucket(dist):
    max_exact = N_BUCKETS // 2
    d = jnp.maximum(dist, 0)
    df = jnp.maximum(d, 1).astype(jnp.float32)
    large = max_exact + (jnp.log(df / max_exact) / math.log(MAX_DISTANCE / max_exact)
                         * (N_BUCKETS - max_exact)).astype(jnp.int32)
    large = jnp.minimum(large, N_BUCKETS - 1)
    return jnp.where(d < max_exact, d, large)


def stick_breaking_attention(q, k, v):
    b_, s_, h_, dh = q.shape
    nb = s_ // BLOCK
    scale = dh ** -0.5
    qb = q.reshape(b_, nb, BLOCK, h_, dh).transpose(1, 0, 3, 2, 4)
    kpos = jnp.arange(s_)

    def one_block(args):
        qblk, n = args
        z = jnp.einsum('bhqd,bshd->bhqs', qblk, k).astype(jnp.float32) * scale
        qpos = n * BLOCK + jnp.arange(BLOCK)
        causal = kpos[None, :] < qpos[:, None]
        log_beta = jax.nn.log_sigmoid(z)
        log_1mb = jnp.where(causal, jax.nn.log_sigmoid(-z), 0.0)
        suffix = lax.cumsum(log_1mb, axis=3, reverse=True) - log_1mb
        a = jnp.where(causal, jnp.exp(log_beta + suffix), 0.0)
        return jnp.einsum('bhqs,bshd->bqhd', a.astype(v.dtype), v)

    o = lax.map(one_block, (qb, jnp.arange(nb)))
    return o.transpose(1, 0, 2, 3, 4).reshape(b_, s_, h_ * dh)


def sliding_window_sink_attention(q, k, v, sinks, bias):
    b_, s_ = q.shape[:2]
    nb = s_ // BLOCK
    scale = HEAD_DIM ** -0.5
    qb = q.reshape(b_, nb, BLOCK, SW_KV_HEADS, SW_GROUP, HEAD_DIM)
    pad = jnp.zeros((b_, BLOCK, SW_KV_HEADS, HEAD_DIM), k.dtype)
    kp = jnp.concatenate([pad, k], axis=1).reshape(b_, nb + 1, BLOCK, SW_KV_HEADS, HEAD_DIM)
    vp = jnp.concatenate([pad, v], axis=1).reshape(b_, nb + 1, BLOCK, SW_KV_HEADS, HEAD_DIM)
    kb = jnp.concatenate([kp[:, :-1], kp[:, 1:]], axis=2)
    vb = jnp.concatenate([vp[:, :-1], vp[:, 1:]], axis=2)
    s = jnp.einsum('bnqkgd,bnskd->bnkgqs', qb, kb).astype(jnp.float32) * scale + bias
    i = jnp.arange(BLOCK)[:, None]
    j = jnp.arange(2 * BLOCK)[None, :]
    dist = BLOCK + i - j
    band = (dist >= 0) & (dist < WINDOW)
    key_abs = (jnp.arange(nb)[:, None, None] - 1) * BLOCK + j[None]
    valid = band[None] & (key_abs >= 0)
    s = jnp.where(valid[None, :, None, None], s, -jnp.inf)
    sink = sinks.astype(jnp.float32).reshape(SW_KV_HEADS, SW_GROUP)[None, None, :, :, None, None]
    m = jnp.maximum(jnp.max(s, axis=-1, keepdims=True), sink)
    e = jnp.exp(s - m)
    denom = jnp.sum(e, axis=-1, keepdims=True) + jnp.exp(sink - m)
    pr = (e / denom).astype(v.dtype)
    o = jnp.einsum('bnkgqs,bnskd->bnqkgd', pr, vb)
    return o.reshape(b_, s_, SW_HEADS * HEAD_DIM)


def window_bias(rel_bias):
    i = jnp.arange(BLOCK)[:, None]
    j = jnp.arange(2 * BLOCK)[None, :]
    dist = BLOCK + i - j
    bias = rel_bias[t5_causal_bucket(dist)]
    return bias.transpose(2, 0, 1).reshape(SW_KV_HEADS, SW_GROUP, BLOCK, 2 * BLOCK)


def _fwd_setup_inputs(seed: int = 0) -> dict:
    key = jax.random.key(seed)
    ks = jax.random.split(key, 20)
    f32 = jnp.float32

    def nrm(k, shape, fan_in):
        return jax.random.normal(k, shape, f32) * (fan_in ** -0.5)

    return {
        "x": jax.random.normal(ks[0], (BATCH, SEQ, D_MODEL), f32),
        "p": jax.random.normal(ks[1], (DEPTH, BATCH, SEQ, PLE_DIM), f32),
        "w_in": nrm(ks[2], (DEPTH, D_MODEL, IN_COLS), D_MODEL),
        "w_up_a": nrm(ks[3], (DEPTH, SB_W, D_MODEL), SB_W),
        "w_up_b": nrm(ks[4], (DEPTH, SW_QW, D_MODEL), SW_QW),
        "w_o": nrm(ks[5], (DEPTH, D_MODEL, D_MODEL), D_MODEL),
        "w_ff1": nrm(ks[6], (DEPTH, D_MODEL, D_FF), D_MODEL),
        "w_ff2": nrm(ks[7], (DEPTH, D_FF, D_MODEL), D_FF),
        "w_pe": nrm(ks[8], (DEPTH, PLE_DIM, D_MODEL), PLE_DIM),
        "w_pg": nrm(ks[9], (DEPTH, D_MODEL, D_MODEL), D_MODEL),
        "g_mix": 1.0 + 0.01 * jax.random.normal(ks[10], (DEPTH, D_MODEL), f32),
        "g_mlp": 1.0 + 0.01 * jax.random.normal(ks[11], (DEPTH, D_MODEL), f32),
        "g_pe": 1.0 + 0.01 * jax.random.normal(ks[12], (DEPTH, D_MODEL), f32),
        "g_final": 1.0 + 0.01 * jax.random.normal(ks[13], (D_MODEL,), f32),
        "sinks": 0.5 * jax.random.normal(ks[14], (DEPTH, SW_HEADS), f32),
        "rel_bias": 0.5 * jax.random.normal(ks[15], (N_BUCKETS, SW_HEADS), f32),
    }


def _fwd_reference(x, p, w_in, w_up_a, w_up_b, w_o, w_ff1, w_ff2, w_pe, w_pg,
              g_mix, g_mlp, g_pe, g_final, sinks, rel_bias):
    b_, s_, _ = x.shape
    bias = window_bias(rel_bias)
    o1 = SB_W
    o2 = o1 + SB_W
    o3 = o2 + SB_W
    o4 = o3 + SW_QW
    o5 = o4 + SW_KVW
    o6 = o5 + SW_KVW
    o7 = o6 + D_MODEL
    for i in range(DEPTH):
        h = rmsnorm(x, g_mix[i])
        proj = h @ w_in[i]
        q_a = proj[..., :o1].reshape(b_, s_, SB_HEADS, HEAD_DIM)
        k_a = proj[..., o1:o2].reshape(b_, s_, SB_HEADS, HEAD_DIM)
        v_a = proj[..., o2:o3].reshape(b_, s_, SB_HEADS, HEAD_DIM)
        q_b = proj[..., o3:o4].reshape(b_, s_, SW_HEADS, HEAD_DIM)
        k_b = proj[..., o4:o5].reshape(b_, s_, SW_KV_HEADS, HEAD_DIM)
        v_b = proj[..., o5:o6].reshape(b_, s_, SW_KV_HEADS, HEAD_DIM)
        gate_a = proj[..., o6:o7]
        gate_b = proj[..., o7:]
        y_a = stick_breaking_attention(q_a, k_a, v_a) @ w_up_a[i]
        y_b = sliding_window_sink_attention(q_b, k_b, v_b, sinks[i], bias) @ w_up_b[i]
        merged = jax.nn.sigmoid(gate_a) * y_a + jax.nn.sigmoid(gate_b) * y_b
        x = x + merged @ w_o[i]
        h = rmsnorm(x, g_mlp[i])
        x = x + jnp.square(jax.nn.relu(h @ w_ff1[i])) @ w_ff2[i]
        pe = p[i] @ w_pe[i]
        x = x + pe * jax.nn.sigmoid(rmsnorm(x, g_pe[i]) @ w_pg[i])
    return rmsnorm(x, g_final)


import jax as _jax
import jax.numpy as _jnp

TWIN_FORMAT = 'train_step'
FWD_PARAMS = ['x', 'p', 'w_in', 'w_up_a', 'w_up_b', 'w_o', 'w_ff1', 'w_ff2', 'w_pe', 'w_pg', 'g_mix', 'g_mlp', 'g_pe', 'g_final', 'sinks', 'rel_bias']
TWIN_WEIGHTS = ['w_in', 'w_up_a', 'w_up_b', 'w_o', 'w_ff1', 'w_ff2', 'w_pe', 'w_pg', 'g_mix', 'g_mlp', 'g_pe', 'g_final', 'sinks', 'rel_bias']
TWIN_DIFF_INPUT = 'x'
TWIN_INPUTS = ['x', 'p', 'w_in', 'w_up_a', 'w_up_b', 'w_o', 'w_ff1', 'w_ff2', 'w_pe', 'w_pg', 'g_mix', 'g_mlp', 'g_pe', 'g_final', 'sinks', 'rel_bias', 'loss_target', 'm_w_in', 'm_w_up_a', 'm_w_up_b', 'm_w_o', 'm_w_ff1', 'm_w_ff2', 'm_w_pe', 'm_w_pg', 'm_g_mix', 'm_g_mlp', 'm_g_pe', 'm_g_final', 'm_sinks', 'm_rel_bias', 'v_w_in', 'v_w_up_a', 'v_w_up_b', 'v_w_o', 'v_w_ff1', 'v_w_ff2', 'v_w_pe', 'v_w_pg', 'v_g_mix', 'v_g_mlp', 'v_g_pe', 'v_g_final', 'v_sinks', 'v_rel_bias']
TWIN_OUTPUTS = ['loss', 'grad_x', 'grad_w_in', 'grad_w_up_a', 'grad_w_up_b', 'grad_w_o', 'grad_w_ff1', 'grad_w_ff2', 'grad_w_pe', 'grad_w_pg', 'grad_g_mix', 'grad_g_mlp', 'grad_g_pe', 'grad_g_final', 'grad_sinks', 'grad_rel_bias', 'delta_w_in', 'delta_w_up_a', 'delta_w_up_b', 'delta_w_o', 'delta_w_ff1', 'delta_w_ff2', 'delta_w_pe', 'delta_w_pg', 'delta_g_mix', 'delta_g_mlp', 'delta_g_pe', 'delta_g_final', 'delta_sinks', 'delta_rel_bias', 'new_m_w_in', 'new_m_w_up_a', 'new_m_w_up_b', 'new_m_w_o', 'new_m_w_ff1', 'new_m_w_ff2', 'new_m_w_pe', 'new_m_w_pg', 'new_m_g_mix', 'new_m_g_mlp', 'new_m_g_pe', 'new_m_g_final', 'new_m_sinks', 'new_m_rel_bias', 'new_v_w_in', 'new_v_w_up_a', 'new_v_w_up_b', 'new_v_w_o', 'new_v_w_ff1', 'new_v_w_ff2', 'new_v_w_pe', 'new_v_w_pg', 'new_v_g_mix', 'new_v_g_mlp', 'new_v_g_pe', 'new_v_g_final', 'new_v_sinks', 'new_v_rel_bias']
TWIN_LEAF_KINDS = {'loss': 'loss', 'grad_x': 'grad_x', 'grad_w_in': 'grad_w', 'grad_w_up_a': 'grad_w', 'grad_w_up_b': 'grad_w', 'grad_w_o': 'grad_w', 'grad_w_ff1': 'grad_w', 'grad_w_ff2': 'grad_w', 'grad_w_pe': 'grad_w', 'grad_w_pg': 'grad_w', 'grad_g_mix': 'grad_w', 'grad_g_mlp': 'grad_w', 'grad_g_pe': 'grad_w', 'grad_g_final': 'grad_w', 'grad_sinks': 'grad_w', 'grad_rel_bias': 'grad_w', 'delta_w_in': 'delta_w', 'delta_w_up_a': 'delta_w', 'delta_w_up_b': 'delta_w', 'delta_w_o': 'delta_w', 'delta_w_ff1': 'delta_w', 'delta_w_ff2': 'delta_w', 'delta_w_pe': 'delta_w', 'delta_w_pg': 'delta_w', 'delta_g_mix': 'delta_w', 'delta_g_mlp': 'delta_w', 'delta_g_pe': 'delta_w', 'delta_g_final': 'delta_w', 'delta_sinks': 'delta_w', 'delta_rel_bias': 'delta_w', 'new_m_w_in': 'new_m', 'new_m_w_up_a': 'new_m', 'new_m_w_up_b': 'new_m', 'new_m_w_o': 'new_m', 'new_m_w_ff1': 'new_m', 'new_m_w_ff2': 'new_m', 'new_m_w_pe': 'new_m', 'new_m_w_pg': 'new_m', 'new_m_g_mix': 'new_m', 'new_m_g_mlp': 'new_m', 'new_m_g_pe': 'new_m', 'new_m_g_final': 'new_m', 'new_m_sinks': 'new_m', 'new_m_rel_bias': 'new_m', 'new_v_w_in': 'new_v', 'new_v_w_up_a': 'new_v', 'new_v_w_up_b': 'new_v', 'new_v_w_o': 'new_v', 'new_v_w_ff1': 'new_v', 'new_v_w_ff2': 'new_v', 'new_v_w_pe': 'new_v', 'new_v_w_pg': 'new_v', 'new_v_g_mix': 'new_v', 'new_v_g_mlp': 'new_v', 'new_v_g_pe': 'new_v', 'new_v_g_final': 'new_v', 'new_v_sinks': 'new_v', 'new_v_rel_bias': 'new_v'}


def _forward(args):
    return _fwd_reference(*[args[k] for k in FWD_PARAMS])


def _output_shape():
    def fwd():
        inp = _fwd_setup_inputs(0)
        return _fwd_reference(*[inp[k] for k in FWD_PARAMS])
    out = _jax.eval_shape(fwd)
    return out.shape, out.dtype

N_MICROBATCH = 1
ADAM_LR = 0.001
ADAM_B1 = 0.9
ADAM_B2 = 0.999
ADAM_EPS = 1e-08
ADAM_WD = 0.01
ADAM_STEP = 10
PER_EXAMPLE_BATCH_AXIS = {'x': 0, 'p': 1, 'loss_target': 0}
SHARED_INPUTS = []
_WEIGHT_DTYPES = {'w_in': _jnp.float32, 'w_up_a': _jnp.float32, 'w_up_b': _jnp.float32, 'w_o': _jnp.float32, 'w_ff1': _jnp.float32, 'w_ff2': _jnp.float32, 'w_pe': _jnp.float32, 'w_pg': _jnp.float32, 'g_mix': _jnp.float32, 'g_mlp': _jnp.float32, 'g_pe': _jnp.float32, 'g_final': _jnp.float32, 'sinks': _jnp.float32, 'rel_bias': _jnp.float32}
MOMENT_SCALE = {'w_in': 3.452498e-02, 'w_up_a': 5.291607e-02, 'w_up_b': 2.114242e-02, 'w_o': 5.610791e-02, 'w_ff1': 6.752866e-02, 'w_ff2': 1.264709e-01, 'w_pe': 5.110596e-02, 'w_pg': 2.011899e-02, 'g_mix': 7.185153e-02, 'g_mlp': 1.411449e-01, 'g_pe': 2.020704e-02, 'g_final': 3.262850e+01, 'sinks': 1.592105e-02, 'rel_bias': 3.481840e-02}


def _to_microbatches(a, axis):
    t = _jnp.moveaxis(a, axis, 0)
    t = t.reshape((N_MICROBATCH, t.shape[0] // N_MICROBATCH) + t.shape[1:])
    return _jnp.moveaxis(t, 1, axis + 1)


def setup_inputs(seed: int = 0) -> dict:
    inp = _fwd_setup_inputs(seed)
    key = _jax.random.fold_in(_jax.random.key(seed), 7919)
    shape, _ = _output_shape()
    out = dict(inp)
    out["loss_target"] = _jax.random.normal(_jax.random.fold_in(key, 0), shape, _jnp.float32)
    for i, name in enumerate(TWIN_WEIGHTS):
        w = inp[name].astype(_jnp.float32)
        if MOMENT_SCALE is None:
            s = _jnp.sqrt(_jnp.mean(_jnp.square(w)) + 1e-30)
        else:
            s = MOMENT_SCALE[name]
        km, kv = _jax.random.split(_jax.random.fold_in(key, i + 1))
        out[name] = w
        out["m_" + name] = s * _jax.random.normal(km, w.shape, _jnp.float32)
        out["v_" + name] = (s * s) * _jax.random.uniform(kv, w.shape, _jnp.float32, 0.5, 1.5)
    if N_MICROBATCH > 1:
        for name, axis in PER_EXAMPLE_BATCH_AXIS.items():
            out[name] = _to_microbatches(out[name], axis)
    return {'x': out['x'], 'p': out['p'], 'w_in': out['w_in'], 'w_up_a': out['w_up_a'], 'w_up_b': out['w_up_b'], 'w_o': out['w_o'], 'w_ff1': out['w_ff1'], 'w_ff2': out['w_ff2'], 'w_pe': out['w_pe'], 'w_pg': out['w_pg'], 'g_mix': out['g_mix'], 'g_mlp': out['g_mlp'], 'g_pe': out['g_pe'], 'g_final': out['g_final'], 'sinks': out['sinks'], 'rel_bias': out['rel_bias'], 'loss_target': out['loss_target'], 'm_w_in': out['m_w_in'], 'm_w_up_a': out['m_w_up_a'], 'm_w_up_b': out['m_w_up_b'], 'm_w_o': out['m_w_o'], 'm_w_ff1': out['m_w_ff1'], 'm_w_ff2': out['m_w_ff2'], 'm_w_pe': out['m_w_pe'], 'm_w_pg': out['m_w_pg'], 'm_g_mix': out['m_g_mix'], 'm_g_mlp': out['m_g_mlp'], 'm_g_pe': out['m_g_pe'], 'm_g_final': out['m_g_final'], 'm_sinks': out['m_sinks'], 'm_rel_bias': out['m_rel_bias'], 'v_w_in': out['v_w_in'], 'v_w_up_a': out['v_w_up_a'], 'v_w_up_b': out['v_w_up_b'], 'v_w_o': out['v_w_o'], 'v_w_ff1': out['v_w_ff1'], 'v_w_ff2': out['v_w_ff2'], 'v_w_pe': out['v_w_pe'], 'v_w_pg': out['v_w_pg'], 'v_g_mix': out['v_g_mix'], 'v_g_mlp': out['v_g_mlp'], 'v_g_pe': out['v_g_pe'], 'v_g_final': out['v_g_final'], 'v_sinks': out['v_sinks'], 'v_rel_bias': out['v_rel_bias']}


def _loss(weights, diff, rest, loss_target):
    with _jax.named_scope("forward"):
        args = {**rest, TWIN_DIFF_INPUT: diff, **{k: w.astype(_WEIGHT_DTYPES[k]) for k, w in weights.items()}}
        y = _forward(args)
    with _jax.named_scope("loss_head"):
        err = _jnp.square(y.astype(_jnp.float32) - loss_target)
        return 0.5 * _jnp.sum(_jnp.mean(err, axis=-1)) if err.ndim else 0.5 * err


def _adamw(w, g, m, v):
    m = ADAM_B1 * m + (1.0 - ADAM_B1) * g
    v = ADAM_B2 * v + (1.0 - ADAM_B2) * _jnp.square(g)
    m_hat = m / (1.0 - ADAM_B1 ** ADAM_STEP)
    v_hat = v / (1.0 - ADAM_B2 ** ADAM_STEP)
    delta = -ADAM_LR * (m_hat / (_jnp.sqrt(v_hat) + ADAM_EPS) + ADAM_WD * w)
    return delta, m, v


def reference(x, p, w_in, w_up_a, w_up_b, w_o, w_ff1, w_ff2, w_pe, w_pg, g_mix, g_mlp, g_pe, g_final, sinks, rel_bias, loss_target, m_w_in, m_w_up_a, m_w_up_b, m_w_o, m_w_ff1, m_w_ff2, m_w_pe, m_w_pg, m_g_mix, m_g_mlp, m_g_pe, m_g_final, m_sinks, m_rel_bias, v_w_in, v_w_up_a, v_w_up_b, v_w_o, v_w_ff1, v_w_ff2, v_w_pe, v_w_pg, v_g_mix, v_g_mlp, v_g_pe, v_g_final, v_sinks, v_rel_bias):
    given = dict(x=x, p=p, w_in=w_in, w_up_a=w_up_a, w_up_b=w_up_b, w_o=w_o, w_ff1=w_ff1, w_ff2=w_ff2, w_pe=w_pe, w_pg=w_pg, g_mix=g_mix, g_mlp=g_mlp, g_pe=g_pe, g_final=g_final, sinks=sinks, rel_bias=rel_bias, loss_target=loss_target, m_w_in=m_w_in, m_w_up_a=m_w_up_a, m_w_up_b=m_w_up_b, m_w_o=m_w_o, m_w_ff1=m_w_ff1, m_w_ff2=m_w_ff2, m_w_pe=m_w_pe, m_w_pg=m_w_pg, m_g_mix=m_g_mix, m_g_mlp=m_g_mlp, m_g_pe=m_g_pe, m_g_final=m_g_final, m_sinks=m_sinks, m_rel_bias=m_rel_bias, v_w_in=v_w_in, v_w_up_a=v_w_up_a, v_w_up_b=v_w_up_b, v_w_o=v_w_o, v_w_ff1=v_w_ff1, v_w_ff2=v_w_ff2, v_w_pe=v_w_pe, v_w_pg=v_w_pg, v_g_mix=v_g_mix, v_g_mlp=v_g_mlp, v_g_pe=v_g_pe, v_g_final=v_g_final, v_sinks=v_sinks, v_rel_bias=v_rel_bias)
    weights = {n: given[n] for n in TWIN_WEIGHTS}
    shared = {n: given[n] for n in SHARED_INPUTS}
    per_example = {n: given[n] for n in ['x', 'p']}
    grad_fn = _jax.value_and_grad(_loss, argnums=(0, 1))

    def one_microbatch(ex, loss_target):
        ex = dict(ex)
        diff = ex.pop(TWIN_DIFF_INPUT)
        return grad_fn(weights, diff, {**shared, **ex}, loss_target)

    if N_MICROBATCH == 1:
        loss, (grad_w, grad_x) = one_microbatch(per_example, given["loss_target"])
    else:
        def body(carry, xs):
            loss_sum, grad_sum = carry
            l_k, (gw_k, gx_k) = one_microbatch(xs[0], xs[1])
            with _jax.named_scope("update"):
                return (loss_sum + l_k, _jax.tree.map(_jnp.add, grad_sum, gw_k)), gx_k

        init = (_jnp.zeros((), _jnp.float32), _jax.tree.map(_jnp.zeros_like, weights))
        (loss, grad_w), grad_x = _jax.lax.scan(body, init, (per_example, given["loss_target"]))
    with _jax.named_scope("update"):
        delta_w, new_m, new_v = {}, {}, {}
        for n in TWIN_WEIGHTS:
            delta_w[n], new_m[n], new_v[n] = _adamw(weights[n], grad_w[n], given["m_" + n], given["v_" + n])
    return (loss, grad_x, *[grad_w[n] for n in TWIN_WEIGHTS], *[delta_w[n] for n in TWIN_WEIGHTS],
            *[new_m[n] for n in TWIN_WEIGHTS], *[new_v[n] for n in TWIN_WEIGHTS])
```

```python
import functools
import math

import numpy as np
import jax
import jax.numpy as jnp
from jax import lax
from jax.experimental import pallas as pl
from jax.experimental.pallas import tpu as pltpu

F32 = jnp.float32
BF16 = jnp.bfloat16
MESH = pl.DeviceIdType.MESH

HEAD_DIM = 64
SB_HEADS = 8
SW_HEADS = 8
SW_KV_HEADS = 2
WINDOW = 128
N_BUCKETS = 32
MAX_DISTANCE = 128
EPS = 1e-6
SB_W = SB_HEADS * HEAD_DIM
SW_QW = SW_HEADS * HEAD_DIM
SW_KVW = SW_KV_HEADS * HEAD_DIM
QKV_W = 3 * SB_W + SW_QW + 2 * SW_KVW
SCALE = HEAD_DIM ** -0.5
LANES = 128
TQ = 128
BK = 128
NEG = -1e30

ADAM_LR = 0.001
ADAM_B1 = 0.9
ADAM_B2 = 0.999
ADAM_EPS = 1e-08
ADAM_WD = 0.01
ADAM_STEP = 10

VMEM_LIMIT = 56 * 1024 * 1024


def _dot(a, b):
    return jnp.dot(a, b, preferred_element_type=F32)


def _dot_nt(a, b):
    return lax.dot_general(a, b, (((1,), (1,)), ((), ())), preferred_element_type=F32)


def _dot_tn(a, b):
    return lax.dot_general(a, b, (((0,), (0,)), ((), ())), preferred_element_type=F32)


def _sum_all(x):
    return jnp.sum(jnp.sum(x, axis=1, keepdims=True), axis=0, keepdims=True)


def _sigmoid(x):
    return 1.0 / (1.0 + jnp.exp(-x))


def _rms(x, g):
    r = lax.rsqrt(jnp.mean(x * x, axis=-1, keepdims=True) + EPS)
    return (x * r) * g


def _rms_bwd(dy, x, g):
    r = lax.rsqrt(jnp.mean(x * x, axis=-1, keepdims=True) + EPS)
    n = x * r
    dg = jnp.sum(dy * n, axis=0, keepdims=True)
    dn = dy * g
    dx = r * (dn - n * jnp.mean(dn * n, axis=-1, keepdims=True))
    return dx, dg


def _params(n_axes):
    return pltpu.CompilerParams(dimension_semantics=("arbitrary",) * n_axes, vmem_limit_bytes=VMEM_LIMIT)


def _rowcall(name, body, row_ins, const_ins, row_outs, acc_outs=(), tm=256):
    s = row_ins[0].shape[0]
    assert s % tm == 0
    in_specs = [pl.BlockSpec((tm, a.shape[1]), lambda i: (i, 0)) for a in row_ins]
    in_specs += [pl.BlockSpec(a.shape, functools.partial(lambda i, nd: (0,) * nd, nd=a.ndim)) for a in const_ins]
    out_shape = [jax.ShapeDtypeStruct((s, c), dt) for c, dt in row_outs]
    out_specs = [pl.BlockSpec((tm, c), lambda i: (i, 0)) for c, _ in row_outs]
    out_shape += [jax.ShapeDtypeStruct(sh, dt) for sh, dt in acc_outs]
    out_specs += [pl.BlockSpec(sh, functools.partial(lambda i, nd: (0,) * nd, nd=len(sh))) for sh, _ in acc_outs]
    return pl.pallas_call(body, out_shape=out_shape, grid=(s // tm,), in_specs=in_specs, out_specs=out_specs,
                          compiler_params=_params(1), name=name)(*row_ins, *const_ins)


def _inproj_fwd(x, g, w, name):
    d = x.shape[1]

    def body(x_ref, g_ref, w_ref, h_ref, qkv_ref, gate_ref):
        hb = _rms(x_ref[...], g_ref[...]).astype(BF16)
        h_ref[...] = hb
        qkv_ref[...] = _dot(hb, w_ref[:, :QKV_W]).astype(BF16)
        gate_ref[...] = _dot(hb, w_ref[:, QKV_W:])

    return _rowcall(name, body, [x], [g, w], [(d, BF16), (QKV_W, BF16), (2 * d, F32)])


def _mixer_fwd(oa, ob, gates, x, wua, wub, wo, name):
    d = x.shape[1]

    def body(oa_ref, ob_ref, gate_ref, x_ref, wua_ref, wub_ref, wo_ref, m_ref, x1_ref):
        ya = _dot(oa_ref[...], wua_ref[...])
        yb = _dot(ob_ref[...], wub_ref[...])
        m = _sigmoid(gate_ref[:, :d]) * ya + _sigmoid(gate_ref[:, d:]) * yb
        mb = m.astype(BF16)
        m_ref[...] = mb
        x1_ref[...] = x_ref[...] + _dot(mb, wo_ref[...])

    return _rowcall(name, body, [oa, ob, gates, x], [wua, wub, wo], [(d, BF16), (d, F32)])


def _ff1_fwd(x1, g, w1, name):
    d, dff = w1.shape

    def body(x_ref, g_ref, w_ref, h_ref, u_ref, a_ref):
        hb = _rms(x_ref[...], g_ref[...]).astype(BF16)
        h_ref[...] = hb
        u = _dot(hb, w_ref[...])
        u_ref[...] = u
        a_ref[...] = jnp.square(jnp.maximum(u, 0.0)).astype(BF16)

    return _rowcall(name, body, [x1], [g, w1], [(d, BF16), (dff, F32), (dff, BF16)])


def _ff2_fwd(a, x1, w2, name):
    d = x1.shape[1]

    def body(a_ref, x_ref, w_ref, o_ref):
        o_ref[...] = x_ref[...] + _dot(a_ref[...], w_ref[...])

    return _rowcall(name, body, [a, x1], [w2], [(d, F32)])[0]


def _ple_fwd(p, x2, g, wpe, wpg, name):
    d = x2.shape[1]

    def body(p_ref, x_ref, g_ref, wpe_ref, wpg_ref, pb_ref, h_ref, pe_ref, gt_ref, x3_ref):
        pb = p_ref[...].astype(BF16)
        pb_ref[...] = pb
        pe = _dot(pb, wpe_ref[...])
        x = x_ref[...]
        hb = _rms(x, g_ref[...]).astype(BF16)
        h_ref[...] = hb
        gt = _dot(hb, wpg_ref[...])
        pe_ref[...] = pe
        gt_ref[...] = gt
        x3_ref[...] = x + pe * _sigmoid(gt)

    return _rowcall(name, body, [p, x2], [g, wpe, wpg],
                    [(p.shape[1], BF16), (d, BF16), (d, F32), (d, F32), (d, F32)])


def _pair_stack(t, lane):
    zero = jnp.zeros_like(t)
    return jnp.concatenate([jnp.where(lane < HEAD_DIM, t, zero), jnp.where(lane >= HEAD_DIM, t, zero)], axis=0)


def _sb_consts():
    jj = lax.broadcasted_iota(jnp.int32, (BK, 2 * BK), 0)
    ss = lax.broadcasted_iota(jnp.int32, (BK, 2 * BK), 1)
    suffix = jnp.where((ss >= BK) | (jj > ss), 1.0, 0.0).astype(BF16)
    prefix_incl = jnp.where((ss >= BK) | (jj <= ss), 1.0, 0.0).astype(BF16)
    prefix_excl = jnp.where((ss >= BK) | (jj < ss), 1.0, 0.0).astype(BF16)
    row = lax.broadcasted_iota(jnp.int32, (2 * TQ, BK), 0)
    row = jnp.where(row >= TQ, row - TQ, row)
    col = lax.broadcasted_iota(jnp.int32, (2 * TQ, BK), 1)
    return suffix, prefix_incl, prefix_excl, col - row


def _split_dot(x, m01):
    hi = x.astype(BF16)
    lo = (x - hi.astype(F32)).astype(BF16)
    return _dot(hi, m01) + _dot(lo, m01)


def _sb_scores(qs, k, mask):
    z = _dot_nt(qs, k) * SCALE
    lb = jnp.minimum(z, 0.0) - jnp.log1p(jnp.exp(-jnp.abs(z)))
    lm = jnp.where(mask, lb - z, 0.0)
    return lb, lm


def _sb_fwd(qkv, name):
    s = qkv.shape[0]
    nq = s // TQ

    def body(q_ref, k_ref, v_ref, o_ref, lt_ref, cf_ref, acc_ref):
        i = pl.program_id(1)
        lane = lax.broadcasted_iota(jnp.int32, (TQ, LANES), 1)
        qs = _pair_stack(q_ref[...], lane)
        suffix, _, _, rel = _sb_consts()
        q0 = i * TQ
        cf_ref[...] = jnp.zeros_like(cf_ref)
        acc_ref[...] = jnp.zeros_like(acc_ref)

        def step(it, carry):
            k0 = pl.multiple_of((i - it) * BK, BK)
            k = k_ref[pl.ds(k0, BK), :]
            v = v_ref[pl.ds(k0, BK), :]
            mask = rel < (q0 - k0)
            lb, lm = _sb_scores(qs, k, mask)
            cs = _split_dot(lm, suffix)
            a = jnp.where(mask, jnp.exp(lb + cs[:, :BK] + cf_ref[...]), 0.0)
            acc_ref[...] += _dot(a.astype(BF16), v)
            cf_ref[...] += cs[:, BK:]
            return carry

        lax.fori_loop(0, i + 1, step, 0)
        o_ref[...] = jnp.where(lane < HEAD_DIM, acc_ref[:TQ, :], acc_ref[TQ:, :]).astype(BF16)
        lt_ref[...] = cf_ref[...]

    npair = SB_W // LANES
    return pl.pallas_call(
        body,
        out_shape=[jax.ShapeDtypeStruct((s, SB_W), BF16), jax.ShapeDtypeStruct((npair, nq, 2 * TQ, BK), F32)],
        grid=(npair, nq),
        in_specs=[pl.BlockSpec((TQ, LANES), lambda j, i: (i, j)),
                  pl.BlockSpec((s, LANES), lambda j, i: (0, npair + j)),
                  pl.BlockSpec((s, LANES), lambda j, i: (0, 2 * npair + j))],
        out_specs=[pl.BlockSpec((TQ, LANES), lambda j, i: (i, j)),
                   pl.BlockSpec((None, None, 2 * TQ, BK), lambda j, i: (j, i, 0, 0))],
        scratch_shapes=[pltpu.VMEM((2 * TQ, BK), F32), pltpu.VMEM((2 * TQ, LANES), F32)],
        compiler_params=_params(2), name=name)(qkv, qkv, qkv)


def _sb_bwd(qkv, lt, doa, name):
    s = qkv.shape[0]
    nq = s // TQ

    def body(q_ref, k_ref, v_ref, lt_ref, do_ref, dq_ref, dk_ref, dv_ref, cp_ref, ce_ref, dqa_ref):
        i = pl.program_id(1)
        lane = lax.broadcasted_iota(jnp.int32, (TQ, LANES), 1)
        qs = _pair_stack(q_ref[...], lane)
        dos = _pair_stack(do_ref[...], lane)
        _, prefix_incl, prefix_excl, rel = _sb_consts()
        q0 = i * TQ

        @pl.when(i == 0)
        def _():
            dk_ref[...] = jnp.zeros_like(dk_ref)
            dv_ref[...] = jnp.zeros_like(dv_ref)

        cp_ref[...] = jnp.zeros_like(cp_ref)
        ce_ref[...] = jnp.zeros_like(ce_ref)
        dqa_ref[...] = jnp.zeros_like(dqa_ref)

        def step(it, carry):
            k0 = pl.multiple_of(it * BK, BK)
            k = k_ref[pl.ds(k0, BK), :]
            v = v_ref[pl.ds(k0, BK), :]
            mask = rel < (q0 - k0)
            lb, lm = _sb_scores(qs, k, mask)
            cs = _split_dot(lm, prefix_incl)
            f = lt_ref[...] - (cs[:, :BK] + cp_ref[...])
            a = jnp.where(mask, jnp.exp(lb + f), 0.0)
            e = a * _dot_nt(dos, v)
            ce = _split_dot(e, prefix_excl)
            big_e = ce[:, :BK] + ce_ref[...]
            sg = jnp.exp(lb)
            dz = (jnp.where(mask, e - sg * (e + big_e), 0.0) * SCALE).astype(BF16)
            dqa_ref[...] += _dot(dz, k)
            dk_ref[pl.ds(k0, BK), :] += _dot_tn(dz, qs)
            dv_ref[pl.ds(k0, BK), :] += _dot_tn(a.astype(BF16), dos)
            cp_ref[...] += cs[:, BK:]
            ce_ref[...] += ce[:, BK:]
            return carry

        lax.fori_loop(0, i + 1, step, 0)
        dq_ref[...] = jnp.where(lane < HEAD_DIM, dqa_ref[:TQ, :], dqa_ref[TQ:, :]).astype(BF16)

    npair = SB_W // LANES
    return pl.pallas_call(
        body,
        out_shape=[jax.ShapeDtypeStruct((s, SB_W), BF16), jax.ShapeDtypeStruct((s, SB_W), F32),
                   jax.ShapeDtypeStruct((s, SB_W), F32)],
        grid=(npair, nq),
        in_specs=[pl.BlockSpec((TQ, LANES), lambda j, i: (i, j)),
                  pl.BlockSpec((s, LANES), lambda j, i: (0, npair + j)),
                  pl.BlockSpec((s, LANES), lambda j, i: (0, 2 * npair + j)),
                  pl.BlockSpec((None, None, 2 * TQ, BK), lambda j, i: (j, i, 0, 0)),
                  pl.BlockSpec((TQ, LANES), lambda j, i: (i, j))],
        out_specs=[pl.BlockSpec((TQ, LANES), lambda j, i: (i, j)),
                   pl.BlockSpec((s, LANES), lambda j, i: (0, j)),
                   pl.BlockSpec((s, LANES), lambda j, i: (0, j))],
        scratch_shapes=[pltpu.VMEM((2 * TQ, BK), F32), pltpu.VMEM((2 * TQ, BK), F32),
                        pltpu.VMEM((2 * TQ, LANES), F32)],
        compiler_params=_params(2), name=name)(qkv, qkv, qkv, lt, doa)


def _bucket_table():
    i = np.arange(TQ)[:, None]
    j = np.arange(2 * BK)[None, :]
    dist = np.maximum(TQ + i - j, 0)
    max_exact = N_BUCKETS // 2
    df = np.maximum(dist, 1).astype(np.float32)
    large = max_exact + (np.log(df / np.float32(max_exact)) / np.float32(math.log(MAX_DISTANCE / max_exact))
                         * np.float32(N_BUCKETS - max_exact)).astype(np.int32)
    large = np.minimum(large, N_BUCKETS - 1)
    return np.where(dist < max_exact, dist, large).astype(np.int32)


def _swa_align_in(t, lane, g):
    tf = t.astype(F32)
    tr = pltpu.roll(tf, HEAD_DIM, 1)
    gmask = (lane >= HEAD_DIM) == (g == 1)
    top = jnp.where(gmask, jnp.where(g == 0, tf, tr), 0.0)
    bot = jnp.where(gmask, jnp.where(g == 1, tf, tr), 0.0)
    return jnp.concatenate([top, bot], axis=0).astype(BF16)


def _swa_align_out(t, lane, g):
    top, bot = t[:TQ, :], t[TQ:, :]
    top = jnp.where(g == 0, top, pltpu.roll(top, HEAD_DIM, 1))
    bot = jnp.where(g == 1, bot, pltpu.roll(bot, HEAD_DIM, 1))
    return jnp.where(lane < HEAD_DIM, top, bot)


def _swa_bias(bias_ref, bucket_ref, rb_ref, j):
    for hh in range(2):
        def add(b, acc):
            return acc + jnp.where(bucket_ref[...] == b, rb_ref[b, 2 * j + hh], 0.0)
        bias_ref[hh * TQ:(hh + 1) * TQ, :] = lax.fori_loop(0, N_BUCKETS, add, jnp.zeros((TQ, 2 * BK), F32))


def _swa_probs(qs, k2, bias, sink_ref, i, j):
    s = _dot_nt(qs, k2) * SCALE + bias
    row = lax.broadcasted_iota(jnp.int32, (2 * TQ, 2 * BK), 0)
    col = lax.broadcasted_iota(jnp.int32, (2 * TQ, 2 * BK), 1)
    dist = TQ + jnp.where(row >= TQ, row - TQ, row) - col
    valid = (dist >= 0) & (dist < WINDOW) & ((col >= BK) | (i > 0))
    s = jnp.where(valid, s, NEG)
    row1 = lax.broadcasted_iota(jnp.int32, (2 * TQ, 1), 0)
    sink = jnp.where(row1 < TQ, sink_ref[2 * j], sink_ref[2 * j + 1])
    m = jnp.maximum(jnp.max(s, axis=1, keepdims=True), sink)
    e = jnp.exp(s - m)
    es = jnp.exp(sink - m)
    den = jnp.sum(e, axis=1, keepdims=True) + es
    return e / den, es / den


def _swa_kv(ref, i):
    prev = pl.multiple_of(jnp.maximum(i - 1, 0) * BK, BK)
    cur = pl.multiple_of(i * BK, BK)
    return jnp.concatenate([ref[pl.ds(prev, BK), :], ref[pl.ds(cur, BK), :]], axis=0), prev, cur


def _swa_fwd(qkv, sinks, rel_bias, name):
    s = qkv.shape[0]
    nq = s // TQ
    npair = SW_QW // LANES
    qcol = 3 * SB_W // LANES
    bucket = jnp.asarray(_bucket_table())

    def body(q_ref, k_ref, v_ref, bucket_ref, sink_ref, rb_ref, o_ref, bias_ref):
        j = pl.program_id(0)
        i = pl.program_id(1)
        g = j // 2
        lane = lax.broadcasted_iota(jnp.int32, (TQ, LANES), 1)

        @pl.when(i == 0)
        def _():
            _swa_bias(bias_ref, bucket_ref, rb_ref, j)

        qs = _swa_align_in(q_ref[...], lane, g)
        k2, _, _ = _swa_kv(k_ref, i)
        v2, _, _ = _swa_kv(v_ref, i)
        pr, _ = _swa_probs(qs, k2, bias_ref[...], sink_ref, i, j)
        o_ref[...] = _swa_align_out(_dot(pr.astype(BF16), v2), lane, g).astype(BF16)

    return pl.pallas_call(
        body, out_shape=jax.ShapeDtypeStruct((s, SW_QW), BF16), grid=(npair, nq),
        in_specs=[pl.BlockSpec((TQ, LANES), lambda j, i: (i, qcol + j)),
                  pl.BlockSpec((s, LANES), lambda j, i: (0, qcol + npair)),
                  pl.BlockSpec((s, LANES), lambda j, i: (0, qcol + npair + 1)),
                  pl.BlockSpec((TQ, 2 * BK), lambda j, i: (0, 0)),
                  pl.BlockSpec(memory_space=pltpu.SMEM),
                  pl.BlockSpec(memory_space=pltpu.SMEM)],
        out_specs=pl.BlockSpec((TQ, LANES), lambda j, i: (i, j)),
        scratch_shapes=[pltpu.VMEM((2 * TQ, 2 * BK), F32)],
        compiler_params=_params(2), name=name)(qkv, qkv, qkv, bucket, sinks, rel_bias)


def _swa_bwd(qkv, ob, dob, sinks, rel_bias, name):
    s = qkv.shape[0]
    nq = s // TQ
    npair = SW_QW // LANES
    qcol = 3 * SB_W // LANES
    bucket = jnp.asarray(_bucket_table())

    def body(q_ref, k_ref, v_ref, o_ref, do_ref, bucket_ref, sink_ref, rb_ref,
             dq_ref, dk_ref, dv_ref, dsink_ref, drel_ref, bias_ref, dsacc_ref):
        j = pl.program_id(0)
        i = pl.program_id(1)
        g = j // 2
        lane = lax.broadcasted_iota(jnp.int32, (TQ, LANES), 1)
        row8 = lax.broadcasted_iota(jnp.int32, (SW_HEADS, LANES), 0)
        lane8 = lax.broadcasted_iota(jnp.int32, (SW_HEADS, LANES), 1)

        @pl.when((i == 0) & (j == 0))
        def _():
            dk_ref[...] = jnp.zeros_like(dk_ref)
            dv_ref[...] = jnp.zeros_like(dv_ref)
            dsink_ref[...] = jnp.zeros_like(dsink_ref)
            drel_ref[...] = jnp.zeros_like(drel_ref)

        @pl.when(i == 0)
        def _():
            _swa_bias(bias_ref, bucket_ref, rb_ref, j)
            dsacc_ref[...] = jnp.zeros_like(dsacc_ref)

        qs = _swa_align_in(q_ref[...], lane, g)
        do = do_ref[...]
        dos = _swa_align_in(do, lane, g)
        dof = do.astype(F32) * o_ref[...].astype(F32)
        d0 = jnp.sum(jnp.where(lane < HEAD_DIM, dof, 0.0), axis=1, keepdims=True)
        d1 = jnp.sum(jnp.where(lane >= HEAD_DIM, dof, 0.0), axis=1, keepdims=True)
        delta = jnp.concatenate([d0, d1], axis=0)
        k2, prev, cur = _swa_kv(k_ref, i)
        v2, _, _ = _swa_kv(v_ref, i)
        pr, psink = _swa_probs(qs, k2, bias_ref[...], sink_ref, i, j)
        ds = pr * (_dot_nt(dos, v2) - delta)
        dsacc_ref[...] += ds
        sd = psink * delta
        ds0 = -jnp.sum(sd[:TQ, :], axis=0, keepdims=True)
        ds1 = -jnp.sum(sd[TQ:, :], axis=0, keepdims=True)
        dsink_ref[...] += jnp.where(row8 == 2 * j, ds0, jnp.where(row8 == 2 * j + 1, ds1, 0.0))
        dsb = ds.astype(BF16)
        dq_ref[...] = _swa_align_out(_dot(dsb, k2) * SCALE, lane, g).astype(BF16)
        dk2 = _dot_tn(dsb, qs) * SCALE
        dv2 = _dot_tn(pr.astype(BF16), dos)
        dk_ref[pl.ds(prev, BK), :] += dk2[:BK, :]
        dk_ref[pl.ds(cur, BK), :] += dk2[BK:, :]
        dv_ref[pl.ds(prev, BK), :] += dv2[:BK, :]
        dv_ref[pl.ds(cur, BK), :] += dv2[BK:, :]

        @pl.when(i == nq - 1)
        def _():
            for hh in range(2):
                def red(b, acc):
                    val = _sum_all(jnp.where(bucket_ref[...] == b, dsacc_ref[hh * TQ:(hh + 1) * TQ, :], 0.0))
                    return jnp.where((row8 == 2 * j + hh) & (lane8 == b), val, acc)
                drel_ref[...] += lax.fori_loop(0, N_BUCKETS, red, jnp.zeros((SW_HEADS, LANES), F32))

    whole = lambda j, i: (0, 0)
    return pl.pallas_call(
        body,
        out_shape=[jax.ShapeDtypeStruct((s, SW_QW), BF16), jax.ShapeDtypeStruct((s, LANES), F32),
                   jax.ShapeDtypeStruct((s, LANES), F32), jax.ShapeDtypeStruct((SW_HEADS, LANES), F32),
                   jax.ShapeDtypeStruct((SW_HEADS, LANES), F32)],
        grid=(npair, nq),
        in_specs=[pl.BlockSpec((TQ, LANES), lambda j, i: (i, qcol + j)),
                  pl.BlockSpec((s, LANES), lambda j, i: (0, qcol + npair)),
                  pl.BlockSpec((s, LANES), lambda j, i: (0, qcol + npair + 1)),
                  pl.BlockSpec((TQ, LANES), lambda j, i: (i, j)),
                  pl.BlockSpec((TQ, LANES), lambda j, i: (i, j)),
                  pl.BlockSpec((TQ, 2 * BK), whole),
                  pl.BlockSpec(memory_space=pltpu.SMEM),
                  pl.BlockSpec(memory_space=pltpu.SMEM)],
        out_specs=[pl.BlockSpec((TQ, LANES), lambda j, i: (i, j)),
                   pl.BlockSpec((s, LANES), whole), pl.BlockSpec((s, LANES), whole),
                   pl.BlockSpec((SW_HEADS, LANES), whole), pl.BlockSpec((SW_HEADS, LANES), whole)],
        scratch_shapes=[pltpu.VMEM((2 * TQ, 2 * BK), F32), pltpu.VMEM((2 * TQ, 2 * BK), F32)],
        compiler_params=_params(2), name=name)(qkv, qkv, qkv, ob, dob, bucket, sinks, rel_bias)


def _acc_init(i, *refs):
    @pl.when(i == 0)
    def _():
        for r in refs:
            r[...] = jnp.zeros_like(r)


def _loss_bwd(x3, target, g, name):
    d = x3.shape[1]

    def body(x_ref, t_ref, g_ref, dx_ref, dg_ref, loss_ref):
        _acc_init(pl.program_id(0), dg_ref, loss_ref)
        x = x_ref[...]
        gv = g_ref[...]
        diff = _rms(x, gv) - t_ref[...]
        loss_ref[...] += 0.5 * jnp.sum(jnp.mean(jnp.square(diff), axis=-1, keepdims=True), axis=0, keepdims=True)
        dx, dg = _rms_bwd(diff * (1.0 / d), x, gv)
        dx_ref[...] = dx
        dg_ref[...] += dg

    return _rowcall(name, body, [x3, target], [g], [(d, F32)], [((1, d), F32), ((1, LANES), F32)])


def _ple_bwd(dx3, pe, gt, x2, g, wpg, name):
    d = x2.shape[1]

    def body(dx3_ref, pe_ref, gt_ref, x_ref, g_ref, w_ref, dpe_ref, dgt_ref, dx2_ref, dg_ref):
        _acc_init(pl.program_id(0), dg_ref)
        dx3 = dx3_ref[...]
        sg = _sigmoid(gt_ref[...])
        dpe_ref[...] = (dx3 * sg).astype(BF16)
        dgt = (dx3 * pe_ref[...] * sg * (1.0 - sg)).astype(BF16)
        dgt_ref[...] = dgt
        dx, dg = _rms_bwd(_dot_nt(dgt, w_ref[...]), x_ref[...], g_ref[...])
        dx2_ref[...] = dx3 + dx
        dg_ref[...] += dg

    return _rowcall(name, body, [dx3, pe, gt, x2], [g, wpg], [(d, BF16), (d, BF16), (d, F32)], [((1, d), F32)])


def _ff2_bwd(dx2, u, w2, name):
    d = dx2.shape[1]
    dff = u.shape[1]

    def body(dx_ref, u_ref, w_ref, du_ref, dxb_ref):
        dxb = dx_ref[...].astype(BF16)
        dxb_ref[...] = dxb
        du_ref[...] = (_dot_nt(dxb, w_ref[...]) * (2.0 * jnp.maximum(u_ref[...], 0.0))).astype(BF16)

    return _rowcall(name, body, [dx2, u], [w2], [(dff, BF16), (d, BF16)])


def _ff1_bwd(du, dx2, x1, g, w1, name):
    d = x1.shape[1]

    def body(du_ref, dx2_ref, x_ref, g_ref, w_ref, dx1_ref, dx1b_ref, dg_ref):
        _acc_init(pl.program_id(0), dg_ref)
        dx, dg = _rms_bwd(_dot_nt(du_ref[...], w_ref[...]), x_ref[...], g_ref[...])
        dx1 = dx2_ref[...] + dx
        dx1_ref[...] = dx1
        dx1b_ref[...] = dx1.astype(BF16)
        dg_ref[...] += dg

    return _rowcall(name, body, [du, dx2, x1], [g, w1], [(d, F32), (d, BF16)], [((1, d), F32)])


def _mixer_bwd(dx1b, gates, oa, ob, wo, wua, wub, name):
    d = dx1b.shape[1]

    def body(dx_ref, gate_ref, oa_ref, ob_ref, wo_ref, wua_ref, wub_ref,
             dya_ref, dyb_ref, dgate_ref, doa_ref, dob_ref):
        dm = _dot_nt(dx_ref[...], wo_ref[...])
        sa = _sigmoid(gate_ref[:, :d])
        sb = _sigmoid(gate_ref[:, d:])
        ya = _dot(oa_ref[...], wua_ref[...])
        yb = _dot(ob_ref[...], wub_ref[...])
        dya = (dm * sa).astype(BF16)
        dyb = (dm * sb).astype(BF16)
        dya_ref[...] = dya
        dyb_ref[...] = dyb
        dgate_ref[:, :d] = (dm * ya * sa * (1.0 - sa)).astype(BF16)
        dgate_ref[:, d:] = (dm * yb * sb * (1.0 - sb)).astype(BF16)
        doa_ref[...] = _dot_nt(dya, wua_ref[...]).astype(BF16)
        dob_ref[...] = _dot_nt(dyb, wub_ref[...]).astype(BF16)

    return _rowcall(name, body, [dx1b, gates, oa, ob], [wo, wua, wub],
                    [(d, BF16), (d, BF16), (2 * d, BF16), (SB_W, BF16), (SW_QW, BF16)])


def _inproj_bwd(dqkv, dgates, dx1, x, g, w, name):
    d = x.shape[1]

    def body(dqkv_ref, dgate_ref, dx1_ref, x_ref, g_ref, w_ref, dx_ref, dg_ref):
        _acc_init(pl.program_id(0), dg_ref)
        dh = _dot_nt(dqkv_ref[...], w_ref[:, :QKV_W]) + _dot_nt(dgate_ref[...], w_ref[:, QKV_W:])
        dx, dg = _rms_bwd(dh, x_ref[...], g_ref[...])
        dx_ref[...] = dx1_ref[...] + dx
        dg_ref[...] += dg

    return _rowcall(name, body, [dqkv, dgates, dx1, x], [g, w], [(d, F32)], [((1, d), F32)])


def _tile(n, cap):
    assert n % LANES == 0
    return max(t for t in range(LANES, min(n, cap) + 1, LANES) if n % t == 0)


def _mm_tn(a, b, name, nshard=1):
    s, ka = a.shape
    nb = b.shape[1]
    n = nb // nshard
    ta = _tile(ka, 512)
    tb = _tile(n, 512)
    per = n // tb

    def body(a_ref, b_ref, o_ref):
        o_ref[...] = _dot_tn(a_ref[...].astype(BF16), b_ref[...].astype(BF16))

    return pl.pallas_call(
        body, out_shape=jax.ShapeDtypeStruct((nshard, ka, n), F32), grid=(nb // tb, ka // ta),
        in_specs=[pl.BlockSpec((s, ta), lambda jb, ia: (0, ia)), pl.BlockSpec((s, tb), lambda jb, ia: (0, jb))],
        out_specs=pl.BlockSpec((None, ta, tb), lambda jb, ia: (jb // per, ia, jb % per)),
        compiler_params=_params(2), name=name)(a, b)


def _place():
    return lax.axis_index("x"), lax.axis_index("y"), lax.axis_index("c")


def _chip_peer(x, y, k):
    return (x ^ (k >> 1), y ^ (k & 1))


def _cast_bf16(w, name):
    l, k, n = w.shape
    tk = min(k, 256)

    def body(w_ref, o_ref):
        o_ref[...] = w_ref[...].astype(BF16)

    return pl.pallas_call(
        body, out_shape=jax.ShapeDtypeStruct(w.shape, BF16), grid=(l, k // tk),
        in_specs=[pl.BlockSpec((None, tk, n), lambda a, i: (a, i, 0))],
        out_specs=pl.BlockSpec((None, tk, n), lambda a, i: (a, i, 0)),
        compiler_params=_params(2), name=name)(w)


def _all_gather_weights(shards):
    nt = len(shards)

    def body(*refs):
        ins, outs = refs[:nt], refs[nt:2 * nt]
        send_sems, recv_sems, local_sems = refs[2 * nt:]
        x, y, c = _place()
        r = 2 * x + y
        sibling = (x, y, 1 - c)
        local, first = [], []
        for t in range(nt):
            src, dst = ins[t], outs[t]
            mine = pltpu.make_async_copy(src, dst.at[:, r], local_sems.at[t])
            mine.start()
            local.append(mine)
            for k in (1, 2, 3):
                cp = pltpu.make_async_remote_copy(
                    src_ref=src.at[:, c], dst_ref=dst.at[:, r, c], send_sem=send_sems.at[t, k - 1],
                    recv_sem=recv_sems.at[t, k - 1], device_id=(*_chip_peer(x, y, k), c), device_id_type=MESH)
                cp.start()
                first.append(cp)
        fwd = []
        for t in range(nt):
            dst = outs[t]
            for k in (1, 2, 3):
                q = r ^ k
                slab = dst.at[:, q, c]
                pltpu.make_async_remote_copy(
                    src_ref=slab, dst_ref=slab, send_sem=send_sems.at[t, k - 1], recv_sem=recv_sems.at[t, k - 1],
                    device_id=sibling, device_id_type=MESH).wait_recv()
                cp = pltpu.make_async_remote_copy(
                    src_ref=slab, dst_ref=slab, send_sem=send_sems.at[t, 2 + k], recv_sem=recv_sems.at[t, 2 + k],
                    device_id=sibling, device_id_type=MESH)
                cp.start()
                fwd.append(cp)
        for t in range(nt):
            dst = outs[t]
            for k in (1, 2, 3):
                other = dst.at[:, r ^ k, 1 - c]
                pltpu.make_async_remote_copy(
                    src_ref=other, dst_ref=other, send_sem=send_sems.at[t, 2 + k], recv_sem=recv_sems.at[t, 2 + k],
                    device_id=sibling, device_id_type=MESH).wait_recv()
        for cp in first + fwd:
            cp.wait_send()
        for cp in local:
            cp.wait()

    any_spec = pl.BlockSpec(memory_space=pl.ANY)
    return pl.pallas_call(
        body,
        out_shape=[jax.ShapeDtypeStruct((w.shape[0], 4) + w.shape[1:], BF16) for w in shards],
        in_specs=[any_spec] * nt, out_specs=[any_spec] * nt,
        scratch_shapes=[pltpu.SemaphoreType.DMA((nt, 6)), pltpu.SemaphoreType.DMA((nt, 6)),
                        pltpu.SemaphoreType.DMA((nt,))],
        name="all_gather_weights")(*shards)


def _rs_to_sibling(grads):
    nt = len(grads)

    def body(*refs):
        ins, outs = refs[:nt], refs[nt:2 * nt]
        send_sems, recv_sems = refs[2 * nt:]
        x, y, c = _place()
        sibling = (x, y, 1 - c)
        cps = []
        for t in range(nt):
            cp = pltpu.make_async_remote_copy(
                src_ref=ins[t].at[:, 1 - c], dst_ref=outs[t], send_sem=send_sems.at[t], recv_sem=recv_sems.at[t],
                device_id=sibling, device_id_type=MESH)
            cp.start()
            cps.append(cp)
        for cp in cps:
            cp.wait()

    any_spec = pl.BlockSpec(memory_space=pl.ANY)
    return pl.pallas_call(
        body, out_shape=[jax.ShapeDtypeStruct((4,) + g.shape[2:], F32) for g in grads],
        in_specs=[any_spec] * nt, out_specs=[any_spec] * nt,
        scratch_shapes=[pltpu.SemaphoreType.DMA((nt,)), pltpu.SemaphoreType.DMA((nt,))],
        name="rs_to_sibling")(*grads)


def _add_half(g, recv, c, name):
    _, _, k2, n = g.shape
    tk = min(k2, 256)

    def body(c_ref, g_ref, r_ref, o_ref):
        o_ref[...] = g_ref[...] + r_ref[...]

    return pl.pallas_call(
        body, out_shape=jax.ShapeDtypeStruct((4, k2, n), F32),
        grid_spec=pltpu.PrefetchScalarGridSpec(
            num_scalar_prefetch=1, grid=(4, k2 // tk),
            in_specs=[pl.BlockSpec((None, None, tk, n), lambda q, i, c_ref: (q, c_ref[0], i, 0)),
                      pl.BlockSpec((None, tk, n), lambda q, i, c_ref: (q, i, 0))],
            out_specs=pl.BlockSpec((None, tk, n), lambda q, i, c_ref: (q, i, 0))),
        compiler_params=_params(2), name=name)(c, g, recv)


def _rs_to_chips(sums):
    nt = len(sums)

    def body(*refs):
        ins, outs = refs[:nt], refs[nt:2 * nt]
        send_sems, recv_sems, local_sems = refs[2 * nt:]
        x, y, c = _place()
        r = 2 * x + y
        cps = []
        for t in range(nt):
            mine = pltpu.make_async_copy(ins[t].at[r], outs[t].at[r], local_sems.at[t])
            mine.start()
            cps.append(mine)
            for k in (1, 2, 3):
                cp = pltpu.make_async_remote_copy(
                    src_ref=ins[t].at[r ^ k], dst_ref=outs[t].at[r], send_sem=send_sems.at[t, k - 1],
                    recv_sem=recv_sems.at[t, k - 1], device_id=(*_chip_peer(x, y, k), c), device_id_type=MESH)
                cp.start()
                cps.append(cp)
        for cp in cps:
            cp.wait()

    any_spec = pl.BlockSpec(memory_space=pl.ANY)
    return pl.pallas_call(
        body, out_shape=[jax.ShapeDtypeStruct(g.shape, F32) for g in sums],
        in_specs=[any_spec] * nt, out_specs=[any_spec] * nt,
        scratch_shapes=[pltpu.SemaphoreType.DMA((nt, 3)), pltpu.SemaphoreType.DMA((nt, 3)),
                        pltpu.SemaphoreType.DMA((nt,))],
        name="rs_to_chips")(*sums)


def _sum4(parts, name):
    _, k2, n = parts.shape
    tk = min(k2, 256)

    def body(p_ref, o_ref):
        o_ref[...] = ((p_ref[0] + p_ref[1]) + p_ref[2]) + p_ref[3]

    return pl.pallas_call(
        body, out_shape=jax.ShapeDtypeStruct((k2, n), F32), grid=(k2 // tk,),
        in_specs=[pl.BlockSpec((4, tk, n), lambda i: (0, i, 0))],
        out_specs=pl.BlockSpec((tk, n), lambda i: (i, 0)),
        compiler_params=_params(1), name=name)(parts)


def _exchange_halves(halves):
    nt = len(halves)

    def body(*refs):
        ins, outs = refs[:nt], refs[nt:2 * nt]
        send_sems, recv_sems, local_sems = refs[2 * nt:]
        x, y, c = _place()
        cps = []
        for t in range(nt):
            mine = pltpu.make_async_copy(ins[t], outs[t].at[c], local_sems.at[t])
            mine.start()
            cps.append(mine)
            cp = pltpu.make_async_remote_copy(
                src_ref=ins[t], dst_ref=outs[t].at[c], send_sem=send_sems.at[t], recv_sem=recv_sems.at[t],
                device_id=(x, y, 1 - c), device_id_type=MESH)
            cp.start()
            cps.append(cp)
        for cp in cps:
            cp.wait()

    any_spec = pl.BlockSpec(memory_space=pl.ANY)
    return pl.pallas_call(
        body, out_shape=[jax.ShapeDtypeStruct((2,) + h.shape, F32) for h in halves],
        in_specs=[any_spec] * nt, out_specs=[any_spec] * nt,
        scratch_shapes=[pltpu.SemaphoreType.DMA((nt,)), pltpu.SemaphoreType.DMA((nt,)),
                        pltpu.SemaphoreType.DMA((nt,))],
        name="exchange_halves")(*halves)


def _adamw_math(w, g, m, v):
    m = ADAM_B1 * m + (1.0 - ADAM_B1) * g
    v = ADAM_B2 * v + (1.0 - ADAM_B2) * jnp.square(g)
    m_hat = m / (1.0 - ADAM_B1 ** ADAM_STEP)
    v_hat = v / (1.0 - ADAM_B2 ** ADAM_STEP)
    delta = -ADAM_LR * (m_hat / (jnp.sqrt(v_hat) + ADAM_EPS) + ADAM_WD * w)
    return delta, m, v


def _adamw(w, m, v, g0, g1, name):
    _, k, n = w.shape
    tk = min(k, 256)
    nk = k // tk

    def body(w_ref, m_ref, v_ref, g0_ref, g1_ref, grad_ref, delta_ref, nm_ref, nv_ref):
        g = jnp.where(pl.program_id(0) == 0, g0_ref[...], g1_ref[...])
        delta, nm, nv = _adamw_math(w_ref[...], g, m_ref[...], v_ref[...])
        grad_ref[...] = g
        delta_ref[...] = delta
        nm_ref[...] = nm
        nv_ref[...] = nv

    lay = pl.BlockSpec((None, tk, n), lambda a, i: (a, i, 0))
    g0_spec = pl.BlockSpec((tk, n), lambda a, i: (jnp.where(a == 0, i, nk - 1), 0))
    g1_spec = pl.BlockSpec((tk, n), lambda a, i: (jnp.where(a == 1, i, 0), 0))
    return pl.pallas_call(
        body, out_shape=[jax.ShapeDtypeStruct(w.shape, F32)] * 4, grid=(2, nk),
        in_specs=[lay, lay, lay, g0_spec, g1_spec], out_specs=[lay] * 4,
        compiler_params=_params(2), name=name)(w, m, v, g0, g1)


def _small_allreduce_adamw(gpart, w, m, v):
    shape = gpart.shape

    def body(g_ref, w_ref, m_ref, v_ref, gsum_ref, delta_ref, nm_ref, nv_ref, recv_ref, send_sems, recv_sems):
        x, y, c = _place()
        me = 4 * x + 2 * y + c
        recv_ref[me] = g_ref[...]
        cps = []
        for k in range(1, 8):
            peer = (x ^ (k >> 2), y ^ ((k >> 1) & 1), c ^ (k & 1))
            cp = pltpu.make_async_remote_copy(
                src_ref=g_ref, dst_ref=recv_ref.at[me], send_sem=send_sems.at[k - 1], recv_sem=recv_sems.at[k - 1],
                device_id=peer, device_id_type=MESH)
            cp.start()
            cps.append(cp)
        for cp in cps:
            cp.wait()
        g = recv_ref[0]
        for dev in range(1, 8):
            g = g + recv_ref[dev]
        delta, nm, nv = _adamw_math(w_ref[...], g, m_ref[...], v_ref[...])
        gsum_ref[...] = g
        delta_ref[...] = delta
        nm_ref[...] = nm
        nv_ref[...] = nv

    vm = pl.BlockSpec(memory_space=pltpu.VMEM)
    return pl.pallas_call(
        body, out_shape=[jax.ShapeDtypeStruct(shape, F32)] * 4, in_specs=[vm] * 4, out_specs=[vm] * 4,
        scratch_shapes=[pltpu.VMEM((8,) + shape, F32), pltpu.SemaphoreType.DMA((7,)), pltpu.SemaphoreType.DMA((7,))],
        name="small_allreduce_adamw")(gpart, w, m, v)


BIG = ("w_in", "w_up_a", "w_up_b", "w_o", "w_ff1", "w_ff2", "w_pe", "w_pg")
COL_SHARDED = ("w_in", "w_up_a", "w_up_b", "w_ff1", "w_pe")
ROW_SHARDED = ("w_o", "w_ff2", "w_pg")
SMALL_ROWS = 16


def _pack_small(g_mix, g_mlp, g_pe, g_final, sinks, rel_bias, loss=None):
    d = g_final.shape[0]
    row = lambda v: jnp.pad(v.reshape(1, -1), ((0, 0), (0, d - v.size)))
    rows = [g_mix, g_mlp, g_pe, g_final.reshape(1, d),
            jnp.zeros((1, d), F32) if loss is None else row(loss), row(sinks), row(rel_bias)]
    out = jnp.concatenate(rows, axis=0)
    return jnp.pad(out, ((0, SMALL_ROWS - out.shape[0]), (0, 0)))


def _unpack_small(a, sinks_shape, rel_shape):
    return (a[0:2], a[2:4], a[4:6], a[6], a[8, :sinks_shape[0] * sinks_shape[1]].reshape(sinks_shape),
            a[9, :rel_shape[0] * rel_shape[1]].reshape(rel_shape))


def kernel(x, p, w_in, w_up_a, w_up_b, w_o, w_ff1, w_ff2, w_pe, w_pg, g_mix, g_mlp, g_pe, g_final, sinks, rel_bias, loss_target, m_w_in, m_w_up_a, m_w_up_b, m_w_o, m_w_ff1, m_w_ff2, m_w_pe, m_w_pg, m_g_mix, m_g_mlp, m_g_pe, m_g_final, m_sinks, m_rel_bias, v_w_in, v_w_up_a, v_w_up_b, v_w_o, v_w_ff1, v_w_ff2, v_w_pe, v_w_pg, v_g_mix, v_g_mlp, v_g_pe, v_g_final, v_sinks, v_rel_bias):
    depth = w_in.shape[0]
    assert depth == 2
    x0 = x[0]
    target = loss_target[0]
    d = x0.shape[1]
    wl = dict(w_in=w_in, w_up_a=w_up_a, w_up_b=w_up_b, w_o=w_o, w_ff1=w_ff1, w_ff2=w_ff2, w_pe=w_pe, w_pg=w_pg)
    ml = dict(w_in=m_w_in, w_up_a=m_w_up_a, w_up_b=m_w_up_b, w_o=m_w_o, w_ff1=m_w_ff1, w_ff2=m_w_ff2, w_pe=m_w_pe, w_pg=m_w_pg)
    vl = dict(w_in=v_w_in, w_up_a=v_w_up_a, w_up_b=v_w_up_b, w_o=v_w_o, w_ff1=v_w_ff1, w_ff2=v_w_ff2, w_pe=v_w_pe, w_pg=v_w_pg)
    c_idx = lax.axis_index("c").astype(jnp.int32).reshape(1)

    shards = []
    for n in BIG:
        w = wl[n]
        l, k, nn = w.shape
        shards.append(_cast_bf16(w, "cast_" + n).reshape(l, 2, k // 2, nn))
    gathered = _all_gather_weights(shards)
    full = {}
    for n, gth in zip(BIG, gathered):
        l, _, _, k2, nn = gth.shape
        gth = gth.reshape(l, 4, 2 * k2, nn)
        if n in COL_SHARDED:
            full[n] = gth.transpose(0, 2, 1, 3).reshape(l, 2 * k2, 4 * nn)
        else:
            full[n] = gth.reshape(l, 8 * k2, nn)

    saved = []
    xi = x0
    for i in range(depth):
        st = dict(x0=xi)
        gm = g_mix[i].reshape(1, d)
        st["h1"], st["qkv"], st["gates"] = _inproj_fwd(xi, gm, full["w_in"][i], f"inproj_fwd_{i}")
        st["oa"], st["lt"] = _sb_fwd(st["qkv"], f"sb_fwd_{i}")
        st["ob"] = _swa_fwd(st["qkv"], sinks[i], rel_bias, f"swa_fwd_{i}")
        st["m"], st["x1"] = _mixer_fwd(st["oa"], st["ob"], st["gates"], xi, full["w_up_a"][i], full["w_up_b"][i],
                                       full["w_o"][i], f"mixer_fwd_{i}")
        st["h2"], st["u"], st["a"] = _ff1_fwd(st["x1"], g_mlp[i].reshape(1, d), full["w_ff1"][i], f"ff1_fwd_{i}")
        st["x2"] = _ff2_fwd(st["a"], st["x1"], full["w_ff2"][i], f"ff2_fwd_{i}")
        st["pb"], st["h3"], st["pe"], st["gt"], xi = _ple_fwd(p[i, 0], st["x2"], g_pe[i].reshape(1, d),
                                                            full["w_pe"][i], full["w_pg"][i], f"ple_fwd_{i}")
        saved.append(st)

    dx, dg_final, loss_part = _loss_bwd(xi, target, g_final.reshape(1, d), "loss_bwd")
    gw = {n: [None] * depth for n in BIG}
    dg_mix, dg_mlp, dg_pe, dsinks = [None] * depth, [None] * depth, [None] * depth, [None] * depth
    drel = jnp.zeros((SW_HEADS, LANES), F32)
    for i in reversed(range(depth)):
        st = saved[i]
        dpe, dgt, dx2, dg_pe[i] = _ple_bwd(dx, st["pe"], st["gt"], st["x2"], g_pe[i].reshape(1, d),
                                           full["w_pg"][i], f"ple_bwd_{i}")
        gw["w_pe"][i] = _mm_tn(st["pb"], dpe, f"dw_pe_{i}", 4)
        gw["w_pg"][i] = _mm_tn(st["h3"], dgt, f"dw_pg_{i}")
        du, dx2b = _ff2_bwd(dx2, st["u"], full["w_ff2"][i], f"ff2_bwd_{i}")
        gw["w_ff2"][i] = _mm_tn(st["a"], dx2b, f"dw_ff2_{i}")
        gw["w_ff1"][i] = _mm_tn(st["h2"], du, f"dw_ff1_{i}", 4)
        dx1, dx1b, dg_mlp[i] = _ff1_bwd(du, dx2, st["x1"], g_mlp[i].reshape(1, d), full["w_ff1"][i], f"ff1_bwd_{i}")
        gw["w_o"][i] = _mm_tn(st["m"], dx1b, f"dw_o_{i}")
        dya, dyb, dgates, doa, dob = _mixer_bwd(dx1b, st["gates"], st["oa"], st["ob"], full["w_o"][i],
                                                full["w_up_a"][i], full["w_up_b"][i], f"mixer_bwd_{i}")
        gw["w_up_a"][i] = _mm_tn(st["oa"], dya, f"dw_up_a_{i}", 4)
        gw["w_up_b"][i] = _mm_tn(st["ob"], dyb, f"dw_up_b_{i}", 4)
        dqa, dka, dva = _sb_bwd(st["qkv"], st["lt"], doa, f"sb_bwd_{i}")
        dqb, dkb, dvb, dsk, drl = _swa_bwd(st["qkv"], st["ob"], dob, sinks[i], rel_bias, f"swa_bwd_{i}")
        dsinks[i] = dsk[:, 0]
        drel = drel + drl
        dqkv = jnp.concatenate([dqa, dka.astype(BF16), dva.astype(BF16), dqb, dkb.astype(BF16), dvb.astype(BF16)],
                               axis=1)
        dw_in = jnp.concatenate([_mm_tn(st["h1"], dqkv, f"dw_in_qkv_{i}")[0],
                                 _mm_tn(st["h1"], dgates, f"dw_in_gate_{i}")[0]], axis=1)
        gw["w_in"][i] = dw_in.reshape(d, 4, dw_in.shape[1] // 4).transpose(1, 0, 2)
        dx, dg_mix[i] = _inproj_bwd(dqkv, dgates, dx1, st["x0"], g_mix[i].reshape(1, d), full["w_in"][i],
                                    f"inproj_bwd_{i}")
    grad_x = dx[None]

    tensors = []
    for n in BIG:
        for i in range(depth):
            g = gw[n][i]
            if n in ROW_SHARDED:
                ka, nb = g.shape[1:]
                g = g.reshape(4, ka // 4, nb)
            _, k, nn = g.shape
            tensors.append(g.reshape(4, 2, k // 2, nn))
    from_sibling = _rs_to_sibling(tensors)
    chip_sums = [_add_half(g, r, c_idx, f"add_half_{t}") for t, (g, r) in enumerate(zip(tensors, from_sibling))]
    from_chips = _rs_to_chips(chip_sums)
    halves = [_sum4(pc, f"sum4_{t}") for t, pc in enumerate(from_chips)]
    both = _exchange_halves(halves)
    outs = {}
    for ti, n in enumerate(BIG):
        g0 = both[2 * ti].reshape(wl[n].shape[1:])
        g1 = both[2 * ti + 1].reshape(wl[n].shape[1:])
        outs[n] = _adamw(wl[n], ml[n], vl[n], g0, g1, "adamw_" + n)

    drel_bias = drel[:, :N_BUCKETS].T
    gsmall = _pack_small(jnp.concatenate(dg_mix, 0), jnp.concatenate(dg_mlp, 0), jnp.concatenate(dg_pe, 0),
                         dg_final[0], jnp.stack(dsinks), drel_bias, loss_part[0, :1])
    wsmall = _pack_small(g_mix, g_mlp, g_pe, g_final, sinks, rel_bias)
    msmall = _pack_small(m_g_mix, m_g_mlp, m_g_pe, m_g_final, m_sinks, m_rel_bias)
    vsmall = _pack_small(v_g_mix, v_g_mlp, v_g_pe, v_g_final, v_sinks, v_rel_bias)
    small = _small_allreduce_adamw(gsmall, wsmall, msmall, vsmall)
    loss = small[0][7, 0]
    small = [_unpack_small(a, sinks.shape, rel_bias.shape) for a in small]

    result = [loss, grad_x]
    for kind in range(4):
        result += [outs[n][kind] for n in BIG]
        result += list(small[kind])
    return tuple(result)
```

```python
import functools
import math

import numpy as np
import jax
import jax.numpy as jnp
from jax import lax
from jax.experimental import pallas as pl
from jax.experimental.pallas import tpu as pltpu

F32 = jnp.float32
BF16 = jnp.bfloat16
MESH = pl.DeviceIdType.MESH

HEAD_DIM = 64
SB_HEADS = 8
SW_HEADS = 8
SW_KV_HEADS = 2
WINDOW = 128
N_BUCKETS = 32
MAX_DISTANCE = 128
EPS = 1e-6
SB_W = SB_HEADS * HEAD_DIM
SW_QW = SW_HEADS * HEAD_DIM
SW_KVW = SW_KV_HEADS * HEAD_DIM
QKV_W = 3 * SB_W + SW_QW + 2 * SW_KVW
SCALE = HEAD_DIM ** -0.5
LANES = 128
TQ = 128
BK = 128
NEG = -1e30
SB_EXHAUSTED = -106.0

ADAM_LR = 0.001
ADAM_B1 = 0.9
ADAM_B2 = 0.999
ADAM_EPS = 1e-08
ADAM_WD = 0.01
ADAM_STEP = 10

VMEM_LIMIT = 56 * 1024 * 1024


def _dot(a, b):
    return jnp.dot(a, b, preferred_element_type=F32)


def _dot_nt(a, b):
    return lax.dot_general(a, b, (((1,), (1,)), ((), ())), preferred_element_type=F32)


def _dot_tn(a, b):
    return lax.dot_general(a, b, (((0,), (0,)), ((), ())), preferred_element_type=F32)


def _sum_all(x):
    return jnp.sum(jnp.sum(x, axis=1, keepdims=True), axis=0, keepdims=True)


def _sigmoid(x):
    return 1.0 / (1.0 + jnp.exp(-x))


def _rms(x, g):
    r = lax.rsqrt(jnp.mean(x * x, axis=-1, keepdims=True) + EPS)
    return (x * r) * g


def _rms_bwd(dy, x, g):
    r = lax.rsqrt(jnp.mean(x * x, axis=-1, keepdims=True) + EPS)
    n = x * r
    dg = jnp.sum(dy * n, axis=0, keepdims=True)
    dn = dy * g
    dx = r * (dn - n * jnp.mean(dn * n, axis=-1, keepdims=True))
    return dx, dg


def _params(n_axes):
    return pltpu.CompilerParams(dimension_semantics=("arbitrary",) * n_axes, vmem_limit_bytes=VMEM_LIMIT)


def _rowcall(name, body, row_ins, const_ins, row_outs, acc_outs=(), tm=256):
    s = row_ins[0].shape[0]
    assert s % tm == 0
    in_specs = [pl.BlockSpec((tm, a.shape[1]), lambda i: (i, 0)) for a in row_ins]
    in_specs += [pl.BlockSpec(a.shape, functools.partial(lambda i, nd: (0,) * nd, nd=a.ndim)) for a in const_ins]
    out_shape = [jax.ShapeDtypeStruct((s, c), dt) for c, dt in row_outs]
    out_specs = [pl.BlockSpec((tm, c), lambda i: (i, 0)) for c, _ in row_outs]
    out_shape += [jax.ShapeDtypeStruct(sh, dt) for sh, dt in acc_outs]
    out_specs += [pl.BlockSpec(sh, functools.partial(lambda i, nd: (0,) * nd, nd=len(sh))) for sh, _ in acc_outs]
    return pl.pallas_call(body, out_shape=out_shape, grid=(s // tm,), in_specs=in_specs, out_specs=out_specs,
                          compiler_params=_params(1), name=name)(*row_ins, *const_ins)


def _inproj_fwd(x, g, w, name):
    d = x.shape[1]

    def body(x_ref, g_ref, w_ref, h_ref, qkv_ref, gate_ref):
        hb = _rms(x_ref[...], g_ref[...]).astype(BF16)
        h_ref[...] = hb
        qkv_ref[...] = _dot(hb, w_ref[:, :QKV_W]).astype(BF16)
        gate_ref[...] = _dot(hb, w_ref[:, QKV_W:])

    return _rowcall(name, body, [x], [g, w], [(d, BF16), (QKV_W, BF16), (2 * d, F32)])


def _mixer_fwd(oa, ob, gates, x, wua, wub, wo, name):
    d = x.shape[1]

    def body(oa_ref, ob_ref, gate_ref, x_ref, wua_ref, wub_ref, wo_ref, m_ref, x1_ref):
        ya = _dot(oa_ref[...], wua_ref[...])
        yb = _dot(ob_ref[...], wub_ref[...])
        m = _sigmoid(gate_ref[:, :d]) * ya + _sigmoid(gate_ref[:, d:]) * yb
        mb = m.astype(BF16)
        m_ref[...] = mb
        x1_ref[...] = x_ref[...] + _dot(mb, wo_ref[...])

    return _rowcall(name, body, [oa, ob, gates, x], [wua, wub, wo], [(d, BF16), (d, F32)])


def _ff1_fwd(x1, g, w1, name):
    d, dff = w1.shape

    def body(x_ref, g_ref, w_ref, h_ref, u_ref, a_ref):
        hb = _rms(x_ref[...], g_ref[...]).astype(BF16)
        h_ref[...] = hb
        u = _dot(hb, w_ref[...])
        u_ref[...] = u
        a_ref[...] = jnp.square(jnp.maximum(u, 0.0)).astype(BF16)

    return _rowcall(name, body, [x1], [g, w1], [(d, BF16), (dff, F32), (dff, BF16)])


def _ff2_fwd(a, x1, w2, name):
    d = x1.shape[1]

    def body(a_ref, x_ref, w_ref, o_ref):
        o_ref[...] = x_ref[...] + _dot(a_ref[...], w_ref[...])

    return _rowcall(name, body, [a, x1], [w2], [(d, F32)])[0]


def _ple_fwd(p, x2, g, wpe, wpg, name):
    d = x2.shape[1]

    def body(p_ref, x_ref, g_ref, wpe_ref, wpg_ref, pb_ref, h_ref, pe_ref, gt_ref, x3_ref):
        pb = p_ref[...].astype(BF16)
        pb_ref[...] = pb
        pe = _dot(pb, wpe_ref[...])
        x = x_ref[...]
        hb = _rms(x, g_ref[...]).astype(BF16)
        h_ref[...] = hb
        gt = _dot(hb, wpg_ref[...])
        pe_ref[...] = pe
        gt_ref[...] = gt
        x3_ref[...] = x + pe * _sigmoid(gt)

    return _rowcall(name, body, [p, x2], [g, wpe, wpg],
                    [(p.shape[1], BF16), (d, BF16), (d, F32), (d, F32), (d, F32)])


def _pair_stack(t, lane):
    zero = jnp.zeros_like(t)
    return jnp.concatenate([jnp.where(lane < HEAD_DIM, t, zero), jnp.where(lane >= HEAD_DIM, t, zero)], axis=0)


def _sb_consts():
    jj = lax.broadcasted_iota(jnp.int32, (BK, 2 * BK), 0)
    ss = lax.broadcasted_iota(jnp.int32, (BK, 2 * BK), 1)
    suffix = jnp.where((ss >= BK) | (jj > ss), 1.0, 0.0).astype(BF16)
    prefix_incl = jnp.where((ss >= BK) | (jj <= ss), 1.0, 0.0).astype(BF16)
    prefix_excl = jnp.where((ss >= BK) | (jj < ss), 1.0, 0.0).astype(BF16)
    row = lax.broadcasted_iota(jnp.int32, (2 * TQ, BK), 0)
    row = jnp.where(row >= TQ, row - TQ, row)
    col = lax.broadcasted_iota(jnp.int32, (2 * TQ, BK), 1)
    return suffix, prefix_incl, prefix_excl, col - row


def _split_dot(x, m01):
    hi = x.astype(BF16)
    lo = (x - hi.astype(F32)).astype(BF16)
    return _dot(hi, m01) + _dot(lo, m01)


def _sb_scores(qs, k, mask):
    z = _dot_nt(qs, k) * SCALE
    lb = jnp.minimum(z, 0.0) - jnp.log1p(jnp.exp(-jnp.abs(z)))
    lm = jnp.where(mask, lb - z, 0.0)
    return lb, lm


def _sb_fwd(qkv, name):
    s = qkv.shape[0]
    nq = s // TQ

    def body(q_ref, k_ref, v_ref, o_ref, lt_ref, nb_ref, cf_ref, acc_ref):
        i = pl.program_id(1)
        lane = lax.broadcasted_iota(jnp.int32, (TQ, LANES), 1)
        qs = _pair_stack(q_ref[...], lane)
        suffix, _, _, rel = _sb_consts()
        q0 = i * TQ
        cf_ref[...] = jnp.zeros_like(cf_ref)
        acc_ref[...] = jnp.zeros_like(acc_ref)

        def more(c):
            return (c[0] <= i) & (c[1] > SB_EXHAUSTED)

        def step(c):
            it = c[0]
            k0 = pl.multiple_of((i - it) * BK, BK)
            k = k_ref[pl.ds(k0, BK), :]
            v = v_ref[pl.ds(k0, BK), :]
            mask = rel < (q0 - k0)
            lb, lm = _sb_scores(qs, k, mask)
            cs = _split_dot(lm, suffix)
            a = jnp.where(mask, jnp.exp(lb + cs[:, :BK] + cf_ref[...]), 0.0)
            acc_ref[...] += _dot(a.astype(BF16), v)
            cf = cf_ref[...] + cs[:, BK:]
            cf_ref[...] = cf
            return it + 1, jnp.max(cf)

        n_blocks, _ = lax.while_loop(more, step, (jnp.int32(0), jnp.float32(0.0)))
        o_ref[...] = jnp.where(lane < HEAD_DIM, acc_ref[:TQ, :], acc_ref[TQ:, :]).astype(BF16)
        lt_ref[...] = cf_ref[...]
        nb_ref[...] = jnp.full(nb_ref.shape, n_blocks, F32)

    npair = SB_W // LANES
    return pl.pallas_call(
        body,
        out_shape=[jax.ShapeDtypeStruct((s, SB_W), BF16), jax.ShapeDtypeStruct((npair, nq, 2 * TQ, BK), F32),
                   jax.ShapeDtypeStruct((npair, nq, 8, LANES), F32)],
        grid=(npair, nq),
        in_specs=[pl.BlockSpec((TQ, LANES), lambda j, i: (i, j)),
                  pl.BlockSpec((s, LANES), lambda j, i: (0, npair + j)),
                  pl.BlockSpec((s, LANES), lambda j, i: (0, 2 * npair + j))],
        out_specs=[pl.BlockSpec((TQ, LANES), lambda j, i: (i, j)),
                   pl.BlockSpec((None, None, 2 * TQ, BK), lambda j, i: (j, i, 0, 0)),
                   pl.BlockSpec((None, None, 8, LANES), lambda j, i: (j, i, 0, 0))],
        scratch_shapes=[pltpu.VMEM((2 * TQ, BK), F32), pltpu.VMEM((2 * TQ, LANES), F32)],
        compiler_params=_params(2), name=name)(qkv, qkv, qkv)


def _sb_bwd(qkv, lt, nb, doa, name):
    s = qkv.shape[0]
    nq = s // TQ

    def body(q_ref, k_ref, v_ref, lt_ref, nb_ref, do_ref, dq_ref, dk_ref, dv_ref, cp_ref, ce_ref, dqa_ref):
        i = pl.program_id(1)
        lane = lax.broadcasted_iota(jnp.int32, (TQ, LANES), 1)
        qs = _pair_stack(q_ref[...], lane)
        dos = _pair_stack(do_ref[...], lane)
        _, prefix_incl, prefix_excl, rel = _sb_consts()
        q0 = i * TQ
        n_blocks = jnp.clip(jnp.max(nb_ref[...]).astype(jnp.int32), 1, i + 1)
        first = i + 1 - n_blocks

        @pl.when(i == 0)
        def _():
            dk_ref[...] = jnp.zeros_like(dk_ref)
            dv_ref[...] = jnp.zeros_like(dv_ref)

        cp_ref[...] = jnp.zeros_like(cp_ref)
        ce_ref[...] = jnp.zeros_like(ce_ref)
        dqa_ref[...] = jnp.zeros_like(dqa_ref)

        def step(it, carry):
            k0 = pl.multiple_of((first + it) * BK, BK)
            k = k_ref[pl.ds(k0, BK), :]
            v = v_ref[pl.ds(k0, BK), :]
            mask = rel < (q0 - k0)
            lb, lm = _sb_scores(qs, k, mask)
            cs = _split_dot(lm, prefix_incl)
            f = lt_ref[...] - (cs[:, :BK] + cp_ref[...])
            a = jnp.where(mask, jnp.exp(lb + f), 0.0)
            e = a * _dot_nt(dos, v)
            ce = _split_dot(e, prefix_excl)
            big_e = ce[:, :BK] + ce_ref[...]
            sg = jnp.exp(lb)
            dz = (jnp.where(mask, e - sg * (e + big_e), 0.0) * SCALE).astype(BF16)
            dqa_ref[...] += _dot(dz, k)
            dk_ref[pl.ds(k0, BK), :] += _dot_tn(dz, qs)
            dv_ref[pl.ds(k0, BK), :] += _dot_tn(a.astype(BF16), dos)
            cp_ref[...] += cs[:, BK:]
            ce_ref[...] += ce[:, BK:]
            return carry

        lax.fori_loop(0, n_blocks, step, 0)
        dq_ref[...] = jnp.where(lane < HEAD_DIM, dqa_ref[:TQ, :], dqa_ref[TQ:, :]).astype(BF16)

    npair = SB_W // LANES
    return pl.pallas_call(
        body,
        out_shape=[jax.ShapeDtypeStruct((s, SB_W), BF16), jax.ShapeDtypeStruct((s, SB_W), F32),
                   jax.ShapeDtypeStruct((s, SB_W), F32)],
        grid=(npair, nq),
        in_specs=[pl.BlockSpec((TQ, LANES), lambda j, i: (i, j)),
                  pl.BlockSpec((s, LANES), lambda j, i: (0, npair + j)),
                  pl.BlockSpec((s, LANES), lambda j, i: (0, 2 * npair + j)),
                  pl.BlockSpec((None, None, 2 * TQ, BK), lambda j, i: (j, i, 0, 0)),
                  pl.BlockSpec((None, None, 8, LANES), lambda j, i: (j, i, 0, 0)),
                  pl.BlockSpec((TQ, LANES), lambda j, i: (i, j))],
        out_specs=[pl.BlockSpec((TQ, LANES), lambda j, i: (i, j)),
                   pl.BlockSpec((s, LANES), lambda j, i: (0, j)),
                   pl.BlockSpec((s, LANES), lambda j, i: (0, j))],
        scratch_shapes=[pltpu.VMEM((2 * TQ, BK), F32), pltpu.VMEM((2 * TQ, BK), F32),
                        pltpu.VMEM((2 * TQ, LANES), F32)],
        compiler_params=_params(2), name=name)(qkv, qkv, qkv, lt, nb, doa)


def _bucket_table():
    i = np.arange(TQ)[:, None]
    j = np.arange(2 * BK)[None, :]
    dist = np.maximum(TQ + i - j, 0)
    max_exact = N_BUCKETS // 2
    df = np.maximum(dist, 1).astype(np.float32)
    large = max_exact + (np.log(df / np.float32(max_exact)) / np.float32(math.log(MAX_DISTANCE / max_exact))
                         * np.float32(N_BUCKETS - max_exact)).astype(np.int32)
    large = np.minimum(large, N_BUCKETS - 1)
    return np.where(dist < max_exact, dist, large).astype(np.int32)


def _swa_align_in(t, lane, g):
    tf = t.astype(F32)
    tr = pltpu.roll(tf, HEAD_DIM, 1)
    gmask = (lane >= HEAD_DIM) == (g == 1)
    top = jnp.where(gmask, jnp.where(g == 0, tf, tr), 0.0)
    bot = jnp.where(gmask, jnp.where(g == 1, tf, tr), 0.0)
    return jnp.concatenate([top, bot], axis=0).astype(BF16)


def _swa_align_out(t, lane, g):
    top, bot = t[:TQ, :], t[TQ:, :]
    top = jnp.where(g == 0, top, pltpu.roll(top, HEAD_DIM, 1))
    bot = jnp.where(g == 1, bot, pltpu.roll(bot, HEAD_DIM, 1))
    return jnp.where(lane < HEAD_DIM, top, bot)


def _swa_bias(bias_ref, bucket_ref, rb_ref, j):
    for hh in range(2):
        def add(b, acc):
            return acc + jnp.where(bucket_ref[...] == b, rb_ref[b, 2 * j + hh], 0.0)
        bias_ref[hh * TQ:(hh + 1) * TQ, :] = lax.fori_loop(0, N_BUCKETS, add, jnp.zeros((TQ, 2 * BK), F32))


def _swa_probs(qs, k2, bias, sink_ref, i, j):
    s = _dot_nt(qs, k2) * SCALE + bias
    row = lax.broadcasted_iota(jnp.int32, (2 * TQ, 2 * BK), 0)
    col = lax.broadcasted_iota(jnp.int32, (2 * TQ, 2 * BK), 1)
    dist = TQ + jnp.where(row >= TQ, row - TQ, row) - col
    valid = (dist >= 0) & (dist < WINDOW) & ((col >= BK) | (i > 0))
    s = jnp.where(valid, s, NEG)
    row1 = lax.broadcasted_iota(jnp.int32, (2 * TQ, 1), 0)
    sink = jnp.where(row1 < TQ, sink_ref[2 * j], sink_ref[2 * j + 1])
    m = jnp.maximum(jnp.max(s, axis=1, keepdims=True), sink)
    e = jnp.exp(s - m)
    es = jnp.exp(sink - m)
    den = jnp.sum(e, axis=1, keepdims=True) + es
    return e / den, es / den


def _swa_kv(ref, i):
    prev = pl.multiple_of(jnp.maximum(i - 1, 0) * BK, BK)
    cur = pl.multiple_of(i * BK, BK)
    return jnp.concatenate([ref[pl.ds(prev, BK), :], ref[pl.ds(cur, BK), :]], axis=0), prev, cur


def _swa_fwd(qkv, sinks, rel_bias, name):
    s = qkv.shape[0]
    nq = s // TQ
    npair = SW_QW // LANES
    qcol = 3 * SB_W // LANES
    bucket = jnp.asarray(_bucket_table())

    def body(q_ref, k_ref, v_ref, bucket_ref, sink_ref, rb_ref, o_ref, bias_ref):
        j = pl.program_id(0)
        i = pl.program_id(1)
        g = j // 2
        lane = lax.broadcasted_iota(jnp.int32, (TQ, LANES), 1)

        @pl.when(i == 0)
        def _():
            _swa_bias(bias_ref, bucket_ref, rb_ref, j)

        qs = _swa_align_in(q_ref[...], lane, g)
        k2, _, _ = _swa_kv(k_ref, i)
        v2, _, _ = _swa_kv(v_ref, i)
        pr, _ = _swa_probs(qs, k2, bias_ref[...], sink_ref, i, j)
        o_ref[...] = _swa_align_out(_dot(pr.astype(BF16), v2), lane, g).astype(BF16)

    return pl.pallas_call(
        body, out_shape=jax.ShapeDtypeStruct((s, SW_QW), BF16), grid=(npair, nq),
        in_specs=[pl.BlockSpec((TQ, LANES), lambda j, i: (i, qcol + j)),
                  pl.BlockSpec((s, LANES), lambda j, i: (0, qcol + npair)),
                  pl.BlockSpec((s, LANES), lambda j, i: (0, qcol + npair + 1)),
                  pl.BlockSpec((TQ, 2 * BK), lambda j, i: (0, 0)),
                  pl.BlockSpec(memory_space=pltpu.SMEM),
                  pl.BlockSpec(memory_space=pltpu.SMEM)],
        out_specs=pl.BlockSpec((TQ, LANES), lambda j, i: (i, j)),
        scratch_shapes=[pltpu.VMEM((2 * TQ, 2 * BK), F32)],
        compiler_params=_params(2), name=name)(qkv, qkv, qkv, bucket, sinks, rel_bias)


def _swa_bwd(qkv, ob, dob, sinks, rel_bias, name):
    s = qkv.shape[0]
    nq = s // TQ
    npair = SW_QW // LANES
    qcol = 3 * SB_W // LANES
    bucket = jnp.asarray(_bucket_table())

    def body(q_ref, k_ref, v_ref, o_ref, do_ref, bucket_ref, sink_ref, rb_ref,
             dq_ref, dk_ref, dv_ref, dsink_ref, drel_ref, bias_ref, dsacc_ref):
        j = pl.program_id(0)
        i = pl.program_id(1)
        g = j // 2
        lane = lax.broadcasted_iota(jnp.int32, (TQ, LANES), 1)
        row8 = lax.broadcasted_iota(jnp.int32, (SW_HEADS, LANES), 0)
        lane8 = lax.broadcasted_iota(jnp.int32, (SW_HEADS, LANES), 1)

        @pl.when((i == 0) & (j == 0))
        def _():
            dk_ref[...] = jnp.zeros_like(dk_ref)
            dv_ref[...] = jnp.zeros_like(dv_ref)
            dsink_ref[...] = jnp.zeros_like(dsink_ref)
            drel_ref[...] = jnp.zeros_like(drel_ref)

        @pl.when(i == 0)
        def _():
            _swa_bias(bias_ref, bucket_ref, rb_ref, j)
            dsacc_ref[...] = jnp.zeros_like(dsacc_ref)

        qs = _swa_align_in(q_ref[...], lane, g)
        do = do_ref[...]
        dos = _swa_align_in(do, lane, g)
        dof = do.astype(F32) * o_ref[...].astype(F32)
        d0 = jnp.sum(jnp.where(lane < HEAD_DIM, dof, 0.0), axis=1, keepdims=True)
        d1 = jnp.sum(jnp.where(lane >= HEAD_DIM, dof, 0.0), axis=1, keepdims=True)
        delta = jnp.concatenate([d0, d1], axis=0)
        k2, prev, cur = _swa_kv(k_ref, i)
        v2, _, _ = _swa_kv(v_ref, i)
        pr, psink = _swa_probs(qs, k2, bias_ref[...], sink_ref, i, j)
        ds = pr * (_dot_nt(dos, v2) - delta)
        dsacc_ref[...] += ds
        sd = psink * delta
        ds0 = -jnp.sum(sd[:TQ, :], axis=0, keepdims=True)
        ds1 = -jnp.sum(sd[TQ:, :], axis=0, keepdims=True)
        dsink_ref[...] += jnp.where(row8 == 2 * j, ds0, jnp.where(row8 == 2 * j + 1, ds1, 0.0))
        dsb = ds.astype(BF16)
        dq_ref[...] = _swa_align_out(_dot(dsb, k2) * SCALE, lane, g).astype(BF16)
        dk2 = _dot_tn(dsb, qs) * SCALE
        dv2 = _dot_tn(pr.astype(BF16), dos)
        dk_ref[pl.ds(prev, BK), :] += dk2[:BK, :]
        dk_ref[pl.ds(cur, BK), :] += dk2[BK:, :]
        dv_ref[pl.ds(prev, BK), :] += dv2[:BK, :]
        dv_ref[pl.ds(cur, BK), :] += dv2[BK:, :]

        @pl.when(i == nq - 1)
        def _():
            for hh in range(2):
                def red(b, acc):
                    val = _sum_all(jnp.where(bucket_ref[...] == b, dsacc_ref[hh * TQ:(hh + 1) * TQ, :], 0.0))
                    return jnp.where((row8 == 2 * j + hh) & (lane8 == b), val, acc)
                drel_ref[...] += lax.fori_loop(0, N_BUCKETS, red, jnp.zeros((SW_HEADS, LANES), F32))

    whole = lambda j, i: (0, 0)
    return pl.pallas_call(
        body,
        out_shape=[jax.ShapeDtypeStruct((s, SW_QW), BF16), jax.ShapeDtypeStruct((s, LANES), F32),
                   jax.ShapeDtypeStruct((s, LANES), F32), jax.ShapeDtypeStruct((SW_HEADS, LANES), F32),
                   jax.ShapeDtypeStruct((SW_HEADS, LANES), F32)],
        grid=(npair, nq),
        in_specs=[pl.BlockSpec((TQ, LANES), lambda j, i: (i, qcol + j)),
                  pl.BlockSpec((s, LANES), lambda j, i: (0, qcol + npair)),
                  pl.BlockSpec((s, LANES), lambda j, i: (0, qcol + npair + 1)),
                  pl.BlockSpec((TQ, LANES), lambda j, i: (i, j)),
                  pl.BlockSpec((TQ, LANES), lambda j, i: (i, j)),
                  pl.BlockSpec((TQ, 2 * BK), whole),
                  pl.BlockSpec(memory_space=pltpu.SMEM),
                  pl.BlockSpec(memory_space=pltpu.SMEM)],
        out_specs=[pl.BlockSpec((TQ, LANES), lambda j, i: (i, j)),
                   pl.BlockSpec((s, LANES), whole), pl.BlockSpec((s, LANES), whole),
                   pl.BlockSpec((SW_HEADS, LANES), whole), pl.BlockSpec((SW_HEADS, LANES), whole)],
        scratch_shapes=[pltpu.VMEM((2 * TQ, 2 * BK), F32), pltpu.VMEM((2 * TQ, 2 * BK), F32)],
        compiler_params=_params(2), name=name)(qkv, qkv, qkv, ob, dob, bucket, sinks, rel_bias)


def _acc_init(i, *refs):
    @pl.when(i == 0)
    def _():
        for r in refs:
            r[...] = jnp.zeros_like(r)


def _loss_bwd(x3, target, g, name):
    d = x3.shape[1]

    def body(x_ref, t_ref, g_ref, dx_ref, dg_ref, loss_ref):
        _acc_init(pl.program_id(0), dg_ref, loss_ref)
        x = x_ref[...]
        gv = g_ref[...]
        diff = _rms(x, gv) - t_ref[...]
        loss_ref[...] += 0.5 * jnp.sum(jnp.mean(jnp.square(diff), axis=-1, keepdims=True), axis=0, keepdims=True)
        dx, dg = _rms_bwd(diff * (1.0 / d), x, gv)
        dx_ref[...] = dx
        dg_ref[...] += dg

    return _rowcall(name, body, [x3, target], [g], [(d, F32)], [((1, d), F32), ((1, LANES), F32)])


def _ple_bwd(dx3, pe, gt, x2, g, wpg, name):
    d = x2.shape[1]

    def body(dx3_ref, pe_ref, gt_ref, x_ref, g_ref, w_ref, dpe_ref, dgt_ref, dx2_ref, dg_ref):
        _acc_init(pl.program_id(0), dg_ref)
        dx3 = dx3_ref[...]
        sg = _sigmoid(gt_ref[...])
        dpe_ref[...] = (dx3 * sg).astype(BF16)
        dgt = (dx3 * pe_ref[...] * sg * (1.0 - sg)).astype(BF16)
        dgt_ref[...] = dgt
        dx, dg = _rms_bwd(_dot_nt(dgt, w_ref[...]), x_ref[...], g_ref[...])
        dx2_ref[...] = dx3 + dx
        dg_ref[...] += dg

    return _rowcall(name, body, [dx3, pe, gt, x2], [g, wpg], [(d, BF16), (d, BF16), (d, F32)], [((1, d), F32)])


def _ff2_bwd(dx2, u, w2, name):
    d = dx2.shape[1]
    dff = u.shape[1]

    def body(dx_ref, u_ref, w_ref, du_ref, dxb_ref):
        dxb = dx_ref[...].astype(BF16)
        dxb_ref[...] = dxb
        du_ref[...] = (_dot_nt(dxb, w_ref[...]) * (2.0 * jnp.maximum(u_ref[...], 0.0))).astype(BF16)

    return _rowcall(name, body, [dx2, u], [w2], [(dff, BF16), (d, BF16)])


def _ff1_bwd(du, dx2, x1, g, w1, name):
    d = x1.shape[1]

    def body(du_ref, dx2_ref, x_ref, g_ref, w_ref, dx1_ref, dx1b_ref, dg_ref):
        _acc_init(pl.program_id(0), dg_ref)
        dx, dg = _rms_bwd(_dot_nt(du_ref[...], w_ref[...]), x_ref[...], g_ref[...])
        dx1 = dx2_ref[...] + dx
        dx1_ref[...] = dx1
        dx1b_ref[...] = dx1.astype(BF16)
        dg_ref[...] += dg

    return _rowcall(name, body, [du, dx2, x1], [g, w1], [(d, F32), (d, BF16)], [((1, d), F32)])


def _mixer_bwd(dx1b, gates, oa, ob, wo, wua, wub, name):
    d = dx1b.shape[1]

    def body(dx_ref, gate_ref, oa_ref, ob_ref, wo_ref, wua_ref, wub_ref,
             dya_ref, dyb_ref, dgate_ref, doa_ref, dob_ref):
        dm = _dot_nt(dx_ref[...], wo_ref[...])
        sa = _sigmoid(gate_ref[:, :d])
        sb = _sigmoid(gate_ref[:, d:])
        ya = _dot(oa_ref[...], wua_ref[...])
        yb = _dot(ob_ref[...], wub_ref[...])
        dya = (dm * sa).astype(BF16)
        dyb = (dm * sb).astype(BF16)
        dya_ref[...] = dya
        dyb_ref[...] = dyb
        dgate_ref[:, :d] = (dm * ya * sa * (1.0 - sa)).astype(BF16)
        dgate_ref[:, d:] = (dm * yb * sb * (1.0 - sb)).astype(BF16)
        doa_ref[...] = _dot_nt(dya, wua_ref[...]).astype(BF16)
        dob_ref[...] = _dot_nt(dyb, wub_ref[...]).astype(BF16)

    return _rowcall(name, body, [dx1b, gates, oa, ob], [wo, wua, wub],
                    [(d, BF16), (d, BF16), (2 * d, BF16), (SB_W, BF16), (SW_QW, BF16)])


def _inproj_bwd(dqkv, dgates, dx1, x, g, w, name):
    d = x.shape[1]

    def body(dqkv_ref, dgate_ref, dx1_ref, x_ref, g_ref, w_ref, dx_ref, dg_ref):
        _acc_init(pl.program_id(0), dg_ref)
        dh = _dot_nt(dqkv_ref[...], w_ref[:, :QKV_W]) + _dot_nt(dgate_ref[...], w_ref[:, QKV_W:])
        dx, dg = _rms_bwd(dh, x_ref[...], g_ref[...])
        dx_ref[...] = dx1_ref[...] + dx
        dg_ref[...] += dg

    return _rowcall(name, body, [dqkv, dgates, dx1, x], [g, w], [(d, F32)], [((1, d), F32)])


def _tile(n, cap):
    assert n % LANES == 0
    return max(t for t in range(LANES, min(n, cap) + 1, LANES) if n % t == 0)


def _mm_tn(a, b, name, nshard=1):
    s, ka = a.shape
    nb = b.shape[1]
    n = nb // nshard
    ta = _tile(ka, 512)
    tb = _tile(n, 512)
    per = n // tb

    def body(a_ref, b_ref, o_ref):
        o_ref[...] = _dot_tn(a_ref[...].astype(BF16), b_ref[...].astype(BF16))

    return pl.pallas_call(
        body, out_shape=jax.ShapeDtypeStruct((nshard, ka, n), F32), grid=(nb // tb, ka // ta),
        in_specs=[pl.BlockSpec((s, ta), lambda jb, ia: (0, ia)), pl.BlockSpec((s, tb), lambda jb, ia: (0, jb))],
        out_specs=pl.BlockSpec((None, ta, tb), lambda jb, ia: (jb // per, ia, jb % per)),
        compiler_params=_params(2), name=name)(a, b)


def _place():
    return lax.axis_index("x"), lax.axis_index("y"), lax.axis_index("c")


def _chip_peer(x, y, k):
    return (x ^ (k >> 1), y ^ (k & 1))


def _cast_bf16(w, r, name):
    l, k, n = w.shape
    tk = min(k, 256)

    def body(r_ref, w_ref, o_ref):
        o_ref[...] = w_ref[...].astype(BF16)

    return pl.pallas_call(
        body, out_shape=jax.ShapeDtypeStruct((l, 4, k, n), BF16),
        grid_spec=pltpu.PrefetchScalarGridSpec(
            num_scalar_prefetch=1, grid=(l, k // tk),
            in_specs=[pl.BlockSpec((None, tk, n), lambda a, i, r_ref: (a, i, 0))],
            out_specs=pl.BlockSpec((None, None, tk, n), lambda a, i, r_ref: (a, r_ref[0], i, 0))),
        compiler_params=_params(2), name=name)(r, w)


def _all_gather_weights(bufs):
    nt = len(bufs)

    def body(*refs):
        ins, outs = refs[:nt], refs[nt:2 * nt]
        send_sems, recv_sems = refs[2 * nt:]
        x, y, c = _place()
        r = 2 * x + y
        sibling = (x, y, 1 - c)
        first = []
        for t in range(nt):
            src, dst = ins[t], outs[t]
            for k in (1, 2, 3):
                cp = pltpu.make_async_remote_copy(
                    src_ref=src.at[:, r, c], dst_ref=dst.at[:, r, c], send_sem=send_sems.at[t, k - 1],
                    recv_sem=recv_sems.at[t, k - 1], device_id=(*_chip_peer(x, y, k), c), device_id_type=MESH)
                cp.start()
                first.append(cp)
        fwd = []
        for t in range(nt):
            dst = outs[t]
            for k in (1, 2, 3):
                q = r ^ k
                slab = dst.at[:, q, c]
                pltpu.make_async_remote_copy(
                    src_ref=slab, dst_ref=slab, send_sem=send_sems.at[t, k - 1], recv_sem=recv_sems.at[t, k - 1],
                    device_id=sibling, device_id_type=MESH).wait_recv()
                cp = pltpu.make_async_remote_copy(
                    src_ref=slab, dst_ref=slab, send_sem=send_sems.at[t, 2 + k], recv_sem=recv_sems.at[t, 2 + k],
                    device_id=sibling, device_id_type=MESH)
                cp.start()
                fwd.append(cp)
        for t in range(nt):
            dst = outs[t]
            for k in (1, 2, 3):
                other = dst.at[:, r ^ k, 1 - c]
                pltpu.make_async_remote_copy(
                    src_ref=other, dst_ref=other, send_sem=send_sems.at[t, 2 + k], recv_sem=recv_sems.at[t, 2 + k],
                    device_id=sibling, device_id_type=MESH).wait_recv()
        for cp in first + fwd:
            cp.wait_send()

    any_spec = pl.BlockSpec(memory_space=pl.ANY)
    return pl.pallas_call(
        body, out_shape=[jax.ShapeDtypeStruct(b.shape, BF16) for b in bufs],
        in_specs=[any_spec] * nt, out_specs=[any_spec] * nt,
        input_output_aliases={t: t for t in range(nt)},
        scratch_shapes=[pltpu.SemaphoreType.DMA((nt, 6)), pltpu.SemaphoreType.DMA((nt, 6))],
        name="all_gather_weights")(*bufs)


def _rs_to_sibling(grads):
    nt = len(grads)

    def body(*refs):
        ins, outs = refs[:nt], refs[nt:2 * nt]
        send_sems, recv_sems = refs[2 * nt:]
        x, y, c = _place()
        sibling = (x, y, 1 - c)
        cps = []
        for t in range(nt):
            cp = pltpu.make_async_remote_copy(
                src_ref=ins[t].at[:, 1 - c], dst_ref=outs[t], send_sem=send_sems.at[t], recv_sem=recv_sems.at[t],
                device_id=sibling, device_id_type=MESH)
            cp.start()
            cps.append(cp)
        for cp in cps:
            cp.wait()

    any_spec = pl.BlockSpec(memory_space=pl.ANY)
    return pl.pallas_call(
        body, out_shape=[jax.ShapeDtypeStruct((4,) + g.shape[2:], F32) for g in grads],
        in_specs=[any_spec] * nt, out_specs=[any_spec] * nt,
        scratch_shapes=[pltpu.SemaphoreType.DMA((nt,)), pltpu.SemaphoreType.DMA((nt,))],
        name="rs_to_sibling")(*grads)


def _add_half(g, recv, cr, name):
    _, _, k2, n = g.shape
    tk = min(k2, 256)

    def body(cr_ref, g_ref, r_ref, sums_ref, mine_ref):
        val = (g_ref[...] + r_ref[...]).astype(BF16)
        sums_ref[...] = val

        @pl.when(pl.program_id(1) == cr_ref[1])
        def _():
            mine_ref[...] = val

    return pl.pallas_call(
        body, out_shape=[jax.ShapeDtypeStruct((4, k2, n), BF16)] * 2,
        grid_spec=pltpu.PrefetchScalarGridSpec(
            num_scalar_prefetch=1, grid=(k2 // tk, 4),
            in_specs=[pl.BlockSpec((None, None, tk, n), lambda i, q, cr_ref: (q, cr_ref[0], i, 0)),
                      pl.BlockSpec((None, tk, n), lambda i, q, cr_ref: (q, i, 0))],
            out_specs=[pl.BlockSpec((None, tk, n), lambda i, q, cr_ref: (q, i, 0)),
                       pl.BlockSpec((None, tk, n), lambda i, q, cr_ref: (cr_ref[1], i, 0))]),
        compiler_params=_params(2), name=name)(cr, g, recv)


def _rs_to_chips(sums, parts):
    nt = len(sums)

    def body(*refs):
        ins, outs = refs[:nt], refs[2 * nt:3 * nt]
        send_sems, recv_sems = refs[3 * nt:]
        x, y, c = _place()
        r = 2 * x + y
        cps = []
        for t in range(nt):
            for k in (1, 2, 3):
                cp = pltpu.make_async_remote_copy(
                    src_ref=ins[t].at[r ^ k], dst_ref=outs[t].at[r], send_sem=send_sems.at[t, k - 1],
                    recv_sem=recv_sems.at[t, k - 1], device_id=(*_chip_peer(x, y, k), c), device_id_type=MESH)
                cp.start()
                cps.append(cp)
        for cp in cps:
            cp.wait()

    any_spec = pl.BlockSpec(memory_space=pl.ANY)
    return pl.pallas_call(
        body, out_shape=[jax.ShapeDtypeStruct(p.shape, p.dtype) for p in parts],
        in_specs=[any_spec] * (2 * nt), out_specs=[any_spec] * nt,
        input_output_aliases={nt + t: t for t in range(nt)},
        scratch_shapes=[pltpu.SemaphoreType.DMA((nt, 3)), pltpu.SemaphoreType.DMA((nt, 3))],
        name="rs_to_chips")(*sums, *parts)


def _sum4(parts, cr, name):
    _, k2, n = parts.shape
    tk = min(k2, 256)

    def body(cr_ref, p_ref, o_ref):
        p = p_ref[...].astype(F32)
        o_ref[...] = ((p[0] + p[1]) + p[2]) + p[3]

    return pl.pallas_call(
        body, out_shape=jax.ShapeDtypeStruct((2, k2, n), F32),
        grid_spec=pltpu.PrefetchScalarGridSpec(
            num_scalar_prefetch=1, grid=(k2 // tk,),
            in_specs=[pl.BlockSpec((4, tk, n), lambda i, cr_ref: (0, i, 0))],
            out_specs=pl.BlockSpec((None, tk, n), lambda i, cr_ref: (cr_ref[0], i, 0))),
        compiler_params=_params(1), name=name)(cr, parts)


def _exchange_halves(both):
    nt = len(both)

    def body(*refs):
        ins, outs = refs[:nt], refs[nt:2 * nt]
        send_sems, recv_sems = refs[2 * nt:]
        x, y, c = _place()
        cps = []
        for t in range(nt):
            cp = pltpu.make_async_remote_copy(
                src_ref=ins[t].at[c], dst_ref=outs[t].at[c], send_sem=send_sems.at[t], recv_sem=recv_sems.at[t],
                device_id=(x, y, 1 - c), device_id_type=MESH)
            cp.start()
            cps.append(cp)
        for cp in cps:
            cp.wait()

    any_spec = pl.BlockSpec(memory_space=pl.ANY)
    return pl.pallas_call(
        body, out_shape=[jax.ShapeDtypeStruct(b.shape, F32) for b in both],
        in_specs=[any_spec] * nt, out_specs=[any_spec] * nt,
        input_output_aliases={t: t for t in range(nt)},
        scratch_shapes=[pltpu.SemaphoreType.DMA((nt,)), pltpu.SemaphoreType.DMA((nt,))],
        name="exchange_halves")(*both)


def _adamw_math(w, g, m, v):
    m = ADAM_B1 * m + (1.0 - ADAM_B1) * g
    v = ADAM_B2 * v + (1.0 - ADAM_B2) * jnp.square(g)
    m_hat = m / (1.0 - ADAM_B1 ** ADAM_STEP)
    v_hat = v / (1.0 - ADAM_B2 ** ADAM_STEP)
    delta = -ADAM_LR * (m_hat / (jnp.sqrt(v_hat) + ADAM_EPS) + ADAM_WD * w)
    return delta, m, v


def _adamw(w, m, v, g0, g1, name):
    _, k, n = w.shape
    tk = min(k, 256)
    nk = k // tk

    def body(w_ref, m_ref, v_ref, g0_ref, g1_ref, grad_ref, delta_ref, nm_ref, nv_ref):
        g = jnp.where(pl.program_id(0) == 0, g0_ref[...], g1_ref[...])
        delta, nm, nv = _adamw_math(w_ref[...], g, m_ref[...], v_ref[...])
        grad_ref[...] = g
        delta_ref[...] = delta
        nm_ref[...] = nm
        nv_ref[...] = nv

    lay = pl.BlockSpec((None, tk, n), lambda a, i: (a, i, 0))
    g0_spec = pl.BlockSpec((tk, n), lambda a, i: (jnp.where(a == 0, i, nk - 1), 0))
    g1_spec = pl.BlockSpec((tk, n), lambda a, i: (jnp.where(a == 1, i, 0), 0))
    return pl.pallas_call(
        body, out_shape=[jax.ShapeDtypeStruct(w.shape, F32)] * 4, grid=(2, nk),
        in_specs=[lay, lay, lay, g0_spec, g1_spec], out_specs=[lay] * 4,
        compiler_params=_params(2), name=name)(w, m, v, g0, g1)


def _small_allreduce_adamw(gpart, w, m, v):
    shape = gpart.shape

    def body(g_ref, w_ref, m_ref, v_ref, gsum_ref, delta_ref, nm_ref, nv_ref, recv_ref, send_sems, recv_sems):
        x, y, c = _place()
        me = 4 * x + 2 * y + c
        recv_ref[me] = g_ref[...]
        cps = []
        for k in range(1, 8):
            peer = (x ^ (k >> 2), y ^ ((k >> 1) & 1), c ^ (k & 1))
            cp = pltpu.make_async_remote_copy(
                src_ref=g_ref, dst_ref=recv_ref.at[me], send_sem=send_sems.at[k - 1], recv_sem=recv_sems.at[k - 1],
                device_id=peer, device_id_type=MESH)
            cp.start()
            cps.append(cp)
        for cp in cps:
            cp.wait()
        g = recv_ref[0]
        for dev in range(1, 8):
            g = g + recv_ref[dev]
        delta, nm, nv = _adamw_math(w_ref[...], g, m_ref[...], v_ref[...])
        gsum_ref[...] = g
        delta_ref[...] = delta
        nm_ref[...] = nm
        nv_ref[...] = nv

    vm = pl.BlockSpec(memory_space=pltpu.VMEM)
    return pl.pallas_call(
        body, out_shape=[jax.ShapeDtypeStruct(shape, F32)] * 4, in_specs=[vm] * 4, out_specs=[vm] * 4,
        scratch_shapes=[pltpu.VMEM((8,) + shape, F32), pltpu.SemaphoreType.DMA((7,)), pltpu.SemaphoreType.DMA((7,))],
        name="small_allreduce_adamw")(gpart, w, m, v)


BIG = ("w_in", "w_up_a", "w_up_b", "w_o", "w_ff1", "w_ff2", "w_pe", "w_pg")
COL_SHARDED = ("w_in", "w_up_a", "w_up_b", "w_ff1", "w_pe")
ROW_SHARDED = ("w_o", "w_ff2", "w_pg")
SMALL_ROWS = 16


def _pack_small(g_mix, g_mlp, g_pe, g_final, sinks, rel_bias, loss=None):
    d = g_final.shape[0]
    row = lambda v: jnp.pad(v.reshape(1, -1), ((0, 0), (0, d - v.size)))
    rows = [g_mix, g_mlp, g_pe, g_final.reshape(1, d),
            jnp.zeros((1, d), F32) if loss is None else row(loss), row(sinks), row(rel_bias)]
    out = jnp.concatenate(rows, axis=0)
    return jnp.pad(out, ((0, SMALL_ROWS - out.shape[0]), (0, 0)))


def _unpack_small(a, sinks_shape, rel_shape):
    return (a[0:2], a[2:4], a[4:6], a[6], a[8, :sinks_shape[0] * sinks_shape[1]].reshape(sinks_shape),
            a[9, :rel_shape[0] * rel_shape[1]].reshape(rel_shape))


def kernel(x, p, w_in, w_up_a, w_up_b, w_o, w_ff1, w_ff2, w_pe, w_pg, g_mix, g_mlp, g_pe, g_final, sinks, rel_bias, loss_target, m_w_in, m_w_up_a, m_w_up_b, m_w_o, m_w_ff1, m_w_ff2, m_w_pe, m_w_pg, m_g_mix, m_g_mlp, m_g_pe, m_g_final, m_sinks, m_rel_bias, v_w_in, v_w_up_a, v_w_up_b, v_w_o, v_w_ff1, v_w_ff2, v_w_pe, v_w_pg, v_g_mix, v_g_mlp, v_g_pe, v_g_final, v_sinks, v_rel_bias):
    depth = w_in.shape[0]
    assert depth == 2
    x0 = x[0]
    target = loss_target[0]
    d = x0.shape[1]
    wl = dict(w_in=w_in, w_up_a=w_up_a, w_up_b=w_up_b, w_o=w_o, w_ff1=w_ff1, w_ff2=w_ff2, w_pe=w_pe, w_pg=w_pg)
    ml = dict(w_in=m_w_in, w_up_a=m_w_up_a, w_up_b=m_w_up_b, w_o=m_w_o, w_ff1=m_w_ff1, w_ff2=m_w_ff2, w_pe=m_w_pe, w_pg=m_w_pg)
    vl = dict(w_in=v_w_in, w_up_a=v_w_up_a, w_up_b=v_w_up_b, w_o=v_w_o, w_ff1=v_w_ff1, w_ff2=v_w_ff2, w_pe=v_w_pe, w_pg=v_w_pg)
    c_idx = lax.axis_index("c").astype(jnp.int32)
    r_idx = (2 * lax.axis_index("x") + lax.axis_index("y")).astype(jnp.int32)
    cr = jnp.stack([c_idx, r_idx])

    bufs = []
    for n in BIG:
        w = wl[n]
        l, k, nn = w.shape
        bufs.append(_cast_bf16(w, r_idx.reshape(1), "cast_" + n).reshape(l, 4, 2, k // 2, nn))
    gathered = _all_gather_weights(bufs)
    full = {}
    for n, gth in zip(BIG, gathered):
        l, _, _, k2, nn = gth.shape
        gth = gth.reshape(l, 4, 2 * k2, nn)
        if n in COL_SHARDED:
            full[n] = gth.transpose(0, 2, 1, 3).reshape(l, 2 * k2, 4 * nn)
        else:
            full[n] = gth.reshape(l, 8 * k2, nn)

    saved = []
    xi = x0
    for i in range(depth):
        st = dict(x0=xi)
        gm = g_mix[i].reshape(1, d)
        st["h1"], st["qkv"], st["gates"] = _inproj_fwd(xi, gm, full["w_in"][i], f"inproj_fwd_{i}")
        st["oa"], st["lt"], st["nb"] = _sb_fwd(st["qkv"], f"sb_fwd_{i}")
        st["ob"] = _swa_fwd(st["qkv"], sinks[i], rel_bias, f"swa_fwd_{i}")
        st["m"], st["x1"] = _mixer_fwd(st["oa"], st["ob"], st["gates"], xi, full["w_up_a"][i], full["w_up_b"][i],
                                       full["w_o"][i], f"mixer_fwd_{i}")
        st["h2"], st["u"], st["a"] = _ff1_fwd(st["x1"], g_mlp[i].reshape(1, d), full["w_ff1"][i], f"ff1_fwd_{i}")
        st["x2"] = _ff2_fwd(st["a"], st["x1"], full["w_ff2"][i], f"ff2_fwd_{i}")
        st["pb"], st["h3"], st["pe"], st["gt"], xi = _ple_fwd(p[i, 0], st["x2"], g_pe[i].reshape(1, d),
                                                            full["w_pe"][i], full["w_pg"][i], f"ple_fwd_{i}")
        saved.append(st)

    dx, dg_final, loss_part = _loss_bwd(xi, target, g_final.reshape(1, d), "loss_bwd")
    gw = {n: [None] * depth for n in BIG}
    dg_mix, dg_mlp, dg_pe, dsinks = [None] * depth, [None] * depth, [None] * depth, [None] * depth
    drel = jnp.zeros((SW_HEADS, LANES), F32)
    for i in reversed(range(depth)):
        st = saved[i]
        dpe, dgt, dx2, dg_pe[i] = _ple_bwd(dx, st["pe"], st["gt"], st["x2"], g_pe[i].reshape(1, d),
                                           full["w_pg"][i], f"ple_bwd_{i}")
        gw["w_pe"][i] = _mm_tn(st["pb"], dpe, f"dw_pe_{i}", 4)
        gw["w_pg"][i] = _mm_tn(st["h3"], dgt, f"dw_pg_{i}")
        du, dx2b = _ff2_bwd(dx2, st["u"], full["w_ff2"][i], f"ff2_bwd_{i}")
        gw["w_ff2"][i] = _mm_tn(st["a"], dx2b, f"dw_ff2_{i}")
        gw["w_ff1"][i] = _mm_tn(st["h2"], du, f"dw_ff1_{i}", 4)
        dx1, dx1b, dg_mlp[i] = _ff1_bwd(du, dx2, st["x1"], g_mlp[i].reshape(1, d), full["w_ff1"][i], f"ff1_bwd_{i}")
        gw["w_o"][i] = _mm_tn(st["m"], dx1b, f"dw_o_{i}")
        dya, dyb, dgates, doa, dob = _mixer_bwd(dx1b, st["gates"], st["oa"], st["ob"], full["w_o"][i],
                                                full["w_up_a"][i], full["w_up_b"][i], f"mixer_bwd_{i}")
        gw["w_up_a"][i] = _mm_tn(st["oa"], dya, f"dw_up_a_{i}", 4)
        gw["w_up_b"][i] = _mm_tn(st["ob"], dyb, f"dw_up_b_{i}", 4)
        dqa, dka, dva = _sb_bwd(st["qkv"], st["lt"], st["nb"], doa, f"sb_bwd_{i}")
        dqb, dkb, dvb, dsk, drl = _swa_bwd(st["qkv"], st["ob"], dob, sinks[i], rel_bias, f"swa_bwd_{i}")
        dsinks[i] = dsk[:, 0]
        drel = drel + drl
        dqkv = jnp.concatenate([dqa, dka.astype(BF16), dva.astype(BF16), dqb, dkb.astype(BF16), dvb.astype(BF16)],
                               axis=1)
        dw_in = jnp.concatenate([_mm_tn(st["h1"], dqkv, f"dw_in_qkv_{i}")[0],
                                 _mm_tn(st["h1"], dgates, f"dw_in_gate_{i}")[0]], axis=1)
        gw["w_in"][i] = dw_in.reshape(d, 4, dw_in.shape[1] // 4).transpose(1, 0, 2)
        dx, dg_mix[i] = _inproj_bwd(dqkv, dgates, dx1, st["x0"], g_mix[i].reshape(1, d), full["w_in"][i],
                                    f"inproj_bwd_{i}")
    grad_x = dx[None]

    tensors = []
    for n in BIG:
        for i in range(depth):
            g = gw[n][i]
            if n in ROW_SHARDED:
                ka, nb = g.shape[1:]
                g = g.reshape(4, ka // 4, nb)
            _, k, nn = g.shape
            tensors.append(g.reshape(4, 2, k // 2, nn))
    from_sibling = _rs_to_sibling(tensors)
    added = [_add_half(g, r, cr, f"add_half_{t}") for t, (g, r) in enumerate(zip(tensors, from_sibling))]
    from_chips = _rs_to_chips([a[0] for a in added], [a[1] for a in added])
    both = _exchange_halves([_sum4(pc, cr, f"sum4_{t}") for t, pc in enumerate(from_chips)])
    outs = {}
    for ti, n in enumerate(BIG):
        g0 = both[2 * ti].reshape(wl[n].shape[1:])
        g1 = both[2 * ti + 1].reshape(wl[n].shape[1:])
        outs[n] = _adamw(wl[n], ml[n], vl[n], g0, g1, "adamw_" + n)

    drel_bias = drel[:, :N_BUCKETS].T
    gsmall = _pack_small(jnp.concatenate(dg_mix, 0), jnp.concatenate(dg_mlp, 0), jnp.concatenate(dg_pe, 0),
                         dg_final[0], jnp.stack(dsinks), drel_bias, loss_part[0, :1])
    wsmall = _pack_small(g_mix, g_mlp, g_pe, g_final, sinks, rel_bias)
    msmall = _pack_small(m_g_mix, m_g_mlp, m_g_pe, m_g_final, m_sinks, m_rel_bias)
    vsmall = _pack_small(v_g_mix, v_g_mlp, v_g_pe, v_g_final, v_sinks, v_rel_bias)
    small = _small_allreduce_adamw(gsmall, wsmall, msmall, vsmall)
    loss = small[0][7, 0]
    small = [_unpack_small(a, sinks.shape, rel_bias.shape) for a in small]

    result = [loss, grad_x]
    for kind in range(4):
        result += [outs[n][kind] for n in BIG]
        result += list(small[kind])
    return tuple(result)
```

```python
import functools
import math

import numpy as np
import jax
import jax.numpy as jnp
from jax import lax
from jax.experimental import pallas as pl
from jax.experimental.pallas import tpu as pltpu

F32 = jnp.float32
BF16 = jnp.bfloat16
MESH = pl.DeviceIdType.MESH

HEAD_DIM = 64
SB_HEADS = 8
SW_HEADS = 8
SW_KV_HEADS = 2
WINDOW = 128
N_BUCKETS = 32
MAX_DISTANCE = 128
EPS = 1e-6
SB_W = SB_HEADS * HEAD_DIM
SW_QW = SW_HEADS * HEAD_DIM
SW_KVW = SW_KV_HEADS * HEAD_DIM
QKV_W = 3 * SB_W + SW_QW + 2 * SW_KVW
SCALE = HEAD_DIM ** -0.5
LANES = 128
TQ = 128
BK = 128
NEG = -1e30
SB_EXHAUSTED = -106.0

ADAM_LR = 0.001
ADAM_B1 = 0.9
ADAM_B2 = 0.999
ADAM_EPS = 1e-08
ADAM_WD = 0.01
ADAM_STEP = 10

VMEM_LIMIT = 56 * 1024 * 1024


def _dot(a, b):
    return jnp.dot(a, b, preferred_element_type=F32)


def _dot_nt(a, b):
    return lax.dot_general(a, b, (((1,), (1,)), ((), ())), preferred_element_type=F32)


def _dot_tn(a, b):
    return lax.dot_general(a, b, (((0,), (0,)), ((), ())), preferred_element_type=F32)


def _sum_all(x):
    return jnp.sum(jnp.sum(x, axis=1, keepdims=True), axis=0, keepdims=True)


def _sigmoid(x):
    return 1.0 / (1.0 + jnp.exp(-x))


def _rms(x, g):
    r = lax.rsqrt(jnp.mean(x * x, axis=-1, keepdims=True) + EPS)
    return (x * r) * g


def _rms_bwd(dy, x, g):
    r = lax.rsqrt(jnp.mean(x * x, axis=-1, keepdims=True) + EPS)
    n = x * r
    dg = jnp.sum(dy * n, axis=0, keepdims=True)
    dn = dy * g
    dx = r * (dn - n * jnp.mean(dn * n, axis=-1, keepdims=True))
    return dx, dg


def _params(n_axes):
    return pltpu.CompilerParams(dimension_semantics=("arbitrary",) * n_axes, vmem_limit_bytes=VMEM_LIMIT)


def _rowcall(name, body, row_ins, const_ins, row_outs, acc_outs=(), tm=256):
    s = row_ins[0].shape[0]
    assert s % tm == 0
    in_specs = [pl.BlockSpec((tm, a.shape[1]), lambda i: (i, 0)) for a in row_ins]
    in_specs += [pl.BlockSpec(a.shape, functools.partial(lambda i, nd: (0,) * nd, nd=a.ndim)) for a in const_ins]
    out_shape = [jax.ShapeDtypeStruct((s, c), dt) for c, dt in row_outs]
    out_specs = [pl.BlockSpec((tm, c), lambda i: (i, 0)) for c, _ in row_outs]
    out_shape += [jax.ShapeDtypeStruct(sh, dt) for sh, dt in acc_outs]
    out_specs += [pl.BlockSpec(sh, functools.partial(lambda i, nd: (0,) * nd, nd=len(sh))) for sh, _ in acc_outs]
    return pl.pallas_call(body, out_shape=out_shape, grid=(s // tm,), in_specs=in_specs, out_specs=out_specs,
                          compiler_params=_params(1), name=name)(*row_ins, *const_ins)


def _inproj_fwd(x, g, w, name):
    d = x.shape[1]

    def body(x_ref, g_ref, w_ref, h_ref, qkv_ref, gate_ref):
        hb = _rms(x_ref[...], g_ref[...]).astype(BF16)
        h_ref[...] = hb
        qkv_ref[...] = _dot(hb, w_ref[:, :QKV_W]).astype(BF16)
        gate_ref[...] = _dot(hb, w_ref[:, QKV_W:])

    return _rowcall(name, body, [x], [g, w], [(d, BF16), (QKV_W, BF16), (2 * d, F32)])


def _mixer_fwd(oa, ob, gates, x, wua, wub, wo, name):
    d = x.shape[1]

    def body(oa_ref, ob_ref, gate_ref, x_ref, wua_ref, wub_ref, wo_ref, m_ref, x1_ref):
        ya = _dot(oa_ref[...], wua_ref[...])
        yb = _dot(ob_ref[...], wub_ref[...])
        m = _sigmoid(gate_ref[:, :d]) * ya + _sigmoid(gate_ref[:, d:]) * yb
        mb = m.astype(BF16)
        m_ref[...] = mb
        x1_ref[...] = x_ref[...] + _dot(mb, wo_ref[...])

    return _rowcall(name, body, [oa, ob, gates, x], [wua, wub, wo], [(d, BF16), (d, F32)])


def _ff1_fwd(x1, g, w1, name):
    d, dff = w1.shape

    def body(x_ref, g_ref, w_ref, h_ref, u_ref, a_ref):
        hb = _rms(x_ref[...], g_ref[...]).astype(BF16)
        h_ref[...] = hb
        u = _dot(hb, w_ref[...])
        u_ref[...] = u
        a_ref[...] = jnp.square(jnp.maximum(u, 0.0)).astype(BF16)

    return _rowcall(name, body, [x1], [g, w1], [(d, BF16), (dff, F32), (dff, BF16)])


def _ff2_fwd(a, x1, w2, name):
    d = x1.shape[1]

    def body(a_ref, x_ref, w_ref, o_ref):
        o_ref[...] = x_ref[...] + _dot(a_ref[...], w_ref[...])

    return _rowcall(name, body, [a, x1], [w2], [(d, F32)])[0]


def _ple_fwd(p, x2, g, wpe, wpg, name):
    d = x2.shape[1]

    def body(p_ref, x_ref, g_ref, wpe_ref, wpg_ref, pb_ref, h_ref, pe_ref, gt_ref, x3_ref):
        pb = p_ref[...].astype(BF16)
        pb_ref[...] = pb
        pe = _dot(pb, wpe_ref[...])
        x = x_ref[...]
        hb = _rms(x, g_ref[...]).astype(BF16)
        h_ref[...] = hb
        gt = _dot(hb, wpg_ref[...])
        pe_ref[...] = pe
        gt_ref[...] = gt
        x3_ref[...] = x + pe * _sigmoid(gt)

    return _rowcall(name, body, [p, x2], [g, wpe, wpg],
                    [(p.shape[1], BF16), (d, BF16), (d, F32), (d, F32), (d, F32)])


def _pair_stack(t, lane):
    zero = jnp.zeros_like(t)
    return jnp.concatenate([jnp.where(lane < HEAD_DIM, t, zero), jnp.where(lane >= HEAD_DIM, t, zero)], axis=0)


def _sb_consts():
    jj = lax.broadcasted_iota(jnp.int32, (BK, 2 * BK), 0)
    ss = lax.broadcasted_iota(jnp.int32, (BK, 2 * BK), 1)
    suffix = jnp.where((ss >= BK) | (jj > ss), 1.0, 0.0).astype(BF16)
    prefix_incl = jnp.where((ss >= BK) | (jj <= ss), 1.0, 0.0).astype(BF16)
    prefix_excl = jnp.where((ss >= BK) | (jj < ss), 1.0, 0.0).astype(BF16)
    row = lax.broadcasted_iota(jnp.int32, (2 * TQ, BK), 0)
    row = jnp.where(row >= TQ, row - TQ, row)
    col = lax.broadcasted_iota(jnp.int32, (2 * TQ, BK), 1)
    return suffix, prefix_incl, prefix_excl, col - row


def _split_dot(x, m01):
    hi = x.astype(BF16)
    lo = (x - hi.astype(F32)).astype(BF16)
    return _dot(hi, m01) + _dot(lo, m01)


def _sb_scores(qs, k, mask):
    z = _dot_nt(qs, k) * SCALE
    lb = jnp.minimum(z, 0.0) - jnp.log1p(jnp.exp(-jnp.abs(z)))
    lm = lb - z
    return lb, lm if mask is None else jnp.where(mask, lm, 0.0)


SB_STRAIGHT = 3
SWA_QB = 4


def _sb_fwd(qkv, name):
    s = qkv.shape[0]
    nq = s // TQ

    def body(q_ref, k_ref, v_ref, o_ref, lt_ref, nb_ref, cf_ref, acc_ref):
        i = pl.program_id(1)
        lane = lax.broadcasted_iota(jnp.int32, (TQ, LANES), 1)
        qs = _pair_stack(q_ref[...], lane)
        suffix, _, _, rel = _sb_consts()
        q0 = i * TQ

        def block(back, masked, cf, acc):
            k0 = pl.multiple_of((i - back) * BK, BK)
            k = k_ref[pl.ds(k0, BK), :]
            v = v_ref[pl.ds(k0, BK), :]
            mask = (rel < (q0 - k0)) if masked else None
            lb, lm = _sb_scores(qs, k, mask)
            cs = _split_dot(lm, suffix)
            a = jnp.exp(lb + cs[:, :BK] + cf)
            if masked:
                a = jnp.where(mask, a, 0.0)
            return cf + cs[:, BK:], acc + _dot(a.astype(BF16), v)

        straight = i >= SB_STRAIGHT - 1

        @pl.when(straight)
        def _():
            cf = jnp.zeros(cf_ref.shape, F32)
            acc = jnp.zeros(acc_ref.shape, F32)
            for back in range(SB_STRAIGHT):
                cf, acc = block(back, back == 0, cf, acc)
            cf_ref[...] = cf
            acc_ref[...] = acc

        @pl.when(jnp.logical_not(straight))
        def _():
            cf_ref[...] = jnp.zeros_like(cf_ref)
            acc_ref[...] = jnp.zeros_like(acc_ref)

        def more(c):
            return (c[0] <= i) & (c[1] > SB_EXHAUSTED)

        def step(c):
            cf, acc = block(c[0], True, cf_ref[...], acc_ref[...])
            cf_ref[...] = cf
            acc_ref[...] = acc
            return c[0] + 1, jnp.max(cf)

        n_blocks, _ = lax.while_loop(
            more, step, (jnp.where(straight, SB_STRAIGHT, 0).astype(jnp.int32), jnp.max(cf_ref[...])))
        o_ref[...] = jnp.where(lane < HEAD_DIM, acc_ref[:TQ, :], acc_ref[TQ:, :]).astype(BF16)
        lt_ref[...] = cf_ref[...]
        nb_ref[...] = jnp.full(nb_ref.shape, n_blocks, F32)

    npair = SB_W // LANES
    return pl.pallas_call(
        body,
        out_shape=[jax.ShapeDtypeStruct((s, SB_W), BF16), jax.ShapeDtypeStruct((npair, nq, 2 * TQ, BK), F32),
                   jax.ShapeDtypeStruct((npair, nq, 8, LANES), F32)],
        grid=(npair, nq),
        in_specs=[pl.BlockSpec((TQ, LANES), lambda j, i: (i, j)),
                  pl.BlockSpec((s, LANES), lambda j, i: (0, npair + j)),
                  pl.BlockSpec((s, LANES), lambda j, i: (0, 2 * npair + j))],
        out_specs=[pl.BlockSpec((TQ, LANES), lambda j, i: (i, j)),
                   pl.BlockSpec((None, None, 2 * TQ, BK), lambda j, i: (j, i, 0, 0)),
                   pl.BlockSpec((None, None, 8, LANES), lambda j, i: (j, i, 0, 0))],
        scratch_shapes=[pltpu.VMEM((2 * TQ, BK), F32), pltpu.VMEM((2 * TQ, LANES), F32)],
        compiler_params=_params(2), name=name)(qkv, qkv, qkv)


def _sb_bwd(qkv, lt, nb, doa, name):
    s = qkv.shape[0]
    nq = s // TQ

    def body(q_ref, k_ref, v_ref, lt_ref, nb_ref, do_ref, dq_ref, dk_ref, dv_ref, cp_ref, ce_ref, dqa_ref):
        i = pl.program_id(1)
        lane = lax.broadcasted_iota(jnp.int32, (TQ, LANES), 1)
        qs = _pair_stack(q_ref[...], lane)
        dos = _pair_stack(do_ref[...], lane)
        _, prefix_incl, prefix_excl, rel = _sb_consts()
        q0 = i * TQ
        n_blocks = jnp.clip(jnp.max(nb_ref[...]).astype(jnp.int32), 1, i + 1)
        first = i + 1 - n_blocks

        @pl.when(i == 0)
        def _():
            dk_ref[...] = jnp.zeros_like(dk_ref)
            dv_ref[...] = jnp.zeros_like(dv_ref)

        def block(kb, masked, cp, ce_in, dq):
            k0 = pl.multiple_of(kb * BK, BK)
            k = k_ref[pl.ds(k0, BK), :]
            v = v_ref[pl.ds(k0, BK), :]
            mask = (rel < (q0 - k0)) if masked else None
            lb, lm = _sb_scores(qs, k, mask)
            cs = _split_dot(lm, prefix_incl)
            a = jnp.exp(lb + (lt_ref[...] - (cs[:, :BK] + cp)))
            if masked:
                a = jnp.where(mask, a, 0.0)
            e = a * _dot_nt(dos, v)
            ce = _split_dot(e, prefix_excl)
            big_e = ce[:, :BK] + ce_in
            dz = e - jnp.exp(lb) * (e + big_e)
            if masked:
                dz = jnp.where(mask, dz, 0.0)
            dz = (dz * SCALE).astype(BF16)
            dk_ref[pl.ds(k0, BK), :] += _dot_tn(dz, qs)
            dv_ref[pl.ds(k0, BK), :] += _dot_tn(a.astype(BF16), dos)
            return cp + cs[:, BK:], ce_in + ce[:, BK:], dq + _dot(dz, k)

        straight = n_blocks == SB_STRAIGHT

        @pl.when(straight)
        def _():
            cp = jnp.zeros(cp_ref.shape, F32)
            ce = jnp.zeros(ce_ref.shape, F32)
            dq = jnp.zeros(dqa_ref.shape, F32)
            for j in range(SB_STRAIGHT):
                cp, ce, dq = block(first + j, j == SB_STRAIGHT - 1, cp, ce, dq)
            dqa_ref[...] = dq

        @pl.when(jnp.logical_not(straight))
        def _():
            cp_ref[...] = jnp.zeros_like(cp_ref)
            ce_ref[...] = jnp.zeros_like(ce_ref)
            dqa_ref[...] = jnp.zeros_like(dqa_ref)

            def step(it, carry):
                cp, ce, dq = block(first + it, True, cp_ref[...], ce_ref[...], dqa_ref[...])
                cp_ref[...] = cp
                ce_ref[...] = ce
                dqa_ref[...] = dq
                return carry

            lax.fori_loop(0, n_blocks, step, 0)

        dq_ref[...] = jnp.where(lane < HEAD_DIM, dqa_ref[:TQ, :], dqa_ref[TQ:, :]).astype(BF16)

    npair = SB_W // LANES
    return pl.pallas_call(
        body,
        out_shape=[jax.ShapeDtypeStruct((s, SB_W), BF16), jax.ShapeDtypeStruct((s, SB_W), F32),
                   jax.ShapeDtypeStruct((s, SB_W), F32)],
        grid=(npair, nq),
        in_specs=[pl.BlockSpec((TQ, LANES), lambda j, i: (i, j)),
                  pl.BlockSpec((s, LANES), lambda j, i: (0, npair + j)),
                  pl.BlockSpec((s, LANES), lambda j, i: (0, 2 * npair + j)),
                  pl.BlockSpec((None, None, 2 * TQ, BK), lambda j, i: (j, i, 0, 0)),
                  pl.BlockSpec((None, None, 8, LANES), lambda j, i: (j, i, 0, 0)),
                  pl.BlockSpec((TQ, LANES), lambda j, i: (i, j))],
        out_specs=[pl.BlockSpec((TQ, LANES), lambda j, i: (i, j)),
                   pl.BlockSpec((s, LANES), lambda j, i: (0, j)),
                   pl.BlockSpec((s, LANES), lambda j, i: (0, j))],
        scratch_shapes=[pltpu.VMEM((2 * TQ, BK), F32), pltpu.VMEM((2 * TQ, BK), F32),
                        pltpu.VMEM((2 * TQ, LANES), F32)],
        compiler_params=_params(2), name=name)(qkv, qkv, qkv, lt, nb, doa)


def _bucket_table():
    i = np.arange(TQ)[:, None]
    j = np.arange(2 * BK)[None, :]
    dist = np.maximum(TQ + i - j, 0)
    max_exact = N_BUCKETS // 2
    df = np.maximum(dist, 1).astype(np.float32)
    large = max_exact + (np.log(df / np.float32(max_exact)) / np.float32(math.log(MAX_DISTANCE / max_exact))
                         * np.float32(N_BUCKETS - max_exact)).astype(np.int32)
    large = np.minimum(large, N_BUCKETS - 1)
    return np.where(dist < max_exact, dist, large).astype(np.int32)


def _swa_align_in(t, lane, g):
    tf = t.astype(F32)
    tr = pltpu.roll(tf, HEAD_DIM, 1)
    gmask = (lane >= HEAD_DIM) == (g == 1)
    top = jnp.where(gmask, jnp.where(g == 0, tf, tr), 0.0)
    bot = jnp.where(gmask, jnp.where(g == 1, tf, tr), 0.0)
    return jnp.concatenate([top, bot], axis=0).astype(BF16)


def _swa_align_out(t, lane, g):
    top, bot = t[:TQ, :], t[TQ:, :]
    top = jnp.where(g == 0, top, pltpu.roll(top, HEAD_DIM, 1))
    bot = jnp.where(g == 1, bot, pltpu.roll(bot, HEAD_DIM, 1))
    return jnp.where(lane < HEAD_DIM, top, bot)


def _swa_bias(bias_ref, bucket_ref, rb_ref, j):
    for hh in range(2):
        def add(b, acc):
            return acc + jnp.where(bucket_ref[...] == b, rb_ref[b, 2 * j + hh], 0.0)
        bias_ref[hh * TQ:(hh + 1) * TQ, :] = lax.fori_loop(0, N_BUCKETS, add, jnp.zeros((TQ, 2 * BK), F32))


def _swa_probs(qs, k2, bias, sink_ref, i, j):
    s = _dot_nt(qs, k2) * SCALE + bias
    row = lax.broadcasted_iota(jnp.int32, (2 * TQ, 2 * BK), 0)
    col = lax.broadcasted_iota(jnp.int32, (2 * TQ, 2 * BK), 1)
    dist = TQ + jnp.where(row >= TQ, row - TQ, row) - col
    valid = (dist >= 0) & (dist < WINDOW) & ((col >= BK) | (i > 0))
    s = jnp.where(valid, s, NEG)
    row1 = lax.broadcasted_iota(jnp.int32, (2 * TQ, 1), 0)
    sink = jnp.where(row1 < TQ, sink_ref[2 * j], sink_ref[2 * j + 1])
    m = jnp.maximum(jnp.max(s, axis=1, keepdims=True), sink)
    e = jnp.exp(s - m)
    es = jnp.exp(sink - m)
    den = jnp.sum(e, axis=1, keepdims=True) + es
    return e / den, es / den


def _swa_kv(ref, i):
    prev = pl.multiple_of(jnp.maximum(i - 1, 0) * BK, BK)
    cur = pl.multiple_of(i * BK, BK)
    return jnp.concatenate([ref[pl.ds(prev, BK), :], ref[pl.ds(cur, BK), :]], axis=0), prev, cur


def _swa_fwd(qkv, sinks, rel_bias, name):
    s = qkv.shape[0]
    nq = s // TQ
    npair = SW_QW // LANES
    qcol = 3 * SB_W // LANES
    bucket = jnp.asarray(_bucket_table())

    def body(q_ref, k_ref, v_ref, bucket_ref, sink_ref, rb_ref, o_ref, bias_ref):
        j = pl.program_id(0)
        step = pl.program_id(1)
        g = j // 2
        lane = lax.broadcasted_iota(jnp.int32, (TQ, LANES), 1)

        @pl.when(step == 0)
        def _():
            _swa_bias(bias_ref, bucket_ref, rb_ref, j)

        for b in range(SWA_QB):
            i = step * SWA_QB + b
            rows = slice(b * TQ, (b + 1) * TQ)
            qs = _swa_align_in(q_ref[rows, :], lane, g)
            k2, _, _ = _swa_kv(k_ref, i)
            v2, _, _ = _swa_kv(v_ref, i)
            pr, _ = _swa_probs(qs, k2, bias_ref[...], sink_ref, i, j)
            o_ref[rows, :] = _swa_align_out(_dot(pr.astype(BF16), v2), lane, g).astype(BF16)

    assert nq % SWA_QB == 0
    return pl.pallas_call(
        body, out_shape=jax.ShapeDtypeStruct((s, SW_QW), BF16), grid=(npair, nq // SWA_QB),
        in_specs=[pl.BlockSpec((SWA_QB * TQ, LANES), lambda j, i: (i, qcol + j)),
                  pl.BlockSpec((s, LANES), lambda j, i: (0, qcol + npair)),
                  pl.BlockSpec((s, LANES), lambda j, i: (0, qcol + npair + 1)),
                  pl.BlockSpec((TQ, 2 * BK), lambda j, i: (0, 0)),
                  pl.BlockSpec(memory_space=pltpu.SMEM),
                  pl.BlockSpec(memory_space=pltpu.SMEM)],
        out_specs=pl.BlockSpec((SWA_QB * TQ, LANES), lambda j, i: (i, j)),
        scratch_shapes=[pltpu.VMEM((2 * TQ, 2 * BK), F32)],
        compiler_params=_params(2), name=name)(qkv, qkv, qkv, bucket, sinks, rel_bias)


def _swa_bwd(qkv, ob, dob, sinks, rel_bias, name):
    s = qkv.shape[0]
    nq = s // TQ
    npair = SW_QW // LANES
    qcol = 3 * SB_W // LANES
    bucket = jnp.asarray(_bucket_table())

    def body(q_ref, k_ref, v_ref, o_ref, do_ref, bucket_ref, sink_ref, rb_ref,
             dq_ref, dk_ref, dv_ref, dsink_ref, drel_ref, bias_ref, dsacc_ref):
        j = pl.program_id(0)
        step = pl.program_id(1)
        g = j // 2
        lane = lax.broadcasted_iota(jnp.int32, (TQ, LANES), 1)
        row8 = lax.broadcasted_iota(jnp.int32, (SW_HEADS, LANES), 0)
        lane8 = lax.broadcasted_iota(jnp.int32, (SW_HEADS, LANES), 1)

        @pl.when((step == 0) & (j == 0))
        def _():
            dk_ref[...] = jnp.zeros_like(dk_ref)
            dv_ref[...] = jnp.zeros_like(dv_ref)
            dsink_ref[...] = jnp.zeros_like(dsink_ref)
            drel_ref[...] = jnp.zeros_like(drel_ref)

        @pl.when(step == 0)
        def _():
            _swa_bias(bias_ref, bucket_ref, rb_ref, j)
            dsacc_ref[...] = jnp.zeros_like(dsacc_ref)

        ds_sum = jnp.zeros(dsacc_ref.shape, F32)
        dsink = jnp.zeros((SW_HEADS, LANES), F32)
        for b in range(SWA_QB):
            i = step * SWA_QB + b
            rows = slice(b * TQ, (b + 1) * TQ)
            qs = _swa_align_in(q_ref[rows, :], lane, g)
            do = do_ref[rows, :]
            dos = _swa_align_in(do, lane, g)
            dof = do.astype(F32) * o_ref[rows, :].astype(F32)
            d0 = jnp.sum(jnp.where(lane < HEAD_DIM, dof, 0.0), axis=1, keepdims=True)
            d1 = jnp.sum(jnp.where(lane >= HEAD_DIM, dof, 0.0), axis=1, keepdims=True)
            delta = jnp.concatenate([d0, d1], axis=0)
            k2, prev, cur = _swa_kv(k_ref, i)
            v2, _, _ = _swa_kv(v_ref, i)
            pr, psink = _swa_probs(qs, k2, bias_ref[...], sink_ref, i, j)
            ds = pr * (_dot_nt(dos, v2) - delta)
            ds_sum = ds_sum + ds
            sd = psink * delta
            ds0 = -jnp.sum(sd[:TQ, :], axis=0, keepdims=True)
            ds1 = -jnp.sum(sd[TQ:, :], axis=0, keepdims=True)
            dsink = dsink + jnp.where(row8 == 2 * j, ds0, jnp.where(row8 == 2 * j + 1, ds1, 0.0))
            dsb = ds.astype(BF16)
            dq_ref[rows, :] = _swa_align_out(_dot(dsb, k2) * SCALE, lane, g).astype(BF16)
            dk2 = _dot_tn(dsb, qs) * SCALE
            dv2 = _dot_tn(pr.astype(BF16), dos)
            dk_ref[pl.ds(prev, BK), :] += dk2[:BK, :]
            dk_ref[pl.ds(cur, BK), :] += dk2[BK:, :]
            dv_ref[pl.ds(prev, BK), :] += dv2[:BK, :]
            dv_ref[pl.ds(cur, BK), :] += dv2[BK:, :]
        dsacc_ref[...] += ds_sum
        dsink_ref[...] += dsink

        @pl.when(step == nq // SWA_QB - 1)
        def _():
            for hh in range(2):
                def red(b, acc):
                    val = _sum_all(jnp.where(bucket_ref[...] == b, dsacc_ref[hh * TQ:(hh + 1) * TQ, :], 0.0))
                    return jnp.where((row8 == 2 * j + hh) & (lane8 == b), val, acc)
                drel_ref[...] += lax.fori_loop(0, N_BUCKETS, red, jnp.zeros((SW_HEADS, LANES), F32))

    whole = lambda j, i: (0, 0)
    return pl.pallas_call(
        body,
        out_shape=[jax.ShapeDtypeStruct((s, SW_QW), BF16), jax.ShapeDtypeStruct((s, LANES), F32),
                   jax.ShapeDtypeStruct((s, LANES), F32), jax.ShapeDtypeStruct((SW_HEADS, LANES), F32),
                   jax.ShapeDtypeStruct((SW_HEADS, LANES), F32)],
        grid=(npair, nq // SWA_QB),
        in_specs=[pl.BlockSpec((SWA_QB * TQ, LANES), lambda j, i: (i, qcol + j)),
                  pl.BlockSpec((s, LANES), lambda j, i: (0, qcol + npair)),
                  pl.BlockSpec((s, LANES), lambda j, i: (0, qcol + npair + 1)),
                  pl.BlockSpec((SWA_QB * TQ, LANES), lambda j, i: (i, j)),
                  pl.BlockSpec((SWA_QB * TQ, LANES), lambda j, i: (i, j)),
                  pl.BlockSpec((TQ, 2 * BK), whole),
                  pl.BlockSpec(memory_space=pltpu.SMEM),
                  pl.BlockSpec(memory_space=pltpu.SMEM)],
        out_specs=[pl.BlockSpec((SWA_QB * TQ, LANES), lambda j, i: (i, j)),
                   pl.BlockSpec((s, LANES), whole), pl.BlockSpec((s, LANES), whole),
                   pl.BlockSpec((SW_HEADS, LANES), whole), pl.BlockSpec((SW_HEADS, LANES), whole)],
        scratch_shapes=[pltpu.VMEM((2 * TQ, 2 * BK), F32), pltpu.VMEM((2 * TQ, 2 * BK), F32)],
        compiler_params=_params(2), name=name)(qkv, qkv, qkv, ob, dob, bucket, sinks, rel_bias)


def _acc_init(i, *refs):
    @pl.when(i == 0)
    def _():
        for r in refs:
            r[...] = jnp.zeros_like(r)


def _loss_bwd(x3, target, g, name):
    d = x3.shape[1]

    def body(x_ref, t_ref, g_ref, dx_ref, dg_ref, loss_ref):
        _acc_init(pl.program_id(0), dg_ref, loss_ref)
        x = x_ref[...]
        gv = g_ref[...]
        diff = _rms(x, gv) - t_ref[...]
        loss_ref[...] += 0.5 * jnp.sum(jnp.mean(jnp.square(diff), axis=-1, keepdims=True), axis=0, keepdims=True)
        dx, dg = _rms_bwd(diff * (1.0 / d), x, gv)
        dx_ref[...] = dx
        dg_ref[...] += dg

    return _rowcall(name, body, [x3, target], [g], [(d, F32)], [((1, d), F32), ((1, LANES), F32)])


def _ple_bwd(dx3, pe, gt, x2, g, wpg, name):
    d = x2.shape[1]

    def body(dx3_ref, pe_ref, gt_ref, x_ref, g_ref, w_ref, dpe_ref, dgt_ref, dx2_ref, dg_ref):
        _acc_init(pl.program_id(0), dg_ref)
        dx3 = dx3_ref[...]
        sg = _sigmoid(gt_ref[...])
        dpe_ref[...] = (dx3 * sg).astype(BF16)
        dgt = (dx3 * pe_ref[...] * sg * (1.0 - sg)).astype(BF16)
        dgt_ref[...] = dgt
        dx, dg = _rms_bwd(_dot_nt(dgt, w_ref[...]), x_ref[...], g_ref[...])
        dx2_ref[...] = dx3 + dx
        dg_ref[...] += dg

    return _rowcall(name, body, [dx3, pe, gt, x2], [g, wpg], [(d, BF16), (d, BF16), (d, F32)], [((1, d), F32)])


def _ff2_bwd(dx2, u, w2, name):
    d = dx2.shape[1]
    dff = u.shape[1]

    def body(dx_ref, u_ref, w_ref, du_ref, dxb_ref):
        dxb = dx_ref[...].astype(BF16)
        dxb_ref[...] = dxb
        du_ref[...] = (_dot_nt(dxb, w_ref[...]) * (2.0 * jnp.maximum(u_ref[...], 0.0))).astype(BF16)

    return _rowcall(name, body, [dx2, u], [w2], [(dff, BF16), (d, BF16)])


def _ff1_bwd(du, dx2, x1, g, w1, name):
    d = x1.shape[1]

    def body(du_ref, dx2_ref, x_ref, g_ref, w_ref, dx1_ref, dx1b_ref, dg_ref):
        _acc_init(pl.program_id(0), dg_ref)
        dx, dg = _rms_bwd(_dot_nt(du_ref[...], w_ref[...]), x_ref[...], g_ref[...])
        dx1 = dx2_ref[...] + dx
        dx1_ref[...] = dx1
        dx1b_ref[...] = dx1.astype(BF16)
        dg_ref[...] += dg

    return _rowcall(name, body, [du, dx2, x1], [g, w1], [(d, F32), (d, BF16)], [((1, d), F32)])


def _mixer_bwd(dx1b, gates, oa, ob, wo, wua, wub, name):
    d = dx1b.shape[1]

    def body(dx_ref, gate_ref, oa_ref, ob_ref, wo_ref, wua_ref, wub_ref,
             dya_ref, dyb_ref, dgate_ref, doa_ref, dob_ref):
        dm = _dot_nt(dx_ref[...], wo_ref[...])
        sa = _sigmoid(gate_ref[:, :d])
        sb = _sigmoid(gate_ref[:, d:])
        ya = _dot(oa_ref[...], wua_ref[...])
        yb = _dot(ob_ref[...], wub_ref[...])
        dya = (dm * sa).astype(BF16)
        dyb = (dm * sb).astype(BF16)
        dya_ref[...] = dya
        dyb_ref[...] = dyb
        dgate_ref[:, :d] = (dm * ya * sa * (1.0 - sa)).astype(BF16)
        dgate_ref[:, d:] = (dm * yb * sb * (1.0 - sb)).astype(BF16)
        doa_ref[...] = _dot_nt(dya, wua_ref[...]).astype(BF16)
        dob_ref[...] = _dot_nt(dyb, wub_ref[...]).astype(BF16)

    return _rowcall(name, body, [dx1b, gates, oa, ob], [wo, wua, wub],
                    [(d, BF16), (d, BF16), (2 * d, BF16), (SB_W, BF16), (SW_QW, BF16)])


def _inproj_bwd(dqkv, dgates, dx1, x, g, w, name):
    d = x.shape[1]

    def body(dqkv_ref, dgate_ref, dx1_ref, x_ref, g_ref, w_ref, dx_ref, dg_ref):
        _acc_init(pl.program_id(0), dg_ref)
        dh = _dot_nt(dqkv_ref[...], w_ref[:, :QKV_W]) + _dot_nt(dgate_ref[...], w_ref[:, QKV_W:])
        dx, dg = _rms_bwd(dh, x_ref[...], g_ref[...])
        dx_ref[...] = dx1_ref[...] + dx
        dg_ref[...] += dg

    return _rowcall(name, body, [dqkv, dgates, dx1, x], [g, w], [(d, F32)], [((1, d), F32)])


def _tile(n, cap):
    assert n % LANES == 0
    return max(t for t in range(LANES, min(n, cap) + 1, LANES) if n % t == 0)


def _mm_tn(a, b, name, nshard=1):
    s, ka = a.shape
    nb = b.shape[1]
    n = nb // nshard
    ta = _tile(ka, 512)
    tb = _tile(n, 512)
    per = n // tb

    def body(a_ref, b_ref, o_ref):
        o_ref[...] = _dot_tn(a_ref[...].astype(BF16), b_ref[...].astype(BF16))

    return pl.pallas_call(
        body, out_shape=jax.ShapeDtypeStruct((nshard, ka, n), F32), grid=(nb // tb, ka // ta),
        in_specs=[pl.BlockSpec((s, ta), lambda jb, ia: (0, ia)), pl.BlockSpec((s, tb), lambda jb, ia: (0, jb))],
        out_specs=pl.BlockSpec((None, ta, tb), lambda jb, ia: (jb // per, ia, jb % per)),
        compiler_params=_params(2), name=name)(a, b)


def _place():
    return lax.axis_index("x"), lax.axis_index("y"), lax.axis_index("c")


def _chip_peer(x, y, k):
    return (x ^ (k >> 1), y ^ (k & 1))


def _cast_bf16(w, r, name):
    l, k, n = w.shape
    tk = min(k, 256)

    def body(r_ref, w_ref, o_ref):
        o_ref[...] = w_ref[...].astype(BF16)

    return pl.pallas_call(
        body, out_shape=jax.ShapeDtypeStruct((l, 4, k, n), BF16),
        grid_spec=pltpu.PrefetchScalarGridSpec(
            num_scalar_prefetch=1, grid=(l, k // tk),
            in_specs=[pl.BlockSpec((None, tk, n), lambda a, i, r_ref: (a, i, 0))],
            out_specs=pl.BlockSpec((None, None, tk, n), lambda a, i, r_ref: (a, r_ref[0], i, 0))),
        compiler_params=_params(2), name=name)(r, w)


def _all_gather_weights(bufs):
    nt = len(bufs)

    def body(*refs):
        ins, outs = refs[:nt], refs[nt:2 * nt]
        send_sems, recv_sems = refs[2 * nt:]
        x, y, c = _place()
        r = 2 * x + y
        sibling = (x, y, 1 - c)
        first = []
        for t in range(nt):
            src, dst = ins[t], outs[t]
            for k in (1, 2, 3):
                cp = pltpu.make_async_remote_copy(
                    src_ref=src.at[:, r, c], dst_ref=dst.at[:, r, c], send_sem=send_sems.at[t, k - 1],
                    recv_sem=recv_sems.at[t, k - 1], device_id=(*_chip_peer(x, y, k), c), device_id_type=MESH)
                cp.start()
                first.append(cp)
        fwd = []
        for t in range(nt):
            dst = outs[t]
            for k in (1, 2, 3):
                q = r ^ k
                slab = dst.at[:, q, c]
                pltpu.make_async_remote_copy(
                    src_ref=slab, dst_ref=slab, send_sem=send_sems.at[t, k - 1], recv_sem=recv_sems.at[t, k - 1],
                    device_id=sibling, device_id_type=MESH).wait_recv()
                cp = pltpu.make_async_remote_copy(
                    src_ref=slab, dst_ref=slab, send_sem=send_sems.at[t, 2 + k], recv_sem=recv_sems.at[t, 2 + k],
                    device_id=sibling, device_id_type=MESH)
                cp.start()
                fwd.append(cp)
        for t in range(nt):
            dst = outs[t]
            for k in (1, 2, 3):
                other = dst.at[:, r ^ k, 1 - c]
                pltpu.make_async_remote_copy(
                    src_ref=other, dst_ref=other, send_sem=send_sems.at[t, 2 + k], recv_sem=recv_sems.at[t, 2 + k],
                    device_id=sibling, device_id_type=MESH).wait_recv()
        for cp in first + fwd:
            cp.wait_send()

    any_spec = pl.BlockSpec(memory_space=pl.ANY)
    return pl.pallas_call(
        body, out_shape=[jax.ShapeDtypeStruct(b.shape, BF16) for b in bufs],
        in_specs=[any_spec] * nt, out_specs=[any_spec] * nt,
        input_output_aliases={t: t for t in range(nt)},
        scratch_shapes=[pltpu.SemaphoreType.DMA((nt, 6)), pltpu.SemaphoreType.DMA((nt, 6))],
        name="all_gather_weights")(*bufs)


def _rs_to_sibling(grads):
    nt = len(grads)

    def body(*refs):
        ins, outs = refs[:nt], refs[nt:2 * nt]
        send_sems, recv_sems = refs[2 * nt:]
        x, y, c = _place()
        sibling = (x, y, 1 - c)
        cps = []
        for t in range(nt):
            cp = pltpu.make_async_remote_copy(
                src_ref=ins[t].at[:, 1 - c], dst_ref=outs[t], send_sem=send_sems.at[t], recv_sem=recv_sems.at[t],
                device_id=sibling, device_id_type=MESH)
            cp.start()
            cps.append(cp)
        for cp in cps:
            cp.wait()

    any_spec = pl.BlockSpec(memory_space=pl.ANY)
    return pl.pallas_call(
        body, out_shape=[jax.ShapeDtypeStruct((4,) + g.shape[2:], F32) for g in grads],
        in_specs=[any_spec] * nt, out_specs=[any_spec] * nt,
        scratch_shapes=[pltpu.SemaphoreType.DMA((nt,)), pltpu.SemaphoreType.DMA((nt,))],
        name="rs_to_sibling")(*grads)


def _add_half(g, recv, cr, name):
    _, _, k2, n = g.shape
    tk = min(k2, 256)

    def body(cr_ref, g_ref, r_ref, sums_ref, mine_ref):
        val = (g_ref[...] + r_ref[...]).astype(BF16)
        sums_ref[...] = val

        @pl.when(pl.program_id(1) == cr_ref[1])
        def _():
            mine_ref[...] = val

    return pl.pallas_call(
        body, out_shape=[jax.ShapeDtypeStruct((4, k2, n), BF16)] * 2,
        grid_spec=pltpu.PrefetchScalarGridSpec(
            num_scalar_prefetch=1, grid=(k2 // tk, 4),
            in_specs=[pl.BlockSpec((None, None, tk, n), lambda i, q, cr_ref: (q, cr_ref[0], i, 0)),
                      pl.BlockSpec((None, tk, n), lambda i, q, cr_ref: (q, i, 0))],
            out_specs=[pl.BlockSpec((None, tk, n), lambda i, q, cr_ref: (q, i, 0)),
                       pl.BlockSpec((None, tk, n), lambda i, q, cr_ref: (cr_ref[1], i, 0))]),
        compiler_params=_params(2), name=name)(cr, g, recv)


def _rs_to_chips(sums, parts):
    nt = len(sums)

    def body(*refs):
        ins, outs = refs[:nt], refs[2 * nt:3 * nt]
        send_sems, recv_sems = refs[3 * nt:]
        x, y, c = _place()
        r = 2 * x + y
        cps = []
        for t in range(nt):
            for k in (1, 2, 3):
                cp = pltpu.make_async_remote_copy(
                    src_ref=ins[t].at[r ^ k], dst_ref=outs[t].at[r], send_sem=send_sems.at[t, k - 1],
                    recv_sem=recv_sems.at[t, k - 1], device_id=(*_chip_peer(x, y, k), c), device_id_type=MESH)
                cp.start()
                cps.append(cp)
        for cp in cps:
            cp.wait()

    any_spec = pl.BlockSpec(memory_space=pl.ANY)
    return pl.pallas_call(
        body, out_shape=[jax.ShapeDtypeStruct(p.shape, p.dtype) for p in parts],
        in_specs=[any_spec] * (2 * nt), out_specs=[any_spec] * nt,
        input_output_aliases={nt + t: t for t in range(nt)},
        scratch_shapes=[pltpu.SemaphoreType.DMA((nt, 3)), pltpu.SemaphoreType.DMA((nt, 3))],
        name="rs_to_chips")(*sums, *parts)


def _sum4(parts, cr, name):
    _, k2, n = parts.shape
    tk = min(k2, 256)

    def body(cr_ref, p_ref, o_ref):
        p = p_ref[...].astype(F32)
        o_ref[...] = ((p[0] + p[1]) + p[2]) + p[3]

    return pl.pallas_call(
        body, out_shape=jax.ShapeDtypeStruct((2, k2, n), F32),
        grid_spec=pltpu.PrefetchScalarGridSpec(
            num_scalar_prefetch=1, grid=(k2 // tk,),
            in_specs=[pl.BlockSpec((4, tk, n), lambda i, cr_ref: (0, i, 0))],
            out_specs=pl.BlockSpec((None, tk, n), lambda i, cr_ref: (cr_ref[0], i, 0))),
        compiler_params=_params(1), name=name)(cr, parts)


def _exchange_halves(both):
    nt = len(both)

    def body(*refs):
        ins, outs = refs[:nt], refs[nt:2 * nt]
        send_sems, recv_sems = refs[2 * nt:]
        x, y, c = _place()
        cps = []
        for t in range(nt):
            cp = pltpu.make_async_remote_copy(
                src_ref=ins[t].at[c], dst_ref=outs[t].at[c], send_sem=send_sems.at[t], recv_sem=recv_sems.at[t],
                device_id=(x, y, 1 - c), device_id_type=MESH)
            cp.start()
            cps.append(cp)
        for cp in cps:
            cp.wait()

    any_spec = pl.BlockSpec(memory_space=pl.ANY)
    return pl.pallas_call(
        body, out_shape=[jax.ShapeDtypeStruct(b.shape, F32) for b in both],
        in_specs=[any_spec] * nt, out_specs=[any_spec] * nt,
        input_output_aliases={t: t for t in range(nt)},
        scratch_shapes=[pltpu.SemaphoreType.DMA((nt,)), pltpu.SemaphoreType.DMA((nt,))],
        name="exchange_halves")(*both)


def _adamw_math(w, g, m, v):
    m = ADAM_B1 * m + (1.0 - ADAM_B1) * g
    v = ADAM_B2 * v + (1.0 - ADAM_B2) * jnp.square(g)
    m_hat = m / (1.0 - ADAM_B1 ** ADAM_STEP)
    v_hat = v / (1.0 - ADAM_B2 ** ADAM_STEP)
    delta = -ADAM_LR * (m_hat / (jnp.sqrt(v_hat) + ADAM_EPS) + ADAM_WD * w)
    return delta, m, v


def _adamw(w, m, v, g0, g1, name):
    _, k, n = w.shape
    tk = min(k, 256)
    nk = k // tk

    def body(w_ref, m_ref, v_ref, g0_ref, g1_ref, grad_ref, delta_ref, nm_ref, nv_ref):
        g = jnp.where(pl.program_id(0) == 0, g0_ref[...], g1_ref[...])
        delta, nm, nv = _adamw_math(w_ref[...], g, m_ref[...], v_ref[...])
        grad_ref[...] = g
        delta_ref[...] = delta
        nm_ref[...] = nm
        nv_ref[...] = nv

    lay = pl.BlockSpec((None, tk, n), lambda a, i: (a, i, 0))
    g0_spec = pl.BlockSpec((tk, n), lambda a, i: (jnp.where(a == 0, i, nk - 1), 0))
    g1_spec = pl.BlockSpec((tk, n), lambda a, i: (jnp.where(a == 1, i, 0), 0))
    return pl.pallas_call(
        body, out_shape=[jax.ShapeDtypeStruct(w.shape, F32)] * 4, grid=(2, nk),
        in_specs=[lay, lay, lay, g0_spec, g1_spec], out_specs=[lay] * 4,
        compiler_params=_params(2), name=name)(w, m, v, g0, g1)


def _small_allreduce_adamw(gpart, w, m, v):
    shape = gpart.shape

    def body(g_ref, w_ref, m_ref, v_ref, gsum_ref, delta_ref, nm_ref, nv_ref, recv_ref, send_sems, recv_sems):
        x, y, c = _place()
        me = 4 * x + 2 * y + c
        recv_ref[me] = g_ref[...]
        cps = []
        for k in range(1, 8):
            peer = (x ^ (k >> 2), y ^ ((k >> 1) & 1), c ^ (k & 1))
            cp = pltpu.make_async_remote_copy(
                src_ref=g_ref, dst_ref=recv_ref.at[me], send_sem=send_sems.at[k - 1], recv_sem=recv_sems.at[k - 1],
                device_id=peer, device_id_type=MESH)
            cp.start()
            cps.append(cp)
        for cp in cps:
            cp.wait()
        g = recv_ref[0]
        for dev in range(1, 8):
            g = g + recv_ref[dev]
        delta, nm, nv = _adamw_math(w_ref[...], g, m_ref[...], v_ref[...])
        gsum_ref[...] = g
        delta_ref[...] = delta
        nm_ref[...] = nm
        nv_ref[...] = nv

    vm = pl.BlockSpec(memory_space=pltpu.VMEM)
    return pl.pallas_call(
        body, out_shape=[jax.ShapeDtypeStruct(shape, F32)] * 4, in_specs=[vm] * 4, out_specs=[vm] * 4,
        scratch_shapes=[pltpu.VMEM((8,) + shape, F32), pltpu.SemaphoreType.DMA((7,)), pltpu.SemaphoreType.DMA((7,))],
        name="small_allreduce_adamw")(gpart, w, m, v)


BIG = ("w_in", "w_up_a", "w_up_b", "w_o", "w_ff1", "w_ff2", "w_pe", "w_pg")
COL_SHARDED = ("w_in", "w_up_a", "w_up_b", "w_ff1", "w_pe")
ROW_SHARDED = ("w_o", "w_ff2", "w_pg")
SMALL_ROWS = 16


def _pack_small(g_mix, g_mlp, g_pe, g_final, sinks, rel_bias, loss=None):
    d = g_final.shape[0]
    row = lambda v: jnp.pad(v.reshape(1, -1), ((0, 0), (0, d - v.size)))
    rows = [g_mix, g_mlp, g_pe, g_final.reshape(1, d),
            jnp.zeros((1, d), F32) if loss is None else row(loss), row(sinks), row(rel_bias)]
    out = jnp.concatenate(rows, axis=0)
    return jnp.pad(out, ((0, SMALL_ROWS - out.shape[0]), (0, 0)))


def _unpack_small(a, sinks_shape, rel_shape):
    return (a[0:2], a[2:4], a[4:6], a[6], a[8, :sinks_shape[0] * sinks_shape[1]].reshape(sinks_shape),
            a[9, :rel_shape[0] * rel_shape[1]].reshape(rel_shape))


def kernel(x, p, w_in, w_up_a, w_up_b, w_o, w_ff1, w_ff2, w_pe, w_pg, g_mix, g_mlp, g_pe, g_final, sinks, rel_bias, loss_target, m_w_in, m_w_up_a, m_w_up_b, m_w_o, m_w_ff1, m_w_ff2, m_w_pe, m_w_pg, m_g_mix, m_g_mlp, m_g_pe, m_g_final, m_sinks, m_rel_bias, v_w_in, v_w_up_a, v_w_up_b, v_w_o, v_w_ff1, v_w_ff2, v_w_pe, v_w_pg, v_g_mix, v_g_mlp, v_g_pe, v_g_final, v_sinks, v_rel_bias):
    depth = w_in.shape[0]
    assert depth == 2
    x0 = x[0]
    target = loss_target[0]
    d = x0.shape[1]
    wl = dict(w_in=w_in, w_up_a=w_up_a, w_up_b=w_up_b, w_o=w_o, w_ff1=w_ff1, w_ff2=w_ff2, w_pe=w_pe, w_pg=w_pg)
    ml = dict(w_in=m_w_in, w_up_a=m_w_up_a, w_up_b=m_w_up_b, w_o=m_w_o, w_ff1=m_w_ff1, w_ff2=m_w_ff2, w_pe=m_w_pe, w_pg=m_w_pg)
    vl = dict(w_in=v_w_in, w_up_a=v_w_up_a, w_up_b=v_w_up_b, w_o=v_w_o, w_ff1=v_w_ff1, w_ff2=v_w_ff2, w_pe=v_w_pe, w_pg=v_w_pg)
    c_idx = lax.axis_index("c").astype(jnp.int32)
    r_idx = (2 * lax.axis_index("x") + lax.axis_index("y")).astype(jnp.int32)
    cr = jnp.stack([c_idx, r_idx])

    bufs = []
    for n in BIG:
        w = wl[n]
        l, k, nn = w.shape
        bufs.append(_cast_bf16(w, r_idx.reshape(1), "cast_" + n).reshape(l, 4, 2, k // 2, nn))
    gathered = _all_gather_weights(bufs)
    full = {}
    for n, gth in zip(BIG, gathered):
        l, _, _, k2, nn = gth.shape
        gth = gth.reshape(l, 4, 2 * k2, nn)
        if n in COL_SHARDED:
            full[n] = gth.transpose(0, 2, 1, 3).reshape(l, 2 * k2, 4 * nn)
        else:
            full[n] = gth.reshape(l, 8 * k2, nn)

    saved = []
    xi = x0
    for i in range(depth):
        st = dict(x0=xi)
        gm = g_mix[i].reshape(1, d)
        st["h1"], st["qkv"], st["gates"] = _inproj_fwd(xi, gm, full["w_in"][i], f"inproj_fwd_{i}")
        st["oa"], st["lt"], st["nb"] = _sb_fwd(st["qkv"], f"sb_fwd_{i}")
        st["ob"] = _swa_fwd(st["qkv"], sinks[i], rel_bias, f"swa_fwd_{i}")
        st["m"], st["x1"] = _mixer_fwd(st["oa"], st["ob"], st["gates"], xi, full["w_up_a"][i], full["w_up_b"][i],
                                       full["w_o"][i], f"mixer_fwd_{i}")
        st["h2"], st["u"], st["a"] = _ff1_fwd(st["x1"], g_mlp[i].reshape(1, d), full["w_ff1"][i], f"ff1_fwd_{i}")
        st["x2"] = _ff2_fwd(st["a"], st["x1"], full["w_ff2"][i], f"ff2_fwd_{i}")
        st["pb"], st["h3"], st["pe"], st["gt"], xi = _ple_fwd(p[i, 0], st["x2"], g_pe[i].reshape(1, d),
                                                            full["w_pe"][i], full["w_pg"][i], f"ple_fwd_{i}")
        saved.append(st)

    dx, dg_final, loss_part = _loss_bwd(xi, target, g_final.reshape(1, d), "loss_bwd")
    gw = {n: [None] * depth for n in BIG}
    dg_mix, dg_mlp, dg_pe, dsinks = [None] * depth, [None] * depth, [None] * depth, [None] * depth
    drel = jnp.zeros((SW_HEADS, LANES), F32)
    for i in reversed(range(depth)):
        st = saved[i]
        dpe, dgt, dx2, dg_pe[i] = _ple_bwd(dx, st["pe"], st["gt"], st["x2"], g_pe[i].reshape(1, d),
                                           full["w_pg"][i], f"ple_bwd_{i}")
        gw["w_pe"][i] = _mm_tn(st["pb"], dpe, f"dw_pe_{i}", 4)
        gw["w_pg"][i] = _mm_tn(st["h3"], dgt, f"dw_pg_{i}")
        du, dx2b = _ff2_bwd(dx2, st["u"], full["w_ff2"][i], f"ff2_bwd_{i}")
        gw["w_ff2"][i] = _mm_tn(st["a"], dx2b, f"dw_ff2_{i}")
        gw["w_ff1"][i] = _mm_tn(st["h2"], du, f"dw_ff1_{i}", 4)
        dx1, dx1b, dg_mlp[i] = _ff1_bwd(du, dx2, st["x1"], g_mlp[i].reshape(1, d), full["w_ff1"][i], f"ff1_bwd_{i}")
        gw["w_o"][i] = _mm_tn(st["m"], dx1b, f"dw_o_{i}")
        dya, dyb, dgates, doa, dob = _mixer_bwd(dx1b, st["gates"], st["oa"], st["ob"], full["w_o"][i],
                                                full["w_up_a"][i], full["w_up_b"][i], f"mixer_bwd_{i}")
        gw["w_up_a"][i] = _mm_tn(st["oa"], dya, f"dw_up_a_{i}", 4)
        gw["w_up_b"][i] = _mm_tn(st["ob"], dyb, f"dw_up_b_{i}", 4)
        dqa, dka, dva = _sb_bwd(st["qkv"], st["lt"], st["nb"], doa, f"sb_bwd_{i}")
        dqb, dkb, dvb, dsk, drl = _swa_bwd(st["qkv"], st["ob"], dob, sinks[i], rel_bias, f"swa_bwd_{i}")
        dsinks[i] = dsk[:, 0]
        drel = drel + drl
        dqkv = jnp.concatenate([dqa, dka.astype(BF16), dva.astype(BF16), dqb, dkb.astype(BF16), dvb.astype(BF16)],
                               axis=1)
        dw_in = jnp.concatenate([_mm_tn(st["h1"], dqkv, f"dw_in_qkv_{i}")[0],
                                 _mm_tn(st["h1"], dgates, f"dw_in_gate_{i}")[0]], axis=1)
        gw["w_in"][i] = dw_in.reshape(d, 4, dw_in.shape[1] // 4).transpose(1, 0, 2)
        dx, dg_mix[i] = _inproj_bwd(dqkv, dgates, dx1, st["x0"], g_mix[i].reshape(1, d), full["w_in"][i],
                                    f"inproj_bwd_{i}")
    grad_x = dx[None]

    tensors = []
    for n in BIG:
        for i in range(depth):
            g = gw[n][i]
            if n in ROW_SHARDED:
                ka, nb = g.shape[1:]
                g = g.reshape(4, ka // 4, nb)
            _, k, nn = g.shape
            tensors.append(g.reshape(4, 2, k // 2, nn))
    from_sibling = _rs_to_sibling(tensors)
    added = [_add_half(g, r, cr, f"add_half_{t}") for t, (g, r) in enumerate(zip(tensors, from_sibling))]
    from_chips = _rs_to_chips([a[0] for a in added], [a[1] for a in added])
    both = _exchange_halves([_sum4(pc, cr, f"sum4_{t}") for t, pc in enumerate(from_chips)])
    outs = {}
    for ti, n in enumerate(BIG):
        g0 = both[2 * ti].reshape(wl[n].shape[1:])
        g1 = both[2 * ti + 1].reshape(wl[n].shape[1:])
        outs[n] = _adamw(wl[n], ml[n], vl[n], g0, g1, "adamw_" + n)

    drel_bias = drel[:, :N_BUCKETS].T
    gsmall = _pack_small(jnp.concatenate(dg_mix, 0), jnp.concatenate(dg_mlp, 0), jnp.concatenate(dg_pe, 0),
                         dg_final[0], jnp.stack(dsinks), drel_bias, loss_part[0, :1])
    wsmall = _pack_small(g_mix, g_mlp, g_pe, g_final, sinks, rel_bias)
    msmall = _pack_small(m_g_mix, m_g_mlp, m_g_pe, m_g_final, m_sinks, m_rel_bias)
    vsmall = _pack_small(v_g_mix, v_g_mlp, v_g_pe, v_g_final, v_sinks, v_rel_bias)
    small = _small_allreduce_adamw(gsmall, wsmall, msmall, vsmall)
    loss = small[0][7, 0]
    small = [_unpack_small(a, sinks.shape, rel_bias.shape) for a in small]

    result = [loss, grad_x]
    for kind in range(4):
        result += [outs[n][kind] for n in BIG]
        result += list(small[kind])
    return tuple(result)
```

```python
import functools
import math

import numpy as np
import jax
import jax.numpy as jnp
from jax import lax
from jax.experimental import pallas as pl
from jax.experimental.pallas import tpu as pltpu

F32 = jnp.float32
BF16 = jnp.bfloat16
MESH = pl.DeviceIdType.MESH

HEAD_DIM = 64
SB_HEADS = 8
SW_HEADS = 8
SW_KV_HEADS = 2
WINDOW = 128
N_BUCKETS = 32
MAX_DISTANCE = 128
EPS = 1e-6
SB_W = SB_HEADS * HEAD_DIM
SW_QW = SW_HEADS * HEAD_DIM
SW_KVW = SW_KV_HEADS * HEAD_DIM
QKV_W = 3 * SB_W + SW_QW + 2 * SW_KVW
SCALE = HEAD_DIM ** -0.5
LANES = 128
TQ = 128
BK = 128
NEG = -1e30
SB_EXHAUSTED = -106.0

ADAM_LR = 0.001
ADAM_B1 = 0.9
ADAM_B2 = 0.999
ADAM_EPS = 1e-08
ADAM_WD = 0.01
ADAM_STEP = 10

VMEM_LIMIT = 56 * 1024 * 1024


def _dot(a, b):
    return jnp.dot(a, b, preferred_element_type=F32)


def _dot_nt(a, b):
    return lax.dot_general(a, b, (((1,), (1,)), ((), ())), preferred_element_type=F32)


def _dot_tn(a, b):
    return lax.dot_general(a, b, (((0,), (0,)), ((), ())), preferred_element_type=F32)


def _sum_all(x):
    return jnp.sum(jnp.sum(x, axis=1, keepdims=True), axis=0, keepdims=True)


def _sigmoid(x):
    return 1.0 / (1.0 + jnp.exp(-x))


def _rms(x, g):
    r = lax.rsqrt(jnp.mean(x * x, axis=-1, keepdims=True) + EPS)
    return (x * r) * g


def _rms_bwd(dy, x, g):
    r = lax.rsqrt(jnp.mean(x * x, axis=-1, keepdims=True) + EPS)
    n = x * r
    dg = jnp.sum(dy * n, axis=0, keepdims=True)
    dn = dy * g
    dx = r * (dn - n * jnp.mean(dn * n, axis=-1, keepdims=True))
    return dx, dg


def _params(n_axes):
    return pltpu.CompilerParams(dimension_semantics=("arbitrary",) * n_axes, vmem_limit_bytes=VMEM_LIMIT)


def _rowcall(name, body, row_ins, const_ins, row_outs, acc_outs=(), tm=512):
    s = row_ins[0].shape[0]
    assert s % tm == 0
    in_specs = [pl.BlockSpec((tm, a.shape[1]), lambda i: (i, 0)) for a in row_ins]
    in_specs += [pl.BlockSpec(a.shape, functools.partial(lambda i, nd: (0,) * nd, nd=a.ndim)) for a in const_ins]
    out_shape = [jax.ShapeDtypeStruct((s, c), dt) for c, dt in row_outs]
    out_specs = [pl.BlockSpec((tm, c), lambda i: (i, 0)) for c, _ in row_outs]
    out_shape += [jax.ShapeDtypeStruct(sh, dt) for sh, dt in acc_outs]
    out_specs += [pl.BlockSpec(sh, functools.partial(lambda i, nd: (0,) * nd, nd=len(sh))) for sh, _ in acc_outs]
    return pl.pallas_call(body, out_shape=out_shape, grid=(s // tm,), in_specs=in_specs, out_specs=out_specs,
                          compiler_params=_params(1), name=name)(*row_ins, *const_ins)


def _inproj_fwd(x, g, w, name):
    d = x.shape[1]

    def body(x_ref, g_ref, w_ref, h_ref, qkv_ref, gate_ref):
        hb = _rms(x_ref[...], g_ref[...]).astype(BF16)
        h_ref[...] = hb
        qkv_ref[...] = _dot(hb, w_ref[:, :QKV_W]).astype(BF16)
        gate_ref[...] = _dot(hb, w_ref[:, QKV_W:])

    return _rowcall(name, body, [x], [g, w], [(d, BF16), (QKV_W, BF16), (2 * d, F32)])


def _mixer_fwd(oa, ob, gates, x, wua, wub, wo, name):
    d = x.shape[1]

    def body(oa_ref, ob_ref, gate_ref, x_ref, wua_ref, wub_ref, wo_ref, m_ref, x1_ref):
        ya = _dot(oa_ref[...], wua_ref[...])
        yb = _dot(ob_ref[...], wub_ref[...])
        m = _sigmoid(gate_ref[:, :d]) * ya + _sigmoid(gate_ref[:, d:]) * yb
        mb = m.astype(BF16)
        m_ref[...] = mb
        x1_ref[...] = x_ref[...] + _dot(mb, wo_ref[...])

    return _rowcall(name, body, [oa, ob, gates, x], [wua, wub, wo], [(d, BF16), (d, F32)])


def _ff1_fwd(x1, g, w1, name):
    d, dff = w1.shape

    def body(x_ref, g_ref, w_ref, h_ref, u_ref, a_ref):
        hb = _rms(x_ref[...], g_ref[...]).astype(BF16)
        h_ref[...] = hb
        u = _dot(hb, w_ref[...])
        u_ref[...] = u
        a_ref[...] = jnp.square(jnp.maximum(u, 0.0)).astype(BF16)

    return _rowcall(name, body, [x1], [g, w1], [(d, BF16), (dff, F32), (dff, BF16)])


def _ff2_fwd(a, x1, w2, name):
    d = x1.shape[1]

    def body(a_ref, x_ref, w_ref, o_ref):
        o_ref[...] = x_ref[...] + _dot(a_ref[...], w_ref[...])

    return _rowcall(name, body, [a, x1], [w2], [(d, F32)])[0]


def _ple_fwd(p, x2, g, wpe, wpg, name):
    d = x2.shape[1]

    def body(p_ref, x_ref, g_ref, wpe_ref, wpg_ref, pb_ref, h_ref, pe_ref, gt_ref, x3_ref):
        pb = p_ref[...].astype(BF16)
        pb_ref[...] = pb
        pe = _dot(pb, wpe_ref[...])
        x = x_ref[...]
        hb = _rms(x, g_ref[...]).astype(BF16)
        h_ref[...] = hb
        gt = _dot(hb, wpg_ref[...])
        pe_ref[...] = pe
        gt_ref[...] = gt
        x3_ref[...] = x + pe * _sigmoid(gt)

    return _rowcall(name, body, [p, x2], [g, wpe, wpg],
                    [(p.shape[1], BF16), (d, BF16), (d, F32), (d, F32), (d, F32)])


def _pair_stack(t, lane):
    zero = jnp.zeros_like(t)
    return jnp.concatenate([jnp.where(lane < HEAD_DIM, t, zero), jnp.where(lane >= HEAD_DIM, t, zero)], axis=0)


def _sb_block_consts():
    jj = lax.broadcasted_iota(jnp.int32, (BK, 2 * BK), 0)
    ss = lax.broadcasted_iota(jnp.int32, (BK, 2 * BK), 1)
    suffix = jnp.where((ss >= BK) | (jj > ss), 1.0, 0.0).astype(BF16)
    prefix_incl = jnp.where((ss >= BK) | (jj <= ss), 1.0, 0.0).astype(BF16)
    prefix_excl = jnp.where((ss >= BK) | (jj < ss), 1.0, 0.0).astype(BF16)
    return suffix, prefix_incl, prefix_excl


def _sb_rel():
    row = lax.broadcasted_iota(jnp.int32, (2 * TQ, BK), 0)
    row = jnp.where(row >= TQ, row - TQ, row)
    col = lax.broadcasted_iota(jnp.int32, (2 * TQ, BK), 1)
    return col - row


def _split_dot(x, m01):
    hi = x.astype(BF16)
    lo = (x - hi.astype(F32)).astype(BF16)
    return _dot(hi, m01) + _dot(lo, m01)


def _sb_scores(qs, k, mask):
    z = _dot_nt(qs, k) * SCALE
    lb = jnp.minimum(z, 0.0) - jnp.log1p(jnp.exp(-jnp.abs(z)))
    lm = lb - z
    return lb, lm if mask is None else jnp.where(mask, lm, 0.0)


SB_STRAIGHT = 3
SB_WIDE = SB_STRAIGHT * BK
SWA_QB = 4


def _sb_wide_consts():
    j = np.arange(SB_WIDE)[:, None]
    s = np.arange(SB_WIDE)[None, :]
    ones = np.ones((SB_WIDE, BK), np.float32)
    as_bf16 = lambda m: jnp.asarray(m.astype(np.float32), dtype=BF16)
    return (as_bf16(np.concatenate([j > s, ones], axis=1)), as_bf16(np.concatenate([j <= s, ones], axis=1)),
            as_bf16(j < s))


def _sb_fwd(qkv, name):
    s = qkv.shape[0]
    nq = s // TQ
    sufw = _sb_wide_consts()[0]

    def body(q_ref, k_ref, v_ref, sufw_ref, o_ref, lt_ref, nb_ref, cf_ref, acc_ref):
        i = pl.program_id(1)
        lane = lax.broadcasted_iota(jnp.int32, (TQ, LANES), 1)
        qs = _pair_stack(q_ref[...], lane)
        rel = _sb_rel()
        q0 = i * TQ

        straight = i >= SB_STRAIGHT - 1

        @pl.when(straight)
        def _():
            w0 = pl.multiple_of((i - (SB_STRAIGHT - 1)) * BK, BK)
            kw = k_ref[pl.ds(w0, SB_WIDE), :]
            lb, lm = _sb_scores(qs, kw, None)
            own = rel < 0
            past = SB_WIDE - BK
            lm = jnp.concatenate([lm[:, :past], jnp.where(own, lm[:, past:], 0.0)], axis=1)
            cs = _split_dot(lm, sufw_ref[...])
            a = jnp.exp(lb + cs[:, :SB_WIDE])
            a = jnp.concatenate([a[:, :past], jnp.where(own, a[:, past:], 0.0)], axis=1)
            acc_ref[...] = _dot(a.astype(BF16), v_ref[pl.ds(w0, SB_WIDE), :])
            cf_ref[...] = cs[:, SB_WIDE:]

        @pl.when(jnp.logical_not(straight))
        def _():
            cf_ref[...] = jnp.zeros_like(cf_ref)
            acc_ref[...] = jnp.zeros_like(acc_ref)

        def more(c):
            return (c[0] <= i) & (c[1] > SB_EXHAUSTED)

        def step(c):
            k0 = pl.multiple_of((i - c[0]) * BK, BK)
            k = k_ref[pl.ds(k0, BK), :]
            v = v_ref[pl.ds(k0, BK), :]
            mask = rel < (q0 - k0)
            lb, lm = _sb_scores(qs, k, mask)
            cs = _split_dot(lm, _sb_block_consts()[0])
            a = jnp.where(mask, jnp.exp(lb + cs[:, :BK] + cf_ref[...]), 0.0)
            acc_ref[...] += _dot(a.astype(BF16), v)
            cf = cf_ref[...] + cs[:, BK:]
            cf_ref[...] = cf
            return c[0] + 1, jnp.max(cf)

        n_blocks, _ = lax.while_loop(
            more, step, (jnp.where(straight, SB_STRAIGHT, 0).astype(jnp.int32), jnp.max(cf_ref[...])))
        o_ref[...] = jnp.where(lane < HEAD_DIM, acc_ref[:TQ, :], acc_ref[TQ:, :]).astype(BF16)
        lt_ref[...] = cf_ref[...]
        nb_ref[...] = jnp.full(nb_ref.shape, n_blocks, F32)

    npair = SB_W // LANES
    return pl.pallas_call(
        body,
        out_shape=[jax.ShapeDtypeStruct((s, SB_W), BF16), jax.ShapeDtypeStruct((npair, nq, 2 * TQ, BK), F32),
                   jax.ShapeDtypeStruct((npair, nq, 8, LANES), F32)],
        grid=(npair, nq),
        in_specs=[pl.BlockSpec((TQ, LANES), lambda j, i: (i, j)),
                  pl.BlockSpec((s, LANES), lambda j, i: (0, npair + j)),
                  pl.BlockSpec((s, LANES), lambda j, i: (0, 2 * npair + j)),
                  pl.BlockSpec(sufw.shape, lambda j, i: (0, 0))],
        out_specs=[pl.BlockSpec((TQ, LANES), lambda j, i: (i, j)),
                   pl.BlockSpec((None, None, 2 * TQ, BK), lambda j, i: (j, i, 0, 0)),
                   pl.BlockSpec((None, None, 8, LANES), lambda j, i: (j, i, 0, 0))],
        scratch_shapes=[pltpu.VMEM((2 * TQ, BK), F32), pltpu.VMEM((2 * TQ, LANES), F32)],
        compiler_params=_params(2), name=name)(qkv, qkv, qkv, sufw)


def _sb_bwd(qkv, lt, nb, doa, name):
    s = qkv.shape[0]
    nq = s // TQ
    _, prew, prexw = _sb_wide_consts()

    def body(q_ref, k_ref, v_ref, lt_ref, nb_ref, do_ref, prew_ref, prexw_ref,
             dq_ref, dk_ref, dv_ref, cp_ref, ce_ref, dqa_ref):
        i = pl.program_id(1)
        lane = lax.broadcasted_iota(jnp.int32, (TQ, LANES), 1)
        qs = _pair_stack(q_ref[...], lane)
        dos = _pair_stack(do_ref[...], lane)
        rel = _sb_rel()
        q0 = i * TQ
        n_blocks = jnp.clip(jnp.max(nb_ref[...]).astype(jnp.int32), 1, i + 1)
        first = i + 1 - n_blocks

        @pl.when(i == 0)
        def _():
            dk_ref[...] = jnp.zeros_like(dk_ref)
            dv_ref[...] = jnp.zeros_like(dv_ref)

        straight = n_blocks == SB_STRAIGHT

        @pl.when(straight)
        def _():
            w0 = pl.multiple_of(first * BK, BK)
            kw = k_ref[pl.ds(w0, SB_WIDE), :]
            vw = v_ref[pl.ds(w0, SB_WIDE), :]
            lb, lm = _sb_scores(qs, kw, None)
            own = rel < 0
            past = SB_WIDE - BK
            on_past_keys = lambda t: jnp.concatenate([t[:, :past], jnp.where(own, t[:, past:], 0.0)], axis=1)
            lm = on_past_keys(lm)
            cs = _split_dot(lm, prew_ref[...])
            lt = lt_ref[...]
            a = on_past_keys(jnp.exp(lb + (jnp.concatenate([lt] * SB_STRAIGHT, axis=1) - cs[:, :SB_WIDE])))
            e = a * _dot_nt(dos, vw)
            big_e = _split_dot(e, prexw_ref[...])
            dz = (on_past_keys(e - jnp.exp(lb) * (e + big_e)) * SCALE).astype(BF16)
            dk_ref[pl.ds(w0, SB_WIDE), :] += _dot_tn(dz, qs)
            dv_ref[pl.ds(w0, SB_WIDE), :] += _dot_tn(a.astype(BF16), dos)
            dqa_ref[...] = _dot(dz, kw)

        @pl.when(jnp.logical_not(straight))
        def _():
            cp_ref[...] = jnp.zeros_like(cp_ref)
            ce_ref[...] = jnp.zeros_like(ce_ref)
            dqa_ref[...] = jnp.zeros_like(dqa_ref)

            def step(it, carry):
                k0 = pl.multiple_of((first + it) * BK, BK)
                k = k_ref[pl.ds(k0, BK), :]
                v = v_ref[pl.ds(k0, BK), :]
                mask = rel < (q0 - k0)
                _, prefix_incl, prefix_excl = _sb_block_consts()
                lb, lm = _sb_scores(qs, k, mask)
                cs = _split_dot(lm, prefix_incl)
                a = jnp.where(mask, jnp.exp(lb + (lt_ref[...] - (cs[:, :BK] + cp_ref[...]))), 0.0)
                e = a * _dot_nt(dos, v)
                ce = _split_dot(e, prefix_excl)
                big_e = ce[:, :BK] + ce_ref[...]
                dz = (jnp.where(mask, e - jnp.exp(lb) * (e + big_e), 0.0) * SCALE).astype(BF16)
                dk_ref[pl.ds(k0, BK), :] += _dot_tn(dz, qs)
                dv_ref[pl.ds(k0, BK), :] += _dot_tn(a.astype(BF16), dos)
                dqa_ref[...] += _dot(dz, k)
                cp_ref[...] += cs[:, BK:]
                ce_ref[...] += ce[:, BK:]
                return carry

            lax.fori_loop(0, n_blocks, step, 0)

        dq_ref[...] = jnp.where(lane < HEAD_DIM, dqa_ref[:TQ, :], dqa_ref[TQ:, :]).astype(BF16)

    npair = SB_W // LANES
    return pl.pallas_call(
        body,
        out_shape=[jax.ShapeDtypeStruct((s, SB_W), BF16), jax.ShapeDtypeStruct((s, SB_W), F32),
                   jax.ShapeDtypeStruct((s, SB_W), F32)],
        grid=(npair, nq),
        in_specs=[pl.BlockSpec((TQ, LANES), lambda j, i: (i, j)),
                  pl.BlockSpec((s, LANES), lambda j, i: (0, npair + j)),
                  pl.BlockSpec((s, LANES), lambda j, i: (0, 2 * npair + j)),
                  pl.BlockSpec((None, None, 2 * TQ, BK), lambda j, i: (j, i, 0, 0)),
                  pl.BlockSpec((None, None, 8, LANES), lambda j, i: (j, i, 0, 0)),
                  pl.BlockSpec((TQ, LANES), lambda j, i: (i, j)),
                  pl.BlockSpec(prew.shape, lambda j, i: (0, 0)),
                  pl.BlockSpec(prexw.shape, lambda j, i: (0, 0))],
        out_specs=[pl.BlockSpec((TQ, LANES), lambda j, i: (i, j)),
                   pl.BlockSpec((s, LANES), lambda j, i: (0, j)),
                   pl.BlockSpec((s, LANES), lambda j, i: (0, j))],
        scratch_shapes=[pltpu.VMEM((2 * TQ, BK), F32), pltpu.VMEM((2 * TQ, BK), F32),
                        pltpu.VMEM((2 * TQ, LANES), F32)],
        compiler_params=_params(2), name=name)(qkv, qkv, qkv, lt, nb, doa, prew, prexw)


def _bucket_table():
    i = np.arange(TQ)[:, None]
    j = np.arange(2 * BK)[None, :]
    dist = np.maximum(TQ + i - j, 0)
    max_exact = N_BUCKETS // 2
    df = np.maximum(dist, 1).astype(np.float32)
    large = max_exact + (np.log(df / np.float32(max_exact)) / np.float32(math.log(MAX_DISTANCE / max_exact))
                         * np.float32(N_BUCKETS - max_exact)).astype(np.int32)
    large = np.minimum(large, N_BUCKETS - 1)
    return np.where(dist < max_exact, dist, large).astype(np.int32)


def _swa_align_in(t, lane, g):
    tf = t.astype(F32)
    tr = pltpu.roll(tf, HEAD_DIM, 1)
    gmask = (lane >= HEAD_DIM) == (g == 1)
    top = jnp.where(gmask, jnp.where(g == 0, tf, tr), 0.0)
    bot = jnp.where(gmask, jnp.where(g == 1, tf, tr), 0.0)
    return jnp.concatenate([top, bot], axis=0).astype(BF16)


def _swa_align_out(t, lane, g):
    top, bot = t[:TQ, :], t[TQ:, :]
    top = jnp.where(g == 0, top, pltpu.roll(top, HEAD_DIM, 1))
    bot = jnp.where(g == 1, bot, pltpu.roll(bot, HEAD_DIM, 1))
    return jnp.where(lane < HEAD_DIM, top, bot)


def _swa_bias(bias_ref, bucket_ref, rb_ref, j):
    for hh in range(2):
        def add(b, acc):
            return acc + jnp.where(bucket_ref[...] == b, rb_ref[b, 2 * j + hh], 0.0)
        bias_ref[hh * TQ:(hh + 1) * TQ, :] = lax.fori_loop(0, N_BUCKETS, add, jnp.zeros((TQ, 2 * BK), F32))


def _swa_probs(qs, k2, bias, sink_ref, i, j):
    s = _dot_nt(qs, k2) * SCALE + bias
    row = lax.broadcasted_iota(jnp.int32, (2 * TQ, 2 * BK), 0)
    col = lax.broadcasted_iota(jnp.int32, (2 * TQ, 2 * BK), 1)
    dist = TQ + jnp.where(row >= TQ, row - TQ, row) - col
    valid = (dist >= 0) & (dist < WINDOW) & ((col >= BK) | (i > 0))
    s = jnp.where(valid, s, NEG)
    row1 = lax.broadcasted_iota(jnp.int32, (2 * TQ, 1), 0)
    sink = jnp.where(row1 < TQ, sink_ref[2 * j], sink_ref[2 * j + 1])
    m = jnp.maximum(jnp.max(s, axis=1, keepdims=True), sink)
    e = jnp.exp(s - m)
    es = jnp.exp(sink - m)
    den = jnp.sum(e, axis=1, keepdims=True) + es
    return e / den, es / den


def _swa_kv(ref, i):
    prev = pl.multiple_of(jnp.maximum(i - 1, 0) * BK, BK)
    cur = pl.multiple_of(i * BK, BK)
    return jnp.concatenate([ref[pl.ds(prev, BK), :], ref[pl.ds(cur, BK), :]], axis=0), prev, cur


def _swa_fwd(qkv, sinks, rel_bias, name):
    s = qkv.shape[0]
    nq = s // TQ
    npair = SW_QW // LANES
    qcol = 3 * SB_W // LANES
    bucket = jnp.asarray(_bucket_table())

    def body(q_ref, k_ref, v_ref, bucket_ref, sink_ref, rb_ref, o_ref, bias_ref):
        j = pl.program_id(0)
        step = pl.program_id(1)
        g = j // 2
        lane = lax.broadcasted_iota(jnp.int32, (TQ, LANES), 1)

        @pl.when(step == 0)
        def _():
            _swa_bias(bias_ref, bucket_ref, rb_ref, j)

        for b in range(SWA_QB):
            i = step * SWA_QB + b
            rows = slice(b * TQ, (b + 1) * TQ)
            qs = _swa_align_in(q_ref[rows, :], lane, g)
            k2, _, _ = _swa_kv(k_ref, i)
            v2, _, _ = _swa_kv(v_ref, i)
            pr, _ = _swa_probs(qs, k2, bias_ref[...], sink_ref, i, j)
            o_ref[rows, :] = _swa_align_out(_dot(pr.astype(BF16), v2), lane, g).astype(BF16)

    assert nq % SWA_QB == 0
    return pl.pallas_call(
        body, out_shape=jax.ShapeDtypeStruct((s, SW_QW), BF16), grid=(npair, nq // SWA_QB),
        in_specs=[pl.BlockSpec((SWA_QB * TQ, LANES), lambda j, i: (i, qcol + j)),
                  pl.BlockSpec((s, LANES), lambda j, i: (0, qcol + npair)),
                  pl.BlockSpec((s, LANES), lambda j, i: (0, qcol + npair + 1)),
                  pl.BlockSpec((TQ, 2 * BK), lambda j, i: (0, 0)),
                  pl.BlockSpec(memory_space=pltpu.SMEM),
                  pl.BlockSpec(memory_space=pltpu.SMEM)],
        out_specs=pl.BlockSpec((SWA_QB * TQ, LANES), lambda j, i: (i, j)),
        scratch_shapes=[pltpu.VMEM((2 * TQ, 2 * BK), F32)],
        compiler_params=_params(2), name=name)(qkv, qkv, qkv, bucket, sinks, rel_bias)


def _swa_bwd(qkv, ob, dob, sinks, rel_bias, name):
    s = qkv.shape[0]
    nq = s // TQ
    npair = SW_QW // LANES
    qcol = 3 * SB_W // LANES
    bucket = jnp.asarray(_bucket_table())

    def body(q_ref, k_ref, v_ref, o_ref, do_ref, bucket_ref, sink_ref, rb_ref,
             dq_ref, dk_ref, dv_ref, dsink_ref, drel_ref, bias_ref, dsacc_ref):
        j = pl.program_id(0)
        step = pl.program_id(1)
        g = j // 2
        lane = lax.broadcasted_iota(jnp.int32, (TQ, LANES), 1)
        row8 = lax.broadcasted_iota(jnp.int32, (SW_HEADS, LANES), 0)
        lane8 = lax.broadcasted_iota(jnp.int32, (SW_HEADS, LANES), 1)

        @pl.when((step == 0) & (j == 0))
        def _():
            dk_ref[...] = jnp.zeros_like(dk_ref)
            dv_ref[...] = jnp.zeros_like(dv_ref)
            dsink_ref[...] = jnp.zeros_like(dsink_ref)
            drel_ref[...] = jnp.zeros_like(drel_ref)

        @pl.when(step == 0)
        def _():
            _swa_bias(bias_ref, bucket_ref, rb_ref, j)
            dsacc_ref[...] = jnp.zeros_like(dsacc_ref)

        ds_sum = jnp.zeros(dsacc_ref.shape, F32)
        dsink = jnp.zeros((SW_HEADS, LANES), F32)
        for b in range(SWA_QB):
            i = step * SWA_QB + b
            rows = slice(b * TQ, (b + 1) * TQ)
            qs = _swa_align_in(q_ref[rows, :], lane, g)
            do = do_ref[rows, :]
            dos = _swa_align_in(do, lane, g)
            dof = do.astype(F32) * o_ref[rows, :].astype(F32)
            d0 = jnp.sum(jnp.where(lane < HEAD_DIM, dof, 0.0), axis=1, keepdims=True)
            d1 = jnp.sum(jnp.where(lane >= HEAD_DIM, dof, 0.0), axis=1, keepdims=True)
            delta = jnp.concatenate([d0, d1], axis=0)
            k2, prev, cur = _swa_kv(k_ref, i)
            v2, _, _ = _swa_kv(v_ref, i)
            pr, psink = _swa_probs(qs, k2, bias_ref[...], sink_ref, i, j)
            ds = pr * (_dot_nt(dos, v2) - delta)
            ds_sum = ds_sum + ds
            sd = psink * delta
            ds0 = -jnp.sum(sd[:TQ, :], axis=0, keepdims=True)
            ds1 = -jnp.sum(sd[TQ:, :], axis=0, keepdims=True)
            dsink = dsink + jnp.where(row8 == 2 * j, ds0, jnp.where(row8 == 2 * j + 1, ds1, 0.0))
            dsb = ds.astype(BF16)
            dq_ref[rows, :] = _swa_align_out(_dot(dsb, k2) * SCALE, lane, g).astype(BF16)
            dk2 = _dot_tn(dsb, qs) * SCALE
            dv2 = _dot_tn(pr.astype(BF16), dos)
            dk_ref[pl.ds(prev, BK), :] += dk2[:BK, :]
            dk_ref[pl.ds(cur, BK), :] += dk2[BK:, :]
            dv_ref[pl.ds(prev, BK), :] += dv2[:BK, :]
            dv_ref[pl.ds(cur, BK), :] += dv2[BK:, :]
        dsacc_ref[...] += ds_sum
        dsink_ref[...] += dsink

        @pl.when(step == nq // SWA_QB - 1)
        def _():
            for hh in range(2):
                def red(b, acc):
                    val = _sum_all(jnp.where(bucket_ref[...] == b, dsacc_ref[hh * TQ:(hh + 1) * TQ, :], 0.0))
                    return jnp.where((row8 == 2 * j + hh) & (lane8 == b), val, acc)
                drel_ref[...] += lax.fori_loop(0, N_BUCKETS, red, jnp.zeros((SW_HEADS, LANES), F32))

    whole = lambda j, i: (0, 0)
    return pl.pallas_call(
        body,
        out_shape=[jax.ShapeDtypeStruct((s, SW_QW), BF16), jax.ShapeDtypeStruct((s, LANES), F32),
                   jax.ShapeDtypeStruct((s, LANES), F32), jax.ShapeDtypeStruct((SW_HEADS, LANES), F32),
                   jax.ShapeDtypeStruct((SW_HEADS, LANES), F32)],
        grid=(npair, nq // SWA_QB),
        in_specs=[pl.BlockSpec((SWA_QB * TQ, LANES), lambda j, i: (i, qcol + j)),
                  pl.BlockSpec((s, LANES), lambda j, i: (0, qcol + npair)),
                  pl.BlockSpec((s, LANES), lambda j, i: (0, qcol + npair + 1)),
                  pl.BlockSpec((SWA_QB * TQ, LANES), lambda j, i: (i, j)),
                  pl.BlockSpec((SWA_QB * TQ, LANES), lambda j, i: (i, j)),
                  pl.BlockSpec((TQ, 2 * BK), whole),
                  pl.BlockSpec(memory_space=pltpu.SMEM),
                  pl.BlockSpec(memory_space=pltpu.SMEM)],
        out_specs=[pl.BlockSpec((SWA_QB * TQ, LANES), lambda j, i: (i, j)),
                   pl.BlockSpec((s, LANES), whole), pl.BlockSpec((s, LANES), whole),
                   pl.BlockSpec((SW_HEADS, LANES), whole), pl.BlockSpec((SW_HEADS, LANES), whole)],
        scratch_shapes=[pltpu.VMEM((2 * TQ, 2 * BK), F32), pltpu.VMEM((2 * TQ, 2 * BK), F32)],
        compiler_params=_params(2), name=name)(qkv, qkv, qkv, ob, dob, bucket, sinks, rel_bias)


def _acc_init(i, *refs):
    @pl.when(i == 0)
    def _():
        for r in refs:
            r[...] = jnp.zeros_like(r)


def _loss_bwd(x3, target, g, name):
    d = x3.shape[1]

    def body(x_ref, t_ref, g_ref, dx_ref, dg_ref, loss_ref):
        _acc_init(pl.program_id(0), dg_ref, loss_ref)
        x = x_ref[...]
        gv = g_ref[...]
        diff = _rms(x, gv) - t_ref[...]
        loss_ref[...] += 0.5 * jnp.sum(jnp.mean(jnp.square(diff), axis=-1, keepdims=True), axis=0, keepdims=True)
        dx, dg = _rms_bwd(diff * (1.0 / d), x, gv)
        dx_ref[...] = dx
        dg_ref[...] += dg

    return _rowcall(name, body, [x3, target], [g], [(d, F32)], [((1, d), F32), ((1, LANES), F32)])


def _ple_bwd(dx3, pe, gt, x2, g, wpg, name):
    d = x2.shape[1]

    def body(dx3_ref, pe_ref, gt_ref, x_ref, g_ref, w_ref, dpe_ref, dgt_ref, dx2_ref, dg_ref):
        _acc_init(pl.program_id(0), dg_ref)
        dx3 = dx3_ref[...]
        sg = _sigmoid(gt_ref[...])
        dpe_ref[...] = (dx3 * sg).astype(BF16)
        dgt = (dx3 * pe_ref[...] * sg * (1.0 - sg)).astype(BF16)
        dgt_ref[...] = dgt
        dx, dg = _rms_bwd(_dot_nt(dgt, w_ref[...]), x_ref[...], g_ref[...])
        dx2_ref[...] = dx3 + dx
        dg_ref[...] += dg

    return _rowcall(name, body, [dx3, pe, gt, x2], [g, wpg], [(d, BF16), (d, BF16), (d, F32)], [((1, d), F32)])


def _ff2_bwd(dx2, u, w2, name):
    d = dx2.shape[1]
    dff = u.shape[1]

    def body(dx_ref, u_ref, w_ref, du_ref, dxb_ref):
        dxb = dx_ref[...].astype(BF16)
        dxb_ref[...] = dxb
        du_ref[...] = (_dot_nt(dxb, w_ref[...]) * (2.0 * jnp.maximum(u_ref[...], 0.0))).astype(BF16)

    return _rowcall(name, body, [dx2, u], [w2], [(dff, BF16), (d, BF16)])


def _ff1_bwd(du, dx2, x1, g, w1, name):
    d = x1.shape[1]

    def body(du_ref, dx2_ref, x_ref, g_ref, w_ref, dx1_ref, dx1b_ref, dg_ref):
        _acc_init(pl.program_id(0), dg_ref)
        dx, dg = _rms_bwd(_dot_nt(du_ref[...], w_ref[...]), x_ref[...], g_ref[...])
        dx1 = dx2_ref[...] + dx
        dx1_ref[...] = dx1
        dx1b_ref[...] = dx1.astype(BF16)
        dg_ref[...] += dg

    return _rowcall(name, body, [du, dx2, x1], [g, w1], [(d, F32), (d, BF16)], [((1, d), F32)])


def _mixer_bwd(dx1b, gates, oa, ob, wo, wua, wub, name):
    d = dx1b.shape[1]

    def body(dx_ref, gate_ref, oa_ref, ob_ref, wo_ref, wua_ref, wub_ref,
             dya_ref, dyb_ref, dgate_ref, doa_ref, dob_ref):
        dm = _dot_nt(dx_ref[...], wo_ref[...])
        sa = _sigmoid(gate_ref[:, :d])
        sb = _sigmoid(gate_ref[:, d:])
        ya = _dot(oa_ref[...], wua_ref[...])
        yb = _dot(ob_ref[...], wub_ref[...])
        dya = (dm * sa).astype(BF16)
        dyb = (dm * sb).astype(BF16)
        dya_ref[...] = dya
        dyb_ref[...] = dyb
        dgate_ref[:, :d] = (dm * ya * sa * (1.0 - sa)).astype(BF16)
        dgate_ref[:, d:] = (dm * yb * sb * (1.0 - sb)).astype(BF16)
        doa_ref[...] = _dot_nt(dya, wua_ref[...]).astype(BF16)
        dob_ref[...] = _dot_nt(dyb, wub_ref[...]).astype(BF16)

    return _rowcall(name, body, [dx1b, gates, oa, ob], [wo, wua, wub],
                    [(d, BF16), (d, BF16), (2 * d, BF16), (SB_W, BF16), (SW_QW, BF16)])


def _inproj_bwd(dqkv, dgates, dx1, x, g, w, name):
    d = x.shape[1]

    def body(dqkv_ref, dgate_ref, dx1_ref, x_ref, g_ref, w_ref, dx_ref, dg_ref):
        _acc_init(pl.program_id(0), dg_ref)
        dh = _dot_nt(dqkv_ref[...], w_ref[:, :QKV_W]) + _dot_nt(dgate_ref[...], w_ref[:, QKV_W:])
        dx, dg = _rms_bwd(dh, x_ref[...], g_ref[...])
        dx_ref[...] = dx1_ref[...] + dx
        dg_ref[...] += dg

    return _rowcall(name, body, [dqkv, dgates, dx1, x], [g, w], [(d, F32)], [((1, d), F32)])


def _tile(n, cap):
    assert n % LANES == 0
    return max(t for t in range(LANES, min(n, cap) + 1, LANES) if n % t == 0)


def _mm_tn(a, b, name, nshard=1):
    s, ka = a.shape
    nb = b.shape[1]
    n = nb // nshard
    ta = _tile(ka, 512)
    tb = _tile(n, 512)
    per = n // tb

    def body(a_ref, b_ref, o_ref):
        o_ref[...] = _dot_tn(a_ref[...].astype(BF16), b_ref[...].astype(BF16))

    return pl.pallas_call(
        body, out_shape=jax.ShapeDtypeStruct((nshard, ka, n), F32), grid=(nb // tb, ka // ta),
        in_specs=[pl.BlockSpec((s, ta), lambda jb, ia: (0, ia)), pl.BlockSpec((s, tb), lambda jb, ia: (0, jb))],
        out_specs=pl.BlockSpec((None, ta, tb), lambda jb, ia: (jb // per, ia, jb % per)),
        compiler_params=_params(2), name=name)(a, b)


def _place():
    return lax.axis_index("x"), lax.axis_index("y"), lax.axis_index("c")


def _chip_peer(x, y, k):
    return (x ^ (k >> 1), y ^ (k & 1))


def _cast_bf16(w, r, name):
    l, k, n = w.shape
    tk = min(k, 256)

    def body(r_ref, w_ref, o_ref):
        o_ref[...] = w_ref[...].astype(BF16)

    return pl.pallas_call(
        body, out_shape=jax.ShapeDtypeStruct((l, 4, k, n), BF16),
        grid_spec=pltpu.PrefetchScalarGridSpec(
            num_scalar_prefetch=1, grid=(l, k // tk),
            in_specs=[pl.BlockSpec((None, tk, n), lambda a, i, r_ref: (a, i, 0))],
            out_specs=pl.BlockSpec((None, None, tk, n), lambda a, i, r_ref: (a, r_ref[0], i, 0))),
        compiler_params=_params(2), name=name)(r, w)


def _all_gather_weights(bufs):
    nt = len(bufs)

    def body(*refs):
        ins, outs = refs[:nt], refs[nt:2 * nt]
        send_sems, recv_sems = refs[2 * nt:]
        x, y, c = _place()
        r = 2 * x + y
        sibling = (x, y, 1 - c)
        first = []
        for t in range(nt):
            src, dst = ins[t], outs[t]
            for k in (1, 2, 3):
                cp = pltpu.make_async_remote_copy(
                    src_ref=src.at[:, r, c], dst_ref=dst.at[:, r, c], send_sem=send_sems.at[t, k - 1],
                    recv_sem=recv_sems.at[t, k - 1], device_id=(*_chip_peer(x, y, k), c), device_id_type=MESH)
                cp.start()
                first.append(cp)
        fwd = []
        for t in range(nt):
            dst = outs[t]
            for k in (1, 2, 3):
                q = r ^ k
                slab = dst.at[:, q, c]
                pltpu.make_async_remote_copy(
                    src_ref=slab, dst_ref=slab, send_sem=send_sems.at[t, k - 1], recv_sem=recv_sems.at[t, k - 1],
                    device_id=sibling, device_id_type=MESH).wait_recv()
                cp = pltpu.make_async_remote_copy(
                    src_ref=slab, dst_ref=slab, send_sem=send_sems.at[t, 2 + k], recv_sem=recv_sems.at[t, 2 + k],
                    device_id=sibling, device_id_type=MESH)
                cp.start()
                fwd.append(cp)
        for t in range(nt):
            dst = outs[t]
            for k in (1, 2, 3):
                other = dst.at[:, r ^ k, 1 - c]
                pltpu.make_async_remote_copy(
                    src_ref=other, dst_ref=other, send_sem=send_sems.at[t, 2 + k], recv_sem=recv_sems.at[t, 2 + k],
                    device_id=sibling, device_id_type=MESH).wait_recv()
        for cp in first + fwd:
            cp.wait_send()

    any_spec = pl.BlockSpec(memory_space=pl.ANY)
    return pl.pallas_call(
        body, out_shape=[jax.ShapeDtypeStruct(b.shape, BF16) for b in bufs],
        in_specs=[any_spec] * nt, out_specs=[any_spec] * nt,
        input_output_aliases={t: t for t in range(nt)},
        scratch_shapes=[pltpu.SemaphoreType.DMA((nt, 6)), pltpu.SemaphoreType.DMA((nt, 6))],
        name="all_gather_weights")(*bufs)


def _rs_to_sibling(grads):
    nt = len(grads)

    def body(*refs):
        ins, outs = refs[:nt], refs[nt:2 * nt]
        send_sems, recv_sems = refs[2 * nt:]
        x, y, c = _place()
        sibling = (x, y, 1 - c)
        cps = []
        for t in range(nt):
            cp = pltpu.make_async_remote_copy(
                src_ref=ins[t].at[:, 1 - c], dst_ref=outs[t], send_sem=send_sems.at[t], recv_sem=recv_sems.at[t],
                device_id=sibling, device_id_type=MESH)
            cp.start()
            cps.append(cp)
        for cp in cps:
            cp.wait()

    any_spec = pl.BlockSpec(memory_space=pl.ANY)
    return pl.pallas_call(
        body, out_shape=[jax.ShapeDtypeStruct((4,) + g.shape[2:], F32) for g in grads],
        in_specs=[any_spec] * nt, out_specs=[any_spec] * nt,
        scratch_shapes=[pltpu.SemaphoreType.DMA((nt,)), pltpu.SemaphoreType.DMA((nt,))],
        name="rs_to_sibling")(*grads)


def _add_half(g, recv, cr, name):
    _, _, k2, n = g.shape
    tk = min(k2, 256)

    def body(cr_ref, g_ref, r_ref, sums_ref, mine_ref):
        val = (g_ref[...] + r_ref[...]).astype(BF16)
        sums_ref[...] = val

        @pl.when(pl.program_id(1) == cr_ref[1])
        def _():
            mine_ref[...] = val

    return pl.pallas_call(
        body, out_shape=[jax.ShapeDtypeStruct((4, k2, n), BF16)] * 2,
        grid_spec=pltpu.PrefetchScalarGridSpec(
            num_scalar_prefetch=1, grid=(k2 // tk, 4),
            in_specs=[pl.BlockSpec((None, None, tk, n), lambda i, q, cr_ref: (q, cr_ref[0], i, 0)),
                      pl.BlockSpec((None, tk, n), lambda i, q, cr_ref: (q, i, 0))],
            out_specs=[pl.BlockSpec((None, tk, n), lambda i, q, cr_ref: (q, i, 0)),
                       pl.BlockSpec((None, tk, n), lambda i, q, cr_ref: (cr_ref[1], i, 0))]),
        compiler_params=_params(2), name=name)(cr, g, recv)


def _rs_to_chips(sums, parts):
    nt = len(sums)

    def body(*refs):
        ins, outs = refs[:nt], refs[2 * nt:3 * nt]
        send_sems, recv_sems = refs[3 * nt:]
        x, y, c = _place()
        r = 2 * x + y
        cps = []
        for t in range(nt):
            for k in (1, 2, 3):
                cp = pltpu.make_async_remote_copy(
                    src_ref=ins[t].at[r ^ k], dst_ref=outs[t].at[r], send_sem=send_sems.at[t, k - 1],
                    recv_sem=recv_sems.at[t, k - 1], device_id=(*_chip_peer(x, y, k), c), device_id_type=MESH)
                cp.start()
                cps.append(cp)
        for cp in cps:
            cp.wait()

    any_spec = pl.BlockSpec(memory_space=pl.ANY)
    return pl.pallas_call(
        body, out_shape=[jax.ShapeDtypeStruct(p.shape, p.dtype) for p in parts],
        in_specs=[any_spec] * (2 * nt), out_specs=[any_spec] * nt,
        input_output_aliases={nt + t: t for t in range(nt)},
        scratch_shapes=[pltpu.SemaphoreType.DMA((nt, 3)), pltpu.SemaphoreType.DMA((nt, 3))],
        name="rs_to_chips")(*sums, *parts)


def _sum4(parts, cr, name):
    _, k2, n = parts.shape
    tk = min(k2, 256)

    def body(cr_ref, p_ref, o_ref):
        p = p_ref[...].astype(F32)
        o_ref[...] = ((p[0] + p[1]) + p[2]) + p[3]

    return pl.pallas_call(
        body, out_shape=jax.ShapeDtypeStruct((2, k2, n), F32),
        grid_spec=pltpu.PrefetchScalarGridSpec(
            num_scalar_prefetch=1, grid=(k2 // tk,),
            in_specs=[pl.BlockSpec((4, tk, n), lambda i, cr_ref: (0, i, 0))],
            out_specs=pl.BlockSpec((None, tk, n), lambda i, cr_ref: (cr_ref[0], i, 0))),
        compiler_params=_params(1), name=name)(cr, parts)


def _exchange_halves(both):
    nt = len(both)

    def body(*refs):
        ins, outs = refs[:nt], refs[nt:2 * nt]
        send_sems, recv_sems = refs[2 * nt:]
        x, y, c = _place()
        cps = []
        for t in range(nt):
            cp = pltpu.make_async_remote_copy(
                src_ref=ins[t].at[c], dst_ref=outs[t].at[c], send_sem=send_sems.at[t], recv_sem=recv_sems.at[t],
                device_id=(x, y, 1 - c), device_id_type=MESH)
            cp.start()
            cps.append(cp)
        for cp in cps:
            cp.wait()

    any_spec = pl.BlockSpec(memory_space=pl.ANY)
    return pl.pallas_call(
        body, out_shape=[jax.ShapeDtypeStruct(b.shape, F32) for b in both],
        in_specs=[any_spec] * nt, out_specs=[any_spec] * nt,
        input_output_aliases={t: t for t in range(nt)},
        scratch_shapes=[pltpu.SemaphoreType.DMA((nt,)), pltpu.SemaphoreType.DMA((nt,))],
        name="exchange_halves")(*both)


def _adamw_math(w, g, m, v):
    m = ADAM_B1 * m + (1.0 - ADAM_B1) * g
    v = ADAM_B2 * v + (1.0 - ADAM_B2) * jnp.square(g)
    m_hat = m / (1.0 - ADAM_B1 ** ADAM_STEP)
    v_hat = v / (1.0 - ADAM_B2 ** ADAM_STEP)
    delta = -ADAM_LR * (m_hat / (jnp.sqrt(v_hat) + ADAM_EPS) + ADAM_WD * w)
    return delta, m, v


def _adamw(w, m, v, g0, g1, name):
    _, k, n = w.shape
    tk = min(k, 256)
    nk = k // tk

    def body(w_ref, m_ref, v_ref, g0_ref, g1_ref, grad_ref, delta_ref, nm_ref, nv_ref):
        g = jnp.where(pl.program_id(0) == 0, g0_ref[...], g1_ref[...])
        delta, nm, nv = _adamw_math(w_ref[...], g, m_ref[...], v_ref[...])
        grad_ref[...] = g
        delta_ref[...] = delta
        nm_ref[...] = nm
        nv_ref[...] = nv

    lay = pl.BlockSpec((None, tk, n), lambda a, i: (a, i, 0))
    g0_spec = pl.BlockSpec((tk, n), lambda a, i: (jnp.where(a == 0, i, nk - 1), 0))
    g1_spec = pl.BlockSpec((tk, n), lambda a, i: (jnp.where(a == 1, i, 0), 0))
    return pl.pallas_call(
        body, out_shape=[jax.ShapeDtypeStruct(w.shape, F32)] * 4, grid=(2, nk),
        in_specs=[lay, lay, lay, g0_spec, g1_spec], out_specs=[lay] * 4,
        compiler_params=_params(2), name=name)(w, m, v, g0, g1)


def _small_allreduce_adamw(gpart, w, m, v):
    shape = gpart.shape

    def body(g_ref, w_ref, m_ref, v_ref, gsum_ref, delta_ref, nm_ref, nv_ref, recv_ref, send_sems, recv_sems):
        x, y, c = _place()
        me = 4 * x + 2 * y + c
        recv_ref[me] = g_ref[...]
        cps = []
        for k in range(1, 8):
            peer = (x ^ (k >> 2), y ^ ((k >> 1) & 1), c ^ (k & 1))
            cp = pltpu.make_async_remote_copy(
                src_ref=g_ref, dst_ref=recv_ref.at[me], send_sem=send_sems.at[k - 1], recv_sem=recv_sems.at[k - 1],
                device_id=peer, device_id_type=MESH)
            cp.start()
            cps.append(cp)
        for cp in cps:
            cp.wait()
        g = recv_ref[0]
        for dev in range(1, 8):
            g = g + recv_ref[dev]
        delta, nm, nv = _adamw_math(w_ref[...], g, m_ref[...], v_ref[...])
        gsum_ref[...] = g
        delta_ref[...] = delta
        nm_ref[...] = nm
        nv_ref[...] = nv

    vm = pl.BlockSpec(memory_space=pltpu.VMEM)
    return pl.pallas_call(
        body, out_shape=[jax.ShapeDtypeStruct(shape, F32)] * 4, in_specs=[vm] * 4, out_specs=[vm] * 4,
        scratch_shapes=[pltpu.VMEM((8,) + shape, F32), pltpu.SemaphoreType.DMA((7,)), pltpu.SemaphoreType.DMA((7,))],
        name="small_allreduce_adamw")(gpart, w, m, v)


BIG = ("w_in", "w_up_a", "w_up_b", "w_o", "w_ff1", "w_ff2", "w_pe", "w_pg")
COL_SHARDED = ("w_in", "w_up_a", "w_up_b", "w_ff1", "w_pe")
ROW_SHARDED = ("w_o", "w_ff2", "w_pg")
SMALL_ROWS = 16


def _pack_small(g_mix, g_mlp, g_pe, g_final, sinks, rel_bias, loss=None):
    d = g_final.shape[0]
    row = lambda v: jnp.pad(v.reshape(1, -1), ((0, 0), (0, d - v.size)))
    rows = [g_mix, g_mlp, g_pe, g_final.reshape(1, d),
            jnp.zeros((1, d), F32) if loss is None else row(loss), row(sinks), row(rel_bias)]
    out = jnp.concatenate(rows, axis=0)
    return jnp.pad(out, ((0, SMALL_ROWS - out.shape[0]), (0, 0)))


def _unpack_small(a, sinks_shape, rel_shape):
    return (a[0:2], a[2:4], a[4:6], a[6], a[8, :sinks_shape[0] * sinks_shape[1]].reshape(sinks_shape),
            a[9, :rel_shape[0] * rel_shape[1]].reshape(rel_shape))


def kernel(x, p, w_in, w_up_a, w_up_b, w_o, w_ff1, w_ff2, w_pe, w_pg, g_mix, g_mlp, g_pe, g_final, sinks, rel_bias, loss_target, m_w_in, m_w_up_a, m_w_up_b, m_w_o, m_w_ff1, m_w_ff2, m_w_pe, m_w_pg, m_g_mix, m_g_mlp, m_g_pe, m_g_final, m_sinks, m_rel_bias, v_w_in, v_w_up_a, v_w_up_b, v_w_o, v_w_ff1, v_w_ff2, v_w_pe, v_w_pg, v_g_mix, v_g_mlp, v_g_pe, v_g_final, v_sinks, v_rel_bias):
    depth = w_in.shape[0]
    assert depth == 2
    x0 = x[0]
    target = loss_target[0]
    d = x0.shape[1]
    wl = dict(w_in=w_in, w_up_a=w_up_a, w_up_b=w_up_b, w_o=w_o, w_ff1=w_ff1, w_ff2=w_ff2, w_pe=w_pe, w_pg=w_pg)
    ml = dict(w_in=m_w_in, w_up_a=m_w_up_a, w_up_b=m_w_up_b, w_o=m_w_o, w_ff1=m_w_ff1, w_ff2=m_w_ff2, w_pe=m_w_pe, w_pg=m_w_pg)
    vl = dict(w_in=v_w_in, w_up_a=v_w_up_a, w_up_b=v_w_up_b, w_o=v_w_o, w_ff1=v_w_ff1, w_ff2=v_w_ff2, w_pe=v_w_pe, w_pg=v_w_pg)
    c_idx = lax.axis_index("c").astype(jnp.int32)
    r_idx = (2 * lax.axis_index("x") + lax.axis_index("y")).astype(jnp.int32)
    cr = jnp.stack([c_idx, r_idx])

    bufs = []
    for n in BIG:
        w = wl[n]
        l, k, nn = w.shape
        bufs.append(_cast_bf16(w, r_idx.reshape(1), "cast_" + n).reshape(l, 4, 2, k // 2, nn))
    gathered = _all_gather_weights(bufs)
    full = {}
    for n, gth in zip(BIG, gathered):
        l, _, _, k2, nn = gth.shape
        gth = gth.reshape(l, 4, 2 * k2, nn)
        if n in COL_SHARDED:
            full[n] = gth.transpose(0, 2, 1, 3).reshape(l, 2 * k2, 4 * nn)
        else:
            full[n] = gth.reshape(l, 8 * k2, nn)

    saved = []
    xi = x0
    for i in range(depth):
        st = dict(x0=xi)
        gm = g_mix[i].reshape(1, d)
        st["h1"], st["qkv"], st["gates"] = _inproj_fwd(xi, gm, full["w_in"][i], f"inproj_fwd_{i}")
        st["oa"], st["lt"], st["nb"] = _sb_fwd(st["qkv"], f"sb_fwd_{i}")
        st["ob"] = _swa_fwd(st["qkv"], sinks[i], rel_bias, f"swa_fwd_{i}")
        st["m"], st["x1"] = _mixer_fwd(st["oa"], st["ob"], st["gates"], xi, full["w_up_a"][i], full["w_up_b"][i],
                                       full["w_o"][i], f"mixer_fwd_{i}")
        st["h2"], st["u"], st["a"] = _ff1_fwd(st["x1"], g_mlp[i].reshape(1, d), full["w_ff1"][i], f"ff1_fwd_{i}")
        st["x2"] = _ff2_fwd(st["a"], st["x1"], full["w_ff2"][i], f"ff2_fwd_{i}")
        st["pb"], st["h3"], st["pe"], st["gt"], xi = _ple_fwd(p[i, 0], st["x2"], g_pe[i].reshape(1, d),
                                                            full["w_pe"][i], full["w_pg"][i], f"ple_fwd_{i}")
        saved.append(st)

    dx, dg_final, loss_part = _loss_bwd(xi, target, g_final.reshape(1, d), "loss_bwd")
    gw = {n: [None] * depth for n in BIG}
    dg_mix, dg_mlp, dg_pe, dsinks = [None] * depth, [None] * depth, [None] * depth, [None] * depth
    drel = jnp.zeros((SW_HEADS, LANES), F32)
    for i in reversed(range(depth)):
        st = saved[i]
        dpe, dgt, dx2, dg_pe[i] = _ple_bwd(dx, st["pe"], st["gt"], st["x2"], g_pe[i].reshape(1, d),
                                           full["w_pg"][i], f"ple_bwd_{i}")
        gw["w_pe"][i] = _mm_tn(st["pb"], dpe, f"dw_pe_{i}", 4)
        gw["w_pg"][i] = _mm_tn(st["h3"], dgt, f"dw_pg_{i}")
        du, dx2b = _ff2_bwd(dx2, st["u"], full["w_ff2"][i], f"ff2_bwd_{i}")
        gw["w_ff2"][i] = _mm_tn(st["a"], dx2b, f"dw_ff2_{i}")
        gw["w_ff1"][i] = _mm_tn(st["h2"], du, f"dw_ff1_{i}", 4)
        dx1, dx1b, dg_mlp[i] = _ff1_bwd(du, dx2, st["x1"], g_mlp[i].reshape(1, d), full["w_ff1"][i], f"ff1_bwd_{i}")
        gw["w_o"][i] = _mm_tn(st["m"], dx1b, f"dw_o_{i}")
        dya, dyb, dgates, doa, dob = _mixer_bwd(dx1b, st["gates"], st["oa"], st["ob"], full["w_o"][i],
                                                full["w_up_a"][i], full["w_up_b"][i], f"mixer_bwd_{i}")
        gw["w_up_a"][i] = _mm_tn(st["oa"], dya, f"dw_up_a_{i}", 4)
        gw["w_up_b"][i] = _mm_tn(st["ob"], dyb, f"dw_up_b_{i}", 4)
        dqa, dka, dva = _sb_bwd(st["qkv"], st["lt"], st["nb"], doa, f"sb_bwd_{i}")
        dqb, dkb, dvb, dsk, drl = _swa_bwd(st["qkv"], st["ob"], dob, sinks[i], rel_bias, f"swa_bwd_{i}")
        dsinks[i] = dsk[:, 0]
        drel = drel + drl
        dqkv = jnp.concatenate([dqa, dka.astype(BF16), dva.astype(BF16), dqb, dkb.astype(BF16), dvb.astype(BF16)],
                               axis=1)
        dw_in = jnp.concatenate([_mm_tn(st["h1"], dqkv, f"dw_in_qkv_{i}")[0],
                                 _mm_tn(st["h1"], dgates, f"dw_in_gate_{i}")[0]], axis=1)
        gw["w_in"][i] = dw_in.reshape(d, 4, dw_in.shape[1] // 4).transpose(1, 0, 2)
        dx, dg_mix[i] = _inproj_bwd(dqkv, dgates, dx1, st["x0"], g_mix[i].reshape(1, d), full["w_in"][i],
                                    f"inproj_bwd_{i}")
    grad_x = dx[None]

    tensors = []
    for n in BIG:
        for i in range(depth):
            g = gw[n][i]
            if n in ROW_SHARDED:
                ka, nb = g.shape[1:]
                g = g.reshape(4, ka // 4, nb)
            _, k, nn = g.shape
            tensors.append(g.reshape(4, 2, k // 2, nn))
    from_sibling = _rs_to_sibling(tensors)
    added = [_add_half(g, r, cr, f"add_half_{t}") for t, (g, r) in enumerate(zip(tensors, from_sibling))]
    from_chips = _rs_to_chips([a[0] for a in added], [a[1] for a in added])
    both = _exchange_halves([_sum4(pc, cr, f"sum4_{t}") for t, pc in enumerate(from_chips)])
    outs = {}
    for ti, n in enumerate(BIG):
        g0 = both[2 * ti].reshape(wl[n].shape[1:])
        g1 = both[2 * ti + 1].reshape(wl[n].shape[1:])
        outs[n] = _adamw(wl[n], ml[n], vl[n], g0, g1, "adamw_" + n)

    drel_bias = drel[:, :N_BUCKETS].T
    gsmall = _pack_small(jnp.concatenate(dg_mix, 0), jnp.concatenate(dg_mlp, 0), jnp.concatenate(dg_pe, 0),
                         dg_final[0], jnp.stack(dsinks), drel_bias, loss_part[0, :1])
    wsmall = _pack_small(g_mix, g_mlp, g_pe, g_final, sinks, rel_bias)
    msmall = _pack_small(m_g_mix, m_g_mlp, m_g_pe, m_g_final, m_sinks, m_rel_bias)
    vsmall = _pack_small(v_g_mix, v_g_mlp, v_g_pe, v_g_final, v_sinks, v_rel_bias)
    small = _small_allreduce_adamw(gsmall, wsmall, msmall, vsmall)
    loss = small[0][7, 0]
    small = [_unpack_small(a, sinks.shape, rel_bias.shape) for a in small]

    result = [loss, grad_x]
    for kind in range(4):
        result += [outs[n][kind] for n in BIG]
        result += list(small[kind])
    return tuple(result)
```

```python
import functools
import math
from typing import Callable, NamedTuple

import numpy as np
import jax
import jax.numpy as jnp
from jax import lax
from jax.experimental import pallas as pl
from jax.experimental.pallas import tpu as pltpu

F32 = jnp.float32
BF16 = jnp.bfloat16
MESH = pl.DeviceIdType.MESH

HEAD_DIM = 64
SB_HEADS = 8
SW_HEADS = 8
SW_KV_HEADS = 2
WINDOW = 128
N_BUCKETS = 32
MAX_DISTANCE = 128
EPS = 1e-6
SB_W = SB_HEADS * HEAD_DIM
SW_QW = SW_HEADS * HEAD_DIM
SW_KVW = SW_KV_HEADS * HEAD_DIM
QKV_W = 3 * SB_W + SW_QW + 2 * SW_KVW
SCALE = HEAD_DIM ** -0.5
LANES = 128
TQ = 128
BK = 128
NEG = -1e30
SB_EXHAUSTED = -106.0

ADAM_LR = 0.001
ADAM_B1 = 0.9
ADAM_B2 = 0.999
ADAM_EPS = 1e-08
ADAM_WD = 0.01
ADAM_STEP = 10

VMEM_LIMIT = 56 * 1024 * 1024


def _dot(a, b):
    return jnp.dot(a, b, preferred_element_type=F32)


def _dot_nt(a, b):
    return lax.dot_general(a, b, (((1,), (1,)), ((), ())), preferred_element_type=F32)


def _dot_tn(a, b):
    return lax.dot_general(a, b, (((0,), (0,)), ((), ())), preferred_element_type=F32)


def _sum_all(x):
    return jnp.sum(jnp.sum(x, axis=1, keepdims=True), axis=0, keepdims=True)


def _sigmoid(x):
    return 1.0 / (1.0 + jnp.exp(-x))


def _rms(x, g):
    r = lax.rsqrt(jnp.mean(x * x, axis=-1, keepdims=True) + EPS)
    return (x * r) * g


def _rms_bwd(dy, x, g):
    r = lax.rsqrt(jnp.mean(x * x, axis=-1, keepdims=True) + EPS)
    n = x * r
    dg = jnp.sum(dy * n, axis=0, keepdims=True)
    dn = dy * g
    dx = r * (dn - n * jnp.mean(dn * n, axis=-1, keepdims=True))
    return dx, dg


def _params(n_axes):
    return pltpu.CompilerParams(dimension_semantics=("arbitrary",) * n_axes, vmem_limit_bytes=VMEM_LIMIT)


def _rowcall(name, body, row_ins, const_ins, row_outs, acc_outs=(), tm=512):
    s = row_ins[0].shape[0]
    assert s % tm == 0
    in_specs = [pl.BlockSpec((tm, a.shape[1]), lambda i: (i, 0)) for a in row_ins]
    in_specs += [pl.BlockSpec(a.shape, functools.partial(lambda i, nd: (0,) * nd, nd=a.ndim)) for a in const_ins]
    out_shape = [jax.ShapeDtypeStruct((s, c), dt) for c, dt in row_outs]
    out_specs = [pl.BlockSpec((tm, c), lambda i: (i, 0)) for c, _ in row_outs]
    out_shape += [jax.ShapeDtypeStruct(sh, dt) for sh, dt in acc_outs]
    out_specs += [pl.BlockSpec(sh, functools.partial(lambda i, nd: (0,) * nd, nd=len(sh))) for sh, _ in acc_outs]
    return pl.pallas_call(body, out_shape=out_shape, grid=(s // tm,), in_specs=in_specs, out_specs=out_specs,
                          compiler_params=_params(1), name=name)(*row_ins, *const_ins)


def _inproj_fwd(x, g, w, name):
    d = x.shape[1]

    def body(x_ref, g_ref, w_ref, h_ref, qkv_ref, gate_ref):
        hb = _rms(x_ref[...], g_ref[...]).astype(BF16)
        h_ref[...] = hb
        qkv_ref[...] = _dot(hb, w_ref[:, :QKV_W]).astype(BF16)
        gate_ref[...] = _dot(hb, w_ref[:, QKV_W:])

    return _rowcall(name, body, [x], [g, w], [(d, BF16), (QKV_W, BF16), (2 * d, F32)])


def _mixer_fwd(oa, ob, gates, x, wua, wub, wo, name):
    d = x.shape[1]

    def body(oa_ref, ob_ref, gate_ref, x_ref, wua_ref, wub_ref, wo_ref, m_ref, x1_ref):
        ya = _dot(oa_ref[...], wua_ref[...])
        yb = _dot(ob_ref[...], wub_ref[...])
        m = _sigmoid(gate_ref[:, :d]) * ya + _sigmoid(gate_ref[:, d:]) * yb
        mb = m.astype(BF16)
        m_ref[...] = mb
        x1_ref[...] = x_ref[...] + _dot(mb, wo_ref[...])

    return _rowcall(name, body, [oa, ob, gates, x], [wua, wub, wo], [(d, BF16), (d, F32)])


def _ff1_fwd(x1, g, w1, name):
    d, dff = w1.shape

    def body(x_ref, g_ref, w_ref, h_ref, u_ref, a_ref):
        hb = _rms(x_ref[...], g_ref[...]).astype(BF16)
        h_ref[...] = hb
        u = _dot(hb, w_ref[...])
        u_ref[...] = u
        a_ref[...] = jnp.square(jnp.maximum(u, 0.0)).astype(BF16)

    return _rowcall(name, body, [x1], [g, w1], [(d, BF16), (dff, F32), (dff, BF16)])


def _ff2_fwd(a, x1, w2, name):
    d = x1.shape[1]

    def body(a_ref, x_ref, w_ref, o_ref):
        o_ref[...] = x_ref[...] + _dot(a_ref[...], w_ref[...])

    return _rowcall(name, body, [a, x1], [w2], [(d, F32)])[0]


def _ple_fwd(p, x2, g, wpe, wpg, name):
    d = x2.shape[1]

    def body(p_ref, x_ref, g_ref, wpe_ref, wpg_ref, pb_ref, h_ref, pe_ref, gt_ref, x3_ref):
        pb = p_ref[...].astype(BF16)
        pb_ref[...] = pb
        pe = _dot(pb, wpe_ref[...])
        x = x_ref[...]
        hb = _rms(x, g_ref[...]).astype(BF16)
        h_ref[...] = hb
        gt = _dot(hb, wpg_ref[...])
        pe_ref[...] = pe
        gt_ref[...] = gt
        x3_ref[...] = x + pe * _sigmoid(gt)

    return _rowcall(name, body, [p, x2], [g, wpe, wpg],
                    [(p.shape[1], BF16), (d, BF16), (d, F32), (d, F32), (d, F32)])


def _pair_stack(t, lane):
    zero = jnp.zeros_like(t)
    return jnp.concatenate([jnp.where(lane < HEAD_DIM, t, zero), jnp.where(lane >= HEAD_DIM, t, zero)], axis=0)


def _sb_block_consts():
    jj = lax.broadcasted_iota(jnp.int32, (BK, 2 * BK), 0)
    ss = lax.broadcasted_iota(jnp.int32, (BK, 2 * BK), 1)
    suffix = jnp.where((ss >= BK) | (jj > ss), 1.0, 0.0).astype(BF16)
    prefix_incl = jnp.where((ss >= BK) | (jj <= ss), 1.0, 0.0).astype(BF16)
    prefix_excl = jnp.where((ss >= BK) | (jj < ss), 1.0, 0.0).astype(BF16)
    return suffix, prefix_incl, prefix_excl


def _sb_rel():
    row = lax.broadcasted_iota(jnp.int32, (2 * TQ, BK), 0)
    row = jnp.where(row >= TQ, row - TQ, row)
    col = lax.broadcasted_iota(jnp.int32, (2 * TQ, BK), 1)
    return col - row


def _split_dot(x, m01):
    hi = x.astype(BF16)
    lo = (x - hi.astype(F32)).astype(BF16)
    return _dot(hi, m01) + _dot(lo, m01)


def _sb_scores(qs, k, mask):
    z = _dot_nt(qs, k) * SCALE
    lb = jnp.minimum(z, 0.0) - jnp.log1p(jnp.exp(-jnp.abs(z)))
    lm = lb - z
    return lb, lm if mask is None else jnp.where(mask, lm, 0.0)


SB_STRAIGHT = 3
SB_WIDE = SB_STRAIGHT * BK
SWA_QB = 4


def _sb_wide_consts():
    j = np.arange(SB_WIDE)[:, None]
    s = np.arange(SB_WIDE)[None, :]
    ones = np.ones((SB_WIDE, BK), np.float32)
    as_bf16 = lambda m: jnp.asarray(m.astype(np.float32), dtype=BF16)
    return (as_bf16(np.concatenate([j > s, ones], axis=1)), as_bf16(np.concatenate([j <= s, ones], axis=1)),
            as_bf16(j < s))


def _side_refs(ex, rest, n_out):
    n_in = len(ex.arrays) if ex else 0
    n_alias = len(ex.aliased) if ex else 0
    ins, rest = rest[:n_in], rest[n_in:]
    outs, rest = rest[:n_out], rest[n_out:]
    return ins, outs, rest[:n_alias], rest[n_alias:]


def _side_specs(ex, n_in, n_out):
    if ex is None:
        return [], [], [], {}, []
    any_spec = pl.BlockSpec(memory_space=pl.ANY)
    return ([any_spec] * len(ex.arrays),
            [jax.ShapeDtypeStruct(ex.arrays[a].shape, ex.arrays[a].dtype) for a in ex.aliased],
            [any_spec] * len(ex.aliased), {n_in + a: n_out + o for o, a in enumerate(ex.aliased)},
            [pltpu.SemaphoreType.DMA(ex.sems), pltpu.SemaphoreType.DMA(ex.sems)])


def _sb_fwd(qkv, name, side=None):
    s = qkv.shape[0]
    nq = s // TQ
    npair = SB_W // LANES
    sufw = _sb_wide_consts()[0]

    def body(q_ref, k_ref, v_ref, sufw_ref, *rest):
        side_in, (o_ref, lt_ref, nb_ref), side_out, scratch = _side_refs(side, rest, 3)
        cf_ref, acc_ref = scratch[:2]
        i = pl.program_id(1)
        if side is not None:
            @pl.when((pl.program_id(0) == 0) & (i == 0))
            def _():
                side.start(side_in, side_out, *scratch[2:])
        lane = lax.broadcasted_iota(jnp.int32, (TQ, LANES), 1)
        qs = _pair_stack(q_ref[...], lane)
        rel = _sb_rel()
        q0 = i * TQ

        straight = i >= SB_STRAIGHT - 1

        @pl.when(straight)
        def _():
            w0 = pl.multiple_of((i - (SB_STRAIGHT - 1)) * BK, BK)
            kw = k_ref[pl.ds(w0, SB_WIDE), :]
            lb, lm = _sb_scores(qs, kw, None)
            own = rel < 0
            past = SB_WIDE - BK
            lm = jnp.concatenate([lm[:, :past], jnp.where(own, lm[:, past:], 0.0)], axis=1)
            cs = _split_dot(lm, sufw_ref[...])
            a = jnp.exp(lb + cs[:, :SB_WIDE])
            a = jnp.concatenate([a[:, :past], jnp.where(own, a[:, past:], 0.0)], axis=1)
            acc_ref[...] = _dot(a.astype(BF16), v_ref[pl.ds(w0, SB_WIDE), :])
            cf_ref[...] = cs[:, SB_WIDE:]

        @pl.when(jnp.logical_not(straight))
        def _():
            cf_ref[...] = jnp.zeros_like(cf_ref)
            acc_ref[...] = jnp.zeros_like(acc_ref)

        def more(c):
            return (c[0] <= i) & (c[1] > SB_EXHAUSTED)

        def step(c):
            k0 = pl.multiple_of((i - c[0]) * BK, BK)
            k = k_ref[pl.ds(k0, BK), :]
            v = v_ref[pl.ds(k0, BK), :]
            mask = rel < (q0 - k0)
            lb, lm = _sb_scores(qs, k, mask)
            cs = _split_dot(lm, _sb_block_consts()[0])
            a = jnp.where(mask, jnp.exp(lb + cs[:, :BK] + cf_ref[...]), 0.0)
            acc_ref[...] += _dot(a.astype(BF16), v)
            cf = cf_ref[...] + cs[:, BK:]
            cf_ref[...] = cf
            return c[0] + 1, jnp.max(cf)

        n_blocks, _ = lax.while_loop(
            more, step, (jnp.where(straight, SB_STRAIGHT, 0).astype(jnp.int32), jnp.max(cf_ref[...])))
        o_ref[...] = jnp.where(lane < HEAD_DIM, acc_ref[:TQ, :], acc_ref[TQ:, :]).astype(BF16)
        lt_ref[...] = cf_ref[...]
        nb_ref[...] = jnp.full(nb_ref.shape, n_blocks, F32)
        if side is not None:
            @pl.when((pl.program_id(0) == npair - 1) & (i == nq - 1))
            def _():
                side.finish(side_in, side_out, *scratch[2:])

    s_in, s_shape, s_out, s_alias, s_sems = _side_specs(side, 4, 3)
    outs = pl.pallas_call(
        body,
        out_shape=[jax.ShapeDtypeStruct((s, SB_W), BF16), jax.ShapeDtypeStruct((npair, nq, 2 * TQ, BK), F32),
                   jax.ShapeDtypeStruct((npair, nq, 8, LANES), F32)] + s_shape,
        grid=(npair, nq),
        in_specs=[pl.BlockSpec((TQ, LANES), lambda j, i: (i, j)),
                  pl.BlockSpec((s, LANES), lambda j, i: (0, npair + j)),
                  pl.BlockSpec((s, LANES), lambda j, i: (0, 2 * npair + j)),
                  pl.BlockSpec(sufw.shape, lambda j, i: (0, 0))] + s_in,
        out_specs=[pl.BlockSpec((TQ, LANES), lambda j, i: (i, j)),
                   pl.BlockSpec((None, None, 2 * TQ, BK), lambda j, i: (j, i, 0, 0)),
                   pl.BlockSpec((None, None, 8, LANES), lambda j, i: (j, i, 0, 0))] + s_out,
        input_output_aliases=s_alias,
        scratch_shapes=[pltpu.VMEM((2 * TQ, BK), F32), pltpu.VMEM((2 * TQ, LANES), F32)] + s_sems,
        compiler_params=_params(2), name=name)(qkv, qkv, qkv, sufw, *(side.arrays if side else ()))
    return outs[0], outs[1], outs[2], outs[3:]


def _sb_bwd(qkv, lt, nb, doa, name, side=None):
    s = qkv.shape[0]
    nq = s // TQ
    npair = SB_W // LANES
    _, prew, prexw = _sb_wide_consts()

    def body(q_ref, k_ref, v_ref, lt_ref, nb_ref, do_ref, prew_ref, prexw_ref, *rest):
        side_in, (dq_ref, dk_ref, dv_ref), side_out, scratch = _side_refs(side, rest, 3)
        cp_ref, ce_ref, dqa_ref = scratch[:3]
        i = pl.program_id(1)
        if side is not None:
            @pl.when((pl.program_id(0) == 0) & (i == 0))
            def _():
                side.start(side_in, side_out, *scratch[3:])
        lane = lax.broadcasted_iota(jnp.int32, (TQ, LANES), 1)
        qs = _pair_stack(q_ref[...], lane)
        dos = _pair_stack(do_ref[...], lane)
        rel = _sb_rel()
        q0 = i * TQ
        n_blocks = jnp.clip(jnp.max(nb_ref[...]).astype(jnp.int32), 1, i + 1)
        first = i + 1 - n_blocks

        @pl.when(i == 0)
        def _():
            dk_ref[...] = jnp.zeros_like(dk_ref)
            dv_ref[...] = jnp.zeros_like(dv_ref)

        straight = n_blocks == SB_STRAIGHT

        @pl.when(straight)
        def _():
            w0 = pl.multiple_of(first * BK, BK)
            kw = k_ref[pl.ds(w0, SB_WIDE), :]
            vw = v_ref[pl.ds(w0, SB_WIDE), :]
            lb, lm = _sb_scores(qs, kw, None)
            own = rel < 0
            past = SB_WIDE - BK
            on_past_keys = lambda t: jnp.concatenate([t[:, :past], jnp.where(own, t[:, past:], 0.0)], axis=1)
            lm = on_past_keys(lm)
            cs = _split_dot(lm, prew_ref[...])
            lt = lt_ref[...]
            a = on_past_keys(jnp.exp(lb + (jnp.concatenate([lt] * SB_STRAIGHT, axis=1) - cs[:, :SB_WIDE])))
            e = a * _dot_nt(dos, vw)
            big_e = _split_dot(e, prexw_ref[...])
            dz = (on_past_keys(e - jnp.exp(lb) * (e + big_e)) * SCALE).astype(BF16)
            dk_ref[pl.ds(w0, SB_WIDE), :] += _dot_tn(dz, qs)
            dv_ref[pl.ds(w0, SB_WIDE), :] += _dot_tn(a.astype(BF16), dos)
            dqa_ref[...] = _dot(dz, kw)

        @pl.when(jnp.logical_not(straight))
        def _():
            cp_ref[...] = jnp.zeros_like(cp_ref)
            ce_ref[...] = jnp.zeros_like(ce_ref)
            dqa_ref[...] = jnp.zeros_like(dqa_ref)

            def step(it, carry):
                k0 = pl.multiple_of((first + it) * BK, BK)
                k = k_ref[pl.ds(k0, BK), :]
                v = v_ref[pl.ds(k0, BK), :]
                mask = rel < (q0 - k0)
                _, prefix_incl, prefix_excl = _sb_block_consts()
                lb, lm = _sb_scores(qs, k, mask)
                cs = _split_dot(lm, prefix_incl)
                a = jnp.where(mask, jnp.exp(lb + (lt_ref[...] - (cs[:, :BK] + cp_ref[...]))), 0.0)
                e = a * _dot_nt(dos, v)
                ce = _split_dot(e, prefix_excl)
                big_e = ce[:, :BK] + ce_ref[...]
                dz = (jnp.where(mask, e - jnp.exp(lb) * (e + big_e), 0.0) * SCALE).astype(BF16)
                dk_ref[pl.ds(k0, BK), :] += _dot_tn(dz, qs)
                dv_ref[pl.ds(k0, BK), :] += _dot_tn(a.astype(BF16), dos)
                dqa_ref[...] += _dot(dz, k)
                cp_ref[...] += cs[:, BK:]
                ce_ref[...] += ce[:, BK:]
                return carry

            lax.fori_loop(0, n_blocks, step, 0)

        dq_ref[...] = jnp.where(lane < HEAD_DIM, dqa_ref[:TQ, :], dqa_ref[TQ:, :]).astype(BF16)
        if side is not None:
            @pl.when((pl.program_id(0) == npair - 1) & (i == nq - 1))
            def _():
                side.finish(side_in, side_out, *scratch[3:])

    s_in, s_shape, s_out, s_alias, s_sems = _side_specs(side, 8, 3)
    outs = pl.pallas_call(
        body,
        out_shape=[jax.ShapeDtypeStruct((s, SB_W), BF16), jax.ShapeDtypeStruct((s, SB_W), F32),
                   jax.ShapeDtypeStruct((s, SB_W), F32)] + s_shape,
        grid=(npair, nq),
        in_specs=[pl.BlockSpec((TQ, LANES), lambda j, i: (i, j)),
                  pl.BlockSpec((s, LANES), lambda j, i: (0, npair + j)),
                  pl.BlockSpec((s, LANES), lambda j, i: (0, 2 * npair + j)),
                  pl.BlockSpec((None, None, 2 * TQ, BK), lambda j, i: (j, i, 0, 0)),
                  pl.BlockSpec((None, None, 8, LANES), lambda j, i: (j, i, 0, 0)),
                  pl.BlockSpec((TQ, LANES), lambda j, i: (i, j)),
                  pl.BlockSpec(prew.shape, lambda j, i: (0, 0)),
                  pl.BlockSpec(prexw.shape, lambda j, i: (0, 0))] + s_in,
        out_specs=[pl.BlockSpec((TQ, LANES), lambda j, i: (i, j)),
                   pl.BlockSpec((s, LANES), lambda j, i: (0, j)),
                   pl.BlockSpec((s, LANES), lambda j, i: (0, j))] + s_out,
        input_output_aliases=s_alias,
        scratch_shapes=[pltpu.VMEM((2 * TQ, BK), F32), pltpu.VMEM((2 * TQ, BK), F32),
                        pltpu.VMEM((2 * TQ, LANES), F32)] + s_sems,
        compiler_params=_params(2), name=name)(qkv, qkv, qkv, lt, nb, doa, prew, prexw,
                                               *(side.arrays if side else ()))
    return outs[0], outs[1], outs[2], outs[3:]


def _bucket_table():
    i = np.arange(TQ)[:, None]
    j = np.arange(2 * BK)[None, :]
    dist = np.maximum(TQ + i - j, 0)
    max_exact = N_BUCKETS // 2
    df = np.maximum(dist, 1).astype(np.float32)
    large = max_exact + (np.log(df / np.float32(max_exact)) / np.float32(math.log(MAX_DISTANCE / max_exact))
                         * np.float32(N_BUCKETS - max_exact)).astype(np.int32)
    large = np.minimum(large, N_BUCKETS - 1)
    return np.where(dist < max_exact, dist, large).astype(np.int32)


def _swa_align_in(t, lane, g):
    tf = t.astype(F32)
    tr = pltpu.roll(tf, HEAD_DIM, 1)
    gmask = (lane >= HEAD_DIM) == (g == 1)
    top = jnp.where(gmask, jnp.where(g == 0, tf, tr), 0.0)
    bot = jnp.where(gmask, jnp.where(g == 1, tf, tr), 0.0)
    return jnp.concatenate([top, bot], axis=0).astype(BF16)


def _swa_align_out(t, lane, g):
    top, bot = t[:TQ, :], t[TQ:, :]
    top = jnp.where(g == 0, top, pltpu.roll(top, HEAD_DIM, 1))
    bot = jnp.where(g == 1, bot, pltpu.roll(bot, HEAD_DIM, 1))
    return jnp.where(lane < HEAD_DIM, top, bot)


def _swa_bias(bias_ref, bucket_ref, rb_ref, j):
    for hh in range(2):
        def add(b, acc):
            return acc + jnp.where(bucket_ref[...] == b, rb_ref[b, 2 * j + hh], 0.0)
        bias_ref[hh * TQ:(hh + 1) * TQ, :] = lax.fori_loop(0, N_BUCKETS, add, jnp.zeros((TQ, 2 * BK), F32))


def _swa_probs(qs, k2, bias, sink_ref, i, j):
    s = _dot_nt(qs, k2) * SCALE + bias
    row = lax.broadcasted_iota(jnp.int32, (2 * TQ, 2 * BK), 0)
    col = lax.broadcasted_iota(jnp.int32, (2 * TQ, 2 * BK), 1)
    dist = TQ + jnp.where(row >= TQ, row - TQ, row) - col
    valid = (dist >= 0) & (dist < WINDOW) & ((col >= BK) | (i > 0))
    s = jnp.where(valid, s, NEG)
    row1 = lax.broadcasted_iota(jnp.int32, (2 * TQ, 1), 0)
    sink = jnp.where(row1 < TQ, sink_ref[2 * j], sink_ref[2 * j + 1])
    m = jnp.maximum(jnp.max(s, axis=1, keepdims=True), sink)
    e = jnp.exp(s - m)
    es = jnp.exp(sink - m)
    den = jnp.sum(e, axis=1, keepdims=True) + es
    return e / den, es / den


def _swa_kv(ref, i):
    prev = pl.multiple_of(jnp.maximum(i - 1, 0) * BK, BK)
    cur = pl.multiple_of(i * BK, BK)
    return jnp.concatenate([ref[pl.ds(prev, BK), :], ref[pl.ds(cur, BK), :]], axis=0), prev, cur


def _swa_fwd(qkv, sinks, rel_bias, name):
    s = qkv.shape[0]
    nq = s // TQ
    npair = SW_QW // LANES
    qcol = 3 * SB_W // LANES
    bucket = jnp.asarray(_bucket_table())

    def body(q_ref, k_ref, v_ref, bucket_ref, sink_ref, rb_ref, o_ref, bias_ref):
        j = pl.program_id(0)
        step = pl.program_id(1)
        g = j // 2
        lane = lax.broadcasted_iota(jnp.int32, (TQ, LANES), 1)

        @pl.when(step == 0)
        def _():
            _swa_bias(bias_ref, bucket_ref, rb_ref, j)

        for b in range(SWA_QB):
            i = step * SWA_QB + b
            rows = slice(b * TQ, (b + 1) * TQ)
            qs = _swa_align_in(q_ref[rows, :], lane, g)
            k2, _, _ = _swa_kv(k_ref, i)
            v2, _, _ = _swa_kv(v_ref, i)
            pr, _ = _swa_probs(qs, k2, bias_ref[...], sink_ref, i, j)
            o_ref[rows, :] = _swa_align_out(_dot(pr.astype(BF16), v2), lane, g).astype(BF16)

    assert nq % SWA_QB == 0
    return pl.pallas_call(
        body, out_shape=jax.ShapeDtypeStruct((s, SW_QW), BF16), grid=(npair, nq // SWA_QB),
        in_specs=[pl.BlockSpec((SWA_QB * TQ, LANES), lambda j, i: (i, qcol + j)),
                  pl.BlockSpec((s, LANES), lambda j, i: (0, qcol + npair)),
                  pl.BlockSpec((s, LANES), lambda j, i: (0, qcol + npair + 1)),
                  pl.BlockSpec((TQ, 2 * BK), lambda j, i: (0, 0)),
                  pl.BlockSpec(memory_space=pltpu.SMEM),
                  pl.BlockSpec(memory_space=pltpu.SMEM)],
        out_specs=pl.BlockSpec((SWA_QB * TQ, LANES), lambda j, i: (i, j)),
        scratch_shapes=[pltpu.VMEM((2 * TQ, 2 * BK), F32)],
        compiler_params=_params(2), name=name)(qkv, qkv, qkv, bucket, sinks, rel_bias)


def _swa_bwd(qkv, ob, dob, sinks, rel_bias, name):
    s = qkv.shape[0]
    nq = s // TQ
    npair = SW_QW // LANES
    qcol = 3 * SB_W // LANES
    bucket = jnp.asarray(_bucket_table())

    def body(q_ref, k_ref, v_ref, o_ref, do_ref, bucket_ref, sink_ref, rb_ref,
             dq_ref, dk_ref, dv_ref, dsink_ref, drel_ref, bias_ref, dsacc_ref):
        j = pl.program_id(0)
        step = pl.program_id(1)
        g = j // 2
        lane = lax.broadcasted_iota(jnp.int32, (TQ, LANES), 1)
        row8 = lax.broadcasted_iota(jnp.int32, (SW_HEADS, LANES), 0)
        lane8 = lax.broadcasted_iota(jnp.int32, (SW_HEADS, LANES), 1)

        @pl.when((step == 0) & (j == 0))
        def _():
            dk_ref[...] = jnp.zeros_like(dk_ref)
            dv_ref[...] = jnp.zeros_like(dv_ref)
            dsink_ref[...] = jnp.zeros_like(dsink_ref)
            drel_ref[...] = jnp.zeros_like(drel_ref)

        @pl.when(step == 0)
        def _():
            _swa_bias(bias_ref, bucket_ref, rb_ref, j)
            dsacc_ref[...] = jnp.zeros_like(dsacc_ref)

        ds_sum = jnp.zeros(dsacc_ref.shape, F32)
        dsink = jnp.zeros((SW_HEADS, LANES), F32)
        for b in range(SWA_QB):
            i = step * SWA_QB + b
            rows = slice(b * TQ, (b + 1) * TQ)
            qs = _swa_align_in(q_ref[rows, :], lane, g)
            do = do_ref[rows, :]
            dos = _swa_align_in(do, lane, g)
            dof = do.astype(F32) * o_ref[rows, :].astype(F32)
            d0 = jnp.sum(jnp.where(lane < HEAD_DIM, dof, 0.0), axis=1, keepdims=True)
            d1 = jnp.sum(jnp.where(lane >= HEAD_DIM, dof, 0.0), axis=1, keepdims=True)
            delta = jnp.concatenate([d0, d1], axis=0)
            k2, prev, cur = _swa_kv(k_ref, i)
            v2, _, _ = _swa_kv(v_ref, i)
            pr, psink = _swa_probs(qs, k2, bias_ref[...], sink_ref, i, j)
            ds = pr * (_dot_nt(dos, v2) - delta)
            ds_sum = ds_sum + ds
            sd = psink * delta
            ds0 = -jnp.sum(sd[:TQ, :], axis=0, keepdims=True)
            ds1 = -jnp.sum(sd[TQ:, :], axis=0, keepdims=True)
            dsink = dsink + jnp.where(row8 == 2 * j, ds0, jnp.where(row8 == 2 * j + 1, ds1, 0.0))
            dsb = ds.astype(BF16)
            dq_ref[rows, :] = _swa_align_out(_dot(dsb, k2) * SCALE, lane, g).astype(BF16)
            dk2 = _dot_tn(dsb, qs) * SCALE
            dv2 = _dot_tn(pr.astype(BF16), dos)
            dk_ref[pl.ds(prev, BK), :] += dk2[:BK, :]
            dk_ref[pl.ds(cur, BK), :] += dk2[BK:, :]
            dv_ref[pl.ds(prev, BK), :] += dv2[:BK, :]
            dv_ref[pl.ds(cur, BK), :] += dv2[BK:, :]
        dsacc_ref[...] += ds_sum
        dsink_ref[...] += dsink

        @pl.when(step == nq // SWA_QB - 1)
        def _():
            for hh in range(2):
                def red(b, acc):
                    val = _sum_all(jnp.where(bucket_ref[...] == b, dsacc_ref[hh * TQ:(hh + 1) * TQ, :], 0.0))
                    return jnp.where((row8 == 2 * j + hh) & (lane8 == b), val, acc)
                drel_ref[...] += lax.fori_loop(0, N_BUCKETS, red, jnp.zeros((SW_HEADS, LANES), F32))

    whole = lambda j, i: (0, 0)
    return pl.pallas_call(
        body,
        out_shape=[jax.ShapeDtypeStruct((s, SW_QW), BF16), jax.ShapeDtypeStruct((s, LANES), F32),
                   jax.ShapeDtypeStruct((s, LANES), F32), jax.ShapeDtypeStruct((SW_HEADS, LANES), F32),
                   jax.ShapeDtypeStruct((SW_HEADS, LANES), F32)],
        grid=(npair, nq // SWA_QB),
        in_specs=[pl.BlockSpec((SWA_QB * TQ, LANES), lambda j, i: (i, qcol + j)),
                  pl.BlockSpec((s, LANES), lambda j, i: (0, qcol + npair)),
                  pl.BlockSpec((s, LANES), lambda j, i: (0, qcol + npair + 1)),
                  pl.BlockSpec((SWA_QB * TQ, LANES), lambda j, i: (i, j)),
                  pl.BlockSpec((SWA_QB * TQ, LANES), lambda j, i: (i, j)),
                  pl.BlockSpec((TQ, 2 * BK), whole),
                  pl.BlockSpec(memory_space=pltpu.SMEM),
                  pl.BlockSpec(memory_space=pltpu.SMEM)],
        out_specs=[pl.BlockSpec((SWA_QB * TQ, LANES), lambda j, i: (i, j)),
                   pl.BlockSpec((s, LANES), whole), pl.BlockSpec((s, LANES), whole),
                   pl.BlockSpec((SW_HEADS, LANES), whole), pl.BlockSpec((SW_HEADS, LANES), whole)],
        scratch_shapes=[pltpu.VMEM((2 * TQ, 2 * BK), F32), pltpu.VMEM((2 * TQ, 2 * BK), F32)],
        compiler_params=_params(2), name=name)(qkv, qkv, qkv, ob, dob, bucket, sinks, rel_bias)


def _acc_init(i, *refs):
    @pl.when(i == 0)
    def _():
        for r in refs:
            r[...] = jnp.zeros_like(r)


def _loss_bwd(x3, target, g, name):
    d = x3.shape[1]

    def body(x_ref, t_ref, g_ref, dx_ref, dg_ref, loss_ref):
        _acc_init(pl.program_id(0), dg_ref, loss_ref)
        x = x_ref[...]
        gv = g_ref[...]
        diff = _rms(x, gv) - t_ref[...]
        loss_ref[...] += 0.5 * jnp.sum(jnp.mean(jnp.square(diff), axis=-1, keepdims=True), axis=0, keepdims=True)
        dx, dg = _rms_bwd(diff * (1.0 / d), x, gv)
        dx_ref[...] = dx
        dg_ref[...] += dg

    return _rowcall(name, body, [x3, target], [g], [(d, F32)], [((1, d), F32), ((1, LANES), F32)])


def _ple_bwd(dx3, pe, gt, x2, g, wpg, name):
    d = x2.shape[1]

    def body(dx3_ref, pe_ref, gt_ref, x_ref, g_ref, w_ref, dpe_ref, dgt_ref, dx2_ref, dg_ref):
        _acc_init(pl.program_id(0), dg_ref)
        dx3 = dx3_ref[...]
        sg = _sigmoid(gt_ref[...])
        dpe_ref[...] = (dx3 * sg).astype(BF16)
        dgt = (dx3 * pe_ref[...] * sg * (1.0 - sg)).astype(BF16)
        dgt_ref[...] = dgt
        dx, dg = _rms_bwd(_dot_nt(dgt, w_ref[...]), x_ref[...], g_ref[...])
        dx2_ref[...] = dx3 + dx
        dg_ref[...] += dg

    return _rowcall(name, body, [dx3, pe, gt, x2], [g, wpg], [(d, BF16), (d, BF16), (d, F32)], [((1, d), F32)])


def _ff2_bwd(dx2, u, w2, name):
    d = dx2.shape[1]
    dff = u.shape[1]

    def body(dx_ref, u_ref, w_ref, du_ref, dxb_ref):
        dxb = dx_ref[...].astype(BF16)
        dxb_ref[...] = dxb
        du_ref[...] = (_dot_nt(dxb, w_ref[...]) * (2.0 * jnp.maximum(u_ref[...], 0.0))).astype(BF16)

    return _rowcall(name, body, [dx2, u], [w2], [(dff, BF16), (d, BF16)])


def _ff1_bwd(du, dx2, x1, g, w1, name):
    d = x1.shape[1]

    def body(du_ref, dx2_ref, x_ref, g_ref, w_ref, dx1_ref, dx1b_ref, dg_ref):
        _acc_init(pl.program_id(0), dg_ref)
        dx, dg = _rms_bwd(_dot_nt(du_ref[...], w_ref[...]), x_ref[...], g_ref[...])
        dx1 = dx2_ref[...] + dx
        dx1_ref[...] = dx1
        dx1b_ref[...] = dx1.astype(BF16)
        dg_ref[...] += dg

    return _rowcall(name, body, [du, dx2, x1], [g, w1], [(d, F32), (d, BF16)], [((1, d), F32)])


def _mixer_bwd(dx1b, gates, oa, ob, wo, wua, wub, name):
    d = dx1b.shape[1]

    def body(dx_ref, gate_ref, oa_ref, ob_ref, wo_ref, wua_ref, wub_ref,
             dya_ref, dyb_ref, dgate_ref, doa_ref, dob_ref):
        dm = _dot_nt(dx_ref[...], wo_ref[...])
        sa = _sigmoid(gate_ref[:, :d])
        sb = _sigmoid(gate_ref[:, d:])
        ya = _dot(oa_ref[...], wua_ref[...])
        yb = _dot(ob_ref[...], wub_ref[...])
        dya = (dm * sa).astype(BF16)
        dyb = (dm * sb).astype(BF16)
        dya_ref[...] = dya
        dyb_ref[...] = dyb
        dgate_ref[:, :d] = (dm * ya * sa * (1.0 - sa)).astype(BF16)
        dgate_ref[:, d:] = (dm * yb * sb * (1.0 - sb)).astype(BF16)
        doa_ref[...] = _dot_nt(dya, wua_ref[...]).astype(BF16)
        dob_ref[...] = _dot_nt(dyb, wub_ref[...]).astype(BF16)

    return _rowcall(name, body, [dx1b, gates, oa, ob], [wo, wua, wub],
                    [(d, BF16), (d, BF16), (2 * d, BF16), (SB_W, BF16), (SW_QW, BF16)])


def _inproj_bwd(dqkv, dgates, dx1, x, g, w, name):
    d = x.shape[1]

    def body(dqkv_ref, dgate_ref, dx1_ref, x_ref, g_ref, w_ref, dx_ref, dg_ref):
        _acc_init(pl.program_id(0), dg_ref)
        dh = _dot_nt(dqkv_ref[...], w_ref[:, :QKV_W]) + _dot_nt(dgate_ref[...], w_ref[:, QKV_W:])
        dx, dg = _rms_bwd(dh, x_ref[...], g_ref[...])
        dx_ref[...] = dx1_ref[...] + dx
        dg_ref[...] += dg

    return _rowcall(name, body, [dqkv, dgates, dx1, x], [g, w], [(d, F32)], [((1, d), F32)])


def _tile(n, cap):
    assert n % LANES == 0
    return max(t for t in range(LANES, min(n, cap) + 1, LANES) if n % t == 0)


def _mm_tn(a, b, name, nshard=1):
    s, ka = a.shape
    nb = b.shape[1]
    n = nb // nshard
    ta = _tile(ka, 512)
    tb = _tile(n, 512)
    per = n // tb

    def body(a_ref, b_ref, o_ref):
        o_ref[...] = _dot_tn(a_ref[...].astype(BF16), b_ref[...].astype(BF16))

    return pl.pallas_call(
        body, out_shape=jax.ShapeDtypeStruct((nshard, ka, n), F32), grid=(nb // tb, ka // ta),
        in_specs=[pl.BlockSpec((s, ta), lambda jb, ia: (0, ia)), pl.BlockSpec((s, tb), lambda jb, ia: (0, jb))],
        out_specs=pl.BlockSpec((None, ta, tb), lambda jb, ia: (jb // per, ia, jb % per)),
        compiler_params=_params(2), name=name)(a, b)


def _place():
    return lax.axis_index("x"), lax.axis_index("y"), lax.axis_index("c")


def _chip_peer(x, y, k):
    return (x ^ (k >> 1), y ^ (k & 1))


def _cast_bf16(w, r, name):
    l, k, n = w.shape
    tk = min(k, 256)

    def body(r_ref, w_ref, o_ref):
        o_ref[...] = w_ref[...].astype(BF16)

    return pl.pallas_call(
        body, out_shape=jax.ShapeDtypeStruct((l, 4, k, n), BF16),
        grid_spec=pltpu.PrefetchScalarGridSpec(
            num_scalar_prefetch=1, grid=(l, k // tk),
            in_specs=[pl.BlockSpec((None, tk, n), lambda a, i, r_ref: (a, i, 0))],
            out_specs=pl.BlockSpec((None, None, tk, n), lambda a, i, r_ref: (a, r_ref[0], i, 0))),
        compiler_params=_params(2), name=name)(r, w)


class _Exchange(NamedTuple):
    arrays: tuple
    aliased: tuple
    sems: tuple
    start: Callable
    finish: Callable


def _all_gather(bufs, items):
    def ici(t, b, l, ins, outs, send_sems, recv_sems, x, y, c, r, k):
        return pltpu.make_async_remote_copy(
            src_ref=ins[b].at[l, r, c], dst_ref=outs[b].at[l, r, c], send_sem=send_sems.at[t, k - 1],
            recv_sem=recv_sems.at[t, k - 1], device_id=(*_chip_peer(x, y, k), c), device_id_type=MESH)

    def d2d(t, b, l, outs, send_sems, recv_sems, x, y, c, r, k, half):
        slab = outs[b].at[l, r ^ k, half]
        return pltpu.make_async_remote_copy(
            src_ref=slab, dst_ref=slab, send_sem=send_sems.at[t, 2 + k], recv_sem=recv_sems.at[t, 2 + k],
            device_id=(x, y, 1 - c), device_id_type=MESH)

    def start(ins, outs, send_sems, recv_sems):
        x, y, c = _place()
        r = 2 * x + y
        for t, (b, l) in enumerate(items):
            for k in (1, 2, 3):
                ici(t, b, l, ins, outs, send_sems, recv_sems, x, y, c, r, k).start()

    def finish(ins, outs, send_sems, recv_sems):
        x, y, c = _place()
        r = 2 * x + y
        for t, (b, l) in enumerate(items):
            for k in (1, 2, 3):
                slab = outs[b].at[l, r ^ k, c]
                pltpu.make_async_remote_copy(
                    src_ref=slab, dst_ref=slab, send_sem=send_sems.at[t, k - 1], recv_sem=recv_sems.at[t, k - 1],
                    device_id=(x, y, 1 - c), device_id_type=MESH).wait_recv()
                d2d(t, b, l, outs, send_sems, recv_sems, x, y, c, r, k, c).start()
        for t, (b, l) in enumerate(items):
            for k in (1, 2, 3):
                d2d(t, b, l, outs, send_sems, recv_sems, x, y, c, r, k, 1 - c).wait_recv()
        for t, (b, l) in enumerate(items):
            for k in (1, 2, 3):
                ici(t, b, l, ins, outs, send_sems, recv_sems, x, y, c, r, k).wait_send()
                d2d(t, b, l, outs, send_sems, recv_sems, x, y, c, r, k, c).wait_send()

    return _Exchange(tuple(bufs), tuple(range(len(bufs))), (len(items), 6), start, finish)


def _run_exchange(name, ex):
    n_in, n_out = len(ex.arrays), len(ex.aliased)

    def body(*refs):
        ins, outs = refs[:n_in], refs[n_in:n_in + n_out]
        send_sems, recv_sems = refs[n_in + n_out:]
        ex.start(ins, outs, send_sems, recv_sems)
        ex.finish(ins, outs, send_sems, recv_sems)

    any_spec = pl.BlockSpec(memory_space=pl.ANY)
    return pl.pallas_call(
        body, out_shape=[jax.ShapeDtypeStruct(ex.arrays[a].shape, ex.arrays[a].dtype) for a in ex.aliased],
        in_specs=[any_spec] * n_in, out_specs=[any_spec] * n_out,
        input_output_aliases={a: o for o, a in enumerate(ex.aliased)},
        scratch_shapes=[pltpu.SemaphoreType.DMA(ex.sems), pltpu.SemaphoreType.DMA(ex.sems)],
        name=name)(*ex.arrays)


def _rs_to_sibling(grads, name):
    nt = len(grads)

    def body(*refs):
        ins, outs = refs[:nt], refs[nt:2 * nt]
        send_sems, recv_sems = refs[2 * nt:]
        x, y, c = _place()
        sibling = (x, y, 1 - c)
        cps = []
        for t in range(nt):
            cp = pltpu.make_async_remote_copy(
                src_ref=ins[t].at[:, 1 - c], dst_ref=outs[t], send_sem=send_sems.at[t], recv_sem=recv_sems.at[t],
                device_id=sibling, device_id_type=MESH)
            cp.start()
            cps.append(cp)
        for cp in cps:
            cp.wait()

    any_spec = pl.BlockSpec(memory_space=pl.ANY)
    return pl.pallas_call(
        body, out_shape=[jax.ShapeDtypeStruct((4,) + g.shape[2:], F32) for g in grads],
        in_specs=[any_spec] * nt, out_specs=[any_spec] * nt,
        scratch_shapes=[pltpu.SemaphoreType.DMA((nt,)), pltpu.SemaphoreType.DMA((nt,))],
        name=name)(*grads)


def _add_half(g, recv, cr, name):
    _, _, k2, n = g.shape
    tk = min(k2, 256)

    def body(cr_ref, g_ref, r_ref, sums_ref, mine_ref):
        val = (g_ref[...] + r_ref[...]).astype(BF16)
        sums_ref[...] = val

        @pl.when(pl.program_id(1) == cr_ref[1])
        def _():
            mine_ref[...] = val

    return pl.pallas_call(
        body, out_shape=[jax.ShapeDtypeStruct((4, k2, n), BF16)] * 2,
        grid_spec=pltpu.PrefetchScalarGridSpec(
            num_scalar_prefetch=1, grid=(k2 // tk, 4),
            in_specs=[pl.BlockSpec((None, None, tk, n), lambda i, q, cr_ref: (q, cr_ref[0], i, 0)),
                      pl.BlockSpec((None, tk, n), lambda i, q, cr_ref: (q, i, 0))],
            out_specs=[pl.BlockSpec((None, tk, n), lambda i, q, cr_ref: (q, i, 0)),
                       pl.BlockSpec((None, tk, n), lambda i, q, cr_ref: (cr_ref[1], i, 0))]),
        compiler_params=_params(2), name=name)(cr, g, recv)


def _rs_to_chips(sums, parts):
    nt = len(sums)

    def copies(ins, outs, send_sems, recv_sems):
        x, y, c = _place()
        r = 2 * x + y
        return [pltpu.make_async_remote_copy(
            src_ref=ins[t].at[r ^ k], dst_ref=outs[t].at[r], send_sem=send_sems.at[t, k - 1],
            recv_sem=recv_sems.at[t, k - 1], device_id=(*_chip_peer(x, y, k), c), device_id_type=MESH)
            for t in range(nt) for k in (1, 2, 3)]

    def start(ins, outs, send_sems, recv_sems):
        for cp in copies(ins, outs, send_sems, recv_sems):
            cp.start()

    def finish(ins, outs, send_sems, recv_sems):
        for cp in copies(ins, outs, send_sems, recv_sems):
            cp.wait()

    return _Exchange(tuple(sums) + tuple(parts), tuple(range(nt, 2 * nt)), (nt, 3), start, finish)


def _sum4(parts, cr, name):
    _, k2, n = parts.shape
    tk = min(k2, 256)

    def body(cr_ref, p_ref, o_ref):
        p = p_ref[...].astype(F32)
        o_ref[...] = ((p[0] + p[1]) + p[2]) + p[3]

    return pl.pallas_call(
        body, out_shape=jax.ShapeDtypeStruct((2, k2, n), F32),
        grid_spec=pltpu.PrefetchScalarGridSpec(
            num_scalar_prefetch=1, grid=(k2 // tk,),
            in_specs=[pl.BlockSpec((4, tk, n), lambda i, cr_ref: (0, i, 0))],
            out_specs=pl.BlockSpec((None, tk, n), lambda i, cr_ref: (cr_ref[0], i, 0))),
        compiler_params=_params(1), name=name)(cr, parts)


def _exchange_halves(both, name):
    nt = len(both)

    def body(*refs):
        ins, outs = refs[:nt], refs[nt:2 * nt]
        send_sems, recv_sems = refs[2 * nt:]
        x, y, c = _place()
        cps = []
        for t in range(nt):
            cp = pltpu.make_async_remote_copy(
                src_ref=ins[t].at[c], dst_ref=outs[t].at[c], send_sem=send_sems.at[t], recv_sem=recv_sems.at[t],
                device_id=(x, y, 1 - c), device_id_type=MESH)
            cp.start()
            cps.append(cp)
        for cp in cps:
            cp.wait()

    any_spec = pl.BlockSpec(memory_space=pl.ANY)
    return pl.pallas_call(
        body, out_shape=[jax.ShapeDtypeStruct(b.shape, F32) for b in both],
        in_specs=[any_spec] * nt, out_specs=[any_spec] * nt,
        input_output_aliases={t: t for t in range(nt)},
        scratch_shapes=[pltpu.SemaphoreType.DMA((nt,)), pltpu.SemaphoreType.DMA((nt,))],
        name=name)(*both)


def _adamw_math(w, g, m, v):
    m = ADAM_B1 * m + (1.0 - ADAM_B1) * g
    v = ADAM_B2 * v + (1.0 - ADAM_B2) * jnp.square(g)
    m_hat = m / (1.0 - ADAM_B1 ** ADAM_STEP)
    v_hat = v / (1.0 - ADAM_B2 ** ADAM_STEP)
    delta = -ADAM_LR * (m_hat / (jnp.sqrt(v_hat) + ADAM_EPS) + ADAM_WD * w)
    return delta, m, v


def _adamw(w, m, v, g0, g1, name):
    _, k, n = w.shape
    tk = min(k, 256)
    nk = k // tk

    def body(w_ref, m_ref, v_ref, g0_ref, g1_ref, grad_ref, delta_ref, nm_ref, nv_ref):
        g = jnp.where(pl.program_id(0) == 0, g0_ref[...], g1_ref[...])
        delta, nm, nv = _adamw_math(w_ref[...], g, m_ref[...], v_ref[...])
        grad_ref[...] = g
        delta_ref[...] = delta
        nm_ref[...] = nm
        nv_ref[...] = nv

    lay = pl.BlockSpec((None, tk, n), lambda a, i: (a, i, 0))
    g0_spec = pl.BlockSpec((tk, n), lambda a, i: (jnp.where(a == 0, i, nk - 1), 0))
    g1_spec = pl.BlockSpec((tk, n), lambda a, i: (jnp.where(a == 1, i, 0), 0))
    return pl.pallas_call(
        body, out_shape=[jax.ShapeDtypeStruct(w.shape, F32)] * 4, grid=(2, nk),
        in_specs=[lay, lay, lay, g0_spec, g1_spec], out_specs=[lay] * 4,
        compiler_params=_params(2), name=name)(w, m, v, g0, g1)


def _small_allreduce_adamw(gpart, w, m, v):
    shape = gpart.shape

    def body(g_ref, w_ref, m_ref, v_ref, gsum_ref, delta_ref, nm_ref, nv_ref, recv_ref, send_sems, recv_sems):
        x, y, c = _place()
        me = 4 * x + 2 * y + c
        recv_ref[me] = g_ref[...]
        cps = []
        for k in range(1, 8):
            peer = (x ^ (k >> 2), y ^ ((k >> 1) & 1), c ^ (k & 1))
            cp = pltpu.make_async_remote_copy(
                src_ref=g_ref, dst_ref=recv_ref.at[me], send_sem=send_sems.at[k - 1], recv_sem=recv_sems.at[k - 1],
                device_id=peer, device_id_type=MESH)
            cp.start()
            cps.append(cp)
        for cp in cps:
            cp.wait()
        g = recv_ref[0]
        for dev in range(1, 8):
            g = g + recv_ref[dev]
        delta, nm, nv = _adamw_math(w_ref[...], g, m_ref[...], v_ref[...])
        gsum_ref[...] = g
        delta_ref[...] = delta
        nm_ref[...] = nm
        nv_ref[...] = nv

    vm = pl.BlockSpec(memory_space=pltpu.VMEM)
    return pl.pallas_call(
        body, out_shape=[jax.ShapeDtypeStruct(shape, F32)] * 4, in_specs=[vm] * 4, out_specs=[vm] * 4,
        scratch_shapes=[pltpu.VMEM((8,) + shape, F32), pltpu.SemaphoreType.DMA((7,)), pltpu.SemaphoreType.DMA((7,))],
        name="small_allreduce_adamw")(gpart, w, m, v)


BIG = ("w_in", "w_up_a", "w_up_b", "w_o", "w_ff1", "w_ff2", "w_pe", "w_pg")
COL_SHARDED = ("w_in", "w_up_a", "w_up_b", "w_ff1", "w_pe")
ROW_SHARDED = ("w_o", "w_ff2", "w_pg")
SMALL_ROWS = 16


def _pack_small(g_mix, g_mlp, g_pe, g_final, sinks, rel_bias, loss=None):
    d = g_final.shape[0]
    row = lambda v: jnp.pad(v.reshape(1, -1), ((0, 0), (0, d - v.size)))
    rows = [g_mix, g_mlp, g_pe, g_final.reshape(1, d),
            jnp.zeros((1, d), F32) if loss is None else row(loss), row(sinks), row(rel_bias)]
    out = jnp.concatenate(rows, axis=0)
    return jnp.pad(out, ((0, SMALL_ROWS - out.shape[0]), (0, 0)))


def _unpack_small(a, sinks_shape, rel_shape):
    return (a[0:2], a[2:4], a[4:6], a[6], a[8, :sinks_shape[0] * sinks_shape[1]].reshape(sinks_shape),
            a[9, :rel_shape[0] * rel_shape[1]].reshape(rel_shape))


def kernel(x, p, w_in, w_up_a, w_up_b, w_o, w_ff1, w_ff2, w_pe, w_pg, g_mix, g_mlp, g_pe, g_final, sinks, rel_bias, loss_target, m_w_in, m_w_up_a, m_w_up_b, m_w_o, m_w_ff1, m_w_ff2, m_w_pe, m_w_pg, m_g_mix, m_g_mlp, m_g_pe, m_g_final, m_sinks, m_rel_bias, v_w_in, v_w_up_a, v_w_up_b, v_w_o, v_w_ff1, v_w_ff2, v_w_pe, v_w_pg, v_g_mix, v_g_mlp, v_g_pe, v_g_final, v_sinks, v_rel_bias):
    depth = w_in.shape[0]
    assert depth == 2
    x0 = x[0]
    target = loss_target[0]
    d = x0.shape[1]
    wl = dict(w_in=w_in, w_up_a=w_up_a, w_up_b=w_up_b, w_o=w_o, w_ff1=w_ff1, w_ff2=w_ff2, w_pe=w_pe, w_pg=w_pg)
    ml = dict(w_in=m_w_in, w_up_a=m_w_up_a, w_up_b=m_w_up_b, w_o=m_w_o, w_ff1=m_w_ff1, w_ff2=m_w_ff2, w_pe=m_w_pe, w_pg=m_w_pg)
    vl = dict(w_in=v_w_in, w_up_a=v_w_up_a, w_up_b=v_w_up_b, w_o=v_w_o, w_ff1=v_w_ff1, w_ff2=v_w_ff2, w_pe=v_w_pe, w_pg=v_w_pg)
    c_idx = lax.axis_index("c").astype(jnp.int32)
    r_idx = (2 * lax.axis_index("x") + lax.axis_index("y")).astype(jnp.int32)
    cr = jnp.stack([c_idx, r_idx])

    bufs = []
    for n in BIG:
        w = wl[n]
        l, k, nn = w.shape
        bufs.append(_cast_bf16(w, r_idx.reshape(1), "cast_" + n).reshape(l, 4, 2, k // 2, nn))
    others = [b for b, n in enumerate(BIG) if n != "w_in"]
    bufs = list(_run_exchange("all_gather_first", _all_gather(bufs, [(BIG.index("w_in"), 0)])))

    def gathered(n, l):
        gth = bufs[BIG.index(n)][l]
        _, _, k2, nn = gth.shape
        gth = gth.reshape(4, 2 * k2, nn)
        return gth.transpose(1, 0, 2).reshape(2 * k2, 4 * nn) if n in COL_SHARDED else gth.reshape(8 * k2, nn)

    full = {n: [None] * depth for n in BIG}
    saved = []
    xi = x0
    for i in range(depth):
        st = dict(x0=xi)
        gm = g_mix[i].reshape(1, d)
        full["w_in"][i] = gathered("w_in", i)
        st["h1"], st["qkv"], st["gates"] = _inproj_fwd(xi, gm, full["w_in"][i], f"inproj_fwd_{i}")
        riding = [(b, i) for b in others] + ([(BIG.index("w_in"), 1)] if i == 0 else [])
        st["oa"], st["lt"], st["nb"], bufs = _sb_fwd(st["qkv"], f"sb_fwd_{i}", _all_gather(bufs, riding))
        bufs = list(bufs)
        for n in BIG:
            if n != "w_in":
                full[n][i] = gathered(n, i)
        st["ob"] = _swa_fwd(st["qkv"], sinks[i], rel_bias, f"swa_fwd_{i}")
        st["m"], st["x1"] = _mixer_fwd(st["oa"], st["ob"], st["gates"], xi, full["w_up_a"][i], full["w_up_b"][i],
                                       full["w_o"][i], f"mixer_fwd_{i}")
        st["h2"], st["u"], st["a"] = _ff1_fwd(st["x1"], g_mlp[i].reshape(1, d), full["w_ff1"][i], f"ff1_fwd_{i}")
        st["x2"] = _ff2_fwd(st["a"], st["x1"], full["w_ff2"][i], f"ff2_fwd_{i}")
        st["pb"], st["h3"], st["pe"], st["gt"], xi = _ple_fwd(p[i, 0], st["x2"], g_pe[i].reshape(1, d),
                                                            full["w_pe"][i], full["w_pg"][i], f"ple_fwd_{i}")
        saved.append(st)

    dx, dg_final, loss_part = _loss_bwd(xi, target, g_final.reshape(1, d), "loss_bwd")
    gw = {n: [None] * depth for n in BIG}
    reduced = {}

    def scatter_front(keys, tag):
        tensors = []
        for n, l in keys:
            g = gw[n][l]
            if n in ROW_SHARDED:
                ka, nb = g.shape[1:]
                g = g.reshape(4, ka // 4, nb)
            _, k, nn = g.shape
            tensors.append(g.reshape(4, 2, k // 2, nn))
        from_sibling = _rs_to_sibling(tensors, f"rs_to_sibling_{tag}")
        added = [_add_half(g, r, cr, f"add_half_{n}_{l}") for (n, l), g, r in zip(keys, tensors, from_sibling)]
        return _rs_to_chips([a[0] for a in added], [a[1] for a in added])

    def scatter_back(keys, parts, tag):
        halves = [_sum4(pc, cr, f"sum4_{n}_{l}") for (n, l), pc in zip(keys, parts)]
        for key, both in zip(keys, _exchange_halves(halves, f"exchange_halves_{tag}")):
            reduced[key] = both

    dg_mix, dg_mlp, dg_pe, dsinks = [None] * depth, [None] * depth, [None] * depth, [None] * depth
    drel = jnp.zeros((SW_HEADS, LANES), F32)
    for i in reversed(range(depth)):
        st = saved[i]
        dpe, dgt, dx2, dg_pe[i] = _ple_bwd(dx, st["pe"], st["gt"], st["x2"], g_pe[i].reshape(1, d),
                                           full["w_pg"][i], f"ple_bwd_{i}")
        gw["w_pe"][i] = _mm_tn(st["pb"], dpe, f"dw_pe_{i}", 4)
        gw["w_pg"][i] = _mm_tn(st["h3"], dgt, f"dw_pg_{i}")
        du, dx2b = _ff2_bwd(dx2, st["u"], full["w_ff2"][i], f"ff2_bwd_{i}")
        gw["w_ff2"][i] = _mm_tn(st["a"], dx2b, f"dw_ff2_{i}")
        gw["w_ff1"][i] = _mm_tn(st["h2"], du, f"dw_ff1_{i}", 4)
        dx1, dx1b, dg_mlp[i] = _ff1_bwd(du, dx2, st["x1"], g_mlp[i].reshape(1, d), full["w_ff1"][i], f"ff1_bwd_{i}")
        gw["w_o"][i] = _mm_tn(st["m"], dx1b, f"dw_o_{i}")
        dya, dyb, dgates, doa, dob = _mixer_bwd(dx1b, st["gates"], st["oa"], st["ob"], full["w_o"][i],
                                                full["w_up_a"][i], full["w_up_b"][i], f"mixer_bwd_{i}")
        gw["w_up_a"][i] = _mm_tn(st["oa"], dya, f"dw_up_a_{i}", 4)
        gw["w_up_b"][i] = _mm_tn(st["ob"], dyb, f"dw_up_b_{i}", 4)
        keys = [(n, i) for n in BIG if n != "w_in"] + ([("w_in", 1)] if i == 0 else [])
        dqa, dka, dva, parts = _sb_bwd(st["qkv"], st["lt"], st["nb"], doa, f"sb_bwd_{i}", scatter_front(keys, i))
        scatter_back(keys, parts, i)
        dqb, dkb, dvb, dsk, drl = _swa_bwd(st["qkv"], st["ob"], dob, sinks[i], rel_bias, f"swa_bwd_{i}")
        dsinks[i] = dsk[:, 0]
        drel = drel + drl
        dqkv = jnp.concatenate([dqa, dka.astype(BF16), dva.astype(BF16), dqb, dkb.astype(BF16), dvb.astype(BF16)],
                               axis=1)
        dw_in = jnp.concatenate([_mm_tn(st["h1"], dqkv, f"dw_in_qkv_{i}")[0],
                                 _mm_tn(st["h1"], dgates, f"dw_in_gate_{i}")[0]], axis=1)
        gw["w_in"][i] = dw_in.reshape(d, 4, dw_in.shape[1] // 4).transpose(1, 0, 2)
        dx, dg_mix[i] = _inproj_bwd(dqkv, dgates, dx1, st["x0"], g_mix[i].reshape(1, d), full["w_in"][i],
                                    f"inproj_bwd_{i}")
    grad_x = dx[None]

    keys = [("w_in", 0)]
    scatter_back(keys, _run_exchange("rs_to_chips_last", scatter_front(keys, "last")), "last")
    outs = {}
    for n in BIG:
        g0, g1 = (reduced[n, l].reshape(wl[n].shape[1:]) for l in range(depth))
        outs[n] = _adamw(wl[n], ml[n], vl[n], g0, g1, "adamw_" + n)

    drel_bias = drel[:, :N_BUCKETS].T
    gsmall = _pack_small(jnp.concatenate(dg_mix, 0), jnp.concatenate(dg_mlp, 0), jnp.concatenate(dg_pe, 0),
                         dg_final[0], jnp.stack(dsinks), drel_bias, loss_part[0, :1])
    wsmall = _pack_small(g_mix, g_mlp, g_pe, g_final, sinks, rel_bias)
    msmall = _pack_small(m_g_mix, m_g_mlp, m_g_pe, m_g_final, m_sinks, m_rel_bias)
    vsmall = _pack_small(v_g_mix, v_g_mlp, v_g_pe, v_g_final, v_sinks, v_rel_bias)
    small = _small_allreduce_adamw(gsmall, wsmall, msmall, vsmall)
    loss = small[0][7, 0]
    small = [_unpack_small(a, sinks.shape, rel_bias.shape) for a in small]

    result = [loss, grad_x]
    for kind in range(4):
        result += [outs[n][kind] for n in BIG]
        result += list(small[kind])
    return tuple(result)
```

```python
import functools
import math
from typing import Callable, NamedTuple

import numpy as np
import jax
import jax.numpy as jnp
from jax import lax
from jax.experimental import pallas as pl
from jax.experimental.pallas import tpu as pltpu

F32 = jnp.float32
BF16 = jnp.bfloat16
MESH = pl.DeviceIdType.MESH

HEAD_DIM = 64
SB_HEADS = 8
SW_HEADS = 8
SW_KV_HEADS = 2
WINDOW = 128
N_BUCKETS = 32
MAX_DISTANCE = 128
EPS = 1e-6
SB_W = SB_HEADS * HEAD_DIM
SW_QW = SW_HEADS * HEAD_DIM
SW_KVW = SW_KV_HEADS * HEAD_DIM
QKV_W = 3 * SB_W + SW_QW + 2 * SW_KVW
SCALE = HEAD_DIM ** -0.5
LANES = 128
TQ = 128
BK = 128
NEG = -1e30
SB_EXHAUSTED = -106.0

ADAM_LR = 0.001
ADAM_B1 = 0.9
ADAM_B2 = 0.999
ADAM_EPS = 1e-08
ADAM_WD = 0.01
ADAM_STEP = 10

VMEM_LIMIT = 56 * 1024 * 1024


def _dot(a, b):
    return jnp.dot(a, b, preferred_element_type=F32)


def _dot_nt(a, b):
    return lax.dot_general(a, b, (((1,), (1,)), ((), ())), preferred_element_type=F32)


def _dot_tn(a, b):
    return lax.dot_general(a, b, (((0,), (0,)), ((), ())), preferred_element_type=F32)


def _sum_all(x):
    return jnp.sum(jnp.sum(x, axis=1, keepdims=True), axis=0, keepdims=True)


def _sigmoid(x):
    return 1.0 / (1.0 + jnp.exp(-x))


def _rms(x, g):
    r = lax.rsqrt(jnp.mean(x * x, axis=-1, keepdims=True) + EPS)
    return (x * r) * g


def _rms_bwd(dy, x, g):
    r = lax.rsqrt(jnp.mean(x * x, axis=-1, keepdims=True) + EPS)
    n = x * r
    dg = jnp.sum(dy * n, axis=0, keepdims=True)
    dn = dy * g
    dx = r * (dn - n * jnp.mean(dn * n, axis=-1, keepdims=True))
    return dx, dg


def _params(n_axes):
    return pltpu.CompilerParams(dimension_semantics=("arbitrary",) * n_axes, vmem_limit_bytes=VMEM_LIMIT)


def _rowcall(name, body, row_ins, const_ins, row_outs, acc_outs=(), tm=512):
    s = row_ins[0].shape[0]
    assert s % tm == 0
    in_specs = [pl.BlockSpec((tm, a.shape[1]), lambda i: (i, 0)) for a in row_ins]
    in_specs += [pl.BlockSpec(a.shape, functools.partial(lambda i, nd: (0,) * nd, nd=a.ndim)) for a in const_ins]
    out_shape = [jax.ShapeDtypeStruct((s, c), dt) for c, dt in row_outs]
    out_specs = [pl.BlockSpec((tm, c), lambda i: (i, 0)) for c, _ in row_outs]
    out_shape += [jax.ShapeDtypeStruct(sh, dt) for sh, dt in acc_outs]
    out_specs += [pl.BlockSpec(sh, functools.partial(lambda i, nd: (0,) * nd, nd=len(sh))) for sh, _ in acc_outs]
    return pl.pallas_call(body, out_shape=out_shape, grid=(s // tm,), in_specs=in_specs, out_specs=out_specs,
                          compiler_params=_params(1), name=name)(*row_ins, *const_ins)


def _dot_cols(a, w_ref):
    return jnp.concatenate([_dot(a, w_ref[r]) for r in range(w_ref.shape[0])], axis=1)


def _dot_cols_t(a, w_ref):
    n = w_ref.shape[2]
    out = _dot_nt(a[:, :n], w_ref[0])
    for r in range(1, w_ref.shape[0]):
        out = out + _dot_nt(a[:, r * n:(r + 1) * n], w_ref[r])
    return out


def _inproj_fwd(x, g, wt, name):
    d = x.shape[1]

    def body(x_ref, g_ref, w_ref, h_ref, qkv_ref, gate_ref):
        hb = _rms(x_ref[...], g_ref[...]).astype(BF16)
        h_ref[...] = hb
        qkv_ref[...] = _dot_nt(hb, w_ref[:QKV_W, :]).astype(BF16)
        gate_ref[...] = _dot_nt(hb, w_ref[QKV_W:, :])

    return _rowcall(name, body, [x], [g, wt], [(d, BF16), (QKV_W, BF16), (2 * d, F32)])


def _mixer_fwd(oa, ob, gates, x, wua, wub, wo, name):
    d = x.shape[1]

    def body(oa_ref, ob_ref, gate_ref, x_ref, wua_ref, wub_ref, wo_ref, m_ref, x1_ref):
        ya = _dot_cols(oa_ref[...], wua_ref)
        yb = _dot_cols(ob_ref[...], wub_ref)
        m = _sigmoid(gate_ref[:, :d]) * ya + _sigmoid(gate_ref[:, d:]) * yb
        mb = m.astype(BF16)
        m_ref[...] = mb
        x1_ref[...] = x_ref[...] + _dot(mb, wo_ref[...])

    return _rowcall(name, body, [oa, ob, gates, x], [wua, wub, wo], [(d, BF16), (d, F32)])


def _ff1_fwd(x1, g, w1, name):
    _, d, nq = w1.shape
    dff = 4 * nq

    def body(x_ref, g_ref, w_ref, h_ref, u_ref, a_ref):
        hb = _rms(x_ref[...], g_ref[...]).astype(BF16)
        h_ref[...] = hb
        u = _dot_cols(hb, w_ref)
        u_ref[...] = u
        a_ref[...] = jnp.square(jnp.maximum(u, 0.0)).astype(BF16)

    return _rowcall(name, body, [x1], [g, w1], [(d, BF16), (dff, F32), (dff, BF16)])


def _ff2_fwd(a, x1, w2, name):
    d = x1.shape[1]

    def body(a_ref, x_ref, w_ref, o_ref):
        o_ref[...] = x_ref[...] + _dot(a_ref[...], w_ref[...])

    return _rowcall(name, body, [a, x1], [w2], [(d, F32)])[0]


def _ple_fwd(p, x2, g, wpe, wpg, name):
    d = x2.shape[1]

    def body(p_ref, x_ref, g_ref, wpe_ref, wpg_ref, pb_ref, h_ref, pe_ref, gt_ref, x3_ref):
        pb = p_ref[...].astype(BF16)
        pb_ref[...] = pb
        pe = _dot_cols(pb, wpe_ref)
        x = x_ref[...]
        hb = _rms(x, g_ref[...]).astype(BF16)
        h_ref[...] = hb
        gt = _dot(hb, wpg_ref[...])
        pe_ref[...] = pe
        gt_ref[...] = gt
        x3_ref[...] = x + pe * _sigmoid(gt)

    return _rowcall(name, body, [p, x2], [g, wpe, wpg],
                    [(p.shape[1], BF16), (d, BF16), (d, F32), (d, F32), (d, F32)])


def _pair_stack(t, lane):
    zero = jnp.zeros_like(t)
    return jnp.concatenate([jnp.where(lane < HEAD_DIM, t, zero), jnp.where(lane >= HEAD_DIM, t, zero)], axis=0)


def _sb_block_consts():
    jj = lax.broadcasted_iota(jnp.int32, (BK, 2 * BK), 0)
    ss = lax.broadcasted_iota(jnp.int32, (BK, 2 * BK), 1)
    suffix = jnp.where((ss >= BK) | (jj > ss), 1.0, 0.0).astype(BF16)
    prefix_incl = jnp.where((ss >= BK) | (jj <= ss), 1.0, 0.0).astype(BF16)
    prefix_excl = jnp.where((ss >= BK) | (jj < ss), 1.0, 0.0).astype(BF16)
    return suffix, prefix_incl, prefix_excl


def _sb_rel():
    row = lax.broadcasted_iota(jnp.int32, (2 * TQ, BK), 0)
    row = jnp.where(row >= TQ, row - TQ, row)
    col = lax.broadcasted_iota(jnp.int32, (2 * TQ, BK), 1)
    return col - row


def _split_dot(x, m01):
    hi = x.astype(BF16)
    lo = (x - hi.astype(F32)).astype(BF16)
    return _dot(hi, m01) + _dot(lo, m01)


def _sb_scores(qs, k, mask):
    z = _dot_nt(qs, k) * SCALE
    lb = jnp.minimum(z, 0.0) - jnp.log1p(jnp.exp(-jnp.abs(z)))
    lm = lb - z
    return lb, lm if mask is None else jnp.where(mask, lm, 0.0)


SB_STRAIGHT = 3
SB_WIDE = SB_STRAIGHT * BK
SWA_QB = 4


def _sb_wide_consts():
    j = np.arange(SB_WIDE)[:, None]
    s = np.arange(SB_WIDE)[None, :]
    ones = np.ones((SB_WIDE, BK), np.float32)
    as_bf16 = lambda m: jnp.asarray(m.astype(np.float32), dtype=BF16)
    return (as_bf16(np.concatenate([j > s, ones], axis=1)), as_bf16(np.concatenate([j <= s, ones], axis=1)),
            as_bf16(j < s))


def _side_refs(ex, rest, n_out):
    n_in = len(ex.arrays) if ex else 0
    n_alias = len(ex.aliased) if ex else 0
    ins, rest = rest[:n_in], rest[n_in:]
    outs, rest = rest[:n_out], rest[n_out:]
    return ins, outs, rest[:n_alias], rest[n_alias:]


def _side_specs(ex, n_in, n_out):
    if ex is None:
        return [], [], [], {}, []
    any_spec = pl.BlockSpec(memory_space=pl.ANY)
    return ([any_spec] * len(ex.arrays),
            [jax.ShapeDtypeStruct(ex.arrays[a].shape, ex.arrays[a].dtype) for a in ex.aliased],
            [any_spec] * len(ex.aliased), {n_in + a: n_out + o for o, a in enumerate(ex.aliased)},
            [pltpu.SemaphoreType.DMA(ex.sems), pltpu.SemaphoreType.DMA(ex.sems)])


def _sb_fwd(qkv, name, side=None):
    s = qkv.shape[0]
    nq = s // TQ
    npair = SB_W // LANES
    sufw = _sb_wide_consts()[0]

    def body(q_ref, k_ref, v_ref, sufw_ref, *rest):
        side_in, (o_ref, lt_ref, nb_ref), side_out, scratch = _side_refs(side, rest, 3)
        cf_ref, acc_ref = scratch[:2]
        i = pl.program_id(1)
        if side is not None:
            @pl.when((pl.program_id(0) == 0) & (i == 0))
            def _():
                side.start(side_in, side_out, *scratch[2:])
        lane = lax.broadcasted_iota(jnp.int32, (TQ, LANES), 1)
        qs = _pair_stack(q_ref[...], lane)
        rel = _sb_rel()
        q0 = i * TQ

        straight = i >= SB_STRAIGHT - 1

        @pl.when(straight)
        def _():
            w0 = pl.multiple_of((i - (SB_STRAIGHT - 1)) * BK, BK)
            kw = k_ref[pl.ds(w0, SB_WIDE), :]
            lb, lm = _sb_scores(qs, kw, None)
            own = rel < 0
            past = SB_WIDE - BK
            lm = jnp.concatenate([lm[:, :past], jnp.where(own, lm[:, past:], 0.0)], axis=1)
            cs = _split_dot(lm, sufw_ref[...])
            a = jnp.exp(lb + cs[:, :SB_WIDE])
            a = jnp.concatenate([a[:, :past], jnp.where(own, a[:, past:], 0.0)], axis=1)
            acc_ref[...] = _dot(a.astype(BF16), v_ref[pl.ds(w0, SB_WIDE), :])
            cf_ref[...] = cs[:, SB_WIDE:]

        @pl.when(jnp.logical_not(straight))
        def _():
            cf_ref[...] = jnp.zeros_like(cf_ref)
            acc_ref[...] = jnp.zeros_like(acc_ref)

        def more(c):
            return (c[0] <= i) & (c[1] > SB_EXHAUSTED)

        def step(c):
            k0 = pl.multiple_of((i - c[0]) * BK, BK)
            k = k_ref[pl.ds(k0, BK), :]
            v = v_ref[pl.ds(k0, BK), :]
            mask = rel < (q0 - k0)
            lb, lm = _sb_scores(qs, k, mask)
            cs = _split_dot(lm, _sb_block_consts()[0])
            a = jnp.where(mask, jnp.exp(lb + cs[:, :BK] + cf_ref[...]), 0.0)
            acc_ref[...] += _dot(a.astype(BF16), v)
            cf = cf_ref[...] + cs[:, BK:]
            cf_ref[...] = cf
            return c[0] + 1, jnp.max(cf)

        n_blocks, _ = lax.while_loop(
            more, step, (jnp.where(straight, SB_STRAIGHT, 0).astype(jnp.int32), jnp.max(cf_ref[...])))
        o_ref[...] = jnp.where(lane < HEAD_DIM, acc_ref[:TQ, :], acc_ref[TQ:, :]).astype(BF16)
        lt_ref[...] = cf_ref[...]
        nb_ref[...] = jnp.full(nb_ref.shape, n_blocks, F32)
        if side is not None:
            @pl.when((pl.program_id(0) == npair - 1) & (i == nq - 1))
            def _():
                side.finish(side_in, side_out, *scratch[2:])

    s_in, s_shape, s_out, s_alias, s_sems = _side_specs(side, 4, 3)
    outs = pl.pallas_call(
        body,
        out_shape=[jax.ShapeDtypeStruct((s, SB_W), BF16), jax.ShapeDtypeStruct((npair, nq, 2 * TQ, BK), F32),
                   jax.ShapeDtypeStruct((npair, nq, 8, LANES), F32)] + s_shape,
        grid=(npair, nq),
        in_specs=[pl.BlockSpec((TQ, LANES), lambda j, i: (i, j)),
                  pl.BlockSpec((s, LANES), lambda j, i: (0, npair + j)),
                  pl.BlockSpec((s, LANES), lambda j, i: (0, 2 * npair + j)),
                  pl.BlockSpec(sufw.shape, lambda j, i: (0, 0))] + s_in,
        out_specs=[pl.BlockSpec((TQ, LANES), lambda j, i: (i, j)),
                   pl.BlockSpec((None, None, 2 * TQ, BK), lambda j, i: (j, i, 0, 0)),
                   pl.BlockSpec((None, None, 8, LANES), lambda j, i: (j, i, 0, 0))] + s_out,
        input_output_aliases=s_alias,
        scratch_shapes=[pltpu.VMEM((2 * TQ, BK), F32), pltpu.VMEM((2 * TQ, LANES), F32)] + s_sems,
        compiler_params=_params(2), name=name)(qkv, qkv, qkv, sufw, *(side.arrays if side else ()))
    return outs[0], outs[1], outs[2], outs[3:]


def _sb_bwd(qkv, lt, nb, doa, name, side=None):
    s = qkv.shape[0]
    nq = s // TQ
    npair = SB_W // LANES
    _, prew, prexw = _sb_wide_consts()

    def body(q_ref, k_ref, v_ref, lt_ref, nb_ref, do_ref, prew_ref, prexw_ref, *rest):
        side_in, (dq_ref, dk_ref, dv_ref), side_out, scratch = _side_refs(side, rest, 3)
        cp_ref, ce_ref, dqa_ref = scratch[:3]
        i = pl.program_id(1)
        if side is not None:
            @pl.when((pl.program_id(0) == 0) & (i == 0))
            def _():
                side.start(side_in, side_out, *scratch[3:])
        lane = lax.broadcasted_iota(jnp.int32, (TQ, LANES), 1)
        qs = _pair_stack(q_ref[...], lane)
        dos = _pair_stack(do_ref[...], lane)
        rel = _sb_rel()
        q0 = i * TQ
        n_blocks = jnp.clip(jnp.max(nb_ref[...]).astype(jnp.int32), 1, i + 1)
        first = i + 1 - n_blocks

        @pl.when(i == 0)
        def _():
            dk_ref[...] = jnp.zeros_like(dk_ref)
            dv_ref[...] = jnp.zeros_like(dv_ref)

        straight = n_blocks == SB_STRAIGHT

        @pl.when(straight)
        def _():
            w0 = pl.multiple_of(first * BK, BK)
            kw = k_ref[pl.ds(w0, SB_WIDE), :]
            vw = v_ref[pl.ds(w0, SB_WIDE), :]
            lb, lm = _sb_scores(qs, kw, None)
            own = rel < 0
            past = SB_WIDE - BK
            on_past_keys = lambda t: jnp.concatenate([t[:, :past], jnp.where(own, t[:, past:], 0.0)], axis=1)
            lm = on_past_keys(lm)
            cs = _split_dot(lm, prew_ref[...])
            lt = lt_ref[...]
            a = on_past_keys(jnp.exp(lb + (jnp.concatenate([lt] * SB_STRAIGHT, axis=1) - cs[:, :SB_WIDE])))
            e = a * _dot_nt(dos, vw)
            big_e = _split_dot(e, prexw_ref[...])
            dz = (on_past_keys(e - jnp.exp(lb) * (e + big_e)) * SCALE).astype(BF16)
            dk_ref[pl.ds(w0, SB_WIDE), :] += _dot_tn(dz, qs)
            dv_ref[pl.ds(w0, SB_WIDE), :] += _dot_tn(a.astype(BF16), dos)
            dqa_ref[...] = _dot(dz, kw)

        @pl.when(jnp.logical_not(straight))
        def _():
            cp_ref[...] = jnp.zeros_like(cp_ref)
            ce_ref[...] = jnp.zeros_like(ce_ref)
            dqa_ref[...] = jnp.zeros_like(dqa_ref)

            def step(it, carry):
                k0 = pl.multiple_of((first + it) * BK, BK)
                k = k_ref[pl.ds(k0, BK), :]
                v = v_ref[pl.ds(k0, BK), :]
                mask = rel < (q0 - k0)
                _, prefix_incl, prefix_excl = _sb_block_consts()
                lb, lm = _sb_scores(qs, k, mask)
                cs = _split_dot(lm, prefix_incl)
                a = jnp.where(mask, jnp.exp(lb + (lt_ref[...] - (cs[:, :BK] + cp_ref[...]))), 0.0)
                e = a * _dot_nt(dos, v)
                ce = _split_dot(e, prefix_excl)
                big_e = ce[:, :BK] + ce_ref[...]
                dz = (jnp.where(mask, e - jnp.exp(lb) * (e + big_e), 0.0) * SCALE).astype(BF16)
                dk_ref[pl.ds(k0, BK), :] += _dot_tn(dz, qs)
                dv_ref[pl.ds(k0, BK), :] += _dot_tn(a.astype(BF16), dos)
                dqa_ref[...] += _dot(dz, k)
                cp_ref[...] += cs[:, BK:]
                ce_ref[...] += ce[:, BK:]
                return carry

            lax.fori_loop(0, n_blocks, step, 0)

        dq_ref[...] = jnp.where(lane < HEAD_DIM, dqa_ref[:TQ, :], dqa_ref[TQ:, :]).astype(BF16)
        if side is not None:
            @pl.when((pl.program_id(0) == npair - 1) & (i == nq - 1))
            def _():
                side.finish(side_in, side_out, *scratch[3:])

    s_in, s_shape, s_out, s_alias, s_sems = _side_specs(side, 8, 3)
    outs = pl.pallas_call(
        body,
        out_shape=[jax.ShapeDtypeStruct((s, SB_W), BF16), jax.ShapeDtypeStruct((s, SB_W), F32),
                   jax.ShapeDtypeStruct((s, SB_W), F32)] + s_shape,
        grid=(npair, nq),
        in_specs=[pl.BlockSpec((TQ, LANES), lambda j, i: (i, j)),
                  pl.BlockSpec((s, LANES), lambda j, i: (0, npair + j)),
                  pl.BlockSpec((s, LANES), lambda j, i: (0, 2 * npair + j)),
                  pl.BlockSpec((None, None, 2 * TQ, BK), lambda j, i: (j, i, 0, 0)),
                  pl.BlockSpec((None, None, 8, LANES), lambda j, i: (j, i, 0, 0)),
                  pl.BlockSpec((TQ, LANES), lambda j, i: (i, j)),
                  pl.BlockSpec(prew.shape, lambda j, i: (0, 0)),
                  pl.BlockSpec(prexw.shape, lambda j, i: (0, 0))] + s_in,
        out_specs=[pl.BlockSpec((TQ, LANES), lambda j, i: (i, j)),
                   pl.BlockSpec((s, LANES), lambda j, i: (0, j)),
                   pl.BlockSpec((s, LANES), lambda j, i: (0, j))] + s_out,
        input_output_aliases=s_alias,
        scratch_shapes=[pltpu.VMEM((2 * TQ, BK), F32), pltpu.VMEM((2 * TQ, BK), F32),
                        pltpu.VMEM((2 * TQ, LANES), F32)] + s_sems,
        compiler_params=_params(2), name=name)(qkv, qkv, qkv, lt, nb, doa, prew, prexw,
                                               *(side.arrays if side else ()))
    return outs[0], outs[1], outs[2], outs[3:]


def _bucket_table():
    i = np.arange(TQ)[:, None]
    j = np.arange(2 * BK)[None, :]
    dist = np.maximum(TQ + i - j, 0)
    max_exact = N_BUCKETS // 2
    df = np.maximum(dist, 1).astype(np.float32)
    large = max_exact + (np.log(df / np.float32(max_exact)) / np.float32(math.log(MAX_DISTANCE / max_exact))
                         * np.float32(N_BUCKETS - max_exact)).astype(np.int32)
    large = np.minimum(large, N_BUCKETS - 1)
    return np.where(dist < max_exact, dist, large).astype(np.int32)


def _swa_align_in(t, lane, g):
    tf = t.astype(F32)
    tr = pltpu.roll(tf, HEAD_DIM, 1)
    gmask = (lane >= HEAD_DIM) == (g == 1)
    top = jnp.where(gmask, jnp.where(g == 0, tf, tr), 0.0)
    bot = jnp.where(gmask, jnp.where(g == 1, tf, tr), 0.0)
    return jnp.concatenate([top, bot], axis=0).astype(BF16)


def _swa_align_out(t, lane, g):
    top, bot = t[:TQ, :], t[TQ:, :]
    top = jnp.where(g == 0, top, pltpu.roll(top, HEAD_DIM, 1))
    bot = jnp.where(g == 1, bot, pltpu.roll(bot, HEAD_DIM, 1))
    return jnp.where(lane < HEAD_DIM, top, bot)


def _swa_bias(bias_ref, bucket_ref, rb_ref, j):
    for hh in range(2):
        def add(b, acc):
            return acc + jnp.where(bucket_ref[...] == b, rb_ref[b, 2 * j + hh], 0.0)
        bias_ref[hh * TQ:(hh + 1) * TQ, :] = lax.fori_loop(0, N_BUCKETS, add, jnp.zeros((TQ, 2 * BK), F32))


def _swa_probs(qs, k2, bias, sink_ref, i, j):
    s = _dot_nt(qs, k2) * SCALE + bias
    row = lax.broadcasted_iota(jnp.int32, (2 * TQ, 2 * BK), 0)
    col = lax.broadcasted_iota(jnp.int32, (2 * TQ, 2 * BK), 1)
    dist = TQ + jnp.where(row >= TQ, row - TQ, row) - col
    valid = (dist >= 0) & (dist < WINDOW) & ((col >= BK) | (i > 0))
    s = jnp.where(valid, s, NEG)
    row1 = lax.broadcasted_iota(jnp.int32, (2 * TQ, 1), 0)
    sink = jnp.where(row1 < TQ, sink_ref[2 * j], sink_ref[2 * j + 1])
    m = jnp.maximum(jnp.max(s, axis=1, keepdims=True), sink)
    e = jnp.exp(s - m)
    es = jnp.exp(sink - m)
    den = jnp.sum(e, axis=1, keepdims=True) + es
    return e / den, es / den


def _swa_kv(ref, i):
    prev = pl.multiple_of(jnp.maximum(i - 1, 0) * BK, BK)
    cur = pl.multiple_of(i * BK, BK)
    return jnp.concatenate([ref[pl.ds(prev, BK), :], ref[pl.ds(cur, BK), :]], axis=0), prev, cur


def _swa_fwd(qkv, sinks, rel_bias, name):
    s = qkv.shape[0]
    nq = s // TQ
    npair = SW_QW // LANES
    qcol = 3 * SB_W // LANES
    bucket = jnp.asarray(_bucket_table())

    def body(q_ref, k_ref, v_ref, bucket_ref, sink_ref, rb_ref, o_ref, bias_ref):
        j = pl.program_id(0)
        step = pl.program_id(1)
        g = j // 2
        lane = lax.broadcasted_iota(jnp.int32, (TQ, LANES), 1)

        @pl.when(step == 0)
        def _():
            _swa_bias(bias_ref, bucket_ref, rb_ref, j)

        for b in range(SWA_QB):
            i = step * SWA_QB + b
            rows = slice(b * TQ, (b + 1) * TQ)
            qs = _swa_align_in(q_ref[rows, :], lane, g)
            k2, _, _ = _swa_kv(k_ref, i)
            v2, _, _ = _swa_kv(v_ref, i)
            pr, _ = _swa_probs(qs, k2, bias_ref[...], sink_ref, i, j)
            o_ref[rows, :] = _swa_align_out(_dot(pr.astype(BF16), v2), lane, g).astype(BF16)

    assert nq % SWA_QB == 0
    return pl.pallas_call(
        body, out_shape=jax.ShapeDtypeStruct((s, SW_QW), BF16), grid=(npair, nq // SWA_QB),
        in_specs=[pl.BlockSpec((SWA_QB * TQ, LANES), lambda j, i: (i, qcol + j)),
                  pl.BlockSpec((s, LANES), lambda j, i: (0, qcol + npair)),
                  pl.BlockSpec((s, LANES), lambda j, i: (0, qcol + npair + 1)),
                  pl.BlockSpec((TQ, 2 * BK), lambda j, i: (0, 0)),
                  pl.BlockSpec(memory_space=pltpu.SMEM),
                  pl.BlockSpec(memory_space=pltpu.SMEM)],
        out_specs=pl.BlockSpec((SWA_QB * TQ, LANES), lambda j, i: (i, j)),
        scratch_shapes=[pltpu.VMEM((2 * TQ, 2 * BK), F32)],
        compiler_params=_params(2), name=name)(qkv, qkv, qkv, bucket, sinks, rel_bias)


def _swa_bwd(qkv, ob, dob, sinks, rel_bias, name):
    s = qkv.shape[0]
    nq = s // TQ
    npair = SW_QW // LANES
    qcol = 3 * SB_W // LANES
    bucket = jnp.asarray(_bucket_table())

    def body(q_ref, k_ref, v_ref, o_ref, do_ref, bucket_ref, sink_ref, rb_ref,
             dq_ref, dk_ref, dv_ref, dsink_ref, drel_ref, bias_ref, dsacc_ref):
        j = pl.program_id(0)
        step = pl.program_id(1)
        g = j // 2
        lane = lax.broadcasted_iota(jnp.int32, (TQ, LANES), 1)
        row8 = lax.broadcasted_iota(jnp.int32, (SW_HEADS, LANES), 0)
        lane8 = lax.broadcasted_iota(jnp.int32, (SW_HEADS, LANES), 1)

        @pl.when((step == 0) & (j == 0))
        def _():
            dk_ref[...] = jnp.zeros_like(dk_ref)
            dv_ref[...] = jnp.zeros_like(dv_ref)
            dsink_ref[...] = jnp.zeros_like(dsink_ref)
            drel_ref[...] = jnp.zeros_like(drel_ref)

        @pl.when(step == 0)
        def _():
            _swa_bias(bias_ref, bucket_ref, rb_ref, j)
            dsacc_ref[...] = jnp.zeros_like(dsacc_ref)

        ds_sum = jnp.zeros(dsacc_ref.shape, F32)
        dsink = jnp.zeros((SW_HEADS, LANES), F32)
        for b in range(SWA_QB):
            i = step * SWA_QB + b
            rows = slice(b * TQ, (b + 1) * TQ)
            qs = _swa_align_in(q_ref[rows, :], lane, g)
            do = do_ref[rows, :]
            dos = _swa_align_in(do, lane, g)
            dof = do.astype(F32) * o_ref[rows, :].astype(F32)
            d0 = jnp.sum(jnp.where(lane < HEAD_DIM, dof, 0.0), axis=1, keepdims=True)
            d1 = jnp.sum(jnp.where(lane >= HEAD_DIM, dof, 0.0), axis=1, keepdims=True)
            delta = jnp.concatenate([d0, d1], axis=0)
            k2, prev, cur = _swa_kv(k_ref, i)
            v2, _, _ = _swa_kv(v_ref, i)
            pr, psink = _swa_probs(qs, k2, bias_ref[...], sink_ref, i, j)
            ds = pr * (_dot_nt(dos, v2) - delta)
            ds_sum = ds_sum + ds
            sd = psink * delta
            ds0 = -jnp.sum(sd[:TQ, :], axis=0, keepdims=True)
            ds1 = -jnp.sum(sd[TQ:, :], axis=0, keepdims=True)
            dsink = dsink + jnp.where(row8 == 2 * j, ds0, jnp.where(row8 == 2 * j + 1, ds1, 0.0))
            dsb = ds.astype(BF16)
            dq_ref[rows, :] = _swa_align_out(_dot(dsb, k2) * SCALE, lane, g).astype(BF16)
            dk2 = _dot_tn(dsb, qs) * SCALE
            dv2 = _dot_tn(pr.astype(BF16), dos)
            dk_ref[pl.ds(prev, BK), :] += dk2[:BK, :]
            dk_ref[pl.ds(cur, BK), :] += dk2[BK:, :]
            dv_ref[pl.ds(prev, BK), :] += dv2[:BK, :]
            dv_ref[pl.ds(cur, BK), :] += dv2[BK:, :]
        dsacc_ref[...] += ds_sum
        dsink_ref[...] += dsink

        @pl.when(step == nq // SWA_QB - 1)
        def _():
            for hh in range(2):
                def red(b, acc):
                    val = _sum_all(jnp.where(bucket_ref[...] == b, dsacc_ref[hh * TQ:(hh + 1) * TQ, :], 0.0))
                    return jnp.where((row8 == 2 * j + hh) & (lane8 == b), val, acc)
                drel_ref[...] += lax.fori_loop(0, N_BUCKETS, red, jnp.zeros((SW_HEADS, LANES), F32))

    whole = lambda j, i: (0, 0)
    return pl.pallas_call(
        body,
        out_shape=[jax.ShapeDtypeStruct((s, SW_QW), BF16), jax.ShapeDtypeStruct((s, LANES), F32),
                   jax.ShapeDtypeStruct((s, LANES), F32), jax.ShapeDtypeStruct((SW_HEADS, LANES), F32),
                   jax.ShapeDtypeStruct((SW_HEADS, LANES), F32)],
        grid=(npair, nq // SWA_QB),
        in_specs=[pl.BlockSpec((SWA_QB * TQ, LANES), lambda j, i: (i, qcol + j)),
                  pl.BlockSpec((s, LANES), lambda j, i: (0, qcol + npair)),
                  pl.BlockSpec((s, LANES), lambda j, i: (0, qcol + npair + 1)),
                  pl.BlockSpec((SWA_QB * TQ, LANES), lambda j, i: (i, j)),
                  pl.BlockSpec((SWA_QB * TQ, LANES), lambda j, i: (i, j)),
                  pl.BlockSpec((TQ, 2 * BK), whole),
                  pl.BlockSpec(memory_space=pltpu.SMEM),
                  pl.BlockSpec(memory_space=pltpu.SMEM)],
        out_specs=[pl.BlockSpec((SWA_QB * TQ, LANES), lambda j, i: (i, j)),
                   pl.BlockSpec((s, LANES), whole), pl.BlockSpec((s, LANES), whole),
                   pl.BlockSpec((SW_HEADS, LANES), whole), pl.BlockSpec((SW_HEADS, LANES), whole)],
        scratch_shapes=[pltpu.VMEM((2 * TQ, 2 * BK), F32), pltpu.VMEM((2 * TQ, 2 * BK), F32)],
        compiler_params=_params(2), name=name)(qkv, qkv, qkv, ob, dob, bucket, sinks, rel_bias)


def _acc_init(i, *refs):
    @pl.when(i == 0)
    def _():
        for r in refs:
            r[...] = jnp.zeros_like(r)


def _loss_bwd(x3, target, g, name):
    d = x3.shape[1]

    def body(x_ref, t_ref, g_ref, dx_ref, dg_ref, loss_ref):
        _acc_init(pl.program_id(0), dg_ref, loss_ref)
        x = x_ref[...]
        gv = g_ref[...]
        diff = _rms(x, gv) - t_ref[...]
        loss_ref[...] += 0.5 * jnp.sum(jnp.mean(jnp.square(diff), axis=-1, keepdims=True), axis=0, keepdims=True)
        dx, dg = _rms_bwd(diff * (1.0 / d), x, gv)
        dx_ref[...] = dx
        dg_ref[...] += dg

    return _rowcall(name, body, [x3, target], [g], [(d, F32)], [((1, d), F32), ((1, LANES), F32)])


def _ple_bwd(dx3, pe, gt, x2, g, wpg, name):
    d = x2.shape[1]

    def body(dx3_ref, pe_ref, gt_ref, x_ref, g_ref, w_ref, dpe_ref, dgt_ref, dx2_ref, dg_ref):
        _acc_init(pl.program_id(0), dg_ref)
        dx3 = dx3_ref[...]
        sg = _sigmoid(gt_ref[...])
        dpe_ref[...] = (dx3 * sg).astype(BF16)
        dgt = (dx3 * pe_ref[...] * sg * (1.0 - sg)).astype(BF16)
        dgt_ref[...] = dgt
        dx, dg = _rms_bwd(_dot_nt(dgt, w_ref[...]), x_ref[...], g_ref[...])
        dx2_ref[...] = dx3 + dx
        dg_ref[...] += dg

    return _rowcall(name, body, [dx3, pe, gt, x2], [g, wpg], [(d, BF16), (d, BF16), (d, F32)], [((1, d), F32)])


def _ff2_bwd(dx2, u, w2, name):
    d = dx2.shape[1]
    dff = u.shape[1]

    def body(dx_ref, u_ref, w_ref, du_ref, dxb_ref):
        dxb = dx_ref[...].astype(BF16)
        dxb_ref[...] = dxb
        du_ref[...] = (_dot_nt(dxb, w_ref[...]) * (2.0 * jnp.maximum(u_ref[...], 0.0))).astype(BF16)

    return _rowcall(name, body, [dx2, u], [w2], [(dff, BF16), (d, BF16)])


def _ff1_bwd(du, dx2, x1, g, w1, name):
    d = x1.shape[1]

    def body(du_ref, dx2_ref, x_ref, g_ref, w_ref, dx1_ref, dx1b_ref, dg_ref):
        _acc_init(pl.program_id(0), dg_ref)
        dx, dg = _rms_bwd(_dot_cols_t(du_ref[...], w_ref), x_ref[...], g_ref[...])
        dx1 = dx2_ref[...] + dx
        dx1_ref[...] = dx1
        dx1b_ref[...] = dx1.astype(BF16)
        dg_ref[...] += dg

    return _rowcall(name, body, [du, dx2, x1], [g, w1], [(d, F32), (d, BF16)], [((1, d), F32)])


def _mixer_bwd(dx1b, gates, oa, ob, wo, wua, wub, name):
    d = dx1b.shape[1]

    def body(dx_ref, gate_ref, oa_ref, ob_ref, wo_ref, wua_ref, wub_ref,
             dya_ref, dyb_ref, dgate_ref, doa_ref, dob_ref):
        dm = _dot_nt(dx_ref[...], wo_ref[...])
        sa = _sigmoid(gate_ref[:, :d])
        sb = _sigmoid(gate_ref[:, d:])
        ya = _dot_cols(oa_ref[...], wua_ref)
        yb = _dot_cols(ob_ref[...], wub_ref)
        dya = (dm * sa).astype(BF16)
        dyb = (dm * sb).astype(BF16)
        dya_ref[...] = dya
        dyb_ref[...] = dyb
        dgate_ref[:, :d] = (dm * ya * sa * (1.0 - sa)).astype(BF16)
        dgate_ref[:, d:] = (dm * yb * sb * (1.0 - sb)).astype(BF16)
        doa_ref[...] = _dot_cols_t(dya, wua_ref).astype(BF16)
        dob_ref[...] = _dot_cols_t(dyb, wub_ref).astype(BF16)

    return _rowcall(name, body, [dx1b, gates, oa, ob], [wo, wua, wub],
                    [(d, BF16), (d, BF16), (2 * d, BF16), (SB_W, BF16), (SW_QW, BF16)])


def _inproj_bwd(dqkv, dgates, dx1, x, g, w, name):
    d = x.shape[1]

    def body(dqkv_ref, dgate_ref, dx1_ref, x_ref, g_ref, w_ref, dx_ref, dg_ref):
        _acc_init(pl.program_id(0), dg_ref)
        dh = _dot(dqkv_ref[...], w_ref[:QKV_W, :]) + _dot(dgate_ref[...], w_ref[QKV_W:, :])
        dx, dg = _rms_bwd(dh, x_ref[...], g_ref[...])
        dx_ref[...] = dx1_ref[...] + dx
        dg_ref[...] += dg

    return _rowcall(name, body, [dqkv, dgates, dx1, x], [g, w], [(d, F32)], [((1, d), F32)])


def _tile(n, cap):
    assert n % LANES == 0
    return max(t for t in range(LANES, min(n, cap) + 1, LANES) if n % t == 0)


def _mm_tn(a, b, name, nshard=1):
    s, ka = a.shape
    nb = b.shape[1]
    n = nb // nshard
    ta = _tile(ka, 512)
    tb = _tile(n, 512)
    per = n // tb

    def body(a_ref, b_ref, o_ref):
        o_ref[...] = _dot_tn(a_ref[...].astype(BF16), b_ref[...].astype(BF16))

    return pl.pallas_call(
        body, out_shape=jax.ShapeDtypeStruct((nshard, ka, n), F32), grid=(nb // tb, ka // ta),
        in_specs=[pl.BlockSpec((s, ta), lambda jb, ia: (0, ia)), pl.BlockSpec((s, tb), lambda jb, ia: (0, jb))],
        out_specs=pl.BlockSpec((None, ta, tb), lambda jb, ia: (jb // per, ia, jb % per)),
        compiler_params=_params(2), name=name)(a, b)


def _mm_tn_pieces(pieces, b, name):
    s, nb = b.shape
    ta = 256
    tiles = [pc.shape[1] // ta for pc in pieces]
    assert all(pc.shape[1] % ta == 0 for pc in pieces)
    starts = [sum(tiles[:p]) for p in range(len(pieces))]
    tb = _tile(nb, 512)

    def body(*refs):
        a_refs, b_ref, o_ref = refs[:-2], refs[-2], refs[-1]
        ia = pl.program_id(1)
        a = a_refs[0][...]
        for p in range(1, len(pieces)):
            a = jnp.where(ia >= starts[p], a_refs[p][...], a)
        o_ref[...] = _dot_tn(a, b_ref[...])

    def piece_spec(p):
        return pl.BlockSpec((s, ta), lambda jb, ia: (0, jnp.clip(ia - starts[p], 0, tiles[p] - 1)))

    return pl.pallas_call(
        body, out_shape=jax.ShapeDtypeStruct((sum(tiles) * ta, nb), F32), grid=(nb // tb, sum(tiles)),
        in_specs=[piece_spec(p) for p in range(len(pieces))] + [pl.BlockSpec((s, tb), lambda jb, ia: (0, jb))],
        out_specs=pl.BlockSpec((ta, tb), lambda jb, ia: (ia, jb)),
        compiler_params=_params(2), name=name)(*pieces, b)


def _place():
    return lax.axis_index("x"), lax.axis_index("y"), lax.axis_index("c")


def _chip_peer(x, y, k):
    return (x ^ (k >> 1), y ^ (k & 1))


def _row_tile(k, cap=256):
    return max(t for t in range(16, min(k, cap) + 1, 16) if k % t == 0)


def _cast_bf16(w, r, name):
    l, k, n = w.shape
    assert l == 2
    tk = _row_tile(k)

    def body(r_ref, w_ref, o0_ref, o1_ref):
        o0_ref[...] = w_ref[0].astype(BF16)
        o1_ref[...] = w_ref[1].astype(BF16)

    out_spec = pl.BlockSpec((None, tk, n), lambda i, r_ref: (r_ref[0], i, 0))
    return pl.pallas_call(
        body, out_shape=[jax.ShapeDtypeStruct((4, k, n), BF16)] * 2,
        grid_spec=pltpu.PrefetchScalarGridSpec(
            num_scalar_prefetch=1, grid=(k // tk,),
            in_specs=[pl.BlockSpec((l, tk, n), lambda i, r_ref: (0, i, 0))],
            out_specs=[out_spec, out_spec]),
        compiler_params=_params(1), name=name)(r, w)


class _Exchange(NamedTuple):
    arrays: tuple
    aliased: tuple
    sems: tuple
    start: Callable
    finish: Callable


def _all_gather(bufs):
    nt = len(bufs)

    def ici(t, ins, outs, send_sems, recv_sems, x, y, c, r, k):
        return pltpu.make_async_remote_copy(
            src_ref=ins[t].at[r, c], dst_ref=outs[t].at[r, c], send_sem=send_sems.at[t, k - 1],
            recv_sem=recv_sems.at[t, k - 1], device_id=(*_chip_peer(x, y, k), c), device_id_type=MESH)

    def d2d(t, outs, send_sems, recv_sems, x, y, c, r, k, half):
        slab = outs[t].at[r ^ k, half]
        return pltpu.make_async_remote_copy(
            src_ref=slab, dst_ref=slab, send_sem=send_sems.at[t, 2 + k], recv_sem=recv_sems.at[t, 2 + k],
            device_id=(x, y, 1 - c), device_id_type=MESH)

    def start(ins, outs, send_sems, recv_sems):
        x, y, c = _place()
        r = 2 * x + y
        for t in range(nt):
            for k in (1, 2, 3):
                ici(t, ins, outs, send_sems, recv_sems, x, y, c, r, k).start()

    def finish(ins, outs, send_sems, recv_sems):
        x, y, c = _place()
        r = 2 * x + y
        for t in range(nt):
            for k in (1, 2, 3):
                slab = outs[t].at[r ^ k, c]
                pltpu.make_async_remote_copy(
                    src_ref=slab, dst_ref=slab, send_sem=send_sems.at[t, k - 1], recv_sem=recv_sems.at[t, k - 1],
                    device_id=(x, y, 1 - c), device_id_type=MESH).wait_recv()
                d2d(t, outs, send_sems, recv_sems, x, y, c, r, k, c).start()
        for t in range(nt):
            for k in (1, 2, 3):
                d2d(t, outs, send_sems, recv_sems, x, y, c, r, k, 1 - c).wait_recv()
        for t in range(nt):
            for k in (1, 2, 3):
                ici(t, ins, outs, send_sems, recv_sems, x, y, c, r, k).wait_send()
                d2d(t, outs, send_sems, recv_sems, x, y, c, r, k, c).wait_send()

    return _Exchange(tuple(bufs), tuple(range(nt)), (nt, 6), start, finish)


def _run_exchange(name, ex):
    n_in, n_out = len(ex.arrays), len(ex.aliased)

    def body(*refs):
        ins, outs = refs[:n_in], refs[n_in:n_in + n_out]
        send_sems, recv_sems = refs[n_in + n_out:]
        ex.start(ins, outs, send_sems, recv_sems)
        ex.finish(ins, outs, send_sems, recv_sems)

    any_spec = pl.BlockSpec(memory_space=pl.ANY)
    return pl.pallas_call(
        body, out_shape=[jax.ShapeDtypeStruct(ex.arrays[a].shape, ex.arrays[a].dtype) for a in ex.aliased],
        in_specs=[any_spec] * n_in, out_specs=[any_spec] * n_out,
        input_output_aliases={a: o for o, a in enumerate(ex.aliased)},
        scratch_shapes=[pltpu.SemaphoreType.DMA(ex.sems), pltpu.SemaphoreType.DMA(ex.sems)],
        name=name)(*ex.arrays)


def _rs_to_sibling(grads, name):
    nt = len(grads)

    def body(*refs):
        ins, outs = refs[:nt], refs[nt:2 * nt]
        send_sems, recv_sems = refs[2 * nt:]
        x, y, c = _place()
        sibling = (x, y, 1 - c)
        cps = []
        for t in range(nt):
            cp = pltpu.make_async_remote_copy(
                src_ref=ins[t].at[:, 1 - c], dst_ref=outs[t], send_sem=send_sems.at[t], recv_sem=recv_sems.at[t],
                device_id=sibling, device_id_type=MESH)
            cp.start()
            cps.append(cp)
        for cp in cps:
            cp.wait()

    any_spec = pl.BlockSpec(memory_space=pl.ANY)
    return pl.pallas_call(
        body, out_shape=[jax.ShapeDtypeStruct((4,) + g.shape[2:], F32) for g in grads],
        in_specs=[any_spec] * nt, out_specs=[any_spec] * nt,
        scratch_shapes=[pltpu.SemaphoreType.DMA((nt,)), pltpu.SemaphoreType.DMA((nt,))],
        name=name)(*grads)


def _add_half(g, recv, cr, name):
    _, _, k2, n = g.shape
    tk = _row_tile(k2)

    def body(cr_ref, g_ref, r_ref, sums_ref, mine_ref):
        val = (g_ref[...] + r_ref[...]).astype(BF16)
        sums_ref[...] = val

        @pl.when(pl.program_id(1) == cr_ref[1])
        def _():
            mine_ref[...] = val

    return pl.pallas_call(
        body, out_shape=[jax.ShapeDtypeStruct((4, k2, n), BF16)] * 2,
        grid_spec=pltpu.PrefetchScalarGridSpec(
            num_scalar_prefetch=1, grid=(k2 // tk, 4),
            in_specs=[pl.BlockSpec((None, None, tk, n), lambda i, q, cr_ref: (q, cr_ref[0], i, 0)),
                      pl.BlockSpec((None, tk, n), lambda i, q, cr_ref: (q, i, 0))],
            out_specs=[pl.BlockSpec((None, tk, n), lambda i, q, cr_ref: (q, i, 0)),
                       pl.BlockSpec((None, tk, n), lambda i, q, cr_ref: (cr_ref[1], i, 0))]),
        compiler_params=_params(2), name=name)(cr, g, recv)


def _rs_to_chips(sums, parts):
    nt = len(sums)

    def copies(ins, outs, send_sems, recv_sems):
        x, y, c = _place()
        r = 2 * x + y
        return [pltpu.make_async_remote_copy(
            src_ref=ins[t].at[r ^ k], dst_ref=outs[t].at[r], send_sem=send_sems.at[t, k - 1],
            recv_sem=recv_sems.at[t, k - 1], device_id=(*_chip_peer(x, y, k), c), device_id_type=MESH)
            for t in range(nt) for k in (1, 2, 3)]

    def start(ins, outs, send_sems, recv_sems):
        for cp in copies(ins, outs, send_sems, recv_sems):
            cp.start()

    def finish(ins, outs, send_sems, recv_sems):
        for cp in copies(ins, outs, send_sems, recv_sems):
            cp.wait()

    return _Exchange(tuple(sums) + tuple(parts), tuple(range(nt, 2 * nt)), (nt, 3), start, finish)


def _sum4(parts, cr, name):
    _, k2, n = parts.shape
    tk = _row_tile(k2)

    def body(cr_ref, p_ref, o_ref):
        p = p_ref[...].astype(F32)
        o_ref[...] = ((p[0] + p[1]) + p[2]) + p[3]

    return pl.pallas_call(
        body, out_shape=jax.ShapeDtypeStruct((2, k2, n), F32),
        grid_spec=pltpu.PrefetchScalarGridSpec(
            num_scalar_prefetch=1, grid=(k2 // tk,),
            in_specs=[pl.BlockSpec((4, tk, n), lambda i, cr_ref: (0, i, 0))],
            out_specs=pl.BlockSpec((None, tk, n), lambda i, cr_ref: (cr_ref[0], i, 0))),
        compiler_params=_params(1), name=name)(cr, parts)


def _exchange_halves(both, name):
    nt = len(both)

    def body(*refs):
        ins, outs = refs[:nt], refs[nt:2 * nt]
        send_sems, recv_sems = refs[2 * nt:]
        x, y, c = _place()
        cps = []
        for t in range(nt):
            cp = pltpu.make_async_remote_copy(
                src_ref=ins[t].at[c], dst_ref=outs[t].at[c], send_sem=send_sems.at[t], recv_sem=recv_sems.at[t],
                device_id=(x, y, 1 - c), device_id_type=MESH)
            cp.start()
            cps.append(cp)
        for cp in cps:
            cp.wait()

    any_spec = pl.BlockSpec(memory_space=pl.ANY)
    return pl.pallas_call(
        body, out_shape=[jax.ShapeDtypeStruct(b.shape, F32) for b in both],
        in_specs=[any_spec] * nt, out_specs=[any_spec] * nt,
        input_output_aliases={t: t for t in range(nt)},
        scratch_shapes=[pltpu.SemaphoreType.DMA((nt,)), pltpu.SemaphoreType.DMA((nt,))],
        name=name)(*both)


def _adamw_math(w, g, m, v):
    m = ADAM_B1 * m + (1.0 - ADAM_B1) * g
    v = ADAM_B2 * v + (1.0 - ADAM_B2) * jnp.square(g)
    m_hat = m / (1.0 - ADAM_B1 ** ADAM_STEP)
    v_hat = v / (1.0 - ADAM_B2 ** ADAM_STEP)
    delta = -ADAM_LR * (m_hat / (jnp.sqrt(v_hat) + ADAM_EPS) + ADAM_WD * w)
    return delta, m, v


def _adamw(w, m, v, g0, g1, name):
    _, k, n = w.shape
    tk = _row_tile(k)
    nk = k // tk

    def body(w_ref, m_ref, v_ref, g0_ref, g1_ref, grad_ref, delta_ref, nm_ref, nv_ref):
        g = jnp.where(pl.program_id(0) == 0, g0_ref[...], g1_ref[...])
        delta, nm, nv = _adamw_math(w_ref[...], g, m_ref[...], v_ref[...])
        grad_ref[...] = g
        delta_ref[...] = delta
        nm_ref[...] = nm
        nv_ref[...] = nv

    lay = pl.BlockSpec((None, tk, n), lambda a, i: (a, i, 0))
    g0_spec = pl.BlockSpec((tk, n), lambda a, i: (jnp.where(a == 0, i, nk - 1), 0))
    g1_spec = pl.BlockSpec((tk, n), lambda a, i: (jnp.where(a == 1, i, 0), 0))
    return pl.pallas_call(
        body, out_shape=[jax.ShapeDtypeStruct(w.shape, F32)] * 4, grid=(2, nk),
        in_specs=[lay, lay, lay, g0_spec, g1_spec], out_specs=[lay] * 4,
        compiler_params=_params(2), name=name)(w, m, v, g0, g1)


def _small_allreduce_adamw(gpart, w, m, v):
    shape = gpart.shape

    def body(g_ref, w_ref, m_ref, v_ref, gsum_ref, delta_ref, nm_ref, nv_ref, recv_ref, send_sems, recv_sems):
        x, y, c = _place()
        me = 4 * x + 2 * y + c
        recv_ref[me] = g_ref[...]
        cps = []
        for k in range(1, 8):
            peer = (x ^ (k >> 2), y ^ ((k >> 1) & 1), c ^ (k & 1))
            cp = pltpu.make_async_remote_copy(
                src_ref=g_ref, dst_ref=recv_ref.at[me], send_sem=send_sems.at[k - 1], recv_sem=recv_sems.at[k - 1],
                device_id=peer, device_id_type=MESH)
            cp.start()
            cps.append(cp)
        for cp in cps:
            cp.wait()
        g = recv_ref[0]
        for dev in range(1, 8):
            g = g + recv_ref[dev]
        delta, nm, nv = _adamw_math(w_ref[...], g, m_ref[...], v_ref[...])
        gsum_ref[...] = g
        delta_ref[...] = delta
        nm_ref[...] = nm
        nv_ref[...] = nv

    vm = pl.BlockSpec(memory_space=pltpu.VMEM)
    return pl.pallas_call(
        body, out_shape=[jax.ShapeDtypeStruct(shape, F32)] * 4, in_specs=[vm] * 4, out_specs=[vm] * 4,
        scratch_shapes=[pltpu.VMEM((8,) + shape, F32), pltpu.SemaphoreType.DMA((7,)), pltpu.SemaphoreType.DMA((7,))],
        name="small_allreduce_adamw")(gpart, w, m, v)


BIG = ("w_in", "w_up_a", "w_up_b", "w_o", "w_ff1", "w_ff2", "w_pe", "w_pg")
COL_SHARDED = ("w_in", "w_up_a", "w_up_b", "w_ff1", "w_pe")
ROW_SHARDED = ("w_o", "w_ff2", "w_pg")
SMALL_ROWS = 16


def _pack_small(g_mix, g_mlp, g_pe, g_final, sinks, rel_bias, loss=None):
    d = g_final.shape[0]
    row = lambda v: jnp.pad(v.reshape(1, -1), ((0, 0), (0, d - v.size)))
    rows = [g_mix, g_mlp, g_pe, g_final.reshape(1, d),
            jnp.zeros((1, d), F32) if loss is None else row(loss), row(sinks), row(rel_bias)]
    out = jnp.concatenate(rows, axis=0)
    return jnp.pad(out, ((0, SMALL_ROWS - out.shape[0]), (0, 0)))


def _unpack_small(a, sinks_shape, rel_shape):
    return (a[0:2], a[2:4], a[4:6], a[6], a[8, :sinks_shape[0] * sinks_shape[1]].reshape(sinks_shape),
            a[9, :rel_shape[0] * rel_shape[1]].reshape(rel_shape))


def kernel(x, p, w_in, w_up_a, w_up_b, w_o, w_ff1, w_ff2, w_pe, w_pg, g_mix, g_mlp, g_pe, g_final, sinks, rel_bias, loss_target, m_w_in, m_w_up_a, m_w_up_b, m_w_o, m_w_ff1, m_w_ff2, m_w_pe, m_w_pg, m_g_mix, m_g_mlp, m_g_pe, m_g_final, m_sinks, m_rel_bias, v_w_in, v_w_up_a, v_w_up_b, v_w_o, v_w_ff1, v_w_ff2, v_w_pe, v_w_pg, v_g_mix, v_g_mlp, v_g_pe, v_g_final, v_sinks, v_rel_bias):
    depth = w_in.shape[0]
    assert depth == 2
    x0 = x[0]
    target = loss_target[0]
    d = x0.shape[1]
    wl = dict(w_in=w_in, w_up_a=w_up_a, w_up_b=w_up_b, w_o=w_o, w_ff1=w_ff1, w_ff2=w_ff2, w_pe=w_pe, w_pg=w_pg)
    ml = dict(w_in=m_w_in, w_up_a=m_w_up_a, w_up_b=m_w_up_b, w_o=m_w_o, w_ff1=m_w_ff1, w_ff2=m_w_ff2, w_pe=m_w_pe, w_pg=m_w_pg)
    vl = dict(w_in=v_w_in, w_up_a=v_w_up_a, w_up_b=v_w_up_b, w_o=v_w_o, w_ff1=v_w_ff1, w_ff2=v_w_ff2, w_pe=v_w_pe, w_pg=v_w_pg)
    c_idx = lax.axis_index("c").astype(jnp.int32)
    r_idx = (2 * lax.axis_index("x") + lax.axis_index("y")).astype(jnp.int32)
    cr = jnp.stack([c_idx, r_idx])

    wl["w_in"], ml["w_in"], vl["w_in"] = (jnp.swapaxes(a, 1, 2) for a in (w_in, m_w_in, v_w_in))

    bufs = {}
    for n in BIG:
        k, nn = wl[n].shape[1:]
        for l, b in enumerate(_cast_bf16(wl[n], r_idx.reshape(1), "cast_" + n)):
            bufs[n, l] = b.reshape(4, 2, k // 2, nn)

    def gather(keys, run):
        for key, b in zip(keys, run(_all_gather([bufs[key] for key in keys]))):
            bufs[key] = b

    def gathered(n, l):
        _, _, k2, nn = bufs[n, l].shape
        if n in ROW_SHARDED or n == "w_in":
            return bufs[n, l].reshape(8 * k2, nn)
        return bufs[n, l].reshape(4, 2 * k2, nn)

    gather([("w_in", 0)], lambda ex: _run_exchange("all_gather_first", ex))

    full = {n: [None] * depth for n in BIG}
    saved = []
    xi = x0
    for i in range(depth):
        st = dict(x0=xi)
        gm = g_mix[i].reshape(1, d)
        full["w_in"][i] = gathered("w_in", i)
        st["h1"], st["qkv"], st["gates"] = _inproj_fwd(xi, gm, full["w_in"][i], f"inproj_fwd_{i}")

        def attend(ex):
            st["oa"], st["lt"], st["nb"], filled = _sb_fwd(st["qkv"], f"sb_fwd_{i}", ex)
            return filled

        gather([(n, i) for n in BIG if n != "w_in"] + ([("w_in", 1)] if i == 0 else []), attend)
        for n in BIG:
            if n != "w_in":
                full[n][i] = gathered(n, i)
        st["ob"] = _swa_fwd(st["qkv"], sinks[i], rel_bias, f"swa_fwd_{i}")
        st["m"], st["x1"] = _mixer_fwd(st["oa"], st["ob"], st["gates"], xi, full["w_up_a"][i], full["w_up_b"][i],
                                       full["w_o"][i], f"mixer_fwd_{i}")
        st["h2"], st["u"], st["a"] = _ff1_fwd(st["x1"], g_mlp[i].reshape(1, d), full["w_ff1"][i], f"ff1_fwd_{i}")
        st["x2"] = _ff2_fwd(st["a"], st["x1"], full["w_ff2"][i], f"ff2_fwd_{i}")
        st["pb"], st["h3"], st["pe"], st["gt"], xi = _ple_fwd(p[i, 0], st["x2"], g_pe[i].reshape(1, d),
                                                            full["w_pe"][i], full["w_pg"][i], f"ple_fwd_{i}")
        saved.append(st)

    dx, dg_final, loss_part = _loss_bwd(xi, target, g_final.reshape(1, d), "loss_bwd")
    gw = {n: [None] * depth for n in BIG}
    reduced = {}

    def scatter_front(keys, tag):
        tensors = []
        for n, l in keys:
            g = gw[n][l]
            if n in ROW_SHARDED:
                ka, nb = g.shape[1:]
                g = g.reshape(4, ka // 4, nb)
            _, k, nn = g.shape
            tensors.append(g.reshape(4, 2, k // 2, nn))
        from_sibling = _rs_to_sibling(tensors, f"rs_to_sibling_{tag}")
        added = [_add_half(g, r, cr, f"add_half_{n}_{l}") for (n, l), g, r in zip(keys, tensors, from_sibling)]
        return _rs_to_chips([a[0] for a in added], [a[1] for a in added])

    def scatter_back(keys, parts, tag):
        halves = [_sum4(pc, cr, f"sum4_{n}_{l}") for (n, l), pc in zip(keys, parts)]
        for key, both in zip(keys, _exchange_halves(halves, f"exchange_halves_{tag}")):
            reduced[key] = both

    dg_mix, dg_mlp, dg_pe, dsinks = [None] * depth, [None] * depth, [None] * depth, [None] * depth
    drel = jnp.zeros((SW_HEADS, LANES), F32)
    for i in reversed(range(depth)):
        st = saved[i]
        dpe, dgt, dx2, dg_pe[i] = _ple_bwd(dx, st["pe"], st["gt"], st["x2"], g_pe[i].reshape(1, d),
                                           full["w_pg"][i], f"ple_bwd_{i}")
        gw["w_pe"][i] = _mm_tn(st["pb"], dpe, f"dw_pe_{i}", 4)
        gw["w_pg"][i] = _mm_tn(st["h3"], dgt, f"dw_pg_{i}")
        du, dx2b = _ff2_bwd(dx2, st["u"], full["w_ff2"][i], f"ff2_bwd_{i}")
        gw["w_ff2"][i] = _mm_tn(st["a"], dx2b, f"dw_ff2_{i}")
        gw["w_ff1"][i] = _mm_tn(st["h2"], du, f"dw_ff1_{i}", 4)
        dx1, dx1b, dg_mlp[i] = _ff1_bwd(du, dx2, st["x1"], g_mlp[i].reshape(1, d), full["w_ff1"][i], f"ff1_bwd_{i}")
        gw["w_o"][i] = _mm_tn(st["m"], dx1b, f"dw_o_{i}")
        dya, dyb, dgates, doa, dob = _mixer_bwd(dx1b, st["gates"], st["oa"], st["ob"], full["w_o"][i],
                                                full["w_up_a"][i], full["w_up_b"][i], f"mixer_bwd_{i}")
        gw["w_up_a"][i] = _mm_tn(st["oa"], dya, f"dw_up_a_{i}", 4)
        gw["w_up_b"][i] = _mm_tn(st["ob"], dyb, f"dw_up_b_{i}", 4)
        keys = [(n, i) for n in BIG if n != "w_in"] + ([("w_in", 1)] if i == 0 else [])
        dqa, dka, dva, parts = _sb_bwd(st["qkv"], st["lt"], st["nb"], doa, f"sb_bwd_{i}", scatter_front(keys, i))
        scatter_back(keys, parts, i)
        dqb, dkb, dvb, dsk, drl = _swa_bwd(st["qkv"], st["ob"], dob, sinks[i], rel_bias, f"swa_bwd_{i}")
        dsinks[i] = dsk[:, 0]
        drel = drel + drl
        dqkv = jnp.concatenate([dqa, dka.astype(BF16), dva.astype(BF16), dqb, dkb.astype(BF16), dvb.astype(BF16)],
                               axis=1)
        dw_in_t = _mm_tn_pieces([dqkv, dgates], st["h1"], f"dw_in_{i}")
        gw["w_in"][i] = dw_in_t.reshape(4, dw_in_t.shape[0] // 4, d)
        dx, dg_mix[i] = _inproj_bwd(dqkv, dgates, dx1, st["x0"], g_mix[i].reshape(1, d), full["w_in"][i],
                                    f"inproj_bwd_{i}")
    grad_x = dx[None]

    keys = [("w_in", 0)]
    scatter_back(keys, _run_exchange("rs_to_chips_last", scatter_front(keys, "last")), "last")
    outs = {}
    for n in BIG:
        g0, g1 = (reduced[n, l].reshape(wl[n].shape[1:]) for l in range(depth))
        outs[n] = _adamw(wl[n], ml[n], vl[n], g0, g1, "adamw_" + n)
    outs["w_in"] = [jnp.swapaxes(a, 1, 2) for a in outs["w_in"]]

    drel_bias = drel[:, :N_BUCKETS].T
    gsmall = _pack_small(jnp.concatenate(dg_mix, 0), jnp.concatenate(dg_mlp, 0), jnp.concatenate(dg_pe, 0),
                         dg_final[0], jnp.stack(dsinks), drel_bias, loss_part[0, :1])
    wsmall = _pack_small(g_mix, g_mlp, g_pe, g_final, sinks, rel_bias)
    msmall = _pack_small(m_g_mix, m_g_mlp, m_g_pe, m_g_final, m_sinks, m_rel_bias)
    vsmall = _pack_small(v_g_mix, v_g_mlp, v_g_pe, v_g_final, v_sinks, v_rel_bias)
    small = _small_allreduce_adamw(gsmall, wsmall, msmall, vsmall)
    loss = small[0][7, 0]
    small = [_unpack_small(a, sinks.shape, rel_bias.shape) for a in small]

    result = [loss, grad_x]
    for kind in range(4):
        result += [outs[n][kind] for n in BIG]
        result += list(small[kind])
    return tuple(result)
```

```python
import functools
import math
from typing import Callable, NamedTuple

import numpy as np
import jax
import jax.numpy as jnp
from jax import lax
from jax.experimental import pallas as pl
from jax.experimental.pallas import tpu as pltpu

F32 = jnp.float32
BF16 = jnp.bfloat16
MESH = pl.DeviceIdType.MESH

HEAD_DIM = 64
SB_HEADS = 8
SW_HEADS = 8
SW_KV_HEADS = 2
WINDOW = 128
N_BUCKETS = 32
MAX_DISTANCE = 128
EPS = 1e-6
SB_W = SB_HEADS * HEAD_DIM
SW_QW = SW_HEADS * HEAD_DIM
SW_KVW = SW_KV_HEADS * HEAD_DIM
QKV_W = 3 * SB_W + SW_QW + 2 * SW_KVW
SCALE = HEAD_DIM ** -0.5
LANES = 128
TQ = 128
BK = 128
NEG = -1e30
SB_EXHAUSTED = -106.0

ADAM_LR = 0.001
ADAM_B1 = 0.9
ADAM_B2 = 0.999
ADAM_EPS = 1e-08
ADAM_WD = 0.01
ADAM_STEP = 10

VMEM_LIMIT = 56 * 1024 * 1024


def _dot(a, b):
    return jnp.dot(a, b, preferred_element_type=F32)


def _dot_nt(a, b):
    return lax.dot_general(a, b, (((1,), (1,)), ((), ())), preferred_element_type=F32)


def _dot_tn(a, b):
    return lax.dot_general(a, b, (((0,), (0,)), ((), ())), preferred_element_type=F32)


def _sum_all(x):
    return jnp.sum(jnp.sum(x, axis=1, keepdims=True), axis=0, keepdims=True)


def _sigmoid(x):
    return 1.0 / (1.0 + jnp.exp(-x))


def _rms(x, g):
    r = lax.rsqrt(jnp.mean(x * x, axis=-1, keepdims=True) + EPS)
    return (x * r) * g


def _rms_bwd(dy, x, g):
    r = lax.rsqrt(jnp.mean(x * x, axis=-1, keepdims=True) + EPS)
    n = x * r
    dg = jnp.sum(dy * n, axis=0, keepdims=True)
    dn = dy * g
    dx = r * (dn - n * jnp.mean(dn * n, axis=-1, keepdims=True))
    return dx, dg


def _params(n_axes):
    return pltpu.CompilerParams(dimension_semantics=("arbitrary",) * n_axes, vmem_limit_bytes=VMEM_LIMIT)


def _rowcall(name, body, row_ins, const_ins, row_outs, acc_outs=(), tm=512, side=None):
    s = row_ins[0].shape[0]
    assert s % tm == 0
    in_specs = [pl.BlockSpec((tm, a.shape[1]), lambda i: (i, 0)) for a in row_ins]
    in_specs += [pl.BlockSpec(a.shape, functools.partial(lambda i, nd: (0,) * nd, nd=a.ndim)) for a in const_ins]
    out_shape = [jax.ShapeDtypeStruct((s, c), dt) for c, dt in row_outs]
    out_specs = [pl.BlockSpec((tm, c), lambda i: (i, 0)) for c, _ in row_outs]
    out_shape += [jax.ShapeDtypeStruct(sh, dt) for sh, dt in acc_outs]
    out_specs += [pl.BlockSpec(sh, functools.partial(lambda i, nd: (0,) * nd, nd=len(sh))) for sh, _ in acc_outs]
    if side is None:
        return pl.pallas_call(body, out_shape=out_shape, grid=(s // tm,), in_specs=in_specs, out_specs=out_specs,
                              compiler_params=_params(1), name=name)(*row_ins, *const_ins)
    n_in, n_out = len(in_specs), len(out_specs)

    def with_side(*refs):
        side_in, outs, side_out, sems = _side_refs(side, refs[n_in:], n_out)

        @pl.when(pl.program_id(0) == 0)
        def _():
            side.start(side_in, side_out, *sems)

        body(*refs[:n_in], *outs)

        @pl.when(pl.program_id(0) == s // tm - 1)
        def _():
            side.finish(side_in, side_out, *sems)

    s_in, s_shape, s_out, s_alias, s_sems = _side_specs(side, n_in, n_out)
    outs = pl.pallas_call(with_side, out_shape=out_shape + s_shape, grid=(s // tm,), in_specs=in_specs + s_in,
                          out_specs=out_specs + s_out, input_output_aliases=s_alias, scratch_shapes=s_sems,
                          compiler_params=_params(1), name=name)(*row_ins, *const_ins, *side.arrays)
    return outs[:n_out], outs[n_out:]


def _dot_cols(a, w_ref):
    return jnp.concatenate([_dot(a, w_ref[r]) for r in range(w_ref.shape[0])], axis=1)


def _dot_cols_t(a, w_ref):
    n = w_ref.shape[2]
    out = _dot_nt(a[:, :n], w_ref[0])
    for r in range(1, w_ref.shape[0]):
        out = out + _dot_nt(a[:, r * n:(r + 1) * n], w_ref[r])
    return out


def _inproj_fwd(x, g, wt, name):
    d = x.shape[1]

    def body(x_ref, g_ref, w_ref, h_ref, qkv_ref, gate_ref):
        hb = _rms(x_ref[...], g_ref[...]).astype(BF16)
        h_ref[...] = hb
        qkv_ref[...] = _dot_nt(hb, w_ref[:QKV_W, :]).astype(BF16)
        gate_ref[...] = _dot_nt(hb, w_ref[QKV_W:, :])

    return _rowcall(name, body, [x], [g, wt], [(d, BF16), (QKV_W, BF16), (2 * d, F32)])


def _mixer_fwd(oa, ob, gates, x, wua, wub, wo, name):
    d = x.shape[1]

    def body(oa_ref, ob_ref, gate_ref, x_ref, wua_ref, wub_ref, wo_ref, m_ref, x1_ref):
        ya = _dot_cols(oa_ref[...], wua_ref)
        yb = _dot_cols(ob_ref[...], wub_ref)
        m = _sigmoid(gate_ref[:, :d]) * ya + _sigmoid(gate_ref[:, d:]) * yb
        mb = m.astype(BF16)
        m_ref[...] = mb
        x1_ref[...] = x_ref[...] + _dot(mb, wo_ref[...])

    return _rowcall(name, body, [oa, ob, gates, x], [wua, wub, wo], [(d, BF16), (d, F32)])


def _ff1_fwd(x1, g, w1, name):
    _, d, nq = w1.shape
    dff = 4 * nq

    def body(x_ref, g_ref, w_ref, h_ref, u_ref, a_ref):
        hb = _rms(x_ref[...], g_ref[...]).astype(BF16)
        h_ref[...] = hb
        u = _dot_cols(hb, w_ref)
        u_ref[...] = u
        a_ref[...] = jnp.square(jnp.maximum(u, 0.0)).astype(BF16)

    return _rowcall(name, body, [x1], [g, w1], [(d, BF16), (dff, F32), (dff, BF16)])


def _ff2_fwd(a, x1, w2, name):
    d = x1.shape[1]

    def body(a_ref, x_ref, w_ref, o_ref):
        o_ref[...] = x_ref[...] + _dot(a_ref[...], w_ref[...])

    return _rowcall(name, body, [a, x1], [w2], [(d, F32)])[0]


def _ple_fwd(p, x2, g, wpe, wpg, name):
    d = x2.shape[1]

    def body(p_ref, x_ref, g_ref, wpe_ref, wpg_ref, pb_ref, h_ref, pe_ref, gt_ref, x3_ref):
        pb = p_ref[...].astype(BF16)
        pb_ref[...] = pb
        pe = _dot_cols(pb, wpe_ref)
        x = x_ref[...]
        hb = _rms(x, g_ref[...]).astype(BF16)
        h_ref[...] = hb
        gt = _dot(hb, wpg_ref[...])
        pe_ref[...] = pe
        gt_ref[...] = gt
        x3_ref[...] = x + pe * _sigmoid(gt)

    return _rowcall(name, body, [p, x2], [g, wpe, wpg],
                    [(p.shape[1], BF16), (d, BF16), (d, F32), (d, F32), (d, F32)])


def _pair_stack(t, lane):
    zero = jnp.zeros_like(t)
    return jnp.concatenate([jnp.where(lane < HEAD_DIM, t, zero), jnp.where(lane >= HEAD_DIM, t, zero)], axis=0)


def _sb_block_consts():
    jj = lax.broadcasted_iota(jnp.int32, (BK, 2 * BK), 0)
    ss = lax.broadcasted_iota(jnp.int32, (BK, 2 * BK), 1)
    suffix = jnp.where((ss >= BK) | (jj > ss), 1.0, 0.0).astype(BF16)
    prefix_incl = jnp.where((ss >= BK) | (jj <= ss), 1.0, 0.0).astype(BF16)
    prefix_excl = jnp.where((ss >= BK) | (jj < ss), 1.0, 0.0).astype(BF16)
    return suffix, prefix_incl, prefix_excl


def _sb_rel():
    row = lax.broadcasted_iota(jnp.int32, (2 * TQ, BK), 0)
    row = jnp.where(row >= TQ, row - TQ, row)
    col = lax.broadcasted_iota(jnp.int32, (2 * TQ, BK), 1)
    return col - row


def _split_dot(x, m01):
    hi = x.astype(BF16)
    lo = (x - hi.astype(F32)).astype(BF16)
    return _dot(hi, m01) + _dot(lo, m01)


def _sb_scores(qs, k, mask):
    z = _dot_nt(qs, k) * SCALE
    lb = jnp.minimum(z, 0.0) - jnp.log1p(jnp.exp(-jnp.abs(z)))
    lm = lb - z
    return lb, lm if mask is None else jnp.where(mask, lm, 0.0)


SB_STRAIGHT = 3
SB_WIDE = SB_STRAIGHT * BK
SB_QB = 2
SWA_QB = 4


def _sb_wide_consts():
    j = np.arange(SB_WIDE)[:, None]
    s = np.arange(SB_WIDE)[None, :]
    ones = np.ones((SB_WIDE, BK), np.float32)
    as_bf16 = lambda m: jnp.asarray(m.astype(np.float32), dtype=BF16)
    return (as_bf16(np.concatenate([j > s, ones], axis=1)), as_bf16(np.concatenate([j <= s, ones], axis=1)),
            as_bf16(j < s))


def _side_refs(ex, rest, n_out):
    n_in = len(ex.arrays) if ex else 0
    n_alias = len(ex.aliased) if ex else 0
    ins, rest = rest[:n_in], rest[n_in:]
    outs, rest = rest[:n_out], rest[n_out:]
    return ins, outs, rest[:n_alias], rest[n_alias:]


def _side_specs(ex, n_in, n_out):
    if ex is None:
        return [], [], [], {}, []
    any_spec = pl.BlockSpec(memory_space=pl.ANY)
    return ([any_spec] * len(ex.arrays),
            [jax.ShapeDtypeStruct(ex.arrays[a].shape, ex.arrays[a].dtype) for a in ex.aliased],
            [any_spec] * len(ex.aliased), {n_in + a: n_out + o for o, a in enumerate(ex.aliased)},
            [pltpu.SemaphoreType.DMA(ex.sems), pltpu.SemaphoreType.DMA(ex.sems)])


def _sb_fwd(qkv, name, side=None):
    s = qkv.shape[0]
    nq = s // TQ
    npair = SB_W // LANES
    sufw = _sb_wide_consts()[0]

    def body(q_ref, k_ref, v_ref, sufw_ref, *rest):
        side_in, (o_ref, lt_ref, nb_ref), side_out, scratch = _side_refs(side, rest, 3)
        cf_ref, acc_ref = scratch[:2]
        step_id = pl.program_id(1)
        if side is not None:
            @pl.when((pl.program_id(0) == 0) & (step_id == 0))
            def _():
                side.start(side_in, side_out, *scratch[2:])
        lane = lax.broadcasted_iota(jnp.int32, (TQ, LANES), 1)
        rel = _sb_rel()
        blocks = [step_id * SB_QB + b for b in range(SB_QB)]
        qs = [_pair_stack(q_ref[b * TQ:(b + 1) * TQ, :], lane) for b in range(SB_QB)]

        straight = blocks[0] >= SB_STRAIGHT - 1

        @pl.when(straight)
        def _():
            for b, i in enumerate(blocks):
                w0 = pl.multiple_of((i - (SB_STRAIGHT - 1)) * BK, BK)
                kw = k_ref[pl.ds(w0, SB_WIDE), :]
                lb, lm = _sb_scores(qs[b], kw, None)
                own = rel < 0
                past = SB_WIDE - BK
                lm = jnp.concatenate([lm[:, :past], jnp.where(own, lm[:, past:], 0.0)], axis=1)
                cs = _split_dot(lm, sufw_ref[...])
                a = jnp.exp(lb + cs[:, :SB_WIDE])
                a = jnp.concatenate([a[:, :past], jnp.where(own, a[:, past:], 0.0)], axis=1)
                acc_ref[b] = _dot(a.astype(BF16), v_ref[pl.ds(w0, SB_WIDE), :])
                cf_ref[b] = cs[:, SB_WIDE:]

        @pl.when(jnp.logical_not(straight))
        def _():
            cf_ref[...] = jnp.zeros_like(cf_ref)
            acc_ref[...] = jnp.zeros_like(acc_ref)

        for b, i in enumerate(blocks):
            q0 = i * TQ

            def more(c, i=i):
                return (c[0] <= i) & (c[1] > SB_EXHAUSTED)

            def step(c, b=b, i=i, q0=q0):
                k0 = pl.multiple_of((i - c[0]) * BK, BK)
                k = k_ref[pl.ds(k0, BK), :]
                v = v_ref[pl.ds(k0, BK), :]
                mask = rel < (q0 - k0)
                lb, lm = _sb_scores(qs[b], k, mask)
                cs = _split_dot(lm, _sb_block_consts()[0])
                a = jnp.where(mask, jnp.exp(lb + cs[:, :BK] + cf_ref[b]), 0.0)
                acc_ref[b] += _dot(a.astype(BF16), v)
                cf = cf_ref[b] + cs[:, BK:]
                cf_ref[b] = cf
                return c[0] + 1, jnp.max(cf)

            n_blocks, _ = lax.while_loop(
                more, step, (jnp.where(straight, SB_STRAIGHT, 0).astype(jnp.int32), jnp.max(cf_ref[b])))
            o_ref[b * TQ:(b + 1) * TQ, :] = jnp.where(lane < HEAD_DIM, acc_ref[b, :TQ, :],
                                                     acc_ref[b, TQ:, :]).astype(BF16)
            lt_ref[b] = cf_ref[b]
            nb_ref[b] = jnp.full(nb_ref.shape[1:], n_blocks, F32)
        if side is not None:
            @pl.when((pl.program_id(0) == npair - 1) & (step_id == nq // SB_QB - 1))
            def _():
                side.finish(side_in, side_out, *scratch[2:])

    s_in, s_shape, s_out, s_alias, s_sems = _side_specs(side, 4, 3)
    outs = pl.pallas_call(
        body,
        out_shape=[jax.ShapeDtypeStruct((s, SB_W), BF16), jax.ShapeDtypeStruct((npair, nq, 2 * TQ, BK), F32),
                   jax.ShapeDtypeStruct((npair, nq, 8, LANES), F32)] + s_shape,
        grid=(npair, nq // SB_QB),
        in_specs=[pl.BlockSpec((SB_QB * TQ, LANES), lambda j, i: (i, j)),
                  pl.BlockSpec((s, LANES), lambda j, i: (0, npair + j)),
                  pl.BlockSpec((s, LANES), lambda j, i: (0, 2 * npair + j)),
                  pl.BlockSpec(sufw.shape, lambda j, i: (0, 0))] + s_in,
        out_specs=[pl.BlockSpec((SB_QB * TQ, LANES), lambda j, i: (i, j)),
                   pl.BlockSpec((None, SB_QB, 2 * TQ, BK), lambda j, i: (j, i, 0, 0)),
                   pl.BlockSpec((None, SB_QB, 8, LANES), lambda j, i: (j, i, 0, 0))] + s_out,
        input_output_aliases=s_alias,
        scratch_shapes=[pltpu.VMEM((SB_QB, 2 * TQ, BK), F32), pltpu.VMEM((SB_QB, 2 * TQ, LANES), F32)] + s_sems,
        compiler_params=_params(2), name=name)(qkv, qkv, qkv, sufw, *(side.arrays if side else ()))
    return outs[0], outs[1], outs[2], outs[3:]


def _sb_bwd(qkv, lt, nb, doa, name, side=None):
    s = qkv.shape[0]
    nq = s // TQ
    npair = SB_W // LANES
    _, prew, prexw = _sb_wide_consts()

    def body(q_ref, k_ref, v_ref, lt_ref, nb_ref, do_ref, prew_ref, prexw_ref, *rest):
        side_in, (dq_ref, dk_ref, dv_ref), side_out, scratch = _side_refs(side, rest, 3)
        cp_ref, ce_ref, dqa_ref = scratch[:3]
        step_id = pl.program_id(1)
        if side is not None:
            @pl.when((pl.program_id(0) == 0) & (step_id == 0))
            def _():
                side.start(side_in, side_out, *scratch[3:])
        lane = lax.broadcasted_iota(jnp.int32, (TQ, LANES), 1)
        rel = _sb_rel()
        blocks = [step_id * SB_QB + b for b in range(SB_QB)]
        rows = [slice(b * TQ, (b + 1) * TQ) for b in range(SB_QB)]
        qs = [_pair_stack(q_ref[rows[b], :], lane) for b in range(SB_QB)]
        dos = [_pair_stack(do_ref[rows[b], :], lane) for b in range(SB_QB)]
        n_blocks = [jnp.clip(jnp.max(nb_ref[b]).astype(jnp.int32), 1, i + 1) for b, i in enumerate(blocks)]
        first = [i + 1 - n for i, n in zip(blocks, n_blocks)]

        @pl.when(step_id == 0)
        def _():
            dk_ref[...] = jnp.zeros_like(dk_ref)
            dv_ref[...] = jnp.zeros_like(dv_ref)

        straight = n_blocks[0] == SB_STRAIGHT
        for n in n_blocks[1:]:
            straight = straight & (n == SB_STRAIGHT)

        @pl.when(straight)
        def _():
            for b in range(SB_QB):
                w0 = pl.multiple_of(first[b] * BK, BK)
                kw = k_ref[pl.ds(w0, SB_WIDE), :]
                vw = v_ref[pl.ds(w0, SB_WIDE), :]
                lb, lm = _sb_scores(qs[b], kw, None)
                own = rel < 0
                past = SB_WIDE - BK
                on_past_keys = lambda t: jnp.concatenate([t[:, :past], jnp.where(own, t[:, past:], 0.0)], axis=1)
                lm = on_past_keys(lm)
                cs = _split_dot(lm, prew_ref[...])
                lt = lt_ref[b]
                a = on_past_keys(jnp.exp(lb + (jnp.concatenate([lt] * SB_STRAIGHT, axis=1) - cs[:, :SB_WIDE])))
                e = a * _dot_nt(dos[b], vw)
                big_e = _split_dot(e, prexw_ref[...])
                dz = (on_past_keys(e - jnp.exp(lb) * (e + big_e)) * SCALE).astype(BF16)
                dk_ref[pl.ds(w0, SB_WIDE), :] += _dot_tn(dz, qs[b])
                dv_ref[pl.ds(w0, SB_WIDE), :] += _dot_tn(a.astype(BF16), dos[b])
                dqa_ref[b] = _dot(dz, kw)

        @pl.when(jnp.logical_not(straight))
        def _():
            cp_ref[...] = jnp.zeros_like(cp_ref)
            ce_ref[...] = jnp.zeros_like(ce_ref)
            dqa_ref[...] = jnp.zeros_like(dqa_ref)
            for b, i in enumerate(blocks):
                q0 = i * TQ

                def step(it, carry, b=b, q0=q0):
                    k0 = pl.multiple_of((first[b] + it) * BK, BK)
                    k = k_ref[pl.ds(k0, BK), :]
                    v = v_ref[pl.ds(k0, BK), :]
                    mask = rel < (q0 - k0)
                    _, prefix_incl, prefix_excl = _sb_block_consts()
                    lb, lm = _sb_scores(qs[b], k, mask)
                    cs = _split_dot(lm, prefix_incl)
                    a = jnp.where(mask, jnp.exp(lb + (lt_ref[b] - (cs[:, :BK] + cp_ref[b]))), 0.0)
                    e = a * _dot_nt(dos[b], v)
                    ce = _split_dot(e, prefix_excl)
                    big_e = ce[:, :BK] + ce_ref[b]
                    dz = (jnp.where(mask, e - jnp.exp(lb) * (e + big_e), 0.0) * SCALE).astype(BF16)
                    dk_ref[pl.ds(k0, BK), :] += _dot_tn(dz, qs[b])
                    dv_ref[pl.ds(k0, BK), :] += _dot_tn(a.astype(BF16), dos[b])
                    dqa_ref[b] += _dot(dz, k)
                    cp_ref[b] += cs[:, BK:]
                    ce_ref[b] += ce[:, BK:]
                    return carry

                lax.fori_loop(0, n_blocks[b], step, 0)

        for b in range(SB_QB):
            dq_ref[rows[b], :] = jnp.where(lane < HEAD_DIM, dqa_ref[b, :TQ, :], dqa_ref[b, TQ:, :]).astype(BF16)
        if side is not None:
            @pl.when((pl.program_id(0) == npair - 1) & (step_id == nq // SB_QB - 1))
            def _():
                side.finish(side_in, side_out, *scratch[3:])

    s_in, s_shape, s_out, s_alias, s_sems = _side_specs(side, 8, 3)
    outs = pl.pallas_call(
        body,
        out_shape=[jax.ShapeDtypeStruct((s, SB_W), BF16), jax.ShapeDtypeStruct((s, SB_W), F32),
                   jax.ShapeDtypeStruct((s, SB_W), F32)] + s_shape,
        grid=(npair, nq // SB_QB),
        in_specs=[pl.BlockSpec((SB_QB * TQ, LANES), lambda j, i: (i, j)),
                  pl.BlockSpec((s, LANES), lambda j, i: (0, npair + j)),
                  pl.BlockSpec((s, LANES), lambda j, i: (0, 2 * npair + j)),
                  pl.BlockSpec((None, SB_QB, 2 * TQ, BK), lambda j, i: (j, i, 0, 0)),
                  pl.BlockSpec((None, SB_QB, 8, LANES), lambda j, i: (j, i, 0, 0)),
                  pl.BlockSpec((SB_QB * TQ, LANES), lambda j, i: (i, j)),
                  pl.BlockSpec(prew.shape, lambda j, i: (0, 0)),
                  pl.BlockSpec(prexw.shape, lambda j, i: (0, 0))] + s_in,
        out_specs=[pl.BlockSpec((SB_QB * TQ, LANES), lambda j, i: (i, j)),
                   pl.BlockSpec((s, LANES), lambda j, i: (0, j)),
                   pl.BlockSpec((s, LANES), lambda j, i: (0, j))] + s_out,
        input_output_aliases=s_alias,
        scratch_shapes=[pltpu.VMEM((SB_QB, 2 * TQ, BK), F32), pltpu.VMEM((SB_QB, 2 * TQ, BK), F32),
                        pltpu.VMEM((SB_QB, 2 * TQ, LANES), F32)] + s_sems,
        compiler_params=_params(2), name=name)(qkv, qkv, qkv, lt, nb, doa, prew, prexw,
                                               *(side.arrays if side else ()))
    return outs[0], outs[1], outs[2], outs[3:]


def _bucket_table():
    i = np.arange(TQ)[:, None]
    j = np.arange(2 * BK)[None, :]
    dist = np.maximum(TQ + i - j, 0)
    max_exact = N_BUCKETS // 2
    df = np.maximum(dist, 1).astype(np.float32)
    large = max_exact + (np.log(df / np.float32(max_exact)) / np.float32(math.log(MAX_DISTANCE / max_exact))
                         * np.float32(N_BUCKETS - max_exact)).astype(np.int32)
    large = np.minimum(large, N_BUCKETS - 1)
    return np.where(dist < max_exact, dist, large).astype(np.int32)


def _swa_align_in(t, lane, g):
    tf = t.astype(F32)
    tr = pltpu.roll(tf, HEAD_DIM, 1)
    gmask = (lane >= HEAD_DIM) == (g == 1)
    top = jnp.where(gmask, jnp.where(g == 0, tf, tr), 0.0)
    bot = jnp.where(gmask, jnp.where(g == 1, tf, tr), 0.0)
    return jnp.concatenate([top, bot], axis=0).astype(BF16)


def _swa_align_out(t, lane, g):
    top, bot = t[:TQ, :], t[TQ:, :]
    top = jnp.where(g == 0, top, pltpu.roll(top, HEAD_DIM, 1))
    bot = jnp.where(g == 1, bot, pltpu.roll(bot, HEAD_DIM, 1))
    return jnp.where(lane < HEAD_DIM, top, bot)


def _swa_bias(bias_ref, bucket_ref, rb_ref, j):
    for hh in range(2):
        def add(b, acc):
            return acc + jnp.where(bucket_ref[...] == b, rb_ref[b, 2 * j + hh], 0.0)
        bias_ref[hh * TQ:(hh + 1) * TQ, :] = lax.fori_loop(0, N_BUCKETS, add, jnp.zeros((TQ, 2 * BK), F32))


def _swa_probs(qs, k2, bias, sink_ref, i, j):
    s = _dot_nt(qs, k2) * SCALE + bias
    row = lax.broadcasted_iota(jnp.int32, (2 * TQ, 2 * BK), 0)
    col = lax.broadcasted_iota(jnp.int32, (2 * TQ, 2 * BK), 1)
    dist = TQ + jnp.where(row >= TQ, row - TQ, row) - col
    valid = (dist >= 0) & (dist < WINDOW) & ((col >= BK) | (i > 0))
    s = jnp.where(valid, s, NEG)
    row1 = lax.broadcasted_iota(jnp.int32, (2 * TQ, 1), 0)
    sink = jnp.where(row1 < TQ, sink_ref[2 * j], sink_ref[2 * j + 1])
    m = jnp.maximum(jnp.max(s, axis=1, keepdims=True), sink)
    e = jnp.exp(s - m)
    es = jnp.exp(sink - m)
    den = jnp.sum(e, axis=1, keepdims=True) + es
    return e / den, es / den


def _swa_kv(ref, i):
    prev = pl.multiple_of(jnp.maximum(i - 1, 0) * BK, BK)
    cur = pl.multiple_of(i * BK, BK)
    return jnp.concatenate([ref[pl.ds(prev, BK), :], ref[pl.ds(cur, BK), :]], axis=0), prev, cur


def _swa_fwd(qkv, sinks, rel_bias, name):
    s = qkv.shape[0]
    nq = s // TQ
    npair = SW_QW // LANES
    qcol = 3 * SB_W // LANES
    bucket = jnp.asarray(_bucket_table())

    def body(q_ref, k_ref, v_ref, bucket_ref, sink_ref, rb_ref, o_ref, bias_ref):
        j = pl.program_id(0)
        step = pl.program_id(1)
        g = j // 2
        lane = lax.broadcasted_iota(jnp.int32, (TQ, LANES), 1)

        @pl.when(step == 0)
        def _():
            _swa_bias(bias_ref, bucket_ref, rb_ref, j)

        for b in range(SWA_QB):
            i = step * SWA_QB + b
            rows = slice(b * TQ, (b + 1) * TQ)
            qs = _swa_align_in(q_ref[rows, :], lane, g)
            k2, _, _ = _swa_kv(k_ref, i)
            v2, _, _ = _swa_kv(v_ref, i)
            pr, _ = _swa_probs(qs, k2, bias_ref[...], sink_ref, i, j)
            o_ref[rows, :] = _swa_align_out(_dot(pr.astype(BF16), v2), lane, g).astype(BF16)

    assert nq % SWA_QB == 0
    return pl.pallas_call(
        body, out_shape=jax.ShapeDtypeStruct((s, SW_QW), BF16), grid=(npair, nq // SWA_QB),
        in_specs=[pl.BlockSpec((SWA_QB * TQ, LANES), lambda j, i: (i, qcol + j)),
                  pl.BlockSpec((s, LANES), lambda j, i: (0, qcol + npair)),
                  pl.BlockSpec((s, LANES), lambda j, i: (0, qcol + npair + 1)),
                  pl.BlockSpec((TQ, 2 * BK), lambda j, i: (0, 0)),
                  pl.BlockSpec(memory_space=pltpu.SMEM),
                  pl.BlockSpec(memory_space=pltpu.SMEM)],
        out_specs=pl.BlockSpec((SWA_QB * TQ, LANES), lambda j, i: (i, j)),
        scratch_shapes=[pltpu.VMEM((2 * TQ, 2 * BK), F32)],
        compiler_params=_params(2), name=name)(qkv, qkv, qkv, bucket, sinks, rel_bias)


def _swa_bwd(qkv, ob, dob, sinks, rel_bias, name):
    s = qkv.shape[0]
    nq = s // TQ
    npair = SW_QW // LANES
    qcol = 3 * SB_W // LANES
    bucket = jnp.asarray(_bucket_table())

    def body(q_ref, k_ref, v_ref, o_ref, do_ref, bucket_ref, sink_ref, rb_ref,
             dq_ref, dk_ref, dv_ref, dsink_ref, drel_ref, bias_ref, dsacc_ref):
        j = pl.program_id(0)
        step = pl.program_id(1)
        g = j // 2
        lane = lax.broadcasted_iota(jnp.int32, (TQ, LANES), 1)
        row8 = lax.broadcasted_iota(jnp.int32, (SW_HEADS, LANES), 0)
        lane8 = lax.broadcasted_iota(jnp.int32, (SW_HEADS, LANES), 1)

        @pl.when((step == 0) & (j == 0))
        def _():
            dk_ref[...] = jnp.zeros_like(dk_ref)
            dv_ref[...] = jnp.zeros_like(dv_ref)
            dsink_ref[...] = jnp.zeros_like(dsink_ref)
            drel_ref[...] = jnp.zeros_like(drel_ref)

        @pl.when(step == 0)
        def _():
            _swa_bias(bias_ref, bucket_ref, rb_ref, j)
            dsacc_ref[...] = jnp.zeros_like(dsacc_ref)

        ds_sum = jnp.zeros(dsacc_ref.shape, F32)
        dsink = jnp.zeros((SW_HEADS, LANES), F32)
        for b in range(SWA_QB):
            i = step * SWA_QB + b
            rows = slice(b * TQ, (b + 1) * TQ)
            qs = _swa_align_in(q_ref[rows, :], lane, g)
            do = do_ref[rows, :]
            dos = _swa_align_in(do, lane, g)
            dof = do.astype(F32) * o_ref[rows, :].astype(F32)
            d0 = jnp.sum(jnp.where(lane < HEAD_DIM, dof, 0.0), axis=1, keepdims=True)
            d1 = jnp.sum(jnp.where(lane >= HEAD_DIM, dof, 0.0), axis=1, keepdims=True)
            delta = jnp.concatenate([d0, d1], axis=0)
            k2, prev, cur = _swa_kv(k_ref, i)
            v2, _, _ = _swa_kv(v_ref, i)
            pr, psink = _swa_probs(qs, k2, bias_ref[...], sink_ref, i, j)
            ds = pr * (_dot_nt(dos, v2) - delta)
            ds_sum = ds_sum + ds
            sd = psink * delta
            ds0 = -jnp.sum(sd[:TQ, :], axis=0, keepdims=True)
            ds1 = -jnp.sum(sd[TQ:, :], axis=0, keepdims=True)
            dsink = dsink + jnp.where(row8 == 2 * j, ds0, jnp.where(row8 == 2 * j + 1, ds1, 0.0))
            dsb = ds.astype(BF16)
            dq_ref[rows, :] = _swa_align_out(_dot(dsb, k2) * SCALE, lane, g).astype(BF16)
            dk2 = _dot_tn(dsb, qs) * SCALE
            dv2 = _dot_tn(pr.astype(BF16), dos)
            dk_ref[pl.ds(prev, BK), :] += dk2[:BK, :]
            dk_ref[pl.ds(cur, BK), :] += dk2[BK:, :]
            dv_ref[pl.ds(prev, BK), :] += dv2[:BK, :]
            dv_ref[pl.ds(cur, BK), :] += dv2[BK:, :]
        dsacc_ref[...] += ds_sum
        dsink_ref[...] += dsink

        @pl.when(step == nq // SWA_QB - 1)
        def _():
            for hh in range(2):
                def red(b, acc):
                    val = _sum_all(jnp.where(bucket_ref[...] == b, dsacc_ref[hh * TQ:(hh + 1) * TQ, :], 0.0))
                    return jnp.where((row8 == 2 * j + hh) & (lane8 == b), val, acc)
                drel_ref[...] += lax.fori_loop(0, N_BUCKETS, red, jnp.zeros((SW_HEADS, LANES), F32))

    whole = lambda j, i: (0, 0)
    return pl.pallas_call(
        body,
        out_shape=[jax.ShapeDtypeStruct((s, SW_QW), BF16), jax.ShapeDtypeStruct((s, LANES), F32),
                   jax.ShapeDtypeStruct((s, LANES), F32), jax.ShapeDtypeStruct((SW_HEADS, LANES), F32),
                   jax.ShapeDtypeStruct((SW_HEADS, LANES), F32)],
        grid=(npair, nq // SWA_QB),
        in_specs=[pl.BlockSpec((SWA_QB * TQ, LANES), lambda j, i: (i, qcol + j)),
                  pl.BlockSpec((s, LANES), lambda j, i: (0, qcol + npair)),
                  pl.BlockSpec((s, LANES), lambda j, i: (0, qcol + npair + 1)),
                  pl.BlockSpec((SWA_QB * TQ, LANES), lambda j, i: (i, j)),
                  pl.BlockSpec((SWA_QB * TQ, LANES), lambda j, i: (i, j)),
                  pl.BlockSpec((TQ, 2 * BK), whole),
                  pl.BlockSpec(memory_space=pltpu.SMEM),
                  pl.BlockSpec(memory_space=pltpu.SMEM)],
        out_specs=[pl.BlockSpec((SWA_QB * TQ, LANES), lambda j, i: (i, j)),
                   pl.BlockSpec((s, LANES), whole), pl.BlockSpec((s, LANES), whole),
                   pl.BlockSpec((SW_HEADS, LANES), whole), pl.BlockSpec((SW_HEADS, LANES), whole)],
        scratch_shapes=[pltpu.VMEM((2 * TQ, 2 * BK), F32), pltpu.VMEM((2 * TQ, 2 * BK), F32)],
        compiler_params=_params(2), name=name)(qkv, qkv, qkv, ob, dob, bucket, sinks, rel_bias)


def _acc_init(i, *refs):
    @pl.when(i == 0)
    def _():
        for r in refs:
            r[...] = jnp.zeros_like(r)


def _loss_bwd(x3, target, g, name):
    d = x3.shape[1]

    def body(x_ref, t_ref, g_ref, dx_ref, dg_ref, loss_ref):
        _acc_init(pl.program_id(0), dg_ref, loss_ref)
        x = x_ref[...]
        gv = g_ref[...]
        diff = _rms(x, gv) - t_ref[...]
        loss_ref[...] += 0.5 * jnp.sum(jnp.mean(jnp.square(diff), axis=-1, keepdims=True), axis=0, keepdims=True)
        dx, dg = _rms_bwd(diff * (1.0 / d), x, gv)
        dx_ref[...] = dx
        dg_ref[...] += dg

    return _rowcall(name, body, [x3, target], [g], [(d, F32)], [((1, d), F32), ((1, LANES), F32)])


def _ple_bwd(dx3, pe, gt, x2, g, wpg, name):
    d = x2.shape[1]

    def body(dx3_ref, pe_ref, gt_ref, x_ref, g_ref, w_ref, dpe_ref, dgt_ref, dx2_ref, dg_ref):
        _acc_init(pl.program_id(0), dg_ref)
        dx3 = dx3_ref[...]
        sg = _sigmoid(gt_ref[...])
        dpe_ref[...] = (dx3 * sg).astype(BF16)
        dgt = (dx3 * pe_ref[...] * sg * (1.0 - sg)).astype(BF16)
        dgt_ref[...] = dgt
        dx, dg = _rms_bwd(_dot_nt(dgt, w_ref[...]), x_ref[...], g_ref[...])
        dx2_ref[...] = dx3 + dx
        dg_ref[...] += dg

    return _rowcall(name, body, [dx3, pe, gt, x2], [g, wpg], [(d, BF16), (d, BF16), (d, F32)], [((1, d), F32)])


def _ff2_bwd(dx2, u, w2, name):
    d = dx2.shape[1]
    dff = u.shape[1]

    def body(dx_ref, u_ref, w_ref, du_ref, dxb_ref):
        dxb = dx_ref[...].astype(BF16)
        dxb_ref[...] = dxb
        du_ref[...] = (_dot_nt(dxb, w_ref[...]) * (2.0 * jnp.maximum(u_ref[...], 0.0))).astype(BF16)

    return _rowcall(name, body, [dx2, u], [w2], [(dff, BF16), (d, BF16)])


def _ff1_bwd(du, dx2, x1, g, w1, name):
    d = x1.shape[1]

    def body(du_ref, dx2_ref, x_ref, g_ref, w_ref, dx1_ref, dx1b_ref, dg_ref):
        _acc_init(pl.program_id(0), dg_ref)
        dx, dg = _rms_bwd(_dot_cols_t(du_ref[...], w_ref), x_ref[...], g_ref[...])
        dx1 = dx2_ref[...] + dx
        dx1_ref[...] = dx1
        dx1b_ref[...] = dx1.astype(BF16)
        dg_ref[...] += dg

    return _rowcall(name, body, [du, dx2, x1], [g, w1], [(d, F32), (d, BF16)], [((1, d), F32)])


def _mixer_bwd(dx1b, gates, oa, ob, wo, wua, wub, name):
    d = dx1b.shape[1]

    def body(dx_ref, gate_ref, oa_ref, ob_ref, wo_ref, wua_ref, wub_ref,
             dya_ref, dyb_ref, dgate_ref, doa_ref, dob_ref):
        dm = _dot_nt(dx_ref[...], wo_ref[...])
        sa = _sigmoid(gate_ref[:, :d])
        sb = _sigmoid(gate_ref[:, d:])
        ya = _dot_cols(oa_ref[...], wua_ref)
        yb = _dot_cols(ob_ref[...], wub_ref)
        dya = (dm * sa).astype(BF16)
        dyb = (dm * sb).astype(BF16)
        dya_ref[...] = dya
        dyb_ref[...] = dyb
        dgate_ref[:, :d] = (dm * ya * sa * (1.0 - sa)).astype(BF16)
        dgate_ref[:, d:] = (dm * yb * sb * (1.0 - sb)).astype(BF16)
        doa_ref[...] = _dot_cols_t(dya, wua_ref).astype(BF16)
        dob_ref[...] = _dot_cols_t(dyb, wub_ref).astype(BF16)

    return _rowcall(name, body, [dx1b, gates, oa, ob], [wo, wua, wub],
                    [(d, BF16), (d, BF16), (2 * d, BF16), (SB_W, BF16), (SW_QW, BF16)])


def _inproj_bwd(dqkv, dgates, dx1, x, g, w, name, side=None):
    d = x.shape[1]

    def body(dqkv_ref, dgate_ref, dx1_ref, x_ref, g_ref, w_ref, dx_ref, dg_ref):
        _acc_init(pl.program_id(0), dg_ref)
        dh = _dot(dqkv_ref[...], w_ref[:QKV_W, :]) + _dot(dgate_ref[...], w_ref[QKV_W:, :])
        dx, dg = _rms_bwd(dh, x_ref[...], g_ref[...])
        dx_ref[...] = dx1_ref[...] + dx
        dg_ref[...] += dg

    return _rowcall(name, body, [dqkv, dgates, dx1, x], [g, w], [(d, F32)], [((1, d), F32)], side=side)


def _tile(n, cap):
    assert n % LANES == 0
    return max(t for t in range(LANES, min(n, cap) + 1, LANES) if n % t == 0)


def _mm_tn(a, b, name, nshard=1):
    s, ka = a.shape
    nb = b.shape[1]
    n = nb // nshard
    ta = _tile(ka, 512)
    tb = _tile(n, 1024)
    per = n // tb

    def body(a_ref, b_ref, o_ref):
        o_ref[...] = _dot_tn(a_ref[...].astype(BF16), b_ref[...].astype(BF16))

    return pl.pallas_call(
        body, out_shape=jax.ShapeDtypeStruct((nshard, ka, n), F32), grid=(nb // tb, ka // ta),
        in_specs=[pl.BlockSpec((s, ta), lambda jb, ia: (0, ia)), pl.BlockSpec((s, tb), lambda jb, ia: (0, jb))],
        out_specs=pl.BlockSpec((None, ta, tb), lambda jb, ia: (jb // per, ia, jb % per)),
        compiler_params=_params(2), name=name)(a, b)


def _mm_tn_pieces(pieces, b, name):
    s, nb = b.shape
    ta = 256
    tiles = [pc.shape[1] // ta for pc in pieces]
    assert all(pc.shape[1] % ta == 0 for pc in pieces)
    starts = [sum(tiles[:p]) for p in range(len(pieces))]
    tb = _tile(nb, 512)

    def body(*refs):
        a_refs, b_ref, o_ref = refs[:-2], refs[-2], refs[-1]
        ia = pl.program_id(1)
        a = a_refs[0][...]
        for p in range(1, len(pieces)):
            a = jnp.where(ia >= starts[p], a_refs[p][...], a)
        o_ref[...] = _dot_tn(a, b_ref[...])

    def piece_spec(p):
        return pl.BlockSpec((s, ta), lambda jb, ia: (0, jnp.clip(ia - starts[p], 0, tiles[p] - 1)))

    return pl.pallas_call(
        body, out_shape=jax.ShapeDtypeStruct((sum(tiles) * ta, nb), F32), grid=(nb // tb, sum(tiles)),
        in_specs=[piece_spec(p) for p in range(len(pieces))] + [pl.BlockSpec((s, tb), lambda jb, ia: (0, jb))],
        out_specs=pl.BlockSpec((ta, tb), lambda jb, ia: (ia, jb)),
        compiler_params=_params(2), name=name)(*pieces, b)


def _place():
    return lax.axis_index("x"), lax.axis_index("y"), lax.axis_index("c")


def _chip_peer(x, y, k):
    return (x ^ (k >> 1), y ^ (k & 1))


def _row_tile(k, cap=544):
    return max(t for t in range(32, min(k, cap) + 1, 32) if k % t == 0)


def _cast_bf16(w, r, name):
    l, k, n = w.shape
    assert l == 2
    tk = _row_tile(k)

    def body(r_ref, w_ref, o0_ref, o1_ref):
        o0_ref[...] = w_ref[0].astype(BF16)
        o1_ref[...] = w_ref[1].astype(BF16)

    out_spec = pl.BlockSpec((None, tk, n), lambda i, r_ref: (r_ref[0], i, 0))
    return pl.pallas_call(
        body, out_shape=[jax.ShapeDtypeStruct((4, k, n), BF16)] * 2,
        grid_spec=pltpu.PrefetchScalarGridSpec(
            num_scalar_prefetch=1, grid=(k // tk,),
            in_specs=[pl.BlockSpec((l, tk, n), lambda i, r_ref: (0, i, 0))],
            out_specs=[out_spec, out_spec]),
        compiler_params=_params(1), name=name)(r, w)


class _Exchange(NamedTuple):
    arrays: tuple
    aliased: tuple
    sems: tuple
    start: Callable
    finish: Callable


def _all_gather(bufs):
    nt = len(bufs)

    def ici(t, ins, outs, send_sems, recv_sems, x, y, c, r, k):
        return pltpu.make_async_remote_copy(
            src_ref=ins[t].at[r, c], dst_ref=outs[t].at[r, c], send_sem=send_sems.at[t, k - 1],
            recv_sem=recv_sems.at[t, k - 1], device_id=(*_chip_peer(x, y, k), c), device_id_type=MESH)

    def d2d(t, outs, send_sems, recv_sems, x, y, c, r, k, half):
        slab = outs[t].at[r ^ k, half]
        return pltpu.make_async_remote_copy(
            src_ref=slab, dst_ref=slab, send_sem=send_sems.at[t, 2 + k], recv_sem=recv_sems.at[t, 2 + k],
            device_id=(x, y, 1 - c), device_id_type=MESH)

    def start(ins, outs, send_sems, recv_sems):
        x, y, c = _place()
        r = 2 * x + y
        for t in range(nt):
            for k in (1, 2, 3):
                ici(t, ins, outs, send_sems, recv_sems, x, y, c, r, k).start()

    def finish(ins, outs, send_sems, recv_sems):
        x, y, c = _place()
        r = 2 * x + y
        for t in range(nt):
            for k in (1, 2, 3):
                slab = outs[t].at[r ^ k, c]
                pltpu.make_async_remote_copy(
                    src_ref=slab, dst_ref=slab, send_sem=send_sems.at[t, k - 1], recv_sem=recv_sems.at[t, k - 1],
                    device_id=(x, y, 1 - c), device_id_type=MESH).wait_recv()
                d2d(t, outs, send_sems, recv_sems, x, y, c, r, k, c).start()
        for t in range(nt):
            for k in (1, 2, 3):
                d2d(t, outs, send_sems, recv_sems, x, y, c, r, k, 1 - c).wait_recv()
        for t in range(nt):
            for k in (1, 2, 3):
                ici(t, ins, outs, send_sems, recv_sems, x, y, c, r, k).wait_send()
                d2d(t, outs, send_sems, recv_sems, x, y, c, r, k, c).wait_send()

    return _Exchange(tuple(bufs), tuple(range(nt)), (nt, 6), start, finish)


def _run_exchange(name, ex):
    n_in, n_out = len(ex.arrays), len(ex.aliased)

    def body(*refs):
        ins, outs = refs[:n_in], refs[n_in:n_in + n_out]
        send_sems, recv_sems = refs[n_in + n_out:]
        ex.start(ins, outs, send_sems, recv_sems)
        ex.finish(ins, outs, send_sems, recv_sems)

    any_spec = pl.BlockSpec(memory_space=pl.ANY)
    return pl.pallas_call(
        body, out_shape=[jax.ShapeDtypeStruct(ex.arrays[a].shape, ex.arrays[a].dtype) for a in ex.aliased],
        in_specs=[any_spec] * n_in, out_specs=[any_spec] * n_out,
        input_output_aliases={a: o for o, a in enumerate(ex.aliased)},
        scratch_shapes=[pltpu.SemaphoreType.DMA(ex.sems), pltpu.SemaphoreType.DMA(ex.sems)],
        name=name)(*ex.arrays)


def _rs_to_sibling(grads, name):
    nt = len(grads)

    def body(*refs):
        ins, outs = refs[:nt], refs[nt:2 * nt]
        send_sems, recv_sems = refs[2 * nt:]
        x, y, c = _place()
        sibling = (x, y, 1 - c)
        cps = []
        for t in range(nt):
            cp = pltpu.make_async_remote_copy(
                src_ref=ins[t].at[:, 1 - c], dst_ref=outs[t], send_sem=send_sems.at[t], recv_sem=recv_sems.at[t],
                device_id=sibling, device_id_type=MESH)
            cp.start()
            cps.append(cp)
        for cp in cps:
            cp.wait()

    any_spec = pl.BlockSpec(memory_space=pl.ANY)
    return pl.pallas_call(
        body, out_shape=[jax.ShapeDtypeStruct((4,) + g.shape[2:], F32) for g in grads],
        in_specs=[any_spec] * nt, out_specs=[any_spec] * nt,
        scratch_shapes=[pltpu.SemaphoreType.DMA((nt,)), pltpu.SemaphoreType.DMA((nt,))],
        name=name)(*grads)


def _add_half(g, recv, cr, name):
    _, _, k2, n = g.shape
    tk = _row_tile(k2)

    def body(cr_ref, g_ref, r_ref, sums_ref, mine_ref):
        val = (g_ref[...] + r_ref[...]).astype(BF16)
        sums_ref[...] = val

        @pl.when(pl.program_id(1) == cr_ref[1])
        def _():
            mine_ref[...] = val

    return pl.pallas_call(
        body, out_shape=[jax.ShapeDtypeStruct((4, k2, n), BF16)] * 2,
        grid_spec=pltpu.PrefetchScalarGridSpec(
            num_scalar_prefetch=1, grid=(k2 // tk, 4),
            in_specs=[pl.BlockSpec((None, None, tk, n), lambda i, q, cr_ref: (q, cr_ref[0], i, 0)),
                      pl.BlockSpec((None, tk, n), lambda i, q, cr_ref: (q, i, 0))],
            out_specs=[pl.BlockSpec((None, tk, n), lambda i, q, cr_ref: (q, i, 0)),
                       pl.BlockSpec((None, tk, n), lambda i, q, cr_ref: (cr_ref[1], i, 0))]),
        compiler_params=_params(2), name=name)(cr, g, recv)


def _rs_to_chips(sums, parts):
    nt = len(sums)

    def copies(ins, outs, send_sems, recv_sems):
        x, y, c = _place()
        r = 2 * x + y
        return [pltpu.make_async_remote_copy(
            src_ref=ins[t].at[r ^ k], dst_ref=outs[t].at[r], send_sem=send_sems.at[t, k - 1],
            recv_sem=recv_sems.at[t, k - 1], device_id=(*_chip_peer(x, y, k), c), device_id_type=MESH)
            for t in range(nt) for k in (1, 2, 3)]

    def start(ins, outs, send_sems, recv_sems):
        for cp in copies(ins, outs, send_sems, recv_sems):
            cp.start()

    def finish(ins, outs, send_sems, recv_sems):
        for cp in copies(ins, outs, send_sems, recv_sems):
            cp.wait()

    return _Exchange(tuple(sums) + tuple(parts), tuple(range(nt, 2 * nt)), (nt, 3), start, finish)


def _sum4(parts, cr, name):
    _, k2, n = parts.shape
    tk = _row_tile(k2)

    def body(cr_ref, p_ref, o_ref):
        p = p_ref[...].astype(F32)
        o_ref[...] = ((p[0] + p[1]) + p[2]) + p[3]

    return pl.pallas_call(
        body, out_shape=jax.ShapeDtypeStruct((2, k2, n), F32),
        grid_spec=pltpu.PrefetchScalarGridSpec(
            num_scalar_prefetch=1, grid=(k2 // tk,),
            in_specs=[pl.BlockSpec((4, tk, n), lambda i, cr_ref: (0, i, 0))],
            out_specs=pl.BlockSpec((None, tk, n), lambda i, cr_ref: (cr_ref[0], i, 0))),
        compiler_params=_params(1), name=name)(cr, parts)


def _exchange_halves(both, name):
    nt = len(both)

    def body(*refs):
        ins, outs = refs[:nt], refs[nt:2 * nt]
        send_sems, recv_sems = refs[2 * nt:]
        x, y, c = _place()
        cps = []
        for t in range(nt):
            cp = pltpu.make_async_remote_copy(
                src_ref=ins[t].at[c], dst_ref=outs[t].at[c], send_sem=send_sems.at[t], recv_sem=recv_sems.at[t],
                device_id=(x, y, 1 - c), device_id_type=MESH)
            cp.start()
            cps.append(cp)
        for cp in cps:
            cp.wait()

    any_spec = pl.BlockSpec(memory_space=pl.ANY)
    return pl.pallas_call(
        body, out_shape=[jax.ShapeDtypeStruct(b.shape, F32) for b in both],
        in_specs=[any_spec] * nt, out_specs=[any_spec] * nt,
        input_output_aliases={t: t for t in range(nt)},
        scratch_shapes=[pltpu.SemaphoreType.DMA((nt,)), pltpu.SemaphoreType.DMA((nt,))],
        name=name)(*both)


def _adamw_math(w, g, m, v):
    m = ADAM_B1 * m + (1.0 - ADAM_B1) * g
    v = ADAM_B2 * v + (1.0 - ADAM_B2) * jnp.square(g)
    m_hat = m / (1.0 - ADAM_B1 ** ADAM_STEP)
    v_hat = v / (1.0 - ADAM_B2 ** ADAM_STEP)
    delta = -ADAM_LR * (m_hat / (jnp.sqrt(v_hat) + ADAM_EPS) + ADAM_WD * w)
    return delta, m, v


def _adamw(w, m, v, g0, g1, name):
    _, k, n = w.shape
    tk = _row_tile(k)
    nk = k // tk

    def body(w_ref, m_ref, v_ref, g0_ref, g1_ref, grad_ref, delta_ref, nm_ref, nv_ref):
        g = jnp.where(pl.program_id(0) == 0, g0_ref[...], g1_ref[...])
        delta, nm, nv = _adamw_math(w_ref[...], g, m_ref[...], v_ref[...])
        grad_ref[...] = g
        delta_ref[...] = delta
        nm_ref[...] = nm
        nv_ref[...] = nv

    lay = pl.BlockSpec((None, tk, n), lambda a, i: (a, i, 0))
    g0_spec = pl.BlockSpec((tk, n), lambda a, i: (jnp.where(a == 0, i, nk - 1), 0))
    g1_spec = pl.BlockSpec((tk, n), lambda a, i: (jnp.where(a == 1, i, 0), 0))
    return pl.pallas_call(
        body, out_shape=[jax.ShapeDtypeStruct(w.shape, F32)] * 4, grid=(2, nk),
        in_specs=[lay, lay, lay, g0_spec, g1_spec], out_specs=[lay] * 4,
        compiler_params=_params(2), name=name)(w, m, v, g0, g1)


def _small_allreduce_adamw(gpart, w, m, v):
    shape = gpart.shape

    def body(g_ref, w_ref, m_ref, v_ref, gsum_ref, delta_ref, nm_ref, nv_ref, recv_ref, send_sems, recv_sems):
        x, y, c = _place()
        me = 4 * x + 2 * y + c
        recv_ref[me] = g_ref[...]
        cps = []
        for k in range(1, 8):
            peer = (x ^ (k >> 2), y ^ ((k >> 1) & 1), c ^ (k & 1))
            cp = pltpu.make_async_remote_copy(
                src_ref=g_ref, dst_ref=recv_ref.at[me], send_sem=send_sems.at[k - 1], recv_sem=recv_sems.at[k - 1],
                device_id=peer, device_id_type=MESH)
            cp.start()
            cps.append(cp)
        for cp in cps:
            cp.wait()
        g = recv_ref[0]
        for dev in range(1, 8):
            g = g + recv_ref[dev]
        delta, nm, nv = _adamw_math(w_ref[...], g, m_ref[...], v_ref[...])
        gsum_ref[...] = g
        delta_ref[...] = delta
        nm_ref[...] = nm
        nv_ref[...] = nv

    vm = pl.BlockSpec(memory_space=pltpu.VMEM)
    return pl.pallas_call(
        body, out_shape=[jax.ShapeDtypeStruct(shape, F32)] * 4, in_specs=[vm] * 4, out_specs=[vm] * 4,
        scratch_shapes=[pltpu.VMEM((8,) + shape, F32), pltpu.SemaphoreType.DMA((7,)), pltpu.SemaphoreType.DMA((7,))],
        name="small_allreduce_adamw")(gpart, w, m, v)


BIG = ("w_in", "w_up_a", "w_up_b", "w_o", "w_ff1", "w_ff2", "w_pe", "w_pg")
COL_SHARDED = ("w_in", "w_up_a", "w_up_b", "w_ff1", "w_pe")
ROW_SHARDED = ("w_o", "w_ff2", "w_pg")
SMALL_ROWS = 16


def _pack_small(g_mix, g_mlp, g_pe, g_final, sinks, rel_bias, loss=None):
    d = g_final.shape[0]
    row = lambda v: jnp.pad(v.reshape(1, -1), ((0, 0), (0, d - v.size)))
    rows = [g_mix, g_mlp, g_pe, g_final.reshape(1, d),
            jnp.zeros((1, d), F32) if loss is None else row(loss), row(sinks), row(rel_bias)]
    out = jnp.concatenate(rows, axis=0)
    return jnp.pad(out, ((0, SMALL_ROWS - out.shape[0]), (0, 0)))


def _unpack_small(a, sinks_shape, rel_shape):
    return (a[0:2], a[2:4], a[4:6], a[6], a[8, :sinks_shape[0] * sinks_shape[1]].reshape(sinks_shape),
            a[9, :rel_shape[0] * rel_shape[1]].reshape(rel_shape))


def kernel(x, p, w_in, w_up_a, w_up_b, w_o, w_ff1, w_ff2, w_pe, w_pg, g_mix, g_mlp, g_pe, g_final, sinks, rel_bias, loss_target, m_w_in, m_w_up_a, m_w_up_b, m_w_o, m_w_ff1, m_w_ff2, m_w_pe, m_w_pg, m_g_mix, m_g_mlp, m_g_pe, m_g_final, m_sinks, m_rel_bias, v_w_in, v_w_up_a, v_w_up_b, v_w_o, v_w_ff1, v_w_ff2, v_w_pe, v_w_pg, v_g_mix, v_g_mlp, v_g_pe, v_g_final, v_sinks, v_rel_bias):
    depth = w_in.shape[0]
    assert depth == 2
    x0 = x[0]
    target = loss_target[0]
    d = x0.shape[1]
    wl = dict(w_in=w_in, w_up_a=w_up_a, w_up_b=w_up_b, w_o=w_o, w_ff1=w_ff1, w_ff2=w_ff2, w_pe=w_pe, w_pg=w_pg)
    ml = dict(w_in=m_w_in, w_up_a=m_w_up_a, w_up_b=m_w_up_b, w_o=m_w_o, w_ff1=m_w_ff1, w_ff2=m_w_ff2, w_pe=m_w_pe, w_pg=m_w_pg)
    vl = dict(w_in=v_w_in, w_up_a=v_w_up_a, w_up_b=v_w_up_b, w_o=v_w_o, w_ff1=v_w_ff1, w_ff2=v_w_ff2, w_pe=v_w_pe, w_pg=v_w_pg)
    c_idx = lax.axis_index("c").astype(jnp.int32)
    r_idx = (2 * lax.axis_index("x") + lax.axis_index("y")).astype(jnp.int32)
    cr = jnp.stack([c_idx, r_idx])

    wl["w_in"], ml["w_in"], vl["w_in"] = (jnp.swapaxes(a, 1, 2) for a in (w_in, m_w_in, v_w_in))

    bufs = {}
    for n in BIG:
        k, nn = wl[n].shape[1:]
        for l, b in enumerate(_cast_bf16(wl[n], r_idx.reshape(1), "cast_" + n)):
            bufs[n, l] = b.reshape(4, 2, k // 2, nn)

    def gather(keys, run):
        for key, b in zip(keys, run(_all_gather([bufs[key] for key in keys]))):
            bufs[key] = b

    def gathered(n, l):
        _, _, k2, nn = bufs[n, l].shape
        if n in ROW_SHARDED or n == "w_in":
            return bufs[n, l].reshape(8 * k2, nn)
        return bufs[n, l].reshape(4, 2 * k2, nn)

    gather([("w_in", 0)], lambda ex: _run_exchange("all_gather_first", ex))

    full = {n: [None] * depth for n in BIG}
    saved = []
    xi = x0
    for i in range(depth):
        st = dict(x0=xi)
        gm = g_mix[i].reshape(1, d)
        full["w_in"][i] = gathered("w_in", i)
        st["h1"], st["qkv"], st["gates"] = _inproj_fwd(xi, gm, full["w_in"][i], f"inproj_fwd_{i}")

        def attend(ex):
            st["oa"], st["lt"], st["nb"], filled = _sb_fwd(st["qkv"], f"sb_fwd_{i}", ex)
            return filled

        gather([(n, i) for n in BIG if n != "w_in"] + ([("w_in", 1)] if i == 0 else []), attend)
        for n in BIG:
            if n != "w_in":
                full[n][i] = gathered(n, i)
        st["ob"] = _swa_fwd(st["qkv"], sinks[i], rel_bias, f"swa_fwd_{i}")
        st["m"], st["x1"] = _mixer_fwd(st["oa"], st["ob"], st["gates"], xi, full["w_up_a"][i], full["w_up_b"][i],
                                       full["w_o"][i], f"mixer_fwd_{i}")
        st["h2"], st["u"], st["a"] = _ff1_fwd(st["x1"], g_mlp[i].reshape(1, d), full["w_ff1"][i], f"ff1_fwd_{i}")
        st["x2"] = _ff2_fwd(st["a"], st["x1"], full["w_ff2"][i], f"ff2_fwd_{i}")
        st["pb"], st["h3"], st["pe"], st["gt"], xi = _ple_fwd(p[i, 0], st["x2"], g_pe[i].reshape(1, d),
                                                            full["w_pe"][i], full["w_pg"][i], f"ple_fwd_{i}")
        saved.append(st)

    dx, dg_final, loss_part = _loss_bwd(xi, target, g_final.reshape(1, d), "loss_bwd")
    gw = {n: [None] * depth for n in BIG}
    reduced = {}

    def scatter_front(keys, tag):
        tensors = []
        for n, l in keys:
            g = gw[n][l]
            if n in ROW_SHARDED:
                ka, nb = g.shape[1:]
                g = g.reshape(4, ka // 4, nb)
            _, k, nn = g.shape
            tensors.append(g.reshape(4, 2, k // 2, nn))
        from_sibling = _rs_to_sibling(tensors, f"rs_to_sibling_{tag}")
        added = [_add_half(g, r, cr, f"add_half_{n}_{l}") for (n, l), g, r in zip(keys, tensors, from_sibling)]
        return _rs_to_chips([a[0] for a in added], [a[1] for a in added])

    def scatter_back(keys, parts, tag):
        halves = [_sum4(pc, cr, f"sum4_{n}_{l}") for (n, l), pc in zip(keys, parts)]
        for key, both in zip(keys, _exchange_halves(halves, f"exchange_halves_{tag}")):
            reduced[key] = both

    dg_mix, dg_mlp, dg_pe, dsinks = [None] * depth, [None] * depth, [None] * depth, [None] * depth
    drel = jnp.zeros((SW_HEADS, LANES), F32)
    for i in reversed(range(depth)):
        st = saved[i]
        dpe, dgt, dx2, dg_pe[i] = _ple_bwd(dx, st["pe"], st["gt"], st["x2"], g_pe[i].reshape(1, d),
                                           full["w_pg"][i], f"ple_bwd_{i}")
        gw["w_pe"][i] = _mm_tn(st["pb"], dpe, f"dw_pe_{i}", 4)
        gw["w_pg"][i] = _mm_tn(st["h3"], dgt, f"dw_pg_{i}")
        du, dx2b = _ff2_bwd(dx2, st["u"], full["w_ff2"][i], f"ff2_bwd_{i}")
        gw["w_ff2"][i] = _mm_tn(st["a"], dx2b, f"dw_ff2_{i}")
        gw["w_ff1"][i] = _mm_tn(st["h2"], du, f"dw_ff1_{i}", 4)
        dx1, dx1b, dg_mlp[i] = _ff1_bwd(du, dx2, st["x1"], g_mlp[i].reshape(1, d), full["w_ff1"][i], f"ff1_bwd_{i}")
        gw["w_o"][i] = _mm_tn(st["m"], dx1b, f"dw_o_{i}")
        dya, dyb, dgates, doa, dob = _mixer_bwd(dx1b, st["gates"], st["oa"], st["ob"], full["w_o"][i],
                                                full["w_up_a"][i], full["w_up_b"][i], f"mixer_bwd_{i}")
        gw["w_up_a"][i] = _mm_tn(st["oa"], dya, f"dw_up_a_{i}", 4)
        gw["w_up_b"][i] = _mm_tn(st["ob"], dyb, f"dw_up_b_{i}", 4)
        keys = [(n, i) for n in BIG if n != "w_in"] + ([("w_in", 1)] if i == 0 else [])
        dqa, dka, dva, parts = _sb_bwd(st["qkv"], st["lt"], st["nb"], doa, f"sb_bwd_{i}", scatter_front(keys, i))
        scatter_back(keys, parts, i)
        dqb, dkb, dvb, dsk, drl = _swa_bwd(st["qkv"], st["ob"], dob, sinks[i], rel_bias, f"swa_bwd_{i}")
        dsinks[i] = dsk[:, 0]
        drel = drel + drl
        dqkv = jnp.concatenate([dqa, dka.astype(BF16), dva.astype(BF16), dqb, dkb.astype(BF16), dvb.astype(BF16)],
                               axis=1)
        dw_in_t = _mm_tn_pieces([dqkv, dgates], st["h1"], f"dw_in_{i}")
        gw["w_in"][i] = dw_in_t.reshape(4, dw_in_t.shape[0] // 4, d)
        if i == 1:
            dx, dg_mix[i] = _inproj_bwd(dqkv, dgates, dx1, st["x0"], g_mix[i].reshape(1, d), full["w_in"][i],
                                        f"inproj_bwd_{i}")
        else:
            keys = [("w_in", 0)]
            (dx, dg_mix[i]), parts = _inproj_bwd(dqkv, dgates, dx1, st["x0"], g_mix[i].reshape(1, d),
                                                 full["w_in"][i], f"inproj_bwd_{i}", scatter_front(keys, "last"))
            scatter_back(keys, parts, "last")
    grad_x = dx[None]

    outs = {}
    for n in BIG:
        g0, g1 = (reduced[n, l].reshape(wl[n].shape[1:]) for l in range(depth))
        outs[n] = _adamw(wl[n], ml[n], vl[n], g0, g1, "adamw_" + n)
    outs["w_in"] = [jnp.swapaxes(a, 1, 2) for a in outs["w_in"]]

    drel_bias = drel[:, :N_BUCKETS].T
    gsmall = _pack_small(jnp.concatenate(dg_mix, 0), jnp.concatenate(dg_mlp, 0), jnp.concatenate(dg_pe, 0),
                         dg_final[0], jnp.stack(dsinks), drel_bias, loss_part[0, :1])
    wsmall = _pack_small(g_mix, g_mlp, g_pe, g_final, sinks, rel_bias)
    msmall = _pack_small(m_g_mix, m_g_mlp, m_g_pe, m_g_final, m_sinks, m_rel_bias)
    vsmall = _pack_small(v_g_mix, v_g_mlp, v_g_pe, v_g_final, v_sinks, v_rel_bias)
    small = _small_allreduce_adamw(gsmall, wsmall, msmall, vsmall)
    loss = small[0][7, 0]
    small = [_unpack_small(a, sinks.shape, rel_bias.shape) for a in small]

    result = [loss, grad_x]
    for kind in range(4):
        result += [outs[n][kind] for n in BIG]
        result += list(small[kind])
    return tuple(result)
```

```python
import functools
import math
from typing import Callable, NamedTuple

import numpy as np
import jax
import jax.numpy as jnp
from jax import lax
from jax.experimental import pallas as pl
from jax.experimental.pallas import tpu as pltpu

F32 = jnp.float32
BF16 = jnp.bfloat16
MESH = pl.DeviceIdType.MESH

HEAD_DIM = 64
SB_HEADS = 8
SW_HEADS = 8
SW_KV_HEADS = 2
WINDOW = 128
N_BUCKETS = 32
MAX_DISTANCE = 128
EPS = 1e-6
SB_W = SB_HEADS * HEAD_DIM
SW_QW = SW_HEADS * HEAD_DIM
SW_KVW = SW_KV_HEADS * HEAD_DIM
QKV_W = 3 * SB_W + SW_QW + 2 * SW_KVW
SCALE = HEAD_DIM ** -0.5
LANES = 128
TQ = 128
BK = 128
NEG = -1e30
SB_EXHAUSTED = -106.0

ADAM_LR = 0.001
ADAM_B1 = 0.9
ADAM_B2 = 0.999
ADAM_EPS = 1e-08
ADAM_WD = 0.01
ADAM_STEP = 10

VMEM_LIMIT = 56 * 1024 * 1024


def _dot(a, b):
    return jnp.dot(a, b, preferred_element_type=F32)


def _dot_nt(a, b):
    return lax.dot_general(a, b, (((1,), (1,)), ((), ())), preferred_element_type=F32)


def _dot_tn(a, b):
    return lax.dot_general(a, b, (((0,), (0,)), ((), ())), preferred_element_type=F32)


def _sum_all(x):
    return jnp.sum(jnp.sum(x, axis=1, keepdims=True), axis=0, keepdims=True)


def _sigmoid(x):
    return 1.0 / (1.0 + jnp.exp(-x))


def _rms(x, g):
    r = lax.rsqrt(jnp.mean(x * x, axis=-1, keepdims=True) + EPS)
    return (x * r) * g


def _rms_bwd(dy, x, g):
    r = lax.rsqrt(jnp.mean(x * x, axis=-1, keepdims=True) + EPS)
    n = x * r
    dg = jnp.sum(dy * n, axis=0, keepdims=True)
    dn = dy * g
    dx = r * (dn - n * jnp.mean(dn * n, axis=-1, keepdims=True))
    return dx, dg


def _params(n_axes):
    return pltpu.CompilerParams(dimension_semantics=("arbitrary",) * n_axes, vmem_limit_bytes=VMEM_LIMIT)


def _rowcall(name, body, row_ins, const_ins, row_outs, acc_outs=(), tm=512, side=None):
    s = row_ins[0].shape[0]
    assert s % tm == 0
    in_specs = [pl.BlockSpec((tm, a.shape[1]), lambda i: (i, 0)) for a in row_ins]
    in_specs += [pl.BlockSpec(a.shape, functools.partial(lambda i, nd: (0,) * nd, nd=a.ndim)) for a in const_ins]
    out_shape = [jax.ShapeDtypeStruct((s, c), dt) for c, dt in row_outs]
    out_specs = [pl.BlockSpec((tm, c), lambda i: (i, 0)) for c, _ in row_outs]
    out_shape += [jax.ShapeDtypeStruct(sh, dt) for sh, dt in acc_outs]
    out_specs += [pl.BlockSpec(sh, functools.partial(lambda i, nd: (0,) * nd, nd=len(sh))) for sh, _ in acc_outs]
    if side is None:
        return pl.pallas_call(body, out_shape=out_shape, grid=(s // tm,), in_specs=in_specs, out_specs=out_specs,
                              compiler_params=_params(1), name=name)(*row_ins, *const_ins)
    n_in, n_out = len(in_specs), len(out_specs)

    def with_side(*refs):
        side_in, outs, side_out, sems = _side_refs(side, refs[n_in:], n_out)

        @pl.when(pl.program_id(0) == 0)
        def _():
            side.start(side_in, side_out, *sems)

        body(*refs[:n_in], *outs)

        @pl.when(pl.program_id(0) == s // tm - 1)
        def _():
            side.finish(side_in, side_out, *sems)

    s_in, s_shape, s_out, s_alias, s_sems = _side_specs(side, n_in, n_out)
    outs = pl.pallas_call(with_side, out_shape=out_shape + s_shape, grid=(s // tm,), in_specs=in_specs + s_in,
                          out_specs=out_specs + s_out, input_output_aliases=s_alias, scratch_shapes=s_sems,
                          compiler_params=_params(1), name=name)(*row_ins, *const_ins, *side.arrays)
    return outs[:n_out], outs[n_out:]


def _dot_cols(a, w_ref):
    return jnp.concatenate([_dot(a, w_ref[r]) for r in range(w_ref.shape[0])], axis=1)


def _dot_cols_t(a, w_ref):
    n = w_ref.shape[2]
    out = _dot_nt(a[:, :n], w_ref[0])
    for r in range(1, w_ref.shape[0]):
        out = out + _dot_nt(a[:, r * n:(r + 1) * n], w_ref[r])
    return out


def _inproj_fwd(x, g, wt, name):
    d = x.shape[1]

    def body(x_ref, g_ref, w_ref, h_ref, qkv_ref, gate_ref):
        hb = _rms(x_ref[...], g_ref[...]).astype(BF16)
        h_ref[...] = hb
        qkv_ref[...] = _dot_nt(hb, w_ref[:QKV_W, :]).astype(BF16)
        gate_ref[...] = _dot_nt(hb, w_ref[QKV_W:, :])

    return _rowcall(name, body, [x], [g, wt], [(d, BF16), (QKV_W, BF16), (2 * d, F32)])


def _mixer_fwd(oa, ob, gates, x, wua, wub, wo, name):
    d = x.shape[1]

    def body(oa_ref, ob_ref, gate_ref, x_ref, wua_ref, wub_ref, wo_ref, m_ref, x1_ref):
        ya = _dot_cols(oa_ref[...], wua_ref)
        yb = _dot_cols(ob_ref[...], wub_ref)
        m = _sigmoid(gate_ref[:, :d]) * ya + _sigmoid(gate_ref[:, d:]) * yb
        mb = m.astype(BF16)
        m_ref[...] = mb
        x1_ref[...] = x_ref[...] + _dot(mb, wo_ref[...])

    return _rowcall(name, body, [oa, ob, gates, x], [wua, wub, wo], [(d, BF16), (d, F32)])


def _ff1_fwd(x1, g, w1, name, side=None):
    _, d, nq = w1.shape
    dff = 4 * nq

    def body(x_ref, g_ref, w_ref, h_ref, u_ref, a_ref):
        hb = _rms(x_ref[...], g_ref[...]).astype(BF16)
        h_ref[...] = hb
        u = _dot_cols(hb, w_ref)
        u_ref[...] = u
        a_ref[...] = jnp.square(jnp.maximum(u, 0.0)).astype(BF16)

    return _rowcall(name, body, [x1], [g, w1], [(d, BF16), (dff, F32), (dff, BF16)], side=side)


def _ff2_fwd(a, x1, w2, name):
    d = x1.shape[1]

    def body(a_ref, x_ref, w_ref, o_ref):
        o_ref[...] = x_ref[...] + _dot(a_ref[...], w_ref[...])

    return _rowcall(name, body, [a, x1], [w2], [(d, F32)])[0]


def _ple_fwd(p, x2, g, wpe, wpg, name):
    d = x2.shape[1]

    def body(p_ref, x_ref, g_ref, wpe_ref, wpg_ref, pb_ref, h_ref, pe_ref, gt_ref, x3_ref):
        pb = p_ref[...].astype(BF16)
        pb_ref[...] = pb
        pe = _dot_cols(pb, wpe_ref)
        x = x_ref[...]
        hb = _rms(x, g_ref[...]).astype(BF16)
        h_ref[...] = hb
        gt = _dot(hb, wpg_ref[...])
        pe_ref[...] = pe
        gt_ref[...] = gt
        x3_ref[...] = x + pe * _sigmoid(gt)

    return _rowcall(name, body, [p, x2], [g, wpe, wpg],
                    [(p.shape[1], BF16), (d, BF16), (d, F32), (d, F32), (d, F32)])


def _pair_stack(t, lane):
    zero = jnp.zeros_like(t)
    return jnp.concatenate([jnp.where(lane < HEAD_DIM, t, zero), jnp.where(lane >= HEAD_DIM, t, zero)], axis=0)


def _sb_rel():
    row = lax.broadcasted_iota(jnp.int32, (2 * TQ, BK), 0)
    row = jnp.where(row >= TQ, row - TQ, row)
    col = lax.broadcasted_iota(jnp.int32, (2 * TQ, BK), 1)
    return col - row


def _split_dot(x, m01):
    hi = x.astype(BF16)
    lo = (x - hi.astype(F32)).astype(BF16)
    return _dot(hi, m01) + _dot(lo, m01)


def _sb_scores(qs, k, mask):
    z = _dot_nt(qs, k) * SCALE
    lb = jnp.minimum(z, 0.0) - jnp.log1p(jnp.exp(-jnp.abs(z)))
    lm = lb - z
    return lb, lm if mask is None else jnp.where(mask, lm, 0.0)


SB_STRAIGHT = 3
SB_WIDE = SB_STRAIGHT * BK
SB_QB = 2
SWA_QB = 4


def _sb_wide_consts():
    j = np.arange(BK)[:, None]
    s = np.arange(BK)[None, :]
    ones = np.ones((BK, BK), np.float32)
    as_bf16 = lambda m: jnp.asarray(np.concatenate([m, ones], axis=1).astype(np.float32), dtype=BF16)
    return as_bf16(j > s), as_bf16(j <= s), as_bf16(j < s)


def _wide_sums(x, m01, suffix):
    parts = [_split_dot(x[:, b * BK:(b + 1) * BK], m01) for b in range(SB_STRAIGHT)]
    order = range(SB_STRAIGHT - 1, -1, -1) if suffix else range(SB_STRAIGHT)
    out = [None] * SB_STRAIGHT
    carry = None
    for b in order:
        out[b] = parts[b][:, :BK] if carry is None else parts[b][:, :BK] + carry
        carry = parts[b][:, BK:] if carry is None else carry + parts[b][:, BK:]
    return jnp.concatenate(out, axis=1), carry


def _side_refs(ex, rest, n_out):
    n_in = len(ex.arrays) if ex else 0
    n_alias = len(ex.aliased) if ex else 0
    ins, rest = rest[:n_in], rest[n_in:]
    outs, rest = rest[:n_out], rest[n_out:]
    return ins, outs, rest[:n_alias], rest[n_alias:]


def _side_specs(ex, n_in, n_out):
    if ex is None:
        return [], [], [], {}, []
    any_spec = pl.BlockSpec(memory_space=pl.ANY)
    return ([any_spec] * len(ex.arrays),
            [jax.ShapeDtypeStruct(ex.arrays[a].shape, ex.arrays[a].dtype) for a in ex.aliased],
            [any_spec] * len(ex.aliased), {n_in + a: n_out + o for o, a in enumerate(ex.aliased)},
            [pltpu.SemaphoreType.DMA(ex.sems), pltpu.SemaphoreType.DMA(ex.sems)])


def _sb_fwd(qkv, name, side=None):
    s = qkv.shape[0]
    nq = s // TQ
    npair = SB_W // LANES
    sufw = _sb_wide_consts()[0]

    def body(q_ref, k_ref, v_ref, sufw_ref, *rest):
        side_in, (o_ref, lt_ref, nb_ref), side_out, scratch = _side_refs(side, rest, 3)
        cf_ref, acc_ref = scratch[:2]
        step_id = pl.program_id(1)
        if side is not None:
            @pl.when((pl.program_id(0) == 0) & (step_id == 0))
            def _():
                side.start(side_in, side_out, *scratch[2:])
        lane = lax.broadcasted_iota(jnp.int32, (TQ, LANES), 1)
        rel = _sb_rel()
        blocks = [step_id * SB_QB + b for b in range(SB_QB)]
        qs = [_pair_stack(q_ref[b * TQ:(b + 1) * TQ, :], lane) for b in range(SB_QB)]

        straight = blocks[0] >= SB_STRAIGHT - 1

        @pl.when(straight)
        def _():
            for b, i in enumerate(blocks):
                w0 = pl.multiple_of((i - (SB_STRAIGHT - 1)) * BK, BK)
                kw = k_ref[pl.ds(w0, SB_WIDE), :]
                lb, lm = _sb_scores(qs[b], kw, None)
                own = rel < 0
                past = SB_WIDE - BK
                lm = jnp.concatenate([lm[:, :past], jnp.where(own, lm[:, past:], 0.0)], axis=1)
                after, total = _wide_sums(lm, sufw_ref[...], True)
                a = jnp.exp(lb + after)
                a = jnp.concatenate([a[:, :past], jnp.where(own, a[:, past:], 0.0)], axis=1)
                acc_ref[b] = _dot(a.astype(BF16), v_ref[pl.ds(w0, SB_WIDE), :])
                cf_ref[b] = total

        @pl.when(jnp.logical_not(straight))
        def _():
            cf_ref[...] = jnp.zeros_like(cf_ref)
            acc_ref[...] = jnp.zeros_like(acc_ref)

        for b, i in enumerate(blocks):
            q0 = i * TQ

            def more(c, i=i):
                return (c[0] <= i) & (c[1] > SB_EXHAUSTED)

            def step(c, b=b, i=i, q0=q0):
                k0 = pl.multiple_of((i - c[0]) * BK, BK)
                k = k_ref[pl.ds(k0, BK), :]
                v = v_ref[pl.ds(k0, BK), :]
                mask = rel < (q0 - k0)
                lb, lm = _sb_scores(qs[b], k, mask)
                cs = _split_dot(lm, sufw_ref[...])
                a = jnp.where(mask, jnp.exp(lb + cs[:, :BK] + cf_ref[b]), 0.0)
                acc_ref[b] += _dot(a.astype(BF16), v)
                cf = cf_ref[b] + cs[:, BK:]
                cf_ref[b] = cf
                return c[0] + 1, jnp.max(cf)

            n_blocks, _ = lax.while_loop(
                more, step, (jnp.where(straight, SB_STRAIGHT, 0).astype(jnp.int32), jnp.max(cf_ref[b])))
            o_ref[b * TQ:(b + 1) * TQ, :] = jnp.where(lane < HEAD_DIM, acc_ref[b, :TQ, :],
                                                     acc_ref[b, TQ:, :]).astype(BF16)
            lt_ref[b] = cf_ref[b]
            nb_ref[b] = jnp.full(nb_ref.shape[1:], n_blocks, F32)
        if side is not None:
            @pl.when((pl.program_id(0) == npair - 1) & (step_id == nq // SB_QB - 1))
            def _():
                side.finish(side_in, side_out, *scratch[2:])

    s_in, s_shape, s_out, s_alias, s_sems = _side_specs(side, 4, 3)
    outs = pl.pallas_call(
        body,
        out_shape=[jax.ShapeDtypeStruct((s, SB_W), BF16), jax.ShapeDtypeStruct((npair, nq, 2 * TQ, BK), F32),
                   jax.ShapeDtypeStruct((npair, nq, 8, LANES), F32)] + s_shape,
        grid=(npair, nq // SB_QB),
        in_specs=[pl.BlockSpec((SB_QB * TQ, LANES), lambda j, i: (i, j)),
                  pl.BlockSpec((s, LANES), lambda j, i: (0, npair + j)),
                  pl.BlockSpec((s, LANES), lambda j, i: (0, 2 * npair + j)),
                  pl.BlockSpec(sufw.shape, lambda j, i: (0, 0))] + s_in,
        out_specs=[pl.BlockSpec((SB_QB * TQ, LANES), lambda j, i: (i, j)),
                   pl.BlockSpec((None, SB_QB, 2 * TQ, BK), lambda j, i: (j, i, 0, 0)),
                   pl.BlockSpec((None, SB_QB, 8, LANES), lambda j, i: (j, i, 0, 0))] + s_out,
        input_output_aliases=s_alias,
        scratch_shapes=[pltpu.VMEM((SB_QB, 2 * TQ, BK), F32), pltpu.VMEM((SB_QB, 2 * TQ, LANES), F32)] + s_sems,
        compiler_params=_params(2), name=name)(qkv, qkv, qkv, sufw, *(side.arrays if side else ()))
    return outs[0], outs[1], outs[2], outs[3:]


def _sb_bwd(qkv, lt, nb, doa, name, side=None):
    s = qkv.shape[0]
    nq = s // TQ
    npair = SB_W // LANES
    _, prew, prexw = _sb_wide_consts()

    def body(q_ref, k_ref, v_ref, lt_ref, nb_ref, do_ref, prew_ref, prexw_ref, *rest):
        side_in, (dq_ref, dk_ref, dv_ref), side_out, scratch = _side_refs(side, rest, 3)
        cp_ref, ce_ref, dqa_ref = scratch[:3]
        step_id = pl.program_id(1)
        if side is not None:
            @pl.when((pl.program_id(0) == 0) & (step_id == 0))
            def _():
                side.start(side_in, side_out, *scratch[3:])
        lane = lax.broadcasted_iota(jnp.int32, (TQ, LANES), 1)
        rel = _sb_rel()
        blocks = [step_id * SB_QB + b for b in range(SB_QB)]
        rows = [slice(b * TQ, (b + 1) * TQ) for b in range(SB_QB)]
        qs = [_pair_stack(q_ref[rows[b], :], lane) for b in range(SB_QB)]
        dos = [_pair_stack(do_ref[rows[b], :], lane) for b in range(SB_QB)]
        n_blocks = [jnp.clip(jnp.max(nb_ref[b]).astype(jnp.int32), 1, i + 1) for b, i in enumerate(blocks)]
        first = [i + 1 - n for i, n in zip(blocks, n_blocks)]

        @pl.when(step_id == 0)
        def _():
            dk_ref[...] = jnp.zeros_like(dk_ref)
            dv_ref[...] = jnp.zeros_like(dv_ref)

        straight = n_blocks[0] == SB_STRAIGHT
        for n in n_blocks[1:]:
            straight = straight & (n == SB_STRAIGHT)

        @pl.when(straight)
        def _():
            for b in range(SB_QB):
                w0 = pl.multiple_of(first[b] * BK, BK)
                kw = k_ref[pl.ds(w0, SB_WIDE), :]
                vw = v_ref[pl.ds(w0, SB_WIDE), :]
                lb, lm = _sb_scores(qs[b], kw, None)
                own = rel < 0
                past = SB_WIDE - BK
                on_past_keys = lambda t: jnp.concatenate([t[:, :past], jnp.where(own, t[:, past:], 0.0)], axis=1)
                lm = on_past_keys(lm)
                upto, _ = _wide_sums(lm, prew_ref[...], False)
                lt = lt_ref[b]
                a = on_past_keys(jnp.exp(lb + (jnp.concatenate([lt] * SB_STRAIGHT, axis=1) - upto)))
                e = a * _dot_nt(dos[b], vw)
                big_e, _ = _wide_sums(e, prexw_ref[...], False)
                dz = (on_past_keys(e - jnp.exp(lb) * (e + big_e)) * SCALE).astype(BF16)
                dk_ref[pl.ds(w0, SB_WIDE), :] += _dot_tn(dz, qs[b])
                dv_ref[pl.ds(w0, SB_WIDE), :] += _dot_tn(a.astype(BF16), dos[b])
                dqa_ref[b] = _dot(dz, kw)

        @pl.when(jnp.logical_not(straight))
        def _():
            cp_ref[...] = jnp.zeros_like(cp_ref)
            ce_ref[...] = jnp.zeros_like(ce_ref)
            dqa_ref[...] = jnp.zeros_like(dqa_ref)
            for b, i in enumerate(blocks):
                q0 = i * TQ

                def step(it, carry, b=b, q0=q0):
                    k0 = pl.multiple_of((first[b] + it) * BK, BK)
                    k = k_ref[pl.ds(k0, BK), :]
                    v = v_ref[pl.ds(k0, BK), :]
                    mask = rel < (q0 - k0)
                    lb, lm = _sb_scores(qs[b], k, mask)
                    cs = _split_dot(lm, prew_ref[...])
                    a = jnp.where(mask, jnp.exp(lb + (lt_ref[b] - (cs[:, :BK] + cp_ref[b]))), 0.0)
                    e = a * _dot_nt(dos[b], v)
                    ce = _split_dot(e, prexw_ref[...])
                    big_e = ce[:, :BK] + ce_ref[b]
                    dz = (jnp.where(mask, e - jnp.exp(lb) * (e + big_e), 0.0) * SCALE).astype(BF16)
                    dk_ref[pl.ds(k0, BK), :] += _dot_tn(dz, qs[b])
                    dv_ref[pl.ds(k0, BK), :] += _dot_tn(a.astype(BF16), dos[b])
                    dqa_ref[b] += _dot(dz, k)
                    cp_ref[b] += cs[:, BK:]
                    ce_ref[b] += ce[:, BK:]
                    return carry

                lax.fori_loop(0, n_blocks[b], step, 0)

        for b in range(SB_QB):
            dq_ref[rows[b], :] = jnp.where(lane < HEAD_DIM, dqa_ref[b, :TQ, :], dqa_ref[b, TQ:, :]).astype(BF16)
        if side is not None:
            @pl.when((pl.program_id(0) == npair - 1) & (step_id == nq // SB_QB - 1))
            def _():
                side.finish(side_in, side_out, *scratch[3:])

    s_in, s_shape, s_out, s_alias, s_sems = _side_specs(side, 8, 3)
    outs = pl.pallas_call(
        body,
        out_shape=[jax.ShapeDtypeStruct((s, SB_W), BF16), jax.ShapeDtypeStruct((s, SB_W), F32),
                   jax.ShapeDtypeStruct((s, SB_W), F32)] + s_shape,
        grid=(npair, nq // SB_QB),
        in_specs=[pl.BlockSpec((SB_QB * TQ, LANES), lambda j, i: (i, j)),
                  pl.BlockSpec((s, LANES), lambda j, i: (0, npair + j)),
                  pl.BlockSpec((s, LANES), lambda j, i: (0, 2 * npair + j)),
                  pl.BlockSpec((None, SB_QB, 2 * TQ, BK), lambda j, i: (j, i, 0, 0)),
                  pl.BlockSpec((None, SB_QB, 8, LANES), lambda j, i: (j, i, 0, 0)),
                  pl.BlockSpec((SB_QB * TQ, LANES), lambda j, i: (i, j)),
                  pl.BlockSpec(prew.shape, lambda j, i: (0, 0)),
                  pl.BlockSpec(prexw.shape, lambda j, i: (0, 0))] + s_in,
        out_specs=[pl.BlockSpec((SB_QB * TQ, LANES), lambda j, i: (i, j)),
                   pl.BlockSpec((s, LANES), lambda j, i: (0, j)),
                   pl.BlockSpec((s, LANES), lambda j, i: (0, j))] + s_out,
        input_output_aliases=s_alias,
        scratch_shapes=[pltpu.VMEM((SB_QB, 2 * TQ, BK), F32), pltpu.VMEM((SB_QB, 2 * TQ, BK), F32),
                        pltpu.VMEM((SB_QB, 2 * TQ, LANES), F32)] + s_sems,
        compiler_params=_params(2), name=name)(qkv, qkv, qkv, lt, nb, doa, prew, prexw,
                                               *(side.arrays if side else ()))
    return outs[0], outs[1], outs[2], outs[3:]


def _bucket_table():
    i = np.arange(TQ)[:, None]
    j = np.arange(2 * BK)[None, :]
    dist = np.maximum(TQ + i - j, 0)
    max_exact = N_BUCKETS // 2
    df = np.maximum(dist, 1).astype(np.float32)
    large = max_exact + (np.log(df / np.float32(max_exact)) / np.float32(math.log(MAX_DISTANCE / max_exact))
                         * np.float32(N_BUCKETS - max_exact)).astype(np.int32)
    large = np.minimum(large, N_BUCKETS - 1)
    return np.where(dist < max_exact, dist, large).astype(np.int32)


def _swa_align_in(t, lane, g):
    tf = t.astype(F32)
    tr = pltpu.roll(tf, HEAD_DIM, 1)
    gmask = (lane >= HEAD_DIM) == (g == 1)
    top = jnp.where(gmask, jnp.where(g == 0, tf, tr), 0.0)
    bot = jnp.where(gmask, jnp.where(g == 1, tf, tr), 0.0)
    return jnp.concatenate([top, bot], axis=0).astype(BF16)


def _swa_align_out(t, lane, g):
    top, bot = t[:TQ, :], t[TQ:, :]
    top = jnp.where(g == 0, top, pltpu.roll(top, HEAD_DIM, 1))
    bot = jnp.where(g == 1, bot, pltpu.roll(bot, HEAD_DIM, 1))
    return jnp.where(lane < HEAD_DIM, top, bot)


def _swa_bias(bias_ref, bucket_ref, rb_ref, j):
    dist = TQ + lax.broadcasted_iota(jnp.int32, (TQ, 2 * BK), 0) - lax.broadcasted_iota(jnp.int32, (TQ, 2 * BK), 1)
    window = (dist >= 0) & (dist < WINDOW)
    for hh in range(2):
        def add(b, acc):
            return acc + jnp.where(bucket_ref[...] == b, rb_ref[b, 2 * j + hh], 0.0)
        bias = lax.fori_loop(0, N_BUCKETS, add, jnp.zeros((TQ, 2 * BK), F32))
        bias_ref[hh * TQ:(hh + 1) * TQ, :] = jnp.where(window, bias, NEG)


def _swa_probs(qs, k2, bias, own_block, sink_ref, i, j):
    s = _dot_nt(qs, k2) * SCALE + bias
    s = jnp.where(own_block | (i > 0), s, NEG)
    row1 = lax.broadcasted_iota(jnp.int32, (2 * TQ, 1), 0)
    sink = jnp.where(row1 < TQ, sink_ref[2 * j], sink_ref[2 * j + 1])
    m = jnp.maximum(jnp.max(s, axis=1, keepdims=True), sink)
    e = jnp.exp(s - m)
    es = jnp.exp(sink - m)
    inv = 1.0 / (jnp.sum(e, axis=1, keepdims=True) + es)
    return e * inv, es * inv


def _swa_kv(ref, i):
    prev = pl.multiple_of(jnp.maximum(i - 1, 0) * BK, BK)
    cur = pl.multiple_of(i * BK, BK)
    return jnp.concatenate([ref[pl.ds(prev, BK), :], ref[pl.ds(cur, BK), :]], axis=0), prev, cur


def _swa_fwd(qkv, sinks, rel_bias, name):
    s = qkv.shape[0]
    nq = s // TQ
    npair = SW_QW // LANES
    qcol = 3 * SB_W // LANES
    bucket = jnp.asarray(_bucket_table())

    def body(q_ref, k_ref, v_ref, bucket_ref, sink_ref, rb_ref, o_ref, bias_ref):
        j = pl.program_id(0)
        step = pl.program_id(1)
        g = j // 2
        lane = lax.broadcasted_iota(jnp.int32, (TQ, LANES), 1)

        @pl.when(step == 0)
        def _():
            _swa_bias(bias_ref, bucket_ref, rb_ref, j)

        own_block = lax.broadcasted_iota(jnp.int32, (2 * TQ, 2 * BK), 1) >= BK
        for b in range(SWA_QB):
            i = step * SWA_QB + b
            rows = slice(b * TQ, (b + 1) * TQ)
            qs = _swa_align_in(q_ref[rows, :], lane, g)
            k2, _, _ = _swa_kv(k_ref, i)
            v2, _, _ = _swa_kv(v_ref, i)
            pr, _ = _swa_probs(qs, k2, bias_ref[...], own_block, sink_ref, i, j)
            o_ref[rows, :] = _swa_align_out(_dot(pr.astype(BF16), v2), lane, g).astype(BF16)

    assert nq % SWA_QB == 0
    return pl.pallas_call(
        body, out_shape=jax.ShapeDtypeStruct((s, SW_QW), BF16), grid=(npair, nq // SWA_QB),
        in_specs=[pl.BlockSpec((SWA_QB * TQ, LANES), lambda j, i: (i, qcol + j)),
                  pl.BlockSpec((s, LANES), lambda j, i: (0, qcol + npair)),
                  pl.BlockSpec((s, LANES), lambda j, i: (0, qcol + npair + 1)),
                  pl.BlockSpec((TQ, 2 * BK), lambda j, i: (0, 0)),
                  pl.BlockSpec(memory_space=pltpu.SMEM),
                  pl.BlockSpec(memory_space=pltpu.SMEM)],
        out_specs=pl.BlockSpec((SWA_QB * TQ, LANES), lambda j, i: (i, j)),
        scratch_shapes=[pltpu.VMEM((2 * TQ, 2 * BK), F32)],
        compiler_params=_params(2), name=name)(qkv, qkv, qkv, bucket, sinks, rel_bias)


def _swa_bwd(qkv, ob, dob, sinks, rel_bias, name):
    s = qkv.shape[0]
    nq = s // TQ
    npair = SW_QW // LANES
    qcol = 3 * SB_W // LANES
    bucket = jnp.asarray(_bucket_table())

    def body(q_ref, k_ref, v_ref, o_ref, do_ref, bucket_ref, sink_ref, rb_ref,
             dq_ref, dk_ref, dv_ref, dsink_ref, drel_ref, bias_ref, dsacc_ref):
        j = pl.program_id(0)
        step = pl.program_id(1)
        g = j // 2
        lane = lax.broadcasted_iota(jnp.int32, (TQ, LANES), 1)
        row8 = lax.broadcasted_iota(jnp.int32, (SW_HEADS, LANES), 0)
        lane8 = lax.broadcasted_iota(jnp.int32, (SW_HEADS, LANES), 1)

        @pl.when((step == 0) & (j == 0))
        def _():
            dk_ref[...] = jnp.zeros_like(dk_ref)
            dv_ref[...] = jnp.zeros_like(dv_ref)
            dsink_ref[...] = jnp.zeros_like(dsink_ref)
            drel_ref[...] = jnp.zeros_like(drel_ref)

        @pl.when(step == 0)
        def _():
            _swa_bias(bias_ref, bucket_ref, rb_ref, j)
            dsacc_ref[...] = jnp.zeros_like(dsacc_ref)

        own_block = lax.broadcasted_iota(jnp.int32, (2 * TQ, 2 * BK), 1) >= BK
        ds_sum = jnp.zeros(dsacc_ref.shape, F32)
        dsink = jnp.zeros((SW_HEADS, LANES), F32)
        for b in range(SWA_QB):
            i = step * SWA_QB + b
            rows = slice(b * TQ, (b + 1) * TQ)
            qs = _swa_align_in(q_ref[rows, :], lane, g)
            do = do_ref[rows, :]
            dos = _swa_align_in(do, lane, g)
            dof = do.astype(F32) * o_ref[rows, :].astype(F32)
            d0 = jnp.sum(jnp.where(lane < HEAD_DIM, dof, 0.0), axis=1, keepdims=True)
            d1 = jnp.sum(jnp.where(lane >= HEAD_DIM, dof, 0.0), axis=1, keepdims=True)
            delta = jnp.concatenate([d0, d1], axis=0)
            k2, prev, cur = _swa_kv(k_ref, i)
            v2, _, _ = _swa_kv(v_ref, i)
            pr, psink = _swa_probs(qs, k2, bias_ref[...], own_block, sink_ref, i, j)
            ds = pr * (_dot_nt(dos, v2) - delta)
            ds_sum = ds_sum + ds
            sd = psink * delta
            ds0 = -jnp.sum(sd[:TQ, :], axis=0, keepdims=True)
            ds1 = -jnp.sum(sd[TQ:, :], axis=0, keepdims=True)
            dsink = dsink + jnp.where(row8 == 2 * j, ds0, jnp.where(row8 == 2 * j + 1, ds1, 0.0))
            dsb = ds.astype(BF16)
            dq_ref[rows, :] = _swa_align_out(_dot(dsb, k2) * SCALE, lane, g).astype(BF16)
            dk2 = _dot_tn(dsb, qs) * SCALE
            dv2 = _dot_tn(pr.astype(BF16), dos)
            dk_ref[pl.ds(prev, BK), :] += dk2[:BK, :]
            dk_ref[pl.ds(cur, BK), :] += dk2[BK:, :]
            dv_ref[pl.ds(prev, BK), :] += dv2[:BK, :]
            dv_ref[pl.ds(cur, BK), :] += dv2[BK:, :]
        dsacc_ref[...] += ds_sum
        dsink_ref[...] += dsink

        @pl.when(step == nq // SWA_QB - 1)
        def _():
            for hh in range(2):
                def red(b, acc):
                    val = _sum_all(jnp.where(bucket_ref[...] == b, dsacc_ref[hh * TQ:(hh + 1) * TQ, :], 0.0))
                    return jnp.where((row8 == 2 * j + hh) & (lane8 == b), val, acc)
                drel_ref[...] += lax.fori_loop(0, N_BUCKETS, red, jnp.zeros((SW_HEADS, LANES), F32))

    whole = lambda j, i: (0, 0)
    return pl.pallas_call(
        body,
        out_shape=[jax.ShapeDtypeStruct((s, SW_QW), BF16), jax.ShapeDtypeStruct((s, LANES), F32),
                   jax.ShapeDtypeStruct((s, LANES), F32), jax.ShapeDtypeStruct((SW_HEADS, LANES), F32),
                   jax.ShapeDtypeStruct((SW_HEADS, LANES), F32)],
        grid=(npair, nq // SWA_QB),
        in_specs=[pl.BlockSpec((SWA_QB * TQ, LANES), lambda j, i: (i, qcol + j)),
                  pl.BlockSpec((s, LANES), lambda j, i: (0, qcol + npair)),
                  pl.BlockSpec((s, LANES), lambda j, i: (0, qcol + npair + 1)),
                  pl.BlockSpec((SWA_QB * TQ, LANES), lambda j, i: (i, j)),
                  pl.BlockSpec((SWA_QB * TQ, LANES), lambda j, i: (i, j)),
                  pl.BlockSpec((TQ, 2 * BK), whole),
                  pl.BlockSpec(memory_space=pltpu.SMEM),
                  pl.BlockSpec(memory_space=pltpu.SMEM)],
        out_specs=[pl.BlockSpec((SWA_QB * TQ, LANES), lambda j, i: (i, j)),
                   pl.BlockSpec((s, LANES), whole), pl.BlockSpec((s, LANES), whole),
                   pl.BlockSpec((SW_HEADS, LANES), whole), pl.BlockSpec((SW_HEADS, LANES), whole)],
        scratch_shapes=[pltpu.VMEM((2 * TQ, 2 * BK), F32), pltpu.VMEM((2 * TQ, 2 * BK), F32)],
        compiler_params=_params(2), name=name)(qkv, qkv, qkv, ob, dob, bucket, sinks, rel_bias)


def _acc_init(i, *refs):
    @pl.when(i == 0)
    def _():
        for r in refs:
            r[...] = jnp.zeros_like(r)


def _loss_bwd(x3, target, g, name):
    d = x3.shape[1]

    def body(x_ref, t_ref, g_ref, dx_ref, dg_ref, loss_ref):
        _acc_init(pl.program_id(0), dg_ref, loss_ref)
        x = x_ref[...]
        gv = g_ref[...]
        diff = _rms(x, gv) - t_ref[...]
        loss_ref[...] += 0.5 * jnp.sum(jnp.mean(jnp.square(diff), axis=-1, keepdims=True), axis=0, keepdims=True)
        dx, dg = _rms_bwd(diff * (1.0 / d), x, gv)
        dx_ref[...] = dx
        dg_ref[...] += dg

    return _rowcall(name, body, [x3, target], [g], [(d, F32)], [((1, d), F32), ((1, LANES), F32)])


def _ple_bwd(dx3, pe, gt, x2, g, wpg, name):
    d = x2.shape[1]

    def body(dx3_ref, pe_ref, gt_ref, x_ref, g_ref, w_ref, dpe_ref, dgt_ref, dx2_ref, dg_ref):
        _acc_init(pl.program_id(0), dg_ref)
        dx3 = dx3_ref[...]
        sg = _sigmoid(gt_ref[...])
        dpe_ref[...] = (dx3 * sg).astype(BF16)
        dgt = (dx3 * pe_ref[...] * sg * (1.0 - sg)).astype(BF16)
        dgt_ref[...] = dgt
        dx, dg = _rms_bwd(_dot_nt(dgt, w_ref[...]), x_ref[...], g_ref[...])
        dx2_ref[...] = dx3 + dx
        dg_ref[...] += dg

    return _rowcall(name, body, [dx3, pe, gt, x2], [g, wpg], [(d, BF16), (d, BF16), (d, F32)], [((1, d), F32)])


def _ff2_bwd(dx2, u, w2, name):
    d = dx2.shape[1]
    dff = u.shape[1]

    def body(dx_ref, u_ref, w_ref, du_ref, dxb_ref):
        dxb = dx_ref[...].astype(BF16)
        dxb_ref[...] = dxb
        du_ref[...] = (_dot_nt(dxb, w_ref[...]) * (2.0 * jnp.maximum(u_ref[...], 0.0))).astype(BF16)

    return _rowcall(name, body, [dx2, u], [w2], [(dff, BF16), (d, BF16)])


def _ff1_bwd(du, dx2, x1, g, w1, name):
    d = x1.shape[1]

    def body(du_ref, dx2_ref, x_ref, g_ref, w_ref, dx1_ref, dx1b_ref, dg_ref):
        _acc_init(pl.program_id(0), dg_ref)
        dx, dg = _rms_bwd(_dot_cols_t(du_ref[...], w_ref), x_ref[...], g_ref[...])
        dx1 = dx2_ref[...] + dx
        dx1_ref[...] = dx1
        dx1b_ref[...] = dx1.astype(BF16)
        dg_ref[...] += dg

    return _rowcall(name, body, [du, dx2, x1], [g, w1], [(d, F32), (d, BF16)], [((1, d), F32)])


def _mixer_bwd(dx1b, gates, oa, ob, wo, wua, wub, name):
    d = dx1b.shape[1]

    def body(dx_ref, gate_ref, oa_ref, ob_ref, wo_ref, wua_ref, wub_ref,
             dya_ref, dyb_ref, dgate_ref, doa_ref, dob_ref):
        dm = _dot_nt(dx_ref[...], wo_ref[...])
        sa = _sigmoid(gate_ref[:, :d])
        sb = _sigmoid(gate_ref[:, d:])
        ya = _dot_cols(oa_ref[...], wua_ref)
        yb = _dot_cols(ob_ref[...], wub_ref)
        dya = (dm * sa).astype(BF16)
        dyb = (dm * sb).astype(BF16)
        dya_ref[...] = dya
        dyb_ref[...] = dyb
        dgate_ref[:, :d] = (dm * ya * sa * (1.0 - sa)).astype(BF16)
        dgate_ref[:, d:] = (dm * yb * sb * (1.0 - sb)).astype(BF16)
        doa_ref[...] = _dot_cols_t(dya, wua_ref).astype(BF16)
        dob_ref[...] = _dot_cols_t(dyb, wub_ref).astype(BF16)

    return _rowcall(name, body, [dx1b, gates, oa, ob], [wo, wua, wub],
                    [(d, BF16), (d, BF16), (2 * d, BF16), (SB_W, BF16), (SW_QW, BF16)])


def _inproj_bwd(dqkv, dgates, dx1, x, g, w, name, side=None):
    d = x.shape[1]

    def body(dqkv_ref, dgate_ref, dx1_ref, x_ref, g_ref, w_ref, dx_ref, dg_ref):
        _acc_init(pl.program_id(0), dg_ref)
        dh = _dot(dqkv_ref[...], w_ref[:QKV_W, :]) + _dot(dgate_ref[...], w_ref[QKV_W:, :])
        dx, dg = _rms_bwd(dh, x_ref[...], g_ref[...])
        dx_ref[...] = dx1_ref[...] + dx
        dg_ref[...] += dg

    return _rowcall(name, body, [dqkv, dgates, dx1, x], [g, w], [(d, F32)], [((1, d), F32)], side=side)


def _tile(n, cap):
    assert n % LANES == 0
    return max(t for t in range(LANES, min(n, cap) + 1, LANES) if n % t == 0)


def _mm_tn(a, b, name, nshard=1):
    s, ka = a.shape
    nb = b.shape[1]
    n = nb // nshard
    ta = _tile(ka, 512)
    tb = _tile(n, 1024)
    per = n // tb

    def body(a_ref, b_ref, o_ref):
        o_ref[...] = _dot_tn(a_ref[...].astype(BF16), b_ref[...].astype(BF16))

    return pl.pallas_call(
        body, out_shape=jax.ShapeDtypeStruct((nshard, ka, n), F32), grid=(nb // tb, ka // ta),
        in_specs=[pl.BlockSpec((s, ta), lambda jb, ia: (0, ia)), pl.BlockSpec((s, tb), lambda jb, ia: (0, jb))],
        out_specs=pl.BlockSpec((None, ta, tb), lambda jb, ia: (jb // per, ia, jb % per)),
        compiler_params=_params(2), name=name)(a, b)


def _mm_tn_pieces(pieces, b, name):
    s, nb = b.shape
    ta = 256
    tiles = [pc.shape[1] // ta for pc in pieces]
    assert all(pc.shape[1] % ta == 0 for pc in pieces)
    starts = [sum(tiles[:p]) for p in range(len(pieces))]
    tb = _tile(nb, 512)

    def body(*refs):
        a_refs, b_ref, o_ref = refs[:-2], refs[-2], refs[-1]
        ia = pl.program_id(1)
        a = a_refs[0][...]
        for p in range(1, len(pieces)):
            a = jnp.where(ia >= starts[p], a_refs[p][...], a)
        o_ref[...] = _dot_tn(a, b_ref[...])

    def piece_spec(p):
        return pl.BlockSpec((s, ta), lambda jb, ia: (0, jnp.clip(ia - starts[p], 0, tiles[p] - 1)))

    return pl.pallas_call(
        body, out_shape=jax.ShapeDtypeStruct((sum(tiles) * ta, nb), F32), grid=(nb // tb, sum(tiles)),
        in_specs=[piece_spec(p) for p in range(len(pieces))] + [pl.BlockSpec((s, tb), lambda jb, ia: (0, jb))],
        out_specs=pl.BlockSpec((ta, tb), lambda jb, ia: (ia, jb)),
        compiler_params=_params(2), name=name)(*pieces, b)


def _place():
    return lax.axis_index("x"), lax.axis_index("y"), lax.axis_index("c")


def _chip_peer(x, y, k):
    return (x ^ (k >> 1), y ^ (k & 1))


def _row_tile(k, cap=544):
    return max(t for t in range(32, min(k, cap) + 1, 32) if k % t == 0)


def _cast_bf16(w, r, name):
    l, k, n = w.shape
    assert l == 2
    tk = _row_tile(k)

    def body(r_ref, w_ref, o0_ref, o1_ref):
        o0_ref[...] = w_ref[0].astype(BF16)
        o1_ref[...] = w_ref[1].astype(BF16)

    out_spec = pl.BlockSpec((None, tk, n), lambda i, r_ref: (r_ref[0], i, 0))
    return pl.pallas_call(
        body, out_shape=[jax.ShapeDtypeStruct((4, k, n), BF16)] * 2,
        grid_spec=pltpu.PrefetchScalarGridSpec(
            num_scalar_prefetch=1, grid=(k // tk,),
            in_specs=[pl.BlockSpec((l, tk, n), lambda i, r_ref: (0, i, 0))],
            out_specs=[out_spec, out_spec]),
        compiler_params=_params(1), name=name)(r, w)


class _Exchange(NamedTuple):
    arrays: tuple
    aliased: tuple
    sems: tuple
    start: Callable
    finish: Callable


def _all_gather(bufs):
    nt = len(bufs)

    def ici(t, ins, outs, send_sems, recv_sems, x, y, c, r, k):
        return pltpu.make_async_remote_copy(
            src_ref=ins[t].at[r, c], dst_ref=outs[t].at[r, c], send_sem=send_sems.at[t, k - 1],
            recv_sem=recv_sems.at[t, k - 1], device_id=(*_chip_peer(x, y, k), c), device_id_type=MESH)

    def d2d(t, outs, send_sems, recv_sems, x, y, c, r, k, half):
        slab = outs[t].at[r ^ k, half]
        return pltpu.make_async_remote_copy(
            src_ref=slab, dst_ref=slab, send_sem=send_sems.at[t, 2 + k], recv_sem=recv_sems.at[t, 2 + k],
            device_id=(x, y, 1 - c), device_id_type=MESH)

    def start(ins, outs, send_sems, recv_sems):
        x, y, c = _place()
        r = 2 * x + y
        for t in range(nt):
            for k in (1, 2, 3):
                ici(t, ins, outs, send_sems, recv_sems, x, y, c, r, k).start()

    def finish(ins, outs, send_sems, recv_sems):
        x, y, c = _place()
        r = 2 * x + y
        for t in range(nt):
            for k in (1, 2, 3):
                slab = outs[t].at[r ^ k, c]
                pltpu.make_async_remote_copy(
                    src_ref=slab, dst_ref=slab, send_sem=send_sems.at[t, k - 1], recv_sem=recv_sems.at[t, k - 1],
                    device_id=(x, y, 1 - c), device_id_type=MESH).wait_recv()
                d2d(t, outs, send_sems, recv_sems, x, y, c, r, k, c).start()
        for t in range(nt):
            for k in (1, 2, 3):
                d2d(t, outs, send_sems, recv_sems, x, y, c, r, k, 1 - c).wait_recv()
        for t in range(nt):
            for k in (1, 2, 3):
                ici(t, ins, outs, send_sems, recv_sems, x, y, c, r, k).wait_send()
                d2d(t, outs, send_sems, recv_sems, x, y, c, r, k, c).wait_send()

    return _Exchange(tuple(bufs), tuple(range(nt)), (nt, 6), start, finish)


def _run_exchange(name, ex):
    n_in, n_out = len(ex.arrays), len(ex.aliased)

    def body(*refs):
        ins, outs = refs[:n_in], refs[n_in:n_in + n_out]
        send_sems, recv_sems = refs[n_in + n_out:]
        ex.start(ins, outs, send_sems, recv_sems)
        ex.finish(ins, outs, send_sems, recv_sems)

    any_spec = pl.BlockSpec(memory_space=pl.ANY)
    return pl.pallas_call(
        body, out_shape=[jax.ShapeDtypeStruct(ex.arrays[a].shape, ex.arrays[a].dtype) for a in ex.aliased],
        in_specs=[any_spec] * n_in, out_specs=[any_spec] * n_out,
        input_output_aliases={a: o for o, a in enumerate(ex.aliased)},
        scratch_shapes=[pltpu.SemaphoreType.DMA(ex.sems), pltpu.SemaphoreType.DMA(ex.sems)],
        name=name)(*ex.arrays)


def _rs_to_sibling(grads, name):
    nt = len(grads)

    def body(*refs):
        ins, outs = refs[:nt], refs[nt:2 * nt]
        send_sems, recv_sems = refs[2 * nt:]
        x, y, c = _place()
        sibling = (x, y, 1 - c)
        cps = []
        for t in range(nt):
            cp = pltpu.make_async_remote_copy(
                src_ref=ins[t].at[:, 1 - c], dst_ref=outs[t], send_sem=send_sems.at[t], recv_sem=recv_sems.at[t],
                device_id=sibling, device_id_type=MESH)
            cp.start()
            cps.append(cp)
        for cp in cps:
            cp.wait()

    any_spec = pl.BlockSpec(memory_space=pl.ANY)
    return pl.pallas_call(
        body, out_shape=[jax.ShapeDtypeStruct((4,) + g.shape[2:], F32) for g in grads],
        in_specs=[any_spec] * nt, out_specs=[any_spec] * nt,
        scratch_shapes=[pltpu.SemaphoreType.DMA((nt,)), pltpu.SemaphoreType.DMA((nt,))],
        name=name)(*grads)


def _add_half(g, recv, cr, name):
    _, _, k2, n = g.shape
    tk = _row_tile(k2)

    def body(cr_ref, g_ref, r_ref, sums_ref, mine_ref):
        val = (g_ref[...] + r_ref[...]).astype(BF16)
        sums_ref[...] = val

        @pl.when(pl.program_id(1) == cr_ref[1])
        def _():
            mine_ref[...] = val

    return pl.pallas_call(
        body, out_shape=[jax.ShapeDtypeStruct((4, k2, n), BF16)] * 2,
        grid_spec=pltpu.PrefetchScalarGridSpec(
            num_scalar_prefetch=1, grid=(k2 // tk, 4),
            in_specs=[pl.BlockSpec((None, None, tk, n), lambda i, q, cr_ref: (q, cr_ref[0], i, 0)),
                      pl.BlockSpec((None, tk, n), lambda i, q, cr_ref: (q, i, 0))],
            out_specs=[pl.BlockSpec((None, tk, n), lambda i, q, cr_ref: (q, i, 0)),
                       pl.BlockSpec((None, tk, n), lambda i, q, cr_ref: (cr_ref[1], i, 0))]),
        compiler_params=_params(2), name=name)(cr, g, recv)


def _rs_to_chips(sums, parts):
    nt = len(sums)

    def copies(ins, outs, send_sems, recv_sems):
        x, y, c = _place()
        r = 2 * x + y
        return [pltpu.make_async_remote_copy(
            src_ref=ins[t].at[r ^ k], dst_ref=outs[t].at[r], send_sem=send_sems.at[t, k - 1],
            recv_sem=recv_sems.at[t, k - 1], device_id=(*_chip_peer(x, y, k), c), device_id_type=MESH)
            for t in range(nt) for k in (1, 2, 3)]

    def start(ins, outs, send_sems, recv_sems):
        for cp in copies(ins, outs, send_sems, recv_sems):
            cp.start()

    def finish(ins, outs, send_sems, recv_sems):
        for cp in copies(ins, outs, send_sems, recv_sems):
            cp.wait()

    return _Exchange(tuple(sums) + tuple(parts), tuple(range(nt, 2 * nt)), (nt, 3), start, finish)


def _sum4(parts, cr, name):
    _, k2, n = parts.shape
    tk = _row_tile(k2)

    def body(cr_ref, p_ref, o_ref):
        p = p_ref[...].astype(F32)
        o_ref[...] = ((p[0] + p[1]) + p[2]) + p[3]

    return pl.pallas_call(
        body, out_shape=jax.ShapeDtypeStruct((2, k2, n), F32),
        grid_spec=pltpu.PrefetchScalarGridSpec(
            num_scalar_prefetch=1, grid=(k2 // tk,),
            in_specs=[pl.BlockSpec((4, tk, n), lambda i, cr_ref: (0, i, 0))],
            out_specs=pl.BlockSpec((None, tk, n), lambda i, cr_ref: (cr_ref[0], i, 0))),
        compiler_params=_params(1), name=name)(cr, parts)


def _exchange_halves(both, name):
    nt = len(both)

    def body(*refs):
        ins, outs = refs[:nt], refs[nt:2 * nt]
        send_sems, recv_sems = refs[2 * nt:]
        x, y, c = _place()
        cps = []
        for t in range(nt):
            cp = pltpu.make_async_remote_copy(
                src_ref=ins[t].at[c], dst_ref=outs[t].at[c], send_sem=send_sems.at[t], recv_sem=recv_sems.at[t],
                device_id=(x, y, 1 - c), device_id_type=MESH)
            cp.start()
            cps.append(cp)
        for cp in cps:
            cp.wait()

    any_spec = pl.BlockSpec(memory_space=pl.ANY)
    return pl.pallas_call(
        body, out_shape=[jax.ShapeDtypeStruct(b.shape, F32) for b in both],
        in_specs=[any_spec] * nt, out_specs=[any_spec] * nt,
        input_output_aliases={t: t for t in range(nt)},
        scratch_shapes=[pltpu.SemaphoreType.DMA((nt,)), pltpu.SemaphoreType.DMA((nt,))],
        name=name)(*both)


def _adamw_math(w, g, m, v):
    m = ADAM_B1 * m + (1.0 - ADAM_B1) * g
    v = ADAM_B2 * v + (1.0 - ADAM_B2) * jnp.square(g)
    m_hat = m / (1.0 - ADAM_B1 ** ADAM_STEP)
    v_hat = v / (1.0 - ADAM_B2 ** ADAM_STEP)
    delta = -ADAM_LR * (m_hat / (jnp.sqrt(v_hat) + ADAM_EPS) + ADAM_WD * w)
    return delta, m, v


def _adamw(w, m, v, g0, g1, name):
    _, k, n = w.shape
    tk = _row_tile(k)
    nk = k // tk

    def body(w_ref, m_ref, v_ref, g0_ref, g1_ref, grad_ref, delta_ref, nm_ref, nv_ref):
        g = jnp.where(pl.program_id(0) == 0, g0_ref[...], g1_ref[...])
        delta, nm, nv = _adamw_math(w_ref[...], g, m_ref[...], v_ref[...])
        grad_ref[...] = g
        delta_ref[...] = delta
        nm_ref[...] = nm
        nv_ref[...] = nv

    lay = pl.BlockSpec((None, tk, n), lambda a, i: (a, i, 0))
    g0_spec = pl.BlockSpec((tk, n), lambda a, i: (jnp.where(a == 0, i, nk - 1), 0))
    g1_spec = pl.BlockSpec((tk, n), lambda a, i: (jnp.where(a == 1, i, 0), 0))
    return pl.pallas_call(
        body, out_shape=[jax.ShapeDtypeStruct(w.shape, F32)] * 4, grid=(2, nk),
        in_specs=[lay, lay, lay, g0_spec, g1_spec], out_specs=[lay] * 4,
        compiler_params=_params(2), name=name)(w, m, v, g0, g1)


def _small_allreduce_adamw(gpart, w, m, v):
    shape = gpart.shape

    def body(g_ref, w_ref, m_ref, v_ref, gsum_ref, delta_ref, nm_ref, nv_ref, recv_ref, send_sems, recv_sems):
        x, y, c = _place()
        me = 4 * x + 2 * y + c
        recv_ref[me] = g_ref[...]
        cps = []
        for k in range(1, 8):
            peer = (x ^ (k >> 2), y ^ ((k >> 1) & 1), c ^ (k & 1))
            cp = pltpu.make_async_remote_copy(
                src_ref=g_ref, dst_ref=recv_ref.at[me], send_sem=send_sems.at[k - 1], recv_sem=recv_sems.at[k - 1],
                device_id=peer, device_id_type=MESH)
            cp.start()
            cps.append(cp)
        for cp in cps:
            cp.wait()
        g = recv_ref[0]
        for dev in range(1, 8):
            g = g + recv_ref[dev]
        delta, nm, nv = _adamw_math(w_ref[...], g, m_ref[...], v_ref[...])
        gsum_ref[...] = g
        delta_ref[...] = delta
        nm_ref[...] = nm
        nv_ref[...] = nv

    vm = pl.BlockSpec(memory_space=pltpu.VMEM)
    return pl.pallas_call(
        body, out_shape=[jax.ShapeDtypeStruct(shape, F32)] * 4, in_specs=[vm] * 4, out_specs=[vm] * 4,
        scratch_shapes=[pltpu.VMEM((8,) + shape, F32), pltpu.SemaphoreType.DMA((7,)), pltpu.SemaphoreType.DMA((7,))],
        name="small_allreduce_adamw")(gpart, w, m, v)


BIG = ("w_in", "w_up_a", "w_up_b", "w_o", "w_ff1", "w_ff2", "w_pe", "w_pg")
COL_SHARDED = ("w_in", "w_up_a", "w_up_b", "w_ff1", "w_pe")
ROW_SHARDED = ("w_o", "w_ff2", "w_pg")
SMALL_ROWS = 16


def _pack_small(g_mix, g_mlp, g_pe, g_final, sinks, rel_bias, loss=None):
    d = g_final.shape[0]
    row = lambda v: jnp.pad(v.reshape(1, -1), ((0, 0), (0, d - v.size)))
    rows = [g_mix, g_mlp, g_pe, g_final.reshape(1, d),
            jnp.zeros((1, d), F32) if loss is None else row(loss), row(sinks), row(rel_bias)]
    out = jnp.concatenate(rows, axis=0)
    return jnp.pad(out, ((0, SMALL_ROWS - out.shape[0]), (0, 0)))


def _unpack_small(a, sinks_shape, rel_shape):
    return (a[0:2], a[2:4], a[4:6], a[6], a[8, :sinks_shape[0] * sinks_shape[1]].reshape(sinks_shape),
            a[9, :rel_shape[0] * rel_shape[1]].reshape(rel_shape))


def kernel(x, p, w_in, w_up_a, w_up_b, w_o, w_ff1, w_ff2, w_pe, w_pg, g_mix, g_mlp, g_pe, g_final, sinks, rel_bias, loss_target, m_w_in, m_w_up_a, m_w_up_b, m_w_o, m_w_ff1, m_w_ff2, m_w_pe, m_w_pg, m_g_mix, m_g_mlp, m_g_pe, m_g_final, m_sinks, m_rel_bias, v_w_in, v_w_up_a, v_w_up_b, v_w_o, v_w_ff1, v_w_ff2, v_w_pe, v_w_pg, v_g_mix, v_g_mlp, v_g_pe, v_g_final, v_sinks, v_rel_bias):
    depth = w_in.shape[0]
    assert depth == 2
    x0 = x[0]
    target = loss_target[0]
    d = x0.shape[1]
    wl = dict(w_in=w_in, w_up_a=w_up_a, w_up_b=w_up_b, w_o=w_o, w_ff1=w_ff1, w_ff2=w_ff2, w_pe=w_pe, w_pg=w_pg)
    ml = dict(w_in=m_w_in, w_up_a=m_w_up_a, w_up_b=m_w_up_b, w_o=m_w_o, w_ff1=m_w_ff1, w_ff2=m_w_ff2, w_pe=m_w_pe, w_pg=m_w_pg)
    vl = dict(w_in=v_w_in, w_up_a=v_w_up_a, w_up_b=v_w_up_b, w_o=v_w_o, w_ff1=v_w_ff1, w_ff2=v_w_ff2, w_pe=v_w_pe, w_pg=v_w_pg)
    c_idx = lax.axis_index("c").astype(jnp.int32)
    r_idx = (2 * lax.axis_index("x") + lax.axis_index("y")).astype(jnp.int32)
    cr = jnp.stack([c_idx, r_idx])

    wl["w_in"], ml["w_in"], vl["w_in"] = (jnp.swapaxes(a, 1, 2) for a in (w_in, m_w_in, v_w_in))

    bufs = {}
    for n in BIG:
        k, nn = wl[n].shape[1:]
        for l, b in enumerate(_cast_bf16(wl[n], r_idx.reshape(1), "cast_" + n)):
            bufs[n, l] = b.reshape(4, 2, k // 2, nn)

    def gather(keys, run):
        for key, b in zip(keys, run(_all_gather([bufs[key] for key in keys]))):
            bufs[key] = b

    def gathered(n, l):
        _, _, k2, nn = bufs[n, l].shape
        if n in ROW_SHARDED or n == "w_in":
            return bufs[n, l].reshape(8 * k2, nn)
        return bufs[n, l].reshape(4, 2 * k2, nn)

    gather([("w_in", 0)], lambda ex: _run_exchange("all_gather_first", ex))

    full = {n: [None] * depth for n in BIG}
    saved = []
    xi = x0
    for i in range(depth):
        st = dict(x0=xi)
        gm = g_mix[i].reshape(1, d)
        full["w_in"][i] = gathered("w_in", i)
        st["h1"], st["qkv"], st["gates"] = _inproj_fwd(xi, gm, full["w_in"][i], f"inproj_fwd_{i}")

        def attend(ex):
            st["oa"], st["lt"], st["nb"], filled = _sb_fwd(st["qkv"], f"sb_fwd_{i}", ex)
            return filled

        gather([(n, i) for n in BIG if n != "w_in"], attend)
        for n in BIG:
            if n != "w_in":
                full[n][i] = gathered(n, i)
        st["ob"] = _swa_fwd(st["qkv"], sinks[i], rel_bias, f"swa_fwd_{i}")
        st["m"], st["x1"] = _mixer_fwd(st["oa"], st["ob"], st["gates"], xi, full["w_up_a"][i], full["w_up_b"][i],
                                       full["w_o"][i], f"mixer_fwd_{i}")
        if i == 0:
            def feed_forward(ex):
                (st["h2"], st["u"], st["a"]), filled = _ff1_fwd(st["x1"], g_mlp[i].reshape(1, d), full["w_ff1"][i],
                                                               f"ff1_fwd_{i}", ex)
                return filled

            gather([("w_in", 1)], feed_forward)
        else:
            st["h2"], st["u"], st["a"] = _ff1_fwd(st["x1"], g_mlp[i].reshape(1, d), full["w_ff1"][i], f"ff1_fwd_{i}")
        st["x2"] = _ff2_fwd(st["a"], st["x1"], full["w_ff2"][i], f"ff2_fwd_{i}")
        st["pb"], st["h3"], st["pe"], st["gt"], xi = _ple_fwd(p[i, 0], st["x2"], g_pe[i].reshape(1, d),
                                                            full["w_pe"][i], full["w_pg"][i], f"ple_fwd_{i}")
        saved.append(st)

    dx, dg_final, loss_part = _loss_bwd(xi, target, g_final.reshape(1, d), "loss_bwd")
    gw = {n: [None] * depth for n in BIG}
    reduced = {}

    def scatter_front(keys, tag):
        tensors = []
        for n, l in keys:
            g = gw[n][l]
            if n in ROW_SHARDED:
                ka, nb = g.shape[1:]
                g = g.reshape(4, ka // 4, nb)
            _, k, nn = g.shape
            tensors.append(g.reshape(4, 2, k // 2, nn))
        from_sibling = _rs_to_sibling(tensors, f"rs_to_sibling_{tag}")
        added = [_add_half(g, r, cr, f"add_half_{n}_{l}") for (n, l), g, r in zip(keys, tensors, from_sibling)]
        return _rs_to_chips([a[0] for a in added], [a[1] for a in added])

    def scatter_back(keys, parts, tag):
        halves = [_sum4(pc, cr, f"sum4_{n}_{l}") for (n, l), pc in zip(keys, parts)]
        for key, both in zip(keys, _exchange_halves(halves, f"exchange_halves_{tag}")):
            reduced[key] = both

    dg_mix, dg_mlp, dg_pe, dsinks = [None] * depth, [None] * depth, [None] * depth, [None] * depth
    drel = jnp.zeros((SW_HEADS, LANES), F32)
    for i in reversed(range(depth)):
        st = saved[i]
        dpe, dgt, dx2, dg_pe[i] = _ple_bwd(dx, st["pe"], st["gt"], st["x2"], g_pe[i].reshape(1, d),
                                           full["w_pg"][i], f"ple_bwd_{i}")
        gw["w_pe"][i] = _mm_tn(st["pb"], dpe, f"dw_pe_{i}", 4)
        gw["w_pg"][i] = _mm_tn(st["h3"], dgt, f"dw_pg_{i}")
        du, dx2b = _ff2_bwd(dx2, st["u"], full["w_ff2"][i], f"ff2_bwd_{i}")
        gw["w_ff2"][i] = _mm_tn(st["a"], dx2b, f"dw_ff2_{i}")
        gw["w_ff1"][i] = _mm_tn(st["h2"], du, f"dw_ff1_{i}", 4)
        dx1, dx1b, dg_mlp[i] = _ff1_bwd(du, dx2, st["x1"], g_mlp[i].reshape(1, d), full["w_ff1"][i], f"ff1_bwd_{i}")
        gw["w_o"][i] = _mm_tn(st["m"], dx1b, f"dw_o_{i}")
        dya, dyb, dgates, doa, dob = _mixer_bwd(dx1b, st["gates"], st["oa"], st["ob"], full["w_o"][i],
                                                full["w_up_a"][i], full["w_up_b"][i], f"mixer_bwd_{i}")
        gw["w_up_a"][i] = _mm_tn(st["oa"], dya, f"dw_up_a_{i}", 4)
        gw["w_up_b"][i] = _mm_tn(st["ob"], dyb, f"dw_up_b_{i}", 4)
        keys = [(n, i) for n in BIG if n != "w_in"] + ([("w_in", 1)] if i == 0 else [])
        dqa, dka, dva, parts = _sb_bwd(st["qkv"], st["lt"], st["nb"], doa, f"sb_bwd_{i}", scatter_front(keys, i))
        scatter_back(keys, parts, i)
        dqb, dkb, dvb, dsk, drl = _swa_bwd(st["qkv"], st["ob"], dob, sinks[i], rel_bias, f"swa_bwd_{i}")
        dsinks[i] = dsk[:, 0]
        drel = drel + drl
        dqkv = jnp.concatenate([dqa, dka.astype(BF16), dva.astype(BF16), dqb, dkb.astype(BF16), dvb.astype(BF16)],
                               axis=1)
        dw_in_t = _mm_tn_pieces([dqkv, dgates], st["h1"], f"dw_in_{i}")
        gw["w_in"][i] = dw_in_t.reshape(4, dw_in_t.shape[0] // 4, d)
        if i == 1:
            dx, dg_mix[i] = _inproj_bwd(dqkv, dgates, dx1, st["x0"], g_mix[i].reshape(1, d), full["w_in"][i],
                                        f"inproj_bwd_{i}")
        else:
            keys = [("w_in", 0)]
            (dx, dg_mix[i]), parts = _inproj_bwd(dqkv, dgates, dx1, st["x0"], g_mix[i].reshape(1, d),
                                                 full["w_in"][i], f"inproj_bwd_{i}", scatter_front(keys, "last"))
            scatter_back(keys, parts, "last")
    grad_x = dx[None]

    outs = {}
    for n in BIG:
        g0, g1 = (reduced[n, l].reshape(wl[n].shape[1:]) for l in range(depth))
        outs[n] = _adamw(wl[n], ml[n], vl[n], g0, g1, "adamw_" + n)
    outs["w_in"] = [jnp.swapaxes(a, 1, 2) for a in outs["w_in"]]

    drel_bias = drel[:, :N_BUCKETS].T
    gsmall = _pack_small(jnp.concatenate(dg_mix, 0), jnp.concatenate(dg_mlp, 0), jnp.concatenate(dg_pe, 0),
                         dg_final[0], jnp.stack(dsinks), drel_bias, loss_part[0, :1])
    wsmall = _pack_small(g_mix, g_mlp, g_pe, g_final, sinks, rel_bias)
    msmall = _pack_small(m_g_mix, m_g_mlp, m_g_pe, m_g_final, m_sinks, m_rel_bias)
    vsmall = _pack_small(v_g_mix, v_g_mlp, v_g_pe, v_g_final, v_sinks, v_rel_bias)
    small = _small_allreduce_adamw(gsmall, wsmall, msmall, vsmall)
    loss = small[0][7, 0]
    small = [_unpack_small(a, sinks.shape, rel_bias.shape) for a in small]

    result = [loss, grad_x]
    for kind in range(4):
        result += [outs[n][kind] for n in BIG]
        result += list(small[kind])
    return tuple(result)
```

```python
import functools
import math
from typing import Callable, NamedTuple

import numpy as np
import jax
import jax.numpy as jnp
from jax import lax
from jax.experimental import pallas as pl
from jax.experimental.pallas import tpu as pltpu

F32 = jnp.float32
BF16 = jnp.bfloat16
MESH = pl.DeviceIdType.MESH

HEAD_DIM = 64
SB_HEADS = 8
SW_HEADS = 8
SW_KV_HEADS = 2
WINDOW = 128
N_BUCKETS = 32
MAX_DISTANCE = 128
EPS = 1e-6
SB_W = SB_HEADS * HEAD_DIM
SW_QW = SW_HEADS * HEAD_DIM
SW_KVW = SW_KV_HEADS * HEAD_DIM
QKV_W = 3 * SB_W + SW_QW + 2 * SW_KVW
SCALE = HEAD_DIM ** -0.5
LANES = 128
TQ = 128
BK = 128
NEG = -1e30
SB_EXHAUSTED = -106.0

ADAM_LR = 0.001
ADAM_B1 = 0.9
ADAM_B2 = 0.999
ADAM_EPS = 1e-08
ADAM_WD = 0.01
ADAM_STEP = 10

VMEM_LIMIT = 56 * 1024 * 1024


def _dot(a, b):
    return jnp.dot(a, b, preferred_element_type=F32)


def _dot_nt(a, b):
    return lax.dot_general(a, b, (((1,), (1,)), ((), ())), preferred_element_type=F32)


def _dot_tn(a, b):
    return lax.dot_general(a, b, (((0,), (0,)), ((), ())), preferred_element_type=F32)


def _sum_all(x):
    return jnp.sum(jnp.sum(x, axis=1, keepdims=True), axis=0, keepdims=True)


def _sigmoid(x):
    return 1.0 / (1.0 + jnp.exp(-x))


def _rms(x, g):
    r = lax.rsqrt(jnp.mean(x * x, axis=-1, keepdims=True) + EPS)
    return (x * r) * g


def _rms_bwd(dy, x, g):
    r = lax.rsqrt(jnp.mean(x * x, axis=-1, keepdims=True) + EPS)
    n = x * r
    dg = jnp.sum(dy * n, axis=0, keepdims=True)
    dn = dy * g
    dx = r * (dn - n * jnp.mean(dn * n, axis=-1, keepdims=True))
    return dx, dg


def _params(n_axes):
    return pltpu.CompilerParams(dimension_semantics=("arbitrary",) * n_axes, vmem_limit_bytes=VMEM_LIMIT)


def _rowcall(name, body, row_ins, const_ins, row_outs, acc_outs=(), tm=512, side=None):
    s = row_ins[0].shape[0]
    assert s % tm == 0
    in_specs = [pl.BlockSpec((tm, a.shape[1]), lambda i: (i, 0)) for a in row_ins]
    in_specs += [pl.BlockSpec(a.shape, functools.partial(lambda i, nd: (0,) * nd, nd=a.ndim)) for a in const_ins]
    out_shape = [jax.ShapeDtypeStruct((s, c), dt) for c, dt in row_outs]
    out_specs = [pl.BlockSpec((tm, c), lambda i: (i, 0)) for c, _ in row_outs]
    out_shape += [jax.ShapeDtypeStruct(sh, dt) for sh, dt in acc_outs]
    out_specs += [pl.BlockSpec(sh, functools.partial(lambda i, nd: (0,) * nd, nd=len(sh))) for sh, _ in acc_outs]
    if side is None:
        return pl.pallas_call(body, out_shape=out_shape, grid=(s // tm,), in_specs=in_specs, out_specs=out_specs,
                              compiler_params=_params(1), name=name)(*row_ins, *const_ins)
    n_in, n_out = len(in_specs), len(out_specs)

    def with_side(*refs):
        side_in, outs, side_out, sems = _side_refs(side, refs[n_in:], n_out)

        @pl.when(pl.program_id(0) == 0)
        def _():
            side.start(side_in, side_out, *sems)

        body(*refs[:n_in], *outs)

        @pl.when(pl.program_id(0) == s // tm - 1)
        def _():
            side.finish(side_in, side_out, *sems)

    s_in, s_shape, s_out, s_alias, s_sems = _side_specs(side, n_in, n_out)
    outs = pl.pallas_call(with_side, out_shape=out_shape + s_shape, grid=(s // tm,), in_specs=in_specs + s_in,
                          out_specs=out_specs + s_out, input_output_aliases=s_alias, scratch_shapes=s_sems,
                          compiler_params=_params(1), name=name)(*row_ins, *const_ins, *side.arrays)
    return outs[:n_out], outs[n_out:]


def _dot_cols(a, w_ref):
    return jnp.concatenate([_dot(a, w_ref[r]) for r in range(w_ref.shape[0])], axis=1)


def _dot_cols_t(a, w_ref):
    n = w_ref.shape[2]
    out = _dot_nt(a[:, :n], w_ref[0])
    for r in range(1, w_ref.shape[0]):
        out = out + _dot_nt(a[:, r * n:(r + 1) * n], w_ref[r])
    return out


def _inproj_fwd(x, g, wt, name):
    d = x.shape[1]

    def body(x_ref, g_ref, w_ref, h_ref, qkv_ref, gate_ref):
        hb = _rms(x_ref[...], g_ref[...]).astype(BF16)
        h_ref[...] = hb
        qkv_ref[...] = _dot_nt(hb, w_ref[:QKV_W, :]).astype(BF16)
        gate_ref[...] = _dot_nt(hb, w_ref[QKV_W:, :])

    return _rowcall(name, body, [x], [g, wt], [(d, BF16), (QKV_W, BF16), (2 * d, F32)])


def _mixer_fwd(oa, ob, gates, x, wua, wub, wo, name):
    d = x.shape[1]

    def body(oa_ref, ob_ref, gate_ref, x_ref, wua_ref, wub_ref, wo_ref, m_ref, x1_ref):
        ya = _dot_cols(oa_ref[...], wua_ref)
        yb = _dot_cols(ob_ref[...], wub_ref)
        m = _sigmoid(gate_ref[:, :d]) * ya + _sigmoid(gate_ref[:, d:]) * yb
        mb = m.astype(BF16)
        m_ref[...] = mb
        x1_ref[...] = x_ref[...] + _dot(mb, wo_ref[...])

    return _rowcall(name, body, [oa, ob, gates, x], [wua, wub, wo], [(d, BF16), (d, F32)])


def _ff1_fwd(x1, g, w1, name, side=None):
    _, d, nq = w1.shape
    dff = 4 * nq

    def body(x_ref, g_ref, w_ref, h_ref, u_ref, a_ref):
        hb = _rms(x_ref[...], g_ref[...]).astype(BF16)
        h_ref[...] = hb
        u = _dot_cols(hb, w_ref)
        u_ref[...] = u
        a_ref[...] = jnp.square(jnp.maximum(u, 0.0)).astype(BF16)

    return _rowcall(name, body, [x1], [g, w1], [(d, BF16), (dff, F32), (dff, BF16)], side=side)


def _ff2_fwd(a, x1, w2, name):
    d = x1.shape[1]

    def body(a_ref, x_ref, w_ref, o_ref):
        o_ref[...] = x_ref[...] + _dot(a_ref[...], w_ref[...])

    return _rowcall(name, body, [a, x1], [w2], [(d, F32)])[0]


def _ple_fwd(p, x2, g, wpe, wpg, name):
    d = x2.shape[1]

    def body(p_ref, x_ref, g_ref, wpe_ref, wpg_ref, pb_ref, h_ref, pe_ref, gt_ref, x3_ref):
        pb = p_ref[...].astype(BF16)
        pb_ref[...] = pb
        pe = _dot_cols(pb, wpe_ref)
        x = x_ref[...]
        hb = _rms(x, g_ref[...]).astype(BF16)
        h_ref[...] = hb
        gt = _dot(hb, wpg_ref[...])
        pe_ref[...] = pe
        gt_ref[...] = gt
        x3_ref[...] = x + pe * _sigmoid(gt)

    return _rowcall(name, body, [p, x2], [g, wpe, wpg],
                    [(p.shape[1], BF16), (d, BF16), (d, F32), (d, F32), (d, F32)])


def _pair_stack(t, lane):
    zero = jnp.zeros_like(t)
    return jnp.concatenate([jnp.where(lane < HEAD_DIM, t, zero), jnp.where(lane >= HEAD_DIM, t, zero)], axis=0)


def _sb_rel():
    row = lax.broadcasted_iota(jnp.int32, (2 * TQ, BK), 0)
    row = jnp.where(row >= TQ, row - TQ, row)
    col = lax.broadcasted_iota(jnp.int32, (2 * TQ, BK), 1)
    return col - row


def _split_dot(x, m01, two_pass=True):
    hi = x.astype(BF16)
    if not two_pass:
        return _dot(hi, m01)
    lo = (x - hi.astype(F32)).astype(BF16)
    return _dot(hi, m01) + _dot(lo, m01)


def _sb_scores(qs, k, mask):
    z = _dot_nt(qs, k) * SCALE
    lb = jnp.minimum(z, 0.0) - jnp.log(1.0 + jnp.exp(-jnp.abs(z)))
    lm = lb - z
    return lb, lm if mask is None else jnp.where(mask, lm, 0.0)


SB_STRAIGHT = 3
SB_WIDE = SB_STRAIGHT * BK
SB_QB = 2
SWA_QB = 4


def _sb_wide_consts():
    j = np.arange(BK)[:, None]
    s = np.arange(BK)[None, :]
    ones = np.ones((BK, BK), np.float32)
    as_bf16 = lambda m: jnp.asarray(np.concatenate([m, ones], axis=1).astype(np.float32), dtype=BF16)
    return as_bf16(j > s), as_bf16(j <= s), as_bf16(j < s)


def _wide_sums(x, m01, suffix, two_pass=True):
    parts = [_split_dot(x[:, b * BK:(b + 1) * BK], m01, two_pass) for b in range(SB_STRAIGHT)]
    order = range(SB_STRAIGHT - 1, -1, -1) if suffix else range(SB_STRAIGHT)
    out = [None] * SB_STRAIGHT
    carry = None
    for b in order:
        out[b] = parts[b][:, :BK] if carry is None else parts[b][:, :BK] + carry
        carry = parts[b][:, BK:] if carry is None else carry + parts[b][:, BK:]
    return jnp.concatenate(out, axis=1), carry


def _side_refs(ex, rest, n_out):
    n_in = len(ex.arrays) if ex else 0
    n_alias = len(ex.aliased) if ex else 0
    ins, rest = rest[:n_in], rest[n_in:]
    outs, rest = rest[:n_out], rest[n_out:]
    return ins, outs, rest[:n_alias], rest[n_alias:]


def _side_specs(ex, n_in, n_out):
    if ex is None:
        return [], [], [], {}, []
    any_spec = pl.BlockSpec(memory_space=pl.ANY)
    return ([any_spec] * len(ex.arrays),
            [jax.ShapeDtypeStruct(ex.arrays[a].shape, ex.arrays[a].dtype) for a in ex.aliased],
            [any_spec] * len(ex.aliased), {n_in + a: n_out + o for o, a in enumerate(ex.aliased)},
            [pltpu.SemaphoreType.DMA(ex.sems), pltpu.SemaphoreType.DMA(ex.sems)])


def _sb_fwd(qkv, name, side=None):
    s = qkv.shape[0]
    nq = s // TQ
    npair = SB_W // LANES
    sufw = _sb_wide_consts()[0]

    def body(q_ref, k_ref, v_ref, sufw_ref, *rest):
        side_in, (o_ref, lt_ref, nb_ref), side_out, scratch = _side_refs(side, rest, 3)
        cf_ref, acc_ref = scratch[:2]
        step_id = pl.program_id(1)
        if side is not None:
            @pl.when((pl.program_id(0) == 0) & (step_id == 0))
            def _():
                side.start(side_in, side_out, *scratch[2:])
        lane = lax.broadcasted_iota(jnp.int32, (TQ, LANES), 1)
        rel = _sb_rel()
        blocks = [step_id * SB_QB + b for b in range(SB_QB)]
        qs = [_pair_stack(q_ref[b * TQ:(b + 1) * TQ, :], lane) for b in range(SB_QB)]

        straight = blocks[0] >= SB_STRAIGHT - 1

        @pl.when(straight)
        def _():
            for b, i in enumerate(blocks):
                w0 = pl.multiple_of((i - (SB_STRAIGHT - 1)) * BK, BK)
                kw = k_ref[pl.ds(w0, SB_WIDE), :]
                lb, lm = _sb_scores(qs[b], kw, None)
                own = rel < 0
                past = SB_WIDE - BK
                lm = jnp.concatenate([lm[:, :past], jnp.where(own, lm[:, past:], 0.0)], axis=1)
                after, total = _wide_sums(lm, sufw_ref[...], True)
                a = jnp.exp(lb + after)
                a = jnp.concatenate([a[:, :past], jnp.where(own, a[:, past:], 0.0)], axis=1)
                acc_ref[b] = _dot(a.astype(BF16), v_ref[pl.ds(w0, SB_WIDE), :])
                cf_ref[b] = total

        @pl.when(jnp.logical_not(straight))
        def _():
            cf_ref[...] = jnp.zeros_like(cf_ref)
            acc_ref[...] = jnp.zeros_like(acc_ref)

        for b, i in enumerate(blocks):
            q0 = i * TQ

            def more(c, i=i):
                return (c[0] <= i) & (c[1] > SB_EXHAUSTED)

            def step(c, b=b, i=i, q0=q0):
                k0 = pl.multiple_of((i - c[0]) * BK, BK)
                k = k_ref[pl.ds(k0, BK), :]
                v = v_ref[pl.ds(k0, BK), :]
                mask = rel < (q0 - k0)
                lb, lm = _sb_scores(qs[b], k, mask)
                cs = _split_dot(lm, sufw_ref[...])
                a = jnp.where(mask, jnp.exp(lb + cs[:, :BK] + cf_ref[b]), 0.0)
                acc_ref[b] += _dot(a.astype(BF16), v)
                cf = cf_ref[b] + cs[:, BK:]
                cf_ref[b] = cf
                return c[0] + 1, jnp.max(cf)

            n_blocks, _ = lax.while_loop(
                more, step, (jnp.where(straight, SB_STRAIGHT, 0).astype(jnp.int32), jnp.max(cf_ref[b])))
            o_ref[b * TQ:(b + 1) * TQ, :] = jnp.where(lane < HEAD_DIM, acc_ref[b, :TQ, :],
                                                     acc_ref[b, TQ:, :]).astype(BF16)
            lt_ref[b] = cf_ref[b]
            nb_ref[b] = jnp.full(nb_ref.shape[1:], n_blocks, F32)
        if side is not None:
            @pl.when((pl.program_id(0) == npair - 1) & (step_id == nq // SB_QB - 1))
            def _():
                side.finish(side_in, side_out, *scratch[2:])

    s_in, s_shape, s_out, s_alias, s_sems = _side_specs(side, 4, 3)
    outs = pl.pallas_call(
        body,
        out_shape=[jax.ShapeDtypeStruct((s, SB_W), BF16), jax.ShapeDtypeStruct((npair, nq, 2 * TQ, BK), F32),
                   jax.ShapeDtypeStruct((npair, nq, 8, LANES), F32)] + s_shape,
        grid=(npair, nq // SB_QB),
        in_specs=[pl.BlockSpec((SB_QB * TQ, LANES), lambda j, i: (i, j)),
                  pl.BlockSpec((s, LANES), lambda j, i: (0, npair + j)),
                  pl.BlockSpec((s, LANES), lambda j, i: (0, 2 * npair + j)),
                  pl.BlockSpec(sufw.shape, lambda j, i: (0, 0))] + s_in,
        out_specs=[pl.BlockSpec((SB_QB * TQ, LANES), lambda j, i: (i, j)),
                   pl.BlockSpec((None, SB_QB, 2 * TQ, BK), lambda j, i: (j, i, 0, 0)),
                   pl.BlockSpec((None, SB_QB, 8, LANES), lambda j, i: (j, i, 0, 0))] + s_out,
        input_output_aliases=s_alias,
        scratch_shapes=[pltpu.VMEM((SB_QB, 2 * TQ, BK), F32), pltpu.VMEM((SB_QB, 2 * TQ, LANES), F32)] + s_sems,
        compiler_params=_params(2), name=name)(qkv, qkv, qkv, sufw, *(side.arrays if side else ()))
    return outs[0], outs[1], outs[2], outs[3:]


def _sb_bwd(qkv, lt, nb, doa, name, side=None):
    s = qkv.shape[0]
    nq = s // TQ
    npair = SB_W // LANES
    _, prew, prexw = _sb_wide_consts()

    def body(q_ref, k_ref, v_ref, lt_ref, nb_ref, do_ref, prew_ref, prexw_ref, *rest):
        side_in, (dq_ref, dk_ref, dv_ref), side_out, scratch = _side_refs(side, rest, 3)
        cp_ref, ce_ref, dqa_ref = scratch[:3]
        step_id = pl.program_id(1)
        if side is not None:
            @pl.when((pl.program_id(0) == 0) & (step_id == 0))
            def _():
                side.start(side_in, side_out, *scratch[3:])
        lane = lax.broadcasted_iota(jnp.int32, (TQ, LANES), 1)
        rel = _sb_rel()
        blocks = [step_id * SB_QB + b for b in range(SB_QB)]
        rows = [slice(b * TQ, (b + 1) * TQ) for b in range(SB_QB)]
        qs = [_pair_stack(q_ref[rows[b], :], lane) for b in range(SB_QB)]
        dos = [_pair_stack(do_ref[rows[b], :], lane) for b in range(SB_QB)]
        n_blocks = [jnp.clip(jnp.max(nb_ref[b]).astype(jnp.int32), 1, i + 1) for b, i in enumerate(blocks)]
        first = [i + 1 - n for i, n in zip(blocks, n_blocks)]

        @pl.when(step_id == 0)
        def _():
            dk_ref[...] = jnp.zeros_like(dk_ref)
            dv_ref[...] = jnp.zeros_like(dv_ref)

        straight = n_blocks[0] == SB_STRAIGHT
        for n in n_blocks[1:]:
            straight = straight & (n == SB_STRAIGHT)

        @pl.when(straight)
        def _():
            for b in range(SB_QB):
                w0 = pl.multiple_of(first[b] * BK, BK)
                kw = k_ref[pl.ds(w0, SB_WIDE), :]
                vw = v_ref[pl.ds(w0, SB_WIDE), :]
                lb, lm = _sb_scores(qs[b], kw, None)
                own = rel < 0
                past = SB_WIDE - BK
                on_past_keys = lambda t: jnp.concatenate([t[:, :past], jnp.where(own, t[:, past:], 0.0)], axis=1)
                lm = on_past_keys(lm)
                upto, _ = _wide_sums(lm, prew_ref[...], False)
                lt = lt_ref[b]
                a = on_past_keys(jnp.exp(lb + (jnp.concatenate([lt] * SB_STRAIGHT, axis=1) - upto)))
                e = a * _dot_nt(dos[b], vw)
                big_e, _ = _wide_sums(e, prexw_ref[...], False, two_pass=False)
                dz = (on_past_keys(e - jnp.exp(lb) * (e + big_e)) * SCALE).astype(BF16)
                dk_ref[pl.ds(w0, SB_WIDE), :] += _dot_tn(dz, qs[b])
                dv_ref[pl.ds(w0, SB_WIDE), :] += _dot_tn(a.astype(BF16), dos[b])
                dqa_ref[b] = _dot(dz, kw)

        @pl.when(jnp.logical_not(straight))
        def _():
            cp_ref[...] = jnp.zeros_like(cp_ref)
            ce_ref[...] = jnp.zeros_like(ce_ref)
            dqa_ref[...] = jnp.zeros_like(dqa_ref)
            for b, i in enumerate(blocks):
                q0 = i * TQ

                def step(it, carry, b=b, q0=q0):
                    k0 = pl.multiple_of((first[b] + it) * BK, BK)
                    k = k_ref[pl.ds(k0, BK), :]
                    v = v_ref[pl.ds(k0, BK), :]
                    mask = rel < (q0 - k0)
                    lb, lm = _sb_scores(qs[b], k, mask)
                    cs = _split_dot(lm, prew_ref[...])
                    a = jnp.where(mask, jnp.exp(lb + (lt_ref[b] - (cs[:, :BK] + cp_ref[b]))), 0.0)
                    e = a * _dot_nt(dos[b], v)
                    ce = _split_dot(e, prexw_ref[...], two_pass=False)
                    big_e = ce[:, :BK] + ce_ref[b]
                    dz = (jnp.where(mask, e - jnp.exp(lb) * (e + big_e), 0.0) * SCALE).astype(BF16)
                    dk_ref[pl.ds(k0, BK), :] += _dot_tn(dz, qs[b])
                    dv_ref[pl.ds(k0, BK), :] += _dot_tn(a.astype(BF16), dos[b])
                    dqa_ref[b] += _dot(dz, k)
                    cp_ref[b] += cs[:, BK:]
                    ce_ref[b] += ce[:, BK:]
                    return carry

                lax.fori_loop(0, n_blocks[b], step, 0)

        for b in range(SB_QB):
            dq_ref[rows[b], :] = jnp.where(lane < HEAD_DIM, dqa_ref[b, :TQ, :], dqa_ref[b, TQ:, :]).astype(BF16)
        if side is not None:
            @pl.when((pl.program_id(0) == npair - 1) & (step_id == nq // SB_QB - 1))
            def _():
                side.finish(side_in, side_out, *scratch[3:])

    s_in, s_shape, s_out, s_alias, s_sems = _side_specs(side, 8, 3)
    outs = pl.pallas_call(
        body,
        out_shape=[jax.ShapeDtypeStruct((s, SB_W), BF16), jax.ShapeDtypeStruct((s, SB_W), F32),
                   jax.ShapeDtypeStruct((s, SB_W), F32)] + s_shape,
        grid=(npair, nq // SB_QB),
        in_specs=[pl.BlockSpec((SB_QB * TQ, LANES), lambda j, i: (i, j)),
                  pl.BlockSpec((s, LANES), lambda j, i: (0, npair + j)),
                  pl.BlockSpec((s, LANES), lambda j, i: (0, 2 * npair + j)),
                  pl.BlockSpec((None, SB_QB, 2 * TQ, BK), lambda j, i: (j, i, 0, 0)),
                  pl.BlockSpec((None, SB_QB, 8, LANES), lambda j, i: (j, i, 0, 0)),
                  pl.BlockSpec((SB_QB * TQ, LANES), lambda j, i: (i, j)),
                  pl.BlockSpec(prew.shape, lambda j, i: (0, 0)),
                  pl.BlockSpec(prexw.shape, lambda j, i: (0, 0))] + s_in,
        out_specs=[pl.BlockSpec((SB_QB * TQ, LANES), lambda j, i: (i, j)),
                   pl.BlockSpec((s, LANES), lambda j, i: (0, j)),
                   pl.BlockSpec((s, LANES), lambda j, i: (0, j))] + s_out,
        input_output_aliases=s_alias,
        scratch_shapes=[pltpu.VMEM((SB_QB, 2 * TQ, BK), F32), pltpu.VMEM((SB_QB, 2 * TQ, BK), F32),
                        pltpu.VMEM((SB_QB, 2 * TQ, LANES), F32)] + s_sems,
        compiler_params=_params(2), name=name)(qkv, qkv, qkv, lt, nb, doa, prew, prexw,
                                               *(side.arrays if side else ()))
    return outs[0], outs[1], outs[2], outs[3:]


def _bucket_table():
    i = np.arange(TQ)[:, None]
    j = np.arange(2 * BK)[None, :]
    dist = np.maximum(TQ + i - j, 0)
    max_exact = N_BUCKETS // 2
    df = np.maximum(dist, 1).astype(np.float32)
    large = max_exact + (np.log(df / np.float32(max_exact)) / np.float32(math.log(MAX_DISTANCE / max_exact))
                         * np.float32(N_BUCKETS - max_exact)).astype(np.int32)
    large = np.minimum(large, N_BUCKETS - 1)
    return np.where(dist < max_exact, dist, large).astype(np.int32)


def _swa_align_in(t, lane, g):
    tf = t.astype(F32)
    tr = pltpu.roll(tf, HEAD_DIM, 1)
    gmask = (lane >= HEAD_DIM) == (g == 1)
    top = jnp.where(gmask, jnp.where(g == 0, tf, tr), 0.0)
    bot = jnp.where(gmask, jnp.where(g == 1, tf, tr), 0.0)
    return jnp.concatenate([top, bot], axis=0).astype(BF16)


def _swa_align_out(t, lane, g):
    top, bot = t[:TQ, :], t[TQ:, :]
    top = jnp.where(g == 0, top, pltpu.roll(top, HEAD_DIM, 1))
    bot = jnp.where(g == 1, bot, pltpu.roll(bot, HEAD_DIM, 1))
    return jnp.where(lane < HEAD_DIM, top, bot)


def _swa_bias(bias_ref, bucket_ref, rb_ref, j):
    dist = TQ + lax.broadcasted_iota(jnp.int32, (TQ, 2 * BK), 0) - lax.broadcasted_iota(jnp.int32, (TQ, 2 * BK), 1)
    window = (dist >= 0) & (dist < WINDOW)
    for hh in range(2):
        def add(b, acc):
            return acc + jnp.where(bucket_ref[...] == b, rb_ref[b, 2 * j + hh], 0.0)
        bias = lax.fori_loop(0, N_BUCKETS, add, jnp.zeros((TQ, 2 * BK), F32))
        bias_ref[hh * TQ:(hh + 1) * TQ, :] = jnp.where(window, bias, NEG)


def _swa_probs(qs, k2, bias, own_block, sink_ref, i, j):
    s = _dot_nt(qs, k2) * SCALE + bias
    s = jnp.where(own_block | (i > 0), s, NEG)
    row1 = lax.broadcasted_iota(jnp.int32, (2 * TQ, 1), 0)
    sink = jnp.where(row1 < TQ, sink_ref[2 * j], sink_ref[2 * j + 1])
    m = jnp.maximum(jnp.max(s, axis=1, keepdims=True), sink)
    e = jnp.exp(s - m)
    es = jnp.exp(sink - m)
    inv = 1.0 / (jnp.sum(e, axis=1, keepdims=True) + es)
    return e * inv, es * inv


def _swa_kv(ref, i):
    prev = pl.multiple_of(jnp.maximum(i - 1, 0) * BK, BK)
    cur = pl.multiple_of(i * BK, BK)
    return jnp.concatenate([ref[pl.ds(prev, BK), :], ref[pl.ds(cur, BK), :]], axis=0), prev, cur


def _swa_fwd(qkv, sinks, rel_bias, name):
    s = qkv.shape[0]
    nq = s // TQ
    npair = SW_QW // LANES
    qcol = 3 * SB_W // LANES
    bucket = jnp.asarray(_bucket_table())

    def body(q_ref, k_ref, v_ref, bucket_ref, sink_ref, rb_ref, o_ref, bias_ref):
        j = pl.program_id(0)
        step = pl.program_id(1)
        g = j // 2
        lane = lax.broadcasted_iota(jnp.int32, (TQ, LANES), 1)

        @pl.when(step == 0)
        def _():
            _swa_bias(bias_ref, bucket_ref, rb_ref, j)

        own_block = lax.broadcasted_iota(jnp.int32, (2 * TQ, 2 * BK), 1) >= BK
        for b in range(SWA_QB):
            i = step * SWA_QB + b
            rows = slice(b * TQ, (b + 1) * TQ)
            qs = _swa_align_in(q_ref[rows, :], lane, g)
            k2, _, _ = _swa_kv(k_ref, i)
            v2, _, _ = _swa_kv(v_ref, i)
            pr, _ = _swa_probs(qs, k2, bias_ref[...], own_block, sink_ref, i, j)
            o_ref[rows, :] = _swa_align_out(_dot(pr.astype(BF16), v2), lane, g).astype(BF16)

    assert nq % SWA_QB == 0
    return pl.pallas_call(
        body, out_shape=jax.ShapeDtypeStruct((s, SW_QW), BF16), grid=(npair, nq // SWA_QB),
        in_specs=[pl.BlockSpec((SWA_QB * TQ, LANES), lambda j, i: (i, qcol + j)),
                  pl.BlockSpec((s, LANES), lambda j, i: (0, qcol + npair)),
                  pl.BlockSpec((s, LANES), lambda j, i: (0, qcol + npair + 1)),
                  pl.BlockSpec((TQ, 2 * BK), lambda j, i: (0, 0)),
                  pl.BlockSpec(memory_space=pltpu.SMEM),
                  pl.BlockSpec(memory_space=pltpu.SMEM)],
        out_specs=pl.BlockSpec((SWA_QB * TQ, LANES), lambda j, i: (i, j)),
        scratch_shapes=[pltpu.VMEM((2 * TQ, 2 * BK), F32)],
        compiler_params=_params(2), name=name)(qkv, qkv, qkv, bucket, sinks, rel_bias)


def _swa_bwd(qkv, ob, dob, sinks, rel_bias, name):
    s = qkv.shape[0]
    nq = s // TQ
    npair = SW_QW // LANES
    qcol = 3 * SB_W // LANES
    bucket = jnp.asarray(_bucket_table())

    def body(q_ref, k_ref, v_ref, o_ref, do_ref, bucket_ref, sink_ref, rb_ref,
             dq_ref, dk_ref, dv_ref, dsink_ref, drel_ref, bias_ref, dsacc_ref):
        j = pl.program_id(0)
        step = pl.program_id(1)
        g = j // 2
        lane = lax.broadcasted_iota(jnp.int32, (TQ, LANES), 1)
        row8 = lax.broadcasted_iota(jnp.int32, (SW_HEADS, LANES), 0)
        lane8 = lax.broadcasted_iota(jnp.int32, (SW_HEADS, LANES), 1)

        @pl.when((step == 0) & (j == 0))
        def _():
            dk_ref[...] = jnp.zeros_like(dk_ref)
            dv_ref[...] = jnp.zeros_like(dv_ref)
            dsink_ref[...] = jnp.zeros_like(dsink_ref)
            drel_ref[...] = jnp.zeros_like(drel_ref)

        @pl.when(step == 0)
        def _():
            _swa_bias(bias_ref, bucket_ref, rb_ref, j)
            dsacc_ref[...] = jnp.zeros_like(dsacc_ref)

        own_block = lax.broadcasted_iota(jnp.int32, (2 * TQ, 2 * BK), 1) >= BK
        ds_sum = jnp.zeros(dsacc_ref.shape, F32)
        dsink = jnp.zeros((SW_HEADS, LANES), F32)
        for b in range(SWA_QB):
            i = step * SWA_QB + b
            rows = slice(b * TQ, (b + 1) * TQ)
            qs = _swa_align_in(q_ref[rows, :], lane, g)
            do = do_ref[rows, :]
            dos = _swa_align_in(do, lane, g)
            dof = do.astype(F32) * o_ref[rows, :].astype(F32)
            d0 = jnp.sum(jnp.where(lane < HEAD_DIM, dof, 0.0), axis=1, keepdims=True)
            d1 = jnp.sum(jnp.where(lane >= HEAD_DIM, dof, 0.0), axis=1, keepdims=True)
            delta = jnp.concatenate([d0, d1], axis=0)
            k2, prev, cur = _swa_kv(k_ref, i)
            v2, _, _ = _swa_kv(v_ref, i)
            pr, psink = _swa_probs(qs, k2, bias_ref[...], own_block, sink_ref, i, j)
            ds = pr * (_dot_nt(dos, v2) - delta)
            ds_sum = ds_sum + ds
            sd = psink * delta
            ds0 = -jnp.sum(sd[:TQ, :], axis=0, keepdims=True)
            ds1 = -jnp.sum(sd[TQ:, :], axis=0, keepdims=True)
            dsink = dsink + jnp.where(row8 == 2 * j, ds0, jnp.where(row8 == 2 * j + 1, ds1, 0.0))
            dsb = ds.astype(BF16)
            dq_ref[rows, :] = _swa_align_out(_dot(dsb, k2) * SCALE, lane, g).astype(BF16)
            dk2 = _dot_tn(dsb, qs) * SCALE
            dv2 = _dot_tn(pr.astype(BF16), dos)
            dk_ref[pl.ds(prev, BK), :] += dk2[:BK, :]
            dk_ref[pl.ds(cur, BK), :] += dk2[BK:, :]
            dv_ref[pl.ds(prev, BK), :] += dv2[:BK, :]
            dv_ref[pl.ds(cur, BK), :] += dv2[BK:, :]
        dsacc_ref[...] += ds_sum
        dsink_ref[...] += dsink

        @pl.when(step == nq // SWA_QB - 1)
        def _():
            for hh in range(2):
                def red(b, acc):
                    val = _sum_all(jnp.where(bucket_ref[...] == b, dsacc_ref[hh * TQ:(hh + 1) * TQ, :], 0.0))
                    return jnp.where((row8 == 2 * j + hh) & (lane8 == b), val, acc)
                drel_ref[...] += lax.fori_loop(0, N_BUCKETS, red, jnp.zeros((SW_HEADS, LANES), F32))

    whole = lambda j, i: (0, 0)
    return pl.pallas_call(
        body,
        out_shape=[jax.ShapeDtypeStruct((s, SW_QW), BF16), jax.ShapeDtypeStruct((s, LANES), F32),
                   jax.ShapeDtypeStruct((s, LANES), F32), jax.ShapeDtypeStruct((SW_HEADS, LANES), F32),
                   jax.ShapeDtypeStruct((SW_HEADS, LANES), F32)],
        grid=(npair, nq // SWA_QB),
        in_specs=[pl.BlockSpec((SWA_QB * TQ, LANES), lambda j, i: (i, qcol + j)),
                  pl.BlockSpec((s, LANES), lambda j, i: (0, qcol + npair)),
                  pl.BlockSpec((s, LANES), lambda j, i: (0, qcol + npair + 1)),
                  pl.BlockSpec((SWA_QB * TQ, LANES), lambda j, i: (i, j)),
                  pl.BlockSpec((SWA_QB * TQ, LANES), lambda j, i: (i, j)),
                  pl.BlockSpec((TQ, 2 * BK), whole),
                  pl.BlockSpec(memory_space=pltpu.SMEM),
                  pl.BlockSpec(memory_space=pltpu.SMEM)],
        out_specs=[pl.BlockSpec((SWA_QB * TQ, LANES), lambda j, i: (i, j)),
                   pl.BlockSpec((s, LANES), whole), pl.BlockSpec((s, LANES), whole),
                   pl.BlockSpec((SW_HEADS, LANES), whole), pl.BlockSpec((SW_HEADS, LANES), whole)],
        scratch_shapes=[pltpu.VMEM((2 * TQ, 2 * BK), F32), pltpu.VMEM((2 * TQ, 2 * BK), F32)],
        compiler_params=_params(2), name=name)(qkv, qkv, qkv, ob, dob, bucket, sinks, rel_bias)


def _acc_init(i, *refs):
    @pl.when(i == 0)
    def _():
        for r in refs:
            r[...] = jnp.zeros_like(r)


def _loss_bwd(x3, target, g, name):
    d = x3.shape[1]

    def body(x_ref, t_ref, g_ref, dx_ref, dg_ref, loss_ref):
        _acc_init(pl.program_id(0), dg_ref, loss_ref)
        x = x_ref[...]
        gv = g_ref[...]
        diff = _rms(x, gv) - t_ref[...]
        loss_ref[...] += 0.5 * jnp.sum(jnp.mean(jnp.square(diff), axis=-1, keepdims=True), axis=0, keepdims=True)
        dx, dg = _rms_bwd(diff * (1.0 / d), x, gv)
        dx_ref[...] = dx
        dg_ref[...] += dg

    return _rowcall(name, body, [x3, target], [g], [(d, F32)], [((1, d), F32), ((1, LANES), F32)])


def _ple_bwd(dx3, pe, gt, x2, g, wpg, name):
    d = x2.shape[1]

    def body(dx3_ref, pe_ref, gt_ref, x_ref, g_ref, w_ref, dpe_ref, dgt_ref, dx2_ref, dg_ref):
        _acc_init(pl.program_id(0), dg_ref)
        dx3 = dx3_ref[...]
        sg = _sigmoid(gt_ref[...])
        dpe_ref[...] = (dx3 * sg).astype(BF16)
        dgt = (dx3 * pe_ref[...] * sg * (1.0 - sg)).astype(BF16)
        dgt_ref[...] = dgt
        dx, dg = _rms_bwd(_dot_nt(dgt, w_ref[...]), x_ref[...], g_ref[...])
        dx2_ref[...] = dx3 + dx
        dg_ref[...] += dg

    return _rowcall(name, body, [dx3, pe, gt, x2], [g, wpg], [(d, BF16), (d, BF16), (d, F32)], [((1, d), F32)])


def _ff2_bwd(dx2, u, w2, name):
    d = dx2.shape[1]
    dff = u.shape[1]

    def body(dx_ref, u_ref, w_ref, du_ref, dxb_ref):
        dxb = dx_ref[...].astype(BF16)
        dxb_ref[...] = dxb
        du_ref[...] = (_dot_nt(dxb, w_ref[...]) * (2.0 * jnp.maximum(u_ref[...], 0.0))).astype(BF16)

    return _rowcall(name, body, [dx2, u], [w2], [(dff, BF16), (d, BF16)])


def _ff1_bwd(du, dx2, x1, g, w1, name):
    d = x1.shape[1]

    def body(du_ref, dx2_ref, x_ref, g_ref, w_ref, dx1_ref, dx1b_ref, dg_ref):
        _acc_init(pl.program_id(0), dg_ref)
        dx, dg = _rms_bwd(_dot_cols_t(du_ref[...], w_ref), x_ref[...], g_ref[...])
        dx1 = dx2_ref[...] + dx
        dx1_ref[...] = dx1
        dx1b_ref[...] = dx1.astype(BF16)
        dg_ref[...] += dg

    return _rowcall(name, body, [du, dx2, x1], [g, w1], [(d, F32), (d, BF16)], [((1, d), F32)])


def _mixer_bwd(dx1b, gates, oa, ob, wo, wua, wub, name, side=None):
    d = dx1b.shape[1]

    def body(dx_ref, gate_ref, oa_ref, ob_ref, wo_ref, wua_ref, wub_ref,
             dya_ref, dyb_ref, dgate_ref, doa_ref, dob_ref):
        dm = _dot_nt(dx_ref[...], wo_ref[...])
        sa = _sigmoid(gate_ref[:, :d])
        sb = _sigmoid(gate_ref[:, d:])
        ya = _dot_cols(oa_ref[...], wua_ref)
        yb = _dot_cols(ob_ref[...], wub_ref)
        dya = (dm * sa).astype(BF16)
        dyb = (dm * sb).astype(BF16)
        dya_ref[...] = dya
        dyb_ref[...] = dyb
        dgate_ref[:, :d] = (dm * ya * sa * (1.0 - sa)).astype(BF16)
        dgate_ref[:, d:] = (dm * yb * sb * (1.0 - sb)).astype(BF16)
        doa_ref[...] = _dot_cols_t(dya, wua_ref).astype(BF16)
        dob_ref[...] = _dot_cols_t(dyb, wub_ref).astype(BF16)

    return _rowcall(name, body, [dx1b, gates, oa, ob], [wo, wua, wub],
                    [(d, BF16), (d, BF16), (2 * d, BF16), (SB_W, BF16), (SW_QW, BF16)], side=side)


def _inproj_bwd(dqkv, dgates, dx1, x, g, w, name, side=None):
    d = x.shape[1]

    def body(dqkv_ref, dgate_ref, dx1_ref, x_ref, g_ref, w_ref, dx_ref, dg_ref):
        _acc_init(pl.program_id(0), dg_ref)
        dh = _dot(dqkv_ref[...], w_ref[:QKV_W, :]) + _dot(dgate_ref[...], w_ref[QKV_W:, :])
        dx, dg = _rms_bwd(dh, x_ref[...], g_ref[...])
        dx_ref[...] = dx1_ref[...] + dx
        dg_ref[...] += dg

    return _rowcall(name, body, [dqkv, dgates, dx1, x], [g, w], [(d, F32)], [((1, d), F32)], side=side)


def _tile(n, cap):
    assert n % LANES == 0
    return max(t for t in range(LANES, min(n, cap) + 1, LANES) if n % t == 0)


def _mm_tn(a, b, name, nshard=1):
    s, ka = a.shape
    nb = b.shape[1]
    n = nb // nshard
    ta = _tile(ka, 512)
    tb = _tile(n, 1024)
    per = n // tb

    def body(a_ref, b_ref, o_ref):
        o_ref[...] = _dot_tn(a_ref[...].astype(BF16), b_ref[...].astype(BF16))

    return pl.pallas_call(
        body, out_shape=jax.ShapeDtypeStruct((nshard, ka, n), F32), grid=(nb // tb, ka // ta),
        in_specs=[pl.BlockSpec((s, ta), lambda jb, ia: (0, ia)), pl.BlockSpec((s, tb), lambda jb, ia: (0, jb))],
        out_specs=pl.BlockSpec((None, ta, tb), lambda jb, ia: (jb // per, ia, jb % per)),
        compiler_params=_params(2), name=name)(a, b)


def _mm_tn_pieces(pieces, b, name):
    s, nb = b.shape
    ta = 256
    tiles = [pc.shape[1] // ta for pc in pieces]
    assert all(pc.shape[1] % ta == 0 for pc in pieces)
    starts = [sum(tiles[:p]) for p in range(len(pieces))]
    tb = _tile(nb, 512)

    def body(*refs):
        a_refs, b_ref, o_ref = refs[:-2], refs[-2], refs[-1]
        ia = pl.program_id(1)
        a = a_refs[0][...]
        for p in range(1, len(pieces)):
            a = jnp.where(ia >= starts[p], a_refs[p][...], a)
        o_ref[...] = _dot_tn(a, b_ref[...])

    def piece_spec(p):
        return pl.BlockSpec((s, ta), lambda jb, ia: (0, jnp.clip(ia - starts[p], 0, tiles[p] - 1)))

    return pl.pallas_call(
        body, out_shape=jax.ShapeDtypeStruct((sum(tiles) * ta, nb), F32), grid=(nb // tb, sum(tiles)),
        in_specs=[piece_spec(p) for p in range(len(pieces))] + [pl.BlockSpec((s, tb), lambda jb, ia: (0, jb))],
        out_specs=pl.BlockSpec((ta, tb), lambda jb, ia: (ia, jb)),
        compiler_params=_params(2), name=name)(*pieces, b)


def _place():
    return lax.axis_index("x"), lax.axis_index("y"), lax.axis_index("c")


def _chip_peer(x, y, k):
    return (x ^ (k >> 1), y ^ (k & 1))


def _row_tile(k, cap=544):
    return max(t for t in range(32, min(k, cap) + 1, 32) if k % t == 0)


def _cast_bf16(w, r, name):
    l, k, n = w.shape
    assert l == 2
    tk = _row_tile(k)

    def body(r_ref, w_ref, o0_ref, o1_ref):
        o0_ref[...] = w_ref[0].astype(BF16)
        o1_ref[...] = w_ref[1].astype(BF16)

    out_spec = pl.BlockSpec((None, tk, n), lambda i, r_ref: (r_ref[0], i, 0))
    return pl.pallas_call(
        body, out_shape=[jax.ShapeDtypeStruct((4, k, n), BF16)] * 2,
        grid_spec=pltpu.PrefetchScalarGridSpec(
            num_scalar_prefetch=1, grid=(k // tk,),
            in_specs=[pl.BlockSpec((l, tk, n), lambda i, r_ref: (0, i, 0))],
            out_specs=[out_spec, out_spec]),
        compiler_params=_params(1), name=name)(r, w)


class _Exchange(NamedTuple):
    arrays: tuple
    aliased: tuple
    sems: tuple
    start: Callable
    finish: Callable


def _all_gather(bufs):
    nt = len(bufs)

    def ici(t, ins, outs, send_sems, recv_sems, x, y, c, r, k):
        return pltpu.make_async_remote_copy(
            src_ref=ins[t].at[r, c], dst_ref=outs[t].at[r, c], send_sem=send_sems.at[t, k - 1],
            recv_sem=recv_sems.at[t, k - 1], device_id=(*_chip_peer(x, y, k), c), device_id_type=MESH)

    def d2d(t, outs, send_sems, recv_sems, x, y, c, r, k, half):
        slab = outs[t].at[r ^ k, half]
        return pltpu.make_async_remote_copy(
            src_ref=slab, dst_ref=slab, send_sem=send_sems.at[t, 2 + k], recv_sem=recv_sems.at[t, 2 + k],
            device_id=(x, y, 1 - c), device_id_type=MESH)

    def start(ins, outs, send_sems, recv_sems):
        x, y, c = _place()
        r = 2 * x + y
        for t in range(nt):
            for k in (1, 2, 3):
                ici(t, ins, outs, send_sems, recv_sems, x, y, c, r, k).start()

    def finish(ins, outs, send_sems, recv_sems):
        x, y, c = _place()
        r = 2 * x + y
        for t in range(nt):
            for k in (1, 2, 3):
                slab = outs[t].at[r ^ k, c]
                pltpu.make_async_remote_copy(
                    src_ref=slab, dst_ref=slab, send_sem=send_sems.at[t, k - 1], recv_sem=recv_sems.at[t, k - 1],
                    device_id=(x, y, 1 - c), device_id_type=MESH).wait_recv()
                d2d(t, outs, send_sems, recv_sems, x, y, c, r, k, c).start()
        for t in range(nt):
            for k in (1, 2, 3):
                d2d(t, outs, send_sems, recv_sems, x, y, c, r, k, 1 - c).wait_recv()
        for t in range(nt):
            for k in (1, 2, 3):
                ici(t, ins, outs, send_sems, recv_sems, x, y, c, r, k).wait_send()
                d2d(t, outs, send_sems, recv_sems, x, y, c, r, k, c).wait_send()

    return _Exchange(tuple(bufs), tuple(range(nt)), (nt, 6), start, finish)


def _run_exchange(name, ex):
    n_in, n_out = len(ex.arrays), len(ex.aliased)

    def body(*refs):
        ins, outs = refs[:n_in], refs[n_in:n_in + n_out]
        send_sems, recv_sems = refs[n_in + n_out:]
        ex.start(ins, outs, send_sems, recv_sems)
        ex.finish(ins, outs, send_sems, recv_sems)

    any_spec = pl.BlockSpec(memory_space=pl.ANY)
    return pl.pallas_call(
        body, out_shape=[jax.ShapeDtypeStruct(ex.arrays[a].shape, ex.arrays[a].dtype) for a in ex.aliased],
        in_specs=[any_spec] * n_in, out_specs=[any_spec] * n_out,
        input_output_aliases={a: o for o, a in enumerate(ex.aliased)},
        scratch_shapes=[pltpu.SemaphoreType.DMA(ex.sems), pltpu.SemaphoreType.DMA(ex.sems)],
        name=name)(*ex.arrays)


def _rs_to_sibling(grads):
    nt = len(grads)
    landing = [lax.empty((4,) + g.shape[2:], F32) for g in grads]

    def copies(ins, outs, send_sems, recv_sems):
        x, y, c = _place()
        return [pltpu.make_async_remote_copy(
            src_ref=ins[t].at[:, 1 - c], dst_ref=outs[t], send_sem=send_sems.at[t], recv_sem=recv_sems.at[t],
            device_id=(x, y, 1 - c), device_id_type=MESH) for t in range(nt)]

    def start(ins, outs, send_sems, recv_sems):
        for cp in copies(ins, outs, send_sems, recv_sems):
            cp.start()

    def finish(ins, outs, send_sems, recv_sems):
        for cp in copies(ins, outs, send_sems, recv_sems):
            cp.wait()

    return _Exchange(tuple(grads) + tuple(landing), tuple(range(nt, 2 * nt)), (nt,), start, finish)


def _add_half(g, recv, cr, name):
    _, _, k2, n = g.shape
    tk = _row_tile(k2)

    def body(cr_ref, g_ref, r_ref, sums_ref, mine_ref):
        val = (g_ref[...] + r_ref[...]).astype(BF16)
        sums_ref[...] = val

        @pl.when(pl.program_id(1) == cr_ref[1])
        def _():
            mine_ref[...] = val

    return pl.pallas_call(
        body, out_shape=[jax.ShapeDtypeStruct((4, k2, n), BF16)] * 2,
        grid_spec=pltpu.PrefetchScalarGridSpec(
            num_scalar_prefetch=1, grid=(k2 // tk, 4),
            in_specs=[pl.BlockSpec((None, None, tk, n), lambda i, q, cr_ref: (q, cr_ref[0], i, 0)),
                      pl.BlockSpec((None, tk, n), lambda i, q, cr_ref: (q, i, 0))],
            out_specs=[pl.BlockSpec((None, tk, n), lambda i, q, cr_ref: (q, i, 0)),
                       pl.BlockSpec((None, tk, n), lambda i, q, cr_ref: (cr_ref[1], i, 0))]),
        compiler_params=_params(2), name=name)(cr, g, recv)


def _rs_to_chips(sums, parts):
    nt = len(sums)

    def copies(ins, outs, send_sems, recv_sems):
        x, y, c = _place()
        r = 2 * x + y
        return [pltpu.make_async_remote_copy(
            src_ref=ins[t].at[r ^ k], dst_ref=outs[t].at[r], send_sem=send_sems.at[t, k - 1],
            recv_sem=recv_sems.at[t, k - 1], device_id=(*_chip_peer(x, y, k), c), device_id_type=MESH)
            for t in range(nt) for k in (1, 2, 3)]

    def start(ins, outs, send_sems, recv_sems):
        for cp in copies(ins, outs, send_sems, recv_sems):
            cp.start()

    def finish(ins, outs, send_sems, recv_sems):
        for cp in copies(ins, outs, send_sems, recv_sems):
            cp.wait()

    return _Exchange(tuple(sums) + tuple(parts), tuple(range(nt, 2 * nt)), (nt, 3), start, finish)


def _sum4(parts, cr, name):
    _, k2, n = parts.shape
    tk = _row_tile(k2)

    def body(cr_ref, p_ref, o_ref):
        p = p_ref[...].astype(F32)
        o_ref[...] = ((p[0] + p[1]) + p[2]) + p[3]

    return pl.pallas_call(
        body, out_shape=jax.ShapeDtypeStruct((2, k2, n), F32),
        grid_spec=pltpu.PrefetchScalarGridSpec(
            num_scalar_prefetch=1, grid=(k2 // tk,),
            in_specs=[pl.BlockSpec((4, tk, n), lambda i, cr_ref: (0, i, 0))],
            out_specs=pl.BlockSpec((None, tk, n), lambda i, cr_ref: (cr_ref[0], i, 0))),
        compiler_params=_params(1), name=name)(cr, parts)


def _exchange_halves(both, name):
    nt = len(both)

    def body(*refs):
        ins, outs = refs[:nt], refs[nt:2 * nt]
        send_sems, recv_sems = refs[2 * nt:]
        x, y, c = _place()
        cps = []
        for t in range(nt):
            cp = pltpu.make_async_remote_copy(
                src_ref=ins[t].at[c], dst_ref=outs[t].at[c], send_sem=send_sems.at[t], recv_sem=recv_sems.at[t],
                device_id=(x, y, 1 - c), device_id_type=MESH)
            cp.start()
            cps.append(cp)
        for cp in cps:
            cp.wait()

    any_spec = pl.BlockSpec(memory_space=pl.ANY)
    return pl.pallas_call(
        body, out_shape=[jax.ShapeDtypeStruct(b.shape, F32) for b in both],
        in_specs=[any_spec] * nt, out_specs=[any_spec] * nt,
        input_output_aliases={t: t for t in range(nt)},
        scratch_shapes=[pltpu.SemaphoreType.DMA((nt,)), pltpu.SemaphoreType.DMA((nt,))],
        name=name)(*both)


def _adamw_math(w, g, m, v):
    m = ADAM_B1 * m + (1.0 - ADAM_B1) * g
    v = ADAM_B2 * v + (1.0 - ADAM_B2) * jnp.square(g)
    m_hat = m / (1.0 - ADAM_B1 ** ADAM_STEP)
    v_hat = v / (1.0 - ADAM_B2 ** ADAM_STEP)
    delta = -ADAM_LR * (m_hat / (jnp.sqrt(v_hat) + ADAM_EPS) + ADAM_WD * w)
    return delta, m, v


def _adamw(w, m, v, g0, g1, name):
    _, k, n = w.shape
    tk = _row_tile(k)
    nk = k // tk

    def body(w_ref, m_ref, v_ref, g0_ref, g1_ref, grad_ref, delta_ref, nm_ref, nv_ref):
        g = jnp.where(pl.program_id(0) == 0, g0_ref[...], g1_ref[...])
        delta, nm, nv = _adamw_math(w_ref[...], g, m_ref[...], v_ref[...])
        grad_ref[...] = g
        delta_ref[...] = delta
        nm_ref[...] = nm
        nv_ref[...] = nv

    lay = pl.BlockSpec((None, tk, n), lambda a, i: (a, i, 0))
    g0_spec = pl.BlockSpec((tk, n), lambda a, i: (jnp.where(a == 0, i, nk - 1), 0))
    g1_spec = pl.BlockSpec((tk, n), lambda a, i: (jnp.where(a == 1, i, 0), 0))
    return pl.pallas_call(
        body, out_shape=[jax.ShapeDtypeStruct(w.shape, F32)] * 4, grid=(2, nk),
        in_specs=[lay, lay, lay, g0_spec, g1_spec], out_specs=[lay] * 4,
        compiler_params=_params(2), name=name)(w, m, v, g0, g1)


def _small_allreduce_adamw(gpart, w, m, v):
    shape = gpart.shape

    def body(g_ref, w_ref, m_ref, v_ref, gsum_ref, delta_ref, nm_ref, nv_ref, recv_ref, send_sems, recv_sems):
        x, y, c = _place()
        me = 4 * x + 2 * y + c
        recv_ref[me] = g_ref[...]
        cps = []
        for k in range(1, 8):
            peer = (x ^ (k >> 2), y ^ ((k >> 1) & 1), c ^ (k & 1))
            cp = pltpu.make_async_remote_copy(
                src_ref=g_ref, dst_ref=recv_ref.at[me], send_sem=send_sems.at[k - 1], recv_sem=recv_sems.at[k - 1],
                device_id=peer, device_id_type=MESH)
            cp.start()
            cps.append(cp)
        for cp in cps:
            cp.wait()
        g = recv_ref[0]
        for dev in range(1, 8):
            g = g + recv_ref[dev]
        delta, nm, nv = _adamw_math(w_ref[...], g, m_ref[...], v_ref[...])
        gsum_ref[...] = g
        delta_ref[...] = delta
        nm_ref[...] = nm
        nv_ref[...] = nv

    vm = pl.BlockSpec(memory_space=pltpu.VMEM)
    return pl.pallas_call(
        body, out_shape=[jax.ShapeDtypeStruct(shape, F32)] * 4, in_specs=[vm] * 4, out_specs=[vm] * 4,
        scratch_shapes=[pltpu.VMEM((8,) + shape, F32), pltpu.SemaphoreType.DMA((7,)), pltpu.SemaphoreType.DMA((7,))],
        name="small_allreduce_adamw")(gpart, w, m, v)


BIG = ("w_in", "w_up_a", "w_up_b", "w_o", "w_ff1", "w_ff2", "w_pe", "w_pg")
COL_SHARDED = ("w_in", "w_up_a", "w_up_b", "w_ff1", "w_pe")
ROW_SHARDED = ("w_o", "w_ff2", "w_pg")
SMALL_ROWS = 16


def _pack_small(g_mix, g_mlp, g_pe, g_final, sinks, rel_bias, loss=None):
    d = g_final.shape[0]
    row = lambda v: jnp.pad(v.reshape(1, -1), ((0, 0), (0, d - v.size)))
    rows = [g_mix, g_mlp, g_pe, g_final.reshape(1, d),
            jnp.zeros((1, d), F32) if loss is None else row(loss), row(sinks), row(rel_bias)]
    out = jnp.concatenate(rows, axis=0)
    return jnp.pad(out, ((0, SMALL_ROWS - out.shape[0]), (0, 0)))


def _unpack_small(a, sinks_shape, rel_shape):
    return (a[0:2], a[2:4], a[4:6], a[6], a[8, :sinks_shape[0] * sinks_shape[1]].reshape(sinks_shape),
            a[9, :rel_shape[0] * rel_shape[1]].reshape(rel_shape))


def kernel(x, p, w_in, w_up_a, w_up_b, w_o, w_ff1, w_ff2, w_pe, w_pg, g_mix, g_mlp, g_pe, g_final, sinks, rel_bias, loss_target, m_w_in, m_w_up_a, m_w_up_b, m_w_o, m_w_ff1, m_w_ff2, m_w_pe, m_w_pg, m_g_mix, m_g_mlp, m_g_pe, m_g_final, m_sinks, m_rel_bias, v_w_in, v_w_up_a, v_w_up_b, v_w_o, v_w_ff1, v_w_ff2, v_w_pe, v_w_pg, v_g_mix, v_g_mlp, v_g_pe, v_g_final, v_sinks, v_rel_bias):
    depth = w_in.shape[0]
    assert depth == 2
    x0 = x[0]
    target = loss_target[0]
    d = x0.shape[1]
    wl = dict(w_in=w_in, w_up_a=w_up_a, w_up_b=w_up_b, w_o=w_o, w_ff1=w_ff1, w_ff2=w_ff2, w_pe=w_pe, w_pg=w_pg)
    ml = dict(w_in=m_w_in, w_up_a=m_w_up_a, w_up_b=m_w_up_b, w_o=m_w_o, w_ff1=m_w_ff1, w_ff2=m_w_ff2, w_pe=m_w_pe, w_pg=m_w_pg)
    vl = dict(w_in=v_w_in, w_up_a=v_w_up_a, w_up_b=v_w_up_b, w_o=v_w_o, w_ff1=v_w_ff1, w_ff2=v_w_ff2, w_pe=v_w_pe, w_pg=v_w_pg)
    c_idx = lax.axis_index("c").astype(jnp.int32)
    r_idx = (2 * lax.axis_index("x") + lax.axis_index("y")).astype(jnp.int32)
    cr = jnp.stack([c_idx, r_idx])

    wl["w_in"], ml["w_in"], vl["w_in"] = (jnp.swapaxes(a, 1, 2) for a in (w_in, m_w_in, v_w_in))

    bufs = {}
    for n in BIG:
        k, nn = wl[n].shape[1:]
        for l, b in enumerate(_cast_bf16(wl[n], r_idx.reshape(1), "cast_" + n)):
            bufs[n, l] = b.reshape(4, 2, k // 2, nn)

    def gather(keys, run):
        for key, b in zip(keys, run(_all_gather([bufs[key] for key in keys]))):
            bufs[key] = b

    def gathered(n, l):
        _, _, k2, nn = bufs[n, l].shape
        if n in ROW_SHARDED or n == "w_in":
            return bufs[n, l].reshape(8 * k2, nn)
        return bufs[n, l].reshape(4, 2 * k2, nn)

    gather([("w_in", 0)], lambda ex: _run_exchange("all_gather_first", ex))

    full = {n: [None] * depth for n in BIG}
    saved = []
    xi = x0
    for i in range(depth):
        st = dict(x0=xi)
        gm = g_mix[i].reshape(1, d)
        full["w_in"][i] = gathered("w_in", i)
        st["h1"], st["qkv"], st["gates"] = _inproj_fwd(xi, gm, full["w_in"][i], f"inproj_fwd_{i}")

        def attend(ex):
            st["oa"], st["lt"], st["nb"], filled = _sb_fwd(st["qkv"], f"sb_fwd_{i}", ex)
            return filled

        gather([(n, i) for n in BIG if n != "w_in"], attend)
        for n in BIG:
            if n != "w_in":
                full[n][i] = gathered(n, i)
        st["ob"] = _swa_fwd(st["qkv"], sinks[i], rel_bias, f"swa_fwd_{i}")
        st["m"], st["x1"] = _mixer_fwd(st["oa"], st["ob"], st["gates"], xi, full["w_up_a"][i], full["w_up_b"][i],
                                       full["w_o"][i], f"mixer_fwd_{i}")
        if i == 0:
            def feed_forward(ex):
                (st["h2"], st["u"], st["a"]), filled = _ff1_fwd(st["x1"], g_mlp[i].reshape(1, d), full["w_ff1"][i],
                                                               f"ff1_fwd_{i}", ex)
                return filled

            gather([("w_in", 1)], feed_forward)
        else:
            st["h2"], st["u"], st["a"] = _ff1_fwd(st["x1"], g_mlp[i].reshape(1, d), full["w_ff1"][i], f"ff1_fwd_{i}")
        st["x2"] = _ff2_fwd(st["a"], st["x1"], full["w_ff2"][i], f"ff2_fwd_{i}")
        st["pb"], st["h3"], st["pe"], st["gt"], xi = _ple_fwd(p[i, 0], st["x2"], g_pe[i].reshape(1, d),
                                                            full["w_pe"][i], full["w_pg"][i], f"ple_fwd_{i}")
        saved.append(st)

    dx, dg_final, loss_part = _loss_bwd(xi, target, g_final.reshape(1, d), "loss_bwd")
    gw = {n: [None] * depth for n in BIG}
    reduced = {}

    chip_sums = {}

    def to_sibling(keys, run):
        tensors = []
        for n, l in keys:
            g = gw[n][l]
            if n in ROW_SHARDED:
                ka, nb = g.shape[1:]
                g = g.reshape(4, ka // 4, nb)
            _, k, nn = g.shape
            tensors.append(g.reshape(4, 2, k // 2, nn))
        for (n, l), g, r in zip(keys, tensors, run(_rs_to_sibling(tensors))):
            chip_sums[n, l] = _add_half(g, r, cr, f"add_half_{n}_{l}")

    def to_chips(keys):
        return _rs_to_chips([chip_sums[k][0] for k in keys], [chip_sums[k][1] for k in keys])

    def scatter_back(keys, parts, tag):
        halves = [_sum4(pc, cr, f"sum4_{n}_{l}") for (n, l), pc in zip(keys, parts)]
        for key, both in zip(keys, _exchange_halves(halves, f"exchange_halves_{tag}")):
            reduced[key] = both

    dg_mix, dg_mlp, dg_pe, dsinks = [None] * depth, [None] * depth, [None] * depth, [None] * depth
    drel = jnp.zeros((SW_HEADS, LANES), F32)
    for i in reversed(range(depth)):
        st = saved[i]
        dpe, dgt, dx2, dg_pe[i] = _ple_bwd(dx, st["pe"], st["gt"], st["x2"], g_pe[i].reshape(1, d),
                                           full["w_pg"][i], f"ple_bwd_{i}")
        gw["w_pe"][i] = _mm_tn(st["pb"], dpe, f"dw_pe_{i}", 4)
        gw["w_pg"][i] = _mm_tn(st["h3"], dgt, f"dw_pg_{i}")
        du, dx2b = _ff2_bwd(dx2, st["u"], full["w_ff2"][i], f"ff2_bwd_{i}")
        gw["w_ff2"][i] = _mm_tn(st["a"], dx2b, f"dw_ff2_{i}")
        gw["w_ff1"][i] = _mm_tn(st["h2"], du, f"dw_ff1_{i}", 4)
        dx1, dx1b, dg_mlp[i] = _ff1_bwd(du, dx2, st["x1"], g_mlp[i].reshape(1, d), full["w_ff1"][i], f"ff1_bwd_{i}")
        gw["w_o"][i] = _mm_tn(st["m"], dx1b, f"dw_o_{i}")
        early = [(n, i) for n in ("w_pe", "w_pg", "w_ff2", "w_ff1", "w_o")] + ([("w_in", 1)] if i == 0 else [])

        def mixer(ex):
            (dya, dyb, dgates, doa, dob), landed = _mixer_bwd(
                dx1b, st["gates"], st["oa"], st["ob"], full["w_o"][i], full["w_up_a"][i], full["w_up_b"][i],
                f"mixer_bwd_{i}", ex)
            st.update(dya=dya, dyb=dyb, dgates=dgates, doa=doa, dob=dob)
            return landed

        to_sibling(early, mixer)
        dgates = st["dgates"]
        gw["w_up_a"][i] = _mm_tn(st["oa"], st["dya"], f"dw_up_a_{i}", 4)
        gw["w_up_b"][i] = _mm_tn(st["ob"], st["dyb"], f"dw_up_b_{i}", 4)
        late = [("w_up_a", i), ("w_up_b", i)]
        to_sibling(late, lambda ex: _run_exchange(f"rs_to_sibling_{i}", ex))
        keys = early + late
        dqa, dka, dva, parts = _sb_bwd(st["qkv"], st["lt"], st["nb"], st["doa"], f"sb_bwd_{i}", to_chips(keys))
        scatter_back(keys, parts, i)
        dob = st["dob"]
        dqb, dkb, dvb, dsk, drl = _swa_bwd(st["qkv"], st["ob"], dob, sinks[i], rel_bias, f"swa_bwd_{i}")
        dsinks[i] = dsk[:, 0]
        drel = drel + drl
        dqkv = jnp.concatenate([dqa, dka.astype(BF16), dva.astype(BF16), dqb, dkb.astype(BF16), dvb.astype(BF16)],
                               axis=1)
        dw_in_t = _mm_tn_pieces([dqkv, dgates], st["h1"], f"dw_in_{i}")
        gw["w_in"][i] = dw_in_t.reshape(4, dw_in_t.shape[0] // 4, d)
        if i == 1:
            dx, dg_mix[i] = _inproj_bwd(dqkv, dgates, dx1, st["x0"], g_mix[i].reshape(1, d), full["w_in"][i],
                                        f"inproj_bwd_{i}")
        else:
            keys = [("w_in", 0)]
            to_sibling(keys, lambda ex: _run_exchange("rs_to_sibling_last", ex))
            (dx, dg_mix[i]), parts = _inproj_bwd(dqkv, dgates, dx1, st["x0"], g_mix[i].reshape(1, d),
                                                 full["w_in"][i], f"inproj_bwd_{i}", to_chips(keys))
            scatter_back(keys, parts, "last")
    grad_x = dx[None]

    outs = {}
    for n in BIG:
        g0, g1 = (reduced[n, l].reshape(wl[n].shape[1:]) for l in range(depth))
        outs[n] = _adamw(wl[n], ml[n], vl[n], g0, g1, "adamw_" + n)
    outs["w_in"] = [jnp.swapaxes(a, 1, 2) for a in outs["w_in"]]

    drel_bias = drel[:, :N_BUCKETS].T
    gsmall = _pack_small(jnp.concatenate(dg_mix, 0), jnp.concatenate(dg_mlp, 0), jnp.concatenate(dg_pe, 0),
                         dg_final[0], jnp.stack(dsinks), drel_bias, loss_part[0, :1])
    wsmall = _pack_small(g_mix, g_mlp, g_pe, g_final, sinks, rel_bias)
    msmall = _pack_small(m_g_mix, m_g_mlp, m_g_pe, m_g_final, m_sinks, m_rel_bias)
    vsmall = _pack_small(v_g_mix, v_g_mlp, v_g_pe, v_g_final, v_sinks, v_rel_bias)
    small = _small_allreduce_adamw(gsmall, wsmall, msmall, vsmall)
    loss = small[0][7, 0]
    small = [_unpack_small(a, sinks.shape, rel_bias.shape) for a in small]

    result = [loss, grad_x]
    for kind in range(4):
        result += [outs[n][kind] for n in BIG]
        result += list(small[kind])
    return tuple(result)
```

```python
import functools
import math
from typing import Callable, NamedTuple

import numpy as np
import jax
import jax.numpy as jnp
from jax import lax
from jax.experimental import pallas as pl
from jax.experimental.pallas import tpu as pltpu

F32 = jnp.float32
BF16 = jnp.bfloat16
MESH = pl.DeviceIdType.MESH

HEAD_DIM = 64
SB_HEADS = 8
SW_HEADS = 8
SW_KV_HEADS = 2
WINDOW = 128
N_BUCKETS = 32
MAX_DISTANCE = 128
EPS = 1e-6
SB_W = SB_HEADS * HEAD_DIM
SW_QW = SW_HEADS * HEAD_DIM
SW_KVW = SW_KV_HEADS * HEAD_DIM
QKV_W = 3 * SB_W + SW_QW + 2 * SW_KVW
SCALE = HEAD_DIM ** -0.5
assert SCALE == 0.125
LANES = 128
TQ = 128
BK = 128
NEG = -1e30
SB_EXHAUSTED = -106.0

ADAM_LR = 0.001
ADAM_B1 = 0.9
ADAM_B2 = 0.999
ADAM_EPS = 1e-08
ADAM_WD = 0.01
ADAM_STEP = 10

VMEM_LIMIT = 56 * 1024 * 1024


def _dot(a, b):
    return jnp.dot(a, b, preferred_element_type=F32)


def _dot_nt(a, b):
    return lax.dot_general(a, b, (((1,), (1,)), ((), ())), preferred_element_type=F32)


def _dot_tn(a, b):
    return lax.dot_general(a, b, (((0,), (0,)), ((), ())), preferred_element_type=F32)


def _sum_all(x):
    return jnp.sum(jnp.sum(x, axis=1, keepdims=True), axis=0, keepdims=True)


def _sigmoid(x):
    return 1.0 / (1.0 + jnp.exp(-x))


def _rms(x, g):
    r = lax.rsqrt(jnp.mean(x * x, axis=-1, keepdims=True) + EPS)
    return (x * r) * g


def _rms_bwd(dy, x, g):
    r = lax.rsqrt(jnp.mean(x * x, axis=-1, keepdims=True) + EPS)
    n = x * r
    dg = jnp.sum(dy * n, axis=0, keepdims=True)
    dn = dy * g
    dx = r * (dn - n * jnp.mean(dn * n, axis=-1, keepdims=True))
    return dx, dg


def _params(n_axes):
    return pltpu.CompilerParams(dimension_semantics=("arbitrary",) * n_axes, vmem_limit_bytes=VMEM_LIMIT)


def _rowcall(name, body, row_ins, const_ins, row_outs, acc_outs=(), tm=512, side=None):
    s = row_ins[0].shape[0]
    assert s % tm == 0
    in_specs = [pl.BlockSpec((tm, a.shape[1]), lambda i: (i, 0)) for a in row_ins]
    in_specs += [pl.BlockSpec(a.shape, functools.partial(lambda i, nd: (0,) * nd, nd=a.ndim)) for a in const_ins]
    out_shape = [jax.ShapeDtypeStruct((s, c), dt) for c, dt in row_outs]
    out_specs = [pl.BlockSpec((tm, c), lambda i: (i, 0)) for c, _ in row_outs]
    out_shape += [jax.ShapeDtypeStruct(sh, dt) for sh, dt in acc_outs]
    out_specs += [pl.BlockSpec(sh, functools.partial(lambda i, nd: (0,) * nd, nd=len(sh))) for sh, _ in acc_outs]
    if side is None:
        return pl.pallas_call(body, out_shape=out_shape, grid=(s // tm,), in_specs=in_specs, out_specs=out_specs,
                              compiler_params=_params(1), name=name)(*row_ins, *const_ins)
    n_in, n_out = len(in_specs), len(out_specs)

    def with_side(*refs):
        side_in, outs, side_out, sems = _side_refs(side, refs[n_in:], n_out)

        @pl.when(pl.program_id(0) == 0)
        def _():
            side.start(side_in, side_out, *sems)

        body(*refs[:n_in], *outs)

        @pl.when(pl.program_id(0) == s // tm - 1)
        def _():
            side.finish(side_in, side_out, *sems)

    s_in, s_shape, s_out, s_alias, s_sems = _side_specs(side, n_in, n_out)
    outs = pl.pallas_call(with_side, out_shape=out_shape + s_shape, grid=(s // tm,), in_specs=in_specs + s_in,
                          out_specs=out_specs + s_out, input_output_aliases=s_alias, scratch_shapes=s_sems,
                          compiler_params=_params(1), name=name)(*row_ins, *const_ins, *side.arrays)
    return outs[:n_out], outs[n_out:]


def _dot_cols(a, w_ref):
    return jnp.concatenate([_dot(a, w_ref[r]) for r in range(w_ref.shape[0])], axis=1)


def _dot_cols_t(a, w_ref):
    n = w_ref.shape[2]
    out = _dot_nt(a[:, :n], w_ref[0])
    for r in range(1, w_ref.shape[0]):
        out = out + _dot_nt(a[:, r * n:(r + 1) * n], w_ref[r])
    return out


def _inproj_fwd(x, g, wt, name):
    d = x.shape[1]

    def body(x_ref, g_ref, w_ref, h_ref, qkv_ref, gate_ref):
        hb = _rms(x_ref[...], g_ref[...]).astype(BF16)
        h_ref[...] = hb
        qkv_ref[...] = _dot_nt(hb, w_ref[:QKV_W, :]).astype(BF16)
        gate_ref[...] = _dot_nt(hb, w_ref[QKV_W:, :])

    return _rowcall(name, body, [x], [g, wt], [(d, BF16), (QKV_W, BF16), (2 * d, F32)])


def _mixer_fwd(oa, ob, gates, x, wua, wub, wo, name):
    d = x.shape[1]

    def body(oa_ref, ob_ref, gate_ref, x_ref, wua_ref, wub_ref, wo_ref, m_ref, x1_ref):
        ya = _dot_cols(oa_ref[...], wua_ref)
        yb = _dot_cols(ob_ref[...], wub_ref)
        m = _sigmoid(gate_ref[:, :d]) * ya + _sigmoid(gate_ref[:, d:]) * yb
        mb = m.astype(BF16)
        m_ref[...] = mb
        x1_ref[...] = x_ref[...] + _dot(mb, wo_ref[...])

    return _rowcall(name, body, [oa, ob, gates, x], [wua, wub, wo], [(d, BF16), (d, F32)])


def _ff1_fwd(x1, g, w1, name, side=None):
    _, d, nq = w1.shape
    dff = 4 * nq

    def body(x_ref, g_ref, w_ref, h_ref, u_ref, a_ref):
        hb = _rms(x_ref[...], g_ref[...]).astype(BF16)
        h_ref[...] = hb
        u = _dot_cols(hb, w_ref)
        u_ref[...] = u
        a_ref[...] = jnp.square(jnp.maximum(u, 0.0)).astype(BF16)

    return _rowcall(name, body, [x1], [g, w1], [(d, BF16), (dff, F32), (dff, BF16)], side=side)


def _ff2_fwd(a, x1, w2, name):
    d = x1.shape[1]

    def body(a_ref, x_ref, w_ref, o_ref):
        o_ref[...] = x_ref[...] + _dot(a_ref[...], w_ref[...])

    return _rowcall(name, body, [a, x1], [w2], [(d, F32)])[0]


def _ple_fwd(p, x2, g, wpe, wpg, name):
    d = x2.shape[1]

    def body(p_ref, x_ref, g_ref, wpe_ref, wpg_ref, pb_ref, h_ref, pe_ref, gt_ref, x3_ref):
        pb = p_ref[...].astype(BF16)
        pb_ref[...] = pb
        pe = _dot_cols(pb, wpe_ref)
        x = x_ref[...]
        hb = _rms(x, g_ref[...]).astype(BF16)
        h_ref[...] = hb
        gt = _dot(hb, wpg_ref[...])
        pe_ref[...] = pe
        gt_ref[...] = gt
        x3_ref[...] = x + pe * _sigmoid(gt)

    return _rowcall(name, body, [p, x2], [g, wpe, wpg],
                    [(p.shape[1], BF16), (d, BF16), (d, F32), (d, F32), (d, F32)])


def _pair_stack(t, lane):
    zero = jnp.zeros_like(t)
    return jnp.concatenate([jnp.where(lane < HEAD_DIM, t, zero), jnp.where(lane >= HEAD_DIM, t, zero)], axis=0)


def _sb_rel():
    row = lax.broadcasted_iota(jnp.int32, (2 * TQ, BK), 0)
    row = jnp.where(row >= TQ, row - TQ, row)
    col = lax.broadcasted_iota(jnp.int32, (2 * TQ, BK), 1)
    return col - row


def _split_dot(x, m01, two_pass=True):
    hi = x.astype(BF16)
    if not two_pass:
        return _dot(hi, m01)
    lo = (x - hi.astype(F32)).astype(BF16)
    return _dot(hi, m01) + _dot(lo, m01)


def _sb_scores(qs, k, mask):
    z = _dot_nt(qs, k)
    lb = jnp.minimum(z, 0.0) - jnp.log(1.0 + jnp.exp(-jnp.abs(z)))
    lm = lb - z
    return lb, lm if mask is None else jnp.where(mask, lm, 0.0)


SB_STRAIGHT = 3
SB_WIDE = SB_STRAIGHT * BK
SB_QB = 2
SWA_QB = 4


def _sb_wide_consts():
    j = np.arange(BK)[:, None]
    s = np.arange(BK)[None, :]
    ones = np.ones((BK, BK), np.float32)
    as_bf16 = lambda m: jnp.asarray(np.concatenate([m, ones], axis=1).astype(np.float32), dtype=BF16)
    return as_bf16(j > s), as_bf16(j <= s), as_bf16(j < s)


def _wide_sums(x, m01, suffix, two_pass=True):
    parts = [_split_dot(x[:, b * BK:(b + 1) * BK], m01, two_pass) for b in range(SB_STRAIGHT)]
    order = range(SB_STRAIGHT - 1, -1, -1) if suffix else range(SB_STRAIGHT)
    out = [None] * SB_STRAIGHT
    carry = None
    for b in order:
        out[b] = parts[b][:, :BK] if carry is None else parts[b][:, :BK] + carry
        carry = parts[b][:, BK:] if carry is None else carry + parts[b][:, BK:]
    return jnp.concatenate(out, axis=1), carry


def _side_refs(ex, rest, n_out):
    n_in = len(ex.arrays) if ex else 0
    n_alias = len(ex.aliased) if ex else 0
    ins, rest = rest[:n_in], rest[n_in:]
    outs, rest = rest[:n_out], rest[n_out:]
    return ins, outs, rest[:n_alias], rest[n_alias:]


def _side_specs(ex, n_in, n_out):
    if ex is None:
        return [], [], [], {}, []
    any_spec = pl.BlockSpec(memory_space=pl.ANY)
    return ([any_spec] * len(ex.arrays),
            [jax.ShapeDtypeStruct(ex.arrays[a].shape, ex.arrays[a].dtype) for a in ex.aliased],
            [any_spec] * len(ex.aliased), {n_in + a: n_out + o for o, a in enumerate(ex.aliased)},
            [pltpu.SemaphoreType.DMA(ex.sems), pltpu.SemaphoreType.DMA(ex.sems)])


def _sb_fwd(qkv, name, side=None):
    s = qkv.shape[0]
    nq = s // TQ
    npair = SB_W // LANES
    sufw = _sb_wide_consts()[0]

    def body(q_ref, k_ref, v_ref, sufw_ref, *rest):
        side_in, (o_ref, lt_ref, nb_ref), side_out, scratch = _side_refs(side, rest, 3)
        cf_ref, acc_ref = scratch[:2]
        step_id = pl.program_id(1)
        if side is not None:
            @pl.when((pl.program_id(0) == 0) & (step_id == 0))
            def _():
                side.start(side_in, side_out, *scratch[2:])
        lane = lax.broadcasted_iota(jnp.int32, (TQ, LANES), 1)
        rel = _sb_rel()
        blocks = [step_id * SB_QB + b for b in range(SB_QB)]
        qs = [_pair_stack(q_ref[b * TQ:(b + 1) * TQ, :] * SCALE, lane) for b in range(SB_QB)]

        straight = blocks[0] >= SB_STRAIGHT - 1

        @pl.when(straight)
        def _():
            for b, i in enumerate(blocks):
                w0 = pl.multiple_of((i - (SB_STRAIGHT - 1)) * BK, BK)
                kw = k_ref[pl.ds(w0, SB_WIDE), :]
                lb, lm = _sb_scores(qs[b], kw, None)
                own = rel < 0
                past = SB_WIDE - BK
                lm = jnp.concatenate([lm[:, :past], jnp.where(own, lm[:, past:], 0.0)], axis=1)
                after, total = _wide_sums(lm, sufw_ref[...], True)
                a = jnp.exp(lb + after)
                a = jnp.concatenate([a[:, :past], jnp.where(own, a[:, past:], 0.0)], axis=1)
                acc_ref[b] = _dot(a.astype(BF16), v_ref[pl.ds(w0, SB_WIDE), :])
                cf_ref[b] = total

        @pl.when(jnp.logical_not(straight))
        def _():
            cf_ref[...] = jnp.zeros_like(cf_ref)
            acc_ref[...] = jnp.zeros_like(acc_ref)

        for b, i in enumerate(blocks):
            q0 = i * TQ

            def more(c, i=i):
                return (c[0] <= i) & (c[1] > SB_EXHAUSTED)

            def step(c, b=b, i=i, q0=q0):
                k0 = pl.multiple_of((i - c[0]) * BK, BK)
                k = k_ref[pl.ds(k0, BK), :]
                v = v_ref[pl.ds(k0, BK), :]
                mask = rel < (q0 - k0)
                lb, lm = _sb_scores(qs[b], k, mask)
                cs = _split_dot(lm, sufw_ref[...])
                a = jnp.where(mask, jnp.exp(lb + cs[:, :BK] + cf_ref[b]), 0.0)
                acc_ref[b] += _dot(a.astype(BF16), v)
                cf = cf_ref[b] + cs[:, BK:]
                cf_ref[b] = cf
                return c[0] + 1, jnp.max(cf)

            n_blocks, _ = lax.while_loop(
                more, step, (jnp.where(straight, SB_STRAIGHT, 0).astype(jnp.int32), jnp.max(cf_ref[b])))
            o_ref[b * TQ:(b + 1) * TQ, :] = jnp.where(lane < HEAD_DIM, acc_ref[b, :TQ, :],
                                                     acc_ref[b, TQ:, :]).astype(BF16)
            lt_ref[b] = cf_ref[b]
            nb_ref[b] = jnp.full(nb_ref.shape[1:], n_blocks, F32)
        if side is not None:
            @pl.when((pl.program_id(0) == npair - 1) & (step_id == nq // SB_QB - 1))
            def _():
                side.finish(side_in, side_out, *scratch[2:])

    s_in, s_shape, s_out, s_alias, s_sems = _side_specs(side, 4, 3)
    outs = pl.pallas_call(
        body,
        out_shape=[jax.ShapeDtypeStruct((s, SB_W), BF16), jax.ShapeDtypeStruct((npair, nq, 2 * TQ, BK), F32),
                   jax.ShapeDtypeStruct((npair, nq, 8, LANES), F32)] + s_shape,
        grid=(npair, nq // SB_QB),
        in_specs=[pl.BlockSpec((SB_QB * TQ, LANES), lambda j, i: (i, j)),
                  pl.BlockSpec((s, LANES), lambda j, i: (0, npair + j)),
                  pl.BlockSpec((s, LANES), lambda j, i: (0, 2 * npair + j)),
                  pl.BlockSpec(sufw.shape, lambda j, i: (0, 0))] + s_in,
        out_specs=[pl.BlockSpec((SB_QB * TQ, LANES), lambda j, i: (i, j)),
                   pl.BlockSpec((None, SB_QB, 2 * TQ, BK), lambda j, i: (j, i, 0, 0)),
                   pl.BlockSpec((None, SB_QB, 8, LANES), lambda j, i: (j, i, 0, 0))] + s_out,
        input_output_aliases=s_alias,
        scratch_shapes=[pltpu.VMEM((SB_QB, 2 * TQ, BK), F32), pltpu.VMEM((SB_QB, 2 * TQ, LANES), F32)] + s_sems,
        compiler_params=_params(2), name=name)(qkv, qkv, qkv, sufw, *(side.arrays if side else ()))
    return outs[0], outs[1], outs[2], outs[3:]


def _sb_bwd(qkv, lt, nb, doa, name, side=None):
    s = qkv.shape[0]
    nq = s // TQ
    npair = SB_W // LANES
    _, prew, prexw = _sb_wide_consts()

    def body(q_ref, k_ref, v_ref, lt_ref, nb_ref, do_ref, prew_ref, prexw_ref, *rest):
        side_in, (dq_ref, dk_out, dv_out), side_out, scratch = _side_refs(side, rest, 3)
        cp_ref, ce_ref, dqa_ref, dk_ref, dv_ref = scratch[:5]
        sems = scratch[5:]
        step_id = pl.program_id(1)
        if side is not None:
            @pl.when((pl.program_id(0) == 0) & (step_id == 0))
            def _():
                side.start(side_in, side_out, *sems)
        lane = lax.broadcasted_iota(jnp.int32, (TQ, LANES), 1)
        rel = _sb_rel()
        blocks = [step_id * SB_QB + b for b in range(SB_QB)]
        rows = [slice(b * TQ, (b + 1) * TQ) for b in range(SB_QB)]
        qs = [_pair_stack(q_ref[rows[b], :] * SCALE, lane) for b in range(SB_QB)]
        dos = [_pair_stack(do_ref[rows[b], :], lane) for b in range(SB_QB)]
        n_blocks = [jnp.clip(jnp.max(nb_ref[b]).astype(jnp.int32), 1, i + 1) for b, i in enumerate(blocks)]
        first = [i + 1 - n for i, n in zip(blocks, n_blocks)]

        @pl.when(step_id == 0)
        def _():
            dk_ref[...] = jnp.zeros_like(dk_ref)
            dv_ref[...] = jnp.zeros_like(dv_ref)

        straight = n_blocks[0] == SB_STRAIGHT
        for n in n_blocks[1:]:
            straight = straight & (n == SB_STRAIGHT)

        @pl.when(straight)
        def _():
            for b in range(SB_QB):
                w0 = pl.multiple_of(first[b] * BK, BK)
                kw = k_ref[pl.ds(w0, SB_WIDE), :]
                vw = v_ref[pl.ds(w0, SB_WIDE), :]
                lb, lm = _sb_scores(qs[b], kw, None)
                own = rel < 0
                past = SB_WIDE - BK
                on_past_keys = lambda t: jnp.concatenate([t[:, :past], jnp.where(own, t[:, past:], 0.0)], axis=1)
                lm = on_past_keys(lm)
                upto, _ = _wide_sums(lm, prew_ref[...], False)
                lt = lt_ref[b]
                a = on_past_keys(jnp.exp(lb + (jnp.concatenate([lt] * SB_STRAIGHT, axis=1) - upto)))
                e = a * _dot_nt(dos[b], vw)
                big_e, _ = _wide_sums(e, prexw_ref[...], False, two_pass=False)
                dz = on_past_keys(e - jnp.exp(lb) * (e + big_e)).astype(BF16)
                dk_ref[pl.ds(w0, SB_WIDE), :] += _dot_tn(dz, qs[b])
                dv_ref[pl.ds(w0, SB_WIDE), :] += _dot_tn(a.astype(BF16), dos[b])
                dqa_ref[b] = _dot(dz, kw)

        @pl.when(jnp.logical_not(straight))
        def _():
            cp_ref[...] = jnp.zeros_like(cp_ref)
            ce_ref[...] = jnp.zeros_like(ce_ref)
            dqa_ref[...] = jnp.zeros_like(dqa_ref)
            for b, i in enumerate(blocks):
                q0 = i * TQ

                def step(it, carry, b=b, q0=q0):
                    k0 = pl.multiple_of((first[b] + it) * BK, BK)
                    k = k_ref[pl.ds(k0, BK), :]
                    v = v_ref[pl.ds(k0, BK), :]
                    mask = rel < (q0 - k0)
                    lb, lm = _sb_scores(qs[b], k, mask)
                    cs = _split_dot(lm, prew_ref[...])
                    a = jnp.where(mask, jnp.exp(lb + (lt_ref[b] - (cs[:, :BK] + cp_ref[b]))), 0.0)
                    e = a * _dot_nt(dos[b], v)
                    ce = _split_dot(e, prexw_ref[...], two_pass=False)
                    big_e = ce[:, :BK] + ce_ref[b]
                    dz = jnp.where(mask, e - jnp.exp(lb) * (e + big_e), 0.0).astype(BF16)
                    dk_ref[pl.ds(k0, BK), :] += _dot_tn(dz, qs[b])
                    dv_ref[pl.ds(k0, BK), :] += _dot_tn(a.astype(BF16), dos[b])
                    dqa_ref[b] += _dot(dz, k)
                    cp_ref[b] += cs[:, BK:]
                    ce_ref[b] += ce[:, BK:]
                    return carry

                lax.fori_loop(0, n_blocks[b], step, 0)

        for b in range(SB_QB):
            dq = jnp.where(lane < HEAD_DIM, dqa_ref[b, :TQ, :], dqa_ref[b, TQ:, :])
            dq_ref[rows[b], :] = (dq * SCALE).astype(BF16)

        @pl.when(step_id == nq // SB_QB - 1)
        def _():
            dk_out[...] = dk_ref[...].astype(BF16)
            dv_out[...] = dv_ref[...].astype(BF16)

        if side is not None:
            @pl.when((pl.program_id(0) == npair - 1) & (step_id == nq // SB_QB - 1))
            def _():
                side.finish(side_in, side_out, *sems)

    s_in, s_shape, s_out, s_alias, s_sems = _side_specs(side, 8, 3)
    outs = pl.pallas_call(
        body,
        out_shape=[jax.ShapeDtypeStruct((s, SB_W), BF16)] * 3 + s_shape,
        grid=(npair, nq // SB_QB),
        in_specs=[pl.BlockSpec((SB_QB * TQ, LANES), lambda j, i: (i, j)),
                  pl.BlockSpec((s, LANES), lambda j, i: (0, npair + j)),
                  pl.BlockSpec((s, LANES), lambda j, i: (0, 2 * npair + j)),
                  pl.BlockSpec((None, SB_QB, 2 * TQ, BK), lambda j, i: (j, i, 0, 0)),
                  pl.BlockSpec((None, SB_QB, 8, LANES), lambda j, i: (j, i, 0, 0)),
                  pl.BlockSpec((SB_QB * TQ, LANES), lambda j, i: (i, j)),
                  pl.BlockSpec(prew.shape, lambda j, i: (0, 0)),
                  pl.BlockSpec(prexw.shape, lambda j, i: (0, 0))] + s_in,
        out_specs=[pl.BlockSpec((SB_QB * TQ, LANES), lambda j, i: (i, j)),
                   pl.BlockSpec((s, LANES), lambda j, i: (0, j)),
                   pl.BlockSpec((s, LANES), lambda j, i: (0, j))] + s_out,
        input_output_aliases=s_alias,
        scratch_shapes=[pltpu.VMEM((SB_QB, 2 * TQ, BK), F32), pltpu.VMEM((SB_QB, 2 * TQ, BK), F32),
                        pltpu.VMEM((SB_QB, 2 * TQ, LANES), F32), pltpu.VMEM((s, LANES), F32),
                        pltpu.VMEM((s, LANES), F32)] + s_sems,
        compiler_params=_params(2), name=name)(qkv, qkv, qkv, lt, nb, doa, prew, prexw,
                                               *(side.arrays if side else ()))
    return outs[0], outs[1], outs[2], outs[3:]


def _bucket_table():
    i = np.arange(TQ)[:, None]
    j = np.arange(2 * BK)[None, :]
    dist = np.maximum(TQ + i - j, 0)
    max_exact = N_BUCKETS // 2
    df = np.maximum(dist, 1).astype(np.float32)
    large = max_exact + (np.log(df / np.float32(max_exact)) / np.float32(math.log(MAX_DISTANCE / max_exact))
                         * np.float32(N_BUCKETS - max_exact)).astype(np.int32)
    large = np.minimum(large, N_BUCKETS - 1)
    return np.where(dist < max_exact, dist, large).astype(np.int32)


def _swa_align_in(t, lane, g):
    tf = t.astype(F32)
    tr = pltpu.roll(tf, HEAD_DIM, 1)
    gmask = (lane >= HEAD_DIM) == (g == 1)
    top = jnp.where(gmask, jnp.where(g == 0, tf, tr), 0.0)
    bot = jnp.where(gmask, jnp.where(g == 1, tf, tr), 0.0)
    return jnp.concatenate([top, bot], axis=0).astype(BF16)


def _swa_align_out(t, lane, g):
    top, bot = t[:TQ, :], t[TQ:, :]
    top = jnp.where(g == 0, top, pltpu.roll(top, HEAD_DIM, 1))
    bot = jnp.where(g == 1, bot, pltpu.roll(bot, HEAD_DIM, 1))
    return jnp.where(lane < HEAD_DIM, top, bot)


def _swa_bias(bias_ref, bucket_ref, rb_ref, j):
    dist = TQ + lax.broadcasted_iota(jnp.int32, (TQ, 2 * BK), 0) - lax.broadcasted_iota(jnp.int32, (TQ, 2 * BK), 1)
    window = (dist >= 0) & (dist < WINDOW)
    for hh in range(2):
        def add(b, acc):
            return acc + jnp.where(bucket_ref[...] == b, rb_ref[b, 2 * j + hh], 0.0)
        bias = lax.fori_loop(0, N_BUCKETS, add, jnp.zeros((TQ, 2 * BK), F32))
        bias_ref[hh * TQ:(hh + 1) * TQ, :] = jnp.where(window, bias, NEG)


def _swa_probs(qs, k2, bias, own_block, sink_ref, i, j):
    s = _dot_nt(qs, k2) + bias
    s = jnp.where(own_block | (i > 0), s, NEG)
    row1 = lax.broadcasted_iota(jnp.int32, (2 * TQ, 1), 0)
    sink = jnp.where(row1 < TQ, sink_ref[2 * j], sink_ref[2 * j + 1])
    m = jnp.maximum(jnp.max(s, axis=1, keepdims=True), sink)
    e = jnp.exp(s - m)
    es = jnp.exp(sink - m)
    inv = 1.0 / (jnp.sum(e, axis=1, keepdims=True) + es)
    return e * inv, es * inv


def _swa_kv(ref, i):
    prev = pl.multiple_of(jnp.maximum(i - 1, 0) * BK, BK)
    cur = pl.multiple_of(i * BK, BK)
    return jnp.concatenate([ref[pl.ds(prev, BK), :], ref[pl.ds(cur, BK), :]], axis=0), prev, cur


def _swa_fwd(qkv, sinks, rel_bias, name):
    s = qkv.shape[0]
    nq = s // TQ
    npair = SW_QW // LANES
    qcol = 3 * SB_W // LANES
    bucket = jnp.asarray(_bucket_table())

    def body(q_ref, k_ref, v_ref, bucket_ref, sink_ref, rb_ref, o_ref, bias_ref):
        j = pl.program_id(0)
        step = pl.program_id(1)
        g = j // 2
        lane = lax.broadcasted_iota(jnp.int32, (TQ, LANES), 1)

        @pl.when(step == 0)
        def _():
            _swa_bias(bias_ref, bucket_ref, rb_ref, j)

        own_block = lax.broadcasted_iota(jnp.int32, (2 * TQ, 2 * BK), 1) >= BK
        for b in range(SWA_QB):
            i = step * SWA_QB + b
            rows = slice(b * TQ, (b + 1) * TQ)
            qs = _swa_align_in(q_ref[rows, :] * SCALE, lane, g)
            k2, _, _ = _swa_kv(k_ref, i)
            v2, _, _ = _swa_kv(v_ref, i)
            pr, _ = _swa_probs(qs, k2, bias_ref[...], own_block, sink_ref, i, j)
            o_ref[rows, :] = _swa_align_out(_dot(pr.astype(BF16), v2), lane, g).astype(BF16)

    assert nq % SWA_QB == 0
    return pl.pallas_call(
        body, out_shape=jax.ShapeDtypeStruct((s, SW_QW), BF16), grid=(npair, nq // SWA_QB),
        in_specs=[pl.BlockSpec((SWA_QB * TQ, LANES), lambda j, i: (i, qcol + j)),
                  pl.BlockSpec((s, LANES), lambda j, i: (0, qcol + npair)),
                  pl.BlockSpec((s, LANES), lambda j, i: (0, qcol + npair + 1)),
                  pl.BlockSpec((TQ, 2 * BK), lambda j, i: (0, 0)),
                  pl.BlockSpec(memory_space=pltpu.SMEM),
                  pl.BlockSpec(memory_space=pltpu.SMEM)],
        out_specs=pl.BlockSpec((SWA_QB * TQ, LANES), lambda j, i: (i, j)),
        scratch_shapes=[pltpu.VMEM((2 * TQ, 2 * BK), F32)],
        compiler_params=_params(2), name=name)(qkv, qkv, qkv, bucket, sinks, rel_bias)


def _swa_bwd(qkv, ob, dob, sinks, rel_bias, name):
    s = qkv.shape[0]
    nq = s // TQ
    npair = SW_QW // LANES
    qcol = 3 * SB_W // LANES
    bucket = jnp.asarray(_bucket_table())

    def body(q_ref, k_ref, v_ref, o_ref, do_ref, bucket_ref, sink_ref, rb_ref,
             dq_ref, dkv_ref, dsink_ref, drel_ref, bias_ref, dsacc_ref, dk_ref, dv_ref):
        j = pl.program_id(0)
        step = pl.program_id(1)
        g = j // 2
        lane = lax.broadcasted_iota(jnp.int32, (TQ, LANES), 1)
        row8 = lax.broadcasted_iota(jnp.int32, (SW_HEADS, LANES), 0)
        lane8 = lax.broadcasted_iota(jnp.int32, (SW_HEADS, LANES), 1)

        @pl.when((step == 0) & (j == 0))
        def _():
            dk_ref[...] = jnp.zeros_like(dk_ref)
            dv_ref[...] = jnp.zeros_like(dv_ref)
            dsink_ref[...] = jnp.zeros_like(dsink_ref)
            drel_ref[...] = jnp.zeros_like(drel_ref)

        @pl.when(step == 0)
        def _():
            _swa_bias(bias_ref, bucket_ref, rb_ref, j)
            dsacc_ref[...] = jnp.zeros_like(dsacc_ref)

        own_block = lax.broadcasted_iota(jnp.int32, (2 * TQ, 2 * BK), 1) >= BK
        ds_sum = jnp.zeros(dsacc_ref.shape, F32)
        dsink = jnp.zeros((SW_HEADS, LANES), F32)
        for b in range(SWA_QB):
            i = step * SWA_QB + b
            rows = slice(b * TQ, (b + 1) * TQ)
            qs = _swa_align_in(q_ref[rows, :] * SCALE, lane, g)
            do = do_ref[rows, :]
            dos = _swa_align_in(do, lane, g)
            dof = do.astype(F32) * o_ref[rows, :].astype(F32)
            d0 = jnp.sum(jnp.where(lane < HEAD_DIM, dof, 0.0), axis=1, keepdims=True)
            d1 = jnp.sum(jnp.where(lane >= HEAD_DIM, dof, 0.0), axis=1, keepdims=True)
            delta = jnp.concatenate([d0, d1], axis=0)
            k2, prev, cur = _swa_kv(k_ref, i)
            v2, _, _ = _swa_kv(v_ref, i)
            pr, psink = _swa_probs(qs, k2, bias_ref[...], own_block, sink_ref, i, j)
            ds = pr * (_dot_nt(dos, v2) - delta)
            ds_sum = ds_sum + ds
            sd = psink * delta
            ds0 = -jnp.sum(sd[:TQ, :], axis=0, keepdims=True)
            ds1 = -jnp.sum(sd[TQ:, :], axis=0, keepdims=True)
            dsink = dsink + jnp.where(row8 == 2 * j, ds0, jnp.where(row8 == 2 * j + 1, ds1, 0.0))
            dsb = ds.astype(BF16)
            dq_ref[rows, :] = _swa_align_out(_dot(dsb, k2) * SCALE, lane, g).astype(BF16)
            dk2 = _dot_tn(dsb, qs)
            dv2 = _dot_tn(pr.astype(BF16), dos)
            dk_ref[pl.ds(prev, BK), :] += dk2[:BK, :]
            dk_ref[pl.ds(cur, BK), :] += dk2[BK:, :]
            dv_ref[pl.ds(prev, BK), :] += dv2[:BK, :]
            dv_ref[pl.ds(cur, BK), :] += dv2[BK:, :]
        dsacc_ref[...] += ds_sum
        dsink_ref[...] += dsink

        @pl.when(step == nq // SWA_QB - 1)
        def _():
            for hh in range(2):
                def red(b, acc):
                    val = _sum_all(jnp.where(bucket_ref[...] == b, dsacc_ref[hh * TQ:(hh + 1) * TQ, :], 0.0))
                    return jnp.where((row8 == 2 * j + hh) & (lane8 == b), val, acc)
                drel_ref[...] += lax.fori_loop(0, N_BUCKETS, red, jnp.zeros((SW_HEADS, LANES), F32))

        @pl.when((step == nq // SWA_QB - 1) & (j == npair - 1))
        def _():
            dkv_ref[:, :LANES] = dk_ref[...].astype(BF16)
            dkv_ref[:, LANES:] = dv_ref[...].astype(BF16)

    whole = lambda j, i: (0, 0)
    return pl.pallas_call(
        body,
        out_shape=[jax.ShapeDtypeStruct((s, SW_QW), BF16), jax.ShapeDtypeStruct((s, 2 * LANES), BF16),
                   jax.ShapeDtypeStruct((SW_HEADS, LANES), F32), jax.ShapeDtypeStruct((SW_HEADS, LANES), F32)],
        grid=(npair, nq // SWA_QB),
        in_specs=[pl.BlockSpec((SWA_QB * TQ, LANES), lambda j, i: (i, qcol + j)),
                  pl.BlockSpec((s, LANES), lambda j, i: (0, qcol + npair)),
                  pl.BlockSpec((s, LANES), lambda j, i: (0, qcol + npair + 1)),
                  pl.BlockSpec((SWA_QB * TQ, LANES), lambda j, i: (i, j)),
                  pl.BlockSpec((SWA_QB * TQ, LANES), lambda j, i: (i, j)),
                  pl.BlockSpec((TQ, 2 * BK), whole),
                  pl.BlockSpec(memory_space=pltpu.SMEM),
                  pl.BlockSpec(memory_space=pltpu.SMEM)],
        out_specs=[pl.BlockSpec((SWA_QB * TQ, LANES), lambda j, i: (i, j)),
                   pl.BlockSpec((s, 2 * LANES), whole),
                   pl.BlockSpec((SW_HEADS, LANES), whole), pl.BlockSpec((SW_HEADS, LANES), whole)],
        scratch_shapes=[pltpu.VMEM((2 * TQ, 2 * BK), F32), pltpu.VMEM((2 * TQ, 2 * BK), F32),
                        pltpu.VMEM((s, LANES), F32), pltpu.VMEM((s, LANES), F32)],
        compiler_params=_params(2), name=name)(qkv, qkv, qkv, ob, dob, bucket, sinks, rel_bias)


def _acc_init(i, *refs):
    @pl.when(i == 0)
    def _():
        for r in refs:
            r[...] = jnp.zeros_like(r)


def _loss_bwd(x3, target, g, name):
    d = x3.shape[1]

    def body(x_ref, t_ref, g_ref, dx_ref, dg_ref, loss_ref):
        _acc_init(pl.program_id(0), dg_ref, loss_ref)
        x = x_ref[...]
        gv = g_ref[...]
        diff = _rms(x, gv) - t_ref[...]
        loss_ref[...] += 0.5 * jnp.sum(jnp.mean(jnp.square(diff), axis=-1, keepdims=True), axis=0, keepdims=True)
        dx, dg = _rms_bwd(diff * (1.0 / d), x, gv)
        dx_ref[...] = dx
        dg_ref[...] += dg

    return _rowcall(name, body, [x3, target], [g], [(d, F32)], [((1, d), F32), ((1, LANES), F32)])


def _ple_bwd(dx3, pe, gt, x2, g, wpg, name):
    d = x2.shape[1]

    def body(dx3_ref, pe_ref, gt_ref, x_ref, g_ref, w_ref, dpe_ref, dgt_ref, dx2_ref, dg_ref):
        _acc_init(pl.program_id(0), dg_ref)
        dx3 = dx3_ref[...]
        sg = _sigmoid(gt_ref[...])
        dpe_ref[...] = (dx3 * sg).astype(BF16)
        dgt = (dx3 * pe_ref[...] * sg * (1.0 - sg)).astype(BF16)
        dgt_ref[...] = dgt
        dx, dg = _rms_bwd(_dot_nt(dgt, w_ref[...]), x_ref[...], g_ref[...])
        dx2_ref[...] = dx3 + dx
        dg_ref[...] += dg

    return _rowcall(name, body, [dx3, pe, gt, x2], [g, wpg], [(d, BF16), (d, BF16), (d, F32)], [((1, d), F32)])


def _ff2_bwd(dx2, u, w2, name):
    d = dx2.shape[1]
    dff = u.shape[1]

    def body(dx_ref, u_ref, w_ref, du_ref, dxb_ref):
        dxb = dx_ref[...].astype(BF16)
        dxb_ref[...] = dxb
        du_ref[...] = (_dot_nt(dxb, w_ref[...]) * (2.0 * jnp.maximum(u_ref[...], 0.0))).astype(BF16)

    return _rowcall(name, body, [dx2, u], [w2], [(dff, BF16), (d, BF16)])


def _ff1_bwd(du, dx2, x1, g, w1, name):
    d = x1.shape[1]

    def body(du_ref, dx2_ref, x_ref, g_ref, w_ref, dx1_ref, dx1b_ref, dg_ref):
        _acc_init(pl.program_id(0), dg_ref)
        dx, dg = _rms_bwd(_dot_cols_t(du_ref[...], w_ref), x_ref[...], g_ref[...])
        dx1 = dx2_ref[...] + dx
        dx1_ref[...] = dx1
        dx1b_ref[...] = dx1.astype(BF16)
        dg_ref[...] += dg

    return _rowcall(name, body, [du, dx2, x1], [g, w1], [(d, F32), (d, BF16)], [((1, d), F32)])


def _mixer_bwd(dx1b, gates, oa, ob, wo, wua, wub, name, side=None):
    d = dx1b.shape[1]

    def body(dx_ref, gate_ref, oa_ref, ob_ref, wo_ref, wua_ref, wub_ref,
             dya_ref, dyb_ref, dgate_ref, doa_ref, dob_ref):
        dm = _dot_nt(dx_ref[...], wo_ref[...])
        sa = _sigmoid(gate_ref[:, :d])
        sb = _sigmoid(gate_ref[:, d:])
        ya = _dot_cols(oa_ref[...], wua_ref)
        yb = _dot_cols(ob_ref[...], wub_ref)
        dya = (dm * sa).astype(BF16)
        dyb = (dm * sb).astype(BF16)
        dya_ref[...] = dya
        dyb_ref[...] = dyb
        dgate_ref[:, :d] = (dm * ya * sa * (1.0 - sa)).astype(BF16)
        dgate_ref[:, d:] = (dm * yb * sb * (1.0 - sb)).astype(BF16)
        doa_ref[...] = _dot_cols_t(dya, wua_ref).astype(BF16)
        dob_ref[...] = _dot_cols_t(dyb, wub_ref).astype(BF16)

    return _rowcall(name, body, [dx1b, gates, oa, ob], [wo, wua, wub],
                    [(d, BF16), (d, BF16), (2 * d, BF16), (SB_W, BF16), (SW_QW, BF16)], side=side)


def _inproj_bwd(pieces, dx1, x, g, wt, name, side=None):
    d = x.shape[1]
    n = len(pieces)
    offsets = [sum(pc.shape[1] for pc in pieces[:p]) for p in range(n + 1)]

    def body(*refs):
        dx1_ref, x_ref, g_ref, w_ref, dx_ref, dg_ref = refs[n:]
        _acc_init(pl.program_id(0), dg_ref)
        dh = _dot(refs[0][...], w_ref[:offsets[1], :])
        for p in range(1, n):
            dh = dh + _dot(refs[p][...], w_ref[offsets[p]:offsets[p + 1], :])
        dx, dg = _rms_bwd(dh, x_ref[...], g_ref[...])
        dx_ref[...] = dx1_ref[...] + dx
        dg_ref[...] += dg

    return _rowcall(name, body, list(pieces) + [dx1, x], [g, wt], [(d, F32)], [((1, d), F32)], side=side)


def _tile(n, cap):
    assert n % LANES == 0
    return max(t for t in range(LANES, min(n, cap) + 1, LANES) if n % t == 0)


def _mm_tn(a, b, name, nshard=1):
    s, ka = a.shape
    nb = b.shape[1]
    n = nb // nshard
    ta = _tile(ka, 512)
    tb = _tile(n, 1024)
    per = n // tb

    def body(a_ref, b_ref, o_ref):
        o_ref[...] = _dot_tn(a_ref[...].astype(BF16), b_ref[...].astype(BF16))

    return pl.pallas_call(
        body, out_shape=jax.ShapeDtypeStruct((nshard, ka, n), F32), grid=(nb // tb, ka // ta),
        in_specs=[pl.BlockSpec((s, ta), lambda jb, ia: (0, ia)), pl.BlockSpec((s, tb), lambda jb, ia: (0, jb))],
        out_specs=pl.BlockSpec((None, ta, tb), lambda jb, ia: (jb // per, ia, jb % per)),
        compiler_params=_params(2), name=name)(a, b)


def _mm_tn_pieces(pieces, b, name):
    s, nb = b.shape
    ta = 256
    tiles = [pc.shape[1] // ta for pc in pieces]
    assert all(pc.shape[1] % ta == 0 for pc in pieces)
    starts = [sum(tiles[:p]) for p in range(len(pieces))]
    tb = _tile(nb, 1024)

    def body(*refs):
        a_refs, b_ref, o_ref = refs[:-2], refs[-2], refs[-1]
        ia = pl.program_id(1)
        for p in range(len(pieces)):
            @pl.when((ia >= starts[p]) & (ia < starts[p] + tiles[p]))
            def _(p=p):
                o_ref[...] = _dot_tn(a_refs[p][...], b_ref[...])

    def piece_spec(p):
        return pl.BlockSpec((s, ta), lambda jb, ia: (0, jnp.clip(ia - starts[p], 0, tiles[p] - 1)))

    return pl.pallas_call(
        body, out_shape=jax.ShapeDtypeStruct((sum(tiles) * ta, nb), F32), grid=(nb // tb, sum(tiles)),
        in_specs=[piece_spec(p) for p in range(len(pieces))] + [pl.BlockSpec((s, tb), lambda jb, ia: (0, jb))],
        out_specs=pl.BlockSpec((ta, tb), lambda jb, ia: (ia, jb)),
        compiler_params=_params(2), name=name)(*pieces, b)


def _place():
    return lax.axis_index("x"), lax.axis_index("y"), lax.axis_index("c")


def _chip_peer(x, y, k):
    return (x ^ (k >> 1), y ^ (k & 1))


def _row_tile(k, cap=544):
    return max(t for t in range(32, min(k, cap) + 1, 32) if k % t == 0)


def _cast_bf16(w, r, name):
    l, k, n = w.shape
    assert l == 2
    tk = _row_tile(k)

    def body(r_ref, w_ref, o0_ref, o1_ref):
        o0_ref[...] = w_ref[0].astype(BF16)
        o1_ref[...] = w_ref[1].astype(BF16)

    out_spec = pl.BlockSpec((None, tk, n), lambda i, r_ref: (r_ref[0], i, 0))
    return pl.pallas_call(
        body, out_shape=[jax.ShapeDtypeStruct((4, k, n), BF16)] * 2,
        grid_spec=pltpu.PrefetchScalarGridSpec(
            num_scalar_prefetch=1, grid=(k // tk,),
            in_specs=[pl.BlockSpec((l, tk, n), lambda i, r_ref: (0, i, 0))],
            out_specs=[out_spec, out_spec]),
        compiler_params=_params(1), name=name)(r, w)


class _Exchange(NamedTuple):
    arrays: tuple
    aliased: tuple
    sems: tuple
    start: Callable
    finish: Callable


def _all_gather(bufs):
    nt = len(bufs)

    def ici(t, ins, outs, send_sems, recv_sems, x, y, c, r, k):
        return pltpu.make_async_remote_copy(
            src_ref=ins[t].at[r, c], dst_ref=outs[t].at[r, c], send_sem=send_sems.at[t, k - 1],
            recv_sem=recv_sems.at[t, k - 1], device_id=(*_chip_peer(x, y, k), c), device_id_type=MESH)

    def d2d(t, outs, send_sems, recv_sems, x, y, c, r, k, half):
        slab = outs[t].at[r ^ k, half]
        return pltpu.make_async_remote_copy(
            src_ref=slab, dst_ref=slab, send_sem=send_sems.at[t, 2 + k], recv_sem=recv_sems.at[t, 2 + k],
            device_id=(x, y, 1 - c), device_id_type=MESH)

    def start(ins, outs, send_sems, recv_sems):
        x, y, c = _place()
        r = 2 * x + y
        for t in range(nt):
            for k in (1, 2, 3):
                ici(t, ins, outs, send_sems, recv_sems, x, y, c, r, k).start()

    def finish(ins, outs, send_sems, recv_sems):
        x, y, c = _place()
        r = 2 * x + y
        for t in range(nt):
            for k in (1, 2, 3):
                slab = outs[t].at[r ^ k, c]
                pltpu.make_async_remote_copy(
                    src_ref=slab, dst_ref=slab, send_sem=send_sems.at[t, k - 1], recv_sem=recv_sems.at[t, k - 1],
                    device_id=(x, y, 1 - c), device_id_type=MESH).wait_recv()
                d2d(t, outs, send_sems, recv_sems, x, y, c, r, k, c).start()
        for t in range(nt):
            for k in (1, 2, 3):
                d2d(t, outs, send_sems, recv_sems, x, y, c, r, k, 1 - c).wait_recv()
        for t in range(nt):
            for k in (1, 2, 3):
                ici(t, ins, outs, send_sems, recv_sems, x, y, c, r, k).wait_send()
                d2d(t, outs, send_sems, recv_sems, x, y, c, r, k, c).wait_send()

    return _Exchange(tuple(bufs), tuple(range(nt)), (nt, 6), start, finish)


def _run_exchange(name, ex):
    n_in, n_out = len(ex.arrays), len(ex.aliased)

    def body(*refs):
        ins, outs = refs[:n_in], refs[n_in:n_in + n_out]
        send_sems, recv_sems = refs[n_in + n_out:]
        ex.start(ins, outs, send_sems, recv_sems)
        ex.finish(ins, outs, send_sems, recv_sems)

    any_spec = pl.BlockSpec(memory_space=pl.ANY)
    return pl.pallas_call(
        body, out_shape=[jax.ShapeDtypeStruct(ex.arrays[a].shape, ex.arrays[a].dtype) for a in ex.aliased],
        in_specs=[any_spec] * n_in, out_specs=[any_spec] * n_out,
        input_output_aliases={a: o for o, a in enumerate(ex.aliased)},
        scratch_shapes=[pltpu.SemaphoreType.DMA(ex.sems), pltpu.SemaphoreType.DMA(ex.sems)],
        name=name)(*ex.arrays)


def _rs_to_sibling(grads):
    nt = len(grads)
    landing = [lax.empty((4,) + g.shape[2:], F32) for g in grads]

    def copies(ins, outs, send_sems, recv_sems):
        x, y, c = _place()
        return [pltpu.make_async_remote_copy(
            src_ref=ins[t].at[:, 1 - c], dst_ref=outs[t], send_sem=send_sems.at[t], recv_sem=recv_sems.at[t],
            device_id=(x, y, 1 - c), device_id_type=MESH) for t in range(nt)]

    def start(ins, outs, send_sems, recv_sems):
        for cp in copies(ins, outs, send_sems, recv_sems):
            cp.start()

    def finish(ins, outs, send_sems, recv_sems):
        for cp in copies(ins, outs, send_sems, recv_sems):
            cp.wait()

    return _Exchange(tuple(grads) + tuple(landing), tuple(range(nt, 2 * nt)), (nt,), start, finish)


def _add_half(g, recv, cr, name):
    _, _, k2, n = g.shape
    tk = _row_tile(k2)

    def body(cr_ref, g_ref, r_ref, sums_ref, mine_ref):
        val = (g_ref[...] + r_ref[...]).astype(BF16)
        sums_ref[...] = val

        @pl.when(pl.program_id(1) == cr_ref[1])
        def _():
            mine_ref[...] = val

    return pl.pallas_call(
        body, out_shape=[jax.ShapeDtypeStruct((4, k2, n), BF16)] * 2,
        grid_spec=pltpu.PrefetchScalarGridSpec(
            num_scalar_prefetch=1, grid=(k2 // tk, 4),
            in_specs=[pl.BlockSpec((None, None, tk, n), lambda i, q, cr_ref: (q, cr_ref[0], i, 0)),
                      pl.BlockSpec((None, tk, n), lambda i, q, cr_ref: (q, i, 0))],
            out_specs=[pl.BlockSpec((None, tk, n), lambda i, q, cr_ref: (q, i, 0)),
                       pl.BlockSpec((None, tk, n), lambda i, q, cr_ref: (cr_ref[1], i, 0))]),
        compiler_params=_params(2), name=name)(cr, g, recv)


def _rs_to_chips(sums, parts):
    nt = len(sums)

    def copies(ins, outs, send_sems, recv_sems):
        x, y, c = _place()
        r = 2 * x + y
        return [pltpu.make_async_remote_copy(
            src_ref=ins[t].at[r ^ k], dst_ref=outs[t].at[r], send_sem=send_sems.at[t, k - 1],
            recv_sem=recv_sems.at[t, k - 1], device_id=(*_chip_peer(x, y, k), c), device_id_type=MESH)
            for t in range(nt) for k in (1, 2, 3)]

    def start(ins, outs, send_sems, recv_sems):
        for cp in copies(ins, outs, send_sems, recv_sems):
            cp.start()

    def finish(ins, outs, send_sems, recv_sems):
        for cp in copies(ins, outs, send_sems, recv_sems):
            cp.wait()

    return _Exchange(tuple(sums) + tuple(parts), tuple(range(nt, 2 * nt)), (nt, 3), start, finish)


def _sum4(parts, cr, name):
    _, k2, n = parts.shape
    tk = _row_tile(k2)

    def body(cr_ref, p_ref, o_ref):
        p = p_ref[...].astype(F32)
        o_ref[...] = ((p[0] + p[1]) + p[2]) + p[3]

    return pl.pallas_call(
        body, out_shape=jax.ShapeDtypeStruct((2, k2, n), F32),
        grid_spec=pltpu.PrefetchScalarGridSpec(
            num_scalar_prefetch=1, grid=(k2 // tk,),
            in_specs=[pl.BlockSpec((4, tk, n), lambda i, cr_ref: (0, i, 0))],
            out_specs=pl.BlockSpec((None, tk, n), lambda i, cr_ref: (cr_ref[0], i, 0))),
        compiler_params=_params(1), name=name)(cr, parts)


def _exchange_halves(both, name):
    nt = len(both)

    def body(*refs):
        ins, outs = refs[:nt], refs[nt:2 * nt]
        send_sems, recv_sems = refs[2 * nt:]
        x, y, c = _place()
        cps = []
        for t in range(nt):
            cp = pltpu.make_async_remote_copy(
                src_ref=ins[t].at[c], dst_ref=outs[t].at[c], send_sem=send_sems.at[t], recv_sem=recv_sems.at[t],
                device_id=(x, y, 1 - c), device_id_type=MESH)
            cp.start()
            cps.append(cp)
        for cp in cps:
            cp.wait()

    any_spec = pl.BlockSpec(memory_space=pl.ANY)
    return pl.pallas_call(
        body, out_shape=[jax.ShapeDtypeStruct(b.shape, F32) for b in both],
        in_specs=[any_spec] * nt, out_specs=[any_spec] * nt,
        input_output_aliases={t: t for t in range(nt)},
        scratch_shapes=[pltpu.SemaphoreType.DMA((nt,)), pltpu.SemaphoreType.DMA((nt,))],
        name=name)(*both)


def _adamw_math(w, g, m, v):
    m = ADAM_B1 * m + (1.0 - ADAM_B1) * g
    v = ADAM_B2 * v + (1.0 - ADAM_B2) * jnp.square(g)
    m_hat = m / (1.0 - ADAM_B1 ** ADAM_STEP)
    v_hat = v / (1.0 - ADAM_B2 ** ADAM_STEP)
    delta = -ADAM_LR * (m_hat / (jnp.sqrt(v_hat) + ADAM_EPS) + ADAM_WD * w)
    return delta, m, v


def _adamw(w, m, v, g0, g1, name):
    _, k, n = w.shape
    tk = _row_tile(k)
    nk = k // tk

    def body(w_ref, m_ref, v_ref, g0_ref, g1_ref, grad_ref, delta_ref, nm_ref, nv_ref):
        g = jnp.where(pl.program_id(0) == 0, g0_ref[...], g1_ref[...])
        delta, nm, nv = _adamw_math(w_ref[...], g, m_ref[...], v_ref[...])
        grad_ref[...] = g
        delta_ref[...] = delta
        nm_ref[...] = nm
        nv_ref[...] = nv

    lay = pl.BlockSpec((None, tk, n), lambda a, i: (a, i, 0))
    g0_spec = pl.BlockSpec((tk, n), lambda a, i: (jnp.where(a == 0, i, nk - 1), 0))
    g1_spec = pl.BlockSpec((tk, n), lambda a, i: (jnp.where(a == 1, i, 0), 0))
    return pl.pallas_call(
        body, out_shape=[jax.ShapeDtypeStruct(w.shape, F32)] * 4, grid=(2, nk),
        in_specs=[lay, lay, lay, g0_spec, g1_spec], out_specs=[lay] * 4,
        compiler_params=_params(2), name=name)(w, m, v, g0, g1)


def _small_allreduce_adamw(gpart, w, m, v):
    shape = gpart.shape

    def body(g_ref, w_ref, m_ref, v_ref, gsum_ref, delta_ref, nm_ref, nv_ref, recv_ref, send_sems, recv_sems):
        x, y, c = _place()
        me = 4 * x + 2 * y + c
        recv_ref[me] = g_ref[...]
        cps = []
        for k in range(1, 8):
            peer = (x ^ (k >> 2), y ^ ((k >> 1) & 1), c ^ (k & 1))
            cp = pltpu.make_async_remote_copy(
                src_ref=g_ref, dst_ref=recv_ref.at[me], send_sem=send_sems.at[k - 1], recv_sem=recv_sems.at[k - 1],
                device_id=peer, device_id_type=MESH)
            cp.start()
            cps.append(cp)
        for cp in cps:
            cp.wait()
        g = recv_ref[0]
        for dev in range(1, 8):
            g = g + recv_ref[dev]
        delta, nm, nv = _adamw_math(w_ref[...], g, m_ref[...], v_ref[...])
        gsum_ref[...] = g
        delta_ref[...] = delta
        nm_ref[...] = nm
        nv_ref[...] = nv

    vm = pl.BlockSpec(memory_space=pltpu.VMEM)
    return pl.pallas_call(
        body, out_shape=[jax.ShapeDtypeStruct(shape, F32)] * 4, in_specs=[vm] * 4, out_specs=[vm] * 4,
        scratch_shapes=[pltpu.VMEM((8,) + shape, F32), pltpu.SemaphoreType.DMA((7,)), pltpu.SemaphoreType.DMA((7,))],
        name="small_allreduce_adamw")(gpart, w, m, v)


BIG = ("w_in", "w_up_a", "w_up_b", "w_o", "w_ff1", "w_ff2", "w_pe", "w_pg")
COL_SHARDED = ("w_in", "w_up_a", "w_up_b", "w_ff1", "w_pe")
ROW_SHARDED = ("w_o", "w_ff2", "w_pg")
SMALL_ROWS = 16


def _pack_small(g_mix, g_mlp, g_pe, g_final, sinks, rel_bias, loss=None):
    d = g_final.shape[0]
    row = lambda v: jnp.pad(v.reshape(1, -1), ((0, 0), (0, d - v.size)))
    rows = [g_mix, g_mlp, g_pe, g_final.reshape(1, d),
            jnp.zeros((1, d), F32) if loss is None else row(loss), row(sinks), row(rel_bias)]
    out = jnp.concatenate(rows, axis=0)
    return jnp.pad(out, ((0, SMALL_ROWS - out.shape[0]), (0, 0)))


def _unpack_small(a, sinks_shape, rel_shape):
    return (a[0:2], a[2:4], a[4:6], a[6], a[8, :sinks_shape[0] * sinks_shape[1]].reshape(sinks_shape),
            a[9, :rel_shape[0] * rel_shape[1]].reshape(rel_shape))


def kernel(x, p, w_in, w_up_a, w_up_b, w_o, w_ff1, w_ff2, w_pe, w_pg, g_mix, g_mlp, g_pe, g_final, sinks, rel_bias, loss_target, m_w_in, m_w_up_a, m_w_up_b, m_w_o, m_w_ff1, m_w_ff2, m_w_pe, m_w_pg, m_g_mix, m_g_mlp, m_g_pe, m_g_final, m_sinks, m_rel_bias, v_w_in, v_w_up_a, v_w_up_b, v_w_o, v_w_ff1, v_w_ff2, v_w_pe, v_w_pg, v_g_mix, v_g_mlp, v_g_pe, v_g_final, v_sinks, v_rel_bias):
    depth = w_in.shape[0]
    assert depth == 2
    x0 = x[0]
    target = loss_target[0]
    d = x0.shape[1]
    wl = dict(w_in=w_in, w_up_a=w_up_a, w_up_b=w_up_b, w_o=w_o, w_ff1=w_ff1, w_ff2=w_ff2, w_pe=w_pe, w_pg=w_pg)
    ml = dict(w_in=m_w_in, w_up_a=m_w_up_a, w_up_b=m_w_up_b, w_o=m_w_o, w_ff1=m_w_ff1, w_ff2=m_w_ff2, w_pe=m_w_pe, w_pg=m_w_pg)
    vl = dict(w_in=v_w_in, w_up_a=v_w_up_a, w_up_b=v_w_up_b, w_o=v_w_o, w_ff1=v_w_ff1, w_ff2=v_w_ff2, w_pe=v_w_pe, w_pg=v_w_pg)
    c_idx = lax.axis_index("c").astype(jnp.int32)
    r_idx = (2 * lax.axis_index("x") + lax.axis_index("y")).astype(jnp.int32)
    cr = jnp.stack([c_idx, r_idx])

    wl["w_in"], ml["w_in"], vl["w_in"] = (jnp.swapaxes(a, 1, 2) for a in (w_in, m_w_in, v_w_in))

    bufs = {}
    for n in BIG:
        k, nn = wl[n].shape[1:]
        for l, b in enumerate(_cast_bf16(wl[n], r_idx.reshape(1), "cast_" + n)):
            bufs[n, l] = b.reshape(4, 2, k // 2, nn)

    def gather(keys, run):
        for key, b in zip(keys, run(_all_gather([bufs[key] for key in keys]))):
            bufs[key] = b

    def gathered(n, l):
        _, _, k2, nn = bufs[n, l].shape
        if n in ROW_SHARDED or n == "w_in":
            return bufs[n, l].reshape(8 * k2, nn)
        return bufs[n, l].reshape(4, 2 * k2, nn)

    gather([("w_in", 0)], lambda ex: _run_exchange("all_gather_first", ex))

    full = {n: [None] * depth for n in BIG}
    saved = []
    xi = x0
    for i in range(depth):
        st = dict(x0=xi)
        gm = g_mix[i].reshape(1, d)
        full["w_in"][i] = gathered("w_in", i)
        st["h1"], st["qkv"], st["gates"] = _inproj_fwd(xi, gm, full["w_in"][i], f"inproj_fwd_{i}")

        def attend(ex):
            st["oa"], st["lt"], st["nb"], filled = _sb_fwd(st["qkv"], f"sb_fwd_{i}", ex)
            return filled

        gather([(n, i) for n in BIG if n != "w_in"], attend)
        for n in BIG:
            if n != "w_in":
                full[n][i] = gathered(n, i)
        st["ob"] = _swa_fwd(st["qkv"], sinks[i], rel_bias, f"swa_fwd_{i}")
        st["m"], st["x1"] = _mixer_fwd(st["oa"], st["ob"], st["gates"], xi, full["w_up_a"][i], full["w_up_b"][i],
                                       full["w_o"][i], f"mixer_fwd_{i}")
        if i == 0:
            def feed_forward(ex):
                (st["h2"], st["u"], st["a"]), filled = _ff1_fwd(st["x1"], g_mlp[i].reshape(1, d), full["w_ff1"][i],
                                                               f"ff1_fwd_{i}", ex)
                return filled

            gather([("w_in", 1)], feed_forward)
        else:
            st["h2"], st["u"], st["a"] = _ff1_fwd(st["x1"], g_mlp[i].reshape(1, d), full["w_ff1"][i], f"ff1_fwd_{i}")
        st["x2"] = _ff2_fwd(st["a"], st["x1"], full["w_ff2"][i], f"ff2_fwd_{i}")
        st["pb"], st["h3"], st["pe"], st["gt"], xi = _ple_fwd(p[i, 0], st["x2"], g_pe[i].reshape(1, d),
                                                            full["w_pe"][i], full["w_pg"][i], f"ple_fwd_{i}")
        saved.append(st)

    dx, dg_final, loss_part = _loss_bwd(xi, target, g_final.reshape(1, d), "loss_bwd")
    gw = {n: [None] * depth for n in BIG}
    reduced = {}

    chip_sums = {}

    def to_sibling(keys, run):
        tensors = []
        for n, l in keys:
            g = gw[n][l]
            if n in ROW_SHARDED:
                ka, nb = g.shape[1:]
                g = g.reshape(4, ka // 4, nb)
            _, k, nn = g.shape
            tensors.append(g.reshape(4, 2, k // 2, nn))
        for (n, l), g, r in zip(keys, tensors, run(_rs_to_sibling(tensors))):
            chip_sums[n, l] = _add_half(g, r, cr, f"add_half_{n}_{l}")

    def to_chips(keys):
        return _rs_to_chips([chip_sums[k][0] for k in keys], [chip_sums[k][1] for k in keys])

    def scatter_back(keys, parts, tag):
        halves = [_sum4(pc, cr, f"sum4_{n}_{l}") for (n, l), pc in zip(keys, parts)]
        for key, both in zip(keys, _exchange_halves(halves, f"exchange_halves_{tag}")):
            reduced[key] = both

    dg_mix, dg_mlp, dg_pe, dsinks = [None] * depth, [None] * depth, [None] * depth, [None] * depth
    drel = jnp.zeros((SW_HEADS, LANES), F32)
    for i in reversed(range(depth)):
        st = saved[i]
        dpe, dgt, dx2, dg_pe[i] = _ple_bwd(dx, st["pe"], st["gt"], st["x2"], g_pe[i].reshape(1, d),
                                           full["w_pg"][i], f"ple_bwd_{i}")
        gw["w_pe"][i] = _mm_tn(st["pb"], dpe, f"dw_pe_{i}", 4)
        gw["w_pg"][i] = _mm_tn(st["h3"], dgt, f"dw_pg_{i}")
        du, dx2b = _ff2_bwd(dx2, st["u"], full["w_ff2"][i], f"ff2_bwd_{i}")
        gw["w_ff2"][i] = _mm_tn(st["a"], dx2b, f"dw_ff2_{i}")
        gw["w_ff1"][i] = _mm_tn(st["h2"], du, f"dw_ff1_{i}", 4)
        dx1, dx1b, dg_mlp[i] = _ff1_bwd(du, dx2, st["x1"], g_mlp[i].reshape(1, d), full["w_ff1"][i], f"ff1_bwd_{i}")
        gw["w_o"][i] = _mm_tn(st["m"], dx1b, f"dw_o_{i}")
        early = [(n, i) for n in ("w_pe", "w_pg", "w_ff2", "w_ff1", "w_o")]

        def mixer(ex):
            (dya, dyb, dgates, doa, dob), landed = _mixer_bwd(
                dx1b, st["gates"], st["oa"], st["ob"], full["w_o"][i], full["w_up_a"][i], full["w_up_b"][i],
                f"mixer_bwd_{i}", ex)
            st.update(dya=dya, dyb=dyb, dgates=dgates, doa=doa, dob=dob)
            return landed

        to_sibling(early, mixer)
        dgates = st["dgates"]
        gw["w_up_a"][i] = _mm_tn(st["oa"], st["dya"], f"dw_up_a_{i}", 4)
        gw["w_up_b"][i] = _mm_tn(st["ob"], st["dyb"], f"dw_up_b_{i}", 4)
        late = [("w_up_a", i), ("w_up_b", i)]
        to_sibling(late, lambda ex: _run_exchange(f"rs_to_sibling_{i}", ex))
        keys = early + late + ([("w_in", 1)] if i == 0 else [])
        dqa, dka, dva, parts = _sb_bwd(st["qkv"], st["lt"], st["nb"], st["doa"], f"sb_bwd_{i}", to_chips(keys))
        scatter_back(keys, parts, i)
        dob = st["dob"]
        dqb, dkvb, dsk, drl = _swa_bwd(st["qkv"], st["ob"], dob, sinks[i], rel_bias, f"swa_bwd_{i}")
        dsinks[i] = dsk[:, 0]
        drel = drel + drl
        dproj = [dqa, dka, dva, dqb, dkvb, dgates]
        dw_in_t = _mm_tn_pieces(dproj, st["h1"], f"dw_in_{i}")
        gw["w_in"][i] = dw_in_t.reshape(4, dw_in_t.shape[0] // 4, d)
        if i == 1:
            def inproj(ex):
                (dx, dg_mix[i]), landed = _inproj_bwd(dproj, dx1, st["x0"], g_mix[i].reshape(1, d),
                                                      full["w_in"][i], f"inproj_bwd_{i}", ex)
                st["dx"] = dx
                return landed

            to_sibling([("w_in", 1)], inproj)
            dx = st["dx"]
        else:
            keys = [("w_in", 0)]
            to_sibling(keys, lambda ex: _run_exchange("rs_to_sibling_last", ex))
            (dx, dg_mix[i]), parts = _inproj_bwd(dproj, dx1, st["x0"], g_mix[i].reshape(1, d),
                                                 full["w_in"][i], f"inproj_bwd_{i}", to_chips(keys))
            scatter_back(keys, parts, "last")
    grad_x = dx[None]

    outs = {}
    for n in BIG:
        g0, g1 = (reduced[n, l].reshape(wl[n].shape[1:]) for l in range(depth))
        outs[n] = _adamw(wl[n], ml[n], vl[n], g0, g1, "adamw_" + n)
    outs["w_in"] = [jnp.swapaxes(a, 1, 2) for a in outs["w_in"]]

    drel_bias = drel[:, :N_BUCKETS].T
    gsmall = _pack_small(jnp.concatenate(dg_mix, 0), jnp.concatenate(dg_mlp, 0), jnp.concatenate(dg_pe, 0),
                         dg_final[0], jnp.stack(dsinks), drel_bias, loss_part[0, :1])
    wsmall = _pack_small(g_mix, g_mlp, g_pe, g_final, sinks, rel_bias)
    msmall = _pack_small(m_g_mix, m_g_mlp, m_g_pe, m_g_final, m_sinks, m_rel_bias)
    vsmall = _pack_small(v_g_mix, v_g_mlp, v_g_pe, v_g_final, v_sinks, v_rel_bias)
    small = _small_allreduce_adamw(gsmall, wsmall, msmall, vsmall)
    loss = small[0][7, 0]
    small = [_unpack_small(a, sinks.shape, rel_bias.shape) for a in small]

    result = [loss, grad_x]
    for kind in range(4):
        result += [outs[n][kind] for n in BIG]
        result += list(small[kind])
    return tuple(result)
```

```python
import functools
import math
from typing import Callable, NamedTuple

import numpy as np
import jax
import jax.numpy as jnp
from jax import lax
from jax.experimental import pallas as pl
from jax.experimental.pallas import tpu as pltpu

F32 = jnp.float32
BF16 = jnp.bfloat16
MESH = pl.DeviceIdType.MESH

HEAD_DIM = 64
SB_HEADS = 8
SW_HEADS = 8
SW_KV_HEADS = 2
WINDOW = 128
N_BUCKETS = 32
MAX_DISTANCE = 128
EPS = 1e-6
SB_W = SB_HEADS * HEAD_DIM
SW_QW = SW_HEADS * HEAD_DIM
SW_KVW = SW_KV_HEADS * HEAD_DIM
QKV_W = 3 * SB_W + SW_QW + 2 * SW_KVW
SCALE = HEAD_DIM ** -0.5
assert SCALE == 0.125
LANES = 128
TQ = 128
BK = 128
NEG = -1e30
SB_EXHAUSTED = -106.0

ADAM_LR = 0.001
ADAM_B1 = 0.9
ADAM_B2 = 0.999
ADAM_EPS = 1e-08
ADAM_WD = 0.01
ADAM_STEP = 10

VMEM_LIMIT = 56 * 1024 * 1024


def _dot(a, b):
    return jnp.dot(a, b, preferred_element_type=F32)


def _dot_nt(a, b):
    return lax.dot_general(a, b, (((1,), (1,)), ((), ())), preferred_element_type=F32)


def _dot_tn(a, b):
    return lax.dot_general(a, b, (((0,), (0,)), ((), ())), preferred_element_type=F32)


def _sum_all(x):
    return jnp.sum(jnp.sum(x, axis=1, keepdims=True), axis=0, keepdims=True)


def _sigmoid(x):
    return 1.0 / (1.0 + jnp.exp(-x))


def _rms(x, g):
    r = lax.rsqrt(jnp.mean(x * x, axis=-1, keepdims=True) + EPS)
    return (x * r) * g


def _rms_bwd(dy, x, g):
    r = lax.rsqrt(jnp.mean(x * x, axis=-1, keepdims=True) + EPS)
    n = x * r
    dg = jnp.sum(dy * n, axis=0, keepdims=True)
    dn = dy * g
    dx = r * (dn - n * jnp.mean(dn * n, axis=-1, keepdims=True))
    return dx, dg


def _params(n_axes):
    return pltpu.CompilerParams(dimension_semantics=("arbitrary",) * n_axes, vmem_limit_bytes=VMEM_LIMIT)


def _rowcall(name, body, row_ins, const_ins, row_outs, acc_outs=(), tm=512, side=None):
    s = row_ins[0].shape[0]
    assert s % tm == 0
    in_specs = [pl.BlockSpec((tm, a.shape[1]), lambda i: (i, 0)) for a in row_ins]
    in_specs += [pl.BlockSpec(a.shape, functools.partial(lambda i, nd: (0,) * nd, nd=a.ndim)) for a in const_ins]
    out_shape = [jax.ShapeDtypeStruct((s, c), dt) for c, dt in row_outs]
    out_specs = [pl.BlockSpec((tm, c), lambda i: (i, 0)) for c, _ in row_outs]
    out_shape += [jax.ShapeDtypeStruct(sh, dt) for sh, dt in acc_outs]
    out_specs += [pl.BlockSpec(sh, functools.partial(lambda i, nd: (0,) * nd, nd=len(sh))) for sh, _ in acc_outs]
    if side is None:
        return pl.pallas_call(body, out_shape=out_shape, grid=(s // tm,), in_specs=in_specs, out_specs=out_specs,
                              compiler_params=_params(1), name=name)(*row_ins, *const_ins)
    n_in, n_out = len(in_specs), len(out_specs)

    def with_side(*refs):
        side_in, outs, side_out, sems = _side_refs(side, refs[n_in:], n_out)

        @pl.when(pl.program_id(0) == 0)
        def _():
            side.start(side_in, side_out, *sems)

        body(*refs[:n_in], *outs)

        @pl.when(pl.program_id(0) == s // tm - 1)
        def _():
            side.finish(side_in, side_out, *sems)

    s_in, s_shape, s_out, s_alias, s_sems = _side_specs(side, n_in, n_out)
    outs = pl.pallas_call(with_side, out_shape=out_shape + s_shape, grid=(s // tm,), in_specs=in_specs + s_in,
                          out_specs=out_specs + s_out, input_output_aliases=s_alias, scratch_shapes=s_sems,
                          compiler_params=_params(1), name=name)(*row_ins, *const_ins, *side.arrays)
    return outs[:n_out], outs[n_out:]


def _dot_cols(a, w_ref):
    return jnp.concatenate([_dot(a, w_ref[r]) for r in range(w_ref.shape[0])], axis=1)


def _dot_cols_t(a, w_ref):
    n = w_ref.shape[2]
    out = _dot_nt(a[:, :n], w_ref[0])
    for r in range(1, w_ref.shape[0]):
        out = out + _dot_nt(a[:, r * n:(r + 1) * n], w_ref[r])
    return out


def _inproj_fwd(x, g, wt, name):
    d = x.shape[1]

    def body(x_ref, g_ref, w_ref, h_ref, qkv_ref, gate_ref):
        hb = _rms(x_ref[...], g_ref[...]).astype(BF16)
        h_ref[...] = hb
        qkv_ref[...] = _dot_nt(hb, w_ref[:QKV_W, :]).astype(BF16)
        gate_ref[...] = _dot_nt(hb, w_ref[QKV_W:, :])

    return _rowcall(name, body, [x], [g, wt], [(d, BF16), (QKV_W, BF16), (2 * d, F32)])


def _mixer_fwd(oa, ob, gates, x, wua, wub, wo, name):
    d = x.shape[1]

    def body(oa_ref, ob_ref, gate_ref, x_ref, wua_ref, wub_ref, wo_ref, m_ref, x1_ref):
        ya = _dot_cols(oa_ref[...], wua_ref)
        yb = _dot_cols(ob_ref[...], wub_ref)
        m = _sigmoid(gate_ref[:, :d]) * ya + _sigmoid(gate_ref[:, d:]) * yb
        mb = m.astype(BF16)
        m_ref[...] = mb
        x1_ref[...] = x_ref[...] + _dot(mb, wo_ref[...])

    return _rowcall(name, body, [oa, ob, gates, x], [wua, wub, wo], [(d, BF16), (d, F32)])


def _ff1_fwd(x1, g, w1, name, side=None):
    _, d, nq = w1.shape
    dff = 4 * nq

    def body(x_ref, g_ref, w_ref, h_ref, r_ref, a_ref):
        hb = _rms(x_ref[...], g_ref[...]).astype(BF16)
        h_ref[...] = hb
        r = jnp.maximum(_dot_cols(hb, w_ref), 0.0)
        r_ref[...] = r.astype(BF16)
        a_ref[...] = jnp.square(r).astype(BF16)

    return _rowcall(name, body, [x1], [g, w1], [(d, BF16), (dff, BF16), (dff, BF16)], side=side)


def _ff2_fwd(a, x1, w2, name):
    d = x1.shape[1]

    def body(a_ref, x_ref, w_ref, o_ref):
        o_ref[...] = x_ref[...] + _dot(a_ref[...], w_ref[...])

    return _rowcall(name, body, [a, x1], [w2], [(d, F32)])[0]


def _ple_fwd(p, x2, g, wpe, wpg, name):
    d = x2.shape[1]

    def body(p_ref, x_ref, g_ref, wpe_ref, wpg_ref, pb_ref, h_ref, pe_ref, gt_ref, x3_ref):
        pb = p_ref[...].astype(BF16)
        pb_ref[...] = pb
        pe = _dot_cols(pb, wpe_ref)
        x = x_ref[...]
        hb = _rms(x, g_ref[...]).astype(BF16)
        h_ref[...] = hb
        gt = _dot(hb, wpg_ref[...])
        pe_ref[...] = pe
        gt_ref[...] = gt
        x3_ref[...] = x + pe * _sigmoid(gt)

    return _rowcall(name, body, [p, x2], [g, wpe, wpg],
                    [(p.shape[1], BF16), (d, BF16), (d, F32), (d, F32), (d, F32)])


def _pair_stack(t, lane):
    zero = jnp.zeros_like(t)
    return jnp.concatenate([jnp.where(lane < HEAD_DIM, t, zero), jnp.where(lane >= HEAD_DIM, t, zero)], axis=0)


def _sb_rel():
    row = lax.broadcasted_iota(jnp.int32, (2 * TQ, BK), 0)
    row = jnp.where(row >= TQ, row - TQ, row)
    col = lax.broadcasted_iota(jnp.int32, (2 * TQ, BK), 1)
    return col - row


def _split_dot(x, m01, two_pass=True):
    hi = x.astype(BF16)
    if not two_pass:
        return _dot(hi, m01)
    lo = (x - hi.astype(F32)).astype(BF16)
    return _dot(hi, m01) + _dot(lo, m01)


def _sb_scores(qs, k, mask):
    z = _dot_nt(qs, k)
    lb = jnp.minimum(z, 0.0) - jnp.log(1.0 + jnp.exp(-jnp.abs(z)))
    lm = lb - z
    return lb, lm if mask is None else jnp.where(mask, lm, 0.0)


SB_STRAIGHT = 3
SB_WIDE = SB_STRAIGHT * BK
SB_QB = 4
SWA_QB = 4


def _sb_wide_consts():
    j = np.arange(BK)[:, None]
    s = np.arange(BK)[None, :]
    ones = np.ones((BK, BK), np.float32)
    as_bf16 = lambda m: jnp.asarray(np.concatenate([m, ones], axis=1).astype(np.float32), dtype=BF16)
    return as_bf16(j > s), as_bf16(j <= s), as_bf16(j < s)


def _wide_sums(x, m01, suffix, two_pass=True):
    parts = [_split_dot(x[:, b * BK:(b + 1) * BK], m01, two_pass) for b in range(SB_STRAIGHT)]
    order = range(SB_STRAIGHT - 1, -1, -1) if suffix else range(SB_STRAIGHT)
    out = [None] * SB_STRAIGHT
    carry = None
    for b in order:
        out[b] = parts[b][:, :BK] if carry is None else parts[b][:, :BK] + carry
        carry = parts[b][:, BK:] if carry is None else carry + parts[b][:, BK:]
    return jnp.concatenate(out, axis=1), carry


def _side_refs(ex, rest, n_out):
    n_in = len(ex.arrays) if ex else 0
    n_alias = len(ex.aliased) if ex else 0
    ins, rest = rest[:n_in], rest[n_in:]
    outs, rest = rest[:n_out], rest[n_out:]
    return ins, outs, rest[:n_alias], rest[n_alias:]


def _side_specs(ex, n_in, n_out):
    if ex is None:
        return [], [], [], {}, []
    any_spec = pl.BlockSpec(memory_space=pl.ANY)
    return ([any_spec] * len(ex.arrays),
            [jax.ShapeDtypeStruct(ex.arrays[a].shape, ex.arrays[a].dtype) for a in ex.aliased],
            [any_spec] * len(ex.aliased), {n_in + a: n_out + o for o, a in enumerate(ex.aliased)},
            [pltpu.SemaphoreType.DMA(ex.sems), pltpu.SemaphoreType.DMA(ex.sems)])


def _sb_fwd(qkv, name, side=None):
    s = qkv.shape[0]
    nq = s // TQ
    npair = SB_W // LANES
    sufw = _sb_wide_consts()[0]

    def body(q_ref, k_ref, v_ref, sufw_ref, *rest):
        side_in, (o_ref, lt_ref, nb_ref), side_out, scratch = _side_refs(side, rest, 3)
        cf_ref, acc_ref = scratch[:2]
        step_id = pl.program_id(1)
        if side is not None:
            @pl.when((pl.program_id(0) == 0) & (step_id == 0))
            def _():
                side.start(side_in, side_out, *scratch[2:])
        lane = lax.broadcasted_iota(jnp.int32, (TQ, LANES), 1)
        rel = _sb_rel()
        blocks = [step_id * SB_QB + b for b in range(SB_QB)]
        qs = [_pair_stack(q_ref[b * TQ:(b + 1) * TQ, :] * SCALE, lane) for b in range(SB_QB)]

        straight = blocks[0] >= SB_STRAIGHT - 1

        @pl.when(straight)
        def _():
            for b, i in enumerate(blocks):
                w0 = pl.multiple_of((i - (SB_STRAIGHT - 1)) * BK, BK)
                kw = k_ref[pl.ds(w0, SB_WIDE), :]
                lb, lm = _sb_scores(qs[b], kw, None)
                own = rel < 0
                past = SB_WIDE - BK
                lm = jnp.concatenate([lm[:, :past], jnp.where(own, lm[:, past:], 0.0)], axis=1)
                after, total = _wide_sums(lm, sufw_ref[...], True)
                a = jnp.exp(lb + after)
                a = jnp.concatenate([a[:, :past], jnp.where(own, a[:, past:], 0.0)], axis=1)
                acc_ref[b] = _dot(a.astype(BF16), v_ref[pl.ds(w0, SB_WIDE), :])
                cf_ref[b] = total

        @pl.when(jnp.logical_not(straight))
        def _():
            cf_ref[...] = jnp.zeros_like(cf_ref)
            acc_ref[...] = jnp.zeros_like(acc_ref)

        for b, i in enumerate(blocks):
            q0 = i * TQ

            def more(c, i=i):
                return (c[0] <= i) & (c[1] > SB_EXHAUSTED)

            def step(c, b=b, i=i, q0=q0):
                k0 = pl.multiple_of((i - c[0]) * BK, BK)
                k = k_ref[pl.ds(k0, BK), :]
                v = v_ref[pl.ds(k0, BK), :]
                mask = rel < (q0 - k0)
                lb, lm = _sb_scores(qs[b], k, mask)
                cs = _split_dot(lm, sufw_ref[...])
                a = jnp.where(mask, jnp.exp(lb + cs[:, :BK] + cf_ref[b]), 0.0)
                acc_ref[b] += _dot(a.astype(BF16), v)
                cf = cf_ref[b] + cs[:, BK:]
                cf_ref[b] = cf
                return c[0] + 1, jnp.max(cf)

            n_blocks, _ = lax.while_loop(
                more, step, (jnp.where(straight, SB_STRAIGHT, 0).astype(jnp.int32), jnp.max(cf_ref[b])))
            o_ref[b * TQ:(b + 1) * TQ, :] = jnp.where(lane < HEAD_DIM, acc_ref[b, :TQ, :],
                                                     acc_ref[b, TQ:, :]).astype(BF16)
            lt_ref[b] = cf_ref[b]
            nb_ref[b] = jnp.full(nb_ref.shape[1:], n_blocks, F32)
        if side is not None:
            @pl.when((pl.program_id(0) == npair - 1) & (step_id == nq // SB_QB - 1))
            def _():
                side.finish(side_in, side_out, *scratch[2:])

    s_in, s_shape, s_out, s_alias, s_sems = _side_specs(side, 4, 3)
    outs = pl.pallas_call(
        body,
        out_shape=[jax.ShapeDtypeStruct((s, SB_W), BF16), jax.ShapeDtypeStruct((npair, nq, 2 * TQ, BK), F32),
                   jax.ShapeDtypeStruct((npair, nq, 8, LANES), F32)] + s_shape,
        grid=(npair, nq // SB_QB),
        in_specs=[pl.BlockSpec((SB_QB * TQ, LANES), lambda j, i: (i, j)),
                  pl.BlockSpec((s, LANES), lambda j, i: (0, npair + j)),
                  pl.BlockSpec((s, LANES), lambda j, i: (0, 2 * npair + j)),
                  pl.BlockSpec(sufw.shape, lambda j, i: (0, 0))] + s_in,
        out_specs=[pl.BlockSpec((SB_QB * TQ, LANES), lambda j, i: (i, j)),
                   pl.BlockSpec((None, SB_QB, 2 * TQ, BK), lambda j, i: (j, i, 0, 0)),
                   pl.BlockSpec((None, SB_QB, 8, LANES), lambda j, i: (j, i, 0, 0))] + s_out,
        input_output_aliases=s_alias,
        scratch_shapes=[pltpu.VMEM((SB_QB, 2 * TQ, BK), F32), pltpu.VMEM((SB_QB, 2 * TQ, LANES), F32)] + s_sems,
        compiler_params=_params(2), name=name)(qkv, qkv, qkv, sufw, *(side.arrays if side else ()))
    return outs[0], outs[1], outs[2], outs[3:]


def _sb_bwd(qkv, lt, nb, doa, name, side=None):
    s = qkv.shape[0]
    nq = s // TQ
    npair = SB_W // LANES
    _, prew, prexw = _sb_wide_consts()

    def body(q_ref, k_ref, v_ref, lt_ref, nb_ref, do_ref, prew_ref, prexw_ref, *rest):
        side_in, (dq_ref, dk_out, dv_out), side_out, scratch = _side_refs(side, rest, 3)
        cp_ref, ce_ref, dqa_ref, dk_ref, dv_ref = scratch[:5]
        sems = scratch[5:]
        step_id = pl.program_id(1)
        if side is not None:
            @pl.when((pl.program_id(0) == 0) & (step_id == 0))
            def _():
                side.start(side_in, side_out, *sems)
        lane = lax.broadcasted_iota(jnp.int32, (TQ, LANES), 1)
        rel = _sb_rel()
        blocks = [step_id * SB_QB + b for b in range(SB_QB)]
        rows = [slice(b * TQ, (b + 1) * TQ) for b in range(SB_QB)]
        qs = [_pair_stack(q_ref[rows[b], :] * SCALE, lane) for b in range(SB_QB)]
        dos = [_pair_stack(do_ref[rows[b], :], lane) for b in range(SB_QB)]
        n_blocks = [jnp.clip(jnp.max(nb_ref[b]).astype(jnp.int32), 1, i + 1) for b, i in enumerate(blocks)]
        first = [i + 1 - n for i, n in zip(blocks, n_blocks)]

        @pl.when(step_id == 0)
        def _():
            dk_ref[...] = jnp.zeros_like(dk_ref)
            dv_ref[...] = jnp.zeros_like(dv_ref)

        straight = n_blocks[0] == SB_STRAIGHT
        for n in n_blocks[1:]:
            straight = straight & (n == SB_STRAIGHT)

        @pl.when(straight)
        def _():
            for b in range(SB_QB):
                w0 = pl.multiple_of(first[b] * BK, BK)
                kw = k_ref[pl.ds(w0, SB_WIDE), :]
                vw = v_ref[pl.ds(w0, SB_WIDE), :]
                lb, lm = _sb_scores(qs[b], kw, None)
                own = rel < 0
                past = SB_WIDE - BK
                on_past_keys = lambda t: jnp.concatenate([t[:, :past], jnp.where(own, t[:, past:], 0.0)], axis=1)
                lm = on_past_keys(lm)
                upto, _ = _wide_sums(lm, prew_ref[...], False)
                lt = lt_ref[b]
                a = on_past_keys(jnp.exp(lb + (jnp.concatenate([lt] * SB_STRAIGHT, axis=1) - upto)))
                e = a * _dot_nt(dos[b], vw)
                big_e, _ = _wide_sums(e, prexw_ref[...], False, two_pass=False)
                dz = on_past_keys(e - jnp.exp(lb) * (e + big_e)).astype(BF16)
                dk_ref[pl.ds(w0, SB_WIDE), :] += _dot_tn(dz, qs[b])
                dv_ref[pl.ds(w0, SB_WIDE), :] += _dot_tn(a.astype(BF16), dos[b])
                dqa_ref[b] = _dot(dz, kw)

        @pl.when(jnp.logical_not(straight))
        def _():
            cp_ref[...] = jnp.zeros_like(cp_ref)
            ce_ref[...] = jnp.zeros_like(ce_ref)
            dqa_ref[...] = jnp.zeros_like(dqa_ref)
            for b, i in enumerate(blocks):
                q0 = i * TQ

                def step(it, carry, b=b, q0=q0):
                    k0 = pl.multiple_of((first[b] + it) * BK, BK)
                    k = k_ref[pl.ds(k0, BK), :]
                    v = v_ref[pl.ds(k0, BK), :]
                    mask = rel < (q0 - k0)
                    lb, lm = _sb_scores(qs[b], k, mask)
                    cs = _split_dot(lm, prew_ref[...])
                    a = jnp.where(mask, jnp.exp(lb + (lt_ref[b] - (cs[:, :BK] + cp_ref[b]))), 0.0)
                    e = a * _dot_nt(dos[b], v)
                    ce = _split_dot(e, prexw_ref[...], two_pass=False)
                    big_e = ce[:, :BK] + ce_ref[b]
                    dz = jnp.where(mask, e - jnp.exp(lb) * (e + big_e), 0.0).astype(BF16)
                    dk_ref[pl.ds(k0, BK), :] += _dot_tn(dz, qs[b])
                    dv_ref[pl.ds(k0, BK), :] += _dot_tn(a.astype(BF16), dos[b])
                    dqa_ref[b] += _dot(dz, k)
                    cp_ref[b] += cs[:, BK:]
                    ce_ref[b] += ce[:, BK:]
                    return carry

                lax.fori_loop(0, n_blocks[b], step, 0)

        for b in range(SB_QB):
            dq = jnp.where(lane < HEAD_DIM, dqa_ref[b, :TQ, :], dqa_ref[b, TQ:, :])
            dq_ref[rows[b], :] = (dq * SCALE).astype(BF16)

        @pl.when(step_id == nq // SB_QB - 1)
        def _():
            dk_out[...] = dk_ref[...].astype(BF16)
            dv_out[...] = dv_ref[...].astype(BF16)

        if side is not None:
            @pl.when((pl.program_id(0) == npair - 1) & (step_id == nq // SB_QB - 1))
            def _():
                side.finish(side_in, side_out, *sems)

    s_in, s_shape, s_out, s_alias, s_sems = _side_specs(side, 8, 3)
    outs = pl.pallas_call(
        body,
        out_shape=[jax.ShapeDtypeStruct((s, SB_W), BF16)] * 3 + s_shape,
        grid=(npair, nq // SB_QB),
        in_specs=[pl.BlockSpec((SB_QB * TQ, LANES), lambda j, i: (i, j)),
                  pl.BlockSpec((s, LANES), lambda j, i: (0, npair + j)),
                  pl.BlockSpec((s, LANES), lambda j, i: (0, 2 * npair + j)),
                  pl.BlockSpec((None, SB_QB, 2 * TQ, BK), lambda j, i: (j, i, 0, 0)),
                  pl.BlockSpec((None, SB_QB, 8, LANES), lambda j, i: (j, i, 0, 0)),
                  pl.BlockSpec((SB_QB * TQ, LANES), lambda j, i: (i, j)),
                  pl.BlockSpec(prew.shape, lambda j, i: (0, 0)),
                  pl.BlockSpec(prexw.shape, lambda j, i: (0, 0))] + s_in,
        out_specs=[pl.BlockSpec((SB_QB * TQ, LANES), lambda j, i: (i, j)),
                   pl.BlockSpec((s, LANES), lambda j, i: (0, j)),
                   pl.BlockSpec((s, LANES), lambda j, i: (0, j))] + s_out,
        input_output_aliases=s_alias,
        scratch_shapes=[pltpu.VMEM((SB_QB, 2 * TQ, BK), F32), pltpu.VMEM((SB_QB, 2 * TQ, BK), F32),
                        pltpu.VMEM((SB_QB, 2 * TQ, LANES), F32), pltpu.VMEM((s, LANES), F32),
                        pltpu.VMEM((s, LANES), F32)] + s_sems,
        compiler_params=_params(2), name=name)(qkv, qkv, qkv, lt, nb, doa, prew, prexw,
                                               *(side.arrays if side else ()))
    return outs[0], outs[1], outs[2], outs[3:]


def _bucket_table():
    i = np.arange(TQ)[:, None]
    j = np.arange(2 * BK)[None, :]
    dist = np.maximum(TQ + i - j, 0)
    max_exact = N_BUCKETS // 2
    df = np.maximum(dist, 1).astype(np.float32)
    large = max_exact + (np.log(df / np.float32(max_exact)) / np.float32(math.log(MAX_DISTANCE / max_exact))
                         * np.float32(N_BUCKETS - max_exact)).astype(np.int32)
    large = np.minimum(large, N_BUCKETS - 1)
    return np.where(dist < max_exact, dist, large).astype(np.int32)


def _swa_align_in(t, lane, g):
    tf = t.astype(F32)
    tr = pltpu.roll(tf, HEAD_DIM, 1)
    gmask = (lane >= HEAD_DIM) == (g == 1)
    top = jnp.where(gmask, jnp.where(g == 0, tf, tr), 0.0)
    bot = jnp.where(gmask, jnp.where(g == 1, tf, tr), 0.0)
    return jnp.concatenate([top, bot], axis=0).astype(BF16)


def _swa_align_out(t, lane, g):
    top, bot = t[:TQ, :], t[TQ:, :]
    top = jnp.where(g == 0, top, pltpu.roll(top, HEAD_DIM, 1))
    bot = jnp.where(g == 1, bot, pltpu.roll(bot, HEAD_DIM, 1))
    return jnp.where(lane < HEAD_DIM, top, bot)


def _swa_bias(bias_ref, bucket_ref, rb_ref, j):
    dist = TQ + lax.broadcasted_iota(jnp.int32, (TQ, 2 * BK), 0) - lax.broadcasted_iota(jnp.int32, (TQ, 2 * BK), 1)
    window = (dist >= 0) & (dist < WINDOW)
    for hh in range(2):
        def add(b, acc):
            return acc + jnp.where(bucket_ref[...] == b, rb_ref[b, 2 * j + hh], 0.0)
        bias = lax.fori_loop(0, N_BUCKETS, add, jnp.zeros((TQ, 2 * BK), F32))
        bias_ref[hh * TQ:(hh + 1) * TQ, :] = jnp.where(window, bias, NEG)


def _swa_probs(qs, k2, bias, own_block, sink_ref, i, j):
    s = _dot_nt(qs, k2) + bias
    s = jnp.where(own_block | (i > 0), s, NEG)
    row1 = lax.broadcasted_iota(jnp.int32, (2 * TQ, 1), 0)
    sink = jnp.where(row1 < TQ, sink_ref[2 * j], sink_ref[2 * j + 1])
    m = jnp.maximum(jnp.max(s, axis=1, keepdims=True), sink)
    e = jnp.exp(s - m)
    es = jnp.exp(sink - m)
    inv = 1.0 / (jnp.sum(e, axis=1, keepdims=True) + es)
    return e * inv, es * inv


def _swa_kv(ref, i):
    prev = pl.multiple_of(jnp.maximum(i - 1, 0) * BK, BK)
    cur = pl.multiple_of(i * BK, BK)
    return jnp.concatenate([ref[pl.ds(prev, BK), :], ref[pl.ds(cur, BK), :]], axis=0), prev, cur


def _swa_fwd(qkv, sinks, rel_bias, name):
    s = qkv.shape[0]
    nq = s // TQ
    npair = SW_QW // LANES
    qcol = 3 * SB_W // LANES
    bucket = jnp.asarray(_bucket_table())

    def body(q_ref, k_ref, v_ref, bucket_ref, sink_ref, rb_ref, o_ref, bias_ref):
        j = pl.program_id(0)
        step = pl.program_id(1)
        g = j // 2
        lane = lax.broadcasted_iota(jnp.int32, (TQ, LANES), 1)

        @pl.when(step == 0)
        def _():
            _swa_bias(bias_ref, bucket_ref, rb_ref, j)

        own_block = lax.broadcasted_iota(jnp.int32, (2 * TQ, 2 * BK), 1) >= BK
        for b in range(SWA_QB):
            i = step * SWA_QB + b
            rows = slice(b * TQ, (b + 1) * TQ)
            qs = _swa_align_in(q_ref[rows, :] * SCALE, lane, g)
            k2, _, _ = _swa_kv(k_ref, i)
            v2, _, _ = _swa_kv(v_ref, i)
            pr, _ = _swa_probs(qs, k2, bias_ref[...], own_block, sink_ref, i, j)
            o_ref[rows, :] = _swa_align_out(_dot(pr.astype(BF16), v2), lane, g).astype(BF16)

    assert nq % SWA_QB == 0
    return pl.pallas_call(
        body, out_shape=jax.ShapeDtypeStruct((s, SW_QW), BF16), grid=(npair, nq // SWA_QB),
        in_specs=[pl.BlockSpec((SWA_QB * TQ, LANES), lambda j, i: (i, qcol + j)),
                  pl.BlockSpec((s, LANES), lambda j, i: (0, qcol + npair)),
                  pl.BlockSpec((s, LANES), lambda j, i: (0, qcol + npair + 1)),
                  pl.BlockSpec((TQ, 2 * BK), lambda j, i: (0, 0)),
                  pl.BlockSpec(memory_space=pltpu.SMEM),
                  pl.BlockSpec(memory_space=pltpu.SMEM)],
        out_specs=pl.BlockSpec((SWA_QB * TQ, LANES), lambda j, i: (i, j)),
        scratch_shapes=[pltpu.VMEM((2 * TQ, 2 * BK), F32)],
        compiler_params=_params(2), name=name)(qkv, qkv, qkv, bucket, sinks, rel_bias)


def _swa_bwd(qkv, ob, dob, sinks, rel_bias, name):
    s = qkv.shape[0]
    nq = s // TQ
    npair = SW_QW // LANES
    qcol = 3 * SB_W // LANES
    bucket = jnp.asarray(_bucket_table())

    def body(q_ref, k_ref, v_ref, o_ref, do_ref, bucket_ref, sink_ref, rb_ref,
             dq_ref, dkv_ref, dsink_ref, drel_ref, bias_ref, dsacc_ref, dk_ref, dv_ref):
        j = pl.program_id(0)
        step = pl.program_id(1)
        g = j // 2
        lane = lax.broadcasted_iota(jnp.int32, (TQ, LANES), 1)
        row8 = lax.broadcasted_iota(jnp.int32, (SW_HEADS, LANES), 0)
        lane8 = lax.broadcasted_iota(jnp.int32, (SW_HEADS, LANES), 1)

        @pl.when((step == 0) & (j == 0))
        def _():
            dk_ref[...] = jnp.zeros_like(dk_ref)
            dv_ref[...] = jnp.zeros_like(dv_ref)
            dsink_ref[...] = jnp.zeros_like(dsink_ref)
            drel_ref[...] = jnp.zeros_like(drel_ref)

        @pl.when(step == 0)
        def _():
            _swa_bias(bias_ref, bucket_ref, rb_ref, j)
            dsacc_ref[...] = jnp.zeros_like(dsacc_ref)

        own_block = lax.broadcasted_iota(jnp.int32, (2 * TQ, 2 * BK), 1) >= BK
        ds_sum = jnp.zeros(dsacc_ref.shape, F32)
        dsink = jnp.zeros((SW_HEADS, LANES), F32)
        for b in range(SWA_QB):
            i = step * SWA_QB + b
            rows = slice(b * TQ, (b + 1) * TQ)
            qs = _swa_align_in(q_ref[rows, :] * SCALE, lane, g)
            do = do_ref[rows, :]
            dos = _swa_align_in(do, lane, g)
            dof = do.astype(F32) * o_ref[rows, :].astype(F32)
            d0 = jnp.sum(jnp.where(lane < HEAD_DIM, dof, 0.0), axis=1, keepdims=True)
            d1 = jnp.sum(jnp.where(lane >= HEAD_DIM, dof, 0.0), axis=1, keepdims=True)
            delta = jnp.concatenate([d0, d1], axis=0)
            k2, prev, cur = _swa_kv(k_ref, i)
            v2, _, _ = _swa_kv(v_ref, i)
            pr, psink = _swa_probs(qs, k2, bias_ref[...], own_block, sink_ref, i, j)
            ds = pr * (_dot_nt(dos, v2) - delta)
            ds_sum = ds_sum + ds
            sd = psink * delta
            ds0 = -jnp.sum(sd[:TQ, :], axis=0, keepdims=True)
            ds1 = -jnp.sum(sd[TQ:, :], axis=0, keepdims=True)
            dsink = dsink + jnp.where(row8 == 2 * j, ds0, jnp.where(row8 == 2 * j + 1, ds1, 0.0))
            dsb = ds.astype(BF16)
            dq_ref[rows, :] = _swa_align_out(_dot(dsb, k2) * SCALE, lane, g).astype(BF16)
            dk2 = _dot_tn(dsb, qs)
            dv2 = _dot_tn(pr.astype(BF16), dos)
            dk_ref[pl.ds(prev, BK), :] += dk2[:BK, :]
            dk_ref[pl.ds(cur, BK), :] += dk2[BK:, :]
            dv_ref[pl.ds(prev, BK), :] += dv2[:BK, :]
            dv_ref[pl.ds(cur, BK), :] += dv2[BK:, :]
        dsacc_ref[...] += ds_sum
        dsink_ref[...] += dsink

        @pl.when(step == nq // SWA_QB - 1)
        def _():
            for hh in range(2):
                def red(b, acc):
                    val = _sum_all(jnp.where(bucket_ref[...] == b, dsacc_ref[hh * TQ:(hh + 1) * TQ, :], 0.0))
                    return jnp.where((row8 == 2 * j + hh) & (lane8 == b), val, acc)
                drel_ref[...] += lax.fori_loop(0, N_BUCKETS, red, jnp.zeros((SW_HEADS, LANES), F32))

        @pl.when((step == nq // SWA_QB - 1) & (j == npair - 1))
        def _():
            dkv_ref[:, :LANES] = dk_ref[...].astype(BF16)
            dkv_ref[:, LANES:] = dv_ref[...].astype(BF16)

    whole = lambda j, i: (0, 0)
    return pl.pallas_call(
        body,
        out_shape=[jax.ShapeDtypeStruct((s, SW_QW), BF16), jax.ShapeDtypeStruct((s, 2 * LANES), BF16),
                   jax.ShapeDtypeStruct((SW_HEADS, LANES), F32), jax.ShapeDtypeStruct((SW_HEADS, LANES), F32)],
        grid=(npair, nq // SWA_QB),
        in_specs=[pl.BlockSpec((SWA_QB * TQ, LANES), lambda j, i: (i, qcol + j)),
                  pl.BlockSpec((s, LANES), lambda j, i: (0, qcol + npair)),
                  pl.BlockSpec((s, LANES), lambda j, i: (0, qcol + npair + 1)),
                  pl.BlockSpec((SWA_QB * TQ, LANES), lambda j, i: (i, j)),
                  pl.BlockSpec((SWA_QB * TQ, LANES), lambda j, i: (i, j)),
                  pl.BlockSpec((TQ, 2 * BK), whole),
                  pl.BlockSpec(memory_space=pltpu.SMEM),
                  pl.BlockSpec(memory_space=pltpu.SMEM)],
        out_specs=[pl.BlockSpec((SWA_QB * TQ, LANES), lambda j, i: (i, j)),
                   pl.BlockSpec((s, 2 * LANES), whole),
                   pl.BlockSpec((SW_HEADS, LANES), whole), pl.BlockSpec((SW_HEADS, LANES), whole)],
        scratch_shapes=[pltpu.VMEM((2 * TQ, 2 * BK), F32), pltpu.VMEM((2 * TQ, 2 * BK), F32),
                        pltpu.VMEM((s, LANES), F32), pltpu.VMEM((s, LANES), F32)],
        compiler_params=_params(2), name=name)(qkv, qkv, qkv, ob, dob, bucket, sinks, rel_bias)


def _acc_init(i, *refs):
    @pl.when(i == 0)
    def _():
        for r in refs:
            r[...] = jnp.zeros_like(r)


def _loss_bwd(x3, target, g, name):
    d = x3.shape[1]

    def body(x_ref, t_ref, g_ref, dx_ref, dg_ref, loss_ref):
        _acc_init(pl.program_id(0), dg_ref, loss_ref)
        x = x_ref[...]
        gv = g_ref[...]
        diff = _rms(x, gv) - t_ref[...]
        loss_ref[...] += 0.5 * jnp.sum(jnp.mean(jnp.square(diff), axis=-1, keepdims=True), axis=0, keepdims=True)
        dx, dg = _rms_bwd(diff * (1.0 / d), x, gv)
        dx_ref[...] = dx
        dg_ref[...] += dg

    return _rowcall(name, body, [x3, target], [g], [(d, F32)], [((1, d), F32), ((1, LANES), F32)])


def _ple_bwd(dx3, pe, gt, x2, g, wpg, name):
    d = x2.shape[1]

    def body(dx3_ref, pe_ref, gt_ref, x_ref, g_ref, w_ref, dpe_ref, dgt_ref, dx2_ref, dg_ref):
        _acc_init(pl.program_id(0), dg_ref)
        dx3 = dx3_ref[...]
        sg = _sigmoid(gt_ref[...])
        dpe_ref[...] = (dx3 * sg).astype(BF16)
        dgt = (dx3 * pe_ref[...] * sg * (1.0 - sg)).astype(BF16)
        dgt_ref[...] = dgt
        dx, dg = _rms_bwd(_dot_nt(dgt, w_ref[...]), x_ref[...], g_ref[...])
        dx2_ref[...] = dx3 + dx
        dg_ref[...] += dg

    return _rowcall(name, body, [dx3, pe, gt, x2], [g, wpg], [(d, BF16), (d, BF16), (d, F32)], [((1, d), F32)])


def _ff2_bwd(dx2, r, w2, name):
    d = dx2.shape[1]
    dff = r.shape[1]

    def body(dx_ref, r_ref, w_ref, du_ref, dxb_ref):
        dxb = dx_ref[...].astype(BF16)
        dxb_ref[...] = dxb
        du_ref[...] = (_dot_nt(dxb, w_ref[...]) * (2.0 * r_ref[...].astype(F32))).astype(BF16)

    return _rowcall(name, body, [dx2, r], [w2], [(dff, BF16), (d, BF16)])


def _ff1_bwd(du, dx2, x1, g, w1, name):
    d = x1.shape[1]

    def body(du_ref, dx2_ref, x_ref, g_ref, w_ref, dx1_ref, dx1b_ref, dg_ref):
        _acc_init(pl.program_id(0), dg_ref)
        dx, dg = _rms_bwd(_dot_cols_t(du_ref[...], w_ref), x_ref[...], g_ref[...])
        dx1 = dx2_ref[...] + dx
        dx1_ref[...] = dx1
        dx1b_ref[...] = dx1.astype(BF16)
        dg_ref[...] += dg

    return _rowcall(name, body, [du, dx2, x1], [g, w1], [(d, F32), (d, BF16)], [((1, d), F32)])


def _mixer_bwd(dx1b, gates, oa, ob, wo, wua, wub, name, side=None):
    d = dx1b.shape[1]

    def body(dx_ref, gate_ref, oa_ref, ob_ref, wo_ref, wua_ref, wub_ref,
             dya_ref, dyb_ref, dgate_ref, doa_ref, dob_ref):
        dm = _dot_nt(dx_ref[...], wo_ref[...])
        sa = _sigmoid(gate_ref[:, :d])
        sb = _sigmoid(gate_ref[:, d:])
        ya = _dot_cols(oa_ref[...], wua_ref)
        yb = _dot_cols(ob_ref[...], wub_ref)
        dya = (dm * sa).astype(BF16)
        dyb = (dm * sb).astype(BF16)
        dya_ref[...] = dya
        dyb_ref[...] = dyb
        dgate_ref[:, :d] = (dm * ya * sa * (1.0 - sa)).astype(BF16)
        dgate_ref[:, d:] = (dm * yb * sb * (1.0 - sb)).astype(BF16)
        doa_ref[...] = _dot_cols_t(dya, wua_ref).astype(BF16)
        dob_ref[...] = _dot_cols_t(dyb, wub_ref).astype(BF16)

    return _rowcall(name, body, [dx1b, gates, oa, ob], [wo, wua, wub],
                    [(d, BF16), (d, BF16), (2 * d, BF16), (SB_W, BF16), (SW_QW, BF16)], side=side)


def _inproj_bwd(pieces, dx1, x, g, wt, name, side=None):
    d = x.shape[1]
    n = len(pieces)
    offsets = [sum(pc.shape[1] for pc in pieces[:p]) for p in range(n + 1)]

    def body(*refs):
        dx1_ref, x_ref, g_ref, w_ref, dx_ref, dg_ref = refs[n:]
        _acc_init(pl.program_id(0), dg_ref)
        dh = _dot(refs[0][...], w_ref[:offsets[1], :])
        for p in range(1, n):
            dh = dh + _dot(refs[p][...], w_ref[offsets[p]:offsets[p + 1], :])
        dx, dg = _rms_bwd(dh, x_ref[...], g_ref[...])
        dx_ref[...] = dx1_ref[...] + dx
        dg_ref[...] += dg

    return _rowcall(name, body, list(pieces) + [dx1, x], [g, wt], [(d, F32)], [((1, d), F32)], side=side)


def _tile(n, cap):
    assert n % LANES == 0
    return max(t for t in range(LANES, min(n, cap) + 1, LANES) if n % t == 0)


def _mm_tn(a, b, name, nshard=1):
    s, ka = a.shape
    nb = b.shape[1]
    n = nb // nshard
    ta = _tile(ka, 512)
    tb = _tile(n, 1024)
    per = n // tb

    def body(a_ref, b_ref, o_ref):
        o_ref[...] = _dot_tn(a_ref[...].astype(BF16), b_ref[...].astype(BF16))

    return pl.pallas_call(
        body, out_shape=jax.ShapeDtypeStruct((nshard, ka, n), F32), grid=(nb // tb, ka // ta),
        in_specs=[pl.BlockSpec((s, ta), lambda jb, ia: (0, ia)), pl.BlockSpec((s, tb), lambda jb, ia: (0, jb))],
        out_specs=pl.BlockSpec((None, ta, tb), lambda jb, ia: (jb // per, ia, jb % per)),
        compiler_params=_params(2), name=name)(a, b)


def _mm_tn_pieces(pieces, b, name):
    s, nb = b.shape
    ta = 256
    tiles = [pc.shape[1] // ta for pc in pieces]
    assert all(pc.shape[1] % ta == 0 for pc in pieces)
    starts = [sum(tiles[:p]) for p in range(len(pieces))]
    tb = _tile(nb, 1024)

    def body(*refs):
        a_refs, b_ref, o_ref = refs[:-2], refs[-2], refs[-1]
        ia = pl.program_id(1)
        for p in range(len(pieces)):
            @pl.when((ia >= starts[p]) & (ia < starts[p] + tiles[p]))
            def _(p=p):
                o_ref[...] = _dot_tn(a_refs[p][...], b_ref[...])

    def piece_spec(p):
        return pl.BlockSpec((s, ta), lambda jb, ia: (0, jnp.clip(ia - starts[p], 0, tiles[p] - 1)))

    return pl.pallas_call(
        body, out_shape=jax.ShapeDtypeStruct((sum(tiles) * ta, nb), F32), grid=(nb // tb, sum(tiles)),
        in_specs=[piece_spec(p) for p in range(len(pieces))] + [pl.BlockSpec((s, tb), lambda jb, ia: (0, jb))],
        out_specs=pl.BlockSpec((ta, tb), lambda jb, ia: (ia, jb)),
        compiler_params=_params(2), name=name)(*pieces, b)


def _place():
    return lax.axis_index("x"), lax.axis_index("y"), lax.axis_index("c")


def _chip_peer(x, y, k):
    return (x ^ (k >> 1), y ^ (k & 1))


def _row_tile(k, cap=544):
    return max(t for t in range(32, min(k, cap) + 1, 32) if k % t == 0)


def _cast_bf16(w, r, name):
    l, k, n = w.shape
    assert l == 2
    tk = _row_tile(k)

    def body(r_ref, w_ref, o0_ref, o1_ref):
        o0_ref[...] = w_ref[0].astype(BF16)
        o1_ref[...] = w_ref[1].astype(BF16)

    out_spec = pl.BlockSpec((None, tk, n), lambda i, r_ref: (r_ref[0], i, 0))
    return pl.pallas_call(
        body, out_shape=[jax.ShapeDtypeStruct((4, k, n), BF16)] * 2,
        grid_spec=pltpu.PrefetchScalarGridSpec(
            num_scalar_prefetch=1, grid=(k // tk,),
            in_specs=[pl.BlockSpec((l, tk, n), lambda i, r_ref: (0, i, 0))],
            out_specs=[out_spec, out_spec]),
        compiler_params=_params(1), name=name)(r, w)


class _Exchange(NamedTuple):
    arrays: tuple
    aliased: tuple
    sems: tuple
    start: Callable
    finish: Callable


def _all_gather(bufs):
    nt = len(bufs)

    def ici(t, ins, outs, send_sems, recv_sems, x, y, c, r, k):
        return pltpu.make_async_remote_copy(
            src_ref=ins[t].at[r, c], dst_ref=outs[t].at[r, c], send_sem=send_sems.at[t, k - 1],
            recv_sem=recv_sems.at[t, k - 1], device_id=(*_chip_peer(x, y, k), c), device_id_type=MESH)

    def d2d(t, outs, send_sems, recv_sems, x, y, c, r, k, half):
        slab = outs[t].at[r ^ k, half]
        return pltpu.make_async_remote_copy(
            src_ref=slab, dst_ref=slab, send_sem=send_sems.at[t, 2 + k], recv_sem=recv_sems.at[t, 2 + k],
            device_id=(x, y, 1 - c), device_id_type=MESH)

    def start(ins, outs, send_sems, recv_sems):
        x, y, c = _place()
        r = 2 * x + y
        for t in range(nt):
            for k in (1, 2, 3):
                ici(t, ins, outs, send_sems, recv_sems, x, y, c, r, k).start()

    def finish(ins, outs, send_sems, recv_sems):
        x, y, c = _place()
        r = 2 * x + y
        for t in range(nt):
            for k in (1, 2, 3):
                slab = outs[t].at[r ^ k, c]
                pltpu.make_async_remote_copy(
                    src_ref=slab, dst_ref=slab, send_sem=send_sems.at[t, k - 1], recv_sem=recv_sems.at[t, k - 1],
                    device_id=(x, y, 1 - c), device_id_type=MESH).wait_recv()
                d2d(t, outs, send_sems, recv_sems, x, y, c, r, k, c).start()
        for t in range(nt):
            for k in (1, 2, 3):
                d2d(t, outs, send_sems, recv_sems, x, y, c, r, k, 1 - c).wait_recv()
        for t in range(nt):
            for k in (1, 2, 3):
                ici(t, ins, outs, send_sems, recv_sems, x, y, c, r, k).wait_send()
                d2d(t, outs, send_sems, recv_sems, x, y, c, r, k, c).wait_send()

    return _Exchange(tuple(bufs), tuple(range(nt)), (nt, 6), start, finish)


def _run_exchange(name, ex):
    n_in, n_out = len(ex.arrays), len(ex.aliased)

    def body(*refs):
        ins, outs = refs[:n_in], refs[n_in:n_in + n_out]
        send_sems, recv_sems = refs[n_in + n_out:]
        ex.start(ins, outs, send_sems, recv_sems)
        ex.finish(ins, outs, send_sems, recv_sems)

    any_spec = pl.BlockSpec(memory_space=pl.ANY)
    return pl.pallas_call(
        body, out_shape=[jax.ShapeDtypeStruct(ex.arrays[a].shape, ex.arrays[a].dtype) for a in ex.aliased],
        in_specs=[any_spec] * n_in, out_specs=[any_spec] * n_out,
        input_output_aliases={a: o for o, a in enumerate(ex.aliased)},
        scratch_shapes=[pltpu.SemaphoreType.DMA(ex.sems), pltpu.SemaphoreType.DMA(ex.sems)],
        name=name)(*ex.arrays)


def _rs_to_sibling(grads):
    nt = len(grads)
    landing = [lax.empty((4,) + g.shape[2:], F32) for g in grads]

    def copies(ins, outs, send_sems, recv_sems):
        x, y, c = _place()
        return [pltpu.make_async_remote_copy(
            src_ref=ins[t].at[:, 1 - c], dst_ref=outs[t], send_sem=send_sems.at[t], recv_sem=recv_sems.at[t],
            device_id=(x, y, 1 - c), device_id_type=MESH) for t in range(nt)]

    def start(ins, outs, send_sems, recv_sems):
        for cp in copies(ins, outs, send_sems, recv_sems):
            cp.start()

    def finish(ins, outs, send_sems, recv_sems):
        for cp in copies(ins, outs, send_sems, recv_sems):
            cp.wait()

    return _Exchange(tuple(grads) + tuple(landing), tuple(range(nt, 2 * nt)), (nt,), start, finish)


def _add_half(g, recv, cr, name):
    _, _, k2, n = g.shape
    tk = _row_tile(k2)

    def body(cr_ref, g_ref, r_ref, sums_ref, mine_ref):
        val = (g_ref[...] + r_ref[...]).astype(BF16)
        sums_ref[...] = val

        @pl.when(pl.program_id(1) == cr_ref[1])
        def _():
            mine_ref[...] = val

    return pl.pallas_call(
        body, out_shape=[jax.ShapeDtypeStruct((4, k2, n), BF16)] * 2,
        grid_spec=pltpu.PrefetchScalarGridSpec(
            num_scalar_prefetch=1, grid=(k2 // tk, 4),
            in_specs=[pl.BlockSpec((None, None, tk, n), lambda i, q, cr_ref: (q, cr_ref[0], i, 0)),
                      pl.BlockSpec((None, tk, n), lambda i, q, cr_ref: (q, i, 0))],
            out_specs=[pl.BlockSpec((None, tk, n), lambda i, q, cr_ref: (q, i, 0)),
                       pl.BlockSpec((None, tk, n), lambda i, q, cr_ref: (cr_ref[1], i, 0))]),
        compiler_params=_params(2), name=name)(cr, g, recv)


def _rs_to_chips(sums, parts):
    nt = len(sums)

    def copies(ins, outs, send_sems, recv_sems):
        x, y, c = _place()
        r = 2 * x + y
        return [pltpu.make_async_remote_copy(
            src_ref=ins[t].at[r ^ k], dst_ref=outs[t].at[r], send_sem=send_sems.at[t, k - 1],
            recv_sem=recv_sems.at[t, k - 1], device_id=(*_chip_peer(x, y, k), c), device_id_type=MESH)
            for t in range(nt) for k in (1, 2, 3)]

    def start(ins, outs, send_sems, recv_sems):
        for cp in copies(ins, outs, send_sems, recv_sems):
            cp.start()

    def finish(ins, outs, send_sems, recv_sems):
        for cp in copies(ins, outs, send_sems, recv_sems):
            cp.wait()

    return _Exchange(tuple(sums) + tuple(parts), tuple(range(nt, 2 * nt)), (nt, 3), start, finish)


def _sum4(parts, cr, name):
    _, k2, n = parts.shape
    tk = _row_tile(k2)

    def body(cr_ref, p_ref, o_ref):
        p = p_ref[...].astype(F32)
        o_ref[...] = ((p[0] + p[1]) + p[2]) + p[3]

    return pl.pallas_call(
        body, out_shape=jax.ShapeDtypeStruct((2, k2, n), F32),
        grid_spec=pltpu.PrefetchScalarGridSpec(
            num_scalar_prefetch=1, grid=(k2 // tk,),
            in_specs=[pl.BlockSpec((4, tk, n), lambda i, cr_ref: (0, i, 0))],
            out_specs=pl.BlockSpec((None, tk, n), lambda i, cr_ref: (cr_ref[0], i, 0))),
        compiler_params=_params(1), name=name)(cr, parts)


def _exchange_halves(both, name):
    nt = len(both)

    def body(*refs):
        ins, outs = refs[:nt], refs[nt:2 * nt]
        send_sems, recv_sems = refs[2 * nt:]
        x, y, c = _place()
        cps = []
        for t in range(nt):
            cp = pltpu.make_async_remote_copy(
                src_ref=ins[t].at[c], dst_ref=outs[t].at[c], send_sem=send_sems.at[t], recv_sem=recv_sems.at[t],
                device_id=(x, y, 1 - c), device_id_type=MESH)
            cp.start()
            cps.append(cp)
        for cp in cps:
            cp.wait()

    any_spec = pl.BlockSpec(memory_space=pl.ANY)
    return pl.pallas_call(
        body, out_shape=[jax.ShapeDtypeStruct(b.shape, F32) for b in both],
        in_specs=[any_spec] * nt, out_specs=[any_spec] * nt,
        input_output_aliases={t: t for t in range(nt)},
        scratch_shapes=[pltpu.SemaphoreType.DMA((nt,)), pltpu.SemaphoreType.DMA((nt,))],
        name=name)(*both)


def _adamw_math(w, g, m, v):
    m = ADAM_B1 * m + (1.0 - ADAM_B1) * g
    v = ADAM_B2 * v + (1.0 - ADAM_B2) * jnp.square(g)
    m_hat = m / (1.0 - ADAM_B1 ** ADAM_STEP)
    v_hat = v / (1.0 - ADAM_B2 ** ADAM_STEP)
    delta = -ADAM_LR * (m_hat / (jnp.sqrt(v_hat) + ADAM_EPS) + ADAM_WD * w)
    return delta, m, v


def _adamw(w, m, v, g0, g1, name):
    _, k, n = w.shape
    tk = _row_tile(k)
    nk = k // tk

    def body(w_ref, m_ref, v_ref, g0_ref, g1_ref, grad_ref, delta_ref, nm_ref, nv_ref):
        g = jnp.where(pl.program_id(0) == 0, g0_ref[...], g1_ref[...])
        delta, nm, nv = _adamw_math(w_ref[...], g, m_ref[...], v_ref[...])
        grad_ref[...] = g
        delta_ref[...] = delta
        nm_ref[...] = nm
        nv_ref[...] = nv

    lay = pl.BlockSpec((None, tk, n), lambda a, i: (a, i, 0))
    g0_spec = pl.BlockSpec((tk, n), lambda a, i: (jnp.where(a == 0, i, nk - 1), 0))
    g1_spec = pl.BlockSpec((tk, n), lambda a, i: (jnp.where(a == 1, i, 0), 0))
    return pl.pallas_call(
        body, out_shape=[jax.ShapeDtypeStruct(w.shape, F32)] * 4, grid=(2, nk),
        in_specs=[lay, lay, lay, g0_spec, g1_spec], out_specs=[lay] * 4,
        compiler_params=_params(2), name=name)(w, m, v, g0, g1)


def _small_allreduce_adamw(gpart, w, m, v):
    shape = gpart.shape

    def body(g_ref, w_ref, m_ref, v_ref, gsum_ref, delta_ref, nm_ref, nv_ref, recv_ref, send_sems, recv_sems):
        x, y, c = _place()
        me = 4 * x + 2 * y + c
        recv_ref[me] = g_ref[...]
        cps = []
        for k in range(1, 8):
            peer = (x ^ (k >> 2), y ^ ((k >> 1) & 1), c ^ (k & 1))
            cp = pltpu.make_async_remote_copy(
                src_ref=g_ref, dst_ref=recv_ref.at[me], send_sem=send_sems.at[k - 1], recv_sem=recv_sems.at[k - 1],
                device_id=peer, device_id_type=MESH)
            cp.start()
            cps.append(cp)
        for cp in cps:
            cp.wait()
        g = recv_ref[0]
        for dev in range(1, 8):
            g = g + recv_ref[dev]
        delta, nm, nv = _adamw_math(w_ref[...], g, m_ref[...], v_ref[...])
        gsum_ref[...] = g
        delta_ref[...] = delta
        nm_ref[...] = nm
        nv_ref[...] = nv

    vm = pl.BlockSpec(memory_space=pltpu.VMEM)
    return pl.pallas_call(
        body, out_shape=[jax.ShapeDtypeStruct(shape, F32)] * 4, in_specs=[vm] * 4, out_specs=[vm] * 4,
        scratch_shapes=[pltpu.VMEM((8,) + shape, F32), pltpu.SemaphoreType.DMA((7,)), pltpu.SemaphoreType.DMA((7,))],
        name="small_allreduce_adamw")(gpart, w, m, v)


BIG = ("w_in", "w_up_a", "w_up_b", "w_o", "w_ff1", "w_ff2", "w_pe", "w_pg")
COL_SHARDED = ("w_in", "w_up_a", "w_up_b", "w_ff1", "w_pe")
ROW_SHARDED = ("w_o", "w_ff2", "w_pg")
SMALL_ROWS = 16


def _pack_small(g_mix, g_mlp, g_pe, g_final, sinks, rel_bias, loss=None):
    d = g_final.shape[0]
    row = lambda v: jnp.pad(v.reshape(1, -1), ((0, 0), (0, d - v.size)))
    rows = [g_mix, g_mlp, g_pe, g_final.reshape(1, d),
            jnp.zeros((1, d), F32) if loss is None else row(loss), row(sinks), row(rel_bias)]
    out = jnp.concatenate(rows, axis=0)
    return jnp.pad(out, ((0, SMALL_ROWS - out.shape[0]), (0, 0)))


def _unpack_small(a, sinks_shape, rel_shape):
    return (a[0:2], a[2:4], a[4:6], a[6], a[8, :sinks_shape[0] * sinks_shape[1]].reshape(sinks_shape),
            a[9, :rel_shape[0] * rel_shape[1]].reshape(rel_shape))


def kernel(x, p, w_in, w_up_a, w_up_b, w_o, w_ff1, w_ff2, w_pe, w_pg, g_mix, g_mlp, g_pe, g_final, sinks, rel_bias, loss_target, m_w_in, m_w_up_a, m_w_up_b, m_w_o, m_w_ff1, m_w_ff2, m_w_pe, m_w_pg, m_g_mix, m_g_mlp, m_g_pe, m_g_final, m_sinks, m_rel_bias, v_w_in, v_w_up_a, v_w_up_b, v_w_o, v_w_ff1, v_w_ff2, v_w_pe, v_w_pg, v_g_mix, v_g_mlp, v_g_pe, v_g_final, v_sinks, v_rel_bias):
    depth = w_in.shape[0]
    assert depth == 2
    x0 = x[0]
    target = loss_target[0]
    d = x0.shape[1]
    wl = dict(w_in=w_in, w_up_a=w_up_a, w_up_b=w_up_b, w_o=w_o, w_ff1=w_ff1, w_ff2=w_ff2, w_pe=w_pe, w_pg=w_pg)
    ml = dict(w_in=m_w_in, w_up_a=m_w_up_a, w_up_b=m_w_up_b, w_o=m_w_o, w_ff1=m_w_ff1, w_ff2=m_w_ff2, w_pe=m_w_pe, w_pg=m_w_pg)
    vl = dict(w_in=v_w_in, w_up_a=v_w_up_a, w_up_b=v_w_up_b, w_o=v_w_o, w_ff1=v_w_ff1, w_ff2=v_w_ff2, w_pe=v_w_pe, w_pg=v_w_pg)
    c_idx = lax.axis_index("c").astype(jnp.int32)
    r_idx = (2 * lax.axis_index("x") + lax.axis_index("y")).astype(jnp.int32)
    cr = jnp.stack([c_idx, r_idx])

    wl["w_in"], ml["w_in"], vl["w_in"] = (jnp.swapaxes(a, 1, 2) for a in (w_in, m_w_in, v_w_in))

    bufs = {}
    for n in BIG:
        k, nn = wl[n].shape[1:]
        for l, b in enumerate(_cast_bf16(wl[n], r_idx.reshape(1), "cast_" + n)):
            bufs[n, l] = b.reshape(4, 2, k // 2, nn)

    def gather(keys, run):
        for key, b in zip(keys, run(_all_gather([bufs[key] for key in keys]))):
            bufs[key] = b

    def gathered(n, l):
        _, _, k2, nn = bufs[n, l].shape
        if n in ROW_SHARDED or n == "w_in":
            return bufs[n, l].reshape(8 * k2, nn)
        return bufs[n, l].reshape(4, 2 * k2, nn)

    gather([("w_in", 0)], lambda ex: _run_exchange("all_gather_first", ex))

    full = {n: [None] * depth for n in BIG}
    saved = []
    xi = x0
    for i in range(depth):
        st = dict(x0=xi)
        gm = g_mix[i].reshape(1, d)
        full["w_in"][i] = gathered("w_in", i)
        st["h1"], st["qkv"], st["gates"] = _inproj_fwd(xi, gm, full["w_in"][i], f"inproj_fwd_{i}")

        def attend(ex):
            st["oa"], st["lt"], st["nb"], filled = _sb_fwd(st["qkv"], f"sb_fwd_{i}", ex)
            return filled

        gather([(n, i) for n in BIG if n != "w_in"], attend)
        for n in BIG:
            if n != "w_in":
                full[n][i] = gathered(n, i)
        st["ob"] = _swa_fwd(st["qkv"], sinks[i], rel_bias, f"swa_fwd_{i}")
        st["m"], st["x1"] = _mixer_fwd(st["oa"], st["ob"], st["gates"], xi, full["w_up_a"][i], full["w_up_b"][i],
                                       full["w_o"][i], f"mixer_fwd_{i}")
        if i == 0:
            def feed_forward(ex):
                (st["h2"], st["u"], st["a"]), filled = _ff1_fwd(st["x1"], g_mlp[i].reshape(1, d), full["w_ff1"][i],
                                                               f"ff1_fwd_{i}", ex)
                return filled

            gather([("w_in", 1)], feed_forward)
        else:
            st["h2"], st["u"], st["a"] = _ff1_fwd(st["x1"], g_mlp[i].reshape(1, d), full["w_ff1"][i], f"ff1_fwd_{i}")
        st["x2"] = _ff2_fwd(st["a"], st["x1"], full["w_ff2"][i], f"ff2_fwd_{i}")
        st["pb"], st["h3"], st["pe"], st["gt"], xi = _ple_fwd(p[i, 0], st["x2"], g_pe[i].reshape(1, d),
                                                            full["w_pe"][i], full["w_pg"][i], f"ple_fwd_{i}")
        saved.append(st)

    dx, dg_final, loss_part = _loss_bwd(xi, target, g_final.reshape(1, d), "loss_bwd")
    gw = {n: [None] * depth for n in BIG}
    reduced = {}

    chip_sums = {}

    def to_sibling(keys, run):
        tensors = []
        for n, l in keys:
            g = gw[n][l]
            if n in ROW_SHARDED:
                ka, nb = g.shape[1:]
                g = g.reshape(4, ka // 4, nb)
            _, k, nn = g.shape
            tensors.append(g.reshape(4, 2, k // 2, nn))
        for (n, l), g, r in zip(keys, tensors, run(_rs_to_sibling(tensors))):
            chip_sums[n, l] = _add_half(g, r, cr, f"add_half_{n}_{l}")

    def to_chips(keys):
        return _rs_to_chips([chip_sums[k][0] for k in keys], [chip_sums[k][1] for k in keys])

    def scatter_back(keys, parts, tag):
        halves = [_sum4(pc, cr, f"sum4_{n}_{l}") for (n, l), pc in zip(keys, parts)]
        for key, both in zip(keys, _exchange_halves(halves, f"exchange_halves_{tag}")):
            reduced[key] = both

    dg_mix, dg_mlp, dg_pe, dsinks = [None] * depth, [None] * depth, [None] * depth, [None] * depth
    drel = jnp.zeros((SW_HEADS, LANES), F32)
    for i in reversed(range(depth)):
        st = saved[i]
        dpe, dgt, dx2, dg_pe[i] = _ple_bwd(dx, st["pe"], st["gt"], st["x2"], g_pe[i].reshape(1, d),
                                           full["w_pg"][i], f"ple_bwd_{i}")
        gw["w_pe"][i] = _mm_tn(st["pb"], dpe, f"dw_pe_{i}", 4)
        gw["w_pg"][i] = _mm_tn(st["h3"], dgt, f"dw_pg_{i}")
        du, dx2b = _ff2_bwd(dx2, st["u"], full["w_ff2"][i], f"ff2_bwd_{i}")
        gw["w_ff2"][i] = _mm_tn(st["a"], dx2b, f"dw_ff2_{i}")
        gw["w_ff1"][i] = _mm_tn(st["h2"], du, f"dw_ff1_{i}", 4)
        dx1, dx1b, dg_mlp[i] = _ff1_bwd(du, dx2, st["x1"], g_mlp[i].reshape(1, d), full["w_ff1"][i], f"ff1_bwd_{i}")
        gw["w_o"][i] = _mm_tn(st["m"], dx1b, f"dw_o_{i}")
        early = [(n, i) for n in ("w_pe", "w_pg", "w_ff2", "w_ff1", "w_o")]

        def mixer(ex):
            (dya, dyb, dgates, doa, dob), landed = _mixer_bwd(
                dx1b, st["gates"], st["oa"], st["ob"], full["w_o"][i], full["w_up_a"][i], full["w_up_b"][i],
                f"mixer_bwd_{i}", ex)
            st.update(dya=dya, dyb=dyb, dgates=dgates, doa=doa, dob=dob)
            return landed

        to_sibling(early, mixer)
        dgates = st["dgates"]
        gw["w_up_a"][i] = _mm_tn(st["oa"], st["dya"], f"dw_up_a_{i}", 4)
        gw["w_up_b"][i] = _mm_tn(st["ob"], st["dyb"], f"dw_up_b_{i}", 4)
        late = [("w_up_a", i), ("w_up_b", i)]
        to_sibling(late, lambda ex: _run_exchange(f"rs_to_sibling_{i}", ex))
        keys = early + late + ([("w_in", 1)] if i == 0 else [])
        dqa, dka, dva, parts = _sb_bwd(st["qkv"], st["lt"], st["nb"], st["doa"], f"sb_bwd_{i}", to_chips(keys))
        scatter_back(keys, parts, i)
        dob = st["dob"]
        dqb, dkvb, dsk, drl = _swa_bwd(st["qkv"], st["ob"], dob, sinks[i], rel_bias, f"swa_bwd_{i}")
        dsinks[i] = dsk[:, 0]
        drel = drel + drl
        dproj = [dqa, dka, dva, dqb, dkvb, dgates]
        dw_in_t = _mm_tn_pieces(dproj, st["h1"], f"dw_in_{i}")
        gw["w_in"][i] = dw_in_t.reshape(4, dw_in_t.shape[0] // 4, d)
        if i == 1:
            def inproj(ex):
                (dx, dg_mix[i]), landed = _inproj_bwd(dproj, dx1, st["x0"], g_mix[i].reshape(1, d),
                                                      full["w_in"][i], f"inproj_bwd_{i}", ex)
                st["dx"] = dx
                return landed

            to_sibling([("w_in", 1)], inproj)
            dx = st["dx"]
        else:
            keys = [("w_in", 0)]
            to_sibling(keys, lambda ex: _run_exchange("rs_to_sibling_last", ex))
            (dx, dg_mix[i]), parts = _inproj_bwd(dproj, dx1, st["x0"], g_mix[i].reshape(1, d),
                                                 full["w_in"][i], f"inproj_bwd_{i}", to_chips(keys))
            scatter_back(keys, parts, "last")
    grad_x = dx[None]

    outs = {}
    for n in BIG:
        g0, g1 = (reduced[n, l].reshape(wl[n].shape[1:]) for l in range(depth))
        outs[n] = _adamw(wl[n], ml[n], vl[n], g0, g1, "adamw_" + n)
    outs["w_in"] = [jnp.swapaxes(a, 1, 2) for a in outs["w_in"]]

    drel_bias = drel[:, :N_BUCKETS].T
    gsmall = _pack_small(jnp.concatenate(dg_mix, 0), jnp.concatenate(dg_mlp, 0), jnp.concatenate(dg_pe, 0),
                         dg_final[0], jnp.stack(dsinks), drel_bias, loss_part[0, :1])
    wsmall = _pack_small(g_mix, g_mlp, g_pe, g_final, sinks, rel_bias)
    msmall = _pack_small(m_g_mix, m_g_mlp, m_g_pe, m_g_final, m_sinks, m_rel_bias)
    vsmall = _pack_small(v_g_mix, v_g_mlp, v_g_pe, v_g_final, v_sinks, v_rel_bias)
    small = _small_allreduce_adamw(gsmall, wsmall, msmall, vsmall)
    loss = small[0][7, 0]
    small = [_unpack_small(a, sinks.shape, rel_bias.shape) for a in small]

    result = [loss, grad_x]
    for kind in range(4):
        result += [outs[n][kind] for n in BIG]
        result += list(small[kind])
    return tuple(result)
```

```python
import functools
import math
from typing import Callable, NamedTuple

import numpy as np
import jax
import jax.numpy as jnp
from jax import lax
from jax.experimental import pallas as pl
from jax.experimental.pallas import tpu as pltpu

F32 = jnp.float32
BF16 = jnp.bfloat16
MESH = pl.DeviceIdType.MESH

HEAD_DIM = 64
SB_HEADS = 8
SW_HEADS = 8
SW_KV_HEADS = 2
WINDOW = 128
N_BUCKETS = 32
MAX_DISTANCE = 128
EPS = 1e-6
SB_W = SB_HEADS * HEAD_DIM
SW_QW = SW_HEADS * HEAD_DIM
SW_KVW = SW_KV_HEADS * HEAD_DIM
QKV_W = 3 * SB_W + SW_QW + 2 * SW_KVW
SCALE = HEAD_DIM ** -0.5
assert SCALE == 0.125
LANES = 128
TQ = 128
BK = 128
NEG = -1e30
SB_EXHAUSTED = -106.0

ADAM_LR = 0.001
ADAM_B1 = 0.9
ADAM_B2 = 0.999
ADAM_EPS = 1e-08
ADAM_WD = 0.01
ADAM_STEP = 10

VMEM_LIMIT = 56 * 1024 * 1024


def _dot(a, b):
    return jnp.dot(a, b, preferred_element_type=F32)


def _dot_nt(a, b):
    return lax.dot_general(a, b, (((1,), (1,)), ((), ())), preferred_element_type=F32)


def _dot_tn(a, b):
    return lax.dot_general(a, b, (((0,), (0,)), ((), ())), preferred_element_type=F32)


def _sum_all(x):
    return jnp.sum(jnp.sum(x, axis=1, keepdims=True), axis=0, keepdims=True)


def _sigmoid(x):
    return 1.0 / (1.0 + jnp.exp(-x))


def _rms(x, g):
    r = lax.rsqrt(jnp.mean(x * x, axis=-1, keepdims=True) + EPS)
    return (x * r) * g


def _rms_bwd(dy, x, g):
    r = lax.rsqrt(jnp.mean(x * x, axis=-1, keepdims=True) + EPS)
    n = x * r
    dg = jnp.sum(dy * n, axis=0, keepdims=True)
    dn = dy * g
    dx = r * (dn - n * jnp.mean(dn * n, axis=-1, keepdims=True))
    return dx, dg


def _params(n_axes):
    return pltpu.CompilerParams(dimension_semantics=("arbitrary",) * n_axes, vmem_limit_bytes=VMEM_LIMIT)


def _rowcall(name, body, row_ins, const_ins, row_outs, acc_outs=(), tm=512, side=None):
    s = row_ins[0].shape[0]
    assert s % tm == 0
    in_specs = [pl.BlockSpec((tm, a.shape[1]), lambda i: (i, 0)) for a in row_ins]
    in_specs += [pl.BlockSpec(a.shape, functools.partial(lambda i, nd: (0,) * nd, nd=a.ndim)) for a in const_ins]
    out_shape = [jax.ShapeDtypeStruct((s, c), dt) for c, dt in row_outs]
    out_specs = [pl.BlockSpec((tm, c), lambda i: (i, 0)) for c, _ in row_outs]
    out_shape += [jax.ShapeDtypeStruct(sh, dt) for sh, dt in acc_outs]
    out_specs += [pl.BlockSpec(sh, functools.partial(lambda i, nd: (0,) * nd, nd=len(sh))) for sh, _ in acc_outs]
    if side is None:
        return pl.pallas_call(body, out_shape=out_shape, grid=(s // tm,), in_specs=in_specs, out_specs=out_specs,
                              compiler_params=_params(1), name=name)(*row_ins, *const_ins)
    n_in, n_out = len(in_specs), len(out_specs)

    def with_side(*refs):
        side_in, outs, side_out, sems = _side_refs(side, refs[n_in:], n_out)

        @pl.when(pl.program_id(0) == 0)
        def _():
            side.start(side_in, side_out, *sems)

        body(*refs[:n_in], *outs)

        @pl.when(pl.program_id(0) == s // tm - 1)
        def _():
            side.finish(side_in, side_out, *sems)

    s_in, s_shape, s_out, s_alias, s_sems = _side_specs(side, n_in, n_out)
    outs = pl.pallas_call(with_side, out_shape=out_shape + s_shape, grid=(s // tm,), in_specs=in_specs + s_in,
                          out_specs=out_specs + s_out, input_output_aliases=s_alias, scratch_shapes=s_sems,
                          compiler_params=_params(1), name=name)(*row_ins, *const_ins, *side.arrays)
    return outs[:n_out], outs[n_out:]


def _dot_cols(a, w_ref):
    return jnp.concatenate([_dot(a, w_ref[r]) for r in range(w_ref.shape[0])], axis=1)


def _dot_cols_t(a, w_ref):
    n = w_ref.shape[2]
    out = _dot_nt(a[:, :n], w_ref[0])
    for r in range(1, w_ref.shape[0]):
        out = out + _dot_nt(a[:, r * n:(r + 1) * n], w_ref[r])
    return out


def _inproj_fwd(x, g, wt, name):
    d = x.shape[1]

    def body(x_ref, g_ref, w_ref, h_ref, qkv_ref, gate_ref):
        hb = _rms(x_ref[...], g_ref[...]).astype(BF16)
        h_ref[...] = hb
        qkv_ref[...] = _dot_nt(hb, w_ref[:QKV_W, :]).astype(BF16)
        gate_ref[...] = _dot_nt(hb, w_ref[QKV_W:, :])

    return _rowcall(name, body, [x], [g, wt], [(d, BF16), (QKV_W, BF16), (2 * d, F32)])


def _mixer_fwd(oa, ob, gates, x, wua, wub, wo, name):
    d = x.shape[1]

    def body(oa_ref, ob_ref, gate_ref, x_ref, wua_ref, wub_ref, wo_ref, m_ref, x1_ref):
        ya = _dot_cols(oa_ref[...], wua_ref)
        yb = _dot_cols(ob_ref[...], wub_ref)
        m = _sigmoid(gate_ref[:, :d]) * ya + _sigmoid(gate_ref[:, d:]) * yb
        mb = m.astype(BF16)
        m_ref[...] = mb
        x1_ref[...] = x_ref[...] + _dot(mb, wo_ref[...])

    return _rowcall(name, body, [oa, ob, gates, x], [wua, wub, wo], [(d, BF16), (d, F32)])


def _ff1_fwd(x1, g, w1, name, side=None):
    _, d, nq = w1.shape
    dff = 4 * nq

    def body(x_ref, g_ref, w_ref, h_ref, r_ref, a_ref):
        hb = _rms(x_ref[...], g_ref[...]).astype(BF16)
        h_ref[...] = hb
        r = jnp.maximum(_dot_cols(hb, w_ref), 0.0)
        r_ref[...] = r.astype(BF16)
        a_ref[...] = jnp.square(r).astype(BF16)

    return _rowcall(name, body, [x1], [g, w1], [(d, BF16), (dff, BF16), (dff, BF16)], side=side)


def _ff2_fwd(a, x1, w2, name):
    d = x1.shape[1]

    def body(a_ref, x_ref, w_ref, o_ref):
        o_ref[...] = x_ref[...] + _dot(a_ref[...], w_ref[...])

    return _rowcall(name, body, [a, x1], [w2], [(d, F32)])[0]


def _ple_fwd(p, x2, g, wpe, wpg, name):
    d = x2.shape[1]

    def body(p_ref, x_ref, g_ref, wpe_ref, wpg_ref, pb_ref, h_ref, pe_ref, gt_ref, x3_ref):
        pb = p_ref[...].astype(BF16)
        pb_ref[...] = pb
        pe = _dot_cols(pb, wpe_ref)
        x = x_ref[...]
        hb = _rms(x, g_ref[...]).astype(BF16)
        h_ref[...] = hb
        gt = _dot(hb, wpg_ref[...])
        pe_ref[...] = pe
        gt_ref[...] = gt
        x3_ref[...] = x + pe * _sigmoid(gt)

    return _rowcall(name, body, [p, x2], [g, wpe, wpg],
                    [(p.shape[1], BF16), (d, BF16), (d, F32), (d, F32), (d, F32)])


def _pair_stack(t, lane):
    zero = jnp.zeros_like(t)
    return jnp.concatenate([jnp.where(lane < HEAD_DIM, t, zero), jnp.where(lane >= HEAD_DIM, t, zero)], axis=0)


def _sb_rel():
    row = lax.broadcasted_iota(jnp.int32, (2 * TQ, BK), 0)
    row = jnp.where(row >= TQ, row - TQ, row)
    col = lax.broadcasted_iota(jnp.int32, (2 * TQ, BK), 1)
    return col - row


def _split_dot(x, m01, two_pass=True):
    hi = x.astype(BF16)
    if not two_pass:
        return _dot(hi, m01)
    lo = (x - hi.astype(F32)).astype(BF16)
    return _dot(hi, m01) + _dot(lo, m01)


def _sb_scores(qs, k, mask):
    z = _dot_nt(qs, k)
    lb = jnp.minimum(z, 0.0) - jnp.log(1.0 + jnp.exp(-jnp.abs(z)))
    lm = lb - z
    return lb, lm if mask is None else jnp.where(mask, lm, 0.0)


SB_STRAIGHT = 3
SB_WIDE = SB_STRAIGHT * BK
SB_QB = 2
SWA_QB = 4


def _sb_wide_consts():
    j = np.arange(BK)[:, None]
    s = np.arange(BK)[None, :]
    ones = np.ones((BK, BK), np.float32)
    as_bf16 = lambda m: jnp.asarray(np.concatenate([m, ones], axis=1).astype(np.float32), dtype=BF16)
    return as_bf16(j > s), as_bf16(j <= s), as_bf16(j < s)


def _wide_sums(x, m01, suffix, two_pass=True):
    parts = [_split_dot(x[:, b * BK:(b + 1) * BK], m01, two_pass) for b in range(SB_STRAIGHT)]
    order = range(SB_STRAIGHT - 1, -1, -1) if suffix else range(SB_STRAIGHT)
    out = [None] * SB_STRAIGHT
    carry = None
    for b in order:
        out[b] = parts[b][:, :BK] if carry is None else parts[b][:, :BK] + carry
        carry = parts[b][:, BK:] if carry is None else carry + parts[b][:, BK:]
    return jnp.concatenate(out, axis=1), carry


def _side_refs(ex, rest, n_out):
    n_in = len(ex.arrays) if ex else 0
    n_alias = len(ex.aliased) if ex else 0
    ins, rest = rest[:n_in], rest[n_in:]
    outs, rest = rest[:n_out], rest[n_out:]
    return ins, outs, rest[:n_alias], rest[n_alias:]


def _side_specs(ex, n_in, n_out):
    if ex is None:
        return [], [], [], {}, []
    any_spec = pl.BlockSpec(memory_space=pl.ANY)
    return ([any_spec] * len(ex.arrays),
            [jax.ShapeDtypeStruct(ex.arrays[a].shape, ex.arrays[a].dtype) for a in ex.aliased],
            [any_spec] * len(ex.aliased), {n_in + a: n_out + o for o, a in enumerate(ex.aliased)},
            [pltpu.SemaphoreType.DMA(ex.sems), pltpu.SemaphoreType.DMA(ex.sems)])


def _sb_fwd(qkv, name, side=None):
    s = qkv.shape[0]
    nq = s // TQ
    npair = SB_W // LANES
    sufw = _sb_wide_consts()[0]

    def body(q_ref, k_ref, v_ref, sufw_ref, *rest):
        side_in, (o_ref, lt_ref, nb_ref), side_out, scratch = _side_refs(side, rest, 3)
        cf_ref, acc_ref = scratch[:2]
        step_id = pl.program_id(1)
        if side is not None:
            @pl.when((pl.program_id(0) == 0) & (step_id == 0))
            def _():
                side.start(side_in, side_out, *scratch[2:])
        lane = lax.broadcasted_iota(jnp.int32, (TQ, LANES), 1)
        rel = _sb_rel()
        blocks = [step_id * SB_QB + b for b in range(SB_QB)]
        qs = [_pair_stack(q_ref[b * TQ:(b + 1) * TQ, :] * SCALE, lane) for b in range(SB_QB)]

        straight = blocks[0] >= SB_STRAIGHT - 1

        @pl.when(straight)
        def _():
            for b, i in enumerate(blocks):
                w0 = pl.multiple_of((i - (SB_STRAIGHT - 1)) * BK, BK)
                kw = k_ref[pl.ds(w0, SB_WIDE), :]
                lb, lm = _sb_scores(qs[b], kw, None)
                own = rel < 0
                past = SB_WIDE - BK
                lm = jnp.concatenate([lm[:, :past], jnp.where(own, lm[:, past:], 0.0)], axis=1)
                after, total = _wide_sums(lm, sufw_ref[...], True)
                a = jnp.exp(lb + after)
                a = jnp.concatenate([a[:, :past], jnp.where(own, a[:, past:], 0.0)], axis=1)
                acc_ref[b] = _dot(a.astype(BF16), v_ref[pl.ds(w0, SB_WIDE), :])
                cf_ref[b] = total

        @pl.when(jnp.logical_not(straight))
        def _():
            cf_ref[...] = jnp.zeros_like(cf_ref)
            acc_ref[...] = jnp.zeros_like(acc_ref)

        for b, i in enumerate(blocks):
            q0 = i * TQ

            def more(c, i=i):
                return (c[0] <= i) & (c[1] > SB_EXHAUSTED)

            def step(c, b=b, i=i, q0=q0):
                k0 = pl.multiple_of((i - c[0]) * BK, BK)
                k = k_ref[pl.ds(k0, BK), :]
                v = v_ref[pl.ds(k0, BK), :]
                mask = rel < (q0 - k0)
                lb, lm = _sb_scores(qs[b], k, mask)
                cs = _split_dot(lm, sufw_ref[...])
                a = jnp.where(mask, jnp.exp(lb + cs[:, :BK] + cf_ref[b]), 0.0)
                acc_ref[b] += _dot(a.astype(BF16), v)
                cf = cf_ref[b] + cs[:, BK:]
                cf_ref[b] = cf
                return c[0] + 1, jnp.max(cf)

            n_blocks, _ = lax.while_loop(
                more, step, (jnp.where(straight, SB_STRAIGHT, 0).astype(jnp.int32), jnp.max(cf_ref[b])))
            o_ref[b * TQ:(b + 1) * TQ, :] = jnp.where(lane < HEAD_DIM, acc_ref[b, :TQ, :],
                                                     acc_ref[b, TQ:, :]).astype(BF16)
            lt_ref[b] = cf_ref[b]
            nb_ref[b] = jnp.full(nb_ref.shape[1:], n_blocks, F32)
        if side is not None:
            @pl.when((pl.program_id(0) == npair - 1) & (step_id == nq // SB_QB - 1))
            def _():
                side.finish(side_in, side_out, *scratch[2:])

    s_in, s_shape, s_out, s_alias, s_sems = _side_specs(side, 4, 3)
    outs = pl.pallas_call(
        body,
        out_shape=[jax.ShapeDtypeStruct((s, SB_W), BF16), jax.ShapeDtypeStruct((npair, nq, 2 * TQ, BK), F32),
                   jax.ShapeDtypeStruct((npair, nq, 8, LANES), F32)] + s_shape,
        grid=(npair, nq // SB_QB),
        in_specs=[pl.BlockSpec((SB_QB * TQ, LANES), lambda j, i: (i, j)),
                  pl.BlockSpec((s, LANES), lambda j, i: (0, npair + j)),
                  pl.BlockSpec((s, LANES), lambda j, i: (0, 2 * npair + j)),
                  pl.BlockSpec(sufw.shape, lambda j, i: (0, 0))] + s_in,
        out_specs=[pl.BlockSpec((SB_QB * TQ, LANES), lambda j, i: (i, j)),
                   pl.BlockSpec((None, SB_QB, 2 * TQ, BK), lambda j, i: (j, i, 0, 0)),
                   pl.BlockSpec((None, SB_QB, 8, LANES), lambda j, i: (j, i, 0, 0))] + s_out,
        input_output_aliases=s_alias,
        scratch_shapes=[pltpu.VMEM((SB_QB, 2 * TQ, BK), F32), pltpu.VMEM((SB_QB, 2 * TQ, LANES), F32)] + s_sems,
        compiler_params=_params(2), name=name)(qkv, qkv, qkv, sufw, *(side.arrays if side else ()))
    return outs[0], outs[1], outs[2], outs[3:]


def _sb_bwd(qkv, lt, nb, doa, name, side=None):
    s = qkv.shape[0]
    nq = s // TQ
    npair = SB_W // LANES
    _, prew, prexw = _sb_wide_consts()

    def body(q_ref, k_ref, v_ref, lt_ref, nb_ref, do_ref, prew_ref, prexw_ref, *rest):
        side_in, (dq_ref, dk_out, dv_out), side_out, scratch = _side_refs(side, rest, 3)
        cp_ref, ce_ref, dqa_ref, dk_ref, dv_ref = scratch[:5]
        sems = scratch[5:]
        step_id = pl.program_id(1)
        if side is not None:
            @pl.when((pl.program_id(0) == 0) & (step_id == 0))
            def _():
                side.start(side_in, side_out, *sems)
        lane = lax.broadcasted_iota(jnp.int32, (TQ, LANES), 1)
        rel = _sb_rel()
        blocks = [step_id * SB_QB + b for b in range(SB_QB)]
        rows = [slice(b * TQ, (b + 1) * TQ) for b in range(SB_QB)]
        qs = [_pair_stack(q_ref[rows[b], :] * SCALE, lane) for b in range(SB_QB)]
        dos = [_pair_stack(do_ref[rows[b], :], lane) for b in range(SB_QB)]
        n_blocks = [jnp.clip(jnp.max(nb_ref[b]).astype(jnp.int32), 1, i + 1) for b, i in enumerate(blocks)]
        first = [i + 1 - n for i, n in zip(blocks, n_blocks)]

        @pl.when(step_id == 0)
        def _():
            dk_ref[...] = jnp.zeros_like(dk_ref)
            dv_ref[...] = jnp.zeros_like(dv_ref)

        straight = n_blocks[0] == SB_STRAIGHT
        for n in n_blocks[1:]:
            straight = straight & (n == SB_STRAIGHT)

        @pl.when(straight)
        def _():
            for b in range(SB_QB):
                w0 = pl.multiple_of(first[b] * BK, BK)
                kw = k_ref[pl.ds(w0, SB_WIDE), :]
                vw = v_ref[pl.ds(w0, SB_WIDE), :]
                lb, lm = _sb_scores(qs[b], kw, None)
                own = rel < 0
                past = SB_WIDE - BK
                on_past_keys = lambda t: jnp.concatenate([t[:, :past], jnp.where(own, t[:, past:], 0.0)], axis=1)
                lm = on_past_keys(lm)
                upto, _ = _wide_sums(lm, prew_ref[...], False)
                lt = lt_ref[b]
                a = on_past_keys(jnp.exp(lb + (jnp.concatenate([lt] * SB_STRAIGHT, axis=1) - upto)))
                e = a * _dot_nt(dos[b], vw)
                big_e, _ = _wide_sums(e, prexw_ref[...], False, two_pass=False)
                dz = on_past_keys(e - jnp.exp(lb) * (e + big_e)).astype(BF16)
                dk_ref[pl.ds(w0, SB_WIDE), :] += _dot_tn(dz, qs[b])
                dv_ref[pl.ds(w0, SB_WIDE), :] += _dot_tn(a.astype(BF16), dos[b])
                dqa_ref[b] = _dot(dz, kw)

        @pl.when(jnp.logical_not(straight))
        def _():
            cp_ref[...] = jnp.zeros_like(cp_ref)
            ce_ref[...] = jnp.zeros_like(ce_ref)
            dqa_ref[...] = jnp.zeros_like(dqa_ref)
            for b, i in enumerate(blocks):
                q0 = i * TQ

                def step(it, carry, b=b, q0=q0):
                    k0 = pl.multiple_of((first[b] + it) * BK, BK)
                    k = k_ref[pl.ds(k0, BK), :]
                    v = v_ref[pl.ds(k0, BK), :]
                    mask = rel < (q0 - k0)
                    lb, lm = _sb_scores(qs[b], k, mask)
                    cs = _split_dot(lm, prew_ref[...])
                    a = jnp.where(mask, jnp.exp(lb + (lt_ref[b] - (cs[:, :BK] + cp_ref[b]))), 0.0)
                    e = a * _dot_nt(dos[b], v)
                    ce = _split_dot(e, prexw_ref[...], two_pass=False)
                    big_e = ce[:, :BK] + ce_ref[b]
                    dz = jnp.where(mask, e - jnp.exp(lb) * (e + big_e), 0.0).astype(BF16)
                    dk_ref[pl.ds(k0, BK), :] += _dot_tn(dz, qs[b])
                    dv_ref[pl.ds(k0, BK), :] += _dot_tn(a.astype(BF16), dos[b])
                    dqa_ref[b] += _dot(dz, k)
                    cp_ref[b] += cs[:, BK:]
                    ce_ref[b] += ce[:, BK:]
                    return carry

                lax.fori_loop(0, n_blocks[b], step, 0)

        for b in range(SB_QB):
            dq = jnp.where(lane < HEAD_DIM, dqa_ref[b, :TQ, :], dqa_ref[b, TQ:, :])
            dq_ref[rows[b], :] = (dq * SCALE).astype(BF16)

        @pl.when(step_id == nq // SB_QB - 1)
        def _():
            dk_out[...] = dk_ref[...].astype(BF16)
            dv_out[...] = dv_ref[...].astype(BF16)

        if side is not None:
            @pl.when((pl.program_id(0) == npair - 1) & (step_id == nq // SB_QB - 1))
            def _():
                side.finish(side_in, side_out, *sems)

    s_in, s_shape, s_out, s_alias, s_sems = _side_specs(side, 8, 3)
    outs = pl.pallas_call(
        body,
        out_shape=[jax.ShapeDtypeStruct((s, SB_W), BF16)] * 3 + s_shape,
        grid=(npair, nq // SB_QB),
        in_specs=[pl.BlockSpec((SB_QB * TQ, LANES), lambda j, i: (i, j)),
                  pl.BlockSpec((s, LANES), lambda j, i: (0, npair + j)),
                  pl.BlockSpec((s, LANES), lambda j, i: (0, 2 * npair + j)),
                  pl.BlockSpec((None, SB_QB, 2 * TQ, BK), lambda j, i: (j, i, 0, 0)),
                  pl.BlockSpec((None, SB_QB, 8, LANES), lambda j, i: (j, i, 0, 0)),
                  pl.BlockSpec((SB_QB * TQ, LANES), lambda j, i: (i, j)),
                  pl.BlockSpec(prew.shape, lambda j, i: (0, 0)),
                  pl.BlockSpec(prexw.shape, lambda j, i: (0, 0))] + s_in,
        out_specs=[pl.BlockSpec((SB_QB * TQ, LANES), lambda j, i: (i, j)),
                   pl.BlockSpec((s, LANES), lambda j, i: (0, j)),
                   pl.BlockSpec((s, LANES), lambda j, i: (0, j))] + s_out,
        input_output_aliases=s_alias,
        scratch_shapes=[pltpu.VMEM((SB_QB, 2 * TQ, BK), F32), pltpu.VMEM((SB_QB, 2 * TQ, BK), F32),
                        pltpu.VMEM((SB_QB, 2 * TQ, LANES), F32), pltpu.VMEM((s, LANES), F32),
                        pltpu.VMEM((s, LANES), F32)] + s_sems,
        compiler_params=_params(2), name=name)(qkv, qkv, qkv, lt, nb, doa, prew, prexw,
                                               *(side.arrays if side else ()))
    return outs[0], outs[1], outs[2], outs[3:]


def _bucket_table():
    i = np.arange(TQ)[:, None]
    j = np.arange(2 * BK)[None, :]
    dist = np.maximum(TQ + i - j, 0)
    max_exact = N_BUCKETS // 2
    df = np.maximum(dist, 1).astype(np.float32)
    large = max_exact + (np.log(df / np.float32(max_exact)) / np.float32(math.log(MAX_DISTANCE / max_exact))
                         * np.float32(N_BUCKETS - max_exact)).astype(np.int32)
    large = np.minimum(large, N_BUCKETS - 1)
    return np.where(dist < max_exact, dist, large).astype(np.int32)


def _swa_align_in(t, lane, g):
    tf = t.astype(F32)
    tr = pltpu.roll(tf, HEAD_DIM, 1)
    gmask = (lane >= HEAD_DIM) == (g == 1)
    top = jnp.where(gmask, jnp.where(g == 0, tf, tr), 0.0)
    bot = jnp.where(gmask, jnp.where(g == 1, tf, tr), 0.0)
    return jnp.concatenate([top, bot], axis=0).astype(BF16)


def _swa_align_out(t, lane, g):
    top, bot = t[:TQ, :], t[TQ:, :]
    top = jnp.where(g == 0, top, pltpu.roll(top, HEAD_DIM, 1))
    bot = jnp.where(g == 1, bot, pltpu.roll(bot, HEAD_DIM, 1))
    return jnp.where(lane < HEAD_DIM, top, bot)


def _swa_bias(bias_ref, bucket_ref, rb_ref, j):
    dist = TQ + lax.broadcasted_iota(jnp.int32, (TQ, 2 * BK), 0) - lax.broadcasted_iota(jnp.int32, (TQ, 2 * BK), 1)
    window = (dist >= 0) & (dist < WINDOW)
    for hh in range(2):
        def add(b, acc):
            return acc + jnp.where(bucket_ref[...] == b, rb_ref[b, 2 * j + hh], 0.0)
        bias = lax.fori_loop(0, N_BUCKETS, add, jnp.zeros((TQ, 2 * BK), F32))
        bias_ref[hh * TQ:(hh + 1) * TQ, :] = jnp.where(window, bias, NEG)


def _swa_probs(qs, k2, bias, own_block, sink_ref, i, j):
    s = _dot_nt(qs, k2) + bias
    s = jnp.where(own_block | (i > 0), s, NEG)
    row1 = lax.broadcasted_iota(jnp.int32, (2 * TQ, 1), 0)
    sink = jnp.where(row1 < TQ, sink_ref[2 * j], sink_ref[2 * j + 1])
    m = jnp.maximum(jnp.max(s, axis=1, keepdims=True), sink)
    e = jnp.exp(s - m)
    es = jnp.exp(sink - m)
    inv = 1.0 / (jnp.sum(e, axis=1, keepdims=True) + es)
    return e * inv, es * inv


def _swa_kv(ref, i):
    prev = pl.multiple_of(jnp.maximum(i - 1, 0) * BK, BK)
    cur = pl.multiple_of(i * BK, BK)
    return jnp.concatenate([ref[pl.ds(prev, BK), :], ref[pl.ds(cur, BK), :]], axis=0), prev, cur


def _swa_fwd(qkv, sinks, rel_bias, name):
    s = qkv.shape[0]
    nq = s // TQ
    npair = SW_QW // LANES
    qcol = 3 * SB_W // LANES
    bucket = jnp.asarray(_bucket_table())

    def body(q_ref, k_ref, v_ref, bucket_ref, sink_ref, rb_ref, o_ref, bias_ref):
        j = pl.program_id(0)
        step = pl.program_id(1)
        g = j // 2
        lane = lax.broadcasted_iota(jnp.int32, (TQ, LANES), 1)

        @pl.when(step == 0)
        def _():
            _swa_bias(bias_ref, bucket_ref, rb_ref, j)

        own_block = lax.broadcasted_iota(jnp.int32, (2 * TQ, 2 * BK), 1) >= BK
        for b in range(SWA_QB):
            i = step * SWA_QB + b
            rows = slice(b * TQ, (b + 1) * TQ)
            qs = _swa_align_in(q_ref[rows, :] * SCALE, lane, g)
            k2, _, _ = _swa_kv(k_ref, i)
            v2, _, _ = _swa_kv(v_ref, i)
            pr, _ = _swa_probs(qs, k2, bias_ref[...], own_block, sink_ref, i, j)
            o_ref[rows, :] = _swa_align_out(_dot(pr.astype(BF16), v2), lane, g).astype(BF16)

    assert nq % SWA_QB == 0
    return pl.pallas_call(
        body, out_shape=jax.ShapeDtypeStruct((s, SW_QW), BF16), grid=(npair, nq // SWA_QB),
        in_specs=[pl.BlockSpec((SWA_QB * TQ, LANES), lambda j, i: (i, qcol + j)),
                  pl.BlockSpec((s, LANES), lambda j, i: (0, qcol + npair)),
                  pl.BlockSpec((s, LANES), lambda j, i: (0, qcol + npair + 1)),
                  pl.BlockSpec((TQ, 2 * BK), lambda j, i: (0, 0)),
                  pl.BlockSpec(memory_space=pltpu.SMEM),
                  pl.BlockSpec(memory_space=pltpu.SMEM)],
        out_specs=pl.BlockSpec((SWA_QB * TQ, LANES), lambda j, i: (i, j)),
        scratch_shapes=[pltpu.VMEM((2 * TQ, 2 * BK), F32)],
        compiler_params=_params(2), name=name)(qkv, qkv, qkv, bucket, sinks, rel_bias)


def _swa_bwd(qkv, ob, dob, sinks, rel_bias, name):
    s = qkv.shape[0]
    nq = s // TQ
    npair = SW_QW // LANES
    qcol = 3 * SB_W // LANES
    bucket = jnp.asarray(_bucket_table())

    def body(q_ref, k_ref, v_ref, o_ref, do_ref, bucket_ref, sink_ref, rb_ref,
             dq_ref, dkv_ref, dsink_ref, drel_ref, bias_ref, dsacc_ref, dk_ref, dv_ref):
        j = pl.program_id(0)
        step = pl.program_id(1)
        g = j // 2
        lane = lax.broadcasted_iota(jnp.int32, (TQ, LANES), 1)
        row8 = lax.broadcasted_iota(jnp.int32, (SW_HEADS, LANES), 0)
        lane8 = lax.broadcasted_iota(jnp.int32, (SW_HEADS, LANES), 1)

        @pl.when((step == 0) & (j == 0))
        def _():
            dk_ref[...] = jnp.zeros_like(dk_ref)
            dv_ref[...] = jnp.zeros_like(dv_ref)
            dsink_ref[...] = jnp.zeros_like(dsink_ref)
            drel_ref[...] = jnp.zeros_like(drel_ref)

        @pl.when(step == 0)
        def _():
            _swa_bias(bias_ref, bucket_ref, rb_ref, j)
            dsacc_ref[...] = jnp.zeros_like(dsacc_ref)

        own_block = lax.broadcasted_iota(jnp.int32, (2 * TQ, 2 * BK), 1) >= BK
        ds_sum = jnp.zeros(dsacc_ref.shape, F32)
        dsink = jnp.zeros((SW_HEADS, LANES), F32)
        for b in range(SWA_QB):
            i = step * SWA_QB + b
            rows = slice(b * TQ, (b + 1) * TQ)
            qs = _swa_align_in(q_ref[rows, :] * SCALE, lane, g)
            do = do_ref[rows, :]
            dos = _swa_align_in(do, lane, g)
            dof = do.astype(F32) * o_ref[rows, :].astype(F32)
            d0 = jnp.sum(jnp.where(lane < HEAD_DIM, dof, 0.0), axis=1, keepdims=True)
            d1 = jnp.sum(jnp.where(lane >= HEAD_DIM, dof, 0.0), axis=1, keepdims=True)
            delta = jnp.concatenate([d0, d1], axis=0)
            k2, prev, cur = _swa_kv(k_ref, i)
            v2, _, _ = _swa_kv(v_ref, i)
            pr, psink = _swa_probs(qs, k2, bias_ref[...], own_block, sink_ref, i, j)
            ds = pr * (_dot_nt(dos, v2) - delta)
            ds_sum = ds_sum + ds
            sd = psink * delta
            ds0 = -jnp.sum(sd[:TQ, :], axis=0, keepdims=True)
            ds1 = -jnp.sum(sd[TQ:, :], axis=0, keepdims=True)
            dsink = dsink + jnp.where(row8 == 2 * j, ds0, jnp.where(row8 == 2 * j + 1, ds1, 0.0))
            dsb = ds.astype(BF16)
            dq_ref[rows, :] = _swa_align_out(_dot(dsb, k2) * SCALE, lane, g).astype(BF16)
            dk2 = _dot_tn(dsb, qs)
            dv2 = _dot_tn(pr.astype(BF16), dos)
            dk_ref[pl.ds(prev, BK), :] += dk2[:BK, :]
            dk_ref[pl.ds(cur, BK), :] += dk2[BK:, :]
            dv_ref[pl.ds(prev, BK), :] += dv2[:BK, :]
            dv_ref[pl.ds(cur, BK), :] += dv2[BK:, :]
        dsacc_ref[...] += ds_sum
        dsink_ref[...] += dsink

        @pl.when(step == nq // SWA_QB - 1)
        def _():
            for hh in range(2):
                def red(b, acc):
                    val = _sum_all(jnp.where(bucket_ref[...] == b, dsacc_ref[hh * TQ:(hh + 1) * TQ, :], 0.0))
                    return jnp.where((row8 == 2 * j + hh) & (lane8 == b), val, acc)
                drel_ref[...] += lax.fori_loop(0, N_BUCKETS, red, jnp.zeros((SW_HEADS, LANES), F32))

        @pl.when((step == nq // SWA_QB - 1) & (j == npair - 1))
        def _():
            dkv_ref[:, :LANES] = dk_ref[...].astype(BF16)
            dkv_ref[:, LANES:] = dv_ref[...].astype(BF16)

    whole = lambda j, i: (0, 0)
    return pl.pallas_call(
        body,
        out_shape=[jax.ShapeDtypeStruct((s, SW_QW), BF16), jax.ShapeDtypeStruct((s, 2 * LANES), BF16),
                   jax.ShapeDtypeStruct((SW_HEADS, LANES), F32), jax.ShapeDtypeStruct((SW_HEADS, LANES), F32)],
        grid=(npair, nq // SWA_QB),
        in_specs=[pl.BlockSpec((SWA_QB * TQ, LANES), lambda j, i: (i, qcol + j)),
                  pl.BlockSpec((s, LANES), lambda j, i: (0, qcol + npair)),
                  pl.BlockSpec((s, LANES), lambda j, i: (0, qcol + npair + 1)),
                  pl.BlockSpec((SWA_QB * TQ, LANES), lambda j, i: (i, j)),
                  pl.BlockSpec((SWA_QB * TQ, LANES), lambda j, i: (i, j)),
                  pl.BlockSpec((TQ, 2 * BK), whole),
                  pl.BlockSpec(memory_space=pltpu.SMEM),
                  pl.BlockSpec(memory_space=pltpu.SMEM)],
        out_specs=[pl.BlockSpec((SWA_QB * TQ, LANES), lambda j, i: (i, j)),
                   pl.BlockSpec((s, 2 * LANES), whole),
                   pl.BlockSpec((SW_HEADS, LANES), whole), pl.BlockSpec((SW_HEADS, LANES), whole)],
        scratch_shapes=[pltpu.VMEM((2 * TQ, 2 * BK), F32), pltpu.VMEM((2 * TQ, 2 * BK), F32),
                        pltpu.VMEM((s, LANES), F32), pltpu.VMEM((s, LANES), F32)],
        compiler_params=_params(2), name=name)(qkv, qkv, qkv, ob, dob, bucket, sinks, rel_bias)


def _acc_init(i, *refs):
    @pl.when(i == 0)
    def _():
        for r in refs:
            r[...] = jnp.zeros_like(r)


def _loss_bwd(x3, target, g, name):
    d = x3.shape[1]

    def body(x_ref, t_ref, g_ref, dx_ref, dg_ref, loss_ref):
        _acc_init(pl.program_id(0), dg_ref, loss_ref)
        x = x_ref[...]
        gv = g_ref[...]
        diff = _rms(x, gv) - t_ref[...]
        loss_ref[...] += 0.5 * jnp.sum(jnp.mean(jnp.square(diff), axis=-1, keepdims=True), axis=0, keepdims=True)
        dx, dg = _rms_bwd(diff * (1.0 / d), x, gv)
        dx_ref[...] = dx
        dg_ref[...] += dg

    return _rowcall(name, body, [x3, target], [g], [(d, F32)], [((1, d), F32), ((1, LANES), F32)])


def _ple_bwd(dx3, pe, gt, x2, g, wpg, name, side=None):
    d = x2.shape[1]

    def body(dx3_ref, pe_ref, gt_ref, x_ref, g_ref, w_ref, dpe_ref, dgt_ref, dx2_ref, dg_ref):
        _acc_init(pl.program_id(0), dg_ref)
        dx3 = dx3_ref[...]
        sg = _sigmoid(gt_ref[...])
        dpe_ref[...] = (dx3 * sg).astype(BF16)
        dgt = (dx3 * pe_ref[...] * sg * (1.0 - sg)).astype(BF16)
        dgt_ref[...] = dgt
        dx, dg = _rms_bwd(_dot_nt(dgt, w_ref[...]), x_ref[...], g_ref[...])
        dx2_ref[...] = dx3 + dx
        dg_ref[...] += dg

    return _rowcall(name, body, [dx3, pe, gt, x2], [g, wpg], [(d, BF16), (d, BF16), (d, F32)], [((1, d), F32)],
                    side=side)


def _ff2_bwd(dx2, r, w2, name):
    d = dx2.shape[1]
    dff = r.shape[1]

    def body(dx_ref, r_ref, w_ref, du_ref, dxb_ref):
        dxb = dx_ref[...].astype(BF16)
        dxb_ref[...] = dxb
        du_ref[...] = (_dot_nt(dxb, w_ref[...]) * (2.0 * r_ref[...].astype(F32))).astype(BF16)

    return _rowcall(name, body, [dx2, r], [w2], [(dff, BF16), (d, BF16)])


def _ff1_bwd(du, dx2, x1, g, w1, name, side=None):
    d = x1.shape[1]

    def body(du_ref, dx2_ref, x_ref, g_ref, w_ref, dx1_ref, dx1b_ref, dg_ref):
        _acc_init(pl.program_id(0), dg_ref)
        dx, dg = _rms_bwd(_dot_cols_t(du_ref[...], w_ref), x_ref[...], g_ref[...])
        dx1 = dx2_ref[...] + dx
        dx1_ref[...] = dx1
        dx1b_ref[...] = dx1.astype(BF16)
        dg_ref[...] += dg

    return _rowcall(name, body, [du, dx2, x1], [g, w1], [(d, F32), (d, BF16)], [((1, d), F32)], side=side)


def _mixer_bwd(dx1b, gates, oa, ob, wo, wua, wub, name, side=None):
    d = dx1b.shape[1]

    def body(dx_ref, gate_ref, oa_ref, ob_ref, wo_ref, wua_ref, wub_ref,
             dya_ref, dyb_ref, dgate_ref, doa_ref, dob_ref):
        dm = _dot_nt(dx_ref[...], wo_ref[...])
        sa = _sigmoid(gate_ref[:, :d])
        sb = _sigmoid(gate_ref[:, d:])
        ya = _dot_cols(oa_ref[...], wua_ref)
        yb = _dot_cols(ob_ref[...], wub_ref)
        dya = (dm * sa).astype(BF16)
        dyb = (dm * sb).astype(BF16)
        dya_ref[...] = dya
        dyb_ref[...] = dyb
        dgate_ref[:, :d] = (dm * ya * sa * (1.0 - sa)).astype(BF16)
        dgate_ref[:, d:] = (dm * yb * sb * (1.0 - sb)).astype(BF16)
        doa_ref[...] = _dot_cols_t(dya, wua_ref).astype(BF16)
        dob_ref[...] = _dot_cols_t(dyb, wub_ref).astype(BF16)

    return _rowcall(name, body, [dx1b, gates, oa, ob], [wo, wua, wub],
                    [(d, BF16), (d, BF16), (2 * d, BF16), (SB_W, BF16), (SW_QW, BF16)], side=side)


def _inproj_bwd(pieces, dx1, x, g, wt, name, side=None):
    d = x.shape[1]
    n = len(pieces)
    offsets = [sum(pc.shape[1] for pc in pieces[:p]) for p in range(n + 1)]

    def body(*refs):
        dx1_ref, x_ref, g_ref, w_ref, dx_ref, dg_ref = refs[n:]
        _acc_init(pl.program_id(0), dg_ref)
        dh = _dot(refs[0][...], w_ref[:offsets[1], :])
        for p in range(1, n):
            dh = dh + _dot(refs[p][...], w_ref[offsets[p]:offsets[p + 1], :])
        dx, dg = _rms_bwd(dh, x_ref[...], g_ref[...])
        dx_ref[...] = dx1_ref[...] + dx
        dg_ref[...] += dg

    return _rowcall(name, body, list(pieces) + [dx1, x], [g, wt], [(d, F32)], [((1, d), F32)], side=side)


def _tile(n, cap):
    assert n % LANES == 0
    return max(t for t in range(LANES, min(n, cap) + 1, LANES) if n % t == 0)


def _mm_tn(a, b, name, nshard=1):
    s, ka = a.shape
    nb = b.shape[1]
    n = nb // nshard
    ta = _tile(ka, 512)
    tb = _tile(n, 1024)
    per = n // tb

    def body(a_ref, b_ref, o_ref):
        o_ref[...] = _dot_tn(a_ref[...].astype(BF16), b_ref[...].astype(BF16))

    return pl.pallas_call(
        body, out_shape=jax.ShapeDtypeStruct((nshard, ka, n), F32), grid=(nb // tb, ka // ta),
        in_specs=[pl.BlockSpec((s, ta), lambda jb, ia: (0, ia)), pl.BlockSpec((s, tb), lambda jb, ia: (0, jb))],
        out_specs=pl.BlockSpec((None, ta, tb), lambda jb, ia: (jb // per, ia, jb % per)),
        compiler_params=_params(2), name=name)(a, b)


def _mm_tn_pieces(pieces, b, name, side=None):
    s, nb = b.shape
    ta = 256
    n_in = len(pieces) + 1
    tiles = [pc.shape[1] // ta for pc in pieces]
    assert all(pc.shape[1] % ta == 0 for pc in pieces)
    starts = [sum(tiles[:p]) for p in range(len(pieces))]
    tb = _tile(nb, 1024)
    grid = (nb // tb, sum(tiles))

    def body(*refs):
        a_refs, b_ref = refs[:n_in - 1], refs[n_in - 1]
        side_in, (o_ref,), side_out, sems = _side_refs(side, refs[n_in:], 1)
        jb, ia = pl.program_id(0), pl.program_id(1)
        if side is not None:
            @pl.when((jb == 0) & (ia == 0))
            def _():
                side.start(side_in, side_out, *sems)
        for p in range(len(pieces)):
            @pl.when((ia >= starts[p]) & (ia < starts[p] + tiles[p]))
            def _(p=p):
                o_ref[...] = _dot_tn(a_refs[p][...], b_ref[...])
        if side is not None:
            @pl.when((jb == grid[0] - 1) & (ia == grid[1] - 1))
            def _():
                side.finish(side_in, side_out, *sems)

    def piece_spec(p):
        return pl.BlockSpec((s, ta), lambda jb, ia: (0, jnp.clip(ia - starts[p], 0, tiles[p] - 1)))

    s_in, s_shape, s_out, s_alias, s_sems = _side_specs(side, n_in, 1)
    outs = pl.pallas_call(
        body, out_shape=[jax.ShapeDtypeStruct((sum(tiles) * ta, nb), F32)] + s_shape, grid=grid,
        in_specs=[piece_spec(p) for p in range(len(pieces))] + [pl.BlockSpec((s, tb), lambda jb, ia: (0, jb))] + s_in,
        out_specs=[pl.BlockSpec((ta, tb), lambda jb, ia: (ia, jb))] + s_out,
        input_output_aliases=s_alias, scratch_shapes=s_sems,
        compiler_params=_params(2), name=name)(*pieces, b, *(side.arrays if side else ()))
    return outs[0] if side is None else (outs[0], outs[1:])


def _place():
    return lax.axis_index("x"), lax.axis_index("y"), lax.axis_index("c")


def _chip_peer(x, y, k):
    return (x ^ (k >> 1), y ^ (k & 1))


def _row_tile(k, cap=544):
    return max(t for t in range(32, min(k, cap) + 1, 32) if k % t == 0)


def _cast_bf16(w, r, name):
    l, k, n = w.shape
    assert l == 2
    tk = _row_tile(k)

    def body(r_ref, w_ref, o0_ref, o1_ref):
        o0_ref[...] = w_ref[0].astype(BF16)
        o1_ref[...] = w_ref[1].astype(BF16)

    out_spec = pl.BlockSpec((None, tk, n), lambda i, r_ref: (r_ref[0], i, 0))
    return pl.pallas_call(
        body, out_shape=[jax.ShapeDtypeStruct((4, k, n), BF16)] * 2,
        grid_spec=pltpu.PrefetchScalarGridSpec(
            num_scalar_prefetch=1, grid=(k // tk,),
            in_specs=[pl.BlockSpec((l, tk, n), lambda i, r_ref: (0, i, 0))],
            out_specs=[out_spec, out_spec]),
        compiler_params=_params(1), name=name)(r, w)


class _Exchange(NamedTuple):
    arrays: tuple
    aliased: tuple
    sems: tuple
    start: Callable
    finish: Callable


def _all_gather(bufs):
    nt = len(bufs)

    def ici(t, ins, outs, send_sems, recv_sems, x, y, c, r, k):
        return pltpu.make_async_remote_copy(
            src_ref=ins[t].at[r, c], dst_ref=outs[t].at[r, c], send_sem=send_sems.at[t, k - 1],
            recv_sem=recv_sems.at[t, k - 1], device_id=(*_chip_peer(x, y, k), c), device_id_type=MESH)

    def d2d(t, outs, send_sems, recv_sems, x, y, c, r, k, half):
        slab = outs[t].at[r ^ k, half]
        return pltpu.make_async_remote_copy(
            src_ref=slab, dst_ref=slab, send_sem=send_sems.at[t, 2 + k], recv_sem=recv_sems.at[t, 2 + k],
            device_id=(x, y, 1 - c), device_id_type=MESH)

    def start(ins, outs, send_sems, recv_sems):
        x, y, c = _place()
        r = 2 * x + y
        for t in range(nt):
            for k in (1, 2, 3):
                ici(t, ins, outs, send_sems, recv_sems, x, y, c, r, k).start()

    def finish(ins, outs, send_sems, recv_sems):
        x, y, c = _place()
        r = 2 * x + y
        for t in range(nt):
            for k in (1, 2, 3):
                slab = outs[t].at[r ^ k, c]
                pltpu.make_async_remote_copy(
                    src_ref=slab, dst_ref=slab, send_sem=send_sems.at[t, k - 1], recv_sem=recv_sems.at[t, k - 1],
                    device_id=(x, y, 1 - c), device_id_type=MESH).wait_recv()
                d2d(t, outs, send_sems, recv_sems, x, y, c, r, k, c).start()
        for t in range(nt):
            for k in (1, 2, 3):
                d2d(t, outs, send_sems, recv_sems, x, y, c, r, k, 1 - c).wait_recv()
        for t in range(nt):
            for k in (1, 2, 3):
                ici(t, ins, outs, send_sems, recv_sems, x, y, c, r, k).wait_send()
                d2d(t, outs, send_sems, recv_sems, x, y, c, r, k, c).wait_send()

    return _Exchange(tuple(bufs), tuple(range(nt)), (nt, 6), start, finish)


def _run_exchange(name, ex):
    n_in, n_out = len(ex.arrays), len(ex.aliased)

    def body(*refs):
        ins, outs = refs[:n_in], refs[n_in:n_in + n_out]
        send_sems, recv_sems = refs[n_in + n_out:]
        ex.start(ins, outs, send_sems, recv_sems)
        ex.finish(ins, outs, send_sems, recv_sems)

    any_spec = pl.BlockSpec(memory_space=pl.ANY)
    return pl.pallas_call(
        body, out_shape=[jax.ShapeDtypeStruct(ex.arrays[a].shape, ex.arrays[a].dtype) for a in ex.aliased],
        in_specs=[any_spec] * n_in, out_specs=[any_spec] * n_out,
        input_output_aliases={a: o for o, a in enumerate(ex.aliased)},
        scratch_shapes=[pltpu.SemaphoreType.DMA(ex.sems), pltpu.SemaphoreType.DMA(ex.sems)],
        name=name)(*ex.arrays)


def _rs_to_sibling(grads):
    nt = len(grads)
    landing = [lax.empty((4,) + g.shape[2:], F32) for g in grads]

    def copies(ins, outs, send_sems, recv_sems):
        x, y, c = _place()
        return [pltpu.make_async_remote_copy(
            src_ref=ins[t].at[:, 1 - c], dst_ref=outs[t], send_sem=send_sems.at[t], recv_sem=recv_sems.at[t],
            device_id=(x, y, 1 - c), device_id_type=MESH) for t in range(nt)]

    def start(ins, outs, send_sems, recv_sems):
        for cp in copies(ins, outs, send_sems, recv_sems):
            cp.start()

    def finish(ins, outs, send_sems, recv_sems):
        for cp in copies(ins, outs, send_sems, recv_sems):
            cp.wait()

    return _Exchange(tuple(grads) + tuple(landing), tuple(range(nt, 2 * nt)), (nt,), start, finish)


def _add_half(g, recv, cr, name):
    _, _, k2, n = g.shape
    tk = _row_tile(k2)

    def body(cr_ref, g_ref, r_ref, sums_ref, mine_ref):
        val = (g_ref[...] + r_ref[...]).astype(BF16)
        sums_ref[...] = val

        @pl.when(pl.program_id(1) == cr_ref[1])
        def _():
            mine_ref[...] = val

    return pl.pallas_call(
        body, out_shape=[jax.ShapeDtypeStruct((4, k2, n), BF16)] * 2,
        grid_spec=pltpu.PrefetchScalarGridSpec(
            num_scalar_prefetch=1, grid=(k2 // tk, 4),
            in_specs=[pl.BlockSpec((None, None, tk, n), lambda i, q, cr_ref: (q, cr_ref[0], i, 0)),
                      pl.BlockSpec((None, tk, n), lambda i, q, cr_ref: (q, i, 0))],
            out_specs=[pl.BlockSpec((None, tk, n), lambda i, q, cr_ref: (q, i, 0)),
                       pl.BlockSpec((None, tk, n), lambda i, q, cr_ref: (cr_ref[1], i, 0))]),
        compiler_params=_params(2), name=name)(cr, g, recv)


def _rs_to_chips(sums, parts):
    nt = len(sums)

    def copies(ins, outs, send_sems, recv_sems):
        x, y, c = _place()
        r = 2 * x + y
        return [pltpu.make_async_remote_copy(
            src_ref=ins[t].at[r ^ k], dst_ref=outs[t].at[r], send_sem=send_sems.at[t, k - 1],
            recv_sem=recv_sems.at[t, k - 1], device_id=(*_chip_peer(x, y, k), c), device_id_type=MESH)
            for t in range(nt) for k in (1, 2, 3)]

    def start(ins, outs, send_sems, recv_sems):
        for cp in copies(ins, outs, send_sems, recv_sems):
            cp.start()

    def finish(ins, outs, send_sems, recv_sems):
        for cp in copies(ins, outs, send_sems, recv_sems):
            cp.wait()

    return _Exchange(tuple(sums) + tuple(parts), tuple(range(nt, 2 * nt)), (nt, 3), start, finish)


def _sum4(parts, cr, name):
    _, k2, n = parts.shape
    tk = _row_tile(k2)

    def body(cr_ref, p_ref, o_ref):
        p = p_ref[...].astype(F32)
        o_ref[...] = ((p[0] + p[1]) + p[2]) + p[3]

    return pl.pallas_call(
        body, out_shape=jax.ShapeDtypeStruct((2, k2, n), F32),
        grid_spec=pltpu.PrefetchScalarGridSpec(
            num_scalar_prefetch=1, grid=(k2 // tk,),
            in_specs=[pl.BlockSpec((4, tk, n), lambda i, cr_ref: (0, i, 0))],
            out_specs=pl.BlockSpec((None, tk, n), lambda i, cr_ref: (cr_ref[0], i, 0))),
        compiler_params=_params(1), name=name)(cr, parts)


def _exchange_halves(both):
    nt = len(both)

    def copies(ins, outs, send_sems, recv_sems):
        x, y, c = _place()
        return [pltpu.make_async_remote_copy(
            src_ref=ins[t].at[c], dst_ref=outs[t].at[c], send_sem=send_sems.at[t], recv_sem=recv_sems.at[t],
            device_id=(x, y, 1 - c), device_id_type=MESH) for t in range(nt)]

    def start(ins, outs, send_sems, recv_sems):
        for cp in copies(ins, outs, send_sems, recv_sems):
            cp.start()

    def finish(ins, outs, send_sems, recv_sems):
        for cp in copies(ins, outs, send_sems, recv_sems):
            cp.wait()

    return _Exchange(tuple(both), tuple(range(nt)), (nt,), start, finish)


def _adamw_math(w, g, m, v):
    m = ADAM_B1 * m + (1.0 - ADAM_B1) * g
    v = ADAM_B2 * v + (1.0 - ADAM_B2) * jnp.square(g)
    m_hat = m / (1.0 - ADAM_B1 ** ADAM_STEP)
    v_hat = v / (1.0 - ADAM_B2 ** ADAM_STEP)
    delta = -ADAM_LR * (m_hat / (jnp.sqrt(v_hat) + ADAM_EPS) + ADAM_WD * w)
    return delta, m, v


def _adamw(w, m, v, g0, g1, name):
    _, k, n = w.shape
    tk = _row_tile(k)
    nk = k // tk

    def body(w_ref, m_ref, v_ref, g0_ref, g1_ref, grad_ref, delta_ref, nm_ref, nv_ref):
        g = jnp.where(pl.program_id(0) == 0, g0_ref[...], g1_ref[...])
        delta, nm, nv = _adamw_math(w_ref[...], g, m_ref[...], v_ref[...])
        grad_ref[...] = g
        delta_ref[...] = delta
        nm_ref[...] = nm
        nv_ref[...] = nv

    lay = pl.BlockSpec((None, tk, n), lambda a, i: (a, i, 0))
    g0_spec = pl.BlockSpec((tk, n), lambda a, i: (jnp.where(a == 0, i, nk - 1), 0))
    g1_spec = pl.BlockSpec((tk, n), lambda a, i: (jnp.where(a == 1, i, 0), 0))
    return pl.pallas_call(
        body, out_shape=[jax.ShapeDtypeStruct(w.shape, F32)] * 4, grid=(2, nk),
        in_specs=[lay, lay, lay, g0_spec, g1_spec], out_specs=[lay] * 4,
        compiler_params=_params(2), name=name)(w, m, v, g0, g1)


def _small_allreduce_adamw(gpart, w, m, v):
    shape = gpart.shape

    def body(g_ref, w_ref, m_ref, v_ref, gsum_ref, delta_ref, nm_ref, nv_ref, recv_ref, send_sems, recv_sems):
        x, y, c = _place()
        me = 4 * x + 2 * y + c
        recv_ref[me] = g_ref[...]
        cps = []
        for k in range(1, 8):
            peer = (x ^ (k >> 2), y ^ ((k >> 1) & 1), c ^ (k & 1))
            cp = pltpu.make_async_remote_copy(
                src_ref=g_ref, dst_ref=recv_ref.at[me], send_sem=send_sems.at[k - 1], recv_sem=recv_sems.at[k - 1],
                device_id=peer, device_id_type=MESH)
            cp.start()
            cps.append(cp)
        for cp in cps:
            cp.wait()
        g = recv_ref[0]
        for dev in range(1, 8):
            g = g + recv_ref[dev]
        delta, nm, nv = _adamw_math(w_ref[...], g, m_ref[...], v_ref[...])
        gsum_ref[...] = g
        delta_ref[...] = delta
        nm_ref[...] = nm
        nv_ref[...] = nv

    vm = pl.BlockSpec(memory_space=pltpu.VMEM)
    return pl.pallas_call(
        body, out_shape=[jax.ShapeDtypeStruct(shape, F32)] * 4, in_specs=[vm] * 4, out_specs=[vm] * 4,
        scratch_shapes=[pltpu.VMEM((8,) + shape, F32), pltpu.SemaphoreType.DMA((7,)), pltpu.SemaphoreType.DMA((7,))],
        name="small_allreduce_adamw")(gpart, w, m, v)


BIG = ("w_in", "w_up_a", "w_up_b", "w_o", "w_ff1", "w_ff2", "w_pe", "w_pg")
COL_SHARDED = ("w_in", "w_up_a", "w_up_b", "w_ff1", "w_pe")
ROW_SHARDED = ("w_o", "w_ff2", "w_pg")
SMALL_ROWS = 16


def _pack_small(g_mix, g_mlp, g_pe, g_final, sinks, rel_bias, loss=None):
    d = g_final.shape[0]
    row = lambda v: jnp.pad(v.reshape(1, -1), ((0, 0), (0, d - v.size)))
    rows = [g_mix, g_mlp, g_pe, g_final.reshape(1, d),
            jnp.zeros((1, d), F32) if loss is None else row(loss), row(sinks), row(rel_bias)]
    out = jnp.concatenate(rows, axis=0)
    return jnp.pad(out, ((0, SMALL_ROWS - out.shape[0]), (0, 0)))


def _unpack_small(a, sinks_shape, rel_shape):
    return (a[0:2], a[2:4], a[4:6], a[6], a[8, :sinks_shape[0] * sinks_shape[1]].reshape(sinks_shape),
            a[9, :rel_shape[0] * rel_shape[1]].reshape(rel_shape))


def kernel(x, p, w_in, w_up_a, w_up_b, w_o, w_ff1, w_ff2, w_pe, w_pg, g_mix, g_mlp, g_pe, g_final, sinks, rel_bias, loss_target, m_w_in, m_w_up_a, m_w_up_b, m_w_o, m_w_ff1, m_w_ff2, m_w_pe, m_w_pg, m_g_mix, m_g_mlp, m_g_pe, m_g_final, m_sinks, m_rel_bias, v_w_in, v_w_up_a, v_w_up_b, v_w_o, v_w_ff1, v_w_ff2, v_w_pe, v_w_pg, v_g_mix, v_g_mlp, v_g_pe, v_g_final, v_sinks, v_rel_bias):
    depth = w_in.shape[0]
    assert depth == 2
    x0 = x[0]
    target = loss_target[0]
    d = x0.shape[1]
    wl = dict(w_in=w_in, w_up_a=w_up_a, w_up_b=w_up_b, w_o=w_o, w_ff1=w_ff1, w_ff2=w_ff2, w_pe=w_pe, w_pg=w_pg)
    ml = dict(w_in=m_w_in, w_up_a=m_w_up_a, w_up_b=m_w_up_b, w_o=m_w_o, w_ff1=m_w_ff1, w_ff2=m_w_ff2, w_pe=m_w_pe, w_pg=m_w_pg)
    vl = dict(w_in=v_w_in, w_up_a=v_w_up_a, w_up_b=v_w_up_b, w_o=v_w_o, w_ff1=v_w_ff1, w_ff2=v_w_ff2, w_pe=v_w_pe, w_pg=v_w_pg)
    c_idx = lax.axis_index("c").astype(jnp.int32)
    r_idx = (2 * lax.axis_index("x") + lax.axis_index("y")).astype(jnp.int32)
    cr = jnp.stack([c_idx, r_idx])

    wl["w_in"], ml["w_in"], vl["w_in"] = (jnp.swapaxes(a, 1, 2) for a in (w_in, m_w_in, v_w_in))

    bufs = {}
    for n in BIG:
        k, nn = wl[n].shape[1:]
        for l, b in enumerate(_cast_bf16(wl[n], r_idx.reshape(1), "cast_" + n)):
            bufs[n, l] = b.reshape(4, 2, k // 2, nn)

    def gather(keys, run):
        for key, b in zip(keys, run(_all_gather([bufs[key] for key in keys]))):
            bufs[key] = b

    def gathered(n, l):
        _, _, k2, nn = bufs[n, l].shape
        if n in ROW_SHARDED or n == "w_in":
            return bufs[n, l].reshape(8 * k2, nn)
        return bufs[n, l].reshape(4, 2 * k2, nn)

    gather([("w_in", 0)], lambda ex: _run_exchange("all_gather_first", ex))

    full = {n: [None] * depth for n in BIG}
    saved = []
    xi = x0
    for i in range(depth):
        st = dict(x0=xi)
        gm = g_mix[i].reshape(1, d)
        full["w_in"][i] = gathered("w_in", i)
        st["h1"], st["qkv"], st["gates"] = _inproj_fwd(xi, gm, full["w_in"][i], f"inproj_fwd_{i}")

        def attend(ex):
            st["oa"], st["lt"], st["nb"], filled = _sb_fwd(st["qkv"], f"sb_fwd_{i}", ex)
            return filled

        gather([(n, i) for n in BIG if n != "w_in"], attend)
        for n in BIG:
            if n != "w_in":
                full[n][i] = gathered(n, i)
        st["ob"] = _swa_fwd(st["qkv"], sinks[i], rel_bias, f"swa_fwd_{i}")
        st["m"], st["x1"] = _mixer_fwd(st["oa"], st["ob"], st["gates"], xi, full["w_up_a"][i], full["w_up_b"][i],
                                       full["w_o"][i], f"mixer_fwd_{i}")
        if i == 0:
            def feed_forward(ex):
                (st["h2"], st["u"], st["a"]), filled = _ff1_fwd(st["x1"], g_mlp[i].reshape(1, d), full["w_ff1"][i],
                                                               f"ff1_fwd_{i}", ex)
                return filled

            gather([("w_in", 1)], feed_forward)
        else:
            st["h2"], st["u"], st["a"] = _ff1_fwd(st["x1"], g_mlp[i].reshape(1, d), full["w_ff1"][i], f"ff1_fwd_{i}")
        st["x2"] = _ff2_fwd(st["a"], st["x1"], full["w_ff2"][i], f"ff2_fwd_{i}")
        st["pb"], st["h3"], st["pe"], st["gt"], xi = _ple_fwd(p[i, 0], st["x2"], g_pe[i].reshape(1, d),
                                                            full["w_pe"][i], full["w_pg"][i], f"ple_fwd_{i}")
        saved.append(st)

    dx, dg_final, loss_part = _loss_bwd(xi, target, g_final.reshape(1, d), "loss_bwd")
    gw = {n: [None] * depth for n in BIG}
    reduced = {}

    chip_sums = {}

    def to_sibling(keys, run):
        tensors = []
        for n, l in keys:
            g = gw[n][l]
            if n in ROW_SHARDED:
                ka, nb = g.shape[1:]
                g = g.reshape(4, ka // 4, nb)
            _, k, nn = g.shape
            tensors.append(g.reshape(4, 2, k // 2, nn))
        for (n, l), g, r in zip(keys, tensors, run(_rs_to_sibling(tensors))):
            chip_sums[n, l] = _add_half(g, r, cr, f"add_half_{n}_{l}")

    def to_chips(keys):
        return _rs_to_chips([chip_sums[k][0] for k in keys], [chip_sums[k][1] for k in keys])

    halves = {}

    def sum_chips(keys, parts):
        for (n, l), pc in zip(keys, parts):
            halves[n, l] = _sum4(pc, cr, f"sum4_{n}_{l}")

    def swap_halves(keys):
        def store(filled):
            for key, both in zip(keys, filled):
                reduced[key] = both
        return _exchange_halves([halves[k] for k in keys]), store

    dg_mix, dg_mlp, dg_pe, dsinks = [None] * depth, [None] * depth, [None] * depth, [None] * depth
    drel = jnp.zeros((SW_HEADS, LANES), F32)
    for i in reversed(range(depth)):
        st = saved[i]
        dpe, dgt, dx2, dg_pe[i] = _ple_bwd(dx, st["pe"], st["gt"], st["x2"], g_pe[i].reshape(1, d),
                                           full["w_pg"][i], f"ple_bwd_{i}")
        gw["w_pe"][i] = _mm_tn(st["pb"], dpe, f"dw_pe_{i}", 4)
        gw["w_pg"][i] = _mm_tn(st["h3"], dgt, f"dw_pg_{i}")
        du, dx2b = _ff2_bwd(dx2, st["u"], full["w_ff2"][i], f"ff2_bwd_{i}")
        gw["w_ff2"][i] = _mm_tn(st["a"], dx2b, f"dw_ff2_{i}")
        gw["w_ff1"][i] = _mm_tn(st["h2"], du, f"dw_ff1_{i}", 4)
        if i == 0:
            (dx1, dx1b, dg_mlp[i]), parts = _ff1_bwd(du, dx2, st["x1"], g_mlp[i].reshape(1, d), full["w_ff1"][i],
                                                    f"ff1_bwd_{i}", to_chips([("w_in", 1)]))
            sum_chips([("w_in", 1)], parts)
        else:
            dx1, dx1b, dg_mlp[i] = _ff1_bwd(du, dx2, st["x1"], g_mlp[i].reshape(1, d), full["w_ff1"][i],
                                            f"ff1_bwd_{i}")
        gw["w_o"][i] = _mm_tn(st["m"], dx1b, f"dw_o_{i}")
        early = [(n, i) for n in ("w_pe", "w_pg", "w_ff2", "w_ff1", "w_o")]

        def mixer(ex):
            (dya, dyb, dgates, doa, dob), landed = _mixer_bwd(
                dx1b, st["gates"], st["oa"], st["ob"], full["w_o"][i], full["w_up_a"][i], full["w_up_b"][i],
                f"mixer_bwd_{i}", ex)
            st.update(dya=dya, dyb=dyb, dgates=dgates, doa=doa, dob=dob)
            return landed

        to_sibling(early, mixer)
        dgates = st["dgates"]
        gw["w_up_a"][i] = _mm_tn(st["oa"], st["dya"], f"dw_up_a_{i}", 4)
        gw["w_up_b"][i] = _mm_tn(st["ob"], st["dyb"], f"dw_up_b_{i}", 4)
        late = [("w_up_a", i), ("w_up_b", i)]
        to_sibling(late, lambda ex: _run_exchange(f"rs_to_sibling_{i}", ex))
        keys = early + late
        dqa, dka, dva, parts = _sb_bwd(st["qkv"], st["lt"], st["nb"], st["doa"], f"sb_bwd_{i}", to_chips(keys))
        sum_chips(keys, parts)
        dob = st["dob"]
        dqb, dkvb, dsk, drl = _swa_bwd(st["qkv"], st["ob"], dob, sinks[i], rel_bias, f"swa_bwd_{i}")
        dsinks[i] = dsk[:, 0]
        drel = drel + drl
        dproj = [dqa, dka, dva, dqb, dkvb, dgates]
        swap, store = swap_halves(keys + ([("w_in", 1)] if i == 0 else []))
        dw_in_t, filled = _mm_tn_pieces(dproj, st["h1"], f"dw_in_{i}", swap)
        store(filled)
        gw["w_in"][i] = dw_in_t.reshape(4, dw_in_t.shape[0] // 4, d)
        if i == 1:
            def inproj(ex):
                (dx, dg_mix[i]), landed = _inproj_bwd(dproj, dx1, st["x0"], g_mix[i].reshape(1, d),
                                                      full["w_in"][i], f"inproj_bwd_{i}", ex)
                st["dx"] = dx
                return landed

            to_sibling([("w_in", 1)], inproj)
            dx = st["dx"]
        else:
            keys = [("w_in", 0)]
            to_sibling(keys, lambda ex: _run_exchange("rs_to_sibling_last", ex))
            (dx, dg_mix[i]), parts = _inproj_bwd(dproj, dx1, st["x0"], g_mix[i].reshape(1, d),
                                                 full["w_in"][i], f"inproj_bwd_{i}", to_chips(keys))
            sum_chips(keys, parts)
            swap, store = swap_halves(keys)
            store(_run_exchange("exchange_halves_last", swap))
    grad_x = dx[None]

    outs = {}
    for n in BIG:
        g0, g1 = (reduced[n, l].reshape(wl[n].shape[1:]) for l in range(depth))
        outs[n] = _adamw(wl[n], ml[n], vl[n], g0, g1, "adamw_" + n)
    outs["w_in"] = [jnp.swapaxes(a, 1, 2) for a in outs["w_in"]]

    drel_bias = drel[:, :N_BUCKETS].T
    gsmall = _pack_small(jnp.concatenate(dg_mix, 0), jnp.concatenate(dg_mlp, 0), jnp.concatenate(dg_pe, 0),
                         dg_final[0], jnp.stack(dsinks), drel_bias, loss_part[0, :1])
    wsmall = _pack_small(g_mix, g_mlp, g_pe, g_final, sinks, rel_bias)
    msmall = _pack_small(m_g_mix, m_g_mlp, m_g_pe, m_g_final, m_sinks, m_rel_bias)
    vsmall = _pack_small(v_g_mix, v_g_mlp, v_g_pe, v_g_final, v_sinks, v_rel_bias)
    small = _small_allreduce_adamw(gsmall, wsmall, msmall, vsmall)
    loss = small[0][7, 0]
    small = [_unpack_small(a, sinks.shape, rel_bias.shape) for a in small]

    result = [loss, grad_x]
    for kind in range(4):
        result += [outs[n][kind] for n in BIG]
        result += list(small[kind])
    return tuple(result)
```

```python
import functools
import math
from typing import Callable, NamedTuple

import numpy as np
import jax
import jax.numpy as jnp
from jax import lax
from jax.experimental import pallas as pl
from jax.experimental.pallas import tpu as pltpu

F32 = jnp.float32
BF16 = jnp.bfloat16
MESH = pl.DeviceIdType.MESH

HEAD_DIM = 64
SB_HEADS = 8
SW_HEADS = 8
SW_KV_HEADS = 2
WINDOW = 128
N_BUCKETS = 32
MAX_DISTANCE = 128
EPS = 1e-6
SB_W = SB_HEADS * HEAD_DIM
SW_QW = SW_HEADS * HEAD_DIM
SW_KVW = SW_KV_HEADS * HEAD_DIM
QKV_W = 3 * SB_W + SW_QW + 2 * SW_KVW
SCALE = HEAD_DIM ** -0.5
assert SCALE == 0.125
LANES = 128
TQ = 128
BK = 128
NEG = -1e30
SB_EXHAUSTED = -106.0

ADAM_LR = 0.001
ADAM_B1 = 0.9
ADAM_B2 = 0.999
ADAM_EPS = 1e-08
ADAM_WD = 0.01
ADAM_STEP = 10

VMEM_LIMIT = 56 * 1024 * 1024


def _dot(a, b):
    return jnp.dot(a, b, preferred_element_type=F32)


def _dot_nt(a, b):
    return lax.dot_general(a, b, (((1,), (1,)), ((), ())), preferred_element_type=F32)


def _dot_tn(a, b):
    return lax.dot_general(a, b, (((0,), (0,)), ((), ())), preferred_element_type=F32)


def _sum_all(x):
    return jnp.sum(jnp.sum(x, axis=1, keepdims=True), axis=0, keepdims=True)


def _sigmoid(x):
    return 1.0 / (1.0 + jnp.exp(-x))


def _rms(x, g):
    r = lax.rsqrt(jnp.mean(x * x, axis=-1, keepdims=True) + EPS)
    return (x * r) * g


def _rms_bwd(dy, x, g):
    r = lax.rsqrt(jnp.mean(x * x, axis=-1, keepdims=True) + EPS)
    n = x * r
    dg = jnp.sum(dy * n, axis=0, keepdims=True)
    dn = dy * g
    dx = r * (dn - n * jnp.mean(dn * n, axis=-1, keepdims=True))
    return dx, dg


def _params(n_axes):
    return pltpu.CompilerParams(dimension_semantics=("arbitrary",) * n_axes, vmem_limit_bytes=VMEM_LIMIT)


def _rowcall(name, body, row_ins, const_ins, row_outs, acc_outs=(), tm=512, side=None):
    s = row_ins[0].shape[0]
    assert s % tm == 0
    in_specs = [pl.BlockSpec((tm, a.shape[1]), lambda i: (i, 0)) for a in row_ins]
    in_specs += [pl.BlockSpec(a.shape, functools.partial(lambda i, nd: (0,) * nd, nd=a.ndim)) for a in const_ins]
    out_shape = [jax.ShapeDtypeStruct((s, c), dt) for c, dt in row_outs]
    out_specs = [pl.BlockSpec((tm, c), lambda i: (i, 0)) for c, _ in row_outs]
    out_shape += [jax.ShapeDtypeStruct(sh, dt) for sh, dt in acc_outs]
    out_specs += [pl.BlockSpec(sh, functools.partial(lambda i, nd: (0,) * nd, nd=len(sh))) for sh, _ in acc_outs]
    if side is None:
        return pl.pallas_call(body, out_shape=out_shape, grid=(s // tm,), in_specs=in_specs, out_specs=out_specs,
                              compiler_params=_params(1), name=name)(*row_ins, *const_ins)
    n_in, n_out = len(in_specs), len(out_specs)

    def with_side(*refs):
        side_in, outs, side_out, sems = _side_refs(side, refs[n_in:], n_out)

        @pl.when(pl.program_id(0) == 0)
        def _():
            side.start(side_in, side_out, *sems)

        body(*refs[:n_in], *outs)

        @pl.when(pl.program_id(0) == s // tm - 1)
        def _():
            side.finish(side_in, side_out, *sems)

    s_in, s_shape, s_out, s_alias, s_sems = _side_specs(side, n_in, n_out)
    outs = pl.pallas_call(with_side, out_shape=out_shape + s_shape, grid=(s // tm,), in_specs=in_specs + s_in,
                          out_specs=out_specs + s_out, input_output_aliases=s_alias, scratch_shapes=s_sems,
                          compiler_params=_params(1), name=name)(*row_ins, *const_ins, *side.arrays)
    return outs[:n_out], outs[n_out:]


def _dot_cols(a, w_ref):
    return jnp.concatenate([_dot(a, w_ref[r]) for r in range(w_ref.shape[0])], axis=1)


def _dot_cols_t(a, w_ref):
    n = w_ref.shape[2]
    out = _dot_nt(a[:, :n], w_ref[0])
    for r in range(1, w_ref.shape[0]):
        out = out + _dot_nt(a[:, r * n:(r + 1) * n], w_ref[r])
    return out


def _inproj_fwd(x, g, wt, name):
    d = x.shape[1]

    def body(x_ref, g_ref, w_ref, h_ref, qkv_ref, gate_ref):
        hb = _rms(x_ref[...], g_ref[...]).astype(BF16)
        h_ref[...] = hb
        qkv_ref[...] = _dot_nt(hb, w_ref[:QKV_W, :]).astype(BF16)
        gate_ref[...] = _dot_nt(hb, w_ref[QKV_W:, :])

    return _rowcall(name, body, [x], [g, wt], [(d, BF16), (QKV_W, BF16), (2 * d, F32)])


def _mixer_fwd(oa, ob, gates, x, wua, wub, wo, name):
    d = x.shape[1]

    def body(oa_ref, ob_ref, gate_ref, x_ref, wua_ref, wub_ref, wo_ref, m_ref, x1_ref):
        ya = _dot_cols(oa_ref[...], wua_ref)
        yb = _dot_cols(ob_ref[...], wub_ref)
        m = _sigmoid(gate_ref[:, :d]) * ya + _sigmoid(gate_ref[:, d:]) * yb
        mb = m.astype(BF16)
        m_ref[...] = mb
        x1_ref[...] = x_ref[...] + _dot(mb, wo_ref[...])

    return _rowcall(name, body, [oa, ob, gates, x], [wua, wub, wo], [(d, BF16), (d, F32)])


def _ff1_fwd(x1, g, w1, name, side=None):
    _, d, nq = w1.shape
    dff = 4 * nq

    def body(x_ref, g_ref, w_ref, h_ref, r_ref, a_ref):
        hb = _rms(x_ref[...], g_ref[...]).astype(BF16)
        h_ref[...] = hb
        r = jnp.maximum(_dot_cols(hb, w_ref), 0.0)
        r_ref[...] = r.astype(BF16)
        a_ref[...] = jnp.square(r).astype(BF16)

    return _rowcall(name, body, [x1], [g, w1], [(d, BF16), (dff, BF16), (dff, BF16)], side=side)


def _ff2_fwd(a, x1, w2, name):
    d = x1.shape[1]

    def body(a_ref, x_ref, w_ref, o_ref):
        o_ref[...] = x_ref[...] + _dot(a_ref[...], w_ref[...])

    return _rowcall(name, body, [a, x1], [w2], [(d, F32)])[0]


def _ple_fwd(p, x2, g, wpe, wpg, name):
    d = x2.shape[1]

    def body(p_ref, x_ref, g_ref, wpe_ref, wpg_ref, pb_ref, h_ref, pe_ref, gt_ref, x3_ref):
        pb = p_ref[...].astype(BF16)
        pb_ref[...] = pb
        pe = _dot_cols(pb, wpe_ref)
        x = x_ref[...]
        hb = _rms(x, g_ref[...]).astype(BF16)
        h_ref[...] = hb
        gt = _dot(hb, wpg_ref[...])
        pe_ref[...] = pe
        gt_ref[...] = gt
        x3_ref[...] = x + pe * _sigmoid(gt)

    return _rowcall(name, body, [p, x2], [g, wpe, wpg],
                    [(p.shape[1], BF16), (d, BF16), (d, F32), (d, F32), (d, F32)])


def _pair_stack(t, lane):
    zero = jnp.zeros_like(t)
    return jnp.concatenate([jnp.where(lane < HEAD_DIM, t, zero), jnp.where(lane >= HEAD_DIM, t, zero)], axis=0)


def _sb_rel():
    row = lax.broadcasted_iota(jnp.int32, (2 * TQ, BK), 0)
    row = jnp.where(row >= TQ, row - TQ, row)
    col = lax.broadcasted_iota(jnp.int32, (2 * TQ, BK), 1)
    return col - row


def _split_dot(x, m01, two_pass=True):
    hi = x.astype(BF16)
    if not two_pass:
        return _dot(hi, m01)
    lo = (x - hi.astype(F32)).astype(BF16)
    return _dot(hi, m01) + _dot(lo, m01)


def _sb_scores(qs, k, mask):
    z = _dot_nt(qs, k)
    lb = jnp.minimum(z, 0.0) - jnp.log(1.0 + jnp.exp(-jnp.abs(z)))
    lm = lb - z
    return lb, lm if mask is None else jnp.where(mask, lm, 0.0)


SB_STRAIGHT = 3
SB_WIDE = SB_STRAIGHT * BK
SB_QB = 2
SWA_QB = 4


def _sb_wide_consts():
    j = np.arange(BK)[:, None]
    s = np.arange(BK)[None, :]
    ones = np.ones((BK, BK), np.float32)
    as_bf16 = lambda m: jnp.asarray(np.concatenate([m, ones], axis=1).astype(np.float32), dtype=BF16)
    return as_bf16(j > s), as_bf16(j <= s), as_bf16(j < s)


def _wide_sums(x, m01, suffix, two_pass=True):
    parts = [_split_dot(x[:, b * BK:(b + 1) * BK], m01, two_pass) for b in range(SB_STRAIGHT)]
    order = range(SB_STRAIGHT - 1, -1, -1) if suffix else range(SB_STRAIGHT)
    out = [None] * SB_STRAIGHT
    carry = None
    for b in order:
        out[b] = parts[b][:, :BK] if carry is None else parts[b][:, :BK] + carry
        carry = parts[b][:, BK:] if carry is None else carry + parts[b][:, BK:]
    return jnp.concatenate(out, axis=1), carry


def _side_refs(ex, rest, n_out):
    n_in = len(ex.arrays) if ex else 0
    n_alias = len(ex.aliased) if ex else 0
    ins, rest = rest[:n_in], rest[n_in:]
    outs, rest = rest[:n_out], rest[n_out:]
    return ins, outs, rest[:n_alias], rest[n_alias:]


def _side_specs(ex, n_in, n_out):
    if ex is None:
        return [], [], [], {}, []
    any_spec = pl.BlockSpec(memory_space=pl.ANY)
    return ([any_spec] * len(ex.arrays),
            [jax.ShapeDtypeStruct(ex.arrays[a].shape, ex.arrays[a].dtype) for a in ex.aliased],
            [any_spec] * len(ex.aliased), {n_in + a: n_out + o for o, a in enumerate(ex.aliased)},
            [pltpu.SemaphoreType.DMA(ex.sems), pltpu.SemaphoreType.DMA(ex.sems)])


def _sb_fwd(qkv, name, side=None):
    s = qkv.shape[0]
    nq = s // TQ
    npair = SB_W // LANES
    sufw = _sb_wide_consts()[0]

    def body(q_ref, k_ref, v_ref, sufw_ref, *rest):
        side_in, (o_ref, lt_ref, nb_ref, a_ref, sg_ref), side_out, scratch = _side_refs(side, rest, 5)
        cf_ref, acc_ref = scratch[:2]
        step_id = pl.program_id(1)
        if side is not None:
            @pl.when((pl.program_id(0) == 0) & (step_id == 0))
            def _():
                side.start(side_in, side_out, *scratch[2:])
        lane = lax.broadcasted_iota(jnp.int32, (TQ, LANES), 1)
        rel = _sb_rel()
        blocks = [step_id * SB_QB + b for b in range(SB_QB)]
        qs = [_pair_stack(q_ref[b * TQ:(b + 1) * TQ, :] * SCALE, lane) for b in range(SB_QB)]

        straight = blocks[0] >= SB_STRAIGHT - 1

        @pl.when(straight)
        def _():
            for b, i in enumerate(blocks):
                w0 = pl.multiple_of((i - (SB_STRAIGHT - 1)) * BK, BK)
                kw = k_ref[pl.ds(w0, SB_WIDE), :]
                lb, lm = _sb_scores(qs[b], kw, None)
                own = rel < 0
                past = SB_WIDE - BK
                lm = jnp.concatenate([lm[:, :past], jnp.where(own, lm[:, past:], 0.0)], axis=1)
                after, total = _wide_sums(lm, sufw_ref[...], True)
                on_past_keys = lambda t: jnp.concatenate([t[:, :past], jnp.where(own, t[:, past:], 0.0)], axis=1)
                a = on_past_keys(jnp.exp(lb + after)).astype(BF16)
                acc_ref[b] = _dot(a, v_ref[pl.ds(w0, SB_WIDE), :])
                cf_ref[b] = total
                a_ref[b] = a
                sg_ref[b] = on_past_keys(jnp.exp(lb)).astype(BF16)

        @pl.when(jnp.logical_not(straight))
        def _():
            cf_ref[...] = jnp.zeros_like(cf_ref)
            acc_ref[...] = jnp.zeros_like(acc_ref)

        for b, i in enumerate(blocks):
            q0 = i * TQ

            def more(c, i=i):
                return (c[0] <= i) & (c[1] > SB_EXHAUSTED)

            def step(c, b=b, i=i, q0=q0):
                k0 = pl.multiple_of((i - c[0]) * BK, BK)
                k = k_ref[pl.ds(k0, BK), :]
                v = v_ref[pl.ds(k0, BK), :]
                mask = rel < (q0 - k0)
                lb, lm = _sb_scores(qs[b], k, mask)
                cs = _split_dot(lm, sufw_ref[...])
                a = jnp.where(mask, jnp.exp(lb + cs[:, :BK] + cf_ref[b]), 0.0)
                acc_ref[b] += _dot(a.astype(BF16), v)
                cf = cf_ref[b] + cs[:, BK:]
                cf_ref[b] = cf
                return c[0] + 1, jnp.max(cf)

            n_blocks, _ = lax.while_loop(
                more, step, (jnp.where(straight, SB_STRAIGHT, 0).astype(jnp.int32), jnp.max(cf_ref[b])))
            o_ref[b * TQ:(b + 1) * TQ, :] = jnp.where(lane < HEAD_DIM, acc_ref[b, :TQ, :],
                                                     acc_ref[b, TQ:, :]).astype(BF16)
            lt_ref[b] = cf_ref[b]
            nb_ref[b] = jnp.full(nb_ref.shape[1:], n_blocks, F32)
        if side is not None:
            @pl.when((pl.program_id(0) == npair - 1) & (step_id == nq // SB_QB - 1))
            def _():
                side.finish(side_in, side_out, *scratch[2:])

    s_in, s_shape, s_out, s_alias, s_sems = _side_specs(side, 4, 5)
    wide = jax.ShapeDtypeStruct((npair, nq, 2 * TQ, SB_WIDE), BF16)
    wide_spec = pl.BlockSpec((None, SB_QB, 2 * TQ, SB_WIDE), lambda j, i: (j, i, 0, 0))
    outs = pl.pallas_call(
        body,
        out_shape=[jax.ShapeDtypeStruct((s, SB_W), BF16), jax.ShapeDtypeStruct((npair, nq, 2 * TQ, BK), F32),
                   jax.ShapeDtypeStruct((npair, nq, 8, LANES), F32), wide, wide] + s_shape,
        grid=(npair, nq // SB_QB),
        in_specs=[pl.BlockSpec((SB_QB * TQ, LANES), lambda j, i: (i, j)),
                  pl.BlockSpec((s, LANES), lambda j, i: (0, npair + j)),
                  pl.BlockSpec((s, LANES), lambda j, i: (0, 2 * npair + j)),
                  pl.BlockSpec(sufw.shape, lambda j, i: (0, 0))] + s_in,
        out_specs=[pl.BlockSpec((SB_QB * TQ, LANES), lambda j, i: (i, j)),
                   pl.BlockSpec((None, SB_QB, 2 * TQ, BK), lambda j, i: (j, i, 0, 0)),
                   pl.BlockSpec((None, SB_QB, 8, LANES), lambda j, i: (j, i, 0, 0)), wide_spec, wide_spec] + s_out,
        input_output_aliases=s_alias,
        scratch_shapes=[pltpu.VMEM((SB_QB, 2 * TQ, BK), F32), pltpu.VMEM((SB_QB, 2 * TQ, LANES), F32)] + s_sems,
        compiler_params=_params(2), name=name)(qkv, qkv, qkv, sufw, *(side.arrays if side else ()))
    return outs[:5], outs[5:]


def _sb_bwd(qkv, lt, nb, a_wide, sg_wide, doa, name, side=None):
    s = qkv.shape[0]
    nq = s // TQ
    npair = SB_W // LANES
    _, prew, prexw = _sb_wide_consts()

    def body(q_ref, k_ref, v_ref, lt_ref, nb_ref, do_ref, prew_ref, prexw_ref, a_ref, sg_ref, *rest):
        side_in, (dq_ref, dk_out, dv_out), side_out, scratch = _side_refs(side, rest, 3)
        cp_ref, ce_ref, dqa_ref, dk_ref, dv_ref = scratch[:5]
        sems = scratch[5:]
        step_id = pl.program_id(1)
        if side is not None:
            @pl.when((pl.program_id(0) == 0) & (step_id == 0))
            def _():
                side.start(side_in, side_out, *sems)
        lane = lax.broadcasted_iota(jnp.int32, (TQ, LANES), 1)
        rel = _sb_rel()
        blocks = [step_id * SB_QB + b for b in range(SB_QB)]
        rows = [slice(b * TQ, (b + 1) * TQ) for b in range(SB_QB)]
        qs = [_pair_stack(q_ref[rows[b], :] * SCALE, lane) for b in range(SB_QB)]
        dos = [_pair_stack(do_ref[rows[b], :], lane) for b in range(SB_QB)]
        n_blocks = [jnp.clip(jnp.max(nb_ref[b]).astype(jnp.int32), 1, i + 1) for b, i in enumerate(blocks)]
        first = [i + 1 - n for i, n in zip(blocks, n_blocks)]

        @pl.when(step_id == 0)
        def _():
            dk_ref[...] = jnp.zeros_like(dk_ref)
            dv_ref[...] = jnp.zeros_like(dv_ref)

        straight = n_blocks[0] == SB_STRAIGHT
        for n in n_blocks[1:]:
            straight = straight & (n == SB_STRAIGHT)

        @pl.when(straight)
        def _():
            for b in range(SB_QB):
                w0 = pl.multiple_of(first[b] * BK, BK)
                kw = k_ref[pl.ds(w0, SB_WIDE), :]
                vw = v_ref[pl.ds(w0, SB_WIDE), :]
                a = a_ref[b]
                e = a.astype(F32) * _dot_nt(dos[b], vw)
                big_e, _ = _wide_sums(e, prexw_ref[...], False, two_pass=False)
                dz = (e - sg_ref[b].astype(F32) * (e + big_e)).astype(BF16)
                dk_ref[pl.ds(w0, SB_WIDE), :] += _dot_tn(dz, qs[b])
                dv_ref[pl.ds(w0, SB_WIDE), :] += _dot_tn(a, dos[b])
                dqa_ref[b] = _dot(dz, kw)

        @pl.when(jnp.logical_not(straight))
        def _():
            cp_ref[...] = jnp.zeros_like(cp_ref)
            ce_ref[...] = jnp.zeros_like(ce_ref)
            dqa_ref[...] = jnp.zeros_like(dqa_ref)
            for b, i in enumerate(blocks):
                q0 = i * TQ

                def step(it, carry, b=b, q0=q0):
                    k0 = pl.multiple_of((first[b] + it) * BK, BK)
                    k = k_ref[pl.ds(k0, BK), :]
                    v = v_ref[pl.ds(k0, BK), :]
                    mask = rel < (q0 - k0)
                    lb, lm = _sb_scores(qs[b], k, mask)
                    cs = _split_dot(lm, prew_ref[...])
                    a = jnp.where(mask, jnp.exp(lb + (lt_ref[b] - (cs[:, :BK] + cp_ref[b]))), 0.0)
                    e = a * _dot_nt(dos[b], v)
                    ce = _split_dot(e, prexw_ref[...], two_pass=False)
                    big_e = ce[:, :BK] + ce_ref[b]
                    dz = jnp.where(mask, e - jnp.exp(lb) * (e + big_e), 0.0).astype(BF16)
                    dk_ref[pl.ds(k0, BK), :] += _dot_tn(dz, qs[b])
                    dv_ref[pl.ds(k0, BK), :] += _dot_tn(a.astype(BF16), dos[b])
                    dqa_ref[b] += _dot(dz, k)
                    cp_ref[b] += cs[:, BK:]
                    ce_ref[b] += ce[:, BK:]
                    return carry

                lax.fori_loop(0, n_blocks[b], step, 0)

        for b in range(SB_QB):
            dq = jnp.where(lane < HEAD_DIM, dqa_ref[b, :TQ, :], dqa_ref[b, TQ:, :])
            dq_ref[rows[b], :] = (dq * SCALE).astype(BF16)

        @pl.when(step_id == nq // SB_QB - 1)
        def _():
            dk_out[...] = dk_ref[...].astype(BF16)
            dv_out[...] = dv_ref[...].astype(BF16)

        if side is not None:
            @pl.when((pl.program_id(0) == npair - 1) & (step_id == nq // SB_QB - 1))
            def _():
                side.finish(side_in, side_out, *sems)

    s_in, s_shape, s_out, s_alias, s_sems = _side_specs(side, 10, 3)
    wide_spec = pl.BlockSpec((None, SB_QB, 2 * TQ, SB_WIDE), lambda j, i: (j, i, 0, 0))
    outs = pl.pallas_call(
        body,
        out_shape=[jax.ShapeDtypeStruct((s, SB_W), BF16)] * 3 + s_shape,
        grid=(npair, nq // SB_QB),
        in_specs=[pl.BlockSpec((SB_QB * TQ, LANES), lambda j, i: (i, j)),
                  pl.BlockSpec((s, LANES), lambda j, i: (0, npair + j)),
                  pl.BlockSpec((s, LANES), lambda j, i: (0, 2 * npair + j)),
                  pl.BlockSpec((None, SB_QB, 2 * TQ, BK), lambda j, i: (j, i, 0, 0)),
                  pl.BlockSpec((None, SB_QB, 8, LANES), lambda j, i: (j, i, 0, 0)),
                  pl.BlockSpec((SB_QB * TQ, LANES), lambda j, i: (i, j)),
                  pl.BlockSpec(prew.shape, lambda j, i: (0, 0)),
                  pl.BlockSpec(prexw.shape, lambda j, i: (0, 0)), wide_spec, wide_spec] + s_in,
        out_specs=[pl.BlockSpec((SB_QB * TQ, LANES), lambda j, i: (i, j)),
                   pl.BlockSpec((s, LANES), lambda j, i: (0, j)),
                   pl.BlockSpec((s, LANES), lambda j, i: (0, j))] + s_out,
        input_output_aliases=s_alias,
        scratch_shapes=[pltpu.VMEM((SB_QB, 2 * TQ, BK), F32), pltpu.VMEM((SB_QB, 2 * TQ, BK), F32),
                        pltpu.VMEM((SB_QB, 2 * TQ, LANES), F32), pltpu.VMEM((s, LANES), F32),
                        pltpu.VMEM((s, LANES), F32)] + s_sems,
        compiler_params=_params(2), name=name)(qkv, qkv, qkv, lt, nb, doa, prew, prexw, a_wide, sg_wide,
                                               *(side.arrays if side else ()))
    return outs[0], outs[1], outs[2], outs[3:]


def _bucket_table():
    i = np.arange(TQ)[:, None]
    j = np.arange(2 * BK)[None, :]
    dist = np.maximum(TQ + i - j, 0)
    max_exact = N_BUCKETS // 2
    df = np.maximum(dist, 1).astype(np.float32)
    large = max_exact + (np.log(df / np.float32(max_exact)) / np.float32(math.log(MAX_DISTANCE / max_exact))
                         * np.float32(N_BUCKETS - max_exact)).astype(np.int32)
    large = np.minimum(large, N_BUCKETS - 1)
    return np.where(dist < max_exact, dist, large).astype(np.int32)


def _swa_align_in(t, lane, g):
    tf = t.astype(F32)
    tr = pltpu.roll(tf, HEAD_DIM, 1)
    gmask = (lane >= HEAD_DIM) == (g == 1)
    top = jnp.where(gmask, jnp.where(g == 0, tf, tr), 0.0)
    bot = jnp.where(gmask, jnp.where(g == 1, tf, tr), 0.0)
    return jnp.concatenate([top, bot], axis=0).astype(BF16)


def _swa_align_out(t, lane, g):
    top, bot = t[:TQ, :], t[TQ:, :]
    top = jnp.where(g == 0, top, pltpu.roll(top, HEAD_DIM, 1))
    bot = jnp.where(g == 1, bot, pltpu.roll(bot, HEAD_DIM, 1))
    return jnp.where(lane < HEAD_DIM, top, bot)


def _swa_bias(bias_ref, bucket_ref, rb_ref, j):
    dist = TQ + lax.broadcasted_iota(jnp.int32, (TQ, 2 * BK), 0) - lax.broadcasted_iota(jnp.int32, (TQ, 2 * BK), 1)
    window = (dist >= 0) & (dist < WINDOW)
    for hh in range(2):
        def add(b, acc):
            return acc + jnp.where(bucket_ref[...] == b, rb_ref[b, 2 * j + hh], 0.0)
        bias = lax.fori_loop(0, N_BUCKETS, add, jnp.zeros((TQ, 2 * BK), F32))
        bias_ref[hh * TQ:(hh + 1) * TQ, :] = jnp.where(window, bias, NEG)


def _swa_probs(qs, k2, bias, own_block, sink_ref, i, j):
    s = _dot_nt(qs, k2) + bias
    s = jnp.where(own_block | (i > 0), s, NEG)
    row1 = lax.broadcasted_iota(jnp.int32, (2 * TQ, 1), 0)
    sink = jnp.where(row1 < TQ, sink_ref[2 * j], sink_ref[2 * j + 1])
    m = jnp.maximum(jnp.max(s, axis=1, keepdims=True), sink)
    e = jnp.exp(s - m)
    es = jnp.exp(sink - m)
    inv = 1.0 / (jnp.sum(e, axis=1, keepdims=True) + es)
    return e * inv, es * inv


def _swa_kv(ref, i):
    prev = pl.multiple_of(jnp.maximum(i - 1, 0) * BK, BK)
    cur = pl.multiple_of(i * BK, BK)
    return jnp.concatenate([ref[pl.ds(prev, BK), :], ref[pl.ds(cur, BK), :]], axis=0), prev, cur


def _swa_fwd(qkv, sinks, rel_bias, name):
    s = qkv.shape[0]
    nq = s // TQ
    npair = SW_QW // LANES
    qcol = 3 * SB_W // LANES
    bucket = jnp.asarray(_bucket_table())

    def body(q_ref, k_ref, v_ref, bucket_ref, sink_ref, rb_ref, o_ref, bias_ref):
        j = pl.program_id(0)
        step = pl.program_id(1)
        g = j // 2
        lane = lax.broadcasted_iota(jnp.int32, (TQ, LANES), 1)

        @pl.when(step == 0)
        def _():
            _swa_bias(bias_ref, bucket_ref, rb_ref, j)

        own_block = lax.broadcasted_iota(jnp.int32, (2 * TQ, 2 * BK), 1) >= BK
        for b in range(SWA_QB):
            i = step * SWA_QB + b
            rows = slice(b * TQ, (b + 1) * TQ)
            qs = _swa_align_in(q_ref[rows, :] * SCALE, lane, g)
            k2, _, _ = _swa_kv(k_ref, i)
            v2, _, _ = _swa_kv(v_ref, i)
            pr, _ = _swa_probs(qs, k2, bias_ref[...], own_block, sink_ref, i, j)
            o_ref[rows, :] = _swa_align_out(_dot(pr.astype(BF16), v2), lane, g).astype(BF16)

    assert nq % SWA_QB == 0
    return pl.pallas_call(
        body, out_shape=jax.ShapeDtypeStruct((s, SW_QW), BF16), grid=(npair, nq // SWA_QB),
        in_specs=[pl.BlockSpec((SWA_QB * TQ, LANES), lambda j, i: (i, qcol + j)),
                  pl.BlockSpec((s, LANES), lambda j, i: (0, qcol + npair)),
                  pl.BlockSpec((s, LANES), lambda j, i: (0, qcol + npair + 1)),
                  pl.BlockSpec((TQ, 2 * BK), lambda j, i: (0, 0)),
                  pl.BlockSpec(memory_space=pltpu.SMEM),
                  pl.BlockSpec(memory_space=pltpu.SMEM)],
        out_specs=pl.BlockSpec((SWA_QB * TQ, LANES), lambda j, i: (i, j)),
        scratch_shapes=[pltpu.VMEM((2 * TQ, 2 * BK), F32)],
        compiler_params=_params(2), name=name)(qkv, qkv, qkv, bucket, sinks, rel_bias)


def _swa_bwd(qkv, ob, dob, sinks, rel_bias, name):
    s = qkv.shape[0]
    nq = s // TQ
    npair = SW_QW // LANES
    qcol = 3 * SB_W // LANES
    bucket = jnp.asarray(_bucket_table())

    def body(q_ref, k_ref, v_ref, o_ref, do_ref, bucket_ref, sink_ref, rb_ref,
             dq_ref, dkv_ref, dsink_ref, drel_ref, bias_ref, dsacc_ref, dk_ref, dv_ref):
        j = pl.program_id(0)
        step = pl.program_id(1)
        g = j // 2
        lane = lax.broadcasted_iota(jnp.int32, (TQ, LANES), 1)
        row8 = lax.broadcasted_iota(jnp.int32, (SW_HEADS, LANES), 0)
        lane8 = lax.broadcasted_iota(jnp.int32, (SW_HEADS, LANES), 1)

        @pl.when((step == 0) & (j == 0))
        def _():
            dk_ref[...] = jnp.zeros_like(dk_ref)
            dv_ref[...] = jnp.zeros_like(dv_ref)
            dsink_ref[...] = jnp.zeros_like(dsink_ref)
            drel_ref[...] = jnp.zeros_like(drel_ref)

        @pl.when(step == 0)
        def _():
            _swa_bias(bias_ref, bucket_ref, rb_ref, j)
            dsacc_ref[...] = jnp.zeros_like(dsacc_ref)

        own_block = lax.broadcasted_iota(jnp.int32, (2 * TQ, 2 * BK), 1) >= BK
        ds_sum = jnp.zeros(dsacc_ref.shape, F32)
        dsink = jnp.zeros((SW_HEADS, LANES), F32)
        for b in range(SWA_QB):
            i = step * SWA_QB + b
            rows = slice(b * TQ, (b + 1) * TQ)
            qs = _swa_align_in(q_ref[rows, :] * SCALE, lane, g)
            do = do_ref[rows, :]
            dos = _swa_align_in(do, lane, g)
            dof = do.astype(F32) * o_ref[rows, :].astype(F32)
            d0 = jnp.sum(jnp.where(lane < HEAD_DIM, dof, 0.0), axis=1, keepdims=True)
            d1 = jnp.sum(jnp.where(lane >= HEAD_DIM, dof, 0.0), axis=1, keepdims=True)
            delta = jnp.concatenate([d0, d1], axis=0)
            k2, prev, cur = _swa_kv(k_ref, i)
            v2, _, _ = _swa_kv(v_ref, i)
            pr, psink = _swa_probs(qs, k2, bias_ref[...], own_block, sink_ref, i, j)
            ds = pr * (_dot_nt(dos, v2) - delta)
            ds_sum = ds_sum + ds
            sd = psink * delta
            ds0 = -jnp.sum(sd[:TQ, :], axis=0, keepdims=True)
            ds1 = -jnp.sum(sd[TQ:, :], axis=0, keepdims=True)
            dsink = dsink + jnp.where(row8 == 2 * j, ds0, jnp.where(row8 == 2 * j + 1, ds1, 0.0))
            dsb = ds.astype(BF16)
            dq_ref[rows, :] = _swa_align_out(_dot(dsb, k2) * SCALE, lane, g).astype(BF16)
            dk2 = _dot_tn(dsb, qs)
            dv2 = _dot_tn(pr.astype(BF16), dos)
            dk_ref[pl.ds(prev, BK), :] += dk2[:BK, :]
            dk_ref[pl.ds(cur, BK), :] += dk2[BK:, :]
            dv_ref[pl.ds(prev, BK), :] += dv2[:BK, :]
            dv_ref[pl.ds(cur, BK), :] += dv2[BK:, :]
        dsacc_ref[...] += ds_sum
        dsink_ref[...] += dsink

        @pl.when(step == nq // SWA_QB - 1)
        def _():
            for hh in range(2):
                def red(b, acc):
                    val = _sum_all(jnp.where(bucket_ref[...] == b, dsacc_ref[hh * TQ:(hh + 1) * TQ, :], 0.0))
                    return jnp.where((row8 == 2 * j + hh) & (lane8 == b), val, acc)
                drel_ref[...] += lax.fori_loop(0, N_BUCKETS, red, jnp.zeros((SW_HEADS, LANES), F32))

        @pl.when((step == nq // SWA_QB - 1) & (j == npair - 1))
        def _():
            dkv_ref[:, :LANES] = dk_ref[...].astype(BF16)
            dkv_ref[:, LANES:] = dv_ref[...].astype(BF16)

    whole = lambda j, i: (0, 0)
    return pl.pallas_call(
        body,
        out_shape=[jax.ShapeDtypeStruct((s, SW_QW), BF16), jax.ShapeDtypeStruct((s, 2 * LANES), BF16),
                   jax.ShapeDtypeStruct((SW_HEADS, LANES), F32), jax.ShapeDtypeStruct((SW_HEADS, LANES), F32)],
        grid=(npair, nq // SWA_QB),
        in_specs=[pl.BlockSpec((SWA_QB * TQ, LANES), lambda j, i: (i, qcol + j)),
                  pl.BlockSpec((s, LANES), lambda j, i: (0, qcol + npair)),
                  pl.BlockSpec((s, LANES), lambda j, i: (0, qcol + npair + 1)),
                  pl.BlockSpec((SWA_QB * TQ, LANES), lambda j, i: (i, j)),
                  pl.BlockSpec((SWA_QB * TQ, LANES), lambda j, i: (i, j)),
                  pl.BlockSpec((TQ, 2 * BK), whole),
                  pl.BlockSpec(memory_space=pltpu.SMEM),
                  pl.BlockSpec(memory_space=pltpu.SMEM)],
        out_specs=[pl.BlockSpec((SWA_QB * TQ, LANES), lambda j, i: (i, j)),
                   pl.BlockSpec((s, 2 * LANES), whole),
                   pl.BlockSpec((SW_HEADS, LANES), whole), pl.BlockSpec((SW_HEADS, LANES), whole)],
        scratch_shapes=[pltpu.VMEM((2 * TQ, 2 * BK), F32), pltpu.VMEM((2 * TQ, 2 * BK), F32),
                        pltpu.VMEM((s, LANES), F32), pltpu.VMEM((s, LANES), F32)],
        compiler_params=_params(2), name=name)(qkv, qkv, qkv, ob, dob, bucket, sinks, rel_bias)


def _acc_init(i, *refs):
    @pl.when(i == 0)
    def _():
        for r in refs:
            r[...] = jnp.zeros_like(r)


def _loss_bwd(x3, target, g, name):
    d = x3.shape[1]

    def body(x_ref, t_ref, g_ref, dx_ref, dg_ref, loss_ref):
        _acc_init(pl.program_id(0), dg_ref, loss_ref)
        x = x_ref[...]
        gv = g_ref[...]
        diff = _rms(x, gv) - t_ref[...]
        loss_ref[...] += 0.5 * jnp.sum(jnp.mean(jnp.square(diff), axis=-1, keepdims=True), axis=0, keepdims=True)
        dx, dg = _rms_bwd(diff * (1.0 / d), x, gv)
        dx_ref[...] = dx
        dg_ref[...] += dg

    return _rowcall(name, body, [x3, target], [g], [(d, F32)], [((1, d), F32), ((1, LANES), F32)])


def _ple_bwd(dx3, pe, gt, x2, g, wpg, name, side=None):
    d = x2.shape[1]

    def body(dx3_ref, pe_ref, gt_ref, x_ref, g_ref, w_ref, dpe_ref, dgt_ref, dx2_ref, dg_ref):
        _acc_init(pl.program_id(0), dg_ref)
        dx3 = dx3_ref[...]
        sg = _sigmoid(gt_ref[...])
        dpe_ref[...] = (dx3 * sg).astype(BF16)
        dgt = (dx3 * pe_ref[...] * sg * (1.0 - sg)).astype(BF16)
        dgt_ref[...] = dgt
        dx, dg = _rms_bwd(_dot_nt(dgt, w_ref[...]), x_ref[...], g_ref[...])
        dx2_ref[...] = dx3 + dx
        dg_ref[...] += dg

    return _rowcall(name, body, [dx3, pe, gt, x2], [g, wpg], [(d, BF16), (d, BF16), (d, F32)], [((1, d), F32)],
                    side=side)


def _ff2_bwd(dx2, r, w2, name):
    d = dx2.shape[1]
    dff = r.shape[1]

    def body(dx_ref, r_ref, w_ref, du_ref, dxb_ref):
        dxb = dx_ref[...].astype(BF16)
        dxb_ref[...] = dxb
        du_ref[...] = (_dot_nt(dxb, w_ref[...]) * (2.0 * r_ref[...].astype(F32))).astype(BF16)

    return _rowcall(name, body, [dx2, r], [w2], [(dff, BF16), (d, BF16)])


def _ff1_bwd(du, dx2, x1, g, w1, name, side=None):
    d = x1.shape[1]

    def body(du_ref, dx2_ref, x_ref, g_ref, w_ref, dx1_ref, dx1b_ref, dg_ref):
        _acc_init(pl.program_id(0), dg_ref)
        dx, dg = _rms_bwd(_dot_cols_t(du_ref[...], w_ref), x_ref[...], g_ref[...])
        dx1 = dx2_ref[...] + dx
        dx1_ref[...] = dx1
        dx1b_ref[...] = dx1.astype(BF16)
        dg_ref[...] += dg

    return _rowcall(name, body, [du, dx2, x1], [g, w1], [(d, F32), (d, BF16)], [((1, d), F32)], side=side)


def _mixer_bwd(dx1b, gates, oa, ob, wo, wua, wub, name, side=None):
    d = dx1b.shape[1]

    def body(dx_ref, gate_ref, oa_ref, ob_ref, wo_ref, wua_ref, wub_ref,
             dya_ref, dyb_ref, dgate_ref, doa_ref, dob_ref):
        dm = _dot_nt(dx_ref[...], wo_ref[...])
        sa = _sigmoid(gate_ref[:, :d])
        sb = _sigmoid(gate_ref[:, d:])
        ya = _dot_cols(oa_ref[...], wua_ref)
        yb = _dot_cols(ob_ref[...], wub_ref)
        dya = (dm * sa).astype(BF16)
        dyb = (dm * sb).astype(BF16)
        dya_ref[...] = dya
        dyb_ref[...] = dyb
        dgate_ref[:, :d] = (dm * ya * sa * (1.0 - sa)).astype(BF16)
        dgate_ref[:, d:] = (dm * yb * sb * (1.0 - sb)).astype(BF16)
        doa_ref[...] = _dot_cols_t(dya, wua_ref).astype(BF16)
        dob_ref[...] = _dot_cols_t(dyb, wub_ref).astype(BF16)

    return _rowcall(name, body, [dx1b, gates, oa, ob], [wo, wua, wub],
                    [(d, BF16), (d, BF16), (2 * d, BF16), (SB_W, BF16), (SW_QW, BF16)], side=side)


def _inproj_bwd(pieces, dx1, x, g, wt, name, side=None):
    d = x.shape[1]
    n = len(pieces)
    offsets = [sum(pc.shape[1] for pc in pieces[:p]) for p in range(n + 1)]

    def body(*refs):
        dx1_ref, x_ref, g_ref, w_ref, dx_ref, dg_ref = refs[n:]
        _acc_init(pl.program_id(0), dg_ref)
        dh = _dot(refs[0][...], w_ref[:offsets[1], :])
        for p in range(1, n):
            dh = dh + _dot(refs[p][...], w_ref[offsets[p]:offsets[p + 1], :])
        dx, dg = _rms_bwd(dh, x_ref[...], g_ref[...])
        dx_ref[...] = dx1_ref[...] + dx
        dg_ref[...] += dg

    return _rowcall(name, body, list(pieces) + [dx1, x], [g, wt], [(d, F32)], [((1, d), F32)], side=side)


def _tile(n, cap):
    assert n % LANES == 0
    return max(t for t in range(LANES, min(n, cap) + 1, LANES) if n % t == 0)


def _mm_tn(a, b, name, nshard=1):
    s, ka = a.shape
    nb = b.shape[1]
    n = nb // nshard
    ta = _tile(ka, 512)
    tb = _tile(n, 1024)
    per = n // tb

    def body(a_ref, b_ref, o_ref):
        o_ref[...] = _dot_tn(a_ref[...].astype(BF16), b_ref[...].astype(BF16))

    return pl.pallas_call(
        body, out_shape=jax.ShapeDtypeStruct((nshard, ka, n), F32), grid=(nb // tb, ka // ta),
        in_specs=[pl.BlockSpec((s, ta), lambda jb, ia: (0, ia)), pl.BlockSpec((s, tb), lambda jb, ia: (0, jb))],
        out_specs=pl.BlockSpec((None, ta, tb), lambda jb, ia: (jb // per, ia, jb % per)),
        compiler_params=_params(2), name=name)(a, b)


def _mm_tn_pieces(pieces, b, name, side=None):
    s, nb = b.shape
    ta = 256
    n_in = len(pieces) + 1
    tiles = [pc.shape[1] // ta for pc in pieces]
    assert all(pc.shape[1] % ta == 0 for pc in pieces)
    starts = [sum(tiles[:p]) for p in range(len(pieces))]
    tb = _tile(nb, 1024)
    grid = (nb // tb, sum(tiles))

    def body(*refs):
        a_refs, b_ref = refs[:n_in - 1], refs[n_in - 1]
        side_in, (o_ref,), side_out, sems = _side_refs(side, refs[n_in:], 1)
        jb, ia = pl.program_id(0), pl.program_id(1)
        if side is not None:
            @pl.when((jb == 0) & (ia == 0))
            def _():
                side.start(side_in, side_out, *sems)
        for p in range(len(pieces)):
            @pl.when((ia >= starts[p]) & (ia < starts[p] + tiles[p]))
            def _(p=p):
                o_ref[...] = _dot_tn(a_refs[p][...], b_ref[...])
        if side is not None:
            @pl.when((jb == grid[0] - 1) & (ia == grid[1] - 1))
            def _():
                side.finish(side_in, side_out, *sems)

    def piece_spec(p):
        return pl.BlockSpec((s, ta), lambda jb, ia: (0, jnp.clip(ia - starts[p], 0, tiles[p] - 1)))

    s_in, s_shape, s_out, s_alias, s_sems = _side_specs(side, n_in, 1)
    outs = pl.pallas_call(
        body, out_shape=[jax.ShapeDtypeStruct((sum(tiles) * ta, nb), F32)] + s_shape, grid=grid,
        in_specs=[piece_spec(p) for p in range(len(pieces))] + [pl.BlockSpec((s, tb), lambda jb, ia: (0, jb))] + s_in,
        out_specs=[pl.BlockSpec((ta, tb), lambda jb, ia: (ia, jb))] + s_out,
        input_output_aliases=s_alias, scratch_shapes=s_sems,
        compiler_params=_params(2), name=name)(*pieces, b, *(side.arrays if side else ()))
    return outs[0] if side is None else (outs[0], outs[1:])


def _place():
    return lax.axis_index("x"), lax.axis_index("y"), lax.axis_index("c")


def _chip_peer(x, y, k):
    return (x ^ (k >> 1), y ^ (k & 1))


def _row_tile(k, cap=544):
    return max(t for t in range(32, min(k, cap) + 1, 32) if k % t == 0)


def _cast_bf16(w, r, name):
    l, k, n = w.shape
    assert l == 2
    tk = _row_tile(k)

    def body(r_ref, w_ref, o0_ref, o1_ref):
        o0_ref[...] = w_ref[0].astype(BF16)
        o1_ref[...] = w_ref[1].astype(BF16)

    out_spec = pl.BlockSpec((None, tk, n), lambda i, r_ref: (r_ref[0], i, 0))
    return pl.pallas_call(
        body, out_shape=[jax.ShapeDtypeStruct((4, k, n), BF16)] * 2,
        grid_spec=pltpu.PrefetchScalarGridSpec(
            num_scalar_prefetch=1, grid=(k // tk,),
            in_specs=[pl.BlockSpec((l, tk, n), lambda i, r_ref: (0, i, 0))],
            out_specs=[out_spec, out_spec]),
        compiler_params=_params(1), name=name)(r, w)


class _Exchange(NamedTuple):
    arrays: tuple
    aliased: tuple
    sems: tuple
    start: Callable
    finish: Callable


def _all_gather(bufs):
    nt = len(bufs)

    def ici(t, ins, outs, send_sems, recv_sems, x, y, c, r, k):
        return pltpu.make_async_remote_copy(
            src_ref=ins[t].at[r, c], dst_ref=outs[t].at[r, c], send_sem=send_sems.at[t, k - 1],
            recv_sem=recv_sems.at[t, k - 1], device_id=(*_chip_peer(x, y, k), c), device_id_type=MESH)

    def d2d(t, outs, send_sems, recv_sems, x, y, c, r, k, half):
        slab = outs[t].at[r ^ k, half]
        return pltpu.make_async_remote_copy(
            src_ref=slab, dst_ref=slab, send_sem=send_sems.at[t, 2 + k], recv_sem=recv_sems.at[t, 2 + k],
            device_id=(x, y, 1 - c), device_id_type=MESH)

    def start(ins, outs, send_sems, recv_sems):
        x, y, c = _place()
        r = 2 * x + y
        for t in range(nt):
            for k in (1, 2, 3):
                ici(t, ins, outs, send_sems, recv_sems, x, y, c, r, k).start()

    def finish(ins, outs, send_sems, recv_sems):
        x, y, c = _place()
        r = 2 * x + y
        for t in range(nt):
            for k in (1, 2, 3):
                slab = outs[t].at[r ^ k, c]
                pltpu.make_async_remote_copy(
                    src_ref=slab, dst_ref=slab, send_sem=send_sems.at[t, k - 1], recv_sem=recv_sems.at[t, k - 1],
                    device_id=(x, y, 1 - c), device_id_type=MESH).wait_recv()
                d2d(t, outs, send_sems, recv_sems, x, y, c, r, k, c).start()
        for t in range(nt):
            for k in (1, 2, 3):
                d2d(t, outs, send_sems, recv_sems, x, y, c, r, k, 1 - c).wait_recv()
        for t in range(nt):
            for k in (1, 2, 3):
                ici(t, ins, outs, send_sems, recv_sems, x, y, c, r, k).wait_send()
                d2d(t, outs, send_sems, recv_sems, x, y, c, r, k, c).wait_send()

    return _Exchange(tuple(bufs), tuple(range(nt)), (nt, 6), start, finish)


def _run_exchange(name, ex):
    n_in, n_out = len(ex.arrays), len(ex.aliased)

    def body(*refs):
        ins, outs = refs[:n_in], refs[n_in:n_in + n_out]
        send_sems, recv_sems = refs[n_in + n_out:]
        ex.start(ins, outs, send_sems, recv_sems)
        ex.finish(ins, outs, send_sems, recv_sems)

    any_spec = pl.BlockSpec(memory_space=pl.ANY)
    return pl.pallas_call(
        body, out_shape=[jax.ShapeDtypeStruct(ex.arrays[a].shape, ex.arrays[a].dtype) for a in ex.aliased],
        in_specs=[any_spec] * n_in, out_specs=[any_spec] * n_out,
        input_output_aliases={a: o for o, a in enumerate(ex.aliased)},
        scratch_shapes=[pltpu.SemaphoreType.DMA(ex.sems), pltpu.SemaphoreType.DMA(ex.sems)],
        name=name)(*ex.arrays)


def _rs_to_sibling(grads):
    nt = len(grads)
    landing = [lax.empty((4,) + g.shape[2:], F32) for g in grads]

    def copies(ins, outs, send_sems, recv_sems):
        x, y, c = _place()
        return [pltpu.make_async_remote_copy(
            src_ref=ins[t].at[:, 1 - c], dst_ref=outs[t], send_sem=send_sems.at[t], recv_sem=recv_sems.at[t],
            device_id=(x, y, 1 - c), device_id_type=MESH) for t in range(nt)]

    def start(ins, outs, send_sems, recv_sems):
        for cp in copies(ins, outs, send_sems, recv_sems):
            cp.start()

    def finish(ins, outs, send_sems, recv_sems):
        for cp in copies(ins, outs, send_sems, recv_sems):
            cp.wait()

    return _Exchange(tuple(grads) + tuple(landing), tuple(range(nt, 2 * nt)), (nt,), start, finish)


def _add_half(g, recv, cr, name):
    _, _, k2, n = g.shape
    tk = _row_tile(k2)

    def body(cr_ref, g_ref, r_ref, sums_ref, mine_ref):
        val = (g_ref[...] + r_ref[...]).astype(BF16)
        sums_ref[...] = val

        @pl.when(pl.program_id(1) == cr_ref[1])
        def _():
            mine_ref[...] = val

    return pl.pallas_call(
        body, out_shape=[jax.ShapeDtypeStruct((4, k2, n), BF16)] * 2,
        grid_spec=pltpu.PrefetchScalarGridSpec(
            num_scalar_prefetch=1, grid=(k2 // tk, 4),
            in_specs=[pl.BlockSpec((None, None, tk, n), lambda i, q, cr_ref: (q, cr_ref[0], i, 0)),
                      pl.BlockSpec((None, tk, n), lambda i, q, cr_ref: (q, i, 0))],
            out_specs=[pl.BlockSpec((None, tk, n), lambda i, q, cr_ref: (q, i, 0)),
                       pl.BlockSpec((None, tk, n), lambda i, q, cr_ref: (cr_ref[1], i, 0))]),
        compiler_params=_params(2), name=name)(cr, g, recv)


def _rs_to_chips(sums, parts):
    nt = len(sums)

    def copies(ins, outs, send_sems, recv_sems):
        x, y, c = _place()
        r = 2 * x + y
        return [pltpu.make_async_remote_copy(
            src_ref=ins[t].at[r ^ k], dst_ref=outs[t].at[r], send_sem=send_sems.at[t, k - 1],
            recv_sem=recv_sems.at[t, k - 1], device_id=(*_chip_peer(x, y, k), c), device_id_type=MESH)
            for t in range(nt) for k in (1, 2, 3)]

    def start(ins, outs, send_sems, recv_sems):
        for cp in copies(ins, outs, send_sems, recv_sems):
            cp.start()

    def finish(ins, outs, send_sems, recv_sems):
        for cp in copies(ins, outs, send_sems, recv_sems):
            cp.wait()

    return _Exchange(tuple(sums) + tuple(parts), tuple(range(nt, 2 * nt)), (nt, 3), start, finish)


def _sum4(parts, cr, name):
    _, k2, n = parts.shape
    tk = _row_tile(k2)

    def body(cr_ref, p_ref, o_ref):
        p = p_ref[...].astype(F32)
        o_ref[...] = ((p[0] + p[1]) + p[2]) + p[3]

    return pl.pallas_call(
        body, out_shape=jax.ShapeDtypeStruct((2, k2, n), F32),
        grid_spec=pltpu.PrefetchScalarGridSpec(
            num_scalar_prefetch=1, grid=(k2 // tk,),
            in_specs=[pl.BlockSpec((4, tk, n), lambda i, cr_ref: (0, i, 0))],
            out_specs=pl.BlockSpec((None, tk, n), lambda i, cr_ref: (cr_ref[0], i, 0))),
        compiler_params=_params(1), name=name)(cr, parts)


def _exchange_halves(both):
    nt = len(both)

    def copies(ins, outs, send_sems, recv_sems):
        x, y, c = _place()
        return [pltpu.make_async_remote_copy(
            src_ref=ins[t].at[c], dst_ref=outs[t].at[c], send_sem=send_sems.at[t], recv_sem=recv_sems.at[t],
            device_id=(x, y, 1 - c), device_id_type=MESH) for t in range(nt)]

    def start(ins, outs, send_sems, recv_sems):
        for cp in copies(ins, outs, send_sems, recv_sems):
            cp.start()

    def finish(ins, outs, send_sems, recv_sems):
        for cp in copies(ins, outs, send_sems, recv_sems):
            cp.wait()

    return _Exchange(tuple(both), tuple(range(nt)), (nt,), start, finish)


def _adamw_math(w, g, m, v):
    m = ADAM_B1 * m + (1.0 - ADAM_B1) * g
    v = ADAM_B2 * v + (1.0 - ADAM_B2) * jnp.square(g)
    m_hat = m / (1.0 - ADAM_B1 ** ADAM_STEP)
    v_hat = v / (1.0 - ADAM_B2 ** ADAM_STEP)
    delta = -ADAM_LR * (m_hat / (jnp.sqrt(v_hat) + ADAM_EPS) + ADAM_WD * w)
    return delta, m, v


def _adamw(w, m, v, g0, g1, name):
    _, k, n = w.shape
    tk = _row_tile(k)
    nk = k // tk

    def body(w_ref, m_ref, v_ref, g0_ref, g1_ref, grad_ref, delta_ref, nm_ref, nv_ref):
        g = jnp.where(pl.program_id(0) == 0, g0_ref[...], g1_ref[...])
        delta, nm, nv = _adamw_math(w_ref[...], g, m_ref[...], v_ref[...])
        grad_ref[...] = g
        delta_ref[...] = delta
        nm_ref[...] = nm
        nv_ref[...] = nv

    lay = pl.BlockSpec((None, tk, n), lambda a, i: (a, i, 0))
    g0_spec = pl.BlockSpec((tk, n), lambda a, i: (jnp.where(a == 0, i, nk - 1), 0))
    g1_spec = pl.BlockSpec((tk, n), lambda a, i: (jnp.where(a == 1, i, 0), 0))
    return pl.pallas_call(
        body, out_shape=[jax.ShapeDtypeStruct(w.shape, F32)] * 4, grid=(2, nk),
        in_specs=[lay, lay, lay, g0_spec, g1_spec], out_specs=[lay] * 4,
        compiler_params=_params(2), name=name)(w, m, v, g0, g1)


def _small_allreduce_adamw(gpart, w, m, v):
    shape = gpart.shape

    def body(g_ref, w_ref, m_ref, v_ref, gsum_ref, delta_ref, nm_ref, nv_ref, recv_ref, send_sems, recv_sems):
        x, y, c = _place()
        me = 4 * x + 2 * y + c
        recv_ref[me] = g_ref[...]
        cps = []
        for k in range(1, 8):
            peer = (x ^ (k >> 2), y ^ ((k >> 1) & 1), c ^ (k & 1))
            cp = pltpu.make_async_remote_copy(
                src_ref=g_ref, dst_ref=recv_ref.at[me], send_sem=send_sems.at[k - 1], recv_sem=recv_sems.at[k - 1],
                device_id=peer, device_id_type=MESH)
            cp.start()
            cps.append(cp)
        for cp in cps:
            cp.wait()
        g = recv_ref[0]
        for dev in range(1, 8):
            g = g + recv_ref[dev]
        delta, nm, nv = _adamw_math(w_ref[...], g, m_ref[...], v_ref[...])
        gsum_ref[...] = g
        delta_ref[...] = delta
        nm_ref[...] = nm
        nv_ref[...] = nv

    vm = pl.BlockSpec(memory_space=pltpu.VMEM)
    return pl.pallas_call(
        body, out_shape=[jax.ShapeDtypeStruct(shape, F32)] * 4, in_specs=[vm] * 4, out_specs=[vm] * 4,
        scratch_shapes=[pltpu.VMEM((8,) + shape, F32), pltpu.SemaphoreType.DMA((7,)), pltpu.SemaphoreType.DMA((7,))],
        name="small_allreduce_adamw")(gpart, w, m, v)


BIG = ("w_in", "w_up_a", "w_up_b", "w_o", "w_ff1", "w_ff2", "w_pe", "w_pg")
COL_SHARDED = ("w_in", "w_up_a", "w_up_b", "w_ff1", "w_pe")
ROW_SHARDED = ("w_o", "w_ff2", "w_pg")
SMALL_ROWS = 16


def _pack_small(g_mix, g_mlp, g_pe, g_final, sinks, rel_bias, loss=None):
    d = g_final.shape[0]
    row = lambda v: jnp.pad(v.reshape(1, -1), ((0, 0), (0, d - v.size)))
    rows = [g_mix, g_mlp, g_pe, g_final.reshape(1, d),
            jnp.zeros((1, d), F32) if loss is None else row(loss), row(sinks), row(rel_bias)]
    out = jnp.concatenate(rows, axis=0)
    return jnp.pad(out, ((0, SMALL_ROWS - out.shape[0]), (0, 0)))


def _unpack_small(a, sinks_shape, rel_shape):
    return (a[0:2], a[2:4], a[4:6], a[6], a[8, :sinks_shape[0] * sinks_shape[1]].reshape(sinks_shape),
            a[9, :rel_shape[0] * rel_shape[1]].reshape(rel_shape))


def kernel(x, p, w_in, w_up_a, w_up_b, w_o, w_ff1, w_ff2, w_pe, w_pg, g_mix, g_mlp, g_pe, g_final, sinks, rel_bias, loss_target, m_w_in, m_w_up_a, m_w_up_b, m_w_o, m_w_ff1, m_w_ff2, m_w_pe, m_w_pg, m_g_mix, m_g_mlp, m_g_pe, m_g_final, m_sinks, m_rel_bias, v_w_in, v_w_up_a, v_w_up_b, v_w_o, v_w_ff1, v_w_ff2, v_w_pe, v_w_pg, v_g_mix, v_g_mlp, v_g_pe, v_g_final, v_sinks, v_rel_bias):
    depth = w_in.shape[0]
    assert depth == 2
    x0 = x[0]
    target = loss_target[0]
    d = x0.shape[1]
    wl = dict(w_in=w_in, w_up_a=w_up_a, w_up_b=w_up_b, w_o=w_o, w_ff1=w_ff1, w_ff2=w_ff2, w_pe=w_pe, w_pg=w_pg)
    ml = dict(w_in=m_w_in, w_up_a=m_w_up_a, w_up_b=m_w_up_b, w_o=m_w_o, w_ff1=m_w_ff1, w_ff2=m_w_ff2, w_pe=m_w_pe, w_pg=m_w_pg)
    vl = dict(w_in=v_w_in, w_up_a=v_w_up_a, w_up_b=v_w_up_b, w_o=v_w_o, w_ff1=v_w_ff1, w_ff2=v_w_ff2, w_pe=v_w_pe, w_pg=v_w_pg)
    c_idx = lax.axis_index("c").astype(jnp.int32)
    r_idx = (2 * lax.axis_index("x") + lax.axis_index("y")).astype(jnp.int32)
    cr = jnp.stack([c_idx, r_idx])

    wl["w_in"], ml["w_in"], vl["w_in"] = (jnp.swapaxes(a, 1, 2) for a in (w_in, m_w_in, v_w_in))

    bufs = {}
    for n in BIG:
        k, nn = wl[n].shape[1:]
        for l, b in enumerate(_cast_bf16(wl[n], r_idx.reshape(1), "cast_" + n)):
            bufs[n, l] = b.reshape(4, 2, k // 2, nn)

    def gather(keys, run):
        for key, b in zip(keys, run(_all_gather([bufs[key] for key in keys]))):
            bufs[key] = b

    def gathered(n, l):
        _, _, k2, nn = bufs[n, l].shape
        if n in ROW_SHARDED or n == "w_in":
            return bufs[n, l].reshape(8 * k2, nn)
        return bufs[n, l].reshape(4, 2 * k2, nn)

    gather([("w_in", 0)], lambda ex: _run_exchange("all_gather_first", ex))

    full = {n: [None] * depth for n in BIG}
    saved = []
    xi = x0
    for i in range(depth):
        st = dict(x0=xi)
        gm = g_mix[i].reshape(1, d)
        full["w_in"][i] = gathered("w_in", i)
        st["h1"], st["qkv"], st["gates"] = _inproj_fwd(xi, gm, full["w_in"][i], f"inproj_fwd_{i}")

        def attend(ex):
            (st["oa"], st["lt"], st["nb"], st["a_wide"], st["sg_wide"]), filled = _sb_fwd(st["qkv"], f"sb_fwd_{i}", ex)
            return filled

        gather([(n, i) for n in BIG if n != "w_in"], attend)
        for n in BIG:
            if n != "w_in":
                full[n][i] = gathered(n, i)
        st["ob"] = _swa_fwd(st["qkv"], sinks[i], rel_bias, f"swa_fwd_{i}")
        st["m"], st["x1"] = _mixer_fwd(st["oa"], st["ob"], st["gates"], xi, full["w_up_a"][i], full["w_up_b"][i],
                                       full["w_o"][i], f"mixer_fwd_{i}")
        if i == 0:
            def feed_forward(ex):
                (st["h2"], st["u"], st["a"]), filled = _ff1_fwd(st["x1"], g_mlp[i].reshape(1, d), full["w_ff1"][i],
                                                               f"ff1_fwd_{i}", ex)
                return filled

            gather([("w_in", 1)], feed_forward)
        else:
            st["h2"], st["u"], st["a"] = _ff1_fwd(st["x1"], g_mlp[i].reshape(1, d), full["w_ff1"][i], f"ff1_fwd_{i}")
        st["x2"] = _ff2_fwd(st["a"], st["x1"], full["w_ff2"][i], f"ff2_fwd_{i}")
        st["pb"], st["h3"], st["pe"], st["gt"], xi = _ple_fwd(p[i, 0], st["x2"], g_pe[i].reshape(1, d),
                                                            full["w_pe"][i], full["w_pg"][i], f"ple_fwd_{i}")
        saved.append(st)

    dx, dg_final, loss_part = _loss_bwd(xi, target, g_final.reshape(1, d), "loss_bwd")
    gw = {n: [None] * depth for n in BIG}
    reduced = {}

    chip_sums = {}

    def to_sibling(keys, run):
        tensors = []
        for n, l in keys:
            g = gw[n][l]
            if n in ROW_SHARDED:
                ka, nb = g.shape[1:]
                g = g.reshape(4, ka // 4, nb)
            _, k, nn = g.shape
            tensors.append(g.reshape(4, 2, k // 2, nn))
        for (n, l), g, r in zip(keys, tensors, run(_rs_to_sibling(tensors))):
            chip_sums[n, l] = _add_half(g, r, cr, f"add_half_{n}_{l}")

    def to_chips(keys):
        return _rs_to_chips([chip_sums[k][0] for k in keys], [chip_sums[k][1] for k in keys])

    halves = {}

    def sum_chips(keys, parts):
        for (n, l), pc in zip(keys, parts):
            halves[n, l] = _sum4(pc, cr, f"sum4_{n}_{l}")

    def swap_halves(keys):
        def store(filled):
            for key, both in zip(keys, filled):
                reduced[key] = both
        return _exchange_halves([halves[k] for k in keys]), store

    dg_mix, dg_mlp, dg_pe, dsinks = [None] * depth, [None] * depth, [None] * depth, [None] * depth
    drel = jnp.zeros((SW_HEADS, LANES), F32)
    for i in reversed(range(depth)):
        st = saved[i]
        dpe, dgt, dx2, dg_pe[i] = _ple_bwd(dx, st["pe"], st["gt"], st["x2"], g_pe[i].reshape(1, d),
                                           full["w_pg"][i], f"ple_bwd_{i}")
        gw["w_pe"][i] = _mm_tn(st["pb"], dpe, f"dw_pe_{i}", 4)
        gw["w_pg"][i] = _mm_tn(st["h3"], dgt, f"dw_pg_{i}")
        du, dx2b = _ff2_bwd(dx2, st["u"], full["w_ff2"][i], f"ff2_bwd_{i}")
        gw["w_ff2"][i] = _mm_tn(st["a"], dx2b, f"dw_ff2_{i}")
        gw["w_ff1"][i] = _mm_tn(st["h2"], du, f"dw_ff1_{i}", 4)
        if i == 0:
            (dx1, dx1b, dg_mlp[i]), parts = _ff1_bwd(du, dx2, st["x1"], g_mlp[i].reshape(1, d), full["w_ff1"][i],
                                                    f"ff1_bwd_{i}", to_chips([("w_in", 1)]))
            sum_chips([("w_in", 1)], parts)
        else:
            dx1, dx1b, dg_mlp[i] = _ff1_bwd(du, dx2, st["x1"], g_mlp[i].reshape(1, d), full["w_ff1"][i],
                                            f"ff1_bwd_{i}")
        gw["w_o"][i] = _mm_tn(st["m"], dx1b, f"dw_o_{i}")
        early = [(n, i) for n in ("w_pe", "w_pg", "w_ff2", "w_ff1", "w_o")]

        def mixer(ex):
            (dya, dyb, dgates, doa, dob), landed = _mixer_bwd(
                dx1b, st["gates"], st["oa"], st["ob"], full["w_o"][i], full["w_up_a"][i], full["w_up_b"][i],
                f"mixer_bwd_{i}", ex)
            st.update(dya=dya, dyb=dyb, dgates=dgates, doa=doa, dob=dob)
            return landed

        to_sibling(early, mixer)
        dgates = st["dgates"]
        gw["w_up_a"][i] = _mm_tn(st["oa"], st["dya"], f"dw_up_a_{i}", 4)
        gw["w_up_b"][i] = _mm_tn(st["ob"], st["dyb"], f"dw_up_b_{i}", 4)
        late = [("w_up_a", i), ("w_up_b", i)]
        to_sibling(late, lambda ex: _run_exchange(f"rs_to_sibling_{i}", ex))
        keys = early + late
        dqa, dka, dva, parts = _sb_bwd(st["qkv"], st["lt"], st["nb"], st["a_wide"], st["sg_wide"], st["doa"],
                                       f"sb_bwd_{i}", to_chips(keys))
        sum_chips(keys, parts)
        dob = st["dob"]
        dqb, dkvb, dsk, drl = _swa_bwd(st["qkv"], st["ob"], dob, sinks[i], rel_bias, f"swa_bwd_{i}")
        dsinks[i] = dsk[:, 0]
        drel = drel + drl
        dproj = [dqa, dka, dva, dqb, dkvb, dgates]
        swap, store = swap_halves(keys + ([("w_in", 1)] if i == 0 else []))
        dw_in_t, filled = _mm_tn_pieces(dproj, st["h1"], f"dw_in_{i}", swap)
        store(filled)
        gw["w_in"][i] = dw_in_t.reshape(4, dw_in_t.shape[0] // 4, d)
        if i == 1:
            def inproj(ex):
                (dx, dg_mix[i]), landed = _inproj_bwd(dproj, dx1, st["x0"], g_mix[i].reshape(1, d),
                                                      full["w_in"][i], f"inproj_bwd_{i}", ex)
                st["dx"] = dx
                return landed

            to_sibling([("w_in", 1)], inproj)
            dx = st["dx"]
        else:
            keys = [("w_in", 0)]
            to_sibling(keys, lambda ex: _run_exchange("rs_to_sibling_last", ex))
            (dx, dg_mix[i]), parts = _inproj_bwd(dproj, dx1, st["x0"], g_mix[i].reshape(1, d),
                                                 full["w_in"][i], f"inproj_bwd_{i}", to_chips(keys))
            sum_chips(keys, parts)
            swap, store = swap_halves(keys)
            store(_run_exchange("exchange_halves_last", swap))
    grad_x = dx[None]

    outs = {}
    for n in BIG:
        g0, g1 = (reduced[n, l].reshape(wl[n].shape[1:]) for l in range(depth))
        outs[n] = _adamw(wl[n], ml[n], vl[n], g0, g1, "adamw_" + n)
    outs["w_in"] = [jnp.swapaxes(a, 1, 2) for a in outs["w_in"]]

    drel_bias = drel[:, :N_BUCKETS].T
    gsmall = _pack_small(jnp.concatenate(dg_mix, 0), jnp.concatenate(dg_mlp, 0), jnp.concatenate(dg_pe, 0),
                         dg_final[0], jnp.stack(dsinks), drel_bias, loss_part[0, :1])
    wsmall = _pack_small(g_mix, g_mlp, g_pe, g_final, sinks, rel_bias)
    msmall = _pack_small(m_g_mix, m_g_mlp, m_g_pe, m_g_final, m_sinks, m_rel_bias)
    vsmall = _pack_small(v_g_mix, v_g_mlp, v_g_pe, v_g_final, v_sinks, v_rel_bias)
    small = _small_allreduce_adamw(gsmall, wsmall, msmall, vsmall)
    loss = small[0][7, 0]
    small = [_unpack_small(a, sinks.shape, rel_bias.shape) for a in small]

    result = [loss, grad_x]
    for kind in range(4):
        result += [outs[n][kind] for n in BIG]
        result += list(small[kind])
    return tuple(result)
```

```python
import functools
import math
from typing import Callable, NamedTuple

import numpy as np
import jax
import jax.numpy as jnp
from jax import lax
from jax.experimental import pallas as pl
from jax.experimental.pallas import tpu as pltpu

F32 = jnp.float32
BF16 = jnp.bfloat16
MESH = pl.DeviceIdType.MESH

HEAD_DIM = 64
SB_HEADS = 8
SW_HEADS = 8
SW_KV_HEADS = 2
WINDOW = 128
N_BUCKETS = 32
MAX_DISTANCE = 128
EPS = 1e-6
SB_W = SB_HEADS * HEAD_DIM
SW_QW = SW_HEADS * HEAD_DIM
SW_KVW = SW_KV_HEADS * HEAD_DIM
QKV_W = 3 * SB_W + SW_QW + 2 * SW_KVW
SCALE = HEAD_DIM ** -0.5
assert SCALE == 0.125
LANES = 128
TQ = 128
BK = 128
NEG = -1e30
SB_EXHAUSTED = -106.0

ADAM_LR = 0.001
ADAM_B1 = 0.9
ADAM_B2 = 0.999
ADAM_EPS = 1e-08
ADAM_WD = 0.01
ADAM_STEP = 10

VMEM_LIMIT = 56 * 1024 * 1024


def _dot(a, b):
    return jnp.dot(a, b, preferred_element_type=F32)


def _dot_nt(a, b):
    return lax.dot_general(a, b, (((1,), (1,)), ((), ())), preferred_element_type=F32)


def _dot_tn(a, b):
    return lax.dot_general(a, b, (((0,), (0,)), ((), ())), preferred_element_type=F32)


def _sum_all(x):
    return jnp.sum(jnp.sum(x, axis=1, keepdims=True), axis=0, keepdims=True)


def _sigmoid(x):
    return 1.0 / (1.0 + jnp.exp(-x))


def _rms(x, g):
    r = lax.rsqrt(jnp.mean(x * x, axis=-1, keepdims=True) + EPS)
    return (x * r) * g


def _rms_bwd(dy, x, g):
    r = lax.rsqrt(jnp.mean(x * x, axis=-1, keepdims=True) + EPS)
    n = x * r
    dg = jnp.sum(dy * n, axis=0, keepdims=True)
    dn = dy * g
    dx = r * (dn - n * jnp.mean(dn * n, axis=-1, keepdims=True))
    return dx, dg


def _params(n_axes):
    return pltpu.CompilerParams(dimension_semantics=("arbitrary",) * n_axes, vmem_limit_bytes=VMEM_LIMIT)


def _rowcall(name, body, row_ins, const_ins, row_outs, acc_outs=(), tm=512, side=None):
    s = row_ins[0].shape[0]
    assert s % tm == 0
    in_specs = [pl.BlockSpec((tm, a.shape[1]), lambda i: (i, 0)) for a in row_ins]
    in_specs += [pl.BlockSpec(a.shape, functools.partial(lambda i, nd: (0,) * nd, nd=a.ndim)) for a in const_ins]
    out_shape = [jax.ShapeDtypeStruct((s, c), dt) for c, dt in row_outs]
    out_specs = [pl.BlockSpec((tm, c), lambda i: (i, 0)) for c, _ in row_outs]
    out_shape += [jax.ShapeDtypeStruct(sh, dt) for sh, dt in acc_outs]
    out_specs += [pl.BlockSpec(sh, functools.partial(lambda i, nd: (0,) * nd, nd=len(sh))) for sh, _ in acc_outs]
    if side is None:
        return pl.pallas_call(body, out_shape=out_shape, grid=(s // tm,), in_specs=in_specs, out_specs=out_specs,
                              compiler_params=_params(1), name=name)(*row_ins, *const_ins)
    n_in, n_out = len(in_specs), len(out_specs)

    def with_side(*refs):
        side_in, outs, side_out, sems = _side_refs(side, refs[n_in:], n_out)

        @pl.when(pl.program_id(0) == 0)
        def _():
            side.start(side_in, side_out, *sems)

        body(*refs[:n_in], *outs)

        @pl.when(pl.program_id(0) == s // tm - 1)
        def _():
            side.finish(side_in, side_out, *sems)

    s_in, s_shape, s_out, s_alias, s_sems = _side_specs(side, n_in, n_out)
    outs = pl.pallas_call(with_side, out_shape=out_shape + s_shape, grid=(s // tm,), in_specs=in_specs + s_in,
                          out_specs=out_specs + s_out, input_output_aliases=s_alias, scratch_shapes=s_sems,
                          compiler_params=_params(1), name=name)(*row_ins, *const_ins, *side.arrays)
    return outs[:n_out], outs[n_out:]


def _dot_cols(a, w_ref):
    return jnp.concatenate([_dot(a, w_ref[r]) for r in range(w_ref.shape[0])], axis=1)


def _dot_cols_t(a, w_ref):
    n = w_ref.shape[2]
    out = _dot_nt(a[:, :n], w_ref[0])
    for r in range(1, w_ref.shape[0]):
        out = out + _dot_nt(a[:, r * n:(r + 1) * n], w_ref[r])
    return out


def _inproj_fwd(x, g, wt, name):
    d = x.shape[1]

    def body(x_ref, g_ref, w_ref, h_ref, qkv_ref, gate_ref):
        hb = _rms(x_ref[...], g_ref[...]).astype(BF16)
        h_ref[...] = hb
        qkv_ref[...] = _dot_nt(hb, w_ref[:QKV_W, :]).astype(BF16)
        gate_ref[...] = _dot_nt(hb, w_ref[QKV_W:, :])

    return _rowcall(name, body, [x], [g, wt], [(d, BF16), (QKV_W, BF16), (2 * d, F32)])


def _mixer_fwd(oa, ob, gates, x, wua, wub, wo, name):
    d = x.shape[1]

    def body(oa_ref, ob_ref, gate_ref, x_ref, wua_ref, wub_ref, wo_ref, m_ref, x1_ref):
        ya = _dot_cols(oa_ref[...], wua_ref)
        yb = _dot_cols(ob_ref[...], wub_ref)
        m = _sigmoid(gate_ref[:, :d]) * ya + _sigmoid(gate_ref[:, d:]) * yb
        mb = m.astype(BF16)
        m_ref[...] = mb
        x1_ref[...] = x_ref[...] + _dot(mb, wo_ref[...])

    return _rowcall(name, body, [oa, ob, gates, x], [wua, wub, wo], [(d, BF16), (d, F32)])


def _ff1_fwd(x1, g, w1, name, side=None):
    _, d, nq = w1.shape
    dff = 4 * nq

    def body(x_ref, g_ref, w_ref, h_ref, r_ref, a_ref):
        hb = _rms(x_ref[...], g_ref[...]).astype(BF16)
        h_ref[...] = hb
        r = jnp.maximum(_dot_cols(hb, w_ref), 0.0)
        r_ref[...] = r.astype(BF16)
        a_ref[...] = jnp.square(r).astype(BF16)

    return _rowcall(name, body, [x1], [g, w1], [(d, BF16), (dff, BF16), (dff, BF16)], side=side)


def _ff2_fwd(a, x1, w2, name):
    d = x1.shape[1]

    def body(a_ref, x_ref, w_ref, o_ref):
        o_ref[...] = x_ref[...] + _dot(a_ref[...], w_ref[...])

    return _rowcall(name, body, [a, x1], [w2], [(d, F32)])[0]


def _ple_fwd(p, x2, g, wpe, wpg, name):
    d = x2.shape[1]

    def body(p_ref, x_ref, g_ref, wpe_ref, wpg_ref, pb_ref, h_ref, pe_ref, gt_ref, x3_ref):
        pb = p_ref[...].astype(BF16)
        pb_ref[...] = pb
        pe = _dot_cols(pb, wpe_ref)
        x = x_ref[...]
        hb = _rms(x, g_ref[...]).astype(BF16)
        h_ref[...] = hb
        gt = _dot(hb, wpg_ref[...])
        pe_ref[...] = pe
        gt_ref[...] = gt
        x3_ref[...] = x + pe * _sigmoid(gt)

    return _rowcall(name, body, [p, x2], [g, wpe, wpg],
                    [(p.shape[1], BF16), (d, BF16), (d, F32), (d, F32), (d, F32)])


def _pair_stack(t, lane):
    zero = jnp.zeros_like(t)
    return jnp.concatenate([jnp.where(lane < HEAD_DIM, t, zero), jnp.where(lane >= HEAD_DIM, t, zero)], axis=0)


def _sb_rel():
    row = lax.broadcasted_iota(jnp.int32, (2 * TQ, BK), 0)
    row = jnp.where(row >= TQ, row - TQ, row)
    col = lax.broadcasted_iota(jnp.int32, (2 * TQ, BK), 1)
    return col - row


def _split_dot(x, m01, two_pass=True):
    hi = x.astype(BF16)
    if not two_pass:
        return _dot(hi, m01)
    lo = (x - hi.astype(F32)).astype(BF16)
    return _dot(hi, m01) + _dot(lo, m01)


def _sb_scores(qs, k, mask):
    z = _dot_nt(qs, k)
    lb = jnp.minimum(z, 0.0) - jnp.log(1.0 + jnp.exp(-jnp.abs(z)))
    lm = lb - z
    return lb, lm if mask is None else jnp.where(mask, lm, 0.0)


SB_STRAIGHT = 3
SB_WIDE = SB_STRAIGHT * BK
SB_QB = 2
SWA_QB = 4


def _sb_wide_consts():
    j = np.arange(BK)[:, None]
    s = np.arange(BK)[None, :]
    ones = np.ones((BK, BK), np.float32)
    as_bf16 = lambda m: jnp.asarray(np.concatenate([m, ones], axis=1).astype(np.float32), dtype=BF16)
    return as_bf16(j > s), as_bf16(j <= s), as_bf16(j < s)


def _wide_sums(x, m01, suffix, two_pass=True):
    parts = [_split_dot(x[:, b * BK:(b + 1) * BK], m01, two_pass) for b in range(SB_STRAIGHT)]
    order = range(SB_STRAIGHT - 1, -1, -1) if suffix else range(SB_STRAIGHT)
    out = [None] * SB_STRAIGHT
    carry = None
    for b in order:
        out[b] = parts[b][:, :BK] if carry is None else parts[b][:, :BK] + carry
        carry = parts[b][:, BK:] if carry is None else carry + parts[b][:, BK:]
    return jnp.concatenate(out, axis=1), carry


def _side_refs(ex, rest, n_out):
    n_in = len(ex.arrays) if ex else 0
    n_alias = len(ex.aliased) if ex else 0
    ins, rest = rest[:n_in], rest[n_in:]
    outs, rest = rest[:n_out], rest[n_out:]
    return ins, outs, rest[:n_alias], rest[n_alias:]


def _side_specs(ex, n_in, n_out):
    if ex is None:
        return [], [], [], {}, []
    any_spec = pl.BlockSpec(memory_space=pl.ANY)
    return ([any_spec] * len(ex.arrays),
            [jax.ShapeDtypeStruct(ex.arrays[a].shape, ex.arrays[a].dtype) for a in ex.aliased],
            [any_spec] * len(ex.aliased), {n_in + a: n_out + o for o, a in enumerate(ex.aliased)},
            [pltpu.SemaphoreType.DMA(ex.sems), pltpu.SemaphoreType.DMA(ex.sems)])


def _sb_fwd(qkv, name, side=None):
    s = qkv.shape[0]
    nq = s // TQ
    npair = SB_W // LANES
    sufw = _sb_wide_consts()[0]

    def body(q_ref, k_ref, v_ref, sufw_ref, *rest):
        side_in, (o_ref, lt_ref, nb_ref, a_ref, sg_ref), side_out, scratch = _side_refs(side, rest, 5)
        cf_ref, acc_ref = scratch[:2]
        step_id = pl.program_id(1)
        if side is not None:
            @pl.when((pl.program_id(0) == 0) & (step_id == 0))
            def _():
                side.start(side_in, side_out, *scratch[2:])
        lane = lax.broadcasted_iota(jnp.int32, (TQ, LANES), 1)
        rel = _sb_rel()
        blocks = [step_id * SB_QB + b for b in range(SB_QB)]
        qs = [_pair_stack(q_ref[b * TQ:(b + 1) * TQ, :] * SCALE, lane) for b in range(SB_QB)]

        straight = blocks[0] >= SB_STRAIGHT - 1

        @pl.when(straight)
        def _():
            for b, i in enumerate(blocks):
                w0 = pl.multiple_of((i - (SB_STRAIGHT - 1)) * BK, BK)
                kw = k_ref[pl.ds(w0, SB_WIDE), :]
                lb, lm = _sb_scores(qs[b], kw, None)
                own = rel < 0
                past = SB_WIDE - BK
                lm = jnp.concatenate([lm[:, :past], jnp.where(own, lm[:, past:], 0.0)], axis=1)
                after, total = _wide_sums(lm, sufw_ref[...], True)
                on_past_keys = lambda t: jnp.concatenate([t[:, :past], jnp.where(own, t[:, past:], 0.0)], axis=1)
                a = on_past_keys(jnp.exp(lb + after)).astype(BF16)
                acc_ref[b] = _dot(a, v_ref[pl.ds(w0, SB_WIDE), :])
                cf_ref[b] = total
                a_ref[b] = a
                sg_ref[b] = on_past_keys(jnp.exp(lb)).astype(BF16)

        @pl.when(jnp.logical_not(straight))
        def _():
            cf_ref[...] = jnp.zeros_like(cf_ref)
            acc_ref[...] = jnp.zeros_like(acc_ref)

        for b, i in enumerate(blocks):
            q0 = i * TQ

            def more(c, i=i):
                return (c[0] <= i) & (c[1] > SB_EXHAUSTED)

            def step(c, b=b, i=i, q0=q0):
                k0 = pl.multiple_of((i - c[0]) * BK, BK)
                k = k_ref[pl.ds(k0, BK), :]
                v = v_ref[pl.ds(k0, BK), :]
                mask = rel < (q0 - k0)
                lb, lm = _sb_scores(qs[b], k, mask)
                cs = _split_dot(lm, sufw_ref[...])
                a = jnp.where(mask, jnp.exp(lb + cs[:, :BK] + cf_ref[b]), 0.0)
                acc_ref[b] += _dot(a.astype(BF16), v)
                cf = cf_ref[b] + cs[:, BK:]
                cf_ref[b] = cf
                return c[0] + 1, jnp.max(cf)

            n_blocks, _ = lax.while_loop(
                more, step, (jnp.where(straight, SB_STRAIGHT, 0).astype(jnp.int32), jnp.max(cf_ref[b])))
            o_ref[b * TQ:(b + 1) * TQ, :] = jnp.where(lane < HEAD_DIM, acc_ref[b, :TQ, :],
                                                     acc_ref[b, TQ:, :]).astype(BF16)
            lt_ref[b] = cf_ref[b]
            nb_ref[b] = jnp.full(nb_ref.shape[1:], n_blocks, F32)
        if side is not None:
            @pl.when((pl.program_id(0) == npair - 1) & (step_id == nq // SB_QB - 1))
            def _():
                side.finish(side_in, side_out, *scratch[2:])

    s_in, s_shape, s_out, s_alias, s_sems = _side_specs(side, 4, 5)
    wide = jax.ShapeDtypeStruct((npair, nq, 2 * TQ, SB_WIDE), BF16)
    wide_spec = pl.BlockSpec((None, SB_QB, 2 * TQ, SB_WIDE), lambda j, i: (j, i, 0, 0))
    outs = pl.pallas_call(
        body,
        out_shape=[jax.ShapeDtypeStruct((s, SB_W), BF16), jax.ShapeDtypeStruct((npair, nq, 2 * TQ, BK), F32),
                   jax.ShapeDtypeStruct((npair, nq, 8, LANES), F32), wide, wide] + s_shape,
        grid=(npair, nq // SB_QB),
        in_specs=[pl.BlockSpec((SB_QB * TQ, LANES), lambda j, i: (i, j)),
                  pl.BlockSpec((s, LANES), lambda j, i: (0, npair + j)),
                  pl.BlockSpec((s, LANES), lambda j, i: (0, 2 * npair + j)),
                  pl.BlockSpec(sufw.shape, lambda j, i: (0, 0))] + s_in,
        out_specs=[pl.BlockSpec((SB_QB * TQ, LANES), lambda j, i: (i, j)),
                   pl.BlockSpec((None, SB_QB, 2 * TQ, BK), lambda j, i: (j, i, 0, 0)),
                   pl.BlockSpec((None, SB_QB, 8, LANES), lambda j, i: (j, i, 0, 0)), wide_spec, wide_spec] + s_out,
        input_output_aliases=s_alias,
        scratch_shapes=[pltpu.VMEM((SB_QB, 2 * TQ, BK), F32), pltpu.VMEM((SB_QB, 2 * TQ, LANES), F32)] + s_sems,
        compiler_params=_params(2), name=name)(qkv, qkv, qkv, sufw, *(side.arrays if side else ()))
    return outs[:5], outs[5:]


def _sb_bwd(qkv, lt, nb, a_wide, sg_wide, doa, name, side=None):
    s = qkv.shape[0]
    nq = s // TQ
    npair = SB_W // LANES
    _, prew, prexw = _sb_wide_consts()

    def body(q_ref, k_ref, v_ref, lt_ref, nb_ref, do_ref, prew_ref, prexw_ref, a_ref, sg_ref, *rest):
        side_in, (dq_ref, dk_out, dv_out), side_out, scratch = _side_refs(side, rest, 3)
        cp_ref, ce_ref, dqa_ref, dk_ref, dv_ref = scratch[:5]
        sems = scratch[5:]
        step_id = pl.program_id(1)
        if side is not None:
            @pl.when((pl.program_id(0) == 0) & (step_id == 0))
            def _():
                side.start(side_in, side_out, *sems)
        lane = lax.broadcasted_iota(jnp.int32, (TQ, LANES), 1)
        rel = _sb_rel()
        blocks = [step_id * SB_QB + b for b in range(SB_QB)]
        rows = [slice(b * TQ, (b + 1) * TQ) for b in range(SB_QB)]
        qs = [_pair_stack(q_ref[rows[b], :] * SCALE, lane) for b in range(SB_QB)]
        dos = [_pair_stack(do_ref[rows[b], :], lane) for b in range(SB_QB)]
        n_blocks = [jnp.clip(jnp.max(nb_ref[b]).astype(jnp.int32), 1, i + 1) for b, i in enumerate(blocks)]
        first = [i + 1 - n for i, n in zip(blocks, n_blocks)]

        @pl.when(step_id == 0)
        def _():
            dk_ref[...] = jnp.zeros_like(dk_ref)
            dv_ref[...] = jnp.zeros_like(dv_ref)

        straight = n_blocks[0] == SB_STRAIGHT
        for n in n_blocks[1:]:
            straight = straight & (n == SB_STRAIGHT)

        @pl.when(straight)
        def _():
            for b in range(SB_QB):
                w0 = pl.multiple_of(first[b] * BK, BK)
                kw = k_ref[pl.ds(w0, SB_WIDE), :]
                vw = v_ref[pl.ds(w0, SB_WIDE), :]
                a = a_ref[b]
                e = a.astype(F32) * _dot_nt(dos[b], vw)
                big_e, _ = _wide_sums(e, prexw_ref[...], False, two_pass=False)
                dz = (e - sg_ref[b].astype(F32) * (e + big_e)).astype(BF16)
                dk_ref[pl.ds(w0, SB_WIDE), :] += _dot_tn(dz, qs[b])
                dv_ref[pl.ds(w0, SB_WIDE), :] += _dot_tn(a, dos[b])
                dqa_ref[b] = _dot(dz, kw)

        @pl.when(jnp.logical_not(straight))
        def _():
            cp_ref[...] = jnp.zeros_like(cp_ref)
            ce_ref[...] = jnp.zeros_like(ce_ref)
            dqa_ref[...] = jnp.zeros_like(dqa_ref)
            for b, i in enumerate(blocks):
                q0 = i * TQ

                def step(it, carry, b=b, q0=q0):
                    k0 = pl.multiple_of((first[b] + it) * BK, BK)
                    k = k_ref[pl.ds(k0, BK), :]
                    v = v_ref[pl.ds(k0, BK), :]
                    mask = rel < (q0 - k0)
                    lb, lm = _sb_scores(qs[b], k, mask)
                    cs = _split_dot(lm, prew_ref[...])
                    a = jnp.where(mask, jnp.exp(lb + (lt_ref[b] - (cs[:, :BK] + cp_ref[b]))), 0.0)
                    e = a * _dot_nt(dos[b], v)
                    ce = _split_dot(e, prexw_ref[...], two_pass=False)
                    big_e = ce[:, :BK] + ce_ref[b]
                    dz = jnp.where(mask, e - jnp.exp(lb) * (e + big_e), 0.0).astype(BF16)
                    dk_ref[pl.ds(k0, BK), :] += _dot_tn(dz, qs[b])
                    dv_ref[pl.ds(k0, BK), :] += _dot_tn(a.astype(BF16), dos[b])
                    dqa_ref[b] += _dot(dz, k)
                    cp_ref[b] += cs[:, BK:]
                    ce_ref[b] += ce[:, BK:]
                    return carry

                lax.fori_loop(0, n_blocks[b], step, 0)

        for b in range(SB_QB):
            dq = jnp.where(lane < HEAD_DIM, dqa_ref[b, :TQ, :], dqa_ref[b, TQ:, :])
            dq_ref[rows[b], :] = (dq * SCALE).astype(BF16)

        @pl.when(step_id == nq // SB_QB - 1)
        def _():
            dk_out[...] = dk_ref[...].astype(BF16)
            dv_out[...] = dv_ref[...].astype(BF16)

        if side is not None:
            @pl.when((pl.program_id(0) == npair - 1) & (step_id == nq // SB_QB - 1))
            def _():
                side.finish(side_in, side_out, *sems)

    s_in, s_shape, s_out, s_alias, s_sems = _side_specs(side, 10, 3)
    wide_spec = pl.BlockSpec((None, SB_QB, 2 * TQ, SB_WIDE), lambda j, i: (j, i, 0, 0))
    outs = pl.pallas_call(
        body,
        out_shape=[jax.ShapeDtypeStruct((s, SB_W), BF16)] * 3 + s_shape,
        grid=(npair, nq // SB_QB),
        in_specs=[pl.BlockSpec((SB_QB * TQ, LANES), lambda j, i: (i, j)),
                  pl.BlockSpec((s, LANES), lambda j, i: (0, npair + j)),
                  pl.BlockSpec((s, LANES), lambda j, i: (0, 2 * npair + j)),
                  pl.BlockSpec((None, SB_QB, 2 * TQ, BK), lambda j, i: (j, i, 0, 0)),
                  pl.BlockSpec((None, SB_QB, 8, LANES), lambda j, i: (j, i, 0, 0)),
                  pl.BlockSpec((SB_QB * TQ, LANES), lambda j, i: (i, j)),
                  pl.BlockSpec(prew.shape, lambda j, i: (0, 0)),
                  pl.BlockSpec(prexw.shape, lambda j, i: (0, 0)), wide_spec, wide_spec] + s_in,
        out_specs=[pl.BlockSpec((SB_QB * TQ, LANES), lambda j, i: (i, j)),
                   pl.BlockSpec((s, LANES), lambda j, i: (0, j)),
                   pl.BlockSpec((s, LANES), lambda j, i: (0, j))] + s_out,
        input_output_aliases=s_alias,
        scratch_shapes=[pltpu.VMEM((SB_QB, 2 * TQ, BK), F32), pltpu.VMEM((SB_QB, 2 * TQ, BK), F32),
                        pltpu.VMEM((SB_QB, 2 * TQ, LANES), F32), pltpu.VMEM((s, LANES), F32),
                        pltpu.VMEM((s, LANES), F32)] + s_sems,
        compiler_params=_params(2), name=name)(qkv, qkv, qkv, lt, nb, doa, prew, prexw, a_wide, sg_wide,
                                               *(side.arrays if side else ()))
    return outs[0], outs[1], outs[2], outs[3:]


def _bucket_table():
    i = np.arange(TQ)[:, None]
    j = np.arange(2 * BK)[None, :]
    dist = np.maximum(TQ + i - j, 0)
    max_exact = N_BUCKETS // 2
    df = np.maximum(dist, 1).astype(np.float32)
    large = max_exact + (np.log(df / np.float32(max_exact)) / np.float32(math.log(MAX_DISTANCE / max_exact))
                         * np.float32(N_BUCKETS - max_exact)).astype(np.int32)
    large = np.minimum(large, N_BUCKETS - 1)
    return np.where(dist < max_exact, dist, large).astype(np.int32)


def _swa_align_in(t, lane, g):
    tf = t.astype(F32)
    tr = pltpu.roll(tf, HEAD_DIM, 1)
    gmask = (lane >= HEAD_DIM) == (g == 1)
    top = jnp.where(gmask, jnp.where(g == 0, tf, tr), 0.0)
    bot = jnp.where(gmask, jnp.where(g == 1, tf, tr), 0.0)
    return jnp.concatenate([top, bot], axis=0).astype(BF16)


def _swa_align_out(t, lane, g):
    top, bot = t[:TQ, :], t[TQ:, :]
    top = jnp.where(g == 0, top, pltpu.roll(top, HEAD_DIM, 1))
    bot = jnp.where(g == 1, bot, pltpu.roll(bot, HEAD_DIM, 1))
    return jnp.where(lane < HEAD_DIM, top, bot)


def _swa_bias(bias_ref, bucket_ref, rb_ref, j):
    dist = TQ + lax.broadcasted_iota(jnp.int32, (TQ, 2 * BK), 0) - lax.broadcasted_iota(jnp.int32, (TQ, 2 * BK), 1)
    window = (dist >= 0) & (dist < WINDOW)
    for hh in range(2):
        def add(b, acc):
            return acc + jnp.where(bucket_ref[...] == b, rb_ref[b, 2 * j + hh], 0.0)
        bias = lax.fori_loop(0, N_BUCKETS, add, jnp.zeros((TQ, 2 * BK), F32))
        bias_ref[hh * TQ:(hh + 1) * TQ, :] = jnp.where(window, bias, NEG)


def _swa_probs(qs, k2, bias, own_block, sink_ref, i, j):
    s = _dot_nt(qs, k2) + bias
    s = jnp.where(own_block | (i > 0), s, NEG)
    row1 = lax.broadcasted_iota(jnp.int32, (2 * TQ, 1), 0)
    sink = jnp.where(row1 < TQ, sink_ref[2 * j], sink_ref[2 * j + 1])
    m = jnp.maximum(jnp.max(s, axis=1, keepdims=True), sink)
    e = jnp.exp(s - m)
    es = jnp.exp(sink - m)
    inv = 1.0 / (jnp.sum(e, axis=1, keepdims=True) + es)
    return e * inv, es * inv


def _swa_kv(ref, i):
    prev = pl.multiple_of(jnp.maximum(i - 1, 0) * BK, BK)
    cur = pl.multiple_of(i * BK, BK)
    return jnp.concatenate([ref[pl.ds(prev, BK), :], ref[pl.ds(cur, BK), :]], axis=0), prev, cur


def _swa_fwd(qkv, sinks, rel_bias, name):
    s = qkv.shape[0]
    nq = s // TQ
    npair = SW_QW // LANES
    qcol = 3 * SB_W // LANES
    bucket = jnp.asarray(_bucket_table())

    def body(q_ref, k_ref, v_ref, bucket_ref, sink_ref, rb_ref, o_ref, bias_ref):
        j = pl.program_id(0)
        step = pl.program_id(1)
        g = j // 2
        lane = lax.broadcasted_iota(jnp.int32, (TQ, LANES), 1)

        @pl.when(step == 0)
        def _():
            _swa_bias(bias_ref, bucket_ref, rb_ref, j)

        own_block = lax.broadcasted_iota(jnp.int32, (2 * TQ, 2 * BK), 1) >= BK
        for b in range(SWA_QB):
            i = step * SWA_QB + b
            rows = slice(b * TQ, (b + 1) * TQ)
            qs = _swa_align_in(q_ref[rows, :] * SCALE, lane, g)
            k2, _, _ = _swa_kv(k_ref, i)
            v2, _, _ = _swa_kv(v_ref, i)
            pr, _ = _swa_probs(qs, k2, bias_ref[...], own_block, sink_ref, i, j)
            o_ref[rows, :] = _swa_align_out(_dot(pr.astype(BF16), v2), lane, g).astype(BF16)

    assert nq % SWA_QB == 0
    return pl.pallas_call(
        body, out_shape=jax.ShapeDtypeStruct((s, SW_QW), BF16), grid=(npair, nq // SWA_QB),
        in_specs=[pl.BlockSpec((SWA_QB * TQ, LANES), lambda j, i: (i, qcol + j)),
                  pl.BlockSpec((s, LANES), lambda j, i: (0, qcol + npair)),
                  pl.BlockSpec((s, LANES), lambda j, i: (0, qcol + npair + 1)),
                  pl.BlockSpec((TQ, 2 * BK), lambda j, i: (0, 0)),
                  pl.BlockSpec(memory_space=pltpu.SMEM),
                  pl.BlockSpec(memory_space=pltpu.SMEM)],
        out_specs=pl.BlockSpec((SWA_QB * TQ, LANES), lambda j, i: (i, j)),
        scratch_shapes=[pltpu.VMEM((2 * TQ, 2 * BK), F32)],
        compiler_params=_params(2), name=name)(qkv, qkv, qkv, bucket, sinks, rel_bias)


def _swa_bwd(qkv, ob, dob, sinks, rel_bias, name, side=None):
    s = qkv.shape[0]
    nq = s // TQ
    npair = SW_QW // LANES
    qcol = 3 * SB_W // LANES
    bucket = jnp.asarray(_bucket_table())

    def body(q_ref, k_ref, v_ref, o_ref, do_ref, bucket_ref, sink_ref, rb_ref, *rest):
        side_in, (dq_ref, dkv_ref, dsink_ref, drel_ref), side_out, scratch = _side_refs(side, rest, 4)
        bias_ref, dsacc_ref, dk_ref, dv_ref = scratch[:4]
        j = pl.program_id(0)
        step = pl.program_id(1)
        if side is not None:
            @pl.when((j == 0) & (step == 0))
            def _():
                side.start(side_in, side_out, *scratch[4:])
        g = j // 2
        lane = lax.broadcasted_iota(jnp.int32, (TQ, LANES), 1)
        row8 = lax.broadcasted_iota(jnp.int32, (SW_HEADS, LANES), 0)
        lane8 = lax.broadcasted_iota(jnp.int32, (SW_HEADS, LANES), 1)

        @pl.when((step == 0) & (j == 0))
        def _():
            dk_ref[...] = jnp.zeros_like(dk_ref)
            dv_ref[...] = jnp.zeros_like(dv_ref)
            dsink_ref[...] = jnp.zeros_like(dsink_ref)
            drel_ref[...] = jnp.zeros_like(drel_ref)

        @pl.when(step == 0)
        def _():
            _swa_bias(bias_ref, bucket_ref, rb_ref, j)
            dsacc_ref[...] = jnp.zeros_like(dsacc_ref)

        own_block = lax.broadcasted_iota(jnp.int32, (2 * TQ, 2 * BK), 1) >= BK
        ds_sum = jnp.zeros(dsacc_ref.shape, F32)
        dsink = jnp.zeros((SW_HEADS, LANES), F32)
        for b in range(SWA_QB):
            i = step * SWA_QB + b
            rows = slice(b * TQ, (b + 1) * TQ)
            qs = _swa_align_in(q_ref[rows, :] * SCALE, lane, g)
            do = do_ref[rows, :]
            dos = _swa_align_in(do, lane, g)
            dof = do.astype(F32) * o_ref[rows, :].astype(F32)
            d0 = jnp.sum(jnp.where(lane < HEAD_DIM, dof, 0.0), axis=1, keepdims=True)
            d1 = jnp.sum(jnp.where(lane >= HEAD_DIM, dof, 0.0), axis=1, keepdims=True)
            delta = jnp.concatenate([d0, d1], axis=0)
            k2, prev, cur = _swa_kv(k_ref, i)
            v2, _, _ = _swa_kv(v_ref, i)
            pr, psink = _swa_probs(qs, k2, bias_ref[...], own_block, sink_ref, i, j)
            ds = pr * (_dot_nt(dos, v2) - delta)
            ds_sum = ds_sum + ds
            sd = psink * delta
            ds0 = -jnp.sum(sd[:TQ, :], axis=0, keepdims=True)
            ds1 = -jnp.sum(sd[TQ:, :], axis=0, keepdims=True)
            dsink = dsink + jnp.where(row8 == 2 * j, ds0, jnp.where(row8 == 2 * j + 1, ds1, 0.0))
            dsb = ds.astype(BF16)
            dq_ref[rows, :] = _swa_align_out(_dot(dsb, k2) * SCALE, lane, g).astype(BF16)
            dk2 = _dot_tn(dsb, qs)
            dv2 = _dot_tn(pr.astype(BF16), dos)
            dk_ref[pl.ds(prev, BK), :] += dk2[:BK, :]
            dk_ref[pl.ds(cur, BK), :] += dk2[BK:, :]
            dv_ref[pl.ds(prev, BK), :] += dv2[:BK, :]
            dv_ref[pl.ds(cur, BK), :] += dv2[BK:, :]
        dsacc_ref[...] += ds_sum
        dsink_ref[...] += dsink

        @pl.when(step == nq // SWA_QB - 1)
        def _():
            for hh in range(2):
                def red(b, acc):
                    val = _sum_all(jnp.where(bucket_ref[...] == b, dsacc_ref[hh * TQ:(hh + 1) * TQ, :], 0.0))
                    return jnp.where((row8 == 2 * j + hh) & (lane8 == b), val, acc)
                drel_ref[...] += lax.fori_loop(0, N_BUCKETS, red, jnp.zeros((SW_HEADS, LANES), F32))

        @pl.when((step == nq // SWA_QB - 1) & (j == npair - 1))
        def _():
            dkv_ref[:, :LANES] = dk_ref[...].astype(BF16)
            dkv_ref[:, LANES:] = dv_ref[...].astype(BF16)
            if side is not None:
                side.finish(side_in, side_out, *scratch[4:])

    whole = lambda j, i: (0, 0)
    s_in, s_shape, s_out, s_alias, s_sems = _side_specs(side, 8, 4)
    outs = pl.pallas_call(
        body,
        out_shape=[jax.ShapeDtypeStruct((s, SW_QW), BF16), jax.ShapeDtypeStruct((s, 2 * LANES), BF16),
                   jax.ShapeDtypeStruct((SW_HEADS, LANES), F32), jax.ShapeDtypeStruct((SW_HEADS, LANES), F32)] + s_shape,
        grid=(npair, nq // SWA_QB),
        in_specs=[pl.BlockSpec((SWA_QB * TQ, LANES), lambda j, i: (i, qcol + j)),
                  pl.BlockSpec((s, LANES), lambda j, i: (0, qcol + npair)),
                  pl.BlockSpec((s, LANES), lambda j, i: (0, qcol + npair + 1)),
                  pl.BlockSpec((SWA_QB * TQ, LANES), lambda j, i: (i, j)),
                  pl.BlockSpec((SWA_QB * TQ, LANES), lambda j, i: (i, j)),
                  pl.BlockSpec((TQ, 2 * BK), whole),
                  pl.BlockSpec(memory_space=pltpu.SMEM),
                  pl.BlockSpec(memory_space=pltpu.SMEM)] + s_in,
        out_specs=[pl.BlockSpec((SWA_QB * TQ, LANES), lambda j, i: (i, j)),
                   pl.BlockSpec((s, 2 * LANES), whole),
                   pl.BlockSpec((SW_HEADS, LANES), whole), pl.BlockSpec((SW_HEADS, LANES), whole)] + s_out,
        input_output_aliases=s_alias,
        scratch_shapes=[pltpu.VMEM((2 * TQ, 2 * BK), F32), pltpu.VMEM((2 * TQ, 2 * BK), F32),
                        pltpu.VMEM((s, LANES), F32), pltpu.VMEM((s, LANES), F32)] + s_sems,
        compiler_params=_params(2), name=name)(qkv, qkv, qkv, ob, dob, bucket, sinks, rel_bias,
                                               *(side.arrays if side else ()))
    return outs[:4], outs[4:]


def _acc_init(i, *refs):
    @pl.when(i == 0)
    def _():
        for r in refs:
            r[...] = jnp.zeros_like(r)


def _loss_bwd(x3, target, g, name):
    d = x3.shape[1]

    def body(x_ref, t_ref, g_ref, dx_ref, dg_ref, loss_ref):
        _acc_init(pl.program_id(0), dg_ref, loss_ref)
        x = x_ref[...]
        gv = g_ref[...]
        diff = _rms(x, gv) - t_ref[...]
        loss_ref[...] += 0.5 * jnp.sum(jnp.mean(jnp.square(diff), axis=-1, keepdims=True), axis=0, keepdims=True)
        dx, dg = _rms_bwd(diff * (1.0 / d), x, gv)
        dx_ref[...] = dx
        dg_ref[...] += dg

    return _rowcall(name, body, [x3, target], [g], [(d, F32)], [((1, d), F32), ((1, LANES), F32)])


def _ple_bwd(dx3, pe, gt, x2, g, wpg, name, side=None):
    d = x2.shape[1]

    def body(dx3_ref, pe_ref, gt_ref, x_ref, g_ref, w_ref, dpe_ref, dgt_ref, dx2_ref, dg_ref):
        _acc_init(pl.program_id(0), dg_ref)
        dx3 = dx3_ref[...]
        sg = _sigmoid(gt_ref[...])
        dpe_ref[...] = (dx3 * sg).astype(BF16)
        dgt = (dx3 * pe_ref[...] * sg * (1.0 - sg)).astype(BF16)
        dgt_ref[...] = dgt
        dx, dg = _rms_bwd(_dot_nt(dgt, w_ref[...]), x_ref[...], g_ref[...])
        dx2_ref[...] = dx3 + dx
        dg_ref[...] += dg

    return _rowcall(name, body, [dx3, pe, gt, x2], [g, wpg], [(d, BF16), (d, BF16), (d, F32)], [((1, d), F32)],
                    side=side)


def _ff2_bwd(dx2, r, w2, name):
    d = dx2.shape[1]
    dff = r.shape[1]

    def body(dx_ref, r_ref, w_ref, du_ref, dxb_ref):
        dxb = dx_ref[...].astype(BF16)
        dxb_ref[...] = dxb
        du_ref[...] = (_dot_nt(dxb, w_ref[...]) * (2.0 * r_ref[...].astype(F32))).astype(BF16)

    return _rowcall(name, body, [dx2, r], [w2], [(dff, BF16), (d, BF16)])


def _ff1_bwd(du, dx2, x1, g, w1, name, side=None):
    d = x1.shape[1]

    def body(du_ref, dx2_ref, x_ref, g_ref, w_ref, dx1_ref, dx1b_ref, dg_ref):
        _acc_init(pl.program_id(0), dg_ref)
        dx, dg = _rms_bwd(_dot_cols_t(du_ref[...], w_ref), x_ref[...], g_ref[...])
        dx1 = dx2_ref[...] + dx
        dx1_ref[...] = dx1
        dx1b_ref[...] = dx1.astype(BF16)
        dg_ref[...] += dg

    return _rowcall(name, body, [du, dx2, x1], [g, w1], [(d, F32), (d, BF16)], [((1, d), F32)], side=side)


def _mixer_bwd(dx1b, gates, oa, ob, wo, wua, wub, name, side=None):
    d = dx1b.shape[1]

    def body(dx_ref, gate_ref, oa_ref, ob_ref, wo_ref, wua_ref, wub_ref,
             dya_ref, dyb_ref, dgate_ref, doa_ref, dob_ref):
        dm = _dot_nt(dx_ref[...], wo_ref[...])
        sa = _sigmoid(gate_ref[:, :d])
        sb = _sigmoid(gate_ref[:, d:])
        ya = _dot_cols(oa_ref[...], wua_ref)
        yb = _dot_cols(ob_ref[...], wub_ref)
        dya = (dm * sa).astype(BF16)
        dyb = (dm * sb).astype(BF16)
        dya_ref[...] = dya
        dyb_ref[...] = dyb
        dgate_ref[:, :d] = (dm * ya * sa * (1.0 - sa)).astype(BF16)
        dgate_ref[:, d:] = (dm * yb * sb * (1.0 - sb)).astype(BF16)
        doa_ref[...] = _dot_cols_t(dya, wua_ref).astype(BF16)
        dob_ref[...] = _dot_cols_t(dyb, wub_ref).astype(BF16)

    return _rowcall(name, body, [dx1b, gates, oa, ob], [wo, wua, wub],
                    [(d, BF16), (d, BF16), (2 * d, BF16), (SB_W, BF16), (SW_QW, BF16)], side=side)


def _inproj_bwd(pieces, dx1, x, g, wt, name, side=None):
    d = x.shape[1]
    n = len(pieces)
    offsets = [sum(pc.shape[1] for pc in pieces[:p]) for p in range(n + 1)]

    def body(*refs):
        dx1_ref, x_ref, g_ref, w_ref, dx_ref, dg_ref = refs[n:]
        _acc_init(pl.program_id(0), dg_ref)
        dh = _dot(refs[0][...], w_ref[:offsets[1], :])
        for p in range(1, n):
            dh = dh + _dot(refs[p][...], w_ref[offsets[p]:offsets[p + 1], :])
        dx, dg = _rms_bwd(dh, x_ref[...], g_ref[...])
        dx_ref[...] = dx1_ref[...] + dx
        dg_ref[...] += dg

    return _rowcall(name, body, list(pieces) + [dx1, x], [g, wt], [(d, F32)], [((1, d), F32)], side=side)


def _tile(n, cap):
    assert n % LANES == 0
    return max(t for t in range(LANES, min(n, cap) + 1, LANES) if n % t == 0)


def _mm_tn(a, b, name, nshard=1):
    s, ka = a.shape
    nb = b.shape[1]
    n = nb // nshard
    ta = _tile(ka, 512)
    tb = _tile(n, 1024)
    per = n // tb

    def body(a_ref, b_ref, o_ref):
        o_ref[...] = _dot_tn(a_ref[...].astype(BF16), b_ref[...].astype(BF16))

    return pl.pallas_call(
        body, out_shape=jax.ShapeDtypeStruct((nshard, ka, n), F32), grid=(nb // tb, ka // ta),
        in_specs=[pl.BlockSpec((s, ta), lambda jb, ia: (0, ia)), pl.BlockSpec((s, tb), lambda jb, ia: (0, jb))],
        out_specs=pl.BlockSpec((None, ta, tb), lambda jb, ia: (jb // per, ia, jb % per)),
        compiler_params=_params(2), name=name)(a, b)


def _mm_tn_pieces(pieces, b, name, side=None):
    s, nb = b.shape
    ta = 256
    n_in = len(pieces) + 1
    tiles = [pc.shape[1] // ta for pc in pieces]
    assert all(pc.shape[1] % ta == 0 for pc in pieces)
    starts = [sum(tiles[:p]) for p in range(len(pieces))]
    tb = _tile(nb, 1024)
    grid = (nb // tb, sum(tiles))

    def body(*refs):
        a_refs, b_ref = refs[:n_in - 1], refs[n_in - 1]
        side_in, (o_ref,), side_out, sems = _side_refs(side, refs[n_in:], 1)
        jb, ia = pl.program_id(0), pl.program_id(1)
        if side is not None:
            @pl.when((jb == 0) & (ia == 0))
            def _():
                side.start(side_in, side_out, *sems)
        for p in range(len(pieces)):
            @pl.when((ia >= starts[p]) & (ia < starts[p] + tiles[p]))
            def _(p=p):
                o_ref[...] = _dot_tn(a_refs[p][...], b_ref[...])
        if side is not None:
            @pl.when((jb == grid[0] - 1) & (ia == grid[1] - 1))
            def _():
                side.finish(side_in, side_out, *sems)

    def piece_spec(p):
        return pl.BlockSpec((s, ta), lambda jb, ia: (0, jnp.clip(ia - starts[p], 0, tiles[p] - 1)))

    s_in, s_shape, s_out, s_alias, s_sems = _side_specs(side, n_in, 1)
    outs = pl.pallas_call(
        body, out_shape=[jax.ShapeDtypeStruct((sum(tiles) * ta, nb), F32)] + s_shape, grid=grid,
        in_specs=[piece_spec(p) for p in range(len(pieces))] + [pl.BlockSpec((s, tb), lambda jb, ia: (0, jb))] + s_in,
        out_specs=[pl.BlockSpec((ta, tb), lambda jb, ia: (ia, jb))] + s_out,
        input_output_aliases=s_alias, scratch_shapes=s_sems,
        compiler_params=_params(2), name=name)(*pieces, b, *(side.arrays if side else ()))
    return outs[0] if side is None else (outs[0], outs[1:])


def _place():
    return lax.axis_index("x"), lax.axis_index("y"), lax.axis_index("c")


def _chip_peer(x, y, k):
    return (x ^ (k >> 1), y ^ (k & 1))


def _row_tile(k, cap=544):
    return max(t for t in range(32, min(k, cap) + 1, 32) if k % t == 0)


def _cast_bf16(w, r, name):
    l, k, n = w.shape
    assert l == 2
    tk = _row_tile(k)

    def body(r_ref, w_ref, o0_ref, o1_ref):
        o0_ref[...] = w_ref[0].astype(BF16)
        o1_ref[...] = w_ref[1].astype(BF16)

    out_spec = pl.BlockSpec((None, tk, n), lambda i, r_ref: (r_ref[0], i, 0))
    return pl.pallas_call(
        body, out_shape=[jax.ShapeDtypeStruct((4, k, n), BF16)] * 2,
        grid_spec=pltpu.PrefetchScalarGridSpec(
            num_scalar_prefetch=1, grid=(k // tk,),
            in_specs=[pl.BlockSpec((l, tk, n), lambda i, r_ref: (0, i, 0))],
            out_specs=[out_spec, out_spec]),
        compiler_params=_params(1), name=name)(r, w)


class _Exchange(NamedTuple):
    arrays: tuple
    aliased: tuple
    sems: tuple
    start: Callable
    finish: Callable


def _all_gather(bufs):
    nt = len(bufs)

    def ici(t, ins, outs, send_sems, recv_sems, x, y, c, r, k):
        return pltpu.make_async_remote_copy(
            src_ref=ins[t].at[r, c], dst_ref=outs[t].at[r, c], send_sem=send_sems.at[t, k - 1],
            recv_sem=recv_sems.at[t, k - 1], device_id=(*_chip_peer(x, y, k), c), device_id_type=MESH)

    def d2d(t, outs, send_sems, recv_sems, x, y, c, r, k, half):
        slab = outs[t].at[r ^ k, half]
        return pltpu.make_async_remote_copy(
            src_ref=slab, dst_ref=slab, send_sem=send_sems.at[t, 2 + k], recv_sem=recv_sems.at[t, 2 + k],
            device_id=(x, y, 1 - c), device_id_type=MESH)

    def start(ins, outs, send_sems, recv_sems):
        x, y, c = _place()
        r = 2 * x + y
        for t in range(nt):
            for k in (1, 2, 3):
                ici(t, ins, outs, send_sems, recv_sems, x, y, c, r, k).start()

    def finish(ins, outs, send_sems, recv_sems):
        x, y, c = _place()
        r = 2 * x + y
        for t in range(nt):
            for k in (1, 2, 3):
                slab = outs[t].at[r ^ k, c]
                pltpu.make_async_remote_copy(
                    src_ref=slab, dst_ref=slab, send_sem=send_sems.at[t, k - 1], recv_sem=recv_sems.at[t, k - 1],
                    device_id=(x, y, 1 - c), device_id_type=MESH).wait_recv()
                d2d(t, outs, send_sems, recv_sems, x, y, c, r, k, c).start()
        for t in range(nt):
            for k in (1, 2, 3):
                d2d(t, outs, send_sems, recv_sems, x, y, c, r, k, 1 - c).wait_recv()
        for t in range(nt):
            for k in (1, 2, 3):
                ici(t, ins, outs, send_sems, recv_sems, x, y, c, r, k).wait_send()
                d2d(t, outs, send_sems, recv_sems, x, y, c, r, k, c).wait_send()

    return _Exchange(tuple(bufs), tuple(range(nt)), (nt, 6), start, finish)


def _run_exchange(name, ex):
    n_in, n_out = len(ex.arrays), len(ex.aliased)

    def body(*refs):
        ins, outs = refs[:n_in], refs[n_in:n_in + n_out]
        send_sems, recv_sems = refs[n_in + n_out:]
        ex.start(ins, outs, send_sems, recv_sems)
        ex.finish(ins, outs, send_sems, recv_sems)

    any_spec = pl.BlockSpec(memory_space=pl.ANY)
    return pl.pallas_call(
        body, out_shape=[jax.ShapeDtypeStruct(ex.arrays[a].shape, ex.arrays[a].dtype) for a in ex.aliased],
        in_specs=[any_spec] * n_in, out_specs=[any_spec] * n_out,
        input_output_aliases={a: o for o, a in enumerate(ex.aliased)},
        scratch_shapes=[pltpu.SemaphoreType.DMA(ex.sems), pltpu.SemaphoreType.DMA(ex.sems)],
        name=name)(*ex.arrays)


def _rs_to_sibling(grads):
    nt = len(grads)
    landing = [lax.empty((4,) + g.shape[2:], F32) for g in grads]

    def copies(ins, outs, send_sems, recv_sems):
        x, y, c = _place()
        return [pltpu.make_async_remote_copy(
            src_ref=ins[t].at[:, 1 - c], dst_ref=outs[t], send_sem=send_sems.at[t], recv_sem=recv_sems.at[t],
            device_id=(x, y, 1 - c), device_id_type=MESH) for t in range(nt)]

    def start(ins, outs, send_sems, recv_sems):
        for cp in copies(ins, outs, send_sems, recv_sems):
            cp.start()

    def finish(ins, outs, send_sems, recv_sems):
        for cp in copies(ins, outs, send_sems, recv_sems):
            cp.wait()

    return _Exchange(tuple(grads) + tuple(landing), tuple(range(nt, 2 * nt)), (nt,), start, finish)


def _add_half(g, recv, cr, name):
    _, _, k2, n = g.shape
    tk = _row_tile(k2)

    def body(cr_ref, g_ref, r_ref, sums_ref, mine_ref):
        val = (g_ref[...] + r_ref[...]).astype(BF16)
        sums_ref[...] = val

        @pl.when(pl.program_id(1) == cr_ref[1])
        def _():
            mine_ref[...] = val

    return pl.pallas_call(
        body, out_shape=[jax.ShapeDtypeStruct((4, k2, n), BF16)] * 2,
        grid_spec=pltpu.PrefetchScalarGridSpec(
            num_scalar_prefetch=1, grid=(k2 // tk, 4),
            in_specs=[pl.BlockSpec((None, None, tk, n), lambda i, q, cr_ref: (q, cr_ref[0], i, 0)),
                      pl.BlockSpec((None, tk, n), lambda i, q, cr_ref: (q, i, 0))],
            out_specs=[pl.BlockSpec((None, tk, n), lambda i, q, cr_ref: (q, i, 0)),
                       pl.BlockSpec((None, tk, n), lambda i, q, cr_ref: (cr_ref[1], i, 0))]),
        compiler_params=_params(2), name=name)(cr, g, recv)


def _rs_to_chips(sums, parts):
    nt = len(sums)

    def copies(ins, outs, send_sems, recv_sems):
        x, y, c = _place()
        r = 2 * x + y
        return [pltpu.make_async_remote_copy(
            src_ref=ins[t].at[r ^ k], dst_ref=outs[t].at[r], send_sem=send_sems.at[t, k - 1],
            recv_sem=recv_sems.at[t, k - 1], device_id=(*_chip_peer(x, y, k), c), device_id_type=MESH)
            for t in range(nt) for k in (1, 2, 3)]

    def start(ins, outs, send_sems, recv_sems):
        for cp in copies(ins, outs, send_sems, recv_sems):
            cp.start()

    def finish(ins, outs, send_sems, recv_sems):
        for cp in copies(ins, outs, send_sems, recv_sems):
            cp.wait()

    return _Exchange(tuple(sums) + tuple(parts), tuple(range(nt, 2 * nt)), (nt, 3), start, finish)


def _sum4(parts, cr, name):
    _, k2, n = parts.shape
    tk = _row_tile(k2)

    def body(cr_ref, p_ref, o_ref):
        p = p_ref[...].astype(F32)
        o_ref[...] = ((p[0] + p[1]) + p[2]) + p[3]

    return pl.pallas_call(
        body, out_shape=jax.ShapeDtypeStruct((2, k2, n), F32),
        grid_spec=pltpu.PrefetchScalarGridSpec(
            num_scalar_prefetch=1, grid=(k2 // tk,),
            in_specs=[pl.BlockSpec((4, tk, n), lambda i, cr_ref: (0, i, 0))],
            out_specs=pl.BlockSpec((None, tk, n), lambda i, cr_ref: (cr_ref[0], i, 0))),
        compiler_params=_params(1), name=name)(cr, parts)


def _exchange_halves(both):
    nt = len(both)

    def copies(ins, outs, send_sems, recv_sems):
        x, y, c = _place()
        return [pltpu.make_async_remote_copy(
            src_ref=ins[t].at[c], dst_ref=outs[t].at[c], send_sem=send_sems.at[t], recv_sem=recv_sems.at[t],
            device_id=(x, y, 1 - c), device_id_type=MESH) for t in range(nt)]

    def start(ins, outs, send_sems, recv_sems):
        for cp in copies(ins, outs, send_sems, recv_sems):
            cp.start()

    def finish(ins, outs, send_sems, recv_sems):
        for cp in copies(ins, outs, send_sems, recv_sems):
            cp.wait()

    return _Exchange(tuple(both), tuple(range(nt)), (nt,), start, finish)


def _adamw_math(w, g, m, v):
    m = ADAM_B1 * m + (1.0 - ADAM_B1) * g
    v = ADAM_B2 * v + (1.0 - ADAM_B2) * jnp.square(g)
    m_hat = m / (1.0 - ADAM_B1 ** ADAM_STEP)
    v_hat = v / (1.0 - ADAM_B2 ** ADAM_STEP)
    delta = -ADAM_LR * (m_hat / (jnp.sqrt(v_hat) + ADAM_EPS) + ADAM_WD * w)
    return delta, m, v


def _adamw(w, m, v, g0, g1, name):
    _, k, n = w.shape
    tk = _row_tile(k)
    nk = k // tk

    def body(w_ref, m_ref, v_ref, g0_ref, g1_ref, grad_ref, delta_ref, nm_ref, nv_ref):
        g = jnp.where(pl.program_id(0) == 0, g0_ref[...], g1_ref[...])
        delta, nm, nv = _adamw_math(w_ref[...], g, m_ref[...], v_ref[...])
        grad_ref[...] = g
        delta_ref[...] = delta
        nm_ref[...] = nm
        nv_ref[...] = nv

    lay = pl.BlockSpec((None, tk, n), lambda a, i: (a, i, 0))
    g0_spec = pl.BlockSpec((tk, n), lambda a, i: (jnp.where(a == 0, i, nk - 1), 0))
    g1_spec = pl.BlockSpec((tk, n), lambda a, i: (jnp.where(a == 1, i, 0), 0))
    return pl.pallas_call(
        body, out_shape=[jax.ShapeDtypeStruct(w.shape, F32)] * 4, grid=(2, nk),
        in_specs=[lay, lay, lay, g0_spec, g1_spec], out_specs=[lay] * 4,
        compiler_params=_params(2), name=name)(w, m, v, g0, g1)


def _small_allreduce_adamw(gpart, w, m, v):
    shape = gpart.shape

    def body(g_ref, w_ref, m_ref, v_ref, gsum_ref, delta_ref, nm_ref, nv_ref, recv_ref, send_sems, recv_sems):
        x, y, c = _place()
        me = 4 * x + 2 * y + c
        recv_ref[me] = g_ref[...]
        cps = []
        for k in range(1, 8):
            peer = (x ^ (k >> 2), y ^ ((k >> 1) & 1), c ^ (k & 1))
            cp = pltpu.make_async_remote_copy(
                src_ref=g_ref, dst_ref=recv_ref.at[me], send_sem=send_sems.at[k - 1], recv_sem=recv_sems.at[k - 1],
                device_id=peer, device_id_type=MESH)
            cp.start()
            cps.append(cp)
        for cp in cps:
            cp.wait()
        g = recv_ref[0]
        for dev in range(1, 8):
            g = g + recv_ref[dev]
        delta, nm, nv = _adamw_math(w_ref[...], g, m_ref[...], v_ref[...])
        gsum_ref[...] = g
        delta_ref[...] = delta
        nm_ref[...] = nm
        nv_ref[...] = nv

    vm = pl.BlockSpec(memory_space=pltpu.VMEM)
    return pl.pallas_call(
        body, out_shape=[jax.ShapeDtypeStruct(shape, F32)] * 4, in_specs=[vm] * 4, out_specs=[vm] * 4,
        scratch_shapes=[pltpu.VMEM((8,) + shape, F32), pltpu.SemaphoreType.DMA((7,)), pltpu.SemaphoreType.DMA((7,))],
        name="small_allreduce_adamw")(gpart, w, m, v)


BIG = ("w_in", "w_up_a", "w_up_b", "w_o", "w_ff1", "w_ff2", "w_pe", "w_pg")
COL_SHARDED = ("w_in", "w_up_a", "w_up_b", "w_ff1", "w_pe")
ROW_SHARDED = ("w_o", "w_ff2", "w_pg")
SMALL_ROWS = 16


def _pack_small(g_mix, g_mlp, g_pe, g_final, sinks, rel_bias, loss=None):
    d = g_final.shape[0]
    row = lambda v: jnp.pad(v.reshape(1, -1), ((0, 0), (0, d - v.size)))
    rows = [g_mix, g_mlp, g_pe, g_final.reshape(1, d),
            jnp.zeros((1, d), F32) if loss is None else row(loss), row(sinks), row(rel_bias)]
    out = jnp.concatenate(rows, axis=0)
    return jnp.pad(out, ((0, SMALL_ROWS - out.shape[0]), (0, 0)))


def _unpack_small(a, sinks_shape, rel_shape):
    return (a[0:2], a[2:4], a[4:6], a[6], a[8, :sinks_shape[0] * sinks_shape[1]].reshape(sinks_shape),
            a[9, :rel_shape[0] * rel_shape[1]].reshape(rel_shape))


def kernel(x, p, w_in, w_up_a, w_up_b, w_o, w_ff1, w_ff2, w_pe, w_pg, g_mix, g_mlp, g_pe, g_final, sinks, rel_bias, loss_target, m_w_in, m_w_up_a, m_w_up_b, m_w_o, m_w_ff1, m_w_ff2, m_w_pe, m_w_pg, m_g_mix, m_g_mlp, m_g_pe, m_g_final, m_sinks, m_rel_bias, v_w_in, v_w_up_a, v_w_up_b, v_w_o, v_w_ff1, v_w_ff2, v_w_pe, v_w_pg, v_g_mix, v_g_mlp, v_g_pe, v_g_final, v_sinks, v_rel_bias):
    depth = w_in.shape[0]
    assert depth == 2
    x0 = x[0]
    target = loss_target[0]
    d = x0.shape[1]
    wl = dict(w_in=w_in, w_up_a=w_up_a, w_up_b=w_up_b, w_o=w_o, w_ff1=w_ff1, w_ff2=w_ff2, w_pe=w_pe, w_pg=w_pg)
    ml = dict(w_in=m_w_in, w_up_a=m_w_up_a, w_up_b=m_w_up_b, w_o=m_w_o, w_ff1=m_w_ff1, w_ff2=m_w_ff2, w_pe=m_w_pe, w_pg=m_w_pg)
    vl = dict(w_in=v_w_in, w_up_a=v_w_up_a, w_up_b=v_w_up_b, w_o=v_w_o, w_ff1=v_w_ff1, w_ff2=v_w_ff2, w_pe=v_w_pe, w_pg=v_w_pg)
    c_idx = lax.axis_index("c").astype(jnp.int32)
    r_idx = (2 * lax.axis_index("x") + lax.axis_index("y")).astype(jnp.int32)
    cr = jnp.stack([c_idx, r_idx])

    wl["w_in"], ml["w_in"], vl["w_in"] = (jnp.swapaxes(a, 1, 2) for a in (w_in, m_w_in, v_w_in))

    bufs = {}
    for n in BIG:
        k, nn = wl[n].shape[1:]
        for l, b in enumerate(_cast_bf16(wl[n], r_idx.reshape(1), "cast_" + n)):
            bufs[n, l] = b.reshape(4, 2, k // 2, nn)

    def gather(keys, run):
        for key, b in zip(keys, run(_all_gather([bufs[key] for key in keys]))):
            bufs[key] = b

    def gathered(n, l):
        _, _, k2, nn = bufs[n, l].shape
        if n in ROW_SHARDED or n == "w_in":
            return bufs[n, l].reshape(8 * k2, nn)
        return bufs[n, l].reshape(4, 2 * k2, nn)

    gather([("w_in", 0)], lambda ex: _run_exchange("all_gather_first", ex))

    full = {n: [None] * depth for n in BIG}
    saved = []
    xi = x0
    for i in range(depth):
        st = dict(x0=xi)
        gm = g_mix[i].reshape(1, d)
        full["w_in"][i] = gathered("w_in", i)
        st["h1"], st["qkv"], st["gates"] = _inproj_fwd(xi, gm, full["w_in"][i], f"inproj_fwd_{i}")

        def attend(ex):
            (st["oa"], st["lt"], st["nb"], st["a_wide"], st["sg_wide"]), filled = _sb_fwd(st["qkv"], f"sb_fwd_{i}", ex)
            return filled

        gather([(n, i) for n in BIG if n != "w_in"], attend)
        for n in BIG:
            if n != "w_in":
                full[n][i] = gathered(n, i)
        st["ob"] = _swa_fwd(st["qkv"], sinks[i], rel_bias, f"swa_fwd_{i}")
        st["m"], st["x1"] = _mixer_fwd(st["oa"], st["ob"], st["gates"], xi, full["w_up_a"][i], full["w_up_b"][i],
                                       full["w_o"][i], f"mixer_fwd_{i}")
        if i == 0:
            def feed_forward(ex):
                (st["h2"], st["u"], st["a"]), filled = _ff1_fwd(st["x1"], g_mlp[i].reshape(1, d), full["w_ff1"][i],
                                                               f"ff1_fwd_{i}", ex)
                return filled

            gather([("w_in", 1)], feed_forward)
        else:
            st["h2"], st["u"], st["a"] = _ff1_fwd(st["x1"], g_mlp[i].reshape(1, d), full["w_ff1"][i], f"ff1_fwd_{i}")
        st["x2"] = _ff2_fwd(st["a"], st["x1"], full["w_ff2"][i], f"ff2_fwd_{i}")
        st["pb"], st["h3"], st["pe"], st["gt"], xi = _ple_fwd(p[i, 0], st["x2"], g_pe[i].reshape(1, d),
                                                            full["w_pe"][i], full["w_pg"][i], f"ple_fwd_{i}")
        saved.append(st)

    dx, dg_final, loss_part = _loss_bwd(xi, target, g_final.reshape(1, d), "loss_bwd")
    gw = {n: [None] * depth for n in BIG}
    reduced = {}

    chip_sums = {}

    def to_sibling(keys, run):
        tensors = []
        for n, l in keys:
            g = gw[n][l]
            if n in ROW_SHARDED:
                ka, nb = g.shape[1:]
                g = g.reshape(4, ka // 4, nb)
            _, k, nn = g.shape
            tensors.append(g.reshape(4, 2, k // 2, nn))
        for (n, l), g, r in zip(keys, tensors, run(_rs_to_sibling(tensors))):
            chip_sums[n, l] = _add_half(g, r, cr, f"add_half_{n}_{l}")

    def to_chips(keys):
        return _rs_to_chips([chip_sums[k][0] for k in keys], [chip_sums[k][1] for k in keys])

    halves = {}

    def sum_chips(keys, parts):
        for (n, l), pc in zip(keys, parts):
            halves[n, l] = _sum4(pc, cr, f"sum4_{n}_{l}")

    def swap_halves(keys):
        def store(filled):
            for key, both in zip(keys, filled):
                reduced[key] = both
        return _exchange_halves([halves[k] for k in keys]), store

    dg_mix, dg_mlp, dg_pe, dsinks = [None] * depth, [None] * depth, [None] * depth, [None] * depth
    drel = jnp.zeros((SW_HEADS, LANES), F32)
    for i in reversed(range(depth)):
        st = saved[i]
        dpe, dgt, dx2, dg_pe[i] = _ple_bwd(dx, st["pe"], st["gt"], st["x2"], g_pe[i].reshape(1, d),
                                           full["w_pg"][i], f"ple_bwd_{i}")
        gw["w_pe"][i] = _mm_tn(st["pb"], dpe, f"dw_pe_{i}", 4)
        gw["w_pg"][i] = _mm_tn(st["h3"], dgt, f"dw_pg_{i}")
        du, dx2b = _ff2_bwd(dx2, st["u"], full["w_ff2"][i], f"ff2_bwd_{i}")
        gw["w_ff2"][i] = _mm_tn(st["a"], dx2b, f"dw_ff2_{i}")
        gw["w_ff1"][i] = _mm_tn(st["h2"], du, f"dw_ff1_{i}", 4)
        if i == 0:
            (dx1, dx1b, dg_mlp[i]), parts = _ff1_bwd(du, dx2, st["x1"], g_mlp[i].reshape(1, d), full["w_ff1"][i],
                                                    f"ff1_bwd_{i}", to_chips([("w_in", 1)]))
            sum_chips([("w_in", 1)], parts)
        else:
            dx1, dx1b, dg_mlp[i] = _ff1_bwd(du, dx2, st["x1"], g_mlp[i].reshape(1, d), full["w_ff1"][i],
                                            f"ff1_bwd_{i}")
        gw["w_o"][i] = _mm_tn(st["m"], dx1b, f"dw_o_{i}")
        early = [(n, i) for n in ("w_pe", "w_pg", "w_ff2", "w_ff1", "w_o")]

        def mixer(ex):
            (dya, dyb, dgates, doa, dob), landed = _mixer_bwd(
                dx1b, st["gates"], st["oa"], st["ob"], full["w_o"][i], full["w_up_a"][i], full["w_up_b"][i],
                f"mixer_bwd_{i}", ex)
            st.update(dya=dya, dyb=dyb, dgates=dgates, doa=doa, dob=dob)
            return landed

        to_sibling(early, mixer)
        dgates = st["dgates"]
        gw["w_up_a"][i] = _mm_tn(st["oa"], st["dya"], f"dw_up_a_{i}", 4)
        gw["w_up_b"][i] = _mm_tn(st["ob"], st["dyb"], f"dw_up_b_{i}", 4)
        late = [("w_up_a", i), ("w_up_b", i)]
        to_sibling(late, lambda ex: _run_exchange(f"rs_to_sibling_{i}", ex))
        keys = early + late
        with_sb = [(n, i) for n in ("w_ff1", "w_o", "w_pg", "w_pe")]
        with_swa = [(n, i) for n in ("w_ff2", "w_up_a", "w_up_b")]
        dqa, dka, dva, parts = _sb_bwd(st["qkv"], st["lt"], st["nb"], st["a_wide"], st["sg_wide"], st["doa"],
                                       f"sb_bwd_{i}", to_chips(with_sb))
        sum_chips(with_sb, parts)
        (dqb, dkvb, dsk, drl), parts = _swa_bwd(st["qkv"], st["ob"], st["dob"], sinks[i], rel_bias, f"swa_bwd_{i}",
                                                to_chips(with_swa))
        sum_chips(with_swa, parts)
        dsinks[i] = dsk[:, 0]
        drel = drel + drl
        dproj = [dqa, dka, dva, dqb, dkvb, dgates]
        swap, store = swap_halves(keys + ([("w_in", 1)] if i == 0 else []))
        dw_in_t, filled = _mm_tn_pieces(dproj, st["h1"], f"dw_in_{i}", swap)
        store(filled)
        gw["w_in"][i] = dw_in_t.reshape(4, dw_in_t.shape[0] // 4, d)
        if i == 1:
            def inproj(ex):
                (dx, dg_mix[i]), landed = _inproj_bwd(dproj, dx1, st["x0"], g_mix[i].reshape(1, d),
                                                      full["w_in"][i], f"inproj_bwd_{i}", ex)
                st["dx"] = dx
                return landed

            to_sibling([("w_in", 1)], inproj)
            dx = st["dx"]
        else:
            keys = [("w_in", 0)]
            to_sibling(keys, lambda ex: _run_exchange("rs_to_sibling_last", ex))
            (dx, dg_mix[i]), parts = _inproj_bwd(dproj, dx1, st["x0"], g_mix[i].reshape(1, d),
                                                 full["w_in"][i], f"inproj_bwd_{i}", to_chips(keys))
            sum_chips(keys, parts)
            swap, store = swap_halves(keys)
            store(_run_exchange("exchange_halves_last", swap))
    grad_x = dx[None]

    outs = {}
    for n in BIG:
        g0, g1 = (reduced[n, l].reshape(wl[n].shape[1:]) for l in range(depth))
        outs[n] = _adamw(wl[n], ml[n], vl[n], g0, g1, "adamw_" + n)
    outs["w_in"] = [jnp.swapaxes(a, 1, 2) for a in outs["w_in"]]

    drel_bias = drel[:, :N_BUCKETS].T
    gsmall = _pack_small(jnp.concatenate(dg_mix, 0), jnp.concatenate(dg_mlp, 0), jnp.concatenate(dg_pe, 0),
                         dg_final[0], jnp.stack(dsinks), drel_bias, loss_part[0, :1])
    wsmall = _pack_small(g_mix, g_mlp, g_pe, g_final, sinks, rel_bias)
    msmall = _pack_small(m_g_mix, m_g_mlp, m_g_pe, m_g_final, m_sinks, m_rel_bias)
    vsmall = _pack_small(v_g_mix, v_g_mlp, v_g_pe, v_g_final, v_sinks, v_rel_bias)
    small = _small_allreduce_adamw(gsmall, wsmall, msmall, vsmall)
    loss = small[0][7, 0]
    small = [_unpack_small(a, sinks.shape, rel_bias.shape) for a in small]

    result = [loss, grad_x]
    for kind in range(4):
        result += [outs[n][kind] for n in BIG]
        result += list(small[kind])
    return tuple(result)
```

```python
import functools
import math
from typing import Callable, NamedTuple

import numpy as np
import jax
import jax.numpy as jnp
from jax import lax
from jax.experimental import pallas as pl
from jax.experimental.pallas import tpu as pltpu

F32 = jnp.float32
BF16 = jnp.bfloat16
MESH = pl.DeviceIdType.MESH

HEAD_DIM = 64
SB_HEADS = 8
SW_HEADS = 8
SW_KV_HEADS = 2
WINDOW = 128
N_BUCKETS = 32
MAX_DISTANCE = 128
EPS = 1e-6
SB_W = SB_HEADS * HEAD_DIM
SW_QW = SW_HEADS * HEAD_DIM
SW_KVW = SW_KV_HEADS * HEAD_DIM
QKV_W = 3 * SB_W + SW_QW + 2 * SW_KVW
SCALE = HEAD_DIM ** -0.5
assert SCALE == 0.125
LANES = 128
TQ = 128
BK = 128
NEG = -1e30
SB_EXHAUSTED = -106.0

ADAM_LR = 0.001
ADAM_B1 = 0.9
ADAM_B2 = 0.999
ADAM_EPS = 1e-08
ADAM_WD = 0.01
ADAM_STEP = 10

VMEM_LIMIT = 56 * 1024 * 1024


def _dot(a, b):
    return jnp.dot(a, b, preferred_element_type=F32)


def _dot_nt(a, b):
    return lax.dot_general(a, b, (((1,), (1,)), ((), ())), preferred_element_type=F32)


def _dot_tn(a, b):
    return lax.dot_general(a, b, (((0,), (0,)), ((), ())), preferred_element_type=F32)


def _sum_all(x):
    return jnp.sum(jnp.sum(x, axis=1, keepdims=True), axis=0, keepdims=True)


def _sigmoid(x):
    return 1.0 / (1.0 + jnp.exp(-x))


def _rms(x, g):
    r = lax.rsqrt(jnp.mean(x * x, axis=-1, keepdims=True) + EPS)
    return (x * r) * g


def _rms_bwd(dy, x, g):
    r = lax.rsqrt(jnp.mean(x * x, axis=-1, keepdims=True) + EPS)
    n = x * r
    dg = jnp.sum(dy * n, axis=0, keepdims=True)
    dn = dy * g
    dx = r * (dn - n * jnp.mean(dn * n, axis=-1, keepdims=True))
    return dx, dg


def _params(n_axes):
    return pltpu.CompilerParams(dimension_semantics=("arbitrary",) * n_axes, vmem_limit_bytes=VMEM_LIMIT)


def _rowcall(name, body, row_ins, const_ins, row_outs, acc_outs=(), tm=512, side=None):
    s = row_ins[0].shape[0]
    assert s % tm == 0
    in_specs = [pl.BlockSpec((tm, a.shape[1]), lambda i: (i, 0)) for a in row_ins]
    in_specs += [pl.BlockSpec(a.shape, functools.partial(lambda i, nd: (0,) * nd, nd=a.ndim)) for a in const_ins]
    out_shape = [jax.ShapeDtypeStruct((s, c), dt) for c, dt in row_outs]
    out_specs = [pl.BlockSpec((tm, c), lambda i: (i, 0)) for c, _ in row_outs]
    out_shape += [jax.ShapeDtypeStruct(sh, dt) for sh, dt in acc_outs]
    out_specs += [pl.BlockSpec(sh, functools.partial(lambda i, nd: (0,) * nd, nd=len(sh))) for sh, _ in acc_outs]
    if side is None:
        return pl.pallas_call(body, out_shape=out_shape, grid=(s // tm,), in_specs=in_specs, out_specs=out_specs,
                              compiler_params=_params(1), name=name)(*row_ins, *const_ins)
    n_in, n_out = len(in_specs), len(out_specs)

    def with_side(*refs):
        side_in, outs, side_out, sems = _side_refs(side, refs[n_in:], n_out)

        @pl.when(pl.program_id(0) == 0)
        def _():
            side.start(side_in, side_out, *sems)

        body(*refs[:n_in], *outs)

        @pl.when(pl.program_id(0) == s // tm - 1)
        def _():
            side.finish(side_in, side_out, *sems)

    s_in, s_shape, s_out, s_alias, s_sems = _side_specs(side, n_in, n_out)
    outs = pl.pallas_call(with_side, out_shape=out_shape + s_shape, grid=(s // tm,), in_specs=in_specs + s_in,
                          out_specs=out_specs + s_out, input_output_aliases=s_alias, scratch_shapes=s_sems,
                          compiler_params=_params(1), name=name)(*row_ins, *const_ins, *side.arrays)
    return outs[:n_out], outs[n_out:]


def _dot_cols(a, w_ref):
    return jnp.concatenate([_dot(a, w_ref[r]) for r in range(w_ref.shape[0])], axis=1)


def _dot_cols_t(a, w_ref):
    n = w_ref.shape[2]
    out = _dot_nt(a[:, :n], w_ref[0])
    for r in range(1, w_ref.shape[0]):
        out = out + _dot_nt(a[:, r * n:(r + 1) * n], w_ref[r])
    return out


def _inproj_fwd(x, g, wt, name):
    d = x.shape[1]

    def body(x_ref, g_ref, w_ref, h_ref, qkv_ref, gate_ref):
        hb = _rms(x_ref[...], g_ref[...]).astype(BF16)
        h_ref[...] = hb
        qkv_ref[...] = _dot_nt(hb, w_ref[:QKV_W, :]).astype(BF16)
        gate_ref[...] = _dot_nt(hb, w_ref[QKV_W:, :])

    return _rowcall(name, body, [x], [g, wt], [(d, BF16), (QKV_W, BF16), (2 * d, F32)])


def _mixer_fwd(oa, ob, gates, x, wua, wub, wo, name):
    d = x.shape[1]

    def body(oa_ref, ob_ref, gate_ref, x_ref, wua_ref, wub_ref, wo_ref, m_ref, x1_ref):
        ya = _dot_cols(oa_ref[...], wua_ref)
        yb = _dot_cols(ob_ref[...], wub_ref)
        m = _sigmoid(gate_ref[:, :d]) * ya + _sigmoid(gate_ref[:, d:]) * yb
        mb = m.astype(BF16)
        m_ref[...] = mb
        x1_ref[...] = x_ref[...] + _dot(mb, wo_ref[...])

    return _rowcall(name, body, [oa, ob, gates, x], [wua, wub, wo], [(d, BF16), (d, F32)])


def _ff1_fwd(x1, g, w1, name, side=None):
    _, d, nq = w1.shape
    dff = 4 * nq

    def body(x_ref, g_ref, w_ref, h_ref, r_ref, a_ref):
        hb = _rms(x_ref[...], g_ref[...]).astype(BF16)
        h_ref[...] = hb
        r = jnp.maximum(_dot_cols(hb, w_ref), 0.0)
        r_ref[...] = r.astype(BF16)
        a_ref[...] = jnp.square(r).astype(BF16)

    return _rowcall(name, body, [x1], [g, w1], [(d, BF16), (dff, BF16), (dff, BF16)], side=side)


def _ff2_fwd(a, x1, w2, name):
    d = x1.shape[1]

    def body(a_ref, x_ref, w_ref, o_ref):
        o_ref[...] = x_ref[...] + _dot(a_ref[...], w_ref[...])

    return _rowcall(name, body, [a, x1], [w2], [(d, F32)])[0]


def _ple_fwd(p, x2, g, wpe, wpg, name):
    d = x2.shape[1]

    def body(p_ref, x_ref, g_ref, wpe_ref, wpg_ref, pb_ref, h_ref, pe_ref, gt_ref, x3_ref):
        pb = p_ref[...].astype(BF16)
        pb_ref[...] = pb
        pe = _dot_cols(pb, wpe_ref)
        x = x_ref[...]
        hb = _rms(x, g_ref[...]).astype(BF16)
        h_ref[...] = hb
        gt = _dot(hb, wpg_ref[...])
        pe_ref[...] = pe
        gt_ref[...] = gt
        x3_ref[...] = x + pe * _sigmoid(gt)

    return _rowcall(name, body, [p, x2], [g, wpe, wpg],
                    [(p.shape[1], BF16), (d, BF16), (d, F32), (d, F32), (d, F32)])


def _pair_stack(t, lane):
    zero = jnp.zeros_like(t)
    return jnp.concatenate([jnp.where(lane < HEAD_DIM, t, zero), jnp.where(lane >= HEAD_DIM, t, zero)], axis=0)


def _sb_rel():
    row = lax.broadcasted_iota(jnp.int32, (2 * TQ, BK), 0)
    row = jnp.where(row >= TQ, row - TQ, row)
    col = lax.broadcasted_iota(jnp.int32, (2 * TQ, BK), 1)
    return col - row


def _split_dot(x, m01, two_pass=True):
    hi = x.astype(BF16)
    if not two_pass:
        return _dot(hi, m01)
    lo = (x - hi.astype(F32)).astype(BF16)
    return _dot(hi, m01) + _dot(lo, m01)


def _sb_scores(qs, k, mask):
    z = _dot_nt(qs, k)
    lb = jnp.minimum(z, 0.0) - jnp.log(1.0 + jnp.exp(-jnp.abs(z)))
    lm = lb - z
    return lb, lm if mask is None else jnp.where(mask, lm, 0.0)


SB_STRAIGHT = 3
SB_WIDE = SB_STRAIGHT * BK
SB_QB = 2
SWA_QB = 4


def _sb_wide_consts():
    j = np.arange(BK)[:, None]
    s = np.arange(BK)[None, :]
    ones = np.ones((BK, BK), np.float32)
    as_bf16 = lambda m: jnp.asarray(np.concatenate([m, ones], axis=1).astype(np.float32), dtype=BF16)
    return as_bf16(j > s), as_bf16(j <= s), as_bf16(j < s)


def _wide_sums(x, m01, suffix, two_pass=True):
    parts = [_split_dot(x[:, b * BK:(b + 1) * BK], m01, two_pass) for b in range(SB_STRAIGHT)]
    order = range(SB_STRAIGHT - 1, -1, -1) if suffix else range(SB_STRAIGHT)
    out = [None] * SB_STRAIGHT
    carry = None
    for b in order:
        out[b] = parts[b][:, :BK] if carry is None else parts[b][:, :BK] + carry
        carry = parts[b][:, BK:] if carry is None else carry + parts[b][:, BK:]
    return jnp.concatenate(out, axis=1), carry


def _side_refs(ex, rest, n_out):
    n_in = len(ex.arrays) if ex else 0
    n_alias = len(ex.aliased) if ex else 0
    ins, rest = rest[:n_in], rest[n_in:]
    outs, rest = rest[:n_out], rest[n_out:]
    return ins, outs, rest[:n_alias], rest[n_alias:]


def _side_specs(ex, n_in, n_out):
    if ex is None:
        return [], [], [], {}, []
    any_spec = pl.BlockSpec(memory_space=pl.ANY)
    return ([any_spec] * len(ex.arrays),
            [jax.ShapeDtypeStruct(ex.arrays[a].shape, ex.arrays[a].dtype) for a in ex.aliased],
            [any_spec] * len(ex.aliased), {n_in + a: n_out + o for o, a in enumerate(ex.aliased)},
            [pltpu.SemaphoreType.DMA(ex.sems), pltpu.SemaphoreType.DMA(ex.sems)])


def _sb_fwd(qkv, name, side=None):
    s = qkv.shape[0]
    nq = s // TQ
    npair = SB_W // LANES
    sufw = _sb_wide_consts()[0]

    def body(q_ref, k_ref, v_ref, sufw_ref, *rest):
        side_in, (o_ref, lt_ref, nb_ref, a_ref, sg_ref), side_out, scratch = _side_refs(side, rest, 5)
        cf_ref, acc_ref = scratch[:2]
        step_id = pl.program_id(1)
        if side is not None:
            @pl.when((pl.program_id(0) == 0) & (step_id == 0))
            def _():
                side.start(side_in, side_out, *scratch[2:])
        lane = lax.broadcasted_iota(jnp.int32, (TQ, LANES), 1)
        rel = _sb_rel()
        blocks = [step_id * SB_QB + b for b in range(SB_QB)]
        qs = [_pair_stack(q_ref[b * TQ:(b + 1) * TQ, :] * SCALE, lane) for b in range(SB_QB)]

        straight = blocks[0] >= SB_STRAIGHT - 1

        @pl.when(straight)
        def _():
            for b, i in enumerate(blocks):
                w0 = pl.multiple_of((i - (SB_STRAIGHT - 1)) * BK, BK)
                kw = k_ref[pl.ds(w0, SB_WIDE), :]
                lb, lm = _sb_scores(qs[b], kw, None)
                own = rel < 0
                past = SB_WIDE - BK
                lm = jnp.concatenate([lm[:, :past], jnp.where(own, lm[:, past:], 0.0)], axis=1)
                after, total = _wide_sums(lm, sufw_ref[...], True)
                on_past_keys = lambda t: jnp.concatenate([t[:, :past], jnp.where(own, t[:, past:], 0.0)], axis=1)
                a = on_past_keys(jnp.exp(lb + after)).astype(BF16)
                acc_ref[b] = _dot(a, v_ref[pl.ds(w0, SB_WIDE), :])
                cf_ref[b] = total
                a_ref[b] = a
                sg_ref[b] = on_past_keys(jnp.exp(lb)).astype(BF16)

        @pl.when(jnp.logical_not(straight))
        def _():
            cf_ref[...] = jnp.zeros_like(cf_ref)
            acc_ref[...] = jnp.zeros_like(acc_ref)

        for b, i in enumerate(blocks):
            q0 = i * TQ

            def more(c, i=i):
                return (c[0] <= i) & (c[1] > SB_EXHAUSTED)

            def step(c, b=b, i=i, q0=q0):
                k0 = pl.multiple_of((i - c[0]) * BK, BK)
                k = k_ref[pl.ds(k0, BK), :]
                v = v_ref[pl.ds(k0, BK), :]
                mask = rel < (q0 - k0)
                lb, lm = _sb_scores(qs[b], k, mask)
                cs = _split_dot(lm, sufw_ref[...])
                a = jnp.where(mask, jnp.exp(lb + cs[:, :BK] + cf_ref[b]), 0.0)
                acc_ref[b] += _dot(a.astype(BF16), v)
                cf = cf_ref[b] + cs[:, BK:]
                cf_ref[b] = cf
                return c[0] + 1, jnp.max(cf)

            n_blocks, _ = lax.while_loop(
                more, step, (jnp.where(straight, SB_STRAIGHT, 0).astype(jnp.int32), jnp.max(cf_ref[b])))
            o_ref[b * TQ:(b + 1) * TQ, :] = jnp.where(lane < HEAD_DIM, acc_ref[b, :TQ, :],
                                                     acc_ref[b, TQ:, :]).astype(BF16)
            lt_ref[b] = cf_ref[b]
            nb_ref[b] = jnp.full(nb_ref.shape[1:], n_blocks, F32)
        if side is not None:
            @pl.when((pl.program_id(0) == npair - 1) & (step_id == nq // SB_QB - 1))
            def _():
                side.finish(side_in, side_out, *scratch[2:])

    s_in, s_shape, s_out, s_alias, s_sems = _side_specs(side, 4, 5)
    wide = jax.ShapeDtypeStruct((npair, nq, 2 * TQ, SB_WIDE), BF16)
    wide_spec = pl.BlockSpec((None, SB_QB, 2 * TQ, SB_WIDE), lambda j, i: (j, i, 0, 0))
    outs = pl.pallas_call(
        body,
        out_shape=[jax.ShapeDtypeStruct((s, SB_W), BF16), jax.ShapeDtypeStruct((npair, nq, 2 * TQ, BK), F32),
                   jax.ShapeDtypeStruct((npair, nq, 8, LANES), F32), wide, wide] + s_shape,
        grid=(npair, nq // SB_QB),
        in_specs=[pl.BlockSpec((SB_QB * TQ, LANES), lambda j, i: (i, j)),
                  pl.BlockSpec((s, LANES), lambda j, i: (0, npair + j)),
                  pl.BlockSpec((s, LANES), lambda j, i: (0, 2 * npair + j)),
                  pl.BlockSpec(sufw.shape, lambda j, i: (0, 0))] + s_in,
        out_specs=[pl.BlockSpec((SB_QB * TQ, LANES), lambda j, i: (i, j)),
                   pl.BlockSpec((None, SB_QB, 2 * TQ, BK), lambda j, i: (j, i, 0, 0)),
                   pl.BlockSpec((None, SB_QB, 8, LANES), lambda j, i: (j, i, 0, 0)), wide_spec, wide_spec] + s_out,
        input_output_aliases=s_alias,
        scratch_shapes=[pltpu.VMEM((SB_QB, 2 * TQ, BK), F32), pltpu.VMEM((SB_QB, 2 * TQ, LANES), F32)] + s_sems,
        compiler_params=_params(2), name=name)(qkv, qkv, qkv, sufw, *(side.arrays if side else ()))
    return outs[:5], outs[5:]


def _sb_bwd(qkv, lt, nb, a_wide, sg_wide, doa, name, side=None):
    s = qkv.shape[0]
    nq = s // TQ
    npair = SB_W // LANES
    _, prew, prexw = _sb_wide_consts()

    def body(q_ref, k_ref, v_ref, lt_ref, nb_ref, do_ref, prew_ref, prexw_ref, a_ref, sg_ref, *rest):
        side_in, (dq_ref, dk_out, dv_out), side_out, scratch = _side_refs(side, rest, 3)
        cp_ref, ce_ref, dqa_ref, dk_ref, dv_ref = scratch[:5]
        sems = scratch[5:]
        step_id = pl.program_id(1)
        if side is not None:
            @pl.when((pl.program_id(0) == 0) & (step_id == 0))
            def _():
                side.start(side_in, side_out, *sems)
        lane = lax.broadcasted_iota(jnp.int32, (TQ, LANES), 1)
        rel = _sb_rel()
        blocks = [step_id * SB_QB + b for b in range(SB_QB)]
        rows = [slice(b * TQ, (b + 1) * TQ) for b in range(SB_QB)]
        qs = [_pair_stack(q_ref[rows[b], :] * SCALE, lane) for b in range(SB_QB)]
        dos = [_pair_stack(do_ref[rows[b], :], lane) for b in range(SB_QB)]
        n_blocks = [jnp.clip(jnp.max(nb_ref[b]).astype(jnp.int32), 1, i + 1) for b, i in enumerate(blocks)]
        first = [i + 1 - n for i, n in zip(blocks, n_blocks)]

        @pl.when(step_id == 0)
        def _():
            dk_ref[...] = jnp.zeros_like(dk_ref)
            dv_ref[...] = jnp.zeros_like(dv_ref)

        straight = n_blocks[0] == SB_STRAIGHT
        for n in n_blocks[1:]:
            straight = straight & (n == SB_STRAIGHT)

        @pl.when(straight)
        def _():
            for b in range(SB_QB):
                w0 = pl.multiple_of(first[b] * BK, BK)
                kw = k_ref[pl.ds(w0, SB_WIDE), :]
                vw = v_ref[pl.ds(w0, SB_WIDE), :]
                a = a_ref[b]
                e = a.astype(F32) * _dot_nt(dos[b], vw)
                big_e, _ = _wide_sums(e, prexw_ref[...], False, two_pass=False)
                dz = (e - sg_ref[b].astype(F32) * (e + big_e)).astype(BF16)
                dk_ref[pl.ds(w0, SB_WIDE), :] += _dot_tn(dz, qs[b])
                dv_ref[pl.ds(w0, SB_WIDE), :] += _dot_tn(a, dos[b])
                dqa_ref[b] = _dot(dz, kw)

        @pl.when(jnp.logical_not(straight))
        def _():
            cp_ref[...] = jnp.zeros_like(cp_ref)
            ce_ref[...] = jnp.zeros_like(ce_ref)
            dqa_ref[...] = jnp.zeros_like(dqa_ref)
            for b, i in enumerate(blocks):
                q0 = i * TQ

                def step(it, carry, b=b, q0=q0):
                    k0 = pl.multiple_of((first[b] + it) * BK, BK)
                    k = k_ref[pl.ds(k0, BK), :]
                    v = v_ref[pl.ds(k0, BK), :]
                    mask = rel < (q0 - k0)
                    lb, lm = _sb_scores(qs[b], k, mask)
                    cs = _split_dot(lm, prew_ref[...])
                    a = jnp.where(mask, jnp.exp(lb + (lt_ref[b] - (cs[:, :BK] + cp_ref[b]))), 0.0)
                    e = a * _dot_nt(dos[b], v)
                    ce = _split_dot(e, prexw_ref[...], two_pass=False)
                    big_e = ce[:, :BK] + ce_ref[b]
                    dz = jnp.where(mask, e - jnp.exp(lb) * (e + big_e), 0.0).astype(BF16)
                    dk_ref[pl.ds(k0, BK), :] += _dot_tn(dz, qs[b])
                    dv_ref[pl.ds(k0, BK), :] += _dot_tn(a.astype(BF16), dos[b])
                    dqa_ref[b] += _dot(dz, k)
                    cp_ref[b] += cs[:, BK:]
                    ce_ref[b] += ce[:, BK:]
                    return carry

                lax.fori_loop(0, n_blocks[b], step, 0)

        for b in range(SB_QB):
            dq = jnp.where(lane < HEAD_DIM, dqa_ref[b, :TQ, :], dqa_ref[b, TQ:, :])
            dq_ref[rows[b], :] = (dq * SCALE).astype(BF16)

        @pl.when(step_id == nq // SB_QB - 1)
        def _():
            dk_out[...] = dk_ref[...].astype(BF16)
            dv_out[...] = dv_ref[...].astype(BF16)

        if side is not None:
            @pl.when((pl.program_id(0) == npair - 1) & (step_id == nq // SB_QB - 1))
            def _():
                side.finish(side_in, side_out, *sems)

    s_in, s_shape, s_out, s_alias, s_sems = _side_specs(side, 10, 3)
    wide_spec = pl.BlockSpec((None, SB_QB, 2 * TQ, SB_WIDE), lambda j, i: (j, i, 0, 0))
    outs = pl.pallas_call(
        body,
        out_shape=[jax.ShapeDtypeStruct((s, SB_W), BF16)] * 3 + s_shape,
        grid=(npair, nq // SB_QB),
        in_specs=[pl.BlockSpec((SB_QB * TQ, LANES), lambda j, i: (i, j)),
                  pl.BlockSpec((s, LANES), lambda j, i: (0, npair + j)),
                  pl.BlockSpec((s, LANES), lambda j, i: (0, 2 * npair + j)),
                  pl.BlockSpec((None, SB_QB, 2 * TQ, BK), lambda j, i: (j, i, 0, 0)),
                  pl.BlockSpec((None, SB_QB, 8, LANES), lambda j, i: (j, i, 0, 0)),
                  pl.BlockSpec((SB_QB * TQ, LANES), lambda j, i: (i, j)),
                  pl.BlockSpec(prew.shape, lambda j, i: (0, 0)),
                  pl.BlockSpec(prexw.shape, lambda j, i: (0, 0)), wide_spec, wide_spec] + s_in,
        out_specs=[pl.BlockSpec((SB_QB * TQ, LANES), lambda j, i: (i, j)),
                   pl.BlockSpec((s, LANES), lambda j, i: (0, j)),
                   pl.BlockSpec((s, LANES), lambda j, i: (0, j))] + s_out,
        input_output_aliases=s_alias,
        scratch_shapes=[pltpu.VMEM((SB_QB, 2 * TQ, BK), F32), pltpu.VMEM((SB_QB, 2 * TQ, BK), F32),
                        pltpu.VMEM((SB_QB, 2 * TQ, LANES), F32), pltpu.VMEM((s, LANES), F32),
                        pltpu.VMEM((s, LANES), F32)] + s_sems,
        compiler_params=_params(2), name=name)(qkv, qkv, qkv, lt, nb, doa, prew, prexw, a_wide, sg_wide,
                                               *(side.arrays if side else ()))
    return outs[0], outs[1], outs[2], outs[3:]


def _bucket_table():
    i = np.arange(TQ)[:, None]
    j = np.arange(2 * BK)[None, :]
    dist = np.maximum(TQ + i - j, 0)
    max_exact = N_BUCKETS // 2
    df = np.maximum(dist, 1).astype(np.float32)
    large = max_exact + (np.log(df / np.float32(max_exact)) / np.float32(math.log(MAX_DISTANCE / max_exact))
                         * np.float32(N_BUCKETS - max_exact)).astype(np.int32)
    large = np.minimum(large, N_BUCKETS - 1)
    return np.where(dist < max_exact, dist, large).astype(np.int32)


def _swa_align_in(t, lane, g):
    tf = t.astype(F32)
    tr = pltpu.roll(tf, HEAD_DIM, 1)
    gmask = (lane >= HEAD_DIM) == (g == 1)
    top = jnp.where(gmask, jnp.where(g == 0, tf, tr), 0.0)
    bot = jnp.where(gmask, jnp.where(g == 1, tf, tr), 0.0)
    return jnp.concatenate([top, bot], axis=0).astype(BF16)


def _swa_align_out(t, lane, g):
    top, bot = t[:TQ, :], t[TQ:, :]
    top = jnp.where(g == 0, top, pltpu.roll(top, HEAD_DIM, 1))
    bot = jnp.where(g == 1, bot, pltpu.roll(bot, HEAD_DIM, 1))
    return jnp.where(lane < HEAD_DIM, top, bot)


def _swa_bias(bias_ref, bucket_ref, rb_ref, j):
    dist = TQ + lax.broadcasted_iota(jnp.int32, (TQ, 2 * BK), 0) - lax.broadcasted_iota(jnp.int32, (TQ, 2 * BK), 1)
    window = (dist >= 0) & (dist < WINDOW)
    for hh in range(2):
        def add(b, acc):
            return acc + jnp.where(bucket_ref[...] == b, rb_ref[b, 2 * j + hh], 0.0)
        bias = lax.fori_loop(0, N_BUCKETS, add, jnp.zeros((TQ, 2 * BK), F32))
        bias_ref[hh * TQ:(hh + 1) * TQ, :] = jnp.where(window, bias, NEG)


def _swa_probs(qs, k2, bias, own_block, sink_ref, i, j):
    s = _dot_nt(qs, k2) + bias
    s = jnp.where(own_block | (i > 0), s, NEG)
    row1 = lax.broadcasted_iota(jnp.int32, (2 * TQ, 1), 0)
    sink = jnp.where(row1 < TQ, sink_ref[2 * j], sink_ref[2 * j + 1])
    m = jnp.maximum(jnp.max(s, axis=1, keepdims=True), sink)
    e = jnp.exp(s - m)
    es = jnp.exp(sink - m)
    inv = 1.0 / (jnp.sum(e, axis=1, keepdims=True) + es)
    return e * inv, es * inv


def _swa_kv(ref, i):
    prev = pl.multiple_of(jnp.maximum(i - 1, 0) * BK, BK)
    cur = pl.multiple_of(i * BK, BK)
    return jnp.concatenate([ref[pl.ds(prev, BK), :], ref[pl.ds(cur, BK), :]], axis=0), prev, cur


def _swa_fwd(qkv, sinks, rel_bias, name):
    s = qkv.shape[0]
    nq = s // TQ
    npair = SW_QW // LANES
    qcol = 3 * SB_W // LANES
    bucket = jnp.asarray(_bucket_table())

    def body(q_ref, k_ref, v_ref, bucket_ref, sink_ref, rb_ref, o_ref, p_ref, bias_ref):
        j = pl.program_id(0)
        step = pl.program_id(1)
        g = j // 2
        lane = lax.broadcasted_iota(jnp.int32, (TQ, LANES), 1)

        @pl.when(step == 0)
        def _():
            _swa_bias(bias_ref, bucket_ref, rb_ref, j)

        own_block = lax.broadcasted_iota(jnp.int32, (2 * TQ, 2 * BK), 1) >= BK
        for b in range(SWA_QB):
            i = step * SWA_QB + b
            rows = slice(b * TQ, (b + 1) * TQ)
            qs = _swa_align_in(q_ref[rows, :] * SCALE, lane, g)
            k2, _, _ = _swa_kv(k_ref, i)
            v2, _, _ = _swa_kv(v_ref, i)
            pr, psink = _swa_probs(qs, k2, bias_ref[...], own_block, sink_ref, i, j)
            prb = pr.astype(BF16)
            o_ref[rows, :] = _swa_align_out(_dot(prb, v2), lane, g).astype(BF16)
            p_ref[b, :, :2 * BK] = prb
            p_ref[b, :, 2 * BK:] = jnp.broadcast_to(psink, (2 * TQ, BK)).astype(BF16)

    assert nq % SWA_QB == 0
    return pl.pallas_call(
        body,
        out_shape=[jax.ShapeDtypeStruct((s, SW_QW), BF16), jax.ShapeDtypeStruct((npair, nq, 2 * TQ, 3 * BK), BF16)],
        grid=(npair, nq // SWA_QB),
        in_specs=[pl.BlockSpec((SWA_QB * TQ, LANES), lambda j, i: (i, qcol + j)),
                  pl.BlockSpec((s, LANES), lambda j, i: (0, qcol + npair)),
                  pl.BlockSpec((s, LANES), lambda j, i: (0, qcol + npair + 1)),
                  pl.BlockSpec((TQ, 2 * BK), lambda j, i: (0, 0)),
                  pl.BlockSpec(memory_space=pltpu.SMEM),
                  pl.BlockSpec(memory_space=pltpu.SMEM)],
        out_specs=[pl.BlockSpec((SWA_QB * TQ, LANES), lambda j, i: (i, j)),
                   pl.BlockSpec((None, SWA_QB, 2 * TQ, 3 * BK), lambda j, i: (j, i, 0, 0))],
        scratch_shapes=[pltpu.VMEM((2 * TQ, 2 * BK), F32)],
        compiler_params=_params(2), name=name)(qkv, qkv, qkv, bucket, sinks, rel_bias)


def _swa_bwd(qkv, ob, dob, probs, name, side=None):
    s = qkv.shape[0]
    nq = s // TQ
    npair = SW_QW // LANES
    qcol = 3 * SB_W // LANES
    bucket = jnp.asarray(_bucket_table())

    def body(q_ref, k_ref, v_ref, o_ref, do_ref, bucket_ref, p_ref, *rest):
        side_in, (dq_ref, dkv_ref, dsink_ref, drel_ref), side_out, scratch = _side_refs(side, rest, 4)
        dsacc_ref, dk_ref, dv_ref = scratch[:3]
        sems = scratch[3:]
        j = pl.program_id(0)
        step = pl.program_id(1)
        if side is not None:
            @pl.when((j == 0) & (step == 0))
            def _():
                side.start(side_in, side_out, *sems)
        g = j // 2
        lane = lax.broadcasted_iota(jnp.int32, (TQ, LANES), 1)
        row8 = lax.broadcasted_iota(jnp.int32, (SW_HEADS, LANES), 0)
        lane8 = lax.broadcasted_iota(jnp.int32, (SW_HEADS, LANES), 1)

        @pl.when((step == 0) & (j == 0))
        def _():
            dk_ref[...] = jnp.zeros_like(dk_ref)
            dv_ref[...] = jnp.zeros_like(dv_ref)
            dsink_ref[...] = jnp.zeros_like(dsink_ref)
            drel_ref[...] = jnp.zeros_like(drel_ref)

        @pl.when(step == 0)
        def _():
            dsacc_ref[...] = jnp.zeros_like(dsacc_ref)

        ds_sum = jnp.zeros(dsacc_ref.shape, F32)
        dsink = jnp.zeros((SW_HEADS, LANES), F32)
        for b in range(SWA_QB):
            i = step * SWA_QB + b
            rows = slice(b * TQ, (b + 1) * TQ)
            qs = _swa_align_in(q_ref[rows, :] * SCALE, lane, g)
            do = do_ref[rows, :]
            dos = _swa_align_in(do, lane, g)
            dof = do.astype(F32) * o_ref[rows, :].astype(F32)
            d0 = jnp.sum(jnp.where(lane < HEAD_DIM, dof, 0.0), axis=1, keepdims=True)
            d1 = jnp.sum(jnp.where(lane >= HEAD_DIM, dof, 0.0), axis=1, keepdims=True)
            delta = jnp.concatenate([d0, d1], axis=0)
            k2, prev, cur = _swa_kv(k_ref, i)
            v2, _, _ = _swa_kv(v_ref, i)
            prb = p_ref[b, :, :2 * BK]
            ds = prb.astype(F32) * (_dot_nt(dos, v2) - delta)
            ds_sum = ds_sum + ds
            sd = p_ref[b, :, 2 * BK:].astype(F32) * delta
            ds0 = -jnp.sum(sd[:TQ, :], axis=0, keepdims=True)
            ds1 = -jnp.sum(sd[TQ:, :], axis=0, keepdims=True)
            dsink = dsink + jnp.where(row8 == 2 * j, ds0, jnp.where(row8 == 2 * j + 1, ds1, 0.0))
            dsb = ds.astype(BF16)
            dq_ref[rows, :] = _swa_align_out(_dot(dsb, k2) * SCALE, lane, g).astype(BF16)
            dk2 = _dot_tn(dsb, qs)
            dv2 = _dot_tn(prb, dos)
            dk_ref[pl.ds(prev, BK), :] += dk2[:BK, :]
            dk_ref[pl.ds(cur, BK), :] += dk2[BK:, :]
            dv_ref[pl.ds(prev, BK), :] += dv2[:BK, :]
            dv_ref[pl.ds(cur, BK), :] += dv2[BK:, :]
        dsacc_ref[...] += ds_sum
        dsink_ref[...] += dsink

        @pl.when(step == nq // SWA_QB - 1)
        def _():
            for hh in range(2):
                def red(b, acc):
                    val = _sum_all(jnp.where(bucket_ref[...] == b, dsacc_ref[hh * TQ:(hh + 1) * TQ, :], 0.0))
                    return jnp.where((row8 == 2 * j + hh) & (lane8 == b), val, acc)
                drel_ref[...] += lax.fori_loop(0, N_BUCKETS, red, jnp.zeros((SW_HEADS, LANES), F32))

        @pl.when((step == nq // SWA_QB - 1) & (j == npair - 1))
        def _():
            dkv_ref[:, :LANES] = dk_ref[...].astype(BF16)
            dkv_ref[:, LANES:] = dv_ref[...].astype(BF16)
            if side is not None:
                side.finish(side_in, side_out, *sems)

    whole = lambda j, i: (0, 0)
    s_in, s_shape, s_out, s_alias, s_sems = _side_specs(side, 7, 4)
    outs = pl.pallas_call(
        body,
        out_shape=[jax.ShapeDtypeStruct((s, SW_QW), BF16), jax.ShapeDtypeStruct((s, 2 * LANES), BF16),
                   jax.ShapeDtypeStruct((SW_HEADS, LANES), F32), jax.ShapeDtypeStruct((SW_HEADS, LANES), F32)] + s_shape,
        grid=(npair, nq // SWA_QB),
        in_specs=[pl.BlockSpec((SWA_QB * TQ, LANES), lambda j, i: (i, qcol + j)),
                  pl.BlockSpec((s, LANES), lambda j, i: (0, qcol + npair)),
                  pl.BlockSpec((s, LANES), lambda j, i: (0, qcol + npair + 1)),
                  pl.BlockSpec((SWA_QB * TQ, LANES), lambda j, i: (i, j)),
                  pl.BlockSpec((SWA_QB * TQ, LANES), lambda j, i: (i, j)),
                  pl.BlockSpec((TQ, 2 * BK), whole),
                  pl.BlockSpec((None, SWA_QB, 2 * TQ, 3 * BK), lambda j, i: (j, i, 0, 0))] + s_in,
        out_specs=[pl.BlockSpec((SWA_QB * TQ, LANES), lambda j, i: (i, j)),
                   pl.BlockSpec((s, 2 * LANES), whole),
                   pl.BlockSpec((SW_HEADS, LANES), whole), pl.BlockSpec((SW_HEADS, LANES), whole)] + s_out,
        input_output_aliases=s_alias,
        scratch_shapes=[pltpu.VMEM((2 * TQ, 2 * BK), F32), pltpu.VMEM((s, LANES), F32),
                        pltpu.VMEM((s, LANES), F32)] + s_sems,
        compiler_params=_params(2), name=name)(qkv, qkv, qkv, ob, dob, bucket, probs,
                                               *(side.arrays if side else ()))
    return outs[:4], outs[4:]


def _acc_init(i, *refs):
    @pl.when(i == 0)
    def _():
        for r in refs:
            r[...] = jnp.zeros_like(r)


def _loss_bwd(x3, target, g, name):
    d = x3.shape[1]

    def body(x_ref, t_ref, g_ref, dx_ref, dg_ref, loss_ref):
        _acc_init(pl.program_id(0), dg_ref, loss_ref)
        x = x_ref[...]
        gv = g_ref[...]
        diff = _rms(x, gv) - t_ref[...]
        loss_ref[...] += 0.5 * jnp.sum(jnp.mean(jnp.square(diff), axis=-1, keepdims=True), axis=0, keepdims=True)
        dx, dg = _rms_bwd(diff * (1.0 / d), x, gv)
        dx_ref[...] = dx
        dg_ref[...] += dg

    return _rowcall(name, body, [x3, target], [g], [(d, F32)], [((1, d), F32), ((1, LANES), F32)])


def _ple_bwd(dx3, pe, gt, x2, g, wpg, name, side=None):
    d = x2.shape[1]

    def body(dx3_ref, pe_ref, gt_ref, x_ref, g_ref, w_ref, dpe_ref, dgt_ref, dx2_ref, dg_ref):
        _acc_init(pl.program_id(0), dg_ref)
        dx3 = dx3_ref[...]
        sg = _sigmoid(gt_ref[...])
        dpe_ref[...] = (dx3 * sg).astype(BF16)
        dgt = (dx3 * pe_ref[...] * sg * (1.0 - sg)).astype(BF16)
        dgt_ref[...] = dgt
        dx, dg = _rms_bwd(_dot_nt(dgt, w_ref[...]), x_ref[...], g_ref[...])
        dx2_ref[...] = dx3 + dx
        dg_ref[...] += dg

    return _rowcall(name, body, [dx3, pe, gt, x2], [g, wpg], [(d, BF16), (d, BF16), (d, F32)], [((1, d), F32)],
                    side=side)


def _ff2_bwd(dx2, r, w2, name):
    d = dx2.shape[1]
    dff = r.shape[1]

    def body(dx_ref, r_ref, w_ref, du_ref, dxb_ref):
        dxb = dx_ref[...].astype(BF16)
        dxb_ref[...] = dxb
        du_ref[...] = (_dot_nt(dxb, w_ref[...]) * (2.0 * r_ref[...].astype(F32))).astype(BF16)

    return _rowcall(name, body, [dx2, r], [w2], [(dff, BF16), (d, BF16)])


def _ff1_bwd(du, dx2, x1, g, w1, name, side=None):
    d = x1.shape[1]

    def body(du_ref, dx2_ref, x_ref, g_ref, w_ref, dx1_ref, dx1b_ref, dg_ref):
        _acc_init(pl.program_id(0), dg_ref)
        dx, dg = _rms_bwd(_dot_cols_t(du_ref[...], w_ref), x_ref[...], g_ref[...])
        dx1 = dx2_ref[...] + dx
        dx1_ref[...] = dx1
        dx1b_ref[...] = dx1.astype(BF16)
        dg_ref[...] += dg

    return _rowcall(name, body, [du, dx2, x1], [g, w1], [(d, F32), (d, BF16)], [((1, d), F32)], side=side)


def _mixer_bwd(dx1b, gates, oa, ob, wo, wua, wub, name, side=None):
    d = dx1b.shape[1]

    def body(dx_ref, gate_ref, oa_ref, ob_ref, wo_ref, wua_ref, wub_ref,
             dya_ref, dyb_ref, dgate_ref, doa_ref, dob_ref):
        dm = _dot_nt(dx_ref[...], wo_ref[...])
        sa = _sigmoid(gate_ref[:, :d])
        sb = _sigmoid(gate_ref[:, d:])
        ya = _dot_cols(oa_ref[...], wua_ref)
        yb = _dot_cols(ob_ref[...], wub_ref)
        dya = (dm * sa).astype(BF16)
        dyb = (dm * sb).astype(BF16)
        dya_ref[...] = dya
        dyb_ref[...] = dyb
        dgate_ref[:, :d] = (dm * ya * sa * (1.0 - sa)).astype(BF16)
        dgate_ref[:, d:] = (dm * yb * sb * (1.0 - sb)).astype(BF16)
        doa_ref[...] = _dot_cols_t(dya, wua_ref).astype(BF16)
        dob_ref[...] = _dot_cols_t(dyb, wub_ref).astype(BF16)

    return _rowcall(name, body, [dx1b, gates, oa, ob], [wo, wua, wub],
                    [(d, BF16), (d, BF16), (2 * d, BF16), (SB_W, BF16), (SW_QW, BF16)], side=side)


def _inproj_bwd(pieces, dx1, x, g, wt, name, side=None):
    d = x.shape[1]
    n = len(pieces)
    offsets = [sum(pc.shape[1] for pc in pieces[:p]) for p in range(n + 1)]

    def body(*refs):
        dx1_ref, x_ref, g_ref, w_ref, dx_ref, dg_ref = refs[n:]
        _acc_init(pl.program_id(0), dg_ref)
        dh = _dot(refs[0][...], w_ref[:offsets[1], :])
        for p in range(1, n):
            dh = dh + _dot(refs[p][...], w_ref[offsets[p]:offsets[p + 1], :])
        dx, dg = _rms_bwd(dh, x_ref[...], g_ref[...])
        dx_ref[...] = dx1_ref[...] + dx
        dg_ref[...] += dg

    return _rowcall(name, body, list(pieces) + [dx1, x], [g, wt], [(d, F32)], [((1, d), F32)], side=side)


def _tile(n, cap):
    assert n % LANES == 0
    return max(t for t in range(LANES, min(n, cap) + 1, LANES) if n % t == 0)


def _mm_tn(a, b, name, nshard=1):
    s, ka = a.shape
    nb = b.shape[1]
    n = nb // nshard
    ta = _tile(ka, 512)
    tb = _tile(n, 1024)
    per = n // tb

    def body(a_ref, b_ref, o_ref):
        o_ref[...] = _dot_tn(a_ref[...].astype(BF16), b_ref[...].astype(BF16))

    return pl.pallas_call(
        body, out_shape=jax.ShapeDtypeStruct((nshard, ka, n), F32), grid=(nb // tb, ka // ta),
        in_specs=[pl.BlockSpec((s, ta), lambda jb, ia: (0, ia)), pl.BlockSpec((s, tb), lambda jb, ia: (0, jb))],
        out_specs=pl.BlockSpec((None, ta, tb), lambda jb, ia: (jb // per, ia, jb % per)),
        compiler_params=_params(2), name=name)(a, b)


def _mm_tn_pieces(pieces, b, name, side=None):
    s, nb = b.shape
    ta = 256
    n_in = len(pieces) + 1
    tiles = [pc.shape[1] // ta for pc in pieces]
    assert all(pc.shape[1] % ta == 0 for pc in pieces)
    starts = [sum(tiles[:p]) for p in range(len(pieces))]
    tb = _tile(nb, 1024)
    grid = (nb // tb, sum(tiles))

    def body(*refs):
        a_refs, b_ref = refs[:n_in - 1], refs[n_in - 1]
        side_in, (o_ref,), side_out, sems = _side_refs(side, refs[n_in:], 1)
        jb, ia = pl.program_id(0), pl.program_id(1)
        if side is not None:
            @pl.when((jb == 0) & (ia == 0))
            def _():
                side.start(side_in, side_out, *sems)
        for p in range(len(pieces)):
            @pl.when((ia >= starts[p]) & (ia < starts[p] + tiles[p]))
            def _(p=p):
                o_ref[...] = _dot_tn(a_refs[p][...], b_ref[...])
        if side is not None:
            @pl.when((jb == grid[0] - 1) & (ia == grid[1] - 1))
            def _():
                side.finish(side_in, side_out, *sems)

    def piece_spec(p):
        return pl.BlockSpec((s, ta), lambda jb, ia: (0, jnp.clip(ia - starts[p], 0, tiles[p] - 1)))

    s_in, s_shape, s_out, s_alias, s_sems = _side_specs(side, n_in, 1)
    outs = pl.pallas_call(
        body, out_shape=[jax.ShapeDtypeStruct((sum(tiles) * ta, nb), F32)] + s_shape, grid=grid,
        in_specs=[piece_spec(p) for p in range(len(pieces))] + [pl.BlockSpec((s, tb), lambda jb, ia: (0, jb))] + s_in,
        out_specs=[pl.BlockSpec((ta, tb), lambda jb, ia: (ia, jb))] + s_out,
        input_output_aliases=s_alias, scratch_shapes=s_sems,
        compiler_params=_params(2), name=name)(*pieces, b, *(side.arrays if side else ()))
    return outs[0] if side is None else (outs[0], outs[1:])


def _place():
    return lax.axis_index("x"), lax.axis_index("y"), lax.axis_index("c")


def _chip_peer(x, y, k):
    return (x ^ (k >> 1), y ^ (k & 1))


def _row_tile(k, cap=544):
    return max(t for t in range(32, min(k, cap) + 1, 32) if k % t == 0)


def _cast_bf16(w, r, name):
    l, k, n = w.shape
    assert l == 2
    tk = _row_tile(k)

    def body(r_ref, w_ref, o0_ref, o1_ref):
        o0_ref[...] = w_ref[0].astype(BF16)
        o1_ref[...] = w_ref[1].astype(BF16)

    out_spec = pl.BlockSpec((None, tk, n), lambda i, r_ref: (r_ref[0], i, 0))
    return pl.pallas_call(
        body, out_shape=[jax.ShapeDtypeStruct((4, k, n), BF16)] * 2,
        grid_spec=pltpu.PrefetchScalarGridSpec(
            num_scalar_prefetch=1, grid=(k // tk,),
            in_specs=[pl.BlockSpec((l, tk, n), lambda i, r_ref: (0, i, 0))],
            out_specs=[out_spec, out_spec]),
        compiler_params=_params(1), name=name)(r, w)


class _Exchange(NamedTuple):
    arrays: tuple
    aliased: tuple
    sems: tuple
    start: Callable
    finish: Callable


def _all_gather(bufs):
    nt = len(bufs)

    def ici(t, ins, outs, send_sems, recv_sems, x, y, c, r, k):
        return pltpu.make_async_remote_copy(
            src_ref=ins[t].at[r, c], dst_ref=outs[t].at[r, c], send_sem=send_sems.at[t, k - 1],
            recv_sem=recv_sems.at[t, k - 1], device_id=(*_chip_peer(x, y, k), c), device_id_type=MESH)

    def d2d(t, outs, send_sems, recv_sems, x, y, c, r, k, half):
        slab = outs[t].at[r ^ k, half]
        return pltpu.make_async_remote_copy(
            src_ref=slab, dst_ref=slab, send_sem=send_sems.at[t, 2 + k], recv_sem=recv_sems.at[t, 2 + k],
            device_id=(x, y, 1 - c), device_id_type=MESH)

    def start(ins, outs, send_sems, recv_sems):
        x, y, c = _place()
        r = 2 * x + y
        for t in range(nt):
            for k in (1, 2, 3):
                ici(t, ins, outs, send_sems, recv_sems, x, y, c, r, k).start()

    def finish(ins, outs, send_sems, recv_sems):
        x, y, c = _place()
        r = 2 * x + y
        for t in range(nt):
            for k in (1, 2, 3):
                slab = outs[t].at[r ^ k, c]
                pltpu.make_async_remote_copy(
                    src_ref=slab, dst_ref=slab, send_sem=send_sems.at[t, k - 1], recv_sem=recv_sems.at[t, k - 1],
                    device_id=(x, y, 1 - c), device_id_type=MESH).wait_recv()
                d2d(t, outs, send_sems, recv_sems, x, y, c, r, k, c).start()
        for t in range(nt):
            for k in (1, 2, 3):
                d2d(t, outs, send_sems, recv_sems, x, y, c, r, k, 1 - c).wait_recv()
        for t in range(nt):
            for k in (1, 2, 3):
                ici(t, ins, outs, send_sems, recv_sems, x, y, c, r, k).wait_send()
                d2d(t, outs, send_sems, recv_sems, x, y, c, r, k, c).wait_send()

    return _Exchange(tuple(bufs), tuple(range(nt)), (nt, 6), start, finish)


def _run_exchange(name, ex):
    n_in, n_out = len(ex.arrays), len(ex.aliased)

    def body(*refs):
        ins, outs = refs[:n_in], refs[n_in:n_in + n_out]
        send_sems, recv_sems = refs[n_in + n_out:]
        ex.start(ins, outs, send_sems, recv_sems)
        ex.finish(ins, outs, send_sems, recv_sems)

    any_spec = pl.BlockSpec(memory_space=pl.ANY)
    return pl.pallas_call(
        body, out_shape=[jax.ShapeDtypeStruct(ex.arrays[a].shape, ex.arrays[a].dtype) for a in ex.aliased],
        in_specs=[any_spec] * n_in, out_specs=[any_spec] * n_out,
        input_output_aliases={a: o for o, a in enumerate(ex.aliased)},
        scratch_shapes=[pltpu.SemaphoreType.DMA(ex.sems), pltpu.SemaphoreType.DMA(ex.sems)],
        name=name)(*ex.arrays)


def _rs_to_sibling(grads):
    nt = len(grads)
    landing = [lax.empty((4,) + g.shape[2:], F32) for g in grads]

    def copies(ins, outs, send_sems, recv_sems):
        x, y, c = _place()
        return [pltpu.make_async_remote_copy(
            src_ref=ins[t].at[:, 1 - c], dst_ref=outs[t], send_sem=send_sems.at[t], recv_sem=recv_sems.at[t],
            device_id=(x, y, 1 - c), device_id_type=MESH) for t in range(nt)]

    def start(ins, outs, send_sems, recv_sems):
        for cp in copies(ins, outs, send_sems, recv_sems):
            cp.start()

    def finish(ins, outs, send_sems, recv_sems):
        for cp in copies(ins, outs, send_sems, recv_sems):
            cp.wait()

    return _Exchange(tuple(grads) + tuple(landing), tuple(range(nt, 2 * nt)), (nt,), start, finish)


def _add_half(g, recv, cr, name):
    _, _, k2, n = g.shape
    tk = _row_tile(k2)

    def body(cr_ref, g_ref, r_ref, sums_ref, mine_ref):
        val = (g_ref[...] + r_ref[...]).astype(BF16)
        sums_ref[...] = val

        @pl.when(pl.program_id(1) == cr_ref[1])
        def _():
            mine_ref[...] = val

    return pl.pallas_call(
        body, out_shape=[jax.ShapeDtypeStruct((4, k2, n), BF16)] * 2,
        grid_spec=pltpu.PrefetchScalarGridSpec(
            num_scalar_prefetch=1, grid=(k2 // tk, 4),
            in_specs=[pl.BlockSpec((None, None, tk, n), lambda i, q, cr_ref: (q, cr_ref[0], i, 0)),
                      pl.BlockSpec((None, tk, n), lambda i, q, cr_ref: (q, i, 0))],
            out_specs=[pl.BlockSpec((None, tk, n), lambda i, q, cr_ref: (q, i, 0)),
                       pl.BlockSpec((None, tk, n), lambda i, q, cr_ref: (cr_ref[1], i, 0))]),
        compiler_params=_params(2), name=name)(cr, g, recv)


def _rs_to_chips(sums, parts):
    nt = len(sums)

    def copies(ins, outs, send_sems, recv_sems):
        x, y, c = _place()
        r = 2 * x + y
        return [pltpu.make_async_remote_copy(
            src_ref=ins[t].at[r ^ k], dst_ref=outs[t].at[r], send_sem=send_sems.at[t, k - 1],
            recv_sem=recv_sems.at[t, k - 1], device_id=(*_chip_peer(x, y, k), c), device_id_type=MESH)
            for t in range(nt) for k in (1, 2, 3)]

    def start(ins, outs, send_sems, recv_sems):
        for cp in copies(ins, outs, send_sems, recv_sems):
            cp.start()

    def finish(ins, outs, send_sems, recv_sems):
        for cp in copies(ins, outs, send_sems, recv_sems):
            cp.wait()

    return _Exchange(tuple(sums) + tuple(parts), tuple(range(nt, 2 * nt)), (nt, 3), start, finish)


def _sum4(parts, cr, name):
    _, k2, n = parts.shape
    tk = _row_tile(k2)

    def body(cr_ref, p_ref, o_ref):
        p = p_ref[...].astype(F32)
        o_ref[...] = ((p[0] + p[1]) + p[2]) + p[3]

    return pl.pallas_call(
        body, out_shape=jax.ShapeDtypeStruct((2, k2, n), F32),
        grid_spec=pltpu.PrefetchScalarGridSpec(
            num_scalar_prefetch=1, grid=(k2 // tk,),
            in_specs=[pl.BlockSpec((4, tk, n), lambda i, cr_ref: (0, i, 0))],
            out_specs=pl.BlockSpec((None, tk, n), lambda i, cr_ref: (cr_ref[0], i, 0))),
        compiler_params=_params(1), name=name)(cr, parts)


def _exchange_halves(both):
    nt = len(both)

    def copies(ins, outs, send_sems, recv_sems):
        x, y, c = _place()
        return [pltpu.make_async_remote_copy(
            src_ref=ins[t].at[c], dst_ref=outs[t].at[c], send_sem=send_sems.at[t], recv_sem=recv_sems.at[t],
            device_id=(x, y, 1 - c), device_id_type=MESH) for t in range(nt)]

    def start(ins, outs, send_sems, recv_sems):
        for cp in copies(ins, outs, send_sems, recv_sems):
            cp.start()

    def finish(ins, outs, send_sems, recv_sems):
        for cp in copies(ins, outs, send_sems, recv_sems):
            cp.wait()

    return _Exchange(tuple(both), tuple(range(nt)), (nt,), start, finish)


def _adamw_math(w, g, m, v):
    m = ADAM_B1 * m + (1.0 - ADAM_B1) * g
    v = ADAM_B2 * v + (1.0 - ADAM_B2) * jnp.square(g)
    m_hat = m / (1.0 - ADAM_B1 ** ADAM_STEP)
    v_hat = v / (1.0 - ADAM_B2 ** ADAM_STEP)
    delta = -ADAM_LR * (m_hat / (jnp.sqrt(v_hat) + ADAM_EPS) + ADAM_WD * w)
    return delta, m, v


def _adamw(w, m, v, g0, g1, name):
    _, k, n = w.shape
    tk = _row_tile(k)
    nk = k // tk

    def body(w_ref, m_ref, v_ref, g0_ref, g1_ref, grad_ref, delta_ref, nm_ref, nv_ref):
        g = jnp.where(pl.program_id(0) == 0, g0_ref[...], g1_ref[...])
        delta, nm, nv = _adamw_math(w_ref[...], g, m_ref[...], v_ref[...])
        grad_ref[...] = g
        delta_ref[...] = delta
        nm_ref[...] = nm
        nv_ref[...] = nv

    lay = pl.BlockSpec((None, tk, n), lambda a, i: (a, i, 0))
    g0_spec = pl.BlockSpec((tk, n), lambda a, i: (jnp.where(a == 0, i, nk - 1), 0))
    g1_spec = pl.BlockSpec((tk, n), lambda a, i: (jnp.where(a == 1, i, 0), 0))
    return pl.pallas_call(
        body, out_shape=[jax.ShapeDtypeStruct(w.shape, F32)] * 4, grid=(2, nk),
        in_specs=[lay, lay, lay, g0_spec, g1_spec], out_specs=[lay] * 4,
        compiler_params=_params(2), name=name)(w, m, v, g0, g1)


def _small_allreduce_adamw(gpart, w, m, v):
    shape = gpart.shape

    def body(g_ref, w_ref, m_ref, v_ref, gsum_ref, delta_ref, nm_ref, nv_ref, recv_ref, send_sems, recv_sems):
        x, y, c = _place()
        me = 4 * x + 2 * y + c
        recv_ref[me] = g_ref[...]
        cps = []
        for k in range(1, 8):
            peer = (x ^ (k >> 2), y ^ ((k >> 1) & 1), c ^ (k & 1))
            cp = pltpu.make_async_remote_copy(
                src_ref=g_ref, dst_ref=recv_ref.at[me], send_sem=send_sems.at[k - 1], recv_sem=recv_sems.at[k - 1],
                device_id=peer, device_id_type=MESH)
            cp.start()
            cps.append(cp)
        for cp in cps:
            cp.wait()
        g = recv_ref[0]
        for dev in range(1, 8):
            g = g + recv_ref[dev]
        delta, nm, nv = _adamw_math(w_ref[...], g, m_ref[...], v_ref[...])
        gsum_ref[...] = g
        delta_ref[...] = delta
        nm_ref[...] = nm
        nv_ref[...] = nv

    vm = pl.BlockSpec(memory_space=pltpu.VMEM)
    return pl.pallas_call(
        body, out_shape=[jax.ShapeDtypeStruct(shape, F32)] * 4, in_specs=[vm] * 4, out_specs=[vm] * 4,
        scratch_shapes=[pltpu.VMEM((8,) + shape, F32), pltpu.SemaphoreType.DMA((7,)), pltpu.SemaphoreType.DMA((7,))],
        name="small_allreduce_adamw")(gpart, w, m, v)


BIG = ("w_in", "w_up_a", "w_up_b", "w_o", "w_ff1", "w_ff2", "w_pe", "w_pg")
COL_SHARDED = ("w_in", "w_up_a", "w_up_b", "w_ff1", "w_pe")
ROW_SHARDED = ("w_o", "w_ff2", "w_pg")
SMALL_ROWS = 16


def _pack_small(g_mix, g_mlp, g_pe, g_final, sinks, rel_bias, loss=None):
    d = g_final.shape[0]
    row = lambda v: jnp.pad(v.reshape(1, -1), ((0, 0), (0, d - v.size)))
    rows = [g_mix, g_mlp, g_pe, g_final.reshape(1, d),
            jnp.zeros((1, d), F32) if loss is None else row(loss), row(sinks), row(rel_bias)]
    out = jnp.concatenate(rows, axis=0)
    return jnp.pad(out, ((0, SMALL_ROWS - out.shape[0]), (0, 0)))


def _unpack_small(a, sinks_shape, rel_shape):
    return (a[0:2], a[2:4], a[4:6], a[6], a[8, :sinks_shape[0] * sinks_shape[1]].reshape(sinks_shape),
            a[9, :rel_shape[0] * rel_shape[1]].reshape(rel_shape))


def kernel(x, p, w_in, w_up_a, w_up_b, w_o, w_ff1, w_ff2, w_pe, w_pg, g_mix, g_mlp, g_pe, g_final, sinks, rel_bias, loss_target, m_w_in, m_w_up_a, m_w_up_b, m_w_o, m_w_ff1, m_w_ff2, m_w_pe, m_w_pg, m_g_mix, m_g_mlp, m_g_pe, m_g_final, m_sinks, m_rel_bias, v_w_in, v_w_up_a, v_w_up_b, v_w_o, v_w_ff1, v_w_ff2, v_w_pe, v_w_pg, v_g_mix, v_g_mlp, v_g_pe, v_g_final, v_sinks, v_rel_bias):
    depth = w_in.shape[0]
    assert depth == 2
    x0 = x[0]
    target = loss_target[0]
    d = x0.shape[1]
    wl = dict(w_in=w_in, w_up_a=w_up_a, w_up_b=w_up_b, w_o=w_o, w_ff1=w_ff1, w_ff2=w_ff2, w_pe=w_pe, w_pg=w_pg)
    ml = dict(w_in=m_w_in, w_up_a=m_w_up_a, w_up_b=m_w_up_b, w_o=m_w_o, w_ff1=m_w_ff1, w_ff2=m_w_ff2, w_pe=m_w_pe, w_pg=m_w_pg)
    vl = dict(w_in=v_w_in, w_up_a=v_w_up_a, w_up_b=v_w_up_b, w_o=v_w_o, w_ff1=v_w_ff1, w_ff2=v_w_ff2, w_pe=v_w_pe, w_pg=v_w_pg)
    c_idx = lax.axis_index("c").astype(jnp.int32)
    r_idx = (2 * lax.axis_index("x") + lax.axis_index("y")).astype(jnp.int32)
    cr = jnp.stack([c_idx, r_idx])

    wl["w_in"], ml["w_in"], vl["w_in"] = (jnp.swapaxes(a, 1, 2) for a in (w_in, m_w_in, v_w_in))

    bufs = {}
    for n in BIG:
        k, nn = wl[n].shape[1:]
        for l, b in enumerate(_cast_bf16(wl[n], r_idx.reshape(1), "cast_" + n)):
            bufs[n, l] = b.reshape(4, 2, k // 2, nn)

    def gather(keys, run):
        for key, b in zip(keys, run(_all_gather([bufs[key] for key in keys]))):
            bufs[key] = b

    def gathered(n, l):
        _, _, k2, nn = bufs[n, l].shape
        if n in ROW_SHARDED or n == "w_in":
            return bufs[n, l].reshape(8 * k2, nn)
        return bufs[n, l].reshape(4, 2 * k2, nn)

    gather([("w_in", 0)], lambda ex: _run_exchange("all_gather_first", ex))

    full = {n: [None] * depth for n in BIG}
    saved = []
    xi = x0
    for i in range(depth):
        st = dict(x0=xi)
        gm = g_mix[i].reshape(1, d)
        full["w_in"][i] = gathered("w_in", i)
        st["h1"], st["qkv"], st["gates"] = _inproj_fwd(xi, gm, full["w_in"][i], f"inproj_fwd_{i}")

        def attend(ex):
            (st["oa"], st["lt"], st["nb"], st["a_wide"], st["sg_wide"]), filled = _sb_fwd(st["qkv"], f"sb_fwd_{i}", ex)
            return filled

        gather([(n, i) for n in BIG if n != "w_in"], attend)
        for n in BIG:
            if n != "w_in":
                full[n][i] = gathered(n, i)
        st["ob"], st["probs"] = _swa_fwd(st["qkv"], sinks[i], rel_bias, f"swa_fwd_{i}")
        st["m"], st["x1"] = _mixer_fwd(st["oa"], st["ob"], st["gates"], xi, full["w_up_a"][i], full["w_up_b"][i],
                                       full["w_o"][i], f"mixer_fwd_{i}")
        if i == 0:
            def feed_forward(ex):
                (st["h2"], st["u"], st["a"]), filled = _ff1_fwd(st["x1"], g_mlp[i].reshape(1, d), full["w_ff1"][i],
                                                               f"ff1_fwd_{i}", ex)
                return filled

            gather([("w_in", 1)], feed_forward)
        else:
            st["h2"], st["u"], st["a"] = _ff1_fwd(st["x1"], g_mlp[i].reshape(1, d), full["w_ff1"][i], f"ff1_fwd_{i}")
        st["x2"] = _ff2_fwd(st["a"], st["x1"], full["w_ff2"][i], f"ff2_fwd_{i}")
        st["pb"], st["h3"], st["pe"], st["gt"], xi = _ple_fwd(p[i, 0], st["x2"], g_pe[i].reshape(1, d),
                                                            full["w_pe"][i], full["w_pg"][i], f"ple_fwd_{i}")
        saved.append(st)

    dx, dg_final, loss_part = _loss_bwd(xi, target, g_final.reshape(1, d), "loss_bwd")
    gw = {n: [None] * depth for n in BIG}
    reduced = {}

    chip_sums = {}

    def to_sibling(keys, run):
        tensors = []
        for n, l in keys:
            g = gw[n][l]
            if n in ROW_SHARDED:
                ka, nb = g.shape[1:]
                g = g.reshape(4, ka // 4, nb)
            _, k, nn = g.shape
            tensors.append(g.reshape(4, 2, k // 2, nn))
        for (n, l), g, r in zip(keys, tensors, run(_rs_to_sibling(tensors))):
            chip_sums[n, l] = _add_half(g, r, cr, f"add_half_{n}_{l}")

    def to_chips(keys):
        return _rs_to_chips([chip_sums[k][0] for k in keys], [chip_sums[k][1] for k in keys])

    halves = {}

    def sum_chips(keys, parts):
        for (n, l), pc in zip(keys, parts):
            halves[n, l] = _sum4(pc, cr, f"sum4_{n}_{l}")

    def swap_halves(keys):
        def store(filled):
            for key, both in zip(keys, filled):
                reduced[key] = both
        return _exchange_halves([halves[k] for k in keys]), store

    dg_mix, dg_mlp, dg_pe, dsinks = [None] * depth, [None] * depth, [None] * depth, [None] * depth
    drel = jnp.zeros((SW_HEADS, LANES), F32)
    for i in reversed(range(depth)):
        st = saved[i]
        dpe, dgt, dx2, dg_pe[i] = _ple_bwd(dx, st["pe"], st["gt"], st["x2"], g_pe[i].reshape(1, d),
                                           full["w_pg"][i], f"ple_bwd_{i}")
        gw["w_pe"][i] = _mm_tn(st["pb"], dpe, f"dw_pe_{i}", 4)
        gw["w_pg"][i] = _mm_tn(st["h3"], dgt, f"dw_pg_{i}")
        du, dx2b = _ff2_bwd(dx2, st["u"], full["w_ff2"][i], f"ff2_bwd_{i}")
        gw["w_ff2"][i] = _mm_tn(st["a"], dx2b, f"dw_ff2_{i}")
        gw["w_ff1"][i] = _mm_tn(st["h2"], du, f"dw_ff1_{i}", 4)
        if i == 0:
            (dx1, dx1b, dg_mlp[i]), parts = _ff1_bwd(du, dx2, st["x1"], g_mlp[i].reshape(1, d), full["w_ff1"][i],
                                                    f"ff1_bwd_{i}", to_chips([("w_in", 1)]))
            sum_chips([("w_in", 1)], parts)
        else:
            dx1, dx1b, dg_mlp[i] = _ff1_bwd(du, dx2, st["x1"], g_mlp[i].reshape(1, d), full["w_ff1"][i],
                                            f"ff1_bwd_{i}")
        gw["w_o"][i] = _mm_tn(st["m"], dx1b, f"dw_o_{i}")
        early = [(n, i) for n in ("w_pe", "w_pg", "w_ff2", "w_ff1", "w_o")]

        def mixer(ex):
            (dya, dyb, dgates, doa, dob), landed = _mixer_bwd(
                dx1b, st["gates"], st["oa"], st["ob"], full["w_o"][i], full["w_up_a"][i], full["w_up_b"][i],
                f"mixer_bwd_{i}", ex)
            st.update(dya=dya, dyb=dyb, dgates=dgates, doa=doa, dob=dob)
            return landed

        to_sibling(early, mixer)
        dgates = st["dgates"]
        gw["w_up_a"][i] = _mm_tn(st["oa"], st["dya"], f"dw_up_a_{i}", 4)
        gw["w_up_b"][i] = _mm_tn(st["ob"], st["dyb"], f"dw_up_b_{i}", 4)
        late = [("w_up_a", i), ("w_up_b", i)]
        to_sibling(late, lambda ex: _run_exchange(f"rs_to_sibling_{i}", ex))
        keys = early + late
        with_sb = [(n, i) for n in ("w_ff1", "w_o", "w_pg", "w_pe")]
        with_swa = [(n, i) for n in ("w_ff2", "w_up_a", "w_up_b")]
        dqa, dka, dva, parts = _sb_bwd(st["qkv"], st["lt"], st["nb"], st["a_wide"], st["sg_wide"], st["doa"],
                                       f"sb_bwd_{i}", to_chips(with_sb))
        sum_chips(with_sb, parts)
        (dqb, dkvb, dsk, drl), parts = _swa_bwd(st["qkv"], st["ob"], st["dob"], st["probs"], f"swa_bwd_{i}",
                                                to_chips(with_swa))
        sum_chips(with_swa, parts)
        dsinks[i] = dsk[:, 0]
        drel = drel + drl
        dproj = [dqa, dka, dva, dqb, dkvb, dgates]
        swap, store = swap_halves(keys + ([("w_in", 1)] if i == 0 else []))
        dw_in_t, filled = _mm_tn_pieces(dproj, st["h1"], f"dw_in_{i}", swap)
        store(filled)
        gw["w_in"][i] = dw_in_t.reshape(4, dw_in_t.shape[0] // 4, d)
        if i == 1:
            def inproj(ex):
                (dx, dg_mix[i]), landed = _inproj_bwd(dproj, dx1, st["x0"], g_mix[i].reshape(1, d),
                                                      full["w_in"][i], f"inproj_bwd_{i}", ex)
                st["dx"] = dx
                return landed

            to_sibling([("w_in", 1)], inproj)
            dx = st["dx"]
        else:
            keys = [("w_in", 0)]
            to_sibling(keys, lambda ex: _run_exchange("rs_to_sibling_last", ex))
            (dx, dg_mix[i]), parts = _inproj_bwd(dproj, dx1, st["x0"], g_mix[i].reshape(1, d),
                                                 full["w_in"][i], f"inproj_bwd_{i}", to_chips(keys))
            sum_chips(keys, parts)
            swap, store = swap_halves(keys)
            store(_run_exchange("exchange_halves_last", swap))
    grad_x = dx[None]

    outs = {}
    for n in BIG:
        g0, g1 = (reduced[n, l].reshape(wl[n].shape[1:]) for l in range(depth))
        outs[n] = _adamw(wl[n], ml[n], vl[n], g0, g1, "adamw_" + n)
    outs["w_in"] = [jnp.swapaxes(a, 1, 2) for a in outs["w_in"]]

    drel_bias = drel[:, :N_BUCKETS].T
    gsmall = _pack_small(jnp.concatenate(dg_mix, 0), jnp.concatenate(dg_mlp, 0), jnp.concatenate(dg_pe, 0),
                         dg_final[0], jnp.stack(dsinks), drel_bias, loss_part[0, :1])
    wsmall = _pack_small(g_mix, g_mlp, g_pe, g_final, sinks, rel_bias)
    msmall = _pack_small(m_g_mix, m_g_mlp, m_g_pe, m_g_final, m_sinks, m_rel_bias)
    vsmall = _pack_small(v_g_mix, v_g_mlp, v_g_pe, v_g_final, v_sinks, v_rel_bias)
    small = _small_allreduce_adamw(gsmall, wsmall, msmall, vsmall)
    loss = small[0][7, 0]
    small = [_unpack_small(a, sinks.shape, rel_bias.shape) for a in small]

    result = [loss, grad_x]
    for kind in range(4):
        result += [outs[n][kind] for n in BIG]
        result += list(small[kind])
    return tuple(result)
```

```python
import functools
import math
from typing import Callable, NamedTuple

import numpy as np
import jax
import jax.numpy as jnp
from jax import lax
from jax.experimental import pallas as pl
from jax.experimental.pallas import tpu as pltpu

F32 = jnp.float32
BF16 = jnp.bfloat16
MESH = pl.DeviceIdType.MESH

HEAD_DIM = 64
SB_HEADS = 8
SW_HEADS = 8
SW_KV_HEADS = 2
WINDOW = 128
N_BUCKETS = 32
MAX_DISTANCE = 128
EPS = 1e-6
SB_W = SB_HEADS * HEAD_DIM
SW_QW = SW_HEADS * HEAD_DIM
SW_KVW = SW_KV_HEADS * HEAD_DIM
QKV_W = 3 * SB_W + SW_QW + 2 * SW_KVW
SCALE = HEAD_DIM ** -0.5
assert SCALE == 0.125
LANES = 128
TQ = 128
BK = 128
NEG = -1e30
SB_EXHAUSTED = -106.0

ADAM_LR = 0.001
ADAM_B1 = 0.9
ADAM_B2 = 0.999
ADAM_EPS = 1e-08
ADAM_WD = 0.01
ADAM_STEP = 10

VMEM_LIMIT = 56 * 1024 * 1024


def _dot(a, b):
    return jnp.dot(a, b, preferred_element_type=F32)


def _dot_nt(a, b):
    return lax.dot_general(a, b, (((1,), (1,)), ((), ())), preferred_element_type=F32)


def _dot_tn(a, b):
    return lax.dot_general(a, b, (((0,), (0,)), ((), ())), preferred_element_type=F32)


def _sum_all(x):
    return jnp.sum(jnp.sum(x, axis=1, keepdims=True), axis=0, keepdims=True)


def _sigmoid(x):
    return 1.0 / (1.0 + jnp.exp(-x))


def _rms(x, g):
    r = lax.rsqrt(jnp.mean(x * x, axis=-1, keepdims=True) + EPS)
    return (x * r) * g


def _rms_bwd(dy, x, g):
    r = lax.rsqrt(jnp.mean(x * x, axis=-1, keepdims=True) + EPS)
    n = x * r
    dg = jnp.sum(dy * n, axis=0, keepdims=True)
    dn = dy * g
    dx = r * (dn - n * jnp.mean(dn * n, axis=-1, keepdims=True))
    return dx, dg


def _params(n_axes):
    return pltpu.CompilerParams(dimension_semantics=("arbitrary",) * n_axes, vmem_limit_bytes=VMEM_LIMIT)


def _rowcall(name, body, row_ins, const_ins, row_outs, acc_outs=(), tm=512, side=None):
    s = row_ins[0].shape[0]
    assert s % tm == 0
    in_specs = [pl.BlockSpec((tm, a.shape[1]), lambda i: (i, 0)) for a in row_ins]
    in_specs += [pl.BlockSpec(a.shape, functools.partial(lambda i, nd: (0,) * nd, nd=a.ndim)) for a in const_ins]
    out_shape = [jax.ShapeDtypeStruct((s, c), dt) for c, dt in row_outs]
    out_specs = [pl.BlockSpec((tm, c), lambda i: (i, 0)) for c, _ in row_outs]
    out_shape += [jax.ShapeDtypeStruct(sh, dt) for sh, dt in acc_outs]
    out_specs += [pl.BlockSpec(sh, functools.partial(lambda i, nd: (0,) * nd, nd=len(sh))) for sh, _ in acc_outs]
    if side is None:
        return pl.pallas_call(body, out_shape=out_shape, grid=(s // tm,), in_specs=in_specs, out_specs=out_specs,
                              compiler_params=_params(1), name=name)(*row_ins, *const_ins)
    n_in, n_out = len(in_specs), len(out_specs)

    def with_side(*refs):
        side_in, outs, side_out, sems = _side_refs(side, refs[n_in:], n_out)

        @pl.when(pl.program_id(0) == 0)
        def _():
            side.start(side_in, side_out, *sems)

        body(*refs[:n_in], *outs)

        @pl.when(pl.program_id(0) == s // tm - 1)
        def _():
            side.finish(side_in, side_out, *sems)

    s_in, s_shape, s_out, s_alias, s_sems = _side_specs(side, n_in, n_out)
    outs = pl.pallas_call(with_side, out_shape=out_shape + s_shape, grid=(s // tm,), in_specs=in_specs + s_in,
                          out_specs=out_specs + s_out, input_output_aliases=s_alias, scratch_shapes=s_sems,
                          compiler_params=_params(1), name=name)(*row_ins, *const_ins, *side.arrays)
    return outs[:n_out], outs[n_out:]


def _dot_cols(a, w_ref):
    return jnp.concatenate([_dot(a, w_ref[r]) for r in range(w_ref.shape[0])], axis=1)


def _dot_cols_t(a, w_ref):
    n = w_ref.shape[2]
    out = _dot_nt(a[:, :n], w_ref[0])
    for r in range(1, w_ref.shape[0]):
        out = out + _dot_nt(a[:, r * n:(r + 1) * n], w_ref[r])
    return out


def _inproj_fwd(x, g, wt, name):
    d = x.shape[1]

    def body(x_ref, g_ref, w_ref, h_ref, qkv_ref, gate_ref):
        hb = _rms(x_ref[...], g_ref[...]).astype(BF16)
        h_ref[...] = hb
        qkv_ref[...] = _dot_nt(hb, w_ref[:QKV_W, :]).astype(BF16)
        gate_ref[...] = _dot_nt(hb, w_ref[QKV_W:, :])

    return _rowcall(name, body, [x], [g, wt], [(d, BF16), (QKV_W, BF16), (2 * d, F32)])


def _mixer_fwd(oa, ob, gates, x, wua, wub, wo, name):
    d = x.shape[1]

    def body(oa_ref, ob_ref, gate_ref, x_ref, wua_ref, wub_ref, wo_ref, m_ref, x1_ref):
        ya = _dot_cols(oa_ref[...], wua_ref)
        yb = _dot_cols(ob_ref[...], wub_ref)
        m = _sigmoid(gate_ref[:, :d]) * ya + _sigmoid(gate_ref[:, d:]) * yb
        mb = m.astype(BF16)
        m_ref[...] = mb
        x1_ref[...] = x_ref[...] + _dot(mb, wo_ref[...])

    return _rowcall(name, body, [oa, ob, gates, x], [wua, wub, wo], [(d, BF16), (d, F32)])


def _ff1_fwd(x1, g, w1, name):
    _, d, nq = w1.shape
    dff = 4 * nq

    def body(x_ref, g_ref, w_ref, h_ref, r_ref, a_ref):
        hb = _rms(x_ref[...], g_ref[...]).astype(BF16)
        h_ref[...] = hb
        r = jnp.maximum(_dot_cols(hb, w_ref), 0.0)
        r_ref[...] = r.astype(BF16)
        a_ref[...] = jnp.square(r).astype(BF16)

    return _rowcall(name, body, [x1], [g, w1], [(d, BF16), (dff, BF16), (dff, BF16)])


def _ff2_fwd(a, x1, w2, name):
    d = x1.shape[1]

    def body(a_ref, x_ref, w_ref, o_ref):
        o_ref[...] = x_ref[...] + _dot(a_ref[...], w_ref[...])

    return _rowcall(name, body, [a, x1], [w2], [(d, F32)])[0]


def _ple_fwd(p, x2, g, wpe, wpg, name):
    d = x2.shape[1]

    def body(p_ref, x_ref, g_ref, wpe_ref, wpg_ref, pb_ref, h_ref, pe_ref, gt_ref, x3_ref):
        pb = p_ref[...].astype(BF16)
        pb_ref[...] = pb
        pe = _dot_cols(pb, wpe_ref)
        x = x_ref[...]
        hb = _rms(x, g_ref[...]).astype(BF16)
        h_ref[...] = hb
        gt = _dot(hb, wpg_ref[...])
        pe_ref[...] = pe
        gt_ref[...] = gt
        x3_ref[...] = x + pe * _sigmoid(gt)

    return _rowcall(name, body, [p, x2], [g, wpe, wpg],
                    [(p.shape[1], BF16), (d, BF16), (d, F32), (d, F32), (d, F32)])


def _pair_stack(t, lane):
    zero = jnp.zeros_like(t)
    return jnp.concatenate([jnp.where(lane < HEAD_DIM, t, zero), jnp.where(lane >= HEAD_DIM, t, zero)], axis=0)


def _sb_rel():
    row = lax.broadcasted_iota(jnp.int32, (2 * TQ, BK), 0)
    row = jnp.where(row >= TQ, row - TQ, row)
    col = lax.broadcasted_iota(jnp.int32, (2 * TQ, BK), 1)
    return col - row


def _split_dot(x, m01, two_pass=True):
    hi = x.astype(BF16)
    if not two_pass:
        return _dot(hi, m01)
    lo = (x - hi.astype(F32)).astype(BF16)
    return _dot(hi, m01) + _dot(lo, m01)


def _sb_scores(qs, k, mask):
    z = _dot_nt(qs, k)
    lb = jnp.minimum(z, 0.0) - jnp.log(1.0 + jnp.exp(-jnp.abs(z)))
    lm = lb - z
    return lb, lm if mask is None else jnp.where(mask, lm, 0.0)


SB_STRAIGHT = 3
SB_WIDE = SB_STRAIGHT * BK
SB_QB = 2
SWA_QB = 4


def _sb_wide_consts():
    j = np.arange(BK)[:, None]
    s = np.arange(BK)[None, :]
    ones = np.ones((BK, BK), np.float32)
    as_bf16 = lambda m: jnp.asarray(np.concatenate([m, ones], axis=1).astype(np.float32), dtype=BF16)
    return as_bf16(j > s), as_bf16(j <= s), as_bf16(j < s)


def _wide_sums(x, m01, suffix, two_pass=True):
    parts = [_split_dot(x[:, b * BK:(b + 1) * BK], m01, two_pass) for b in range(SB_STRAIGHT)]
    order = range(SB_STRAIGHT - 1, -1, -1) if suffix else range(SB_STRAIGHT)
    out = [None] * SB_STRAIGHT
    carry = None
    for b in order:
        out[b] = parts[b][:, :BK] if carry is None else parts[b][:, :BK] + carry
        carry = parts[b][:, BK:] if carry is None else carry + parts[b][:, BK:]
    return jnp.concatenate(out, axis=1), carry


def _side_refs(ex, rest, n_out):
    n_in = len(ex.arrays) if ex else 0
    n_alias = len(ex.aliased) if ex else 0
    ins, rest = rest[:n_in], rest[n_in:]
    outs, rest = rest[:n_out], rest[n_out:]
    return ins, outs, rest[:n_alias], rest[n_alias:]


def _side_specs(ex, n_in, n_out):
    if ex is None:
        return [], [], [], {}, []
    any_spec = pl.BlockSpec(memory_space=pl.ANY)
    return ([any_spec] * len(ex.arrays),
            [jax.ShapeDtypeStruct(ex.arrays[a].shape, ex.arrays[a].dtype) for a in ex.aliased],
            [any_spec] * len(ex.aliased), {n_in + a: n_out + o for o, a in enumerate(ex.aliased)},
            [pltpu.SemaphoreType.DMA(ex.sems), pltpu.SemaphoreType.DMA(ex.sems)])


def _sb_fwd(qkv, name, side=None):
    s = qkv.shape[0]
    nq = s // TQ
    npair = SB_W // LANES
    sufw = _sb_wide_consts()[0]

    def body(q_ref, k_ref, v_ref, sufw_ref, *rest):
        side_in, (o_ref, lt_ref, nb_ref, a_ref, sg_ref), side_out, scratch = _side_refs(side, rest, 5)
        cf_ref, acc_ref = scratch[:2]
        step_id = pl.program_id(1)
        if side is not None:
            @pl.when((pl.program_id(0) == 0) & (step_id == 0))
            def _():
                side.start(side_in, side_out, *scratch[2:])
        lane = lax.broadcasted_iota(jnp.int32, (TQ, LANES), 1)
        rel = _sb_rel()
        blocks = [step_id * SB_QB + b for b in range(SB_QB)]
        qs = [_pair_stack(q_ref[b * TQ:(b + 1) * TQ, :] * SCALE, lane) for b in range(SB_QB)]

        straight = blocks[0] >= SB_STRAIGHT - 1

        @pl.when(straight)
        def _():
            for b, i in enumerate(blocks):
                w0 = pl.multiple_of((i - (SB_STRAIGHT - 1)) * BK, BK)
                kw = k_ref[pl.ds(w0, SB_WIDE), :]
                lb, lm = _sb_scores(qs[b], kw, None)
                own = rel < 0
                past = SB_WIDE - BK
                lm = jnp.concatenate([lm[:, :past], jnp.where(own, lm[:, past:], 0.0)], axis=1)
                after, total = _wide_sums(lm, sufw_ref[...], True)
                on_past_keys = lambda t: jnp.concatenate([t[:, :past], jnp.where(own, t[:, past:], 0.0)], axis=1)
                a = on_past_keys(jnp.exp(lb + after)).astype(BF16)
                acc_ref[b] = _dot(a, v_ref[pl.ds(w0, SB_WIDE), :])
                cf_ref[b] = total
                a_ref[b] = a
                sg_ref[b] = on_past_keys(jnp.exp(lb)).astype(BF16)

        @pl.when(jnp.logical_not(straight))
        def _():
            cf_ref[...] = jnp.zeros_like(cf_ref)
            acc_ref[...] = jnp.zeros_like(acc_ref)

        for b, i in enumerate(blocks):
            q0 = i * TQ

            def more(c, i=i):
                return (c[0] <= i) & (c[1] > SB_EXHAUSTED)

            def step(c, b=b, i=i, q0=q0):
                k0 = pl.multiple_of((i - c[0]) * BK, BK)
                k = k_ref[pl.ds(k0, BK), :]
                v = v_ref[pl.ds(k0, BK), :]
                mask = rel < (q0 - k0)
                lb, lm = _sb_scores(qs[b], k, mask)
                cs = _split_dot(lm, sufw_ref[...])
                a = jnp.where(mask, jnp.exp(lb + cs[:, :BK] + cf_ref[b]), 0.0)
                acc_ref[b] += _dot(a.astype(BF16), v)
                cf = cf_ref[b] + cs[:, BK:]
                cf_ref[b] = cf
                return c[0] + 1, jnp.max(cf)

            n_blocks, _ = lax.while_loop(
                more, step, (jnp.where(straight, SB_STRAIGHT, 0).astype(jnp.int32), jnp.max(cf_ref[b])))
            o_ref[b * TQ:(b + 1) * TQ, :] = jnp.where(lane < HEAD_DIM, acc_ref[b, :TQ, :],
                                                     acc_ref[b, TQ:, :]).astype(BF16)
            lt_ref[b] = cf_ref[b]
            nb_ref[b] = jnp.full(nb_ref.shape[1:], n_blocks, F32)
        if side is not None:
            @pl.when((pl.program_id(0) == npair - 1) & (step_id == nq // SB_QB - 1))
            def _():
                side.finish(side_in, side_out, *scratch[2:])

    s_in, s_shape, s_out, s_alias, s_sems = _side_specs(side, 4, 5)
    wide = jax.ShapeDtypeStruct((npair, nq, 2 * TQ, SB_WIDE), BF16)
    wide_spec = pl.BlockSpec((None, SB_QB, 2 * TQ, SB_WIDE), lambda j, i: (j, i, 0, 0))
    outs = pl.pallas_call(
        body,
        out_shape=[jax.ShapeDtypeStruct((s, SB_W), BF16), jax.ShapeDtypeStruct((npair, nq, 2 * TQ, BK), F32),
                   jax.ShapeDtypeStruct((npair, nq, 8, LANES), F32), wide, wide] + s_shape,
        grid=(npair, nq // SB_QB),
        in_specs=[pl.BlockSpec((SB_QB * TQ, LANES), lambda j, i: (i, j)),
                  pl.BlockSpec((s, LANES), lambda j, i: (0, npair + j)),
                  pl.BlockSpec((s, LANES), lambda j, i: (0, 2 * npair + j)),
                  pl.BlockSpec(sufw.shape, lambda j, i: (0, 0))] + s_in,
        out_specs=[pl.BlockSpec((SB_QB * TQ, LANES), lambda j, i: (i, j)),
                   pl.BlockSpec((None, SB_QB, 2 * TQ, BK), lambda j, i: (j, i, 0, 0)),
                   pl.BlockSpec((None, SB_QB, 8, LANES), lambda j, i: (j, i, 0, 0)), wide_spec, wide_spec] + s_out,
        input_output_aliases=s_alias,
        scratch_shapes=[pltpu.VMEM((SB_QB, 2 * TQ, BK), F32), pltpu.VMEM((SB_QB, 2 * TQ, LANES), F32)] + s_sems,
        compiler_params=_params(2), name=name)(qkv, qkv, qkv, sufw, *(side.arrays if side else ()))
    return outs[:5], outs[5:]


def _sb_bwd(qkv, lt, nb, a_wide, sg_wide, doa, name, side=None):
    s = qkv.shape[0]
    nq = s // TQ
    npair = SB_W // LANES
    _, prew, prexw = _sb_wide_consts()

    def body(q_ref, k_ref, v_ref, lt_ref, nb_ref, do_ref, prew_ref, prexw_ref, a_ref, sg_ref, *rest):
        side_in, (dq_ref, dk_out, dv_out), side_out, scratch = _side_refs(side, rest, 3)
        cp_ref, ce_ref, dqa_ref, dk_ref, dv_ref = scratch[:5]
        sems = scratch[5:]
        step_id = pl.program_id(1)
        if side is not None:
            @pl.when((pl.program_id(0) == 0) & (step_id == 0))
            def _():
                side.start(side_in, side_out, *sems)
        lane = lax.broadcasted_iota(jnp.int32, (TQ, LANES), 1)
        rel = _sb_rel()
        blocks = [step_id * SB_QB + b for b in range(SB_QB)]
        rows = [slice(b * TQ, (b + 1) * TQ) for b in range(SB_QB)]
        qs = [_pair_stack(q_ref[rows[b], :] * SCALE, lane) for b in range(SB_QB)]
        dos = [_pair_stack(do_ref[rows[b], :], lane) for b in range(SB_QB)]
        n_blocks = [jnp.clip(jnp.max(nb_ref[b]).astype(jnp.int32), 1, i + 1) for b, i in enumerate(blocks)]
        first = [i + 1 - n for i, n in zip(blocks, n_blocks)]

        @pl.when(step_id == 0)
        def _():
            dk_ref[...] = jnp.zeros_like(dk_ref)
            dv_ref[...] = jnp.zeros_like(dv_ref)

        straight = n_blocks[0] == SB_STRAIGHT
        for n in n_blocks[1:]:
            straight = straight & (n == SB_STRAIGHT)

        @pl.when(straight)
        def _():
            for b in range(SB_QB):
                w0 = pl.multiple_of(first[b] * BK, BK)
                kw = k_ref[pl.ds(w0, SB_WIDE), :]
                vw = v_ref[pl.ds(w0, SB_WIDE), :]
                a = a_ref[b]
                e = a.astype(F32) * _dot_nt(dos[b], vw)
                big_e, _ = _wide_sums(e, prexw_ref[...], False, two_pass=False)
                dz = (e - sg_ref[b].astype(F32) * (e + big_e)).astype(BF16)
                dk_ref[pl.ds(w0, SB_WIDE), :] += _dot_tn(dz, qs[b])
                dv_ref[pl.ds(w0, SB_WIDE), :] += _dot_tn(a, dos[b])
                dqa_ref[b] = _dot(dz, kw)

        @pl.when(jnp.logical_not(straight))
        def _():
            cp_ref[...] = jnp.zeros_like(cp_ref)
            ce_ref[...] = jnp.zeros_like(ce_ref)
            dqa_ref[...] = jnp.zeros_like(dqa_ref)
            for b, i in enumerate(blocks):
                q0 = i * TQ

                def step(it, carry, b=b, q0=q0):
                    k0 = pl.multiple_of((first[b] + it) * BK, BK)
                    k = k_ref[pl.ds(k0, BK), :]
                    v = v_ref[pl.ds(k0, BK), :]
                    mask = rel < (q0 - k0)
                    lb, lm = _sb_scores(qs[b], k, mask)
                    cs = _split_dot(lm, prew_ref[...])
                    a = jnp.where(mask, jnp.exp(lb + (lt_ref[b] - (cs[:, :BK] + cp_ref[b]))), 0.0)
                    e = a * _dot_nt(dos[b], v)
                    ce = _split_dot(e, prexw_ref[...], two_pass=False)
                    big_e = ce[:, :BK] + ce_ref[b]
                    dz = jnp.where(mask, e - jnp.exp(lb) * (e + big_e), 0.0).astype(BF16)
                    dk_ref[pl.ds(k0, BK), :] += _dot_tn(dz, qs[b])
                    dv_ref[pl.ds(k0, BK), :] += _dot_tn(a.astype(BF16), dos[b])
                    dqa_ref[b] += _dot(dz, k)
                    cp_ref[b] += cs[:, BK:]
                    ce_ref[b] += ce[:, BK:]
                    return carry

                lax.fori_loop(0, n_blocks[b], step, 0)

        for b in range(SB_QB):
            dq = jnp.where(lane < HEAD_DIM, dqa_ref[b, :TQ, :], dqa_ref[b, TQ:, :])
            dq_ref[rows[b], :] = (dq * SCALE).astype(BF16)

        @pl.when(step_id == nq // SB_QB - 1)
        def _():
            dk_out[...] = dk_ref[...].astype(BF16)
            dv_out[...] = dv_ref[...].astype(BF16)

        if side is not None:
            @pl.when((pl.program_id(0) == npair - 1) & (step_id == nq // SB_QB - 1))
            def _():
                side.finish(side_in, side_out, *sems)

    s_in, s_shape, s_out, s_alias, s_sems = _side_specs(side, 10, 3)
    wide_spec = pl.BlockSpec((None, SB_QB, 2 * TQ, SB_WIDE), lambda j, i: (j, i, 0, 0))
    outs = pl.pallas_call(
        body,
        out_shape=[jax.ShapeDtypeStruct((s, SB_W), BF16)] * 3 + s_shape,
        grid=(npair, nq // SB_QB),
        in_specs=[pl.BlockSpec((SB_QB * TQ, LANES), lambda j, i: (i, j)),
                  pl.BlockSpec((s, LANES), lambda j, i: (0, npair + j)),
                  pl.BlockSpec((s, LANES), lambda j, i: (0, 2 * npair + j)),
                  pl.BlockSpec((None, SB_QB, 2 * TQ, BK), lambda j, i: (j, i, 0, 0)),
                  pl.BlockSpec((None, SB_QB, 8, LANES), lambda j, i: (j, i, 0, 0)),
                  pl.BlockSpec((SB_QB * TQ, LANES), lambda j, i: (i, j)),
                  pl.BlockSpec(prew.shape, lambda j, i: (0, 0)),
                  pl.BlockSpec(prexw.shape, lambda j, i: (0, 0)), wide_spec, wide_spec] + s_in,
        out_specs=[pl.BlockSpec((SB_QB * TQ, LANES), lambda j, i: (i, j)),
                   pl.BlockSpec((s, LANES), lambda j, i: (0, j)),
                   pl.BlockSpec((s, LANES), lambda j, i: (0, j))] + s_out,
        input_output_aliases=s_alias,
        scratch_shapes=[pltpu.VMEM((SB_QB, 2 * TQ, BK), F32), pltpu.VMEM((SB_QB, 2 * TQ, BK), F32),
                        pltpu.VMEM((SB_QB, 2 * TQ, LANES), F32), pltpu.VMEM((s, LANES), F32),
                        pltpu.VMEM((s, LANES), F32)] + s_sems,
        compiler_params=_params(2), name=name)(qkv, qkv, qkv, lt, nb, doa, prew, prexw, a_wide, sg_wide,
                                               *(side.arrays if side else ()))
    return outs[0], outs[1], outs[2], outs[3:]


def _bucket_table():
    i = np.arange(TQ)[:, None]
    j = np.arange(2 * BK)[None, :]
    dist = np.maximum(TQ + i - j, 0)
    max_exact = N_BUCKETS // 2
    df = np.maximum(dist, 1).astype(np.float32)
    large = max_exact + (np.log(df / np.float32(max_exact)) / np.float32(math.log(MAX_DISTANCE / max_exact))
                         * np.float32(N_BUCKETS - max_exact)).astype(np.int32)
    large = np.minimum(large, N_BUCKETS - 1)
    return np.where(dist < max_exact, dist, large).astype(np.int32)


def _swa_align_in(t, lane, g):
    tf = t.astype(F32)
    tr = pltpu.roll(tf, HEAD_DIM, 1)
    gmask = (lane >= HEAD_DIM) == (g == 1)
    top = jnp.where(gmask, jnp.where(g == 0, tf, tr), 0.0)
    bot = jnp.where(gmask, jnp.where(g == 1, tf, tr), 0.0)
    return jnp.concatenate([top, bot], axis=0).astype(BF16)


def _swa_align_out(t, lane, g):
    top, bot = t[:TQ, :], t[TQ:, :]
    top = jnp.where(g == 0, top, pltpu.roll(top, HEAD_DIM, 1))
    bot = jnp.where(g == 1, bot, pltpu.roll(bot, HEAD_DIM, 1))
    return jnp.where(lane < HEAD_DIM, top, bot)


def _swa_bias(bias_ref, bucket_ref, rb_ref, j):
    dist = TQ + lax.broadcasted_iota(jnp.int32, (TQ, 2 * BK), 0) - lax.broadcasted_iota(jnp.int32, (TQ, 2 * BK), 1)
    window = (dist >= 0) & (dist < WINDOW)
    for hh in range(2):
        def add(b, acc):
            return acc + jnp.where(bucket_ref[...] == b, rb_ref[b, 2 * j + hh], 0.0)
        bias = lax.fori_loop(0, N_BUCKETS, add, jnp.zeros((TQ, 2 * BK), F32))
        bias_ref[hh * TQ:(hh + 1) * TQ, :] = jnp.where(window, bias, NEG)


def _swa_probs(qs, k2, bias, own_block, sink_ref, i, j):
    s = _dot_nt(qs, k2) + bias
    s = jnp.where(own_block | (i > 0), s, NEG)
    row1 = lax.broadcasted_iota(jnp.int32, (2 * TQ, 1), 0)
    sink = jnp.where(row1 < TQ, sink_ref[2 * j], sink_ref[2 * j + 1])
    m = jnp.maximum(jnp.max(s, axis=1, keepdims=True), sink)
    e = jnp.exp(s - m)
    es = jnp.exp(sink - m)
    inv = 1.0 / (jnp.sum(e, axis=1, keepdims=True) + es)
    return e * inv, es * inv


def _swa_kv(ref, i):
    prev = pl.multiple_of(jnp.maximum(i - 1, 0) * BK, BK)
    cur = pl.multiple_of(i * BK, BK)
    return jnp.concatenate([ref[pl.ds(prev, BK), :], ref[pl.ds(cur, BK), :]], axis=0), prev, cur


def _swa_fwd(qkv, sinks, rel_bias, name, side=None):
    s = qkv.shape[0]
    nq = s // TQ
    npair = SW_QW // LANES
    qcol = 3 * SB_W // LANES
    bucket = jnp.asarray(_bucket_table())

    def body(q_ref, k_ref, v_ref, bucket_ref, sink_ref, rb_ref, *rest):
        side_in, (o_ref, p_ref), side_out, scratch = _side_refs(side, rest, 2)
        bias_ref, sems = scratch[0], scratch[1:]
        j = pl.program_id(0)
        step = pl.program_id(1)
        if side is not None:
            @pl.when((j == 0) & (step == 0))
            def _():
                side.start(side_in, side_out, *sems)
        g = j // 2
        lane = lax.broadcasted_iota(jnp.int32, (TQ, LANES), 1)

        @pl.when(step == 0)
        def _():
            _swa_bias(bias_ref, bucket_ref, rb_ref, j)

        own_block = lax.broadcasted_iota(jnp.int32, (2 * TQ, 2 * BK), 1) >= BK
        for b in range(SWA_QB):
            i = step * SWA_QB + b
            rows = slice(b * TQ, (b + 1) * TQ)
            qs = _swa_align_in(q_ref[rows, :] * SCALE, lane, g)
            k2, _, _ = _swa_kv(k_ref, i)
            v2, _, _ = _swa_kv(v_ref, i)
            pr, psink = _swa_probs(qs, k2, bias_ref[...], own_block, sink_ref, i, j)
            prb = pr.astype(BF16)
            o_ref[rows, :] = _swa_align_out(_dot(prb, v2), lane, g).astype(BF16)
            p_ref[b, :, :2 * BK] = prb
            p_ref[b, :, 2 * BK:] = jnp.broadcast_to(psink, (2 * TQ, BK)).astype(BF16)
        if side is not None:
            @pl.when((j == npair - 1) & (step == nq // SWA_QB - 1))
            def _():
                side.finish(side_in, side_out, *sems)

    assert nq % SWA_QB == 0
    s_in, s_shape, s_out, s_alias, s_sems = _side_specs(side, 6, 2)
    outs = pl.pallas_call(
        body,
        out_shape=[jax.ShapeDtypeStruct((s, SW_QW), BF16),
                   jax.ShapeDtypeStruct((npair, nq, 2 * TQ, 3 * BK), BF16)] + s_shape,
        grid=(npair, nq // SWA_QB),
        in_specs=[pl.BlockSpec((SWA_QB * TQ, LANES), lambda j, i: (i, qcol + j)),
                  pl.BlockSpec((s, LANES), lambda j, i: (0, qcol + npair)),
                  pl.BlockSpec((s, LANES), lambda j, i: (0, qcol + npair + 1)),
                  pl.BlockSpec((TQ, 2 * BK), lambda j, i: (0, 0)),
                  pl.BlockSpec(memory_space=pltpu.SMEM),
                  pl.BlockSpec(memory_space=pltpu.SMEM)] + s_in,
        out_specs=[pl.BlockSpec((SWA_QB * TQ, LANES), lambda j, i: (i, j)),
                   pl.BlockSpec((None, SWA_QB, 2 * TQ, 3 * BK), lambda j, i: (j, i, 0, 0))] + s_out,
        input_output_aliases=s_alias,
        scratch_shapes=[pltpu.VMEM((2 * TQ, 2 * BK), F32)] + s_sems,
        compiler_params=_params(2), name=name)(qkv, qkv, qkv, bucket, sinks, rel_bias,
                                               *(side.arrays if side else ()))
    return outs[:2], outs[2:]


def _swa_bwd(qkv, ob, dob, probs, name, side=None):
    s = qkv.shape[0]
    nq = s // TQ
    npair = SW_QW // LANES
    qcol = 3 * SB_W // LANES
    bucket = jnp.asarray(_bucket_table())

    def body(q_ref, k_ref, v_ref, o_ref, do_ref, bucket_ref, p_ref, *rest):
        side_in, (dq_ref, dkv_ref, dsink_ref, drel_ref), side_out, scratch = _side_refs(side, rest, 4)
        dsacc_ref, dk_ref, dv_ref = scratch[:3]
        sems = scratch[3:]
        j = pl.program_id(0)
        step = pl.program_id(1)
        if side is not None:
            @pl.when((j == 0) & (step == 0))
            def _():
                side.start(side_in, side_out, *sems)
        g = j // 2
        lane = lax.broadcasted_iota(jnp.int32, (TQ, LANES), 1)
        row8 = lax.broadcasted_iota(jnp.int32, (SW_HEADS, LANES), 0)
        lane8 = lax.broadcasted_iota(jnp.int32, (SW_HEADS, LANES), 1)

        @pl.when((step == 0) & (j == 0))
        def _():
            dk_ref[...] = jnp.zeros_like(dk_ref)
            dv_ref[...] = jnp.zeros_like(dv_ref)
            dsink_ref[...] = jnp.zeros_like(dsink_ref)
            drel_ref[...] = jnp.zeros_like(drel_ref)

        @pl.when(step == 0)
        def _():
            dsacc_ref[...] = jnp.zeros_like(dsacc_ref)

        ds_sum = jnp.zeros(dsacc_ref.shape, F32)
        dsink = jnp.zeros((SW_HEADS, LANES), F32)
        for b in range(SWA_QB):
            i = step * SWA_QB + b
            rows = slice(b * TQ, (b + 1) * TQ)
            qs = _swa_align_in(q_ref[rows, :] * SCALE, lane, g)
            do = do_ref[rows, :]
            dos = _swa_align_in(do, lane, g)
            dof = do.astype(F32) * o_ref[rows, :].astype(F32)
            d0 = jnp.sum(jnp.where(lane < HEAD_DIM, dof, 0.0), axis=1, keepdims=True)
            d1 = jnp.sum(jnp.where(lane >= HEAD_DIM, dof, 0.0), axis=1, keepdims=True)
            delta = jnp.concatenate([d0, d1], axis=0)
            k2, prev, cur = _swa_kv(k_ref, i)
            v2, _, _ = _swa_kv(v_ref, i)
            prb = p_ref[b, :, :2 * BK]
            ds = prb.astype(F32) * (_dot_nt(dos, v2) - delta)
            ds_sum = ds_sum + ds
            sd = p_ref[b, :, 2 * BK:].astype(F32) * delta
            ds0 = -jnp.sum(sd[:TQ, :], axis=0, keepdims=True)
            ds1 = -jnp.sum(sd[TQ:, :], axis=0, keepdims=True)
            dsink = dsink + jnp.where(row8 == 2 * j, ds0, jnp.where(row8 == 2 * j + 1, ds1, 0.0))
            dsb = ds.astype(BF16)
            dq_ref[rows, :] = _swa_align_out(_dot(dsb, k2) * SCALE, lane, g).astype(BF16)
            dk2 = _dot_tn(dsb, qs)
            dv2 = _dot_tn(prb, dos)
            dk_ref[pl.ds(prev, BK), :] += dk2[:BK, :]
            dk_ref[pl.ds(cur, BK), :] += dk2[BK:, :]
            dv_ref[pl.ds(prev, BK), :] += dv2[:BK, :]
            dv_ref[pl.ds(cur, BK), :] += dv2[BK:, :]
        dsacc_ref[...] += ds_sum
        dsink_ref[...] += dsink

        @pl.when(step == nq // SWA_QB - 1)
        def _():
            for hh in range(2):
                def red(b, acc):
                    val = _sum_all(jnp.where(bucket_ref[...] == b, dsacc_ref[hh * TQ:(hh + 1) * TQ, :], 0.0))
                    return jnp.where((row8 == 2 * j + hh) & (lane8 == b), val, acc)
                drel_ref[...] += lax.fori_loop(0, N_BUCKETS, red, jnp.zeros((SW_HEADS, LANES), F32))

        @pl.when((step == nq // SWA_QB - 1) & (j == npair - 1))
        def _():
            dkv_ref[:, :LANES] = dk_ref[...].astype(BF16)
            dkv_ref[:, LANES:] = dv_ref[...].astype(BF16)
            if side is not None:
                side.finish(side_in, side_out, *sems)

    whole = lambda j, i: (0, 0)
    s_in, s_shape, s_out, s_alias, s_sems = _side_specs(side, 7, 4)
    outs = pl.pallas_call(
        body,
        out_shape=[jax.ShapeDtypeStruct((s, SW_QW), BF16), jax.ShapeDtypeStruct((s, 2 * LANES), BF16),
                   jax.ShapeDtypeStruct((SW_HEADS, LANES), F32), jax.ShapeDtypeStruct((SW_HEADS, LANES), F32)] + s_shape,
        grid=(npair, nq // SWA_QB),
        in_specs=[pl.BlockSpec((SWA_QB * TQ, LANES), lambda j, i: (i, qcol + j)),
                  pl.BlockSpec((s, LANES), lambda j, i: (0, qcol + npair)),
                  pl.BlockSpec((s, LANES), lambda j, i: (0, qcol + npair + 1)),
                  pl.BlockSpec((SWA_QB * TQ, LANES), lambda j, i: (i, j)),
                  pl.BlockSpec((SWA_QB * TQ, LANES), lambda j, i: (i, j)),
                  pl.BlockSpec((TQ, 2 * BK), whole),
                  pl.BlockSpec((None, SWA_QB, 2 * TQ, 3 * BK), lambda j, i: (j, i, 0, 0))] + s_in,
        out_specs=[pl.BlockSpec((SWA_QB * TQ, LANES), lambda j, i: (i, j)),
                   pl.BlockSpec((s, 2 * LANES), whole),
                   pl.BlockSpec((SW_HEADS, LANES), whole), pl.BlockSpec((SW_HEADS, LANES), whole)] + s_out,
        input_output_aliases=s_alias,
        scratch_shapes=[pltpu.VMEM((2 * TQ, 2 * BK), F32), pltpu.VMEM((s, LANES), F32),
                        pltpu.VMEM((s, LANES), F32)] + s_sems,
        compiler_params=_params(2), name=name)(qkv, qkv, qkv, ob, dob, bucket, probs,
                                               *(side.arrays if side else ()))
    return outs[:4], outs[4:]


def _acc_init(i, *refs):
    @pl.when(i == 0)
    def _():
        for r in refs:
            r[...] = jnp.zeros_like(r)


def _loss_bwd(x3, target, g, name):
    d = x3.shape[1]

    def body(x_ref, t_ref, g_ref, dx_ref, dg_ref, loss_ref):
        _acc_init(pl.program_id(0), dg_ref, loss_ref)
        x = x_ref[...]
        gv = g_ref[...]
        diff = _rms(x, gv) - t_ref[...]
        loss_ref[...] += 0.5 * jnp.sum(jnp.mean(jnp.square(diff), axis=-1, keepdims=True), axis=0, keepdims=True)
        dx, dg = _rms_bwd(diff * (1.0 / d), x, gv)
        dx_ref[...] = dx
        dg_ref[...] += dg

    return _rowcall(name, body, [x3, target], [g], [(d, F32)], [((1, d), F32), ((1, LANES), F32)])


def _ple_bwd(dx3, pe, gt, x2, g, wpg, name):
    d = x2.shape[1]

    def body(dx3_ref, pe_ref, gt_ref, x_ref, g_ref, w_ref, dpe_ref, dgt_ref, dx2_ref, dg_ref):
        _acc_init(pl.program_id(0), dg_ref)
        dx3 = dx3_ref[...]
        sg = _sigmoid(gt_ref[...])
        dpe_ref[...] = (dx3 * sg).astype(BF16)
        dgt = (dx3 * pe_ref[...] * sg * (1.0 - sg)).astype(BF16)
        dgt_ref[...] = dgt
        dx, dg = _rms_bwd(_dot_nt(dgt, w_ref[...]), x_ref[...], g_ref[...])
        dx2_ref[...] = dx3 + dx
        dg_ref[...] += dg

    return _rowcall(name, body, [dx3, pe, gt, x2], [g, wpg], [(d, BF16), (d, BF16), (d, F32)], [((1, d), F32)])


def _ff2_bwd(dx2, r, w2, name):
    d = dx2.shape[1]
    dff = r.shape[1]

    def body(dx_ref, r_ref, w_ref, du_ref, dxb_ref):
        dxb = dx_ref[...].astype(BF16)
        dxb_ref[...] = dxb
        du_ref[...] = (_dot_nt(dxb, w_ref[...]) * (2.0 * r_ref[...].astype(F32))).astype(BF16)

    return _rowcall(name, body, [dx2, r], [w2], [(dff, BF16), (d, BF16)])


def _ff1_bwd(du, dx2, x1, g, w1, name, side=None):
    d = x1.shape[1]

    def body(du_ref, dx2_ref, x_ref, g_ref, w_ref, dx1_ref, dx1b_ref, dg_ref):
        _acc_init(pl.program_id(0), dg_ref)
        dx, dg = _rms_bwd(_dot_cols_t(du_ref[...], w_ref), x_ref[...], g_ref[...])
        dx1 = dx2_ref[...] + dx
        dx1_ref[...] = dx1
        dx1b_ref[...] = dx1.astype(BF16)
        dg_ref[...] += dg

    return _rowcall(name, body, [du, dx2, x1], [g, w1], [(d, F32), (d, BF16)], [((1, d), F32)], side=side)


def _mixer_bwd(dx1b, gates, oa, ob, wo, wua, wub, name, side=None):
    d = dx1b.shape[1]

    def body(dx_ref, gate_ref, oa_ref, ob_ref, wo_ref, wua_ref, wub_ref,
             dya_ref, dyb_ref, dgate_ref, doa_ref, dob_ref):
        dm = _dot_nt(dx_ref[...], wo_ref[...])
        sa = _sigmoid(gate_ref[:, :d])
        sb = _sigmoid(gate_ref[:, d:])
        ya = _dot_cols(oa_ref[...], wua_ref)
        yb = _dot_cols(ob_ref[...], wub_ref)
        dya = (dm * sa).astype(BF16)
        dyb = (dm * sb).astype(BF16)
        dya_ref[...] = dya
        dyb_ref[...] = dyb
        dgate_ref[:, :d] = (dm * ya * sa * (1.0 - sa)).astype(BF16)
        dgate_ref[:, d:] = (dm * yb * sb * (1.0 - sb)).astype(BF16)
        doa_ref[...] = _dot_cols_t(dya, wua_ref).astype(BF16)
        dob_ref[...] = _dot_cols_t(dyb, wub_ref).astype(BF16)

    return _rowcall(name, body, [dx1b, gates, oa, ob], [wo, wua, wub],
                    [(d, BF16), (d, BF16), (2 * d, BF16), (SB_W, BF16), (SW_QW, BF16)], side=side)


def _inproj_bwd(pieces, dx1, x, g, wt, name, side=None):
    d = x.shape[1]
    n = len(pieces)
    offsets = [sum(pc.shape[1] for pc in pieces[:p]) for p in range(n + 1)]

    def body(*refs):
        dx1_ref, x_ref, g_ref, w_ref, dx_ref, dg_ref = refs[n:]
        _acc_init(pl.program_id(0), dg_ref)
        dh = _dot(refs[0][...], w_ref[:offsets[1], :])
        for p in range(1, n):
            dh = dh + _dot(refs[p][...], w_ref[offsets[p]:offsets[p + 1], :])
        dx, dg = _rms_bwd(dh, x_ref[...], g_ref[...])
        dx_ref[...] = dx1_ref[...] + dx
        dg_ref[...] += dg

    return _rowcall(name, body, list(pieces) + [dx1, x], [g, wt], [(d, F32)], [((1, d), F32)], side=side)


def _tile(n, cap):
    assert n % LANES == 0
    return max(t for t in range(LANES, min(n, cap) + 1, LANES) if n % t == 0)


def _mm_tn(a, b, name, nshard=1):
    s, ka = a.shape
    nb = b.shape[1]
    n = nb // nshard
    ta = _tile(ka, 512)
    tb = _tile(n, 1024)
    per = n // tb

    def body(a_ref, b_ref, o_ref):
        o_ref[...] = _dot_tn(a_ref[...].astype(BF16), b_ref[...].astype(BF16))

    return pl.pallas_call(
        body, out_shape=jax.ShapeDtypeStruct((nshard, ka, n), F32), grid=(nb // tb, ka // ta),
        in_specs=[pl.BlockSpec((s, ta), lambda jb, ia: (0, ia)), pl.BlockSpec((s, tb), lambda jb, ia: (0, jb))],
        out_specs=pl.BlockSpec((None, ta, tb), lambda jb, ia: (jb // per, ia, jb % per)),
        compiler_params=_params(2), name=name)(a, b)


def _mm_tn_pieces(pieces, b, name, side=None):
    s, nb = b.shape
    ta = 256
    n_in = len(pieces) + 1
    tiles = [pc.shape[1] // ta for pc in pieces]
    assert all(pc.shape[1] % ta == 0 for pc in pieces)
    starts = [sum(tiles[:p]) for p in range(len(pieces))]
    tb = _tile(nb, 1024)
    grid = (nb // tb, sum(tiles))

    def body(*refs):
        a_refs, b_ref = refs[:n_in - 1], refs[n_in - 1]
        side_in, (o_ref,), side_out, sems = _side_refs(side, refs[n_in:], 1)
        jb, ia = pl.program_id(0), pl.program_id(1)
        if side is not None:
            @pl.when((jb == 0) & (ia == 0))
            def _():
                side.start(side_in, side_out, *sems)
        for p in range(len(pieces)):
            @pl.when((ia >= starts[p]) & (ia < starts[p] + tiles[p]))
            def _(p=p):
                o_ref[...] = _dot_tn(a_refs[p][...], b_ref[...])
        if side is not None:
            @pl.when((jb == grid[0] - 1) & (ia == grid[1] - 1))
            def _():
                side.finish(side_in, side_out, *sems)

    def piece_spec(p):
        return pl.BlockSpec((s, ta), lambda jb, ia: (0, jnp.clip(ia - starts[p], 0, tiles[p] - 1)))

    s_in, s_shape, s_out, s_alias, s_sems = _side_specs(side, n_in, 1)
    outs = pl.pallas_call(
        body, out_shape=[jax.ShapeDtypeStruct((sum(tiles) * ta, nb), F32)] + s_shape, grid=grid,
        in_specs=[piece_spec(p) for p in range(len(pieces))] + [pl.BlockSpec((s, tb), lambda jb, ia: (0, jb))] + s_in,
        out_specs=[pl.BlockSpec((ta, tb), lambda jb, ia: (ia, jb))] + s_out,
        input_output_aliases=s_alias, scratch_shapes=s_sems,
        compiler_params=_params(2), name=name)(*pieces, b, *(side.arrays if side else ()))
    return outs[0] if side is None else (outs[0], outs[1:])


def _place():
    return lax.axis_index("x"), lax.axis_index("y"), lax.axis_index("c")


def _chip_peer(x, y, k):
    return (x ^ (k >> 1), y ^ (k & 1))


def _row_tile(k, cap=544):
    return max(t for t in range(32, min(k, cap) + 1, 32) if k % t == 0)


def _cast_bf16(w, r, name):
    l, k, n = w.shape
    assert l == 2
    tk = _row_tile(k)

    def body(r_ref, w_ref, o0_ref, o1_ref):
        o0_ref[...] = w_ref[0].astype(BF16)
        o1_ref[...] = w_ref[1].astype(BF16)

    out_spec = pl.BlockSpec((None, tk, n), lambda i, r_ref: (r_ref[0], i, 0))
    return pl.pallas_call(
        body, out_shape=[jax.ShapeDtypeStruct((4, k, n), BF16)] * 2,
        grid_spec=pltpu.PrefetchScalarGridSpec(
            num_scalar_prefetch=1, grid=(k // tk,),
            in_specs=[pl.BlockSpec((l, tk, n), lambda i, r_ref: (0, i, 0))],
            out_specs=[out_spec, out_spec]),
        compiler_params=_params(1), name=name)(r, w)


class _Exchange(NamedTuple):
    arrays: tuple
    aliased: tuple
    sems: tuple
    start: Callable
    finish: Callable


def _all_gather(bufs):
    nt = len(bufs)

    def ici(t, ins, outs, send_sems, recv_sems, x, y, c, r, k):
        return pltpu.make_async_remote_copy(
            src_ref=ins[t].at[r, c], dst_ref=outs[t].at[r, c], send_sem=send_sems.at[t, k - 1],
            recv_sem=recv_sems.at[t, k - 1], device_id=(*_chip_peer(x, y, k), c), device_id_type=MESH)

    def d2d(t, outs, send_sems, recv_sems, x, y, c, r, k, half):
        slab = outs[t].at[r ^ k, half]
        return pltpu.make_async_remote_copy(
            src_ref=slab, dst_ref=slab, send_sem=send_sems.at[t, 2 + k], recv_sem=recv_sems.at[t, 2 + k],
            device_id=(x, y, 1 - c), device_id_type=MESH)

    def start(ins, outs, send_sems, recv_sems):
        x, y, c = _place()
        r = 2 * x + y
        for t in range(nt):
            for k in (1, 2, 3):
                ici(t, ins, outs, send_sems, recv_sems, x, y, c, r, k).start()

    def finish(ins, outs, send_sems, recv_sems):
        x, y, c = _place()
        r = 2 * x + y
        for t in range(nt):
            for k in (1, 2, 3):
                slab = outs[t].at[r ^ k, c]
                pltpu.make_async_remote_copy(
                    src_ref=slab, dst_ref=slab, send_sem=send_sems.at[t, k - 1], recv_sem=recv_sems.at[t, k - 1],
                    device_id=(x, y, 1 - c), device_id_type=MESH).wait_recv()
                d2d(t, outs, send_sems, recv_sems, x, y, c, r, k, c).start()
        for t in range(nt):
            for k in (1, 2, 3):
                d2d(t, outs, send_sems, recv_sems, x, y, c, r, k, 1 - c).wait_recv()
        for t in range(nt):
            for k in (1, 2, 3):
                ici(t, ins, outs, send_sems, recv_sems, x, y, c, r, k).wait_send()
                d2d(t, outs, send_sems, recv_sems, x, y, c, r, k, c).wait_send()

    return _Exchange(tuple(bufs), tuple(range(nt)), (nt, 6), start, finish)


def _run_exchange(name, ex):
    n_in, n_out = len(ex.arrays), len(ex.aliased)

    def body(*refs):
        ins, outs = refs[:n_in], refs[n_in:n_in + n_out]
        send_sems, recv_sems = refs[n_in + n_out:]
        ex.start(ins, outs, send_sems, recv_sems)
        ex.finish(ins, outs, send_sems, recv_sems)

    any_spec = pl.BlockSpec(memory_space=pl.ANY)
    return pl.pallas_call(
        body, out_shape=[jax.ShapeDtypeStruct(ex.arrays[a].shape, ex.arrays[a].dtype) for a in ex.aliased],
        in_specs=[any_spec] * n_in, out_specs=[any_spec] * n_out,
        input_output_aliases={a: o for o, a in enumerate(ex.aliased)},
        scratch_shapes=[pltpu.SemaphoreType.DMA(ex.sems), pltpu.SemaphoreType.DMA(ex.sems)],
        name=name)(*ex.arrays)


def _rs_to_sibling(grads):
    nt = len(grads)
    landing = [lax.empty((4,) + g.shape[2:], F32) for g in grads]

    def copies(ins, outs, send_sems, recv_sems):
        x, y, c = _place()
        return [pltpu.make_async_remote_copy(
            src_ref=ins[t].at[:, 1 - c], dst_ref=outs[t], send_sem=send_sems.at[t], recv_sem=recv_sems.at[t],
            device_id=(x, y, 1 - c), device_id_type=MESH) for t in range(nt)]

    def start(ins, outs, send_sems, recv_sems):
        for cp in copies(ins, outs, send_sems, recv_sems):
            cp.start()

    def finish(ins, outs, send_sems, recv_sems):
        for cp in copies(ins, outs, send_sems, recv_sems):
            cp.wait()

    return _Exchange(tuple(grads) + tuple(landing), tuple(range(nt, 2 * nt)), (nt,), start, finish)


def _add_half(g, recv, cr, name):
    _, _, k2, n = g.shape
    tk = _row_tile(k2)

    def body(cr_ref, g_ref, r_ref, sums_ref, mine_ref):
        val = (g_ref[...] + r_ref[...]).astype(BF16)
        sums_ref[...] = val

        @pl.when(pl.program_id(1) == cr_ref[1])
        def _():
            mine_ref[...] = val

    return pl.pallas_call(
        body, out_shape=[jax.ShapeDtypeStruct((4, k2, n), BF16)] * 2,
        grid_spec=pltpu.PrefetchScalarGridSpec(
            num_scalar_prefetch=1, grid=(k2 // tk, 4),
            in_specs=[pl.BlockSpec((None, None, tk, n), lambda i, q, cr_ref: (q, cr_ref[0], i, 0)),
                      pl.BlockSpec((None, tk, n), lambda i, q, cr_ref: (q, i, 0))],
            out_specs=[pl.BlockSpec((None, tk, n), lambda i, q, cr_ref: (q, i, 0)),
                       pl.BlockSpec((None, tk, n), lambda i, q, cr_ref: (cr_ref[1], i, 0))]),
        compiler_params=_params(2), name=name)(cr, g, recv)


def _rs_to_chips(sums, parts):
    nt = len(sums)

    def copies(ins, outs, send_sems, recv_sems):
        x, y, c = _place()
        r = 2 * x + y
        return [pltpu.make_async_remote_copy(
            src_ref=ins[t].at[r ^ k], dst_ref=outs[t].at[r], send_sem=send_sems.at[t, k - 1],
            recv_sem=recv_sems.at[t, k - 1], device_id=(*_chip_peer(x, y, k), c), device_id_type=MESH)
            for t in range(nt) for k in (1, 2, 3)]

    def start(ins, outs, send_sems, recv_sems):
        for cp in copies(ins, outs, send_sems, recv_sems):
            cp.start()

    def finish(ins, outs, send_sems, recv_sems):
        for cp in copies(ins, outs, send_sems, recv_sems):
            cp.wait()

    return _Exchange(tuple(sums) + tuple(parts), tuple(range(nt, 2 * nt)), (nt, 3), start, finish)


def _sum4(parts, cr, name):
    _, k2, n = parts.shape
    tk = _row_tile(k2)

    def body(cr_ref, p_ref, o_ref):
        p = p_ref[...].astype(F32)
        o_ref[...] = ((p[0] + p[1]) + p[2]) + p[3]

    return pl.pallas_call(
        body, out_shape=jax.ShapeDtypeStruct((2, k2, n), F32),
        grid_spec=pltpu.PrefetchScalarGridSpec(
            num_scalar_prefetch=1, grid=(k2 // tk,),
            in_specs=[pl.BlockSpec((4, tk, n), lambda i, cr_ref: (0, i, 0))],
            out_specs=pl.BlockSpec((None, tk, n), lambda i, cr_ref: (cr_ref[0], i, 0))),
        compiler_params=_params(1), name=name)(cr, parts)


def _exchange_halves(both):
    nt = len(both)

    def copies(ins, outs, send_sems, recv_sems):
        x, y, c = _place()
        return [pltpu.make_async_remote_copy(
            src_ref=ins[t].at[c], dst_ref=outs[t].at[c], send_sem=send_sems.at[t], recv_sem=recv_sems.at[t],
            device_id=(x, y, 1 - c), device_id_type=MESH) for t in range(nt)]

    def start(ins, outs, send_sems, recv_sems):
        for cp in copies(ins, outs, send_sems, recv_sems):
            cp.start()

    def finish(ins, outs, send_sems, recv_sems):
        for cp in copies(ins, outs, send_sems, recv_sems):
            cp.wait()

    return _Exchange(tuple(both), tuple(range(nt)), (nt,), start, finish)


def _adamw_math(w, g, m, v):
    m = ADAM_B1 * m + (1.0 - ADAM_B1) * g
    v = ADAM_B2 * v + (1.0 - ADAM_B2) * jnp.square(g)
    m_hat = m / (1.0 - ADAM_B1 ** ADAM_STEP)
    v_hat = v / (1.0 - ADAM_B2 ** ADAM_STEP)
    delta = -ADAM_LR * (m_hat / (jnp.sqrt(v_hat) + ADAM_EPS) + ADAM_WD * w)
    return delta, m, v


def _adamw(w, m, v, g0, g1, name):
    _, k, n = w.shape
    tk = _row_tile(k)
    nk = k // tk

    def body(w_ref, m_ref, v_ref, g0_ref, g1_ref, grad_ref, delta_ref, nm_ref, nv_ref):
        g = jnp.where(pl.program_id(0) == 0, g0_ref[...], g1_ref[...])
        delta, nm, nv = _adamw_math(w_ref[...], g, m_ref[...], v_ref[...])
        grad_ref[...] = g
        delta_ref[...] = delta
        nm_ref[...] = nm
        nv_ref[...] = nv

    lay = pl.BlockSpec((None, tk, n), lambda a, i: (a, i, 0))
    g0_spec = pl.BlockSpec((tk, n), lambda a, i: (jnp.where(a == 0, i, nk - 1), 0))
    g1_spec = pl.BlockSpec((tk, n), lambda a, i: (jnp.where(a == 1, i, 0), 0))
    return pl.pallas_call(
        body, out_shape=[jax.ShapeDtypeStruct(w.shape, F32)] * 4, grid=(2, nk),
        in_specs=[lay, lay, lay, g0_spec, g1_spec], out_specs=[lay] * 4,
        compiler_params=_params(2), name=name)(w, m, v, g0, g1)


def _small_allreduce_adamw(gpart, w, m, v):
    shape = gpart.shape

    def body(g_ref, w_ref, m_ref, v_ref, gsum_ref, delta_ref, nm_ref, nv_ref, recv_ref, send_sems, recv_sems):
        x, y, c = _place()
        me = 4 * x + 2 * y + c
        recv_ref[me] = g_ref[...]
        cps = []
        for k in range(1, 8):
            peer = (x ^ (k >> 2), y ^ ((k >> 1) & 1), c ^ (k & 1))
            cp = pltpu.make_async_remote_copy(
                src_ref=g_ref, dst_ref=recv_ref.at[me], send_sem=send_sems.at[k - 1], recv_sem=recv_sems.at[k - 1],
                device_id=peer, device_id_type=MESH)
            cp.start()
            cps.append(cp)
        for cp in cps:
            cp.wait()
        g = recv_ref[0]
        for dev in range(1, 8):
            g = g + recv_ref[dev]
        delta, nm, nv = _adamw_math(w_ref[...], g, m_ref[...], v_ref[...])
        gsum_ref[...] = g
        delta_ref[...] = delta
        nm_ref[...] = nm
        nv_ref[...] = nv

    vm = pl.BlockSpec(memory_space=pltpu.VMEM)
    return pl.pallas_call(
        body, out_shape=[jax.ShapeDtypeStruct(shape, F32)] * 4, in_specs=[vm] * 4, out_specs=[vm] * 4,
        scratch_shapes=[pltpu.VMEM((8,) + shape, F32), pltpu.SemaphoreType.DMA((7,)), pltpu.SemaphoreType.DMA((7,))],
        name="small_allreduce_adamw")(gpart, w, m, v)


BIG = ("w_in", "w_up_a", "w_up_b", "w_o", "w_ff1", "w_ff2", "w_pe", "w_pg")
COL_SHARDED = ("w_in", "w_up_a", "w_up_b", "w_ff1", "w_pe")
ROW_SHARDED = ("w_o", "w_ff2", "w_pg")
SMALL_ROWS = 16


def _pack_small(g_mix, g_mlp, g_pe, g_final, sinks, rel_bias, loss=None):
    d = g_final.shape[0]
    row = lambda v: jnp.pad(v.reshape(1, -1), ((0, 0), (0, d - v.size)))
    rows = [g_mix, g_mlp, g_pe, g_final.reshape(1, d),
            jnp.zeros((1, d), F32) if loss is None else row(loss), row(sinks), row(rel_bias)]
    out = jnp.concatenate(rows, axis=0)
    return jnp.pad(out, ((0, SMALL_ROWS - out.shape[0]), (0, 0)))


def _unpack_small(a, sinks_shape, rel_shape):
    return (a[0:2], a[2:4], a[4:6], a[6], a[8, :sinks_shape[0] * sinks_shape[1]].reshape(sinks_shape),
            a[9, :rel_shape[0] * rel_shape[1]].reshape(rel_shape))


def kernel(x, p, w_in, w_up_a, w_up_b, w_o, w_ff1, w_ff2, w_pe, w_pg, g_mix, g_mlp, g_pe, g_final, sinks, rel_bias, loss_target, m_w_in, m_w_up_a, m_w_up_b, m_w_o, m_w_ff1, m_w_ff2, m_w_pe, m_w_pg, m_g_mix, m_g_mlp, m_g_pe, m_g_final, m_sinks, m_rel_bias, v_w_in, v_w_up_a, v_w_up_b, v_w_o, v_w_ff1, v_w_ff2, v_w_pe, v_w_pg, v_g_mix, v_g_mlp, v_g_pe, v_g_final, v_sinks, v_rel_bias):
    depth = w_in.shape[0]
    assert depth == 2
    x0 = x[0]
    target = loss_target[0]
    d = x0.shape[1]
    wl = dict(w_in=w_in, w_up_a=w_up_a, w_up_b=w_up_b, w_o=w_o, w_ff1=w_ff1, w_ff2=w_ff2, w_pe=w_pe, w_pg=w_pg)
    ml = dict(w_in=m_w_in, w_up_a=m_w_up_a, w_up_b=m_w_up_b, w_o=m_w_o, w_ff1=m_w_ff1, w_ff2=m_w_ff2, w_pe=m_w_pe, w_pg=m_w_pg)
    vl = dict(w_in=v_w_in, w_up_a=v_w_up_a, w_up_b=v_w_up_b, w_o=v_w_o, w_ff1=v_w_ff1, w_ff2=v_w_ff2, w_pe=v_w_pe, w_pg=v_w_pg)
    c_idx = lax.axis_index("c").astype(jnp.int32)
    r_idx = (2 * lax.axis_index("x") + lax.axis_index("y")).astype(jnp.int32)
    cr = jnp.stack([c_idx, r_idx])

    wl["w_in"], ml["w_in"], vl["w_in"] = (jnp.swapaxes(a, 1, 2) for a in (w_in, m_w_in, v_w_in))

    bufs = {}
    for n in BIG:
        k, nn = wl[n].shape[1:]
        for l, b in enumerate(_cast_bf16(wl[n], r_idx.reshape(1), "cast_" + n)):
            bufs[n, l] = b.reshape(4, 2, k // 2, nn)

    def gather(keys, run):
        for key, b in zip(keys, run(_all_gather([bufs[key] for key in keys]))):
            bufs[key] = b

    def gathered(n, l):
        _, _, k2, nn = bufs[n, l].shape
        if n in ROW_SHARDED or n == "w_in":
            return bufs[n, l].reshape(8 * k2, nn)
        return bufs[n, l].reshape(4, 2 * k2, nn)

    gather([("w_in", 0)], lambda ex: _run_exchange("all_gather_first", ex))

    full = {n: [None] * depth for n in BIG}
    saved = []
    xi = x0
    for i in range(depth):
        st = dict(x0=xi)
        gm = g_mix[i].reshape(1, d)
        full["w_in"][i] = gathered("w_in", i)
        st["h1"], st["qkv"], st["gates"] = _inproj_fwd(xi, gm, full["w_in"][i], f"inproj_fwd_{i}")

        def attend(ex):
            (st["oa"], st["lt"], st["nb"], st["a_wide"], st["sg_wide"]), filled = _sb_fwd(st["qkv"], f"sb_fwd_{i}", ex)
            return filled

        gather([(n, i) for n in BIG if n != "w_in"], attend)
        for n in BIG:
            if n != "w_in":
                full[n][i] = gathered(n, i)
        def window(ex):
            (st["ob"], st["probs"]), filled = _swa_fwd(st["qkv"], sinks[i], rel_bias, f"swa_fwd_{i}", ex)
            return filled

        if i == 0:
            gather([("w_in", 1)], window)
        else:
            window(None)
        st["m"], st["x1"] = _mixer_fwd(st["oa"], st["ob"], st["gates"], xi, full["w_up_a"][i], full["w_up_b"][i],
                                       full["w_o"][i], f"mixer_fwd_{i}")
        st["h2"], st["u"], st["a"] = _ff1_fwd(st["x1"], g_mlp[i].reshape(1, d), full["w_ff1"][i], f"ff1_fwd_{i}")
        st["x2"] = _ff2_fwd(st["a"], st["x1"], full["w_ff2"][i], f"ff2_fwd_{i}")
        st["pb"], st["h3"], st["pe"], st["gt"], xi = _ple_fwd(p[i, 0], st["x2"], g_pe[i].reshape(1, d),
                                                            full["w_pe"][i], full["w_pg"][i], f"ple_fwd_{i}")
        saved.append(st)

    dx, dg_final, loss_part = _loss_bwd(xi, target, g_final.reshape(1, d), "loss_bwd")
    gw = {n: [None] * depth for n in BIG}
    reduced = {}

    chip_sums = {}

    def to_sibling(keys, run):
        tensors = []
        for n, l in keys:
            g = gw[n][l]
            if n in ROW_SHARDED:
                ka, nb = g.shape[1:]
                g = g.reshape(4, ka // 4, nb)
            _, k, nn = g.shape
            tensors.append(g.reshape(4, 2, k // 2, nn))
        for (n, l), g, r in zip(keys, tensors, run(_rs_to_sibling(tensors))):
            chip_sums[n, l] = _add_half(g, r, cr, f"add_half_{n}_{l}")

    def to_chips(keys):
        return _rs_to_chips([chip_sums[k][0] for k in keys], [chip_sums[k][1] for k in keys])

    halves = {}

    def sum_chips(keys, parts):
        for (n, l), pc in zip(keys, parts):
            halves[n, l] = _sum4(pc, cr, f"sum4_{n}_{l}")

    def swap_halves(keys):
        def store(filled):
            for key, both in zip(keys, filled):
                reduced[key] = both
        return _exchange_halves([halves[k] for k in keys]), store

    dg_mix, dg_mlp, dg_pe, dsinks = [None] * depth, [None] * depth, [None] * depth, [None] * depth
    drel = jnp.zeros((SW_HEADS, LANES), F32)
    for i in reversed(range(depth)):
        st = saved[i]
        dpe, dgt, dx2, dg_pe[i] = _ple_bwd(dx, st["pe"], st["gt"], st["x2"], g_pe[i].reshape(1, d),
                                           full["w_pg"][i], f"ple_bwd_{i}")
        gw["w_pe"][i] = _mm_tn(st["pb"], dpe, f"dw_pe_{i}", 4)
        gw["w_pg"][i] = _mm_tn(st["h3"], dgt, f"dw_pg_{i}")
        du, dx2b = _ff2_bwd(dx2, st["u"], full["w_ff2"][i], f"ff2_bwd_{i}")
        gw["w_ff2"][i] = _mm_tn(st["a"], dx2b, f"dw_ff2_{i}")
        gw["w_ff1"][i] = _mm_tn(st["h2"], du, f"dw_ff1_{i}", 4)
        if i == 0:
            (dx1, dx1b, dg_mlp[i]), parts = _ff1_bwd(du, dx2, st["x1"], g_mlp[i].reshape(1, d), full["w_ff1"][i],
                                                    f"ff1_bwd_{i}", to_chips([("w_in", 1)]))
            sum_chips([("w_in", 1)], parts)
        else:
            dx1, dx1b, dg_mlp[i] = _ff1_bwd(du, dx2, st["x1"], g_mlp[i].reshape(1, d), full["w_ff1"][i],
                                            f"ff1_bwd_{i}")
        gw["w_o"][i] = _mm_tn(st["m"], dx1b, f"dw_o_{i}")
        early = [(n, i) for n in ("w_pe", "w_pg", "w_ff2", "w_ff1", "w_o")]

        def mixer(ex):
            (dya, dyb, dgates, doa, dob), landed = _mixer_bwd(
                dx1b, st["gates"], st["oa"], st["ob"], full["w_o"][i], full["w_up_a"][i], full["w_up_b"][i],
                f"mixer_bwd_{i}", ex)
            st.update(dya=dya, dyb=dyb, dgates=dgates, doa=doa, dob=dob)
            return landed

        to_sibling(early, mixer)
        dgates = st["dgates"]
        gw["w_up_a"][i] = _mm_tn(st["oa"], st["dya"], f"dw_up_a_{i}", 4)
        gw["w_up_b"][i] = _mm_tn(st["ob"], st["dyb"], f"dw_up_b_{i}", 4)
        late = [("w_up_a", i), ("w_up_b", i)]
        to_sibling(late, lambda ex: _run_exchange(f"rs_to_sibling_{i}", ex))
        keys = early + late
        with_sb = [(n, i) for n in ("w_ff1", "w_o", "w_pg", "w_pe")]
        with_swa = [(n, i) for n in ("w_ff2", "w_up_a", "w_up_b")]
        dqa, dka, dva, parts = _sb_bwd(st["qkv"], st["lt"], st["nb"], st["a_wide"], st["sg_wide"], st["doa"],
                                       f"sb_bwd_{i}", to_chips(with_sb))
        sum_chips(with_sb, parts)
        (dqb, dkvb, dsk, drl), parts = _swa_bwd(st["qkv"], st["ob"], st["dob"], st["probs"], f"swa_bwd_{i}",
                                                to_chips(with_swa))
        sum_chips(with_swa, parts)
        dsinks[i] = dsk[:, 0]
        drel = drel + drl
        dproj = [dqa, dka, dva, dqb, dkvb, dgates]
        swap, store = swap_halves(keys + ([("w_in", 1)] if i == 0 else []))
        dw_in_t, filled = _mm_tn_pieces(dproj, st["h1"], f"dw_in_{i}", swap)
        store(filled)
        gw["w_in"][i] = dw_in_t.reshape(4, dw_in_t.shape[0] // 4, d)
        if i == 1:
            def inproj(ex):
                (dx, dg_mix[i]), landed = _inproj_bwd(dproj, dx1, st["x0"], g_mix[i].reshape(1, d),
                                                      full["w_in"][i], f"inproj_bwd_{i}", ex)
                st["dx"] = dx
                return landed

            to_sibling([("w_in", 1)], inproj)
            dx = st["dx"]
        else:
            keys = [("w_in", 0)]
            to_sibling(keys, lambda ex: _run_exchange("rs_to_sibling_last", ex))
            (dx, dg_mix[i]), parts = _inproj_bwd(dproj, dx1, st["x0"], g_mix[i].reshape(1, d),
                                                 full["w_in"][i], f"inproj_bwd_{i}", to_chips(keys))
            sum_chips(keys, parts)
            swap, store = swap_halves(keys)
            store(_run_exchange("exchange_halves_last", swap))
    grad_x = dx[None]

    outs = {}
    for n in BIG:
        g0, g1 = (reduced[n, l].reshape(wl[n].shape[1:]) for l in range(depth))
        outs[n] = _adamw(wl[n], ml[n], vl[n], g0, g1, "adamw_" + n)
    outs["w_in"] = [jnp.swapaxes(a, 1, 2) for a in outs["w_in"]]

    drel_bias = drel[:, :N_BUCKETS].T
    gsmall = _pack_small(jnp.concatenate(dg_mix, 0), jnp.concatenate(dg_mlp, 0), jnp.concatenate(dg_pe, 0),
                         dg_final[0], jnp.stack(dsinks), drel_bias, loss_part[0, :1])
    wsmall = _pack_small(g_mix, g_mlp, g_pe, g_final, sinks, rel_bias)
    msmall = _pack_small(m_g_mix, m_g_mlp, m_g_pe, m_g_final, m_sinks, m_rel_bias)
    vsmall = _pack_small(v_g_mix, v_g_mlp, v_g_pe, v_g_final, v_sinks, v_rel_bias)
    small = _small_allreduce_adamw(gsmall, wsmall, msmall, vsmall)
    loss = small[0][7, 0]
    small = [_unpack_small(a, sinks.shape, rel_bias.shape) for a in small]

    result = [loss, grad_x]
    for kind in range(4):
        result += [outs[n][kind] for n in BIG]
        result += list(small[kind])
    return tuple(result)
```

```python
import functools
import math
from typing import Callable, NamedTuple

import numpy as np
import jax
import jax.numpy as jnp
from jax import lax
from jax.experimental import pallas as pl
from jax.experimental.pallas import tpu as pltpu

F32 = jnp.float32
BF16 = jnp.bfloat16
MESH = pl.DeviceIdType.MESH

HEAD_DIM = 64
SB_HEADS = 8
SW_HEADS = 8
SW_KV_HEADS = 2
WINDOW = 128
N_BUCKETS = 32
MAX_DISTANCE = 128
EPS = 1e-6
SB_W = SB_HEADS * HEAD_DIM
SW_QW = SW_HEADS * HEAD_DIM
SW_KVW = SW_KV_HEADS * HEAD_DIM
QKV_W = 3 * SB_W + SW_QW + 2 * SW_KVW
SCALE = HEAD_DIM ** -0.5
assert SCALE == 0.125
LANES = 128
TQ = 128
BK = 128
NEG = -1e30
SB_EXHAUSTED = -106.0

ADAM_LR = 0.001
ADAM_B1 = 0.9
ADAM_B2 = 0.999
ADAM_EPS = 1e-08
ADAM_WD = 0.01
ADAM_STEP = 10

VMEM_LIMIT = 56 * 1024 * 1024


def _dot(a, b):
    return jnp.dot(a, b, preferred_element_type=F32)


def _dot_nt(a, b):
    return lax.dot_general(a, b, (((1,), (1,)), ((), ())), preferred_element_type=F32)


def _dot_tn(a, b):
    return lax.dot_general(a, b, (((0,), (0,)), ((), ())), preferred_element_type=F32)


def _sum_all(x):
    return jnp.sum(jnp.sum(x, axis=1, keepdims=True), axis=0, keepdims=True)


def _sigmoid(x):
    return 1.0 / (1.0 + jnp.exp(-x))


def _rms(x, g):
    r = lax.rsqrt(jnp.mean(x * x, axis=-1, keepdims=True) + EPS)
    return (x * r) * g


def _rms_bwd(dy, x, g):
    r = lax.rsqrt(jnp.mean(x * x, axis=-1, keepdims=True) + EPS)
    n = x * r
    dg = jnp.sum(dy * n, axis=0, keepdims=True)
    dn = dy * g
    dx = r * (dn - n * jnp.mean(dn * n, axis=-1, keepdims=True))
    return dx, dg


def _params(n_axes):
    return pltpu.CompilerParams(dimension_semantics=("arbitrary",) * n_axes, vmem_limit_bytes=VMEM_LIMIT)


def _rowcall(name, body, row_ins, const_ins, row_outs, acc_outs=(), tm=512, side=None):
    s = row_ins[0].shape[0]
    assert s % tm == 0
    in_specs = [pl.BlockSpec((tm, a.shape[1]), lambda i: (i, 0)) for a in row_ins]
    in_specs += [pl.BlockSpec(a.shape, functools.partial(lambda i, nd: (0,) * nd, nd=a.ndim)) for a in const_ins]
    out_shape = [jax.ShapeDtypeStruct((s, c), dt) for c, dt in row_outs]
    out_specs = [pl.BlockSpec((tm, c), lambda i: (i, 0)) for c, _ in row_outs]
    out_shape += [jax.ShapeDtypeStruct(sh, dt) for sh, dt in acc_outs]
    out_specs += [pl.BlockSpec(sh, functools.partial(lambda i, nd: (0,) * nd, nd=len(sh))) for sh, _ in acc_outs]
    if side is None:
        return pl.pallas_call(body, out_shape=out_shape, grid=(s // tm,), in_specs=in_specs, out_specs=out_specs,
                              compiler_params=_params(1), name=name)(*row_ins, *const_ins)
    n_in, n_out = len(in_specs), len(out_specs)

    def with_side(*refs):
        side_in, outs, side_out, sems = _side_refs(side, refs[n_in:], n_out)

        @pl.when(pl.program_id(0) == 0)
        def _():
            side.start(side_in, side_out, *sems)

        body(*refs[:n_in], *outs)

        @pl.when(pl.program_id(0) == s // tm - 1)
        def _():
            side.finish(side_in, side_out, *sems)

    s_in, s_shape, s_out, s_alias, s_sems = _side_specs(side, n_in, n_out)
    outs = pl.pallas_call(with_side, out_shape=out_shape + s_shape, grid=(s // tm,), in_specs=in_specs + s_in,
                          out_specs=out_specs + s_out, input_output_aliases=s_alias, scratch_shapes=s_sems,
                          compiler_params=_params(1), name=name)(*row_ins, *const_ins, *side.arrays)
    return outs[:n_out], outs[n_out:]


def _dot_cols(a, w_ref):
    return jnp.concatenate([_dot(a, w_ref[r]) for r in range(w_ref.shape[0])], axis=1)


def _dot_cols_t(a, w_ref):
    n = w_ref.shape[2]
    out = _dot_nt(a[:, :n], w_ref[0])
    for r in range(1, w_ref.shape[0]):
        out = out + _dot_nt(a[:, r * n:(r + 1) * n], w_ref[r])
    return out


def _inproj_fwd(x, g, wt, name):
    d = x.shape[1]

    def body(x_ref, g_ref, w_ref, h_ref, qkv_ref, gate_ref):
        hb = _rms(x_ref[...], g_ref[...]).astype(BF16)
        h_ref[...] = hb
        qkv_ref[...] = _dot_nt(hb, w_ref[:QKV_W, :]).astype(BF16)
        gate_ref[...] = _dot_nt(hb, w_ref[QKV_W:, :])

    return _rowcall(name, body, [x], [g, wt], [(d, BF16), (QKV_W, BF16), (2 * d, F32)])


def _mixer_fwd(oa, ob, gates, x, wua, wub, wo, name):
    d = x.shape[1]

    def body(oa_ref, ob_ref, gate_ref, x_ref, wua_ref, wub_ref, wo_ref, m_ref, x1_ref):
        ya = _dot_cols(oa_ref[...], wua_ref)
        yb = _dot_cols(ob_ref[...], wub_ref)
        m = _sigmoid(gate_ref[:, :d]) * ya + _sigmoid(gate_ref[:, d:]) * yb
        mb = m.astype(BF16)
        m_ref[...] = mb
        x1_ref[...] = x_ref[...] + _dot(mb, wo_ref[...])

    return _rowcall(name, body, [oa, ob, gates, x], [wua, wub, wo], [(d, BF16), (d, F32)])


def _ff1_fwd(x1, g, w1, name):
    _, d, nq = w1.shape
    dff = 4 * nq

    def body(x_ref, g_ref, w_ref, h_ref, r_ref, a_ref):
        hb = _rms(x_ref[...], g_ref[...]).astype(BF16)
        h_ref[...] = hb
        r = jnp.maximum(_dot_cols(hb, w_ref), 0.0)
        r_ref[...] = r.astype(BF16)
        a_ref[...] = jnp.square(r).astype(BF16)

    return _rowcall(name, body, [x1], [g, w1], [(d, BF16), (dff, BF16), (dff, BF16)])


def _ff2_fwd(a, x1, w2, name):
    d = x1.shape[1]

    def body(a_ref, x_ref, w_ref, o_ref):
        o_ref[...] = x_ref[...] + _dot(a_ref[...], w_ref[...])

    return _rowcall(name, body, [a, x1], [w2], [(d, F32)])[0]


def _ple_fwd(p, x2, g, wpe, wpg, name):
    d = x2.shape[1]

    def body(p_ref, x_ref, g_ref, wpe_ref, wpg_ref, pb_ref, h_ref, pe_ref, gt_ref, x3_ref):
        pb = p_ref[...].astype(BF16)
        pb_ref[...] = pb
        pe = _dot_cols(pb, wpe_ref)
        x = x_ref[...]
        hb = _rms(x, g_ref[...]).astype(BF16)
        h_ref[...] = hb
        gt = _dot(hb, wpg_ref[...])
        pe_ref[...] = pe
        gt_ref[...] = gt
        x3_ref[...] = x + pe * _sigmoid(gt)

    return _rowcall(name, body, [p, x2], [g, wpe, wpg],
                    [(p.shape[1], BF16), (d, BF16), (d, F32), (d, F32), (d, F32)])


def _pair_stack(t, lane):
    zero = jnp.zeros_like(t)
    return jnp.concatenate([jnp.where(lane < HEAD_DIM, t, zero), jnp.where(lane >= HEAD_DIM, t, zero)], axis=0)


def _sb_rel():
    row = lax.broadcasted_iota(jnp.int32, (2 * TQ, BK), 0)
    row = jnp.where(row >= TQ, row - TQ, row)
    col = lax.broadcasted_iota(jnp.int32, (2 * TQ, BK), 1)
    return col - row


def _split_dot(x, m01, two_pass=True):
    hi = x.astype(BF16)
    if not two_pass:
        return _dot(hi, m01)
    lo = (x - hi.astype(F32)).astype(BF16)
    return _dot(hi, m01) + _dot(lo, m01)


def _sb_scores(qs, k, mask):
    z = _dot_nt(qs, k)
    lb = jnp.minimum(z, 0.0) - jnp.log(1.0 + jnp.exp(-jnp.abs(z)))
    lm = lb - z
    return lb, lm if mask is None else jnp.where(mask, lm, 0.0)


SB_STRAIGHT = 3
SB_WIDE = SB_STRAIGHT * BK
SB_QB = 2
SWA_QB = 4


def _sb_wide_consts():
    j = np.arange(BK)[:, None]
    s = np.arange(BK)[None, :]
    ones = np.ones((BK, BK), np.float32)
    as_bf16 = lambda m: jnp.asarray(np.concatenate([m, ones], axis=1).astype(np.float32), dtype=BF16)
    return as_bf16(j > s), as_bf16(j <= s), as_bf16(j < s)


def _wide_sums(x, m01, suffix, two_pass=True):
    parts = [_split_dot(x[:, b * BK:(b + 1) * BK], m01, two_pass) for b in range(SB_STRAIGHT)]
    order = range(SB_STRAIGHT - 1, -1, -1) if suffix else range(SB_STRAIGHT)
    out = [None] * SB_STRAIGHT
    carry = None
    for b in order:
        out[b] = parts[b][:, :BK] if carry is None else parts[b][:, :BK] + carry
        carry = parts[b][:, BK:] if carry is None else carry + parts[b][:, BK:]
    return jnp.concatenate(out, axis=1), carry


def _side_refs(ex, rest, n_out):
    n_in = len(ex.arrays) if ex else 0
    n_alias = len(ex.aliased) if ex else 0
    ins, rest = rest[:n_in], rest[n_in:]
    outs, rest = rest[:n_out], rest[n_out:]
    return ins, outs, rest[:n_alias], rest[n_alias:]


def _side_specs(ex, n_in, n_out):
    if ex is None:
        return [], [], [], {}, []
    any_spec = pl.BlockSpec(memory_space=pl.ANY)
    return ([any_spec] * len(ex.arrays),
            [jax.ShapeDtypeStruct(ex.arrays[a].shape, ex.arrays[a].dtype) for a in ex.aliased],
            [any_spec] * len(ex.aliased), {n_in + a: n_out + o for o, a in enumerate(ex.aliased)},
            [pltpu.SemaphoreType.DMA(ex.sems), pltpu.SemaphoreType.DMA(ex.sems)])


def _sb_fwd(qkv, name, side=None):
    s = qkv.shape[0]
    nq = s // TQ
    npair = SB_W // LANES
    sufw = _sb_wide_consts()[0]

    def body(q_ref, k_ref, v_ref, sufw_ref, *rest):
        side_in, (o_ref, lt_ref, nb_ref, a_ref, sg_ref), side_out, scratch = _side_refs(side, rest, 5)
        cf_ref, acc_ref = scratch[:2]
        step_id = pl.program_id(1)
        if side is not None:
            @pl.when((pl.program_id(0) == 0) & (step_id == 0))
            def _():
                side.start(side_in, side_out, *scratch[2:])
        lane = lax.broadcasted_iota(jnp.int32, (TQ, LANES), 1)
        rel = _sb_rel()
        blocks = [step_id * SB_QB + b for b in range(SB_QB)]
        qs = [_pair_stack(q_ref[b * TQ:(b + 1) * TQ, :] * SCALE, lane) for b in range(SB_QB)]

        straight = blocks[0] >= SB_STRAIGHT - 1

        @pl.when(straight)
        def _():
            for b, i in enumerate(blocks):
                w0 = pl.multiple_of((i - (SB_STRAIGHT - 1)) * BK, BK)
                kw = k_ref[pl.ds(w0, SB_WIDE), :]
                lb, lm = _sb_scores(qs[b], kw, None)
                own = rel < 0
                past = SB_WIDE - BK
                lm = jnp.concatenate([lm[:, :past], jnp.where(own, lm[:, past:], 0.0)], axis=1)
                after, total = _wide_sums(lm, sufw_ref[...], True)
                on_past_keys = lambda t: jnp.concatenate([t[:, :past], jnp.where(own, t[:, past:], 0.0)], axis=1)
                a = on_past_keys(jnp.exp(lb + after)).astype(BF16)
                acc_ref[b] = _dot(a, v_ref[pl.ds(w0, SB_WIDE), :])
                cf_ref[b] = total
                a_ref[b] = a
                sg_ref[b] = on_past_keys(jnp.exp(lb)).astype(BF16)

        @pl.when(jnp.logical_not(straight))
        def _():
            cf_ref[...] = jnp.zeros_like(cf_ref)
            acc_ref[...] = jnp.zeros_like(acc_ref)

        for b, i in enumerate(blocks):
            q0 = i * TQ

            def more(c, i=i):
                return (c[0] <= i) & (c[1] > SB_EXHAUSTED)

            def step(c, b=b, i=i, q0=q0):
                k0 = pl.multiple_of((i - c[0]) * BK, BK)
                k = k_ref[pl.ds(k0, BK), :]
                v = v_ref[pl.ds(k0, BK), :]
                mask = rel < (q0 - k0)
                lb, lm = _sb_scores(qs[b], k, mask)
                cs = _split_dot(lm, sufw_ref[...])
                a = jnp.where(mask, jnp.exp(lb + cs[:, :BK] + cf_ref[b]), 0.0)
                acc_ref[b] += _dot(a.astype(BF16), v)
                cf = cf_ref[b] + cs[:, BK:]
                cf_ref[b] = cf
                return c[0] + 1, jnp.max(cf)

            n_blocks, _ = lax.while_loop(
                more, step, (jnp.where(straight, SB_STRAIGHT, 0).astype(jnp.int32), jnp.max(cf_ref[b])))
            o_ref[b * TQ:(b + 1) * TQ, :] = jnp.where(lane < HEAD_DIM, acc_ref[b, :TQ, :],
                                                     acc_ref[b, TQ:, :]).astype(BF16)
            lt_ref[b] = cf_ref[b]
            nb_ref[b] = jnp.full(nb_ref.shape[1:], n_blocks, F32)
        if side is not None:
            @pl.when((pl.program_id(0) == npair - 1) & (step_id == nq // SB_QB - 1))
            def _():
                side.finish(side_in, side_out, *scratch[2:])

    s_in, s_shape, s_out, s_alias, s_sems = _side_specs(side, 4, 5)
    wide = jax.ShapeDtypeStruct((npair, nq, 2 * TQ, SB_WIDE), BF16)
    wide_spec = pl.BlockSpec((None, SB_QB, 2 * TQ, SB_WIDE), lambda j, i: (j, i, 0, 0))
    outs = pl.pallas_call(
        body,
        out_shape=[jax.ShapeDtypeStruct((s, SB_W), BF16), jax.ShapeDtypeStruct((npair, nq, 2 * TQ, BK), F32),
                   jax.ShapeDtypeStruct((npair, nq, 8, LANES), F32), wide, wide] + s_shape,
        grid=(npair, nq // SB_QB),
        in_specs=[pl.BlockSpec((SB_QB * TQ, LANES), lambda j, i: (i, j)),
                  pl.BlockSpec((s, LANES), lambda j, i: (0, npair + j)),
                  pl.BlockSpec((s, LANES), lambda j, i: (0, 2 * npair + j)),
                  pl.BlockSpec(sufw.shape, lambda j, i: (0, 0))] + s_in,
        out_specs=[pl.BlockSpec((SB_QB * TQ, LANES), lambda j, i: (i, j)),
                   pl.BlockSpec((None, SB_QB, 2 * TQ, BK), lambda j, i: (j, i, 0, 0)),
                   pl.BlockSpec((None, SB_QB, 8, LANES), lambda j, i: (j, i, 0, 0)), wide_spec, wide_spec] + s_out,
        input_output_aliases=s_alias,
        scratch_shapes=[pltpu.VMEM((SB_QB, 2 * TQ, BK), F32), pltpu.VMEM((SB_QB, 2 * TQ, LANES), F32)] + s_sems,
        compiler_params=_params(2), name=name)(qkv, qkv, qkv, sufw, *(side.arrays if side else ()))
    return outs[:5], outs[5:]


def _sb_bwd(qkv, lt, nb, a_wide, sg_wide, doa, name, side=None):
    s = qkv.shape[0]
    nq = s // TQ
    npair = SB_W // LANES
    _, prew, prexw = _sb_wide_consts()

    def body(q_ref, k_ref, v_ref, lt_ref, nb_ref, do_ref, prew_ref, prexw_ref, a_ref, sg_ref, *rest):
        side_in, (dq_ref, dk_out, dv_out), side_out, scratch = _side_refs(side, rest, 3)
        cp_ref, ce_ref, dqa_ref, dk_ref, dv_ref = scratch[:5]
        sems = scratch[5:]
        step_id = pl.program_id(1)
        if side is not None:
            @pl.when((pl.program_id(0) == 0) & (step_id == 0))
            def _():
                side.start(side_in, side_out, *sems)
        lane = lax.broadcasted_iota(jnp.int32, (TQ, LANES), 1)
        rel = _sb_rel()
        blocks = [step_id * SB_QB + b for b in range(SB_QB)]
        rows = [slice(b * TQ, (b + 1) * TQ) for b in range(SB_QB)]
        qs = [_pair_stack(q_ref[rows[b], :] * SCALE, lane) for b in range(SB_QB)]
        dos = [_pair_stack(do_ref[rows[b], :], lane) for b in range(SB_QB)]
        n_blocks = [jnp.clip(jnp.max(nb_ref[b]).astype(jnp.int32), 1, i + 1) for b, i in enumerate(blocks)]
        first = [i + 1 - n for i, n in zip(blocks, n_blocks)]

        @pl.when(step_id == 0)
        def _():
            dk_ref[...] = jnp.zeros_like(dk_ref)
            dv_ref[...] = jnp.zeros_like(dv_ref)

        straight = n_blocks[0] == SB_STRAIGHT
        for n in n_blocks[1:]:
            straight = straight & (n == SB_STRAIGHT)

        @pl.when(straight)
        def _():
            for b in range(SB_QB):
                w0 = pl.multiple_of(first[b] * BK, BK)
                kw = k_ref[pl.ds(w0, SB_WIDE), :]
                vw = v_ref[pl.ds(w0, SB_WIDE), :]
                a = a_ref[b]
                e = a.astype(F32) * _dot_nt(dos[b], vw)
                big_e, _ = _wide_sums(e, prexw_ref[...], False, two_pass=False)
                dz = (e - sg_ref[b].astype(F32) * (e + big_e)).astype(BF16)
                dk_ref[pl.ds(w0, SB_WIDE), :] += _dot_tn(dz, qs[b])
                dv_ref[pl.ds(w0, SB_WIDE), :] += _dot_tn(a, dos[b])
                dqa_ref[b] = _dot(dz, kw)

        @pl.when(jnp.logical_not(straight))
        def _():
            cp_ref[...] = jnp.zeros_like(cp_ref)
            ce_ref[...] = jnp.zeros_like(ce_ref)
            dqa_ref[...] = jnp.zeros_like(dqa_ref)
            for b, i in enumerate(blocks):
                q0 = i * TQ

                def step(it, carry, b=b, q0=q0):
                    k0 = pl.multiple_of((first[b] + it) * BK, BK)
                    k = k_ref[pl.ds(k0, BK), :]
                    v = v_ref[pl.ds(k0, BK), :]
                    mask = rel < (q0 - k0)
                    lb, lm = _sb_scores(qs[b], k, mask)
                    cs = _split_dot(lm, prew_ref[...])
                    a = jnp.where(mask, jnp.exp(lb + (lt_ref[b] - (cs[:, :BK] + cp_ref[b]))), 0.0)
                    e = a * _dot_nt(dos[b], v)
                    ce = _split_dot(e, prexw_ref[...], two_pass=False)
                    big_e = ce[:, :BK] + ce_ref[b]
                    dz = jnp.where(mask, e - jnp.exp(lb) * (e + big_e), 0.0).astype(BF16)
                    dk_ref[pl.ds(k0, BK), :] += _dot_tn(dz, qs[b])
                    dv_ref[pl.ds(k0, BK), :] += _dot_tn(a.astype(BF16), dos[b])
                    dqa_ref[b] += _dot(dz, k)
                    cp_ref[b] += cs[:, BK:]
                    ce_ref[b] += ce[:, BK:]
                    return carry

                lax.fori_loop(0, n_blocks[b], step, 0)

        for b in range(SB_QB):
            dq = jnp.where(lane < HEAD_DIM, dqa_ref[b, :TQ, :], dqa_ref[b, TQ:, :])
            dq_ref[rows[b], :] = (dq * SCALE).astype(BF16)

        @pl.when(step_id == nq // SB_QB - 1)
        def _():
            dk_out[...] = dk_ref[...].astype(BF16)
            dv_out[...] = dv_ref[...].astype(BF16)

        if side is not None:
            @pl.when((pl.program_id(0) == npair - 1) & (step_id == nq // SB_QB - 1))
            def _():
                side.finish(side_in, side_out, *sems)

    s_in, s_shape, s_out, s_alias, s_sems = _side_specs(side, 10, 3)
    wide_spec = pl.BlockSpec((None, SB_QB, 2 * TQ, SB_WIDE), lambda j, i: (j, i, 0, 0))
    outs = pl.pallas_call(
        body,
        out_shape=[jax.ShapeDtypeStruct((s, SB_W), BF16)] * 3 + s_shape,
        grid=(npair, nq // SB_QB),
        in_specs=[pl.BlockSpec((SB_QB * TQ, LANES), lambda j, i: (i, j)),
                  pl.BlockSpec((s, LANES), lambda j, i: (0, npair + j)),
                  pl.BlockSpec((s, LANES), lambda j, i: (0, 2 * npair + j)),
                  pl.BlockSpec((None, SB_QB, 2 * TQ, BK), lambda j, i: (j, i, 0, 0)),
                  pl.BlockSpec((None, SB_QB, 8, LANES), lambda j, i: (j, i, 0, 0)),
                  pl.BlockSpec((SB_QB * TQ, LANES), lambda j, i: (i, j)),
                  pl.BlockSpec(prew.shape, lambda j, i: (0, 0)),
                  pl.BlockSpec(prexw.shape, lambda j, i: (0, 0)), wide_spec, wide_spec] + s_in,
        out_specs=[pl.BlockSpec((SB_QB * TQ, LANES), lambda j, i: (i, j)),
                   pl.BlockSpec((s, LANES), lambda j, i: (0, j)),
                   pl.BlockSpec((s, LANES), lambda j, i: (0, j))] + s_out,
        input_output_aliases=s_alias,
        scratch_shapes=[pltpu.VMEM((SB_QB, 2 * TQ, BK), F32), pltpu.VMEM((SB_QB, 2 * TQ, BK), F32),
                        pltpu.VMEM((SB_QB, 2 * TQ, LANES), F32), pltpu.VMEM((s, LANES), F32),
                        pltpu.VMEM((s, LANES), F32)] + s_sems,
        compiler_params=_params(2), name=name)(qkv, qkv, qkv, lt, nb, doa, prew, prexw, a_wide, sg_wide,
                                               *(side.arrays if side else ()))
    return outs[0], outs[1], outs[2], outs[3:]


def _bucket_table():
    i = np.arange(TQ)[:, None]
    j = np.arange(2 * BK)[None, :]
    dist = np.maximum(TQ + i - j, 0)
    max_exact = N_BUCKETS // 2
    df = np.maximum(dist, 1).astype(np.float32)
    large = max_exact + (np.log(df / np.float32(max_exact)) / np.float32(math.log(MAX_DISTANCE / max_exact))
                         * np.float32(N_BUCKETS - max_exact)).astype(np.int32)
    large = np.minimum(large, N_BUCKETS - 1)
    return np.where(dist < max_exact, dist, large).astype(np.int32)


def _swa_align_in(t, lane, g):
    tf = t.astype(F32)
    tr = pltpu.roll(tf, HEAD_DIM, 1)
    gmask = (lane >= HEAD_DIM) == (g == 1)
    top = jnp.where(gmask, jnp.where(g == 0, tf, tr), 0.0)
    bot = jnp.where(gmask, jnp.where(g == 1, tf, tr), 0.0)
    return jnp.concatenate([top, bot], axis=0).astype(BF16)


def _swa_align_out(t, lane, g):
    top, bot = t[:TQ, :], t[TQ:, :]
    top = jnp.where(g == 0, top, pltpu.roll(top, HEAD_DIM, 1))
    bot = jnp.where(g == 1, bot, pltpu.roll(bot, HEAD_DIM, 1))
    return jnp.where(lane < HEAD_DIM, top, bot)


def _swa_bias(bias_ref, bucket_ref, rb_ref, j):
    dist = TQ + lax.broadcasted_iota(jnp.int32, (TQ, 2 * BK), 0) - lax.broadcasted_iota(jnp.int32, (TQ, 2 * BK), 1)
    window = (dist >= 0) & (dist < WINDOW)
    for hh in range(2):
        def add(b, acc):
            return acc + jnp.where(bucket_ref[...] == b, rb_ref[b, 2 * j + hh], 0.0)
        bias = lax.fori_loop(0, N_BUCKETS, add, jnp.zeros((TQ, 2 * BK), F32))
        bias_ref[hh * TQ:(hh + 1) * TQ, :] = jnp.where(window, bias, NEG)


def _swa_probs(qs, k2, bias, own_block, sink_ref, i, j):
    s = _dot_nt(qs, k2) + bias
    s = jnp.where(own_block | (i > 0), s, NEG)
    row1 = lax.broadcasted_iota(jnp.int32, (2 * TQ, 1), 0)
    sink = jnp.where(row1 < TQ, sink_ref[2 * j], sink_ref[2 * j + 1])
    m = jnp.maximum(jnp.max(s, axis=1, keepdims=True), sink)
    e = jnp.exp(s - m)
    es = jnp.exp(sink - m)
    inv = 1.0 / (jnp.sum(e, axis=1, keepdims=True) + es)
    return e * inv, es * inv


def _swa_kv(ref, i):
    prev = pl.multiple_of(jnp.maximum(i - 1, 0) * BK, BK)
    cur = pl.multiple_of(i * BK, BK)
    return jnp.concatenate([ref[pl.ds(prev, BK), :], ref[pl.ds(cur, BK), :]], axis=0), prev, cur


def _swa_fwd(qkv, sinks, rel_bias, name, side=None):
    s = qkv.shape[0]
    nq = s // TQ
    npair = SW_QW // LANES
    qcol = 3 * SB_W // LANES
    bucket = jnp.asarray(_bucket_table())

    def body(q_ref, k_ref, v_ref, bucket_ref, sink_ref, rb_ref, *rest):
        side_in, (o_ref, p_ref), side_out, scratch = _side_refs(side, rest, 2)
        bias_ref, sems = scratch[0], scratch[1:]
        j = pl.program_id(0)
        step = pl.program_id(1)
        if side is not None:
            @pl.when((j == 0) & (step == 0))
            def _():
                side.start(side_in, side_out, *sems)
        g = j // 2
        lane = lax.broadcasted_iota(jnp.int32, (TQ, LANES), 1)

        @pl.when(step == 0)
        def _():
            _swa_bias(bias_ref, bucket_ref, rb_ref, j)

        own_block = lax.broadcasted_iota(jnp.int32, (2 * TQ, 2 * BK), 1) >= BK
        for b in range(SWA_QB):
            i = step * SWA_QB + b
            rows = slice(b * TQ, (b + 1) * TQ)
            qs = _swa_align_in(q_ref[rows, :] * SCALE, lane, g)
            k2, _, _ = _swa_kv(k_ref, i)
            v2, _, _ = _swa_kv(v_ref, i)
            pr, psink = _swa_probs(qs, k2, bias_ref[...], own_block, sink_ref, i, j)
            prb = pr.astype(BF16)
            o_ref[rows, :] = _swa_align_out(_dot(prb, v2), lane, g).astype(BF16)
            p_ref[b, :, :2 * BK] = prb
            p_ref[b, :, 2 * BK:] = jnp.broadcast_to(psink, (2 * TQ, BK)).astype(BF16)
        if side is not None:
            @pl.when((j == npair - 1) & (step == nq // SWA_QB - 1))
            def _():
                side.finish(side_in, side_out, *sems)

    assert nq % SWA_QB == 0
    s_in, s_shape, s_out, s_alias, s_sems = _side_specs(side, 6, 2)
    outs = pl.pallas_call(
        body,
        out_shape=[jax.ShapeDtypeStruct((s, SW_QW), BF16),
                   jax.ShapeDtypeStruct((npair, nq, 2 * TQ, 3 * BK), BF16)] + s_shape,
        grid=(npair, nq // SWA_QB),
        in_specs=[pl.BlockSpec((SWA_QB * TQ, LANES), lambda j, i: (i, qcol + j)),
                  pl.BlockSpec((s, LANES), lambda j, i: (0, qcol + npair)),
                  pl.BlockSpec((s, LANES), lambda j, i: (0, qcol + npair + 1)),
                  pl.BlockSpec((TQ, 2 * BK), lambda j, i: (0, 0)),
                  pl.BlockSpec(memory_space=pltpu.SMEM),
                  pl.BlockSpec(memory_space=pltpu.SMEM)] + s_in,
        out_specs=[pl.BlockSpec((SWA_QB * TQ, LANES), lambda j, i: (i, j)),
                   pl.BlockSpec((None, SWA_QB, 2 * TQ, 3 * BK), lambda j, i: (j, i, 0, 0))] + s_out,
        input_output_aliases=s_alias,
        scratch_shapes=[pltpu.VMEM((2 * TQ, 2 * BK), F32)] + s_sems,
        compiler_params=_params(2), name=name)(qkv, qkv, qkv, bucket, sinks, rel_bias,
                                               *(side.arrays if side else ()))
    return outs[:2], outs[2:]


def _swa_bwd(qkv, ob, dob, probs, name, side=None):
    s = qkv.shape[0]
    nq = s // TQ
    npair = SW_QW // LANES
    qcol = 3 * SB_W // LANES
    bucket = jnp.asarray(_bucket_table())

    def body(q_ref, k_ref, v_ref, o_ref, do_ref, bucket_ref, p_ref, *rest):
        side_in, (dq_ref, dkv_ref, dsink_ref, drel_ref), side_out, scratch = _side_refs(side, rest, 4)
        dsacc_ref, dk_ref, dv_ref = scratch[:3]
        sems = scratch[3:]
        j = pl.program_id(0)
        step = pl.program_id(1)
        if side is not None:
            @pl.when((j == 0) & (step == 0))
            def _():
                side.start(side_in, side_out, *sems)
        g = j // 2
        lane = lax.broadcasted_iota(jnp.int32, (TQ, LANES), 1)
        row8 = lax.broadcasted_iota(jnp.int32, (SW_HEADS, LANES), 0)
        lane8 = lax.broadcasted_iota(jnp.int32, (SW_HEADS, LANES), 1)

        @pl.when((step == 0) & (j == 0))
        def _():
            dk_ref[...] = jnp.zeros_like(dk_ref)
            dv_ref[...] = jnp.zeros_like(dv_ref)
            dsink_ref[...] = jnp.zeros_like(dsink_ref)
            drel_ref[...] = jnp.zeros_like(drel_ref)

        @pl.when(step == 0)
        def _():
            dsacc_ref[...] = jnp.zeros_like(dsacc_ref)

        ds_sum = jnp.zeros(dsacc_ref.shape, F32)
        dsink = jnp.zeros((SW_HEADS, LANES), F32)
        for b in range(SWA_QB):
            i = step * SWA_QB + b
            rows = slice(b * TQ, (b + 1) * TQ)
            qs = _swa_align_in(q_ref[rows, :] * SCALE, lane, g)
            do = do_ref[rows, :]
            dos = _swa_align_in(do, lane, g)
            dof = do.astype(F32) * o_ref[rows, :].astype(F32)
            d0 = jnp.sum(jnp.where(lane < HEAD_DIM, dof, 0.0), axis=1, keepdims=True)
            d1 = jnp.sum(jnp.where(lane >= HEAD_DIM, dof, 0.0), axis=1, keepdims=True)
            delta = jnp.concatenate([d0, d1], axis=0)
            k2, prev, cur = _swa_kv(k_ref, i)
            v2, _, _ = _swa_kv(v_ref, i)
            prb = p_ref[b, :, :2 * BK]
            ds = prb.astype(F32) * (_dot_nt(dos, v2) - delta)
            ds_sum = ds_sum + ds
            sd = p_ref[b, :, 2 * BK:].astype(F32) * delta
            ds0 = -jnp.sum(sd[:TQ, :], axis=0, keepdims=True)
            ds1 = -jnp.sum(sd[TQ:, :], axis=0, keepdims=True)
            dsink = dsink + jnp.where(row8 == 2 * j, ds0, jnp.where(row8 == 2 * j + 1, ds1, 0.0))
            dsb = ds.astype(BF16)
            dq_ref[rows, :] = _swa_align_out(_dot(dsb, k2) * SCALE, lane, g).astype(BF16)
            dk2 = _dot_tn(dsb, qs)
            dv2 = _dot_tn(prb, dos)
            dk_ref[pl.ds(prev, BK), :] += dk2[:BK, :]
            dk_ref[pl.ds(cur, BK), :] += dk2[BK:, :]
            dv_ref[pl.ds(prev, BK), :] += dv2[:BK, :]
            dv_ref[pl.ds(cur, BK), :] += dv2[BK:, :]
        dsacc_ref[...] += ds_sum
        dsink_ref[...] += dsink

        @pl.when(step == nq // SWA_QB - 1)
        def _():
            for hh in range(2):
                def red(b, acc):
                    val = _sum_all(jnp.where(bucket_ref[...] == b, dsacc_ref[hh * TQ:(hh + 1) * TQ, :], 0.0))
                    return jnp.where((row8 == 2 * j + hh) & (lane8 == b), val, acc)
                drel_ref[...] += lax.fori_loop(0, N_BUCKETS, red, jnp.zeros((SW_HEADS, LANES), F32))

        @pl.when((step == nq // SWA_QB - 1) & (j == npair - 1))
        def _():
            dkv_ref[:, :LANES] = dk_ref[...].astype(BF16)
            dkv_ref[:, LANES:] = dv_ref[...].astype(BF16)
            if side is not None:
                side.finish(side_in, side_out, *sems)

    whole = lambda j, i: (0, 0)
    s_in, s_shape, s_out, s_alias, s_sems = _side_specs(side, 7, 4)
    outs = pl.pallas_call(
        body,
        out_shape=[jax.ShapeDtypeStruct((s, SW_QW), BF16), jax.ShapeDtypeStruct((s, 2 * LANES), BF16),
                   jax.ShapeDtypeStruct((SW_HEADS, LANES), F32), jax.ShapeDtypeStruct((SW_HEADS, LANES), F32)] + s_shape,
        grid=(npair, nq // SWA_QB),
        in_specs=[pl.BlockSpec((SWA_QB * TQ, LANES), lambda j, i: (i, qcol + j)),
                  pl.BlockSpec((s, LANES), lambda j, i: (0, qcol + npair)),
                  pl.BlockSpec((s, LANES), lambda j, i: (0, qcol + npair + 1)),
                  pl.BlockSpec((SWA_QB * TQ, LANES), lambda j, i: (i, j)),
                  pl.BlockSpec((SWA_QB * TQ, LANES), lambda j, i: (i, j)),
                  pl.BlockSpec((TQ, 2 * BK), whole),
                  pl.BlockSpec((None, SWA_QB, 2 * TQ, 3 * BK), lambda j, i: (j, i, 0, 0))] + s_in,
        out_specs=[pl.BlockSpec((SWA_QB * TQ, LANES), lambda j, i: (i, j)),
                   pl.BlockSpec((s, 2 * LANES), whole),
                   pl.BlockSpec((SW_HEADS, LANES), whole), pl.BlockSpec((SW_HEADS, LANES), whole)] + s_out,
        input_output_aliases=s_alias,
        scratch_shapes=[pltpu.VMEM((2 * TQ, 2 * BK), F32), pltpu.VMEM((s, LANES), F32),
                        pltpu.VMEM((s, LANES), F32)] + s_sems,
        compiler_params=_params(2), name=name)(qkv, qkv, qkv, ob, dob, bucket, probs,
                                               *(side.arrays if side else ()))
    return outs[:4], outs[4:]


def _acc_init(i, *refs):
    @pl.when(i == 0)
    def _():
        for r in refs:
            r[...] = jnp.zeros_like(r)


def _loss_bwd(x3, target, g, name):
    d = x3.shape[1]

    def body(x_ref, t_ref, g_ref, dx_ref, dg_ref, loss_ref):
        _acc_init(pl.program_id(0), dg_ref, loss_ref)
        x = x_ref[...]
        gv = g_ref[...]
        diff = _rms(x, gv) - t_ref[...]
        loss_ref[...] += 0.5 * jnp.sum(jnp.mean(jnp.square(diff), axis=-1, keepdims=True), axis=0, keepdims=True)
        dx, dg = _rms_bwd(diff * (1.0 / d), x, gv)
        dx_ref[...] = dx
        dg_ref[...] += dg

    return _rowcall(name, body, [x3, target], [g], [(d, F32)], [((1, d), F32), ((1, LANES), F32)])


def _ple_bwd(dx3, pe, gt, x2, g, wpg, name):
    d = x2.shape[1]

    def body(dx3_ref, pe_ref, gt_ref, x_ref, g_ref, w_ref, dpe_ref, dgt_ref, dx2_ref, dg_ref):
        _acc_init(pl.program_id(0), dg_ref)
        dx3 = dx3_ref[...]
        sg = _sigmoid(gt_ref[...])
        dpe_ref[...] = (dx3 * sg).astype(BF16)
        dgt = (dx3 * pe_ref[...] * sg * (1.0 - sg)).astype(BF16)
        dgt_ref[...] = dgt
        dx, dg = _rms_bwd(_dot_nt(dgt, w_ref[...]), x_ref[...], g_ref[...])
        dx2_ref[...] = dx3 + dx
        dg_ref[...] += dg

    return _rowcall(name, body, [dx3, pe, gt, x2], [g, wpg], [(d, BF16), (d, BF16), (d, F32)], [((1, d), F32)])


def _ff2_bwd(dx2, r, w2, name):
    d = dx2.shape[1]
    dff = r.shape[1]

    def body(dx_ref, r_ref, w_ref, du_ref, dxb_ref):
        dxb = dx_ref[...].astype(BF16)
        dxb_ref[...] = dxb
        du_ref[...] = (_dot_nt(dxb, w_ref[...]) * (2.0 * r_ref[...].astype(F32))).astype(BF16)

    return _rowcall(name, body, [dx2, r], [w2], [(dff, BF16), (d, BF16)])


def _ff1_bwd(du, dx2, x1, g, w1, name, side=None):
    d = x1.shape[1]

    def body(du_ref, dx2_ref, x_ref, g_ref, w_ref, dx1_ref, dx1b_ref, dg_ref):
        _acc_init(pl.program_id(0), dg_ref)
        dx, dg = _rms_bwd(_dot_cols_t(du_ref[...], w_ref), x_ref[...], g_ref[...])
        dx1 = dx2_ref[...] + dx
        dx1_ref[...] = dx1
        dx1b_ref[...] = dx1.astype(BF16)
        dg_ref[...] += dg

    return _rowcall(name, body, [du, dx2, x1], [g, w1], [(d, F32), (d, BF16)], [((1, d), F32)], side=side)


def _mixer_bwd(dx1b, gates, oa, ob, wo, wua, wub, name, side=None):
    d = dx1b.shape[1]

    def body(dx_ref, gate_ref, oa_ref, ob_ref, wo_ref, wua_ref, wub_ref,
             dya_ref, dyb_ref, dgate_ref, doa_ref, dob_ref):
        dm = _dot_nt(dx_ref[...], wo_ref[...])
        sa = _sigmoid(gate_ref[:, :d])
        sb = _sigmoid(gate_ref[:, d:])
        ya = _dot_cols(oa_ref[...], wua_ref)
        yb = _dot_cols(ob_ref[...], wub_ref)
        dya = (dm * sa).astype(BF16)
        dyb = (dm * sb).astype(BF16)
        dya_ref[...] = dya
        dyb_ref[...] = dyb
        dgate_ref[:, :d] = (dm * ya * sa * (1.0 - sa)).astype(BF16)
        dgate_ref[:, d:] = (dm * yb * sb * (1.0 - sb)).astype(BF16)
        doa_ref[...] = _dot_cols_t(dya, wua_ref).astype(BF16)
        dob_ref[...] = _dot_cols_t(dyb, wub_ref).astype(BF16)

    return _rowcall(name, body, [dx1b, gates, oa, ob], [wo, wua, wub],
                    [(d, BF16), (d, BF16), (2 * d, BF16), (SB_W, BF16), (SW_QW, BF16)], side=side)


def _inproj_bwd(pieces, dx1, x, g, wt, name, side=None):
    d = x.shape[1]
    n = len(pieces)
    offsets = [sum(pc.shape[1] for pc in pieces[:p]) for p in range(n + 1)]

    def body(*refs):
        dx1_ref, x_ref, g_ref, w_ref, dx_ref, dg_ref = refs[n:]
        _acc_init(pl.program_id(0), dg_ref)
        dh = _dot(refs[0][...], w_ref[:offsets[1], :])
        for p in range(1, n):
            dh = dh + _dot(refs[p][...], w_ref[offsets[p]:offsets[p + 1], :])
        dx, dg = _rms_bwd(dh, x_ref[...], g_ref[...])
        dx_ref[...] = dx1_ref[...] + dx
        dg_ref[...] += dg

    return _rowcall(name, body, list(pieces) + [dx1, x], [g, wt], [(d, F32)], [((1, d), F32)], side=side)


def _tile(n, cap):
    assert n % LANES == 0
    return max(t for t in range(LANES, min(n, cap) + 1, LANES) if n % t == 0)


def _mm_tn(a, b, name, nshard=1):
    s, ka = a.shape
    nb = b.shape[1]
    n = nb // nshard
    ta = _tile(ka, 512)
    tb = _tile(n, 1024)
    per = n // tb

    def body(a_ref, b_ref, o_ref):
        o_ref[...] = _dot_tn(a_ref[...].astype(BF16), b_ref[...].astype(BF16))

    return pl.pallas_call(
        body, out_shape=jax.ShapeDtypeStruct((nshard, ka, n), F32), grid=(nb // tb, ka // ta),
        in_specs=[pl.BlockSpec((s, ta), lambda jb, ia: (0, ia)), pl.BlockSpec((s, tb), lambda jb, ia: (0, jb))],
        out_specs=pl.BlockSpec((None, ta, tb), lambda jb, ia: (jb // per, ia, jb % per)),
        compiler_params=_params(2), name=name)(a, b)


def _mm_tn_pieces(pieces, b, name, side=None):
    s, nb = b.shape
    ta = 256
    n_in = len(pieces) + 1
    tiles = [pc.shape[1] // ta for pc in pieces]
    assert all(pc.shape[1] % ta == 0 for pc in pieces)
    starts = [sum(tiles[:p]) for p in range(len(pieces))]
    tb = _tile(nb, 1024)
    grid = (nb // tb, sum(tiles))

    def body(*refs):
        a_refs, b_ref = refs[:n_in - 1], refs[n_in - 1]
        side_in, (o_ref,), side_out, sems = _side_refs(side, refs[n_in:], 1)
        jb, ia = pl.program_id(0), pl.program_id(1)
        if side is not None:
            @pl.when((jb == 0) & (ia == 0))
            def _():
                side.start(side_in, side_out, *sems)
        for p in range(len(pieces)):
            @pl.when((ia >= starts[p]) & (ia < starts[p] + tiles[p]))
            def _(p=p):
                o_ref[...] = _dot_tn(a_refs[p][...], b_ref[...])
        if side is not None:
            @pl.when((jb == grid[0] - 1) & (ia == grid[1] - 1))
            def _():
                side.finish(side_in, side_out, *sems)

    def piece_spec(p):
        return pl.BlockSpec((s, ta), lambda jb, ia: (0, jnp.clip(ia - starts[p], 0, tiles[p] - 1)))

    s_in, s_shape, s_out, s_alias, s_sems = _side_specs(side, n_in, 1)
    outs = pl.pallas_call(
        body, out_shape=[jax.ShapeDtypeStruct((sum(tiles) * ta, nb), F32)] + s_shape, grid=grid,
        in_specs=[piece_spec(p) for p in range(len(pieces))] + [pl.BlockSpec((s, tb), lambda jb, ia: (0, jb))] + s_in,
        out_specs=[pl.BlockSpec((ta, tb), lambda jb, ia: (ia, jb))] + s_out,
        input_output_aliases=s_alias, scratch_shapes=s_sems,
        compiler_params=_params(2), name=name)(*pieces, b, *(side.arrays if side else ()))
    return outs[0] if side is None else (outs[0], outs[1:])


def _place():
    return lax.axis_index("x"), lax.axis_index("y"), lax.axis_index("c")


def _chip_peer(x, y, k):
    return (x ^ (k >> 1), y ^ (k & 1))


def _row_tile(k, cap=544):
    return max(t for t in range(32, min(k, cap) + 1, 32) if k % t == 0)


def _cast_bf16(w, r, name):
    l, k, n = w.shape
    assert l == 2
    tk = _row_tile(k)

    def body(r_ref, w_ref, o0_ref, o1_ref):
        o0_ref[...] = w_ref[0].astype(BF16)
        o1_ref[...] = w_ref[1].astype(BF16)

    out_spec = pl.BlockSpec((None, tk, n), lambda i, r_ref: (r_ref[0], i, 0))
    return pl.pallas_call(
        body, out_shape=[jax.ShapeDtypeStruct((4, k, n), BF16)] * 2,
        grid_spec=pltpu.PrefetchScalarGridSpec(
            num_scalar_prefetch=1, grid=(k // tk,),
            in_specs=[pl.BlockSpec((l, tk, n), lambda i, r_ref: (0, i, 0))],
            out_specs=[out_spec, out_spec]),
        compiler_params=_params(1), name=name)(r, w)


class _Exchange(NamedTuple):
    arrays: tuple
    aliased: tuple
    sems: tuple
    start: Callable
    finish: Callable


def _all_gather(bufs):
    nt = len(bufs)

    def ici(t, ins, outs, send_sems, recv_sems, x, y, c, r, k):
        return pltpu.make_async_remote_copy(
            src_ref=ins[t].at[r, c], dst_ref=outs[t].at[r, c], send_sem=send_sems.at[t, k - 1],
            recv_sem=recv_sems.at[t, k - 1], device_id=(*_chip_peer(x, y, k), c), device_id_type=MESH)

    def d2d(t, outs, send_sems, recv_sems, x, y, c, r, k, half):
        slab = outs[t].at[r ^ k, half]
        return pltpu.make_async_remote_copy(
            src_ref=slab, dst_ref=slab, send_sem=send_sems.at[t, 2 + k], recv_sem=recv_sems.at[t, 2 + k],
            device_id=(x, y, 1 - c), device_id_type=MESH)

    def start(ins, outs, send_sems, recv_sems):
        x, y, c = _place()
        r = 2 * x + y
        for t in range(nt):
            for k in (1, 2, 3):
                ici(t, ins, outs, send_sems, recv_sems, x, y, c, r, k).start()

    def finish(ins, outs, send_sems, recv_sems):
        x, y, c = _place()
        r = 2 * x + y
        for t in range(nt):
            for k in (1, 2, 3):
                slab = outs[t].at[r ^ k, c]
                pltpu.make_async_remote_copy(
                    src_ref=slab, dst_ref=slab, send_sem=send_sems.at[t, k - 1], recv_sem=recv_sems.at[t, k - 1],
                    device_id=(x, y, 1 - c), device_id_type=MESH).wait_recv()
                d2d(t, outs, send_sems, recv_sems, x, y, c, r, k, c).start()
        for t in range(nt):
            for k in (1, 2, 3):
                d2d(t, outs, send_sems, recv_sems, x, y, c, r, k, 1 - c).wait_recv()
        for t in range(nt):
            for k in (1, 2, 3):
                ici(t, ins, outs, send_sems, recv_sems, x, y, c, r, k).wait_send()
                d2d(t, outs, send_sems, recv_sems, x, y, c, r, k, c).wait_send()

    return _Exchange(tuple(bufs), tuple(range(nt)), (nt, 6), start, finish)


def _run_exchange(name, ex):
    n_in, n_out = len(ex.arrays), len(ex.aliased)

    def body(*refs):
        ins, outs = refs[:n_in], refs[n_in:n_in + n_out]
        send_sems, recv_sems = refs[n_in + n_out:]
        ex.start(ins, outs, send_sems, recv_sems)
        ex.finish(ins, outs, send_sems, recv_sems)

    any_spec = pl.BlockSpec(memory_space=pl.ANY)
    return pl.pallas_call(
        body, out_shape=[jax.ShapeDtypeStruct(ex.arrays[a].shape, ex.arrays[a].dtype) for a in ex.aliased],
        in_specs=[any_spec] * n_in, out_specs=[any_spec] * n_out,
        input_output_aliases={a: o for o, a in enumerate(ex.aliased)},
        scratch_shapes=[pltpu.SemaphoreType.DMA(ex.sems), pltpu.SemaphoreType.DMA(ex.sems)],
        name=name)(*ex.arrays)


def _rs_to_sibling(grads):
    nt = len(grads)
    landing = [lax.empty((4,) + g.shape[2:], F32) for g in grads]

    def copies(ins, outs, send_sems, recv_sems):
        x, y, c = _place()
        return [pltpu.make_async_remote_copy(
            src_ref=ins[t].at[:, 1 - c], dst_ref=outs[t], send_sem=send_sems.at[t], recv_sem=recv_sems.at[t],
            device_id=(x, y, 1 - c), device_id_type=MESH) for t in range(nt)]

    def start(ins, outs, send_sems, recv_sems):
        for cp in copies(ins, outs, send_sems, recv_sems):
            cp.start()

    def finish(ins, outs, send_sems, recv_sems):
        for cp in copies(ins, outs, send_sems, recv_sems):
            cp.wait()

    return _Exchange(tuple(grads) + tuple(landing), tuple(range(nt, 2 * nt)), (nt,), start, finish)


def _add_half(g, recv, cr, name):
    _, _, k2, n = g.shape
    tk = _row_tile(k2)

    def body(cr_ref, g_ref, r_ref, sums_ref, mine_ref):
        val = (g_ref[...] + r_ref[...]).astype(BF16)
        sums_ref[...] = val

        @pl.when(pl.program_id(1) == cr_ref[1])
        def _():
            mine_ref[...] = val

    return pl.pallas_call(
        body, out_shape=[jax.ShapeDtypeStruct((4, k2, n), BF16)] * 2,
        grid_spec=pltpu.PrefetchScalarGridSpec(
            num_scalar_prefetch=1, grid=(k2 // tk, 4),
            in_specs=[pl.BlockSpec((None, None, tk, n), lambda i, q, cr_ref: (q, cr_ref[0], i, 0)),
                      pl.BlockSpec((None, tk, n), lambda i, q, cr_ref: (q, i, 0))],
            out_specs=[pl.BlockSpec((None, tk, n), lambda i, q, cr_ref: (q, i, 0)),
                       pl.BlockSpec((None, tk, n), lambda i, q, cr_ref: (cr_ref[1], i, 0))]),
        compiler_params=_params(2), name=name)(cr, g, recv)


def _rs_to_chips(sums, parts):
    nt = len(sums)

    def copies(ins, outs, send_sems, recv_sems):
        x, y, c = _place()
        r = 2 * x + y
        return [pltpu.make_async_remote_copy(
            src_ref=ins[t].at[r ^ k], dst_ref=outs[t].at[r], send_sem=send_sems.at[t, k - 1],
            recv_sem=recv_sems.at[t, k - 1], device_id=(*_chip_peer(x, y, k), c), device_id_type=MESH)
            for t in range(nt) for k in (1, 2, 3)]

    def start(ins, outs, send_sems, recv_sems):
        for cp in copies(ins, outs, send_sems, recv_sems):
            cp.start()

    def finish(ins, outs, send_sems, recv_sems):
        for cp in copies(ins, outs, send_sems, recv_sems):
            cp.wait()

    return _Exchange(tuple(sums) + tuple(parts), tuple(range(nt, 2 * nt)), (nt, 3), start, finish)


def _sum4(parts, cr, name):
    _, k2, n = parts.shape
    tk = _row_tile(k2)

    def body(cr_ref, p_ref, o_ref):
        p = p_ref[...].astype(F32)
        o_ref[...] = ((p[0] + p[1]) + p[2]) + p[3]

    return pl.pallas_call(
        body, out_shape=jax.ShapeDtypeStruct((2, k2, n), F32),
        grid_spec=pltpu.PrefetchScalarGridSpec(
            num_scalar_prefetch=1, grid=(k2 // tk,),
            in_specs=[pl.BlockSpec((4, tk, n), lambda i, cr_ref: (0, i, 0))],
            out_specs=pl.BlockSpec((None, tk, n), lambda i, cr_ref: (cr_ref[0], i, 0))),
        compiler_params=_params(1), name=name)(cr, parts)


def _exchange_halves(both):
    nt = len(both)

    def copies(ins, outs, send_sems, recv_sems):
        x, y, c = _place()
        return [pltpu.make_async_remote_copy(
            src_ref=ins[t].at[c], dst_ref=outs[t].at[c], send_sem=send_sems.at[t], recv_sem=recv_sems.at[t],
            device_id=(x, y, 1 - c), device_id_type=MESH) for t in range(nt)]

    def start(ins, outs, send_sems, recv_sems):
        for cp in copies(ins, outs, send_sems, recv_sems):
            cp.start()

    def finish(ins, outs, send_sems, recv_sems):
        for cp in copies(ins, outs, send_sems, recv_sems):
            cp.wait()

    return _Exchange(tuple(both), tuple(range(nt)), (nt,), start, finish)


def _adamw_math(w, g, m, v):
    m = ADAM_B1 * m + (1.0 - ADAM_B1) * g
    v = ADAM_B2 * v + (1.0 - ADAM_B2) * jnp.square(g)
    m_hat = m / (1.0 - ADAM_B1 ** ADAM_STEP)
    v_hat = v / (1.0 - ADAM_B2 ** ADAM_STEP)
    delta = -ADAM_LR * (m_hat / (jnp.sqrt(v_hat) + ADAM_EPS) + ADAM_WD * w)
    return delta, m, v


def _adamw(w, m, v, g0, g1, name):
    _, k, n = w.shape
    tk = _row_tile(k)
    nk = k // tk

    def body(w_ref, m_ref, v_ref, g0_ref, g1_ref, grad_ref, delta_ref, nm_ref, nv_ref):
        g = jnp.where(pl.program_id(0) == 0, g0_ref[...], g1_ref[...])
        delta, nm, nv = _adamw_math(w_ref[...], g, m_ref[...], v_ref[...])
        grad_ref[...] = g
        delta_ref[...] = delta
        nm_ref[...] = nm
        nv_ref[...] = nv

    lay = pl.BlockSpec((None, tk, n), lambda a, i: (a, i, 0))
    g0_spec = pl.BlockSpec((tk, n), lambda a, i: (jnp.where(a == 0, i, nk - 1), 0))
    g1_spec = pl.BlockSpec((tk, n), lambda a, i: (jnp.where(a == 1, i, 0), 0))
    return pl.pallas_call(
        body, out_shape=[jax.ShapeDtypeStruct(w.shape, F32)] * 4, grid=(2, nk),
        in_specs=[lay, lay, lay, g0_spec, g1_spec], out_specs=[lay] * 4,
        compiler_params=_params(2), name=name)(w, m, v, g0, g1)


def _small_allreduce_adamw(gpart, w, m, v):
    shape = gpart.shape

    def body(g_ref, w_ref, m_ref, v_ref, gsum_ref, delta_ref, nm_ref, nv_ref, recv_ref, send_sems, recv_sems):
        x, y, c = _place()
        me = 4 * x + 2 * y + c
        recv_ref[me] = g_ref[...]
        cps = []
        for k in range(1, 8):
            peer = (x ^ (k >> 2), y ^ ((k >> 1) & 1), c ^ (k & 1))
            cp = pltpu.make_async_remote_copy(
                src_ref=g_ref, dst_ref=recv_ref.at[me], send_sem=send_sems.at[k - 1], recv_sem=recv_sems.at[k - 1],
                device_id=peer, device_id_type=MESH)
            cp.start()
            cps.append(cp)
        for cp in cps:
            cp.wait()
        g = recv_ref[0]
        for dev in range(1, 8):
            g = g + recv_ref[dev]
        delta, nm, nv = _adamw_math(w_ref[...], g, m_ref[...], v_ref[...])
        gsum_ref[...] = g
        delta_ref[...] = delta
        nm_ref[...] = nm
        nv_ref[...] = nv

    vm = pl.BlockSpec(memory_space=pltpu.VMEM)
    return pl.pallas_call(
        body, out_shape=[jax.ShapeDtypeStruct(shape, F32)] * 4, in_specs=[vm] * 4, out_specs=[vm] * 4,
        scratch_shapes=[pltpu.VMEM((8,) + shape, F32), pltpu.SemaphoreType.DMA((7,)), pltpu.SemaphoreType.DMA((7,))],
        name="small_allreduce_adamw")(gpart, w, m, v)


BIG = ("w_in", "w_up_a", "w_up_b", "w_o", "w_ff1", "w_ff2", "w_pe", "w_pg")
COL_SHARDED = ("w_in", "w_up_a", "w_up_b", "w_ff1", "w_pe")
ROW_SHARDED = ("w_o", "w_ff2", "w_pg")
SMALL_ROWS = 16


def _pack_small(g_mix, g_mlp, g_pe, g_final, sinks, rel_bias, loss=None):
    d = g_final.shape[0]
    row = lambda v: jnp.pad(v.reshape(1, -1), ((0, 0), (0, d - v.size)))
    rows = [g_mix, g_mlp, g_pe, g_final.reshape(1, d),
            jnp.zeros((1, d), F32) if loss is None else row(loss), row(sinks), row(rel_bias)]
    out = jnp.concatenate(rows, axis=0)
    return jnp.pad(out, ((0, SMALL_ROWS - out.shape[0]), (0, 0)))


def _unpack_small(a, sinks_shape, rel_shape):
    return (a[0:2], a[2:4], a[4:6], a[6], a[8, :sinks_shape[0] * sinks_shape[1]].reshape(sinks_shape),
            a[9, :rel_shape[0] * rel_shape[1]].reshape(rel_shape))


def kernel(x, p, w_in, w_up_a, w_up_b, w_o, w_ff1, w_ff2, w_pe, w_pg, g_mix, g_mlp, g_pe, g_final, sinks, rel_bias, loss_target, m_w_in, m_w_up_a, m_w_up_b, m_w_o, m_w_ff1, m_w_ff2, m_w_pe, m_w_pg, m_g_mix, m_g_mlp, m_g_pe, m_g_final, m_sinks, m_rel_bias, v_w_in, v_w_up_a, v_w_up_b, v_w_o, v_w_ff1, v_w_ff2, v_w_pe, v_w_pg, v_g_mix, v_g_mlp, v_g_pe, v_g_final, v_sinks, v_rel_bias):
    depth = w_in.shape[0]
    assert depth == 2
    x0 = x[0]
    target = loss_target[0]
    d = x0.shape[1]
    wl = dict(w_in=w_in, w_up_a=w_up_a, w_up_b=w_up_b, w_o=w_o, w_ff1=w_ff1, w_ff2=w_ff2, w_pe=w_pe, w_pg=w_pg)
    ml = dict(w_in=m_w_in, w_up_a=m_w_up_a, w_up_b=m_w_up_b, w_o=m_w_o, w_ff1=m_w_ff1, w_ff2=m_w_ff2, w_pe=m_w_pe, w_pg=m_w_pg)
    vl = dict(w_in=v_w_in, w_up_a=v_w_up_a, w_up_b=v_w_up_b, w_o=v_w_o, w_ff1=v_w_ff1, w_ff2=v_w_ff2, w_pe=v_w_pe, w_pg=v_w_pg)
    c_idx = lax.axis_index("c").astype(jnp.int32)
    r_idx = (2 * lax.axis_index("x") + lax.axis_index("y")).astype(jnp.int32)
    cr = jnp.stack([c_idx, r_idx])

    wl["w_in"], ml["w_in"], vl["w_in"] = (jnp.swapaxes(a, 1, 2) for a in (w_in, m_w_in, v_w_in))

    bufs = {}
    for n in BIG:
        k, nn = wl[n].shape[1:]
        for l, b in enumerate(_cast_bf16(wl[n], r_idx.reshape(1), "cast_" + n)):
            bufs[n, l] = b.reshape(4, 2, k // 2, nn)

    def gather(keys, run):
        for key, b in zip(keys, run(_all_gather([bufs[key] for key in keys]))):
            bufs[key] = b

    def gathered(n, l):
        _, _, k2, nn = bufs[n, l].shape
        if n in ROW_SHARDED or n == "w_in":
            return bufs[n, l].reshape(8 * k2, nn)
        return bufs[n, l].reshape(4, 2 * k2, nn)

    gather([("w_in", 0)], lambda ex: _run_exchange("all_gather_first", ex))

    full = {n: [None] * depth for n in BIG}
    saved = []
    xi = x0
    for i in range(depth):
        st = dict(x0=xi)
        gm = g_mix[i].reshape(1, d)
        full["w_in"][i] = gathered("w_in", i)
        st["h1"], st["qkv"], st["gates"] = _inproj_fwd(xi, gm, full["w_in"][i], f"inproj_fwd_{i}")

        def attend(ex):
            (st["oa"], st["lt"], st["nb"], st["a_wide"], st["sg_wide"]), filled = _sb_fwd(st["qkv"], f"sb_fwd_{i}", ex)
            return filled

        with_window = [("w_in", 1)] if i == 0 else [(n, i) for n in ("w_up_a", "w_up_b", "w_o")]
        gather([(n, i) for n in BIG if n != "w_in" and (n, i) not in with_window], attend)

        def window(ex):
            (st["ob"], st["probs"]), filled = _swa_fwd(st["qkv"], sinks[i], rel_bias, f"swa_fwd_{i}", ex)
            return filled

        gather(with_window, window)
        for n in BIG:
            if n != "w_in":
                full[n][i] = gathered(n, i)
        st["m"], st["x1"] = _mixer_fwd(st["oa"], st["ob"], st["gates"], xi, full["w_up_a"][i], full["w_up_b"][i],
                                       full["w_o"][i], f"mixer_fwd_{i}")
        st["h2"], st["u"], st["a"] = _ff1_fwd(st["x1"], g_mlp[i].reshape(1, d), full["w_ff1"][i], f"ff1_fwd_{i}")
        st["x2"] = _ff2_fwd(st["a"], st["x1"], full["w_ff2"][i], f"ff2_fwd_{i}")
        st["pb"], st["h3"], st["pe"], st["gt"], xi = _ple_fwd(p[i, 0], st["x2"], g_pe[i].reshape(1, d),
                                                            full["w_pe"][i], full["w_pg"][i], f"ple_fwd_{i}")
        saved.append(st)

    dx, dg_final, loss_part = _loss_bwd(xi, target, g_final.reshape(1, d), "loss_bwd")
    gw = {n: [None] * depth for n in BIG}
    reduced = {}

    chip_sums = {}

    def to_sibling(keys, run):
        tensors = []
        for n, l in keys:
            g = gw[n][l]
            if n in ROW_SHARDED:
                ka, nb = g.shape[1:]
                g = g.reshape(4, ka // 4, nb)
            _, k, nn = g.shape
            tensors.append(g.reshape(4, 2, k // 2, nn))
        for (n, l), g, r in zip(keys, tensors, run(_rs_to_sibling(tensors))):
            chip_sums[n, l] = _add_half(g, r, cr, f"add_half_{n}_{l}")

    def to_chips(keys):
        return _rs_to_chips([chip_sums[k][0] for k in keys], [chip_sums[k][1] for k in keys])

    halves = {}

    def sum_chips(keys, parts):
        for (n, l), pc in zip(keys, parts):
            halves[n, l] = _sum4(pc, cr, f"sum4_{n}_{l}")

    def swap_halves(keys):
        def store(filled):
            for key, both in zip(keys, filled):
                reduced[key] = both
        return _exchange_halves([halves[k] for k in keys]), store

    dg_mix, dg_mlp, dg_pe, dsinks = [None] * depth, [None] * depth, [None] * depth, [None] * depth
    drel = jnp.zeros((SW_HEADS, LANES), F32)
    for i in reversed(range(depth)):
        st = saved[i]
        dpe, dgt, dx2, dg_pe[i] = _ple_bwd(dx, st["pe"], st["gt"], st["x2"], g_pe[i].reshape(1, d),
                                           full["w_pg"][i], f"ple_bwd_{i}")
        gw["w_pe"][i] = _mm_tn(st["pb"], dpe, f"dw_pe_{i}", 4)
        gw["w_pg"][i] = _mm_tn(st["h3"], dgt, f"dw_pg_{i}")
        du, dx2b = _ff2_bwd(dx2, st["u"], full["w_ff2"][i], f"ff2_bwd_{i}")
        gw["w_ff2"][i] = _mm_tn(st["a"], dx2b, f"dw_ff2_{i}")
        gw["w_ff1"][i] = _mm_tn(st["h2"], du, f"dw_ff1_{i}", 4)
        if i == 0:
            (dx1, dx1b, dg_mlp[i]), parts = _ff1_bwd(du, dx2, st["x1"], g_mlp[i].reshape(1, d), full["w_ff1"][i],
                                                    f"ff1_bwd_{i}", to_chips([("w_in", 1)]))
            sum_chips([("w_in", 1)], parts)
        else:
            dx1, dx1b, dg_mlp[i] = _ff1_bwd(du, dx2, st["x1"], g_mlp[i].reshape(1, d), full["w_ff1"][i],
                                            f"ff1_bwd_{i}")
        gw["w_o"][i] = _mm_tn(st["m"], dx1b, f"dw_o_{i}")
        early = [(n, i) for n in ("w_pe", "w_pg", "w_ff2", "w_ff1", "w_o")]

        def mixer(ex):
            (dya, dyb, dgates, doa, dob), landed = _mixer_bwd(
                dx1b, st["gates"], st["oa"], st["ob"], full["w_o"][i], full["w_up_a"][i], full["w_up_b"][i],
                f"mixer_bwd_{i}", ex)
            st.update(dya=dya, dyb=dyb, dgates=dgates, doa=doa, dob=dob)
            return landed

        to_sibling(early, mixer)
        dgates = st["dgates"]
        gw["w_up_a"][i] = _mm_tn(st["oa"], st["dya"], f"dw_up_a_{i}", 4)
        gw["w_up_b"][i] = _mm_tn(st["ob"], st["dyb"], f"dw_up_b_{i}", 4)
        late = [("w_up_a", i), ("w_up_b", i)]
        to_sibling(late, lambda ex: _run_exchange(f"rs_to_sibling_{i}", ex))
        keys = early + late
        with_sb = [(n, i) for n in ("w_ff1", "w_o", "w_pg", "w_pe")]
        with_swa = [(n, i) for n in ("w_ff2", "w_up_a", "w_up_b")]
        dqa, dka, dva, parts = _sb_bwd(st["qkv"], st["lt"], st["nb"], st["a_wide"], st["sg_wide"], st["doa"],
                                       f"sb_bwd_{i}", to_chips(with_sb))
        sum_chips(with_sb, parts)
        (dqb, dkvb, dsk, drl), parts = _swa_bwd(st["qkv"], st["ob"], st["dob"], st["probs"], f"swa_bwd_{i}",
                                                to_chips(with_swa))
        sum_chips(with_swa, parts)
        dsinks[i] = dsk[:, 0]
        drel = drel + drl
        dproj = [dqa, dka, dva, dqb, dkvb, dgates]
        swap, store = swap_halves(keys + ([("w_in", 1)] if i == 0 else []))
        dw_in_t, filled = _mm_tn_pieces(dproj, st["h1"], f"dw_in_{i}", swap)
        store(filled)
        gw["w_in"][i] = dw_in_t.reshape(4, dw_in_t.shape[0] // 4, d)
        if i == 1:
            def inproj(ex):
                (dx, dg_mix[i]), landed = _inproj_bwd(dproj, dx1, st["x0"], g_mix[i].reshape(1, d),
                                                      full["w_in"][i], f"inproj_bwd_{i}", ex)
                st["dx"] = dx
                return landed

            to_sibling([("w_in", 1)], inproj)
            dx = st["dx"]
        else:
            keys = [("w_in", 0)]
            to_sibling(keys, lambda ex: _run_exchange("rs_to_sibling_last", ex))
            (dx, dg_mix[i]), parts = _inproj_bwd(dproj, dx1, st["x0"], g_mix[i].reshape(1, d),
                                                 full["w_in"][i], f"inproj_bwd_{i}", to_chips(keys))
            sum_chips(keys, parts)
            swap, store = swap_halves(keys)
            store(_run_exchange("exchange_halves_last", swap))
    grad_x = dx[None]

    outs = {}
    for n in BIG:
        g0, g1 = (reduced[n, l].reshape(wl[n].shape[1:]) for l in range(depth))
        outs[n] = _adamw(wl[n], ml[n], vl[n], g0, g1, "adamw_" + n)
    outs["w_in"] = [jnp.swapaxes(a, 1, 2) for a in outs["w_in"]]

    drel_bias = drel[:, :N_BUCKETS].T
    gsmall = _pack_small(jnp.concatenate(dg_mix, 0), jnp.concatenate(dg_mlp, 0), jnp.concatenate(dg_pe, 0),
                         dg_final[0], jnp.stack(dsinks), drel_bias, loss_part[0, :1])
    wsmall = _pack_small(g_mix, g_mlp, g_pe, g_final, sinks, rel_bias)
    msmall = _pack_small(m_g_mix, m_g_mlp, m_g_pe, m_g_final, m_sinks, m_rel_bias)
    vsmall = _pack_small(v_g_mix, v_g_mlp, v_g_pe, v_g_final, v_sinks, v_rel_bias)
    small = _small_allreduce_adamw(gsmall, wsmall, msmall, vsmall)
    loss = small[0][7, 0]
    small = [_unpack_small(a, sinks.shape, rel_bias.shape) for a in small]

    result = [loss, grad_x]
    for kind in range(4):
        result += [outs[n][kind] for n in BIG]
        result += list(small[kind])
    return tuple(result)
```

```python
import functools
import math
from typing import Callable, NamedTuple

import numpy as np
import jax
import jax.numpy as jnp
from jax import lax
from jax.experimental import pallas as pl
from jax.experimental.pallas import tpu as pltpu

F32 = jnp.float32
BF16 = jnp.bfloat16
MESH = pl.DeviceIdType.MESH

HEAD_DIM = 64
SB_HEADS = 8
SW_HEADS = 8
SW_KV_HEADS = 2
WINDOW = 128
N_BUCKETS = 32
MAX_DISTANCE = 128
EPS = 1e-6
SB_W = SB_HEADS * HEAD_DIM
SW_QW = SW_HEADS * HEAD_DIM
SW_KVW = SW_KV_HEADS * HEAD_DIM
QKV_W = 3 * SB_W + SW_QW + 2 * SW_KVW
SCALE = HEAD_DIM ** -0.5
assert SCALE == 0.125
LANES = 128
TQ = 128
BK = 128
NEG = -1e30
SB_EXHAUSTED = -106.0

ADAM_LR = 0.001
ADAM_B1 = 0.9
ADAM_B2 = 0.999
ADAM_EPS = 1e-08
ADAM_WD = 0.01
ADAM_STEP = 10

VMEM_LIMIT = 56 * 1024 * 1024


def _dot(a, b):
    return jnp.dot(a, b, preferred_element_type=F32)


def _dot_nt(a, b):
    return lax.dot_general(a, b, (((1,), (1,)), ((), ())), preferred_element_type=F32)


def _dot_tn(a, b):
    return lax.dot_general(a, b, (((0,), (0,)), ((), ())), preferred_element_type=F32)


def _sum_all(x):
    return jnp.sum(jnp.sum(x, axis=1, keepdims=True), axis=0, keepdims=True)


def _sigmoid(x):
    return 1.0 / (1.0 + jnp.exp(-x))


def _rms(x, g):
    r = lax.rsqrt(jnp.mean(x * x, axis=-1, keepdims=True) + EPS)
    return (x * r) * g


def _rms_bwd(dy, x, g):
    r = lax.rsqrt(jnp.mean(x * x, axis=-1, keepdims=True) + EPS)
    n = x * r
    dg = jnp.sum(dy * n, axis=0, keepdims=True)
    dn = dy * g
    dx = r * (dn - n * jnp.mean(dn * n, axis=-1, keepdims=True))
    return dx, dg


def _params(n_axes):
    return pltpu.CompilerParams(dimension_semantics=("arbitrary",) * n_axes, vmem_limit_bytes=VMEM_LIMIT)


def _rowcall(name, body, row_ins, const_ins, row_outs, acc_outs=(), tm=512, side=None):
    s = row_ins[0].shape[0]
    assert s % tm == 0
    in_specs = [pl.BlockSpec((tm, a.shape[1]), lambda i: (i, 0)) for a in row_ins]
    in_specs += [pl.BlockSpec(a.shape, functools.partial(lambda i, nd: (0,) * nd, nd=a.ndim)) for a in const_ins]
    out_shape = [jax.ShapeDtypeStruct((s, c), dt) for c, dt in row_outs]
    out_specs = [pl.BlockSpec((tm, c), lambda i: (i, 0)) for c, _ in row_outs]
    out_shape += [jax.ShapeDtypeStruct(sh, dt) for sh, dt in acc_outs]
    out_specs += [pl.BlockSpec(sh, functools.partial(lambda i, nd: (0,) * nd, nd=len(sh))) for sh, _ in acc_outs]
    if side is None:
        return pl.pallas_call(body, out_shape=out_shape, grid=(s // tm,), in_specs=in_specs, out_specs=out_specs,
                              compiler_params=_params(1), name=name)(*row_ins, *const_ins)
    n_in, n_out = len(in_specs), len(out_specs)

    def with_side(*refs):
        side_in, outs, side_out, sems = _side_refs(side, refs[n_in:], n_out)

        @pl.when(pl.program_id(0) == 0)
        def _():
            side.start(side_in, side_out, *sems)

        body(*refs[:n_in], *outs)

        @pl.when(pl.program_id(0) == s // tm - 1)
        def _():
            side.finish(side_in, side_out, *sems)

    s_in, s_shape, s_out, s_alias, s_sems = _side_specs(side, n_in, n_out)
    outs = pl.pallas_call(with_side, out_shape=out_shape + s_shape, grid=(s // tm,), in_specs=in_specs + s_in,
                          out_specs=out_specs + s_out, input_output_aliases=s_alias, scratch_shapes=s_sems,
                          compiler_params=_params(1), name=name)(*row_ins, *const_ins, *side.arrays)
    return outs[:n_out], outs[n_out:]


def _dot_cols(a, w_ref):
    return jnp.concatenate([_dot(a, w_ref[r]) for r in range(w_ref.shape[0])], axis=1)


def _dot_cols_t(a, w_ref):
    n = w_ref.shape[2]
    out = _dot_nt(a[:, :n], w_ref[0])
    for r in range(1, w_ref.shape[0]):
        out = out + _dot_nt(a[:, r * n:(r + 1) * n], w_ref[r])
    return out


def _inproj_fwd(x, g, wt, name):
    d = x.shape[1]

    def body(x_ref, g_ref, w_ref, h_ref, qkv_ref, gate_ref):
        hb = _rms(x_ref[...], g_ref[...]).astype(BF16)
        h_ref[...] = hb
        qkv_ref[...] = _dot_nt(hb, w_ref[:QKV_W, :]).astype(BF16)
        gate_ref[...] = _dot_nt(hb, w_ref[QKV_W:, :])

    return _rowcall(name, body, [x], [g, wt], [(d, BF16), (QKV_W, BF16), (2 * d, F32)])


def _mixer_fwd(oa, ob, gates, x, wua, wub, wo, name):
    d = x.shape[1]

    def body(oa_ref, ob_ref, gate_ref, x_ref, wua_ref, wub_ref, wo_ref, m_ref, x1_ref):
        ya = _dot_cols(oa_ref[...], wua_ref)
        yb = _dot_cols(ob_ref[...], wub_ref)
        m = _sigmoid(gate_ref[:, :d]) * ya + _sigmoid(gate_ref[:, d:]) * yb
        mb = m.astype(BF16)
        m_ref[...] = mb
        x1_ref[...] = x_ref[...] + _dot(mb, wo_ref[...])

    return _rowcall(name, body, [oa, ob, gates, x], [wua, wub, wo], [(d, BF16), (d, F32)])


def _ff1_fwd(x1, g, w1, name):
    _, d, nq = w1.shape
    dff = 4 * nq

    def body(x_ref, g_ref, w_ref, h_ref, r_ref, a_ref):
        hb = _rms(x_ref[...], g_ref[...]).astype(BF16)
        h_ref[...] = hb
        r = jnp.maximum(_dot_cols(hb, w_ref), 0.0)
        r_ref[...] = r.astype(BF16)
        a_ref[...] = jnp.square(r).astype(BF16)

    return _rowcall(name, body, [x1], [g, w1], [(d, BF16), (dff, BF16), (dff, BF16)])


def _ff2_fwd(a, x1, w2, name):
    d = x1.shape[1]

    def body(a_ref, x_ref, w_ref, o_ref):
        o_ref[...] = x_ref[...] + _dot(a_ref[...], w_ref[...])

    return _rowcall(name, body, [a, x1], [w2], [(d, F32)])[0]


def _ple_fwd(p, x2, g, wpe, wpg, name):
    d = x2.shape[1]

    def body(p_ref, x_ref, g_ref, wpe_ref, wpg_ref, pb_ref, h_ref, pe_ref, gt_ref, x3_ref):
        pb = p_ref[...].astype(BF16)
        pb_ref[...] = pb
        pe = _dot_cols(pb, wpe_ref)
        x = x_ref[...]
        hb = _rms(x, g_ref[...]).astype(BF16)
        h_ref[...] = hb
        gt = _dot(hb, wpg_ref[...])
        pe_ref[...] = pe
        gt_ref[...] = gt
        x3_ref[...] = x + pe * _sigmoid(gt)

    return _rowcall(name, body, [p, x2], [g, wpe, wpg],
                    [(p.shape[1], BF16), (d, BF16), (d, F32), (d, F32), (d, F32)])


def _pair_stack(t, lane):
    zero = jnp.zeros_like(t)
    return jnp.concatenate([jnp.where(lane < HEAD_DIM, t, zero), jnp.where(lane >= HEAD_DIM, t, zero)], axis=0)


def _sb_rel():
    row = lax.broadcasted_iota(jnp.int32, (2 * TQ, BK), 0)
    row = jnp.where(row >= TQ, row - TQ, row)
    col = lax.broadcasted_iota(jnp.int32, (2 * TQ, BK), 1)
    return col - row


def _split_dot(x, m01, two_pass=True):
    hi = x.astype(BF16)
    if not two_pass:
        return _dot(hi, m01)
    lo = (x - hi.astype(F32)).astype(BF16)
    return _dot(hi, m01) + _dot(lo, m01)


def _sb_scores(qs, k, mask):
    z = _dot_nt(qs, k)
    lb = jnp.minimum(z, 0.0) - jnp.log(1.0 + jnp.exp(-jnp.abs(z)))
    lm = lb - z
    return lb, lm if mask is None else jnp.where(mask, lm, 0.0)


SB_STRAIGHT = 3
SB_WIDE = SB_STRAIGHT * BK
SB_QB = 2
SWA_QB = 4


def _sb_wide_consts():
    j = np.arange(BK)[:, None]
    s = np.arange(BK)[None, :]
    ones = np.ones((BK, BK), np.float32)
    as_bf16 = lambda m: jnp.asarray(np.concatenate([m, ones], axis=1).astype(np.float32), dtype=BF16)
    return as_bf16(j > s), as_bf16(j <= s), as_bf16(j < s)


def _wide_sums(x, m01, suffix, two_pass=True):
    parts = [_split_dot(x[:, b * BK:(b + 1) * BK], m01, two_pass) for b in range(SB_STRAIGHT)]
    order = range(SB_STRAIGHT - 1, -1, -1) if suffix else range(SB_STRAIGHT)
    out = [None] * SB_STRAIGHT
    carry = None
    for b in order:
        out[b] = parts[b][:, :BK] if carry is None else parts[b][:, :BK] + carry
        carry = parts[b][:, BK:] if carry is None else carry + parts[b][:, BK:]
    return jnp.concatenate(out, axis=1), carry


def _side_refs(ex, rest, n_out):
    n_in = len(ex.arrays) if ex else 0
    n_alias = len(ex.aliased) if ex else 0
    ins, rest = rest[:n_in], rest[n_in:]
    outs, rest = rest[:n_out], rest[n_out:]
    return ins, outs, rest[:n_alias], rest[n_alias:]


def _side_specs(ex, n_in, n_out):
    if ex is None:
        return [], [], [], {}, []
    any_spec = pl.BlockSpec(memory_space=pl.ANY)
    return ([any_spec] * len(ex.arrays),
            [jax.ShapeDtypeStruct(ex.arrays[a].shape, ex.arrays[a].dtype) for a in ex.aliased],
            [any_spec] * len(ex.aliased), {n_in + a: n_out + o for o, a in enumerate(ex.aliased)},
            [pltpu.SemaphoreType.DMA(shape) for shape in ex.sems for _ in range(2)])


def _sb_fwd(qkv, name, side=None):
    s = qkv.shape[0]
    nq = s // TQ
    npair = SB_W // LANES
    sufw = _sb_wide_consts()[0]

    def body(q_ref, k_ref, v_ref, sufw_ref, *rest):
        side_in, (o_ref, lt_ref, nb_ref, a_ref, sg_ref), side_out, scratch = _side_refs(side, rest, 5)
        cf_ref, acc_ref = scratch[:2]
        step_id = pl.program_id(1)
        if side is not None:
            @pl.when((pl.program_id(0) == 0) & (step_id == 0))
            def _():
                side.start(side_in, side_out, *scratch[2:])
        lane = lax.broadcasted_iota(jnp.int32, (TQ, LANES), 1)
        rel = _sb_rel()
        blocks = [step_id * SB_QB + b for b in range(SB_QB)]
        qs = [_pair_stack(q_ref[b * TQ:(b + 1) * TQ, :] * SCALE, lane) for b in range(SB_QB)]

        straight = blocks[0] >= SB_STRAIGHT - 1

        @pl.when(straight)
        def _():
            for b, i in enumerate(blocks):
                w0 = pl.multiple_of((i - (SB_STRAIGHT - 1)) * BK, BK)
                kw = k_ref[pl.ds(w0, SB_WIDE), :]
                lb, lm = _sb_scores(qs[b], kw, None)
                own = rel < 0
                past = SB_WIDE - BK
                lm = jnp.concatenate([lm[:, :past], jnp.where(own, lm[:, past:], 0.0)], axis=1)
                after, total = _wide_sums(lm, sufw_ref[...], True)
                on_past_keys = lambda t: jnp.concatenate([t[:, :past], jnp.where(own, t[:, past:], 0.0)], axis=1)
                a = on_past_keys(jnp.exp(lb + after)).astype(BF16)
                acc_ref[b] = _dot(a, v_ref[pl.ds(w0, SB_WIDE), :])
                cf_ref[b] = total
                a_ref[b] = a
                sg_ref[b] = on_past_keys(jnp.exp(lb)).astype(BF16)

        @pl.when(jnp.logical_not(straight))
        def _():
            cf_ref[...] = jnp.zeros_like(cf_ref)
            acc_ref[...] = jnp.zeros_like(acc_ref)

        for b, i in enumerate(blocks):
            q0 = i * TQ

            def more(c, i=i):
                return (c[0] <= i) & (c[1] > SB_EXHAUSTED)

            def step(c, b=b, i=i, q0=q0):
                k0 = pl.multiple_of((i - c[0]) * BK, BK)
                k = k_ref[pl.ds(k0, BK), :]
                v = v_ref[pl.ds(k0, BK), :]
                mask = rel < (q0 - k0)
                lb, lm = _sb_scores(qs[b], k, mask)
                cs = _split_dot(lm, sufw_ref[...])
                a = jnp.where(mask, jnp.exp(lb + cs[:, :BK] + cf_ref[b]), 0.0)
                acc_ref[b] += _dot(a.astype(BF16), v)
                cf = cf_ref[b] + cs[:, BK:]
                cf_ref[b] = cf
                return c[0] + 1, jnp.max(cf)

            n_blocks, _ = lax.while_loop(
                more, step, (jnp.where(straight, SB_STRAIGHT, 0).astype(jnp.int32), jnp.max(cf_ref[b])))
            o_ref[b * TQ:(b + 1) * TQ, :] = jnp.where(lane < HEAD_DIM, acc_ref[b, :TQ, :],
                                                     acc_ref[b, TQ:, :]).astype(BF16)
            lt_ref[b] = cf_ref[b]
            nb_ref[b] = jnp.full(nb_ref.shape[1:], n_blocks, F32)
        if side is not None:
            @pl.when((pl.program_id(0) == npair - 1) & (step_id == nq // SB_QB - 1))
            def _():
                side.finish(side_in, side_out, *scratch[2:])

    s_in, s_shape, s_out, s_alias, s_sems = _side_specs(side, 4, 5)
    wide = jax.ShapeDtypeStruct((npair, nq, 2 * TQ, SB_WIDE), BF16)
    wide_spec = pl.BlockSpec((None, SB_QB, 2 * TQ, SB_WIDE), lambda j, i: (j, i, 0, 0))
    outs = pl.pallas_call(
        body,
        out_shape=[jax.ShapeDtypeStruct((s, SB_W), BF16), jax.ShapeDtypeStruct((npair, nq, 2 * TQ, BK), F32),
                   jax.ShapeDtypeStruct((npair, nq, 8, LANES), F32), wide, wide] + s_shape,
        grid=(npair, nq // SB_QB),
        in_specs=[pl.BlockSpec((SB_QB * TQ, LANES), lambda j, i: (i, j)),
                  pl.BlockSpec((s, LANES), lambda j, i: (0, npair + j)),
                  pl.BlockSpec((s, LANES), lambda j, i: (0, 2 * npair + j)),
                  pl.BlockSpec(sufw.shape, lambda j, i: (0, 0))] + s_in,
        out_specs=[pl.BlockSpec((SB_QB * TQ, LANES), lambda j, i: (i, j)),
                   pl.BlockSpec((None, SB_QB, 2 * TQ, BK), lambda j, i: (j, i, 0, 0)),
                   pl.BlockSpec((None, SB_QB, 8, LANES), lambda j, i: (j, i, 0, 0)), wide_spec, wide_spec] + s_out,
        input_output_aliases=s_alias,
        scratch_shapes=[pltpu.VMEM((SB_QB, 2 * TQ, BK), F32), pltpu.VMEM((SB_QB, 2 * TQ, LANES), F32)] + s_sems,
        compiler_params=_params(2), name=name)(qkv, qkv, qkv, sufw, *(side.arrays if side else ()))
    return outs[:5], outs[5:]


def _sb_bwd(qkv, lt, nb, a_wide, sg_wide, doa, name, side=None):
    s = qkv.shape[0]
    nq = s // TQ
    npair = SB_W // LANES
    _, prew, prexw = _sb_wide_consts()

    def body(q_ref, k_ref, v_ref, lt_ref, nb_ref, do_ref, prew_ref, prexw_ref, a_ref, sg_ref, *rest):
        side_in, (dq_ref, dk_out, dv_out), side_out, scratch = _side_refs(side, rest, 3)
        cp_ref, ce_ref, dqa_ref, dk_ref, dv_ref = scratch[:5]
        sems = scratch[5:]
        step_id = pl.program_id(1)
        if side is not None:
            @pl.when((pl.program_id(0) == 0) & (step_id == 0))
            def _():
                side.start(side_in, side_out, *sems)
        lane = lax.broadcasted_iota(jnp.int32, (TQ, LANES), 1)
        rel = _sb_rel()
        blocks = [step_id * SB_QB + b for b in range(SB_QB)]
        rows = [slice(b * TQ, (b + 1) * TQ) for b in range(SB_QB)]
        qs = [_pair_stack(q_ref[rows[b], :] * SCALE, lane) for b in range(SB_QB)]
        dos = [_pair_stack(do_ref[rows[b], :], lane) for b in range(SB_QB)]
        n_blocks = [jnp.clip(jnp.max(nb_ref[b]).astype(jnp.int32), 1, i + 1) for b, i in enumerate(blocks)]
        first = [i + 1 - n for i, n in zip(blocks, n_blocks)]

        @pl.when(step_id == 0)
        def _():
            dk_ref[...] = jnp.zeros_like(dk_ref)
            dv_ref[...] = jnp.zeros_like(dv_ref)

        straight = n_blocks[0] == SB_STRAIGHT
        for n in n_blocks[1:]:
            straight = straight & (n == SB_STRAIGHT)

        @pl.when(straight)
        def _():
            for b in range(SB_QB):
                w0 = pl.multiple_of(first[b] * BK, BK)
                kw = k_ref[pl.ds(w0, SB_WIDE), :]
                vw = v_ref[pl.ds(w0, SB_WIDE), :]
                a = a_ref[b]
                e = a.astype(F32) * _dot_nt(dos[b], vw)
                big_e, _ = _wide_sums(e, prexw_ref[...], False, two_pass=False)
                dz = (e - sg_ref[b].astype(F32) * (e + big_e)).astype(BF16)
                dk_ref[pl.ds(w0, SB_WIDE), :] += _dot_tn(dz, qs[b])
                dv_ref[pl.ds(w0, SB_WIDE), :] += _dot_tn(a, dos[b])
                dqa_ref[b] = _dot(dz, kw)

        @pl.when(jnp.logical_not(straight))
        def _():
            cp_ref[...] = jnp.zeros_like(cp_ref)
            ce_ref[...] = jnp.zeros_like(ce_ref)
            dqa_ref[...] = jnp.zeros_like(dqa_ref)
            for b, i in enumerate(blocks):
                q0 = i * TQ

                def step(it, carry, b=b, q0=q0):
                    k0 = pl.multiple_of((first[b] + it) * BK, BK)
                    k = k_ref[pl.ds(k0, BK), :]
                    v = v_ref[pl.ds(k0, BK), :]
                    mask = rel < (q0 - k0)
                    lb, lm = _sb_scores(qs[b], k, mask)
                    cs = _split_dot(lm, prew_ref[...])
                    a = jnp.where(mask, jnp.exp(lb + (lt_ref[b] - (cs[:, :BK] + cp_ref[b]))), 0.0)
                    e = a * _dot_nt(dos[b], v)
                    ce = _split_dot(e, prexw_ref[...], two_pass=False)
                    big_e = ce[:, :BK] + ce_ref[b]
                    dz = jnp.where(mask, e - jnp.exp(lb) * (e + big_e), 0.0).astype(BF16)
                    dk_ref[pl.ds(k0, BK), :] += _dot_tn(dz, qs[b])
                    dv_ref[pl.ds(k0, BK), :] += _dot_tn(a.astype(BF16), dos[b])
                    dqa_ref[b] += _dot(dz, k)
                    cp_ref[b] += cs[:, BK:]
                    ce_ref[b] += ce[:, BK:]
                    return carry

                lax.fori_loop(0, n_blocks[b], step, 0)

        for b in range(SB_QB):
            dq = jnp.where(lane < HEAD_DIM, dqa_ref[b, :TQ, :], dqa_ref[b, TQ:, :])
            dq_ref[rows[b], :] = (dq * SCALE).astype(BF16)

        @pl.when(step_id == nq // SB_QB - 1)
        def _():
            dk_out[...] = dk_ref[...].astype(BF16)
            dv_out[...] = dv_ref[...].astype(BF16)

        if side is not None:
            @pl.when((pl.program_id(0) == npair - 1) & (step_id == nq // SB_QB - 1))
            def _():
                side.finish(side_in, side_out, *sems)

    s_in, s_shape, s_out, s_alias, s_sems = _side_specs(side, 10, 3)
    wide_spec = pl.BlockSpec((None, SB_QB, 2 * TQ, SB_WIDE), lambda j, i: (j, i, 0, 0))
    outs = pl.pallas_call(
        body,
        out_shape=[jax.ShapeDtypeStruct((s, SB_W), BF16)] * 3 + s_shape,
        grid=(npair, nq // SB_QB),
        in_specs=[pl.BlockSpec((SB_QB * TQ, LANES), lambda j, i: (i, j)),
                  pl.BlockSpec((s, LANES), lambda j, i: (0, npair + j)),
                  pl.BlockSpec((s, LANES), lambda j, i: (0, 2 * npair + j)),
                  pl.BlockSpec((None, SB_QB, 2 * TQ, BK), lambda j, i: (j, i, 0, 0)),
                  pl.BlockSpec((None, SB_QB, 8, LANES), lambda j, i: (j, i, 0, 0)),
                  pl.BlockSpec((SB_QB * TQ, LANES), lambda j, i: (i, j)),
                  pl.BlockSpec(prew.shape, lambda j, i: (0, 0)),
                  pl.BlockSpec(prexw.shape, lambda j, i: (0, 0)), wide_spec, wide_spec] + s_in,
        out_specs=[pl.BlockSpec((SB_QB * TQ, LANES), lambda j, i: (i, j)),
                   pl.BlockSpec((s, LANES), lambda j, i: (0, j)),
                   pl.BlockSpec((s, LANES), lambda j, i: (0, j))] + s_out,
        input_output_aliases=s_alias,
        scratch_shapes=[pltpu.VMEM((SB_QB, 2 * TQ, BK), F32), pltpu.VMEM((SB_QB, 2 * TQ, BK), F32),
                        pltpu.VMEM((SB_QB, 2 * TQ, LANES), F32), pltpu.VMEM((s, LANES), F32),
                        pltpu.VMEM((s, LANES), F32)] + s_sems,
        compiler_params=_params(2), name=name)(qkv, qkv, qkv, lt, nb, doa, prew, prexw, a_wide, sg_wide,
                                               *(side.arrays if side else ()))
    return outs[0], outs[1], outs[2], outs[3:]


def _bucket_table():
    i = np.arange(TQ)[:, None]
    j = np.arange(2 * BK)[None, :]
    dist = np.maximum(TQ + i - j, 0)
    max_exact = N_BUCKETS // 2
    df = np.maximum(dist, 1).astype(np.float32)
    large = max_exact + (np.log(df / np.float32(max_exact)) / np.float32(math.log(MAX_DISTANCE / max_exact))
                         * np.float32(N_BUCKETS - max_exact)).astype(np.int32)
    large = np.minimum(large, N_BUCKETS - 1)
    return np.where(dist < max_exact, dist, large).astype(np.int32)


def _swa_align_in(t, lane, g):
    tf = t.astype(F32)
    tr = pltpu.roll(tf, HEAD_DIM, 1)
    gmask = (lane >= HEAD_DIM) == (g == 1)
    top = jnp.where(gmask, jnp.where(g == 0, tf, tr), 0.0)
    bot = jnp.where(gmask, jnp.where(g == 1, tf, tr), 0.0)
    return jnp.concatenate([top, bot], axis=0).astype(BF16)


def _swa_align_out(t, lane, g):
    top, bot = t[:TQ, :], t[TQ:, :]
    top = jnp.where(g == 0, top, pltpu.roll(top, HEAD_DIM, 1))
    bot = jnp.where(g == 1, bot, pltpu.roll(bot, HEAD_DIM, 1))
    return jnp.where(lane < HEAD_DIM, top, bot)


def _swa_bias(bias_ref, bucket_ref, rb_ref, j):
    dist = TQ + lax.broadcasted_iota(jnp.int32, (TQ, 2 * BK), 0) - lax.broadcasted_iota(jnp.int32, (TQ, 2 * BK), 1)
    window = (dist >= 0) & (dist < WINDOW)
    for hh in range(2):
        def add(b, acc):
            return acc + jnp.where(bucket_ref[...] == b, rb_ref[b, 2 * j + hh], 0.0)
        bias = lax.fori_loop(0, N_BUCKETS, add, jnp.zeros((TQ, 2 * BK), F32))
        bias_ref[hh * TQ:(hh + 1) * TQ, :] = jnp.where(window, bias, NEG)


def _swa_probs(qs, k2, bias, own_block, sink_ref, i, j):
    s = _dot_nt(qs, k2) + bias
    s = jnp.where(own_block | (i > 0), s, NEG)
    row1 = lax.broadcasted_iota(jnp.int32, (2 * TQ, 1), 0)
    sink = jnp.where(row1 < TQ, sink_ref[2 * j], sink_ref[2 * j + 1])
    m = jnp.maximum(jnp.max(s, axis=1, keepdims=True), sink)
    e = jnp.exp(s - m)
    es = jnp.exp(sink - m)
    inv = 1.0 / (jnp.sum(e, axis=1, keepdims=True) + es)
    return e * inv, es * inv


def _swa_kv(ref, i):
    prev = pl.multiple_of(jnp.maximum(i - 1, 0) * BK, BK)
    cur = pl.multiple_of(i * BK, BK)
    return jnp.concatenate([ref[pl.ds(prev, BK), :], ref[pl.ds(cur, BK), :]], axis=0), prev, cur


def _swa_fwd(qkv, sinks, rel_bias, name, side=None):
    s = qkv.shape[0]
    nq = s // TQ
    npair = SW_QW // LANES
    qcol = 3 * SB_W // LANES
    bucket = jnp.asarray(_bucket_table())

    def body(q_ref, k_ref, v_ref, bucket_ref, sink_ref, rb_ref, *rest):
        side_in, (o_ref, p_ref), side_out, scratch = _side_refs(side, rest, 2)
        bias_ref, sems = scratch[0], scratch[1:]
        j = pl.program_id(0)
        step = pl.program_id(1)
        if side is not None:
            @pl.when((j == 0) & (step == 0))
            def _():
                side.start(side_in, side_out, *sems)
        g = j // 2
        lane = lax.broadcasted_iota(jnp.int32, (TQ, LANES), 1)

        @pl.when(step == 0)
        def _():
            _swa_bias(bias_ref, bucket_ref, rb_ref, j)

        own_block = lax.broadcasted_iota(jnp.int32, (2 * TQ, 2 * BK), 1) >= BK
        for b in range(SWA_QB):
            i = step * SWA_QB + b
            rows = slice(b * TQ, (b + 1) * TQ)
            qs = _swa_align_in(q_ref[rows, :] * SCALE, lane, g)
            k2, _, _ = _swa_kv(k_ref, i)
            v2, _, _ = _swa_kv(v_ref, i)
            pr, psink = _swa_probs(qs, k2, bias_ref[...], own_block, sink_ref, i, j)
            prb = pr.astype(BF16)
            o_ref[rows, :] = _swa_align_out(_dot(prb, v2), lane, g).astype(BF16)
            p_ref[b, :, :2 * BK] = prb
            p_ref[b, :, 2 * BK:] = jnp.broadcast_to(psink, (2 * TQ, BK)).astype(BF16)
        if side is not None:
            @pl.when((j == npair - 1) & (step == nq // SWA_QB - 1))
            def _():
                side.finish(side_in, side_out, *sems)

    assert nq % SWA_QB == 0
    s_in, s_shape, s_out, s_alias, s_sems = _side_specs(side, 6, 2)
    outs = pl.pallas_call(
        body,
        out_shape=[jax.ShapeDtypeStruct((s, SW_QW), BF16),
                   jax.ShapeDtypeStruct((npair, nq, 2 * TQ, 3 * BK), BF16)] + s_shape,
        grid=(npair, nq // SWA_QB),
        in_specs=[pl.BlockSpec((SWA_QB * TQ, LANES), lambda j, i: (i, qcol + j)),
                  pl.BlockSpec((s, LANES), lambda j, i: (0, qcol + npair)),
                  pl.BlockSpec((s, LANES), lambda j, i: (0, qcol + npair + 1)),
                  pl.BlockSpec((TQ, 2 * BK), lambda j, i: (0, 0)),
                  pl.BlockSpec(memory_space=pltpu.SMEM),
                  pl.BlockSpec(memory_space=pltpu.SMEM)] + s_in,
        out_specs=[pl.BlockSpec((SWA_QB * TQ, LANES), lambda j, i: (i, j)),
                   pl.BlockSpec((None, SWA_QB, 2 * TQ, 3 * BK), lambda j, i: (j, i, 0, 0))] + s_out,
        input_output_aliases=s_alias,
        scratch_shapes=[pltpu.VMEM((2 * TQ, 2 * BK), F32)] + s_sems,
        compiler_params=_params(2), name=name)(qkv, qkv, qkv, bucket, sinks, rel_bias,
                                               *(side.arrays if side else ()))
    return outs[:2], outs[2:]


def _swa_bwd(qkv, ob, dob, probs, name, side=None):
    s = qkv.shape[0]
    nq = s // TQ
    npair = SW_QW // LANES
    qcol = 3 * SB_W // LANES
    bucket = jnp.asarray(_bucket_table())

    def body(q_ref, k_ref, v_ref, o_ref, do_ref, bucket_ref, p_ref, *rest):
        side_in, (dq_ref, dkv_ref, dsink_ref, drel_ref), side_out, scratch = _side_refs(side, rest, 4)
        dsacc_ref, dk_ref, dv_ref = scratch[:3]
        sems = scratch[3:]
        j = pl.program_id(0)
        step = pl.program_id(1)
        if side is not None:
            @pl.when((j == 0) & (step == 0))
            def _():
                side.start(side_in, side_out, *sems)
        g = j // 2
        lane = lax.broadcasted_iota(jnp.int32, (TQ, LANES), 1)
        row8 = lax.broadcasted_iota(jnp.int32, (SW_HEADS, LANES), 0)
        lane8 = lax.broadcasted_iota(jnp.int32, (SW_HEADS, LANES), 1)

        @pl.when((step == 0) & (j == 0))
        def _():
            dk_ref[...] = jnp.zeros_like(dk_ref)
            dv_ref[...] = jnp.zeros_like(dv_ref)
            dsink_ref[...] = jnp.zeros_like(dsink_ref)
            drel_ref[...] = jnp.zeros_like(drel_ref)

        @pl.when(step == 0)
        def _():
            dsacc_ref[...] = jnp.zeros_like(dsacc_ref)

        ds_sum = jnp.zeros(dsacc_ref.shape, F32)
        dsink = jnp.zeros((SW_HEADS, LANES), F32)
        for b in range(SWA_QB):
            i = step * SWA_QB + b
            rows = slice(b * TQ, (b + 1) * TQ)
            qs = _swa_align_in(q_ref[rows, :] * SCALE, lane, g)
            do = do_ref[rows, :]
            dos = _swa_align_in(do, lane, g)
            dof = do.astype(F32) * o_ref[rows, :].astype(F32)
            d0 = jnp.sum(jnp.where(lane < HEAD_DIM, dof, 0.0), axis=1, keepdims=True)
            d1 = jnp.sum(jnp.where(lane >= HEAD_DIM, dof, 0.0), axis=1, keepdims=True)
            delta = jnp.concatenate([d0, d1], axis=0)
            k2, prev, cur = _swa_kv(k_ref, i)
            v2, _, _ = _swa_kv(v_ref, i)
            prb = p_ref[b, :, :2 * BK]
            ds = prb.astype(F32) * (_dot_nt(dos, v2) - delta)
            ds_sum = ds_sum + ds
            sd = p_ref[b, :, 2 * BK:].astype(F32) * delta
            ds0 = -jnp.sum(sd[:TQ, :], axis=0, keepdims=True)
            ds1 = -jnp.sum(sd[TQ:, :], axis=0, keepdims=True)
            dsink = dsink + jnp.where(row8 == 2 * j, ds0, jnp.where(row8 == 2 * j + 1, ds1, 0.0))
            dsb = ds.astype(BF16)
            dq_ref[rows, :] = _swa_align_out(_dot(dsb, k2) * SCALE, lane, g).astype(BF16)
            dk2 = _dot_tn(dsb, qs)
            dv2 = _dot_tn(prb, dos)
            dk_ref[pl.ds(prev, BK), :] += dk2[:BK, :]
            dk_ref[pl.ds(cur, BK), :] += dk2[BK:, :]
            dv_ref[pl.ds(prev, BK), :] += dv2[:BK, :]
            dv_ref[pl.ds(cur, BK), :] += dv2[BK:, :]
        dsacc_ref[...] += ds_sum
        dsink_ref[...] += dsink

        @pl.when(step == nq // SWA_QB - 1)
        def _():
            for hh in range(2):
                def red(b, acc):
                    val = _sum_all(jnp.where(bucket_ref[...] == b, dsacc_ref[hh * TQ:(hh + 1) * TQ, :], 0.0))
                    return jnp.where((row8 == 2 * j + hh) & (lane8 == b), val, acc)
                drel_ref[...] += lax.fori_loop(0, N_BUCKETS, red, jnp.zeros((SW_HEADS, LANES), F32))

        @pl.when((step == nq // SWA_QB - 1) & (j == npair - 1))
        def _():
            dkv_ref[:, :LANES] = dk_ref[...].astype(BF16)
            dkv_ref[:, LANES:] = dv_ref[...].astype(BF16)
            if side is not None:
                side.finish(side_in, side_out, *sems)

    whole = lambda j, i: (0, 0)
    s_in, s_shape, s_out, s_alias, s_sems = _side_specs(side, 7, 4)
    outs = pl.pallas_call(
        body,
        out_shape=[jax.ShapeDtypeStruct((s, SW_QW), BF16), jax.ShapeDtypeStruct((s, 2 * LANES), BF16),
                   jax.ShapeDtypeStruct((SW_HEADS, LANES), F32), jax.ShapeDtypeStruct((SW_HEADS, LANES), F32)] + s_shape,
        grid=(npair, nq // SWA_QB),
        in_specs=[pl.BlockSpec((SWA_QB * TQ, LANES), lambda j, i: (i, qcol + j)),
                  pl.BlockSpec((s, LANES), lambda j, i: (0, qcol + npair)),
                  pl.BlockSpec((s, LANES), lambda j, i: (0, qcol + npair + 1)),
                  pl.BlockSpec((SWA_QB * TQ, LANES), lambda j, i: (i, j)),
                  pl.BlockSpec((SWA_QB * TQ, LANES), lambda j, i: (i, j)),
                  pl.BlockSpec((TQ, 2 * BK), whole),
                  pl.BlockSpec((None, SWA_QB, 2 * TQ, 3 * BK), lambda j, i: (j, i, 0, 0))] + s_in,
        out_specs=[pl.BlockSpec((SWA_QB * TQ, LANES), lambda j, i: (i, j)),
                   pl.BlockSpec((s, 2 * LANES), whole),
                   pl.BlockSpec((SW_HEADS, LANES), whole), pl.BlockSpec((SW_HEADS, LANES), whole)] + s_out,
        input_output_aliases=s_alias,
        scratch_shapes=[pltpu.VMEM((2 * TQ, 2 * BK), F32), pltpu.VMEM((s, LANES), F32),
                        pltpu.VMEM((s, LANES), F32)] + s_sems,
        compiler_params=_params(2), name=name)(qkv, qkv, qkv, ob, dob, bucket, probs,
                                               *(side.arrays if side else ()))
    return outs[:4], outs[4:]


def _acc_init(i, *refs):
    @pl.when(i == 0)
    def _():
        for r in refs:
            r[...] = jnp.zeros_like(r)


def _loss_bwd(x3, target, g, name):
    d = x3.shape[1]

    def body(x_ref, t_ref, g_ref, dx_ref, dg_ref, loss_ref):
        _acc_init(pl.program_id(0), dg_ref, loss_ref)
        x = x_ref[...]
        gv = g_ref[...]
        diff = _rms(x, gv) - t_ref[...]
        loss_ref[...] += 0.5 * jnp.sum(jnp.mean(jnp.square(diff), axis=-1, keepdims=True), axis=0, keepdims=True)
        dx, dg = _rms_bwd(diff * (1.0 / d), x, gv)
        dx_ref[...] = dx
        dg_ref[...] += dg

    return _rowcall(name, body, [x3, target], [g], [(d, F32)], [((1, d), F32), ((1, LANES), F32)])


def _ple_bwd(dx3, pe, gt, x2, g, wpg, name):
    d = x2.shape[1]

    def body(dx3_ref, pe_ref, gt_ref, x_ref, g_ref, w_ref, dpe_ref, dgt_ref, dx2_ref, dg_ref):
        _acc_init(pl.program_id(0), dg_ref)
        dx3 = dx3_ref[...]
        sg = _sigmoid(gt_ref[...])
        dpe_ref[...] = (dx3 * sg).astype(BF16)
        dgt = (dx3 * pe_ref[...] * sg * (1.0 - sg)).astype(BF16)
        dgt_ref[...] = dgt
        dx, dg = _rms_bwd(_dot_nt(dgt, w_ref[...]), x_ref[...], g_ref[...])
        dx2_ref[...] = dx3 + dx
        dg_ref[...] += dg

    return _rowcall(name, body, [dx3, pe, gt, x2], [g, wpg], [(d, BF16), (d, BF16), (d, F32)], [((1, d), F32)])


def _ff2_bwd(dx2, r, w2, name):
    d = dx2.shape[1]
    dff = r.shape[1]

    def body(dx_ref, r_ref, w_ref, du_ref, dxb_ref):
        dxb = dx_ref[...].astype(BF16)
        dxb_ref[...] = dxb
        du_ref[...] = (_dot_nt(dxb, w_ref[...]) * (2.0 * r_ref[...].astype(F32))).astype(BF16)

    return _rowcall(name, body, [dx2, r], [w2], [(dff, BF16), (d, BF16)])


def _ff1_bwd(du, dx2, x1, g, w1, name, side=None):
    d = x1.shape[1]

    def body(du_ref, dx2_ref, x_ref, g_ref, w_ref, dx1_ref, dx1b_ref, dg_ref):
        _acc_init(pl.program_id(0), dg_ref)
        dx, dg = _rms_bwd(_dot_cols_t(du_ref[...], w_ref), x_ref[...], g_ref[...])
        dx1 = dx2_ref[...] + dx
        dx1_ref[...] = dx1
        dx1b_ref[...] = dx1.astype(BF16)
        dg_ref[...] += dg

    return _rowcall(name, body, [du, dx2, x1], [g, w1], [(d, F32), (d, BF16)], [((1, d), F32)], side=side)


def _mixer_bwd(dx1b, gates, oa, ob, wo, wua, wub, name, side=None):
    d = dx1b.shape[1]

    def body(dx_ref, gate_ref, oa_ref, ob_ref, wo_ref, wua_ref, wub_ref,
             dya_ref, dyb_ref, dgate_ref, doa_ref, dob_ref):
        dm = _dot_nt(dx_ref[...], wo_ref[...])
        sa = _sigmoid(gate_ref[:, :d])
        sb = _sigmoid(gate_ref[:, d:])
        ya = _dot_cols(oa_ref[...], wua_ref)
        yb = _dot_cols(ob_ref[...], wub_ref)
        dya = (dm * sa).astype(BF16)
        dyb = (dm * sb).astype(BF16)
        dya_ref[...] = dya
        dyb_ref[...] = dyb
        dgate_ref[:, :d] = (dm * ya * sa * (1.0 - sa)).astype(BF16)
        dgate_ref[:, d:] = (dm * yb * sb * (1.0 - sb)).astype(BF16)
        doa_ref[...] = _dot_cols_t(dya, wua_ref).astype(BF16)
        dob_ref[...] = _dot_cols_t(dyb, wub_ref).astype(BF16)

    return _rowcall(name, body, [dx1b, gates, oa, ob], [wo, wua, wub],
                    [(d, BF16), (d, BF16), (2 * d, BF16), (SB_W, BF16), (SW_QW, BF16)], side=side)


def _inproj_bwd(pieces, dx1, x, g, wt, name, side=None):
    d = x.shape[1]
    n = len(pieces)
    offsets = [sum(pc.shape[1] for pc in pieces[:p]) for p in range(n + 1)]

    def body(*refs):
        dx1_ref, x_ref, g_ref, w_ref, dx_ref, dg_ref = refs[n:]
        _acc_init(pl.program_id(0), dg_ref)
        dh = _dot(refs[0][...], w_ref[:offsets[1], :])
        for p in range(1, n):
            dh = dh + _dot(refs[p][...], w_ref[offsets[p]:offsets[p + 1], :])
        dx, dg = _rms_bwd(dh, x_ref[...], g_ref[...])
        dx_ref[...] = dx1_ref[...] + dx
        dg_ref[...] += dg

    return _rowcall(name, body, list(pieces) + [dx1, x], [g, wt], [(d, F32)], [((1, d), F32)], side=side)


def _tile(n, cap):
    assert n % LANES == 0
    return max(t for t in range(LANES, min(n, cap) + 1, LANES) if n % t == 0)


def _mm_tn(a, b, name, nshard=1):
    s, ka = a.shape
    nb = b.shape[1]
    n = nb // nshard
    ta = _tile(ka, 512)
    tb = _tile(n, 1024)
    per = n // tb

    def body(a_ref, b_ref, o_ref):
        o_ref[...] = _dot_tn(a_ref[...].astype(BF16), b_ref[...].astype(BF16))

    return pl.pallas_call(
        body, out_shape=jax.ShapeDtypeStruct((nshard, ka, n), F32), grid=(nb // tb, ka // ta),
        in_specs=[pl.BlockSpec((s, ta), lambda jb, ia: (0, ia)), pl.BlockSpec((s, tb), lambda jb, ia: (0, jb))],
        out_specs=pl.BlockSpec((None, ta, tb), lambda jb, ia: (jb // per, ia, jb % per)),
        compiler_params=_params(2), name=name)(a, b)


def _mm_tn_pieces(pieces, b, name, side=None):
    s, nb = b.shape
    ta = 256
    n_in = len(pieces) + 1
    tiles = [pc.shape[1] // ta for pc in pieces]
    assert all(pc.shape[1] % ta == 0 for pc in pieces)
    starts = [sum(tiles[:p]) for p in range(len(pieces))]
    tb = _tile(nb, 1024)
    grid = (nb // tb, sum(tiles))

    def body(*refs):
        a_refs, b_ref = refs[:n_in - 1], refs[n_in - 1]
        side_in, (o_ref,), side_out, sems = _side_refs(side, refs[n_in:], 1)
        jb, ia = pl.program_id(0), pl.program_id(1)
        if side is not None:
            @pl.when((jb == 0) & (ia == 0))
            def _():
                side.start(side_in, side_out, *sems)
        for p in range(len(pieces)):
            @pl.when((ia >= starts[p]) & (ia < starts[p] + tiles[p]))
            def _(p=p):
                o_ref[...] = _dot_tn(a_refs[p][...], b_ref[...])
        if side is not None:
            @pl.when((jb == grid[0] - 1) & (ia == grid[1] - 1))
            def _():
                side.finish(side_in, side_out, *sems)

    def piece_spec(p):
        return pl.BlockSpec((s, ta), lambda jb, ia: (0, jnp.clip(ia - starts[p], 0, tiles[p] - 1)))

    s_in, s_shape, s_out, s_alias, s_sems = _side_specs(side, n_in, 1)
    outs = pl.pallas_call(
        body, out_shape=[jax.ShapeDtypeStruct((sum(tiles) * ta, nb), F32)] + s_shape, grid=grid,
        in_specs=[piece_spec(p) for p in range(len(pieces))] + [pl.BlockSpec((s, tb), lambda jb, ia: (0, jb))] + s_in,
        out_specs=[pl.BlockSpec((ta, tb), lambda jb, ia: (ia, jb))] + s_out,
        input_output_aliases=s_alias, scratch_shapes=s_sems,
        compiler_params=_params(2), name=name)(*pieces, b, *(side.arrays if side else ()))
    return outs[0] if side is None else (outs[0], outs[1:])


def _place():
    return lax.axis_index("x"), lax.axis_index("y"), lax.axis_index("c")


def _chip_peer(x, y, k):
    return (x ^ (k >> 1), y ^ (k & 1))


def _row_tile(k, cap=544):
    return max(t for t in range(32, min(k, cap) + 1, 32) if k % t == 0)


def _cast_bf16(w, r, name):
    l, k, n = w.shape
    assert l == 2
    tk = _row_tile(k)

    def body(r_ref, w_ref, o0_ref, o1_ref):
        o0_ref[...] = w_ref[0].astype(BF16)
        o1_ref[...] = w_ref[1].astype(BF16)

    out_spec = pl.BlockSpec((None, tk, n), lambda i, r_ref: (r_ref[0], i, 0))
    return pl.pallas_call(
        body, out_shape=[jax.ShapeDtypeStruct((4, k, n), BF16)] * 2,
        grid_spec=pltpu.PrefetchScalarGridSpec(
            num_scalar_prefetch=1, grid=(k // tk,),
            in_specs=[pl.BlockSpec((l, tk, n), lambda i, r_ref: (0, i, 0))],
            out_specs=[out_spec, out_spec]),
        compiler_params=_params(1), name=name)(r, w)


class _Exchange(NamedTuple):
    arrays: tuple
    aliased: tuple
    sems: tuple
    start: Callable
    finish: Callable


def _both(a, b):
    na, ma, ka = len(a.arrays), len(a.aliased), 2 * len(a.sems)

    def start(ins, outs, *sems):
        a.start(ins[:na], outs[:ma], *sems[:ka])
        b.start(ins[na:], outs[ma:], *sems[ka:])

    def finish(ins, outs, *sems):
        a.finish(ins[:na], outs[:ma], *sems[:ka])
        b.finish(ins[na:], outs[ma:], *sems[ka:])

    return _Exchange(a.arrays + b.arrays, a.aliased + tuple(na + i for i in b.aliased), a.sems + b.sems, start, finish)


def _all_gather(bufs):
    nt = len(bufs)

    def ici(t, ins, outs, send_sems, recv_sems, x, y, c, r, k):
        return pltpu.make_async_remote_copy(
            src_ref=ins[t].at[r, c], dst_ref=outs[t].at[r, c], send_sem=send_sems.at[t, k - 1],
            recv_sem=recv_sems.at[t, k - 1], device_id=(*_chip_peer(x, y, k), c), device_id_type=MESH)

    def d2d(t, outs, send_sems, recv_sems, x, y, c, r, k, half):
        slab = outs[t].at[r ^ k, half]
        return pltpu.make_async_remote_copy(
            src_ref=slab, dst_ref=slab, send_sem=send_sems.at[t, 2 + k], recv_sem=recv_sems.at[t, 2 + k],
            device_id=(x, y, 1 - c), device_id_type=MESH)

    def start(ins, outs, send_sems, recv_sems):
        x, y, c = _place()
        r = 2 * x + y
        for t in range(nt):
            for k in (1, 2, 3):
                ici(t, ins, outs, send_sems, recv_sems, x, y, c, r, k).start()

    def finish(ins, outs, send_sems, recv_sems):
        x, y, c = _place()
        r = 2 * x + y
        for t in range(nt):
            for k in (1, 2, 3):
                slab = outs[t].at[r ^ k, c]
                pltpu.make_async_remote_copy(
                    src_ref=slab, dst_ref=slab, send_sem=send_sems.at[t, k - 1], recv_sem=recv_sems.at[t, k - 1],
                    device_id=(x, y, 1 - c), device_id_type=MESH).wait_recv()
                d2d(t, outs, send_sems, recv_sems, x, y, c, r, k, c).start()
        for t in range(nt):
            for k in (1, 2, 3):
                d2d(t, outs, send_sems, recv_sems, x, y, c, r, k, 1 - c).wait_recv()
        for t in range(nt):
            for k in (1, 2, 3):
                ici(t, ins, outs, send_sems, recv_sems, x, y, c, r, k).wait_send()
                d2d(t, outs, send_sems, recv_sems, x, y, c, r, k, c).wait_send()

    return _Exchange(tuple(bufs), tuple(range(nt)), ((nt, 6),), start, finish)


def _run_exchange(name, ex):
    n_in, n_out = len(ex.arrays), len(ex.aliased)

    def body(*refs):
        ins, outs = refs[:n_in], refs[n_in:n_in + n_out]
        send_sems, recv_sems = refs[n_in + n_out:]
        ex.start(ins, outs, send_sems, recv_sems)
        ex.finish(ins, outs, send_sems, recv_sems)

    any_spec = pl.BlockSpec(memory_space=pl.ANY)
    return pl.pallas_call(
        body, out_shape=[jax.ShapeDtypeStruct(ex.arrays[a].shape, ex.arrays[a].dtype) for a in ex.aliased],
        in_specs=[any_spec] * n_in, out_specs=[any_spec] * n_out,
        input_output_aliases={a: o for o, a in enumerate(ex.aliased)},
        scratch_shapes=[pltpu.SemaphoreType.DMA(shape) for shape in ex.sems for _ in range(2)],
        name=name)(*ex.arrays)


def _rs_to_sibling(grads):
    nt = len(grads)
    landing = [lax.empty((4,) + g.shape[2:], F32) for g in grads]

    def copies(ins, outs, send_sems, recv_sems):
        x, y, c = _place()
        return [pltpu.make_async_remote_copy(
            src_ref=ins[t].at[:, 1 - c], dst_ref=outs[t], send_sem=send_sems.at[t], recv_sem=recv_sems.at[t],
            device_id=(x, y, 1 - c), device_id_type=MESH) for t in range(nt)]

    def start(ins, outs, send_sems, recv_sems):
        for cp in copies(ins, outs, send_sems, recv_sems):
            cp.start()

    def finish(ins, outs, send_sems, recv_sems):
        for cp in copies(ins, outs, send_sems, recv_sems):
            cp.wait()

    return _Exchange(tuple(grads) + tuple(landing), tuple(range(nt, 2 * nt)), ((nt,),), start, finish)


def _add_half(g, recv, cr, name):
    _, _, k2, n = g.shape
    tk = _row_tile(k2)

    def body(cr_ref, g_ref, r_ref, sums_ref, mine_ref):
        val = (g_ref[...] + r_ref[...]).astype(BF16)
        sums_ref[...] = val

        @pl.when(pl.program_id(1) == cr_ref[1])
        def _():
            mine_ref[...] = val

    return pl.pallas_call(
        body, out_shape=[jax.ShapeDtypeStruct((4, k2, n), BF16)] * 2,
        grid_spec=pltpu.PrefetchScalarGridSpec(
            num_scalar_prefetch=1, grid=(k2 // tk, 4),
            in_specs=[pl.BlockSpec((None, None, tk, n), lambda i, q, cr_ref: (q, cr_ref[0], i, 0)),
                      pl.BlockSpec((None, tk, n), lambda i, q, cr_ref: (q, i, 0))],
            out_specs=[pl.BlockSpec((None, tk, n), lambda i, q, cr_ref: (q, i, 0)),
                       pl.BlockSpec((None, tk, n), lambda i, q, cr_ref: (cr_ref[1], i, 0))]),
        compiler_params=_params(2), name=name)(cr, g, recv)


def _rs_to_chips(sums, parts):
    nt = len(sums)

    def copies(ins, outs, send_sems, recv_sems):
        x, y, c = _place()
        r = 2 * x + y
        return [pltpu.make_async_remote_copy(
            src_ref=ins[t].at[r ^ k], dst_ref=outs[t].at[r], send_sem=send_sems.at[t, k - 1],
            recv_sem=recv_sems.at[t, k - 1], device_id=(*_chip_peer(x, y, k), c), device_id_type=MESH)
            for t in range(nt) for k in (1, 2, 3)]

    def start(ins, outs, send_sems, recv_sems):
        for cp in copies(ins, outs, send_sems, recv_sems):
            cp.start()

    def finish(ins, outs, send_sems, recv_sems):
        for cp in copies(ins, outs, send_sems, recv_sems):
            cp.wait()

    return _Exchange(tuple(sums) + tuple(parts), tuple(range(nt, 2 * nt)), ((nt, 3),), start, finish)


def _sum4(parts, cr, name):
    _, k2, n = parts.shape
    tk = _row_tile(k2)

    def body(cr_ref, p_ref, o_ref):
        p = p_ref[...].astype(F32)
        o_ref[...] = ((p[0] + p[1]) + p[2]) + p[3]

    return pl.pallas_call(
        body, out_shape=jax.ShapeDtypeStruct((2, k2, n), F32),
        grid_spec=pltpu.PrefetchScalarGridSpec(
            num_scalar_prefetch=1, grid=(k2 // tk,),
            in_specs=[pl.BlockSpec((4, tk, n), lambda i, cr_ref: (0, i, 0))],
            out_specs=pl.BlockSpec((None, tk, n), lambda i, cr_ref: (cr_ref[0], i, 0))),
        compiler_params=_params(1), name=name)(cr, parts)


def _exchange_halves(both):
    nt = len(both)

    def copies(ins, outs, send_sems, recv_sems):
        x, y, c = _place()
        return [pltpu.make_async_remote_copy(
            src_ref=ins[t].at[c], dst_ref=outs[t].at[c], send_sem=send_sems.at[t], recv_sem=recv_sems.at[t],
            device_id=(x, y, 1 - c), device_id_type=MESH) for t in range(nt)]

    def start(ins, outs, send_sems, recv_sems):
        for cp in copies(ins, outs, send_sems, recv_sems):
            cp.start()

    def finish(ins, outs, send_sems, recv_sems):
        for cp in copies(ins, outs, send_sems, recv_sems):
            cp.wait()

    return _Exchange(tuple(both), tuple(range(nt)), ((nt,),), start, finish)


def _adamw_math(w, g, m, v):
    m = ADAM_B1 * m + (1.0 - ADAM_B1) * g
    v = ADAM_B2 * v + (1.0 - ADAM_B2) * jnp.square(g)
    m_hat = m / (1.0 - ADAM_B1 ** ADAM_STEP)
    v_hat = v / (1.0 - ADAM_B2 ** ADAM_STEP)
    delta = -ADAM_LR * (m_hat / (jnp.sqrt(v_hat) + ADAM_EPS) + ADAM_WD * w)
    return delta, m, v


def _adamw(w, m, v, g0, g1, name):
    _, k, n = w.shape
    tk = _row_tile(k)
    nk = k // tk

    def body(w_ref, m_ref, v_ref, g0_ref, g1_ref, grad_ref, delta_ref, nm_ref, nv_ref):
        g = jnp.where(pl.program_id(0) == 0, g0_ref[...], g1_ref[...])
        delta, nm, nv = _adamw_math(w_ref[...], g, m_ref[...], v_ref[...])
        grad_ref[...] = g
        delta_ref[...] = delta
        nm_ref[...] = nm
        nv_ref[...] = nv

    lay = pl.BlockSpec((None, tk, n), lambda a, i: (a, i, 0))
    g0_spec = pl.BlockSpec((tk, n), lambda a, i: (jnp.where(a == 0, i, nk - 1), 0))
    g1_spec = pl.BlockSpec((tk, n), lambda a, i: (jnp.where(a == 1, i, 0), 0))
    return pl.pallas_call(
        body, out_shape=[jax.ShapeDtypeStruct(w.shape, F32)] * 4, grid=(2, nk),
        in_specs=[lay, lay, lay, g0_spec, g1_spec], out_specs=[lay] * 4,
        compiler_params=_params(2), name=name)(w, m, v, g0, g1)


def _small_allreduce_adamw(gpart, w, m, v):
    shape = gpart.shape

    def body(g_ref, w_ref, m_ref, v_ref, gsum_ref, delta_ref, nm_ref, nv_ref, recv_ref, send_sems, recv_sems):
        x, y, c = _place()
        me = 4 * x + 2 * y + c
        recv_ref[me] = g_ref[...]
        cps = []
        for k in range(1, 8):
            peer = (x ^ (k >> 2), y ^ ((k >> 1) & 1), c ^ (k & 1))
            cp = pltpu.make_async_remote_copy(
                src_ref=g_ref, dst_ref=recv_ref.at[me], send_sem=send_sems.at[k - 1], recv_sem=recv_sems.at[k - 1],
                device_id=peer, device_id_type=MESH)
            cp.start()
            cps.append(cp)
        for cp in cps:
            cp.wait()
        g = recv_ref[0]
        for dev in range(1, 8):
            g = g + recv_ref[dev]
        delta, nm, nv = _adamw_math(w_ref[...], g, m_ref[...], v_ref[...])
        gsum_ref[...] = g
        delta_ref[...] = delta
        nm_ref[...] = nm
        nv_ref[...] = nv

    vm = pl.BlockSpec(memory_space=pltpu.VMEM)
    return pl.pallas_call(
        body, out_shape=[jax.ShapeDtypeStruct(shape, F32)] * 4, in_specs=[vm] * 4, out_specs=[vm] * 4,
        scratch_shapes=[pltpu.VMEM((8,) + shape, F32), pltpu.SemaphoreType.DMA((7,)), pltpu.SemaphoreType.DMA((7,))],
        name="small_allreduce_adamw")(gpart, w, m, v)


BIG = ("w_in", "w_up_a", "w_up_b", "w_o", "w_ff1", "w_ff2", "w_pe", "w_pg")
COL_SHARDED = ("w_in", "w_up_a", "w_up_b", "w_ff1", "w_pe")
ROW_SHARDED = ("w_o", "w_ff2", "w_pg")
SMALL_ROWS = 16


def _pack_small(g_mix, g_mlp, g_pe, g_final, sinks, rel_bias, loss=None):
    d = g_final.shape[0]
    row = lambda v: jnp.pad(v.reshape(1, -1), ((0, 0), (0, d - v.size)))
    rows = [g_mix, g_mlp, g_pe, g_final.reshape(1, d),
            jnp.zeros((1, d), F32) if loss is None else row(loss), row(sinks), row(rel_bias)]
    out = jnp.concatenate(rows, axis=0)
    return jnp.pad(out, ((0, SMALL_ROWS - out.shape[0]), (0, 0)))


def _unpack_small(a, sinks_shape, rel_shape):
    return (a[0:2], a[2:4], a[4:6], a[6], a[8, :sinks_shape[0] * sinks_shape[1]].reshape(sinks_shape),
            a[9, :rel_shape[0] * rel_shape[1]].reshape(rel_shape))


def kernel(x, p, w_in, w_up_a, w_up_b, w_o, w_ff1, w_ff2, w_pe, w_pg, g_mix, g_mlp, g_pe, g_final, sinks, rel_bias, loss_target, m_w_in, m_w_up_a, m_w_up_b, m_w_o, m_w_ff1, m_w_ff2, m_w_pe, m_w_pg, m_g_mix, m_g_mlp, m_g_pe, m_g_final, m_sinks, m_rel_bias, v_w_in, v_w_up_a, v_w_up_b, v_w_o, v_w_ff1, v_w_ff2, v_w_pe, v_w_pg, v_g_mix, v_g_mlp, v_g_pe, v_g_final, v_sinks, v_rel_bias):
    depth = w_in.shape[0]
    assert depth == 2
    x0 = x[0]
    target = loss_target[0]
    d = x0.shape[1]
    wl = dict(w_in=w_in, w_up_a=w_up_a, w_up_b=w_up_b, w_o=w_o, w_ff1=w_ff1, w_ff2=w_ff2, w_pe=w_pe, w_pg=w_pg)
    ml = dict(w_in=m_w_in, w_up_a=m_w_up_a, w_up_b=m_w_up_b, w_o=m_w_o, w_ff1=m_w_ff1, w_ff2=m_w_ff2, w_pe=m_w_pe, w_pg=m_w_pg)
    vl = dict(w_in=v_w_in, w_up_a=v_w_up_a, w_up_b=v_w_up_b, w_o=v_w_o, w_ff1=v_w_ff1, w_ff2=v_w_ff2, w_pe=v_w_pe, w_pg=v_w_pg)
    c_idx = lax.axis_index("c").astype(jnp.int32)
    r_idx = (2 * lax.axis_index("x") + lax.axis_index("y")).astype(jnp.int32)
    cr = jnp.stack([c_idx, r_idx])

    wl["w_in"], ml["w_in"], vl["w_in"] = (jnp.swapaxes(a, 1, 2) for a in (w_in, m_w_in, v_w_in))

    bufs = {}
    for n in BIG:
        k, nn = wl[n].shape[1:]
        for l, b in enumerate(_cast_bf16(wl[n], r_idx.reshape(1), "cast_" + n)):
            bufs[n, l] = b.reshape(4, 2, k // 2, nn)

    def gather(keys, run):
        for key, b in zip(keys, run(_all_gather([bufs[key] for key in keys]))):
            bufs[key] = b

    def gathered(n, l):
        _, _, k2, nn = bufs[n, l].shape
        if n in ROW_SHARDED or n == "w_in":
            return bufs[n, l].reshape(8 * k2, nn)
        return bufs[n, l].reshape(4, 2 * k2, nn)

    gather([("w_in", 0)], lambda ex: _run_exchange("all_gather_first", ex))

    full = {n: [None] * depth for n in BIG}
    saved = []
    xi = x0
    for i in range(depth):
        st = dict(x0=xi)
        gm = g_mix[i].reshape(1, d)
        full["w_in"][i] = gathered("w_in", i)
        st["h1"], st["qkv"], st["gates"] = _inproj_fwd(xi, gm, full["w_in"][i], f"inproj_fwd_{i}")

        def attend(ex):
            (st["oa"], st["lt"], st["nb"], st["a_wide"], st["sg_wide"]), filled = _sb_fwd(st["qkv"], f"sb_fwd_{i}", ex)
            return filled

        with_window = [("w_in", 1)] if i == 0 else [(n, i) for n in ("w_up_a", "w_up_b", "w_o")]
        gather([(n, i) for n in BIG if n != "w_in" and (n, i) not in with_window], attend)

        def window(ex):
            (st["ob"], st["probs"]), filled = _swa_fwd(st["qkv"], sinks[i], rel_bias, f"swa_fwd_{i}", ex)
            return filled

        gather(with_window, window)
        for n in BIG:
            if n != "w_in":
                full[n][i] = gathered(n, i)
        st["m"], st["x1"] = _mixer_fwd(st["oa"], st["ob"], st["gates"], xi, full["w_up_a"][i], full["w_up_b"][i],
                                       full["w_o"][i], f"mixer_fwd_{i}")
        st["h2"], st["u"], st["a"] = _ff1_fwd(st["x1"], g_mlp[i].reshape(1, d), full["w_ff1"][i], f"ff1_fwd_{i}")
        st["x2"] = _ff2_fwd(st["a"], st["x1"], full["w_ff2"][i], f"ff2_fwd_{i}")
        st["pb"], st["h3"], st["pe"], st["gt"], xi = _ple_fwd(p[i, 0], st["x2"], g_pe[i].reshape(1, d),
                                                            full["w_pe"][i], full["w_pg"][i], f"ple_fwd_{i}")
        saved.append(st)

    dx, dg_final, loss_part = _loss_bwd(xi, target, g_final.reshape(1, d), "loss_bwd")
    gw = {n: [None] * depth for n in BIG}
    reduced = {}

    chip_sums = {}

    def by_halves(keys):
        tensors = []
        for n, l in keys:
            g = gw[n][l]
            if n in ROW_SHARDED:
                ka, nb = g.shape[1:]
                g = g.reshape(4, ka // 4, nb)
            _, k, nn = g.shape
            tensors.append(g.reshape(4, 2, k // 2, nn))
        return tensors

    def add_halves(keys, tensors, landed):
        for (n, l), g, r in zip(keys, tensors, landed):
            chip_sums[n, l] = _add_half(g, r, cr, f"add_half_{n}_{l}")

    def to_sibling(keys, run):
        tensors = by_halves(keys)
        add_halves(keys, tensors, run(_rs_to_sibling(tensors)))

    def to_chips(keys):
        return _rs_to_chips([chip_sums[k][0] for k in keys], [chip_sums[k][1] for k in keys])

    halves = {}

    def sum_chips(keys, parts):
        for (n, l), pc in zip(keys, parts):
            halves[n, l] = _sum4(pc, cr, f"sum4_{n}_{l}")

    def swap_halves(keys):
        def store(filled):
            for key, both in zip(keys, filled):
                reduced[key] = both
        return _exchange_halves([halves[k] for k in keys]), store

    dg_mix, dg_mlp, dg_pe, dsinks = [None] * depth, [None] * depth, [None] * depth, [None] * depth
    drel = jnp.zeros((SW_HEADS, LANES), F32)
    for i in reversed(range(depth)):
        st = saved[i]
        dpe, dgt, dx2, dg_pe[i] = _ple_bwd(dx, st["pe"], st["gt"], st["x2"], g_pe[i].reshape(1, d),
                                           full["w_pg"][i], f"ple_bwd_{i}")
        gw["w_pe"][i] = _mm_tn(st["pb"], dpe, f"dw_pe_{i}", 4)
        gw["w_pg"][i] = _mm_tn(st["h3"], dgt, f"dw_pg_{i}")
        du, dx2b = _ff2_bwd(dx2, st["u"], full["w_ff2"][i], f"ff2_bwd_{i}")
        gw["w_ff2"][i] = _mm_tn(st["a"], dx2b, f"dw_ff2_{i}")
        gw["w_ff1"][i] = _mm_tn(st["h2"], du, f"dw_ff1_{i}", 4)
        if i == 0:
            (dx1, dx1b, dg_mlp[i]), parts = _ff1_bwd(du, dx2, st["x1"], g_mlp[i].reshape(1, d), full["w_ff1"][i],
                                                    f"ff1_bwd_{i}", to_chips([("w_in", 1)]))
            sum_chips([("w_in", 1)], parts)
        else:
            dx1, dx1b, dg_mlp[i] = _ff1_bwd(du, dx2, st["x1"], g_mlp[i].reshape(1, d), full["w_ff1"][i],
                                            f"ff1_bwd_{i}")
        gw["w_o"][i] = _mm_tn(st["m"], dx1b, f"dw_o_{i}")
        early = [(n, i) for n in ("w_pe", "w_pg", "w_ff2", "w_ff1", "w_o")]

        def mixer(ex):
            (dya, dyb, dgates, doa, dob), landed = _mixer_bwd(
                dx1b, st["gates"], st["oa"], st["ob"], full["w_o"][i], full["w_up_a"][i], full["w_up_b"][i],
                f"mixer_bwd_{i}", ex)
            st.update(dya=dya, dyb=dyb, dgates=dgates, doa=doa, dob=dob)
            return landed

        to_sibling(early, mixer)
        dgates = st["dgates"]
        gw["w_up_a"][i] = _mm_tn(st["oa"], st["dya"], f"dw_up_a_{i}", 4)
        gw["w_up_b"][i] = _mm_tn(st["ob"], st["dyb"], f"dw_up_b_{i}", 4)
        late = [("w_up_a", i), ("w_up_b", i)]
        late_halves = by_halves(late)
        keys = early + late
        with_sb = [(n, i) for n in ("w_ff1", "w_o", "w_pg", "w_pe")]
        with_swa = [(n, i) for n in ("w_ff2", "w_up_a", "w_up_b")]
        dqa, dka, dva, filled = _sb_bwd(st["qkv"], st["lt"], st["nb"], st["a_wide"], st["sg_wide"], st["doa"],
                                        f"sb_bwd_{i}", _both(_rs_to_sibling(late_halves), to_chips(with_sb)))
        add_halves(late, late_halves, filled[:len(late)])
        sum_chips(with_sb, filled[len(late):])
        (dqb, dkvb, dsk, drl), parts = _swa_bwd(st["qkv"], st["ob"], st["dob"], st["probs"], f"swa_bwd_{i}",
                                                to_chips(with_swa))
        sum_chips(with_swa, parts)
        dsinks[i] = dsk[:, 0]
        drel = drel + drl
        dproj = [dqa, dka, dva, dqb, dkvb, dgates]
        swap, store = swap_halves(keys + ([("w_in", 1)] if i == 0 else []))
        dw_in_t, filled = _mm_tn_pieces(dproj, st["h1"], f"dw_in_{i}", swap)
        store(filled)
        gw["w_in"][i] = dw_in_t.reshape(4, dw_in_t.shape[0] // 4, d)
        if i == 1:
            def inproj(ex):
                (dx, dg_mix[i]), landed = _inproj_bwd(dproj, dx1, st["x0"], g_mix[i].reshape(1, d),
                                                      full["w_in"][i], f"inproj_bwd_{i}", ex)
                st["dx"] = dx
                return landed

            to_sibling([("w_in", 1)], inproj)
            dx = st["dx"]
        else:
            keys = [("w_in", 0)]
            to_sibling(keys, lambda ex: _run_exchange("rs_to_sibling_last", ex))
            (dx, dg_mix[i]), parts = _inproj_bwd(dproj, dx1, st["x0"], g_mix[i].reshape(1, d),
                                                 full["w_in"][i], f"inproj_bwd_{i}", to_chips(keys))
            sum_chips(keys, parts)
            swap, store = swap_halves(keys)
            store(_run_exchange("exchange_halves_last", swap))
    grad_x = dx[None]

    outs = {}
    for n in BIG:
        g0, g1 = (reduced[n, l].reshape(wl[n].shape[1:]) for l in range(depth))
        outs[n] = _adamw(wl[n], ml[n], vl[n], g0, g1, "adamw_" + n)
    outs["w_in"] = [jnp.swapaxes(a, 1, 2) for a in outs["w_in"]]

    drel_bias = drel[:, :N_BUCKETS].T
    gsmall = _pack_small(jnp.concatenate(dg_mix, 0), jnp.concatenate(dg_mlp, 0), jnp.concatenate(dg_pe, 0),
                         dg_final[0], jnp.stack(dsinks), drel_bias, loss_part[0, :1])
    wsmall = _pack_small(g_mix, g_mlp, g_pe, g_final, sinks, rel_bias)
    msmall = _pack_small(m_g_mix, m_g_mlp, m_g_pe, m_g_final, m_sinks, m_rel_bias)
    vsmall = _pack_small(v_g_mix, v_g_mlp, v_g_pe, v_g_final, v_sinks, v_rel_bias)
    small = _small_allreduce_adamw(gsmall, wsmall, msmall, vsmall)
    loss = small[0][7, 0]
    small = [_unpack_small(a, sinks.shape, rel_bias.shape) for a in small]

    result = [loss, grad_x]
    for kind in range(4):
        result += [outs[n][kind] for n in BIG]
        result += list(small[kind])
    return tuple(result)
```

```python
import functools
import math
from typing import Callable, NamedTuple

import numpy as np
import jax
import jax.numpy as jnp
from jax import lax
from jax.experimental import pallas as pl
from jax.experimental.pallas import tpu as pltpu

F32 = jnp.float32
BF16 = jnp.bfloat16
MESH = pl.DeviceIdType.MESH

HEAD_DIM = 64
SB_HEADS = 8
SW_HEADS = 8
SW_KV_HEADS = 2
WINDOW = 128
N_BUCKETS = 32
MAX_DISTANCE = 128
EPS = 1e-6
SB_W = SB_HEADS * HEAD_DIM
SW_QW = SW_HEADS * HEAD_DIM
SW_KVW = SW_KV_HEADS * HEAD_DIM
QKV_W = 3 * SB_W + SW_QW + 2 * SW_KVW
SCALE = HEAD_DIM ** -0.5
assert SCALE == 0.125
LANES = 128
TQ = 128
BK = 128
NEG = -1e30
SB_EXHAUSTED = -106.0

ADAM_LR = 0.001
ADAM_B1 = 0.9
ADAM_B2 = 0.999
ADAM_EPS = 1e-08
ADAM_WD = 0.01
ADAM_STEP = 10

VMEM_LIMIT = 56 * 1024 * 1024


def _dot(a, b):
    return jnp.dot(a, b, preferred_element_type=F32)


def _dot_nt(a, b):
    return lax.dot_general(a, b, (((1,), (1,)), ((), ())), preferred_element_type=F32)


def _dot_tn(a, b):
    return lax.dot_general(a, b, (((0,), (0,)), ((), ())), preferred_element_type=F32)


def _sum_all(x):
    return jnp.sum(jnp.sum(x, axis=1, keepdims=True), axis=0, keepdims=True)


def _sigmoid(x):
    return 1.0 / (1.0 + jnp.exp(-x))


def _rms(x, g):
    r = lax.rsqrt(jnp.mean(x * x, axis=-1, keepdims=True) + EPS)
    return (x * r) * g


def _rms_bwd(dy, x, g):
    r = lax.rsqrt(jnp.mean(x * x, axis=-1, keepdims=True) + EPS)
    n = x * r
    dg = jnp.sum(dy * n, axis=0, keepdims=True)
    dn = dy * g
    dx = r * (dn - n * jnp.mean(dn * n, axis=-1, keepdims=True))
    return dx, dg


def _params(n_axes):
    return pltpu.CompilerParams(dimension_semantics=("arbitrary",) * n_axes, vmem_limit_bytes=VMEM_LIMIT)


def _rowcall(name, body, row_ins, const_ins, row_outs, acc_outs=(), tm=512, side=None):
    s = row_ins[0].shape[0]
    assert s % tm == 0
    in_specs = [pl.BlockSpec((tm, a.shape[1]), lambda i: (i, 0)) for a in row_ins]
    in_specs += [pl.BlockSpec(a.shape, functools.partial(lambda i, nd: (0,) * nd, nd=a.ndim)) for a in const_ins]
    out_shape = [jax.ShapeDtypeStruct((s, c), dt) for c, dt in row_outs]
    out_specs = [pl.BlockSpec((tm, c), lambda i: (i, 0)) for c, _ in row_outs]
    out_shape += [jax.ShapeDtypeStruct(sh, dt) for sh, dt in acc_outs]
    out_specs += [pl.BlockSpec(sh, functools.partial(lambda i, nd: (0,) * nd, nd=len(sh))) for sh, _ in acc_outs]
    if side is None:
        return pl.pallas_call(body, out_shape=out_shape, grid=(s // tm,), in_specs=in_specs, out_specs=out_specs,
                              compiler_params=_params(1), name=name)(*row_ins, *const_ins)
    n_in, n_out = len(in_specs), len(out_specs)

    def with_side(*refs):
        side_in, outs, side_out, sems = _side_refs(side, refs[n_in:], n_out)

        @pl.when(pl.program_id(0) == 0)
        def _():
            side.start(side_in, side_out, *sems)

        body(*refs[:n_in], *outs)

        @pl.when(pl.program_id(0) == s // tm - 1)
        def _():
            side.finish(side_in, side_out, *sems)

    s_in, s_shape, s_out, s_alias, s_sems = _side_specs(side, n_in, n_out)
    outs = pl.pallas_call(with_side, out_shape=out_shape + s_shape, grid=(s // tm,), in_specs=in_specs + s_in,
                          out_specs=out_specs + s_out, input_output_aliases=s_alias, scratch_shapes=s_sems,
                          compiler_params=_params(1), name=name)(*row_ins, *const_ins, *side.arrays)
    return outs[:n_out], outs[n_out:]


def _dot_cols(a, w_ref):
    return jnp.concatenate([_dot(a, w_ref[r]) for r in range(w_ref.shape[0])], axis=1)


def _dot_cols_t(a, w_ref):
    n = w_ref.shape[2]
    out = _dot_nt(a[:, :n], w_ref[0])
    for r in range(1, w_ref.shape[0]):
        out = out + _dot_nt(a[:, r * n:(r + 1) * n], w_ref[r])
    return out


def _inproj_fwd(x, g, wt, name):
    d = x.shape[1]

    def body(x_ref, g_ref, w_ref, h_ref, qkv_ref, gate_ref):
        hb = _rms(x_ref[...], g_ref[...]).astype(BF16)
        h_ref[...] = hb
        qkv_ref[...] = _dot_nt(hb, w_ref[:QKV_W, :]).astype(BF16)
        gate_ref[...] = _dot_nt(hb, w_ref[QKV_W:, :])

    return _rowcall(name, body, [x], [g, wt], [(d, BF16), (QKV_W, BF16), (2 * d, F32)])


def _mixer_fwd(oa, ob, gates, x, wua, wub, wo, name):
    d = x.shape[1]

    def body(oa_ref, ob_ref, gate_ref, x_ref, wua_ref, wub_ref, wo_ref, m_ref, x1_ref):
        ya = _dot_cols(oa_ref[...], wua_ref)
        yb = _dot_cols(ob_ref[...], wub_ref)
        m = _sigmoid(gate_ref[:, :d]) * ya + _sigmoid(gate_ref[:, d:]) * yb
        mb = m.astype(BF16)
        m_ref[...] = mb
        x1_ref[...] = x_ref[...] + _dot(mb, wo_ref[...])

    return _rowcall(name, body, [oa, ob, gates, x], [wua, wub, wo], [(d, BF16), (d, F32)])


def _ff1_fwd(x1, g, w1, name):
    _, d, nq = w1.shape
    dff = 4 * nq

    def body(x_ref, g_ref, w_ref, h_ref, r_ref, a_ref):
        hb = _rms(x_ref[...], g_ref[...]).astype(BF16)
        h_ref[...] = hb
        r = jnp.maximum(_dot_cols(hb, w_ref), 0.0)
        r_ref[...] = r.astype(BF16)
        a_ref[...] = jnp.square(r).astype(BF16)

    return _rowcall(name, body, [x1], [g, w1], [(d, BF16), (dff, BF16), (dff, BF16)])


def _ff2_fwd(a, x1, w2, name):
    d = x1.shape[1]

    def body(a_ref, x_ref, w_ref, o_ref):
        o_ref[...] = x_ref[...] + _dot(a_ref[...], w_ref[...])

    return _rowcall(name, body, [a, x1], [w2], [(d, F32)])[0]


def _ple_fwd(p, x2, g, wpe, wpg, name):
    d = x2.shape[1]

    def body(p_ref, x_ref, g_ref, wpe_ref, wpg_ref, pb_ref, h_ref, pe_ref, gt_ref, x3_ref):
        pb = p_ref[...].astype(BF16)
        pb_ref[...] = pb
        pe = _dot_cols(pb, wpe_ref)
        x = x_ref[...]
        hb = _rms(x, g_ref[...]).astype(BF16)
        h_ref[...] = hb
        gt = _dot(hb, wpg_ref[...])
        pe_ref[...] = pe
        gt_ref[...] = gt
        x3_ref[...] = x + pe * _sigmoid(gt)

    return _rowcall(name, body, [p, x2], [g, wpe, wpg],
                    [(p.shape[1], BF16), (d, BF16), (d, F32), (d, F32), (d, F32)])


def _pair_stack(t, lane):
    zero = jnp.zeros_like(t)
    return jnp.concatenate([jnp.where(lane < HEAD_DIM, t, zero), jnp.where(lane >= HEAD_DIM, t, zero)], axis=0)


def _sb_rel():
    row = lax.broadcasted_iota(jnp.int32, (2 * TQ, BK), 0)
    row = jnp.where(row >= TQ, row - TQ, row)
    col = lax.broadcasted_iota(jnp.int32, (2 * TQ, BK), 1)
    return col - row


def _split_dot(x, m01, two_pass=True):
    hi = x.astype(BF16)
    if not two_pass:
        return _dot(hi, m01)
    lo = (x - hi.astype(F32)).astype(BF16)
    return _dot(hi, m01) + _dot(lo, m01)


def _sb_scores(qs, k, mask):
    z = _dot_nt(qs, k)
    lb = jnp.minimum(z, 0.0) - jnp.log(1.0 + jnp.exp(-jnp.abs(z)))
    lm = lb - z
    return lb, lm if mask is None else jnp.where(mask, lm, 0.0)


SB_STRAIGHT = 3
SB_WIDE = SB_STRAIGHT * BK
SB_QB = 2
SWA_QB = 4


def _sb_wide_consts():
    j = np.arange(BK)[:, None]
    s = np.arange(BK)[None, :]
    ones = np.ones((BK, BK), np.float32)
    as_bf16 = lambda m: jnp.asarray(np.concatenate([m, ones], axis=1).astype(np.float32), dtype=BF16)
    return as_bf16(j > s), as_bf16(j <= s), as_bf16(j < s)


def _wide_sums(x, m01, suffix, two_pass=True):
    parts = [_split_dot(x[:, b * BK:(b + 1) * BK], m01, two_pass) for b in range(SB_STRAIGHT)]
    order = range(SB_STRAIGHT - 1, -1, -1) if suffix else range(SB_STRAIGHT)
    out = [None] * SB_STRAIGHT
    carry = None
    for b in order:
        out[b] = parts[b][:, :BK] if carry is None else parts[b][:, :BK] + carry
        carry = parts[b][:, BK:] if carry is None else carry + parts[b][:, BK:]
    return jnp.concatenate(out, axis=1), carry


def _side_refs(ex, rest, n_out):
    n_in = len(ex.arrays) if ex else 0
    n_alias = len(ex.aliased) if ex else 0
    ins, rest = rest[:n_in], rest[n_in:]
    outs, rest = rest[:n_out], rest[n_out:]
    return ins, outs, rest[:n_alias], rest[n_alias:]


def _side_specs(ex, n_in, n_out):
    if ex is None:
        return [], [], [], {}, []
    any_spec = pl.BlockSpec(memory_space=pl.ANY)
    return ([any_spec] * len(ex.arrays),
            [jax.ShapeDtypeStruct(ex.arrays[a].shape, ex.arrays[a].dtype) for a in ex.aliased],
            [any_spec] * len(ex.aliased), {n_in + a: n_out + o for o, a in enumerate(ex.aliased)},
            [pltpu.SemaphoreType.DMA(shape) for shape in ex.sems for _ in range(2)])


def _sb_fwd(qkv, name, side=None):
    s = qkv.shape[0]
    nq = s // TQ
    npair = SB_W // LANES
    sufw = _sb_wide_consts()[0]

    def body(q_ref, k_ref, v_ref, sufw_ref, *rest):
        side_in, (o_ref, lt_ref, nb_ref, a_ref, sg_ref), side_out, scratch = _side_refs(side, rest, 5)
        cf_ref, acc_ref = scratch[:2]
        step_id = pl.program_id(1)
        if side is not None:
            @pl.when((pl.program_id(0) == 0) & (step_id == 0))
            def _():
                side.start(side_in, side_out, *scratch[2:])
        lane = lax.broadcasted_iota(jnp.int32, (TQ, LANES), 1)
        rel = _sb_rel()
        blocks = [step_id * SB_QB + b for b in range(SB_QB)]
        qs = [_pair_stack(q_ref[b * TQ:(b + 1) * TQ, :] * SCALE, lane) for b in range(SB_QB)]

        straight = blocks[0] >= SB_STRAIGHT - 1

        @pl.when(straight)
        def _():
            for b, i in enumerate(blocks):
                w0 = pl.multiple_of((i - (SB_STRAIGHT - 1)) * BK, BK)
                kw = k_ref[pl.ds(w0, SB_WIDE), :]
                lb, lm = _sb_scores(qs[b], kw, None)
                own = rel < 0
                past = SB_WIDE - BK
                lm = jnp.concatenate([lm[:, :past], jnp.where(own, lm[:, past:], 0.0)], axis=1)
                after, total = _wide_sums(lm, sufw_ref[...], True)
                on_past_keys = lambda t: jnp.concatenate([t[:, :past], jnp.where(own, t[:, past:], 0.0)], axis=1)
                a = on_past_keys(jnp.exp(lb + after)).astype(BF16)
                acc_ref[b] = _dot(a, v_ref[pl.ds(w0, SB_WIDE), :])
                cf_ref[b] = total
                a_ref[b] = a
                sg_ref[b] = on_past_keys(jnp.exp(lb)).astype(BF16)

        @pl.when(jnp.logical_not(straight))
        def _():
            cf_ref[...] = jnp.zeros_like(cf_ref)
            acc_ref[...] = jnp.zeros_like(acc_ref)

        for b, i in enumerate(blocks):
            q0 = i * TQ

            def more(c, i=i):
                return (c[0] <= i) & (c[1] > SB_EXHAUSTED)

            def step(c, b=b, i=i, q0=q0):
                k0 = pl.multiple_of((i - c[0]) * BK, BK)
                k = k_ref[pl.ds(k0, BK), :]
                v = v_ref[pl.ds(k0, BK), :]
                mask = rel < (q0 - k0)
                lb, lm = _sb_scores(qs[b], k, mask)
                cs = _split_dot(lm, sufw_ref[...])
                a = jnp.where(mask, jnp.exp(lb + cs[:, :BK] + cf_ref[b]), 0.0)
                acc_ref[b] += _dot(a.astype(BF16), v)
                cf = cf_ref[b] + cs[:, BK:]
                cf_ref[b] = cf
                return c[0] + 1, jnp.max(cf)

            n_blocks, _ = lax.while_loop(
                more, step, (jnp.where(straight, SB_STRAIGHT, 0).astype(jnp.int32), jnp.max(cf_ref[b])))
            o_ref[b * TQ:(b + 1) * TQ, :] = jnp.where(lane < HEAD_DIM, acc_ref[b, :TQ, :],
                                                     acc_ref[b, TQ:, :]).astype(BF16)
            lt_ref[b] = cf_ref[b]
            nb_ref[b] = jnp.full(nb_ref.shape[1:], n_blocks, F32)
        if side is not None:
            @pl.when((pl.program_id(0) == npair - 1) & (step_id == nq // SB_QB - 1))
            def _():
                side.finish(side_in, side_out, *scratch[2:])

    s_in, s_shape, s_out, s_alias, s_sems = _side_specs(side, 4, 5)
    wide = jax.ShapeDtypeStruct((npair, nq, 2 * TQ, SB_WIDE), BF16)
    wide_spec = pl.BlockSpec((None, SB_QB, 2 * TQ, SB_WIDE), lambda j, i: (j, i, 0, 0))
    outs = pl.pallas_call(
        body,
        out_shape=[jax.ShapeDtypeStruct((s, SB_W), BF16), jax.ShapeDtypeStruct((npair, nq, 2 * TQ, BK), F32),
                   jax.ShapeDtypeStruct((npair, nq, 8, LANES), F32), wide, wide] + s_shape,
        grid=(npair, nq // SB_QB),
        in_specs=[pl.BlockSpec((SB_QB * TQ, LANES), lambda j, i: (i, j)),
                  pl.BlockSpec((s, LANES), lambda j, i: (0, npair + j)),
                  pl.BlockSpec((s, LANES), lambda j, i: (0, 2 * npair + j)),
                  pl.BlockSpec(sufw.shape, lambda j, i: (0, 0))] + s_in,
        out_specs=[pl.BlockSpec((SB_QB * TQ, LANES), lambda j, i: (i, j)),
                   pl.BlockSpec((None, SB_QB, 2 * TQ, BK), lambda j, i: (j, i, 0, 0)),
                   pl.BlockSpec((None, SB_QB, 8, LANES), lambda j, i: (j, i, 0, 0)), wide_spec, wide_spec] + s_out,
        input_output_aliases=s_alias,
        scratch_shapes=[pltpu.VMEM((SB_QB, 2 * TQ, BK), F32), pltpu.VMEM((SB_QB, 2 * TQ, LANES), F32)] + s_sems,
        compiler_params=_params(2), name=name)(qkv, qkv, qkv, sufw, *(side.arrays if side else ()))
    return outs[:5], outs[5:]


def _sb_bwd(qkv, lt, nb, a_wide, sg_wide, doa, name, side=None):
    s = qkv.shape[0]
    nq = s // TQ
    npair = SB_W // LANES
    _, prew, prexw = _sb_wide_consts()

    def body(q_ref, k_ref, v_ref, lt_ref, nb_ref, do_ref, prew_ref, prexw_ref, a_ref, sg_ref, *rest):
        side_in, (dq_ref, dk_out, dv_out), side_out, scratch = _side_refs(side, rest, 3)
        cp_ref, ce_ref, dqa_ref, dk_ref, dv_ref = scratch[:5]
        sems = scratch[5:]
        step_id = pl.program_id(1)
        if side is not None:
            @pl.when((pl.program_id(0) == 0) & (step_id == 0))
            def _():
                side.start(side_in, side_out, *sems)
        lane = lax.broadcasted_iota(jnp.int32, (TQ, LANES), 1)
        rel = _sb_rel()
        blocks = [step_id * SB_QB + b for b in range(SB_QB)]
        rows = [slice(b * TQ, (b + 1) * TQ) for b in range(SB_QB)]
        qs = [_pair_stack(q_ref[rows[b], :] * SCALE, lane) for b in range(SB_QB)]
        dos = [_pair_stack(do_ref[rows[b], :], lane) for b in range(SB_QB)]
        n_blocks = [jnp.clip(jnp.max(nb_ref[b]).astype(jnp.int32), 1, i + 1) for b, i in enumerate(blocks)]
        first = [i + 1 - n for i, n in zip(blocks, n_blocks)]

        @pl.when(step_id == 0)
        def _():
            dk_ref[...] = jnp.zeros_like(dk_ref)
            dv_ref[...] = jnp.zeros_like(dv_ref)

        straight = n_blocks[0] == SB_STRAIGHT
        for n in n_blocks[1:]:
            straight = straight & (n == SB_STRAIGHT)

        @pl.when(straight)
        def _():
            for b in range(SB_QB):
                w0 = pl.multiple_of(first[b] * BK, BK)
                kw = k_ref[pl.ds(w0, SB_WIDE), :]
                vw = v_ref[pl.ds(w0, SB_WIDE), :]
                a = a_ref[b]
                e = a.astype(F32) * _dot_nt(dos[b], vw)
                big_e, _ = _wide_sums(e, prexw_ref[...], False, two_pass=False)
                dz = (e - sg_ref[b].astype(F32) * (e + big_e)).astype(BF16)
                dk_ref[pl.ds(w0, SB_WIDE), :] += _dot_tn(dz, qs[b])
                dv_ref[pl.ds(w0, SB_WIDE), :] += _dot_tn(a, dos[b])
                dqa_ref[b] = _dot(dz, kw)

        @pl.when(jnp.logical_not(straight))
        def _():
            cp_ref[...] = jnp.zeros_like(cp_ref)
            ce_ref[...] = jnp.zeros_like(ce_ref)
            dqa_ref[...] = jnp.zeros_like(dqa_ref)
            for b, i in enumerate(blocks):
                q0 = i * TQ

                def step(it, carry, b=b, q0=q0):
                    k0 = pl.multiple_of((first[b] + it) * BK, BK)
                    k = k_ref[pl.ds(k0, BK), :]
                    v = v_ref[pl.ds(k0, BK), :]
                    mask = rel < (q0 - k0)
                    lb, lm = _sb_scores(qs[b], k, mask)
                    cs = _split_dot(lm, prew_ref[...])
                    a = jnp.where(mask, jnp.exp(lb + (lt_ref[b] - (cs[:, :BK] + cp_ref[b]))), 0.0)
                    e = a * _dot_nt(dos[b], v)
                    ce = _split_dot(e, prexw_ref[...], two_pass=False)
                    big_e = ce[:, :BK] + ce_ref[b]
                    dz = jnp.where(mask, e - jnp.exp(lb) * (e + big_e), 0.0).astype(BF16)
                    dk_ref[pl.ds(k0, BK), :] += _dot_tn(dz, qs[b])
                    dv_ref[pl.ds(k0, BK), :] += _dot_tn(a.astype(BF16), dos[b])
                    dqa_ref[b] += _dot(dz, k)
                    cp_ref[b] += cs[:, BK:]
                    ce_ref[b] += ce[:, BK:]
                    return carry

                lax.fori_loop(0, n_blocks[b], step, 0)

        for b in range(SB_QB):
            dq = jnp.where(lane < HEAD_DIM, dqa_ref[b, :TQ, :], dqa_ref[b, TQ:, :])
            dq_ref[rows[b], :] = (dq * SCALE).astype(BF16)

        @pl.when(step_id == nq // SB_QB - 1)
        def _():
            dk_out[...] = dk_ref[...].astype(BF16)
            dv_out[...] = dv_ref[...].astype(BF16)

        if side is not None:
            @pl.when((pl.program_id(0) == npair - 1) & (step_id == nq // SB_QB - 1))
            def _():
                side.finish(side_in, side_out, *sems)

    s_in, s_shape, s_out, s_alias, s_sems = _side_specs(side, 10, 3)
    wide_spec = pl.BlockSpec((None, SB_QB, 2 * TQ, SB_WIDE), lambda j, i: (j, i, 0, 0))
    outs = pl.pallas_call(
        body,
        out_shape=[jax.ShapeDtypeStruct((s, SB_W), BF16)] * 3 + s_shape,
        grid=(npair, nq // SB_QB),
        in_specs=[pl.BlockSpec((SB_QB * TQ, LANES), lambda j, i: (i, j)),
                  pl.BlockSpec((s, LANES), lambda j, i: (0, npair + j)),
                  pl.BlockSpec((s, LANES), lambda j, i: (0, 2 * npair + j)),
                  pl.BlockSpec((None, SB_QB, 2 * TQ, BK), lambda j, i: (j, i, 0, 0)),
                  pl.BlockSpec((None, SB_QB, 8, LANES), lambda j, i: (j, i, 0, 0)),
                  pl.BlockSpec((SB_QB * TQ, LANES), lambda j, i: (i, j)),
                  pl.BlockSpec(prew.shape, lambda j, i: (0, 0)),
                  pl.BlockSpec(prexw.shape, lambda j, i: (0, 0)), wide_spec, wide_spec] + s_in,
        out_specs=[pl.BlockSpec((SB_QB * TQ, LANES), lambda j, i: (i, j)),
                   pl.BlockSpec((s, LANES), lambda j, i: (0, j)),
                   pl.BlockSpec((s, LANES), lambda j, i: (0, j))] + s_out,
        input_output_aliases=s_alias,
        scratch_shapes=[pltpu.VMEM((SB_QB, 2 * TQ, BK), F32), pltpu.VMEM((SB_QB, 2 * TQ, BK), F32),
                        pltpu.VMEM((SB_QB, 2 * TQ, LANES), F32), pltpu.VMEM((s, LANES), F32),
                        pltpu.VMEM((s, LANES), F32)] + s_sems,
        compiler_params=_params(2), name=name)(qkv, qkv, qkv, lt, nb, doa, prew, prexw, a_wide, sg_wide,
                                               *(side.arrays if side else ()))
    return outs[0], outs[1], outs[2], outs[3:]


def _bucket_table():
    i = np.arange(TQ)[:, None]
    j = np.arange(2 * BK)[None, :]
    dist = np.maximum(TQ + i - j, 0)
    max_exact = N_BUCKETS // 2
    df = np.maximum(dist, 1).astype(np.float32)
    large = max_exact + (np.log(df / np.float32(max_exact)) / np.float32(math.log(MAX_DISTANCE / max_exact))
                         * np.float32(N_BUCKETS - max_exact)).astype(np.int32)
    large = np.minimum(large, N_BUCKETS - 1)
    return np.where(dist < max_exact, dist, large).astype(np.int32)


def _swa_align_in(t, lane, g):
    tf = t.astype(F32)
    tr = pltpu.roll(tf, HEAD_DIM, 1)
    gmask = (lane >= HEAD_DIM) == (g == 1)
    top = jnp.where(gmask, jnp.where(g == 0, tf, tr), 0.0)
    bot = jnp.where(gmask, jnp.where(g == 1, tf, tr), 0.0)
    return jnp.concatenate([top, bot], axis=0).astype(BF16)


def _swa_align_out(t, lane, g):
    top, bot = t[:TQ, :], t[TQ:, :]
    top = jnp.where(g == 0, top, pltpu.roll(top, HEAD_DIM, 1))
    bot = jnp.where(g == 1, bot, pltpu.roll(bot, HEAD_DIM, 1))
    return jnp.where(lane < HEAD_DIM, top, bot)


def _swa_bias(bias_ref, bucket_ref, rb_ref, j):
    dist = TQ + lax.broadcasted_iota(jnp.int32, (TQ, 2 * BK), 0) - lax.broadcasted_iota(jnp.int32, (TQ, 2 * BK), 1)
    window = (dist >= 0) & (dist < WINDOW)
    for hh in range(2):
        def add(b, acc):
            return acc + jnp.where(bucket_ref[...] == b, rb_ref[b, 2 * j + hh], 0.0)
        bias = lax.fori_loop(0, N_BUCKETS, add, jnp.zeros((TQ, 2 * BK), F32))
        bias_ref[hh * TQ:(hh + 1) * TQ, :] = jnp.where(window, bias, NEG)


def _swa_probs(qs, k2, bias, own_block, sink_ref, i, j):
    s = _dot_nt(qs, k2) + bias
    s = jnp.where(own_block | (i > 0), s, NEG)
    row1 = lax.broadcasted_iota(jnp.int32, (2 * TQ, 1), 0)
    sink = jnp.where(row1 < TQ, sink_ref[2 * j], sink_ref[2 * j + 1])
    m = jnp.maximum(jnp.max(s, axis=1, keepdims=True), sink)
    e = jnp.exp(s - m)
    es = jnp.exp(sink - m)
    inv = 1.0 / (jnp.sum(e, axis=1, keepdims=True) + es)
    return e * inv, es * inv


def _swa_kv(ref, i):
    prev = pl.multiple_of(jnp.maximum(i - 1, 0) * BK, BK)
    cur = pl.multiple_of(i * BK, BK)
    return jnp.concatenate([ref[pl.ds(prev, BK), :], ref[pl.ds(cur, BK), :]], axis=0), prev, cur


def _swa_fwd(qkv, sinks, rel_bias, name, side=None):
    s = qkv.shape[0]
    nq = s // TQ
    npair = SW_QW // LANES
    qcol = 3 * SB_W // LANES
    bucket = jnp.asarray(_bucket_table())

    def body(q_ref, k_ref, v_ref, bucket_ref, sink_ref, rb_ref, *rest):
        side_in, (o_ref, p_ref), side_out, scratch = _side_refs(side, rest, 2)
        bias_ref, sems = scratch[0], scratch[1:]
        j = pl.program_id(0)
        step = pl.program_id(1)
        if side is not None:
            @pl.when((j == 0) & (step == 0))
            def _():
                side.start(side_in, side_out, *sems)
        g = j // 2
        lane = lax.broadcasted_iota(jnp.int32, (TQ, LANES), 1)

        @pl.when(step == 0)
        def _():
            _swa_bias(bias_ref, bucket_ref, rb_ref, j)

        own_block = lax.broadcasted_iota(jnp.int32, (2 * TQ, 2 * BK), 1) >= BK
        for b in range(SWA_QB):
            i = step * SWA_QB + b
            rows = slice(b * TQ, (b + 1) * TQ)
            qs = _swa_align_in(q_ref[rows, :] * SCALE, lane, g)
            k2, _, _ = _swa_kv(k_ref, i)
            v2, _, _ = _swa_kv(v_ref, i)
            pr, psink = _swa_probs(qs, k2, bias_ref[...], own_block, sink_ref, i, j)
            prb = pr.astype(BF16)
            o_ref[rows, :] = _swa_align_out(_dot(prb, v2), lane, g).astype(BF16)
            p_ref[b, :, :2 * BK] = prb
            p_ref[b, :, 2 * BK:] = jnp.broadcast_to(psink, (2 * TQ, BK)).astype(BF16)
        if side is not None:
            @pl.when((j == npair - 1) & (step == nq // SWA_QB - 1))
            def _():
                side.finish(side_in, side_out, *sems)

    assert nq % SWA_QB == 0
    s_in, s_shape, s_out, s_alias, s_sems = _side_specs(side, 6, 2)
    outs = pl.pallas_call(
        body,
        out_shape=[jax.ShapeDtypeStruct((s, SW_QW), BF16),
                   jax.ShapeDtypeStruct((npair, nq, 2 * TQ, 3 * BK), BF16)] + s_shape,
        grid=(npair, nq // SWA_QB),
        in_specs=[pl.BlockSpec((SWA_QB * TQ, LANES), lambda j, i: (i, qcol + j)),
                  pl.BlockSpec((s, LANES), lambda j, i: (0, qcol + npair)),
                  pl.BlockSpec((s, LANES), lambda j, i: (0, qcol + npair + 1)),
                  pl.BlockSpec((TQ, 2 * BK), lambda j, i: (0, 0)),
                  pl.BlockSpec(memory_space=pltpu.SMEM),
                  pl.BlockSpec(memory_space=pltpu.SMEM)] + s_in,
        out_specs=[pl.BlockSpec((SWA_QB * TQ, LANES), lambda j, i: (i, j)),
                   pl.BlockSpec((None, SWA_QB, 2 * TQ, 3 * BK), lambda j, i: (j, i, 0, 0))] + s_out,
        input_output_aliases=s_alias,
        scratch_shapes=[pltpu.VMEM((2 * TQ, 2 * BK), F32)] + s_sems,
        compiler_params=_params(2), name=name)(qkv, qkv, qkv, bucket, sinks, rel_bias,
                                               *(side.arrays if side else ()))
    return outs[:2], outs[2:]


def _swa_bwd(qkv, ob, dob, probs, name, side=None):
    s = qkv.shape[0]
    nq = s // TQ
    npair = SW_QW // LANES
    qcol = 3 * SB_W // LANES
    bucket = jnp.asarray(_bucket_table())

    def body(q_ref, k_ref, v_ref, o_ref, do_ref, bucket_ref, p_ref, *rest):
        side_in, (dq_ref, dkv_ref, dsink_ref, drel_ref), side_out, scratch = _side_refs(side, rest, 4)
        dsacc_ref, dk_ref, dv_ref = scratch[:3]
        sems = scratch[3:]
        j = pl.program_id(0)
        step = pl.program_id(1)
        if side is not None:
            @pl.when((j == 0) & (step == 0))
            def _():
                side.start(side_in, side_out, *sems)
        g = j // 2
        lane = lax.broadcasted_iota(jnp.int32, (TQ, LANES), 1)
        row8 = lax.broadcasted_iota(jnp.int32, (SW_HEADS, LANES), 0)
        lane8 = lax.broadcasted_iota(jnp.int32, (SW_HEADS, LANES), 1)

        @pl.when((step == 0) & (j == 0))
        def _():
            dk_ref[...] = jnp.zeros_like(dk_ref)
            dv_ref[...] = jnp.zeros_like(dv_ref)
            dsink_ref[...] = jnp.zeros_like(dsink_ref)
            drel_ref[...] = jnp.zeros_like(drel_ref)

        @pl.when(step == 0)
        def _():
            dsacc_ref[...] = jnp.zeros_like(dsacc_ref)

        ds_sum = jnp.zeros(dsacc_ref.shape, F32)
        dsink = jnp.zeros((SW_HEADS, LANES), F32)
        for b in range(SWA_QB):
            i = step * SWA_QB + b
            rows = slice(b * TQ, (b + 1) * TQ)
            qs = _swa_align_in(q_ref[rows, :] * SCALE, lane, g)
            do = do_ref[rows, :]
            dos = _swa_align_in(do, lane, g)
            dof = do.astype(F32) * o_ref[rows, :].astype(F32)
            d0 = jnp.sum(jnp.where(lane < HEAD_DIM, dof, 0.0), axis=1, keepdims=True)
            d1 = jnp.sum(jnp.where(lane >= HEAD_DIM, dof, 0.0), axis=1, keepdims=True)
            delta = jnp.concatenate([d0, d1], axis=0)
            k2, prev, cur = _swa_kv(k_ref, i)
            v2, _, _ = _swa_kv(v_ref, i)
            prb = p_ref[b, :, :2 * BK]
            ds = prb.astype(F32) * (_dot_nt(dos, v2) - delta)
            ds_sum = ds_sum + ds
            sd = p_ref[b, :, 2 * BK:].astype(F32) * delta
            ds0 = -jnp.sum(sd[:TQ, :], axis=0, keepdims=True)
            ds1 = -jnp.sum(sd[TQ:, :], axis=0, keepdims=True)
            dsink = dsink + jnp.where(row8 == 2 * j, ds0, jnp.where(row8 == 2 * j + 1, ds1, 0.0))
            dsb = ds.astype(BF16)
            dq_ref[rows, :] = _swa_align_out(_dot(dsb, k2) * SCALE, lane, g).astype(BF16)
            dk2 = _dot_tn(dsb, qs)
            dv2 = _dot_tn(prb, dos)
            dk_ref[pl.ds(prev, BK), :] += dk2[:BK, :]
            dk_ref[pl.ds(cur, BK), :] += dk2[BK:, :]
            dv_ref[pl.ds(prev, BK), :] += dv2[:BK, :]
            dv_ref[pl.ds(cur, BK), :] += dv2[BK:, :]
        dsacc_ref[...] += ds_sum
        dsink_ref[...] += dsink

        @pl.when(step == nq // SWA_QB - 1)
        def _():
            for hh in range(2):
                def red(b, acc):
                    val = _sum_all(jnp.where(bucket_ref[...] == b, dsacc_ref[hh * TQ:(hh + 1) * TQ, :], 0.0))
                    return jnp.where((row8 == 2 * j + hh) & (lane8 == b), val, acc)
                drel_ref[...] += lax.fori_loop(0, N_BUCKETS, red, jnp.zeros((SW_HEADS, LANES), F32))

        @pl.when((step == nq // SWA_QB - 1) & (j == npair - 1))
        def _():
            dkv_ref[:, :LANES] = dk_ref[...].astype(BF16)
            dkv_ref[:, LANES:] = dv_ref[...].astype(BF16)
            if side is not None:
                side.finish(side_in, side_out, *sems)

    whole = lambda j, i: (0, 0)
    s_in, s_shape, s_out, s_alias, s_sems = _side_specs(side, 7, 4)
    outs = pl.pallas_call(
        body,
        out_shape=[jax.ShapeDtypeStruct((s, SW_QW), BF16), jax.ShapeDtypeStruct((s, 2 * LANES), BF16),
                   jax.ShapeDtypeStruct((SW_HEADS, LANES), F32), jax.ShapeDtypeStruct((SW_HEADS, LANES), F32)] + s_shape,
        grid=(npair, nq // SWA_QB),
        in_specs=[pl.BlockSpec((SWA_QB * TQ, LANES), lambda j, i: (i, qcol + j)),
                  pl.BlockSpec((s, LANES), lambda j, i: (0, qcol + npair)),
                  pl.BlockSpec((s, LANES), lambda j, i: (0, qcol + npair + 1)),
                  pl.BlockSpec((SWA_QB * TQ, LANES), lambda j, i: (i, j)),
                  pl.BlockSpec((SWA_QB * TQ, LANES), lambda j, i: (i, j)),
                  pl.BlockSpec((TQ, 2 * BK), whole),
                  pl.BlockSpec((None, SWA_QB, 2 * TQ, 3 * BK), lambda j, i: (j, i, 0, 0))] + s_in,
        out_specs=[pl.BlockSpec((SWA_QB * TQ, LANES), lambda j, i: (i, j)),
                   pl.BlockSpec((s, 2 * LANES), whole),
                   pl.BlockSpec((SW_HEADS, LANES), whole), pl.BlockSpec((SW_HEADS, LANES), whole)] + s_out,
        input_output_aliases=s_alias,
        scratch_shapes=[pltpu.VMEM((2 * TQ, 2 * BK), F32), pltpu.VMEM((s, LANES), F32),
                        pltpu.VMEM((s, LANES), F32)] + s_sems,
        compiler_params=_params(2), name=name)(qkv, qkv, qkv, ob, dob, bucket, probs,
                                               *(side.arrays if side else ()))
    return outs[:4], outs[4:]


def _acc_init(i, *refs):
    @pl.when(i == 0)
    def _():
        for r in refs:
            r[...] = jnp.zeros_like(r)


def _loss_bwd(x3, target, g, name):
    d = x3.shape[1]

    def body(x_ref, t_ref, g_ref, dx_ref, dg_ref, loss_ref):
        _acc_init(pl.program_id(0), dg_ref, loss_ref)
        x = x_ref[...]
        gv = g_ref[...]
        diff = _rms(x, gv) - t_ref[...]
        loss_ref[...] += 0.5 * jnp.sum(jnp.mean(jnp.square(diff), axis=-1, keepdims=True), axis=0, keepdims=True)
        dx, dg = _rms_bwd(diff * (1.0 / d), x, gv)
        dx_ref[...] = dx
        dg_ref[...] += dg

    return _rowcall(name, body, [x3, target], [g], [(d, F32)], [((1, d), F32), ((1, LANES), F32)])


def _ple_bwd(dx3, pe, gt, x2, g, wpg, name):
    d = x2.shape[1]

    def body(dx3_ref, pe_ref, gt_ref, x_ref, g_ref, w_ref, dpe_ref, dgt_ref, dx2_ref, dg_ref):
        _acc_init(pl.program_id(0), dg_ref)
        dx3 = dx3_ref[...]
        sg = _sigmoid(gt_ref[...])
        dpe_ref[...] = (dx3 * sg).astype(BF16)
        dgt = (dx3 * pe_ref[...] * sg * (1.0 - sg)).astype(BF16)
        dgt_ref[...] = dgt
        dx, dg = _rms_bwd(_dot_nt(dgt, w_ref[...]), x_ref[...], g_ref[...])
        dx2_ref[...] = dx3 + dx
        dg_ref[...] += dg

    return _rowcall(name, body, [dx3, pe, gt, x2], [g, wpg], [(d, BF16), (d, BF16), (d, F32)], [((1, d), F32)])


def _ff2_bwd(dx2, r, w2, name):
    d = dx2.shape[1]
    dff = r.shape[1]

    def body(dx_ref, r_ref, w_ref, du_ref, dxb_ref):
        dxb = dx_ref[...].astype(BF16)
        dxb_ref[...] = dxb
        du_ref[...] = (_dot_nt(dxb, w_ref[...]) * (2.0 * r_ref[...].astype(F32))).astype(BF16)

    return _rowcall(name, body, [dx2, r], [w2], [(dff, BF16), (d, BF16)])


def _ff1_bwd(du, dx2, x1, g, w1, name, side=None):
    d = x1.shape[1]

    def body(du_ref, dx2_ref, x_ref, g_ref, w_ref, dx1_ref, dx1b_ref, dg_ref):
        _acc_init(pl.program_id(0), dg_ref)
        dx, dg = _rms_bwd(_dot_cols_t(du_ref[...], w_ref), x_ref[...], g_ref[...])
        dx1 = dx2_ref[...] + dx
        dx1_ref[...] = dx1
        dx1b_ref[...] = dx1.astype(BF16)
        dg_ref[...] += dg

    return _rowcall(name, body, [du, dx2, x1], [g, w1], [(d, F32), (d, BF16)], [((1, d), F32)], side=side)


def _mixer_bwd(dx1b, gates, oa, ob, wo, wua, wub, name, side=None):
    d = dx1b.shape[1]

    def body(dx_ref, gate_ref, oa_ref, ob_ref, wo_ref, wua_ref, wub_ref,
             dya_ref, dyb_ref, dgate_ref, doa_ref, dob_ref):
        dm = _dot_nt(dx_ref[...], wo_ref[...])
        sa = _sigmoid(gate_ref[:, :d])
        sb = _sigmoid(gate_ref[:, d:])
        ya = _dot_cols(oa_ref[...], wua_ref)
        yb = _dot_cols(ob_ref[...], wub_ref)
        dya = (dm * sa).astype(BF16)
        dyb = (dm * sb).astype(BF16)
        dya_ref[...] = dya
        dyb_ref[...] = dyb
        dgate_ref[:, :d] = (dm * ya * sa * (1.0 - sa)).astype(BF16)
        dgate_ref[:, d:] = (dm * yb * sb * (1.0 - sb)).astype(BF16)
        doa_ref[...] = _dot_cols_t(dya, wua_ref).astype(BF16)
        dob_ref[...] = _dot_cols_t(dyb, wub_ref).astype(BF16)

    return _rowcall(name, body, [dx1b, gates, oa, ob], [wo, wua, wub],
                    [(d, BF16), (d, BF16), (2 * d, BF16), (SB_W, BF16), (SW_QW, BF16)], side=side)


def _inproj_bwd(pieces, dx1, x, g, wt, name, side=None):
    d = x.shape[1]
    n = len(pieces)
    offsets = [sum(pc.shape[1] for pc in pieces[:p]) for p in range(n + 1)]

    def body(*refs):
        dx1_ref, x_ref, g_ref, w_ref, dx_ref, dg_ref = refs[n:]
        _acc_init(pl.program_id(0), dg_ref)
        dh = _dot(refs[0][...], w_ref[:offsets[1], :])
        for p in range(1, n):
            dh = dh + _dot(refs[p][...], w_ref[offsets[p]:offsets[p + 1], :])
        dx, dg = _rms_bwd(dh, x_ref[...], g_ref[...])
        dx_ref[...] = dx1_ref[...] + dx
        dg_ref[...] += dg

    return _rowcall(name, body, list(pieces) + [dx1, x], [g, wt], [(d, F32)], [((1, d), F32)], side=side)


def _tile(n, cap):
    assert n % LANES == 0
    return max(t for t in range(LANES, min(n, cap) + 1, LANES) if n % t == 0)


def _mm_tn(a, b, name, nshard=1):
    s, ka = a.shape
    nb = b.shape[1]
    n = nb // nshard
    ta = _tile(ka, 512)
    tb = _tile(n, 1024)
    per = n // tb

    def body(a_ref, b_ref, o_ref):
        o_ref[...] = _dot_tn(a_ref[...].astype(BF16), b_ref[...].astype(BF16))

    return pl.pallas_call(
        body, out_shape=jax.ShapeDtypeStruct((nshard, ka, n), F32), grid=(nb // tb, ka // ta),
        in_specs=[pl.BlockSpec((s, ta), lambda jb, ia: (0, ia)), pl.BlockSpec((s, tb), lambda jb, ia: (0, jb))],
        out_specs=pl.BlockSpec((None, ta, tb), lambda jb, ia: (jb // per, ia, jb % per)),
        compiler_params=_params(2), name=name)(a, b)


def _mm_tn_pieces(pieces, b, name, side=None):
    s, nb = b.shape
    ta = 256
    n_in = len(pieces) + 1
    tiles = [pc.shape[1] // ta for pc in pieces]
    assert all(pc.shape[1] % ta == 0 for pc in pieces)
    starts = [sum(tiles[:p]) for p in range(len(pieces))]
    tb = _tile(nb, 1024)
    grid = (nb // tb, sum(tiles))

    def body(*refs):
        a_refs, b_ref = refs[:n_in - 1], refs[n_in - 1]
        side_in, (o_ref,), side_out, sems = _side_refs(side, refs[n_in:], 1)
        jb, ia = pl.program_id(0), pl.program_id(1)
        if side is not None:
            @pl.when((jb == 0) & (ia == 0))
            def _():
                side.start(side_in, side_out, *sems)
        for p in range(len(pieces)):
            @pl.when((ia >= starts[p]) & (ia < starts[p] + tiles[p]))
            def _(p=p):
                o_ref[...] = _dot_tn(a_refs[p][...], b_ref[...])
        if side is not None:
            @pl.when((jb == grid[0] - 1) & (ia == grid[1] - 1))
            def _():
                side.finish(side_in, side_out, *sems)

    def piece_spec(p):
        return pl.BlockSpec((s, ta), lambda jb, ia: (0, jnp.clip(ia - starts[p], 0, tiles[p] - 1)))

    s_in, s_shape, s_out, s_alias, s_sems = _side_specs(side, n_in, 1)
    outs = pl.pallas_call(
        body, out_shape=[jax.ShapeDtypeStruct((sum(tiles) * ta, nb), F32)] + s_shape, grid=grid,
        in_specs=[piece_spec(p) for p in range(len(pieces))] + [pl.BlockSpec((s, tb), lambda jb, ia: (0, jb))] + s_in,
        out_specs=[pl.BlockSpec((ta, tb), lambda jb, ia: (ia, jb))] + s_out,
        input_output_aliases=s_alias, scratch_shapes=s_sems,
        compiler_params=_params(2), name=name)(*pieces, b, *(side.arrays if side else ()))
    return outs[0] if side is None else (outs[0], outs[1:])


def _place():
    return lax.axis_index("x"), lax.axis_index("y"), lax.axis_index("c")


def _chip_peer(x, y, k):
    return (x ^ (k >> 1), y ^ (k & 1))


def _row_tile(k, cap=544):
    return max(t for t in range(32, min(k, cap) + 1, 32) if k % t == 0)


def _cast_bf16(w, r, name):
    l, k, n = w.shape
    assert l == 2
    tk = _row_tile(k)

    def body(r_ref, w_ref, o0_ref, o1_ref):
        o0_ref[...] = w_ref[0].astype(BF16)
        o1_ref[...] = w_ref[1].astype(BF16)

    out_spec = pl.BlockSpec((None, tk, n), lambda i, r_ref: (r_ref[0], i, 0))
    return pl.pallas_call(
        body, out_shape=[jax.ShapeDtypeStruct((4, k, n), BF16)] * 2,
        grid_spec=pltpu.PrefetchScalarGridSpec(
            num_scalar_prefetch=1, grid=(k // tk,),
            in_specs=[pl.BlockSpec((l, tk, n), lambda i, r_ref: (0, i, 0))],
            out_specs=[out_spec, out_spec]),
        compiler_params=_params(1), name=name)(r, w)


class _Exchange(NamedTuple):
    arrays: tuple
    aliased: tuple
    sems: tuple
    start: Callable
    finish: Callable


def _both(a, b):
    na, ma, ka = len(a.arrays), len(a.aliased), 2 * len(a.sems)

    def start(ins, outs, *sems):
        a.start(ins[:na], outs[:ma], *sems[:ka])
        b.start(ins[na:], outs[ma:], *sems[ka:])

    def finish(ins, outs, *sems):
        a.finish(ins[:na], outs[:ma], *sems[:ka])
        b.finish(ins[na:], outs[ma:], *sems[ka:])

    return _Exchange(a.arrays + b.arrays, a.aliased + tuple(na + i for i in b.aliased), a.sems + b.sems, start, finish)


def _all_gather(bufs):
    nt = len(bufs)

    def ici(t, ins, outs, send_sems, recv_sems, x, y, c, r, k):
        return pltpu.make_async_remote_copy(
            src_ref=ins[t].at[r, c], dst_ref=outs[t].at[r, c], send_sem=send_sems.at[t, k - 1],
            recv_sem=recv_sems.at[t, k - 1], device_id=(*_chip_peer(x, y, k), c), device_id_type=MESH)

    def d2d(t, outs, send_sems, recv_sems, x, y, c, r, k, half):
        slab = outs[t].at[r ^ k, half]
        return pltpu.make_async_remote_copy(
            src_ref=slab, dst_ref=slab, send_sem=send_sems.at[t, 2 + k], recv_sem=recv_sems.at[t, 2 + k],
            device_id=(x, y, 1 - c), device_id_type=MESH)

    def start(ins, outs, send_sems, recv_sems):
        x, y, c = _place()
        r = 2 * x + y
        for t in range(nt):
            for k in (1, 2, 3):
                ici(t, ins, outs, send_sems, recv_sems, x, y, c, r, k).start()

    def finish(ins, outs, send_sems, recv_sems):
        x, y, c = _place()
        r = 2 * x + y
        for t in range(nt):
            for k in (1, 2, 3):
                slab = outs[t].at[r ^ k, c]
                pltpu.make_async_remote_copy(
                    src_ref=slab, dst_ref=slab, send_sem=send_sems.at[t, k - 1], recv_sem=recv_sems.at[t, k - 1],
                    device_id=(x, y, 1 - c), device_id_type=MESH).wait_recv()
                d2d(t, outs, send_sems, recv_sems, x, y, c, r, k, c).start()
        for t in range(nt):
            for k in (1, 2, 3):
                d2d(t, outs, send_sems, recv_sems, x, y, c, r, k, 1 - c).wait_recv()
        for t in range(nt):
            for k in (1, 2, 3):
                ici(t, ins, outs, send_sems, recv_sems, x, y, c, r, k).wait_send()
                d2d(t, outs, send_sems, recv_sems, x, y, c, r, k, c).wait_send()

    return _Exchange(tuple(bufs), tuple(range(nt)), ((nt, 6),), start, finish)


def _run_exchange(name, ex):
    n_in, n_out = len(ex.arrays), len(ex.aliased)

    def body(*refs):
        ins, outs = refs[:n_in], refs[n_in:n_in + n_out]
        send_sems, recv_sems = refs[n_in + n_out:]
        ex.start(ins, outs, send_sems, recv_sems)
        ex.finish(ins, outs, send_sems, recv_sems)

    any_spec = pl.BlockSpec(memory_space=pl.ANY)
    return pl.pallas_call(
        body, out_shape=[jax.ShapeDtypeStruct(ex.arrays[a].shape, ex.arrays[a].dtype) for a in ex.aliased],
        in_specs=[any_spec] * n_in, out_specs=[any_spec] * n_out,
        input_output_aliases={a: o for o, a in enumerate(ex.aliased)},
        scratch_shapes=[pltpu.SemaphoreType.DMA(shape) for shape in ex.sems for _ in range(2)],
        name=name)(*ex.arrays)


def _rs_to_sibling(grads):
    nt = len(grads)
    landing = [lax.empty((4,) + g.shape[2:], F32) for g in grads]

    def copies(ins, outs, send_sems, recv_sems):
        x, y, c = _place()
        return [pltpu.make_async_remote_copy(
            src_ref=ins[t].at[:, 1 - c], dst_ref=outs[t], send_sem=send_sems.at[t], recv_sem=recv_sems.at[t],
            device_id=(x, y, 1 - c), device_id_type=MESH) for t in range(nt)]

    def start(ins, outs, send_sems, recv_sems):
        for cp in copies(ins, outs, send_sems, recv_sems):
            cp.start()

    def finish(ins, outs, send_sems, recv_sems):
        for cp in copies(ins, outs, send_sems, recv_sems):
            cp.wait()

    return _Exchange(tuple(grads) + tuple(landing), tuple(range(nt, 2 * nt)), ((nt,),), start, finish)


def _add_half(g, recv, cr, name):
    _, _, k2, n = g.shape
    tk = _row_tile(k2)

    def body(cr_ref, g_ref, r_ref, sums_ref, mine_ref):
        val = (g_ref[...] + r_ref[...]).astype(BF16)
        sums_ref[...] = val

        @pl.when(pl.program_id(1) == cr_ref[1])
        def _():
            mine_ref[...] = val

    return pl.pallas_call(
        body, out_shape=[jax.ShapeDtypeStruct((4, k2, n), BF16)] * 2,
        grid_spec=pltpu.PrefetchScalarGridSpec(
            num_scalar_prefetch=1, grid=(k2 // tk, 4),
            in_specs=[pl.BlockSpec((None, None, tk, n), lambda i, q, cr_ref: (q, cr_ref[0], i, 0)),
                      pl.BlockSpec((None, tk, n), lambda i, q, cr_ref: (q, i, 0))],
            out_specs=[pl.BlockSpec((None, tk, n), lambda i, q, cr_ref: (q, i, 0)),
                       pl.BlockSpec((None, tk, n), lambda i, q, cr_ref: (cr_ref[1], i, 0))]),
        compiler_params=_params(2), name=name)(cr, g, recv)


def _rs_to_chips(sums, parts):
    nt = len(sums)

    def copies(ins, outs, send_sems, recv_sems):
        x, y, c = _place()
        r = 2 * x + y
        return [pltpu.make_async_remote_copy(
            src_ref=ins[t].at[r ^ k], dst_ref=outs[t].at[r], send_sem=send_sems.at[t, k - 1],
            recv_sem=recv_sems.at[t, k - 1], device_id=(*_chip_peer(x, y, k), c), device_id_type=MESH)
            for t in range(nt) for k in (1, 2, 3)]

    def start(ins, outs, send_sems, recv_sems):
        for cp in copies(ins, outs, send_sems, recv_sems):
            cp.start()

    def finish(ins, outs, send_sems, recv_sems):
        for cp in copies(ins, outs, send_sems, recv_sems):
            cp.wait()

    return _Exchange(tuple(sums) + tuple(parts), tuple(range(nt, 2 * nt)), ((nt, 3),), start, finish)


def _sum4(parts, cr, name):
    _, k2, n = parts.shape
    tk = _row_tile(k2)

    def body(cr_ref, p_ref, o_ref):
        p = p_ref[...].astype(F32)
        o_ref[...] = ((p[0] + p[1]) + p[2]) + p[3]

    return pl.pallas_call(
        body, out_shape=jax.ShapeDtypeStruct((2, k2, n), F32),
        grid_spec=pltpu.PrefetchScalarGridSpec(
            num_scalar_prefetch=1, grid=(k2 // tk,),
            in_specs=[pl.BlockSpec((4, tk, n), lambda i, cr_ref: (0, i, 0))],
            out_specs=pl.BlockSpec((None, tk, n), lambda i, cr_ref: (cr_ref[0], i, 0))),
        compiler_params=_params(1), name=name)(cr, parts)


def _exchange_halves(both):
    nt = len(both)

    def copies(ins, outs, send_sems, recv_sems):
        x, y, c = _place()
        return [pltpu.make_async_remote_copy(
            src_ref=ins[t].at[c], dst_ref=outs[t].at[c], send_sem=send_sems.at[t], recv_sem=recv_sems.at[t],
            device_id=(x, y, 1 - c), device_id_type=MESH) for t in range(nt)]

    def start(ins, outs, send_sems, recv_sems):
        for cp in copies(ins, outs, send_sems, recv_sems):
            cp.start()

    def finish(ins, outs, send_sems, recv_sems):
        for cp in copies(ins, outs, send_sems, recv_sems):
            cp.wait()

    return _Exchange(tuple(both), tuple(range(nt)), ((nt,),), start, finish)


def _adamw_math(w, g, m, v):
    m = ADAM_B1 * m + (1.0 - ADAM_B1) * g
    v = ADAM_B2 * v + (1.0 - ADAM_B2) * jnp.square(g)
    m_hat = m / (1.0 - ADAM_B1 ** ADAM_STEP)
    v_hat = v / (1.0 - ADAM_B2 ** ADAM_STEP)
    delta = -ADAM_LR * (m_hat / (jnp.sqrt(v_hat) + ADAM_EPS) + ADAM_WD * w)
    return delta, m, v


def _adamw(w, m, v, g0, g1, name, side=None):
    _, k, n = w.shape
    tk = _row_tile(k)
    nk = k // tk

    def body(w_ref, m_ref, v_ref, g0_ref, g1_ref, *rest):
        side_in, (grad_ref, delta_ref, nm_ref, nv_ref), side_out, sems = _side_refs(side, rest, 4)
        layer, i = pl.program_id(0), pl.program_id(1)
        if side is not None:
            @pl.when((layer == 0) & (i == 0))
            def _():
                side.start(side_in, side_out, *sems)
        g = jnp.where(layer == 0, g0_ref[...], g1_ref[...])
        delta, nm, nv = _adamw_math(w_ref[...], g, m_ref[...], v_ref[...])
        grad_ref[...] = g
        delta_ref[...] = delta
        nm_ref[...] = nm
        nv_ref[...] = nv
        if side is not None:
            @pl.when((layer == 1) & (i == nk - 1))
            def _():
                side.finish(side_in, side_out, *sems)

    lay = pl.BlockSpec((None, tk, n), lambda a, i: (a, i, 0))
    g0_spec = pl.BlockSpec((tk, n), lambda a, i: (jnp.where(a == 0, i, nk - 1), 0))
    g1_spec = pl.BlockSpec((tk, n), lambda a, i: (jnp.where(a == 1, i, 0), 0))
    s_in, s_shape, s_out, s_alias, s_sems = _side_specs(side, 5, 4)
    outs = pl.pallas_call(
        body, out_shape=[jax.ShapeDtypeStruct(w.shape, F32)] * 4 + s_shape, grid=(2, nk),
        in_specs=[lay, lay, lay, g0_spec, g1_spec] + s_in, out_specs=[lay] * 4 + s_out,
        input_output_aliases=s_alias, scratch_shapes=s_sems,
        compiler_params=_params(2), name=name)(w, m, v, g0, g1, *(side.arrays if side else ()))
    return outs[:4] if side is None else (outs[:4], outs[4:])


def _small_allreduce_adamw(gpart, w, m, v):
    shape = gpart.shape

    def body(g_ref, w_ref, m_ref, v_ref, gsum_ref, delta_ref, nm_ref, nv_ref, recv_ref, send_sems, recv_sems):
        x, y, c = _place()
        me = 4 * x + 2 * y + c
        recv_ref[me] = g_ref[...]
        cps = []
        for k in range(1, 8):
            peer = (x ^ (k >> 2), y ^ ((k >> 1) & 1), c ^ (k & 1))
            cp = pltpu.make_async_remote_copy(
                src_ref=g_ref, dst_ref=recv_ref.at[me], send_sem=send_sems.at[k - 1], recv_sem=recv_sems.at[k - 1],
                device_id=peer, device_id_type=MESH)
            cp.start()
            cps.append(cp)
        for cp in cps:
            cp.wait()
        g = recv_ref[0]
        for dev in range(1, 8):
            g = g + recv_ref[dev]
        delta, nm, nv = _adamw_math(w_ref[...], g, m_ref[...], v_ref[...])
        gsum_ref[...] = g
        delta_ref[...] = delta
        nm_ref[...] = nm
        nv_ref[...] = nv

    vm = pl.BlockSpec(memory_space=pltpu.VMEM)
    return pl.pallas_call(
        body, out_shape=[jax.ShapeDtypeStruct(shape, F32)] * 4, in_specs=[vm] * 4, out_specs=[vm] * 4,
        scratch_shapes=[pltpu.VMEM((8,) + shape, F32), pltpu.SemaphoreType.DMA((7,)), pltpu.SemaphoreType.DMA((7,))],
        name="small_allreduce_adamw")(gpart, w, m, v)


BIG = ("w_in", "w_up_a", "w_up_b", "w_o", "w_ff1", "w_ff2", "w_pe", "w_pg")
COL_SHARDED = ("w_in", "w_up_a", "w_up_b", "w_ff1", "w_pe")
ROW_SHARDED = ("w_o", "w_ff2", "w_pg")
SMALL_ROWS = 16


def _pack_small(g_mix, g_mlp, g_pe, g_final, sinks, rel_bias, loss=None):
    d = g_final.shape[0]
    row = lambda v: jnp.pad(v.reshape(1, -1), ((0, 0), (0, d - v.size)))
    rows = [g_mix, g_mlp, g_pe, g_final.reshape(1, d),
            jnp.zeros((1, d), F32) if loss is None else row(loss), row(sinks), row(rel_bias)]
    out = jnp.concatenate(rows, axis=0)
    return jnp.pad(out, ((0, SMALL_ROWS - out.shape[0]), (0, 0)))


def _unpack_small(a, sinks_shape, rel_shape):
    return (a[0:2], a[2:4], a[4:6], a[6], a[8, :sinks_shape[0] * sinks_shape[1]].reshape(sinks_shape),
            a[9, :rel_shape[0] * rel_shape[1]].reshape(rel_shape))


def kernel(x, p, w_in, w_up_a, w_up_b, w_o, w_ff1, w_ff2, w_pe, w_pg, g_mix, g_mlp, g_pe, g_final, sinks, rel_bias, loss_target, m_w_in, m_w_up_a, m_w_up_b, m_w_o, m_w_ff1, m_w_ff2, m_w_pe, m_w_pg, m_g_mix, m_g_mlp, m_g_pe, m_g_final, m_sinks, m_rel_bias, v_w_in, v_w_up_a, v_w_up_b, v_w_o, v_w_ff1, v_w_ff2, v_w_pe, v_w_pg, v_g_mix, v_g_mlp, v_g_pe, v_g_final, v_sinks, v_rel_bias):
    depth = w_in.shape[0]
    assert depth == 2
    x0 = x[0]
    target = loss_target[0]
    d = x0.shape[1]
    wl = dict(w_in=w_in, w_up_a=w_up_a, w_up_b=w_up_b, w_o=w_o, w_ff1=w_ff1, w_ff2=w_ff2, w_pe=w_pe, w_pg=w_pg)
    ml = dict(w_in=m_w_in, w_up_a=m_w_up_a, w_up_b=m_w_up_b, w_o=m_w_o, w_ff1=m_w_ff1, w_ff2=m_w_ff2, w_pe=m_w_pe, w_pg=m_w_pg)
    vl = dict(w_in=v_w_in, w_up_a=v_w_up_a, w_up_b=v_w_up_b, w_o=v_w_o, w_ff1=v_w_ff1, w_ff2=v_w_ff2, w_pe=v_w_pe, w_pg=v_w_pg)
    c_idx = lax.axis_index("c").astype(jnp.int32)
    r_idx = (2 * lax.axis_index("x") + lax.axis_index("y")).astype(jnp.int32)
    cr = jnp.stack([c_idx, r_idx])

    wl["w_in"], ml["w_in"], vl["w_in"] = (jnp.swapaxes(a, 1, 2) for a in (w_in, m_w_in, v_w_in))

    bufs = {}
    for n in BIG:
        k, nn = wl[n].shape[1:]
        for l, b in enumerate(_cast_bf16(wl[n], r_idx.reshape(1), "cast_" + n)):
            bufs[n, l] = b.reshape(4, 2, k // 2, nn)

    def gather(keys, run):
        for key, b in zip(keys, run(_all_gather([bufs[key] for key in keys]))):
            bufs[key] = b

    def gathered(n, l):
        _, _, k2, nn = bufs[n, l].shape
        if n in ROW_SHARDED or n == "w_in":
            return bufs[n, l].reshape(8 * k2, nn)
        return bufs[n, l].reshape(4, 2 * k2, nn)

    gather([("w_in", 0)], lambda ex: _run_exchange("all_gather_first", ex))

    full = {n: [None] * depth for n in BIG}
    saved = []
    xi = x0
    for i in range(depth):
        st = dict(x0=xi)
        gm = g_mix[i].reshape(1, d)
        full["w_in"][i] = gathered("w_in", i)
        st["h1"], st["qkv"], st["gates"] = _inproj_fwd(xi, gm, full["w_in"][i], f"inproj_fwd_{i}")

        def attend(ex):
            (st["oa"], st["lt"], st["nb"], st["a_wide"], st["sg_wide"]), filled = _sb_fwd(st["qkv"], f"sb_fwd_{i}", ex)
            return filled

        with_window = [("w_in", 1)] if i == 0 else [(n, i) for n in ("w_up_a", "w_up_b", "w_o")]
        gather([(n, i) for n in BIG if n != "w_in" and (n, i) not in with_window], attend)

        def window(ex):
            (st["ob"], st["probs"]), filled = _swa_fwd(st["qkv"], sinks[i], rel_bias, f"swa_fwd_{i}", ex)
            return filled

        gather(with_window, window)
        for n in BIG:
            if n != "w_in":
                full[n][i] = gathered(n, i)
        st["m"], st["x1"] = _mixer_fwd(st["oa"], st["ob"], st["gates"], xi, full["w_up_a"][i], full["w_up_b"][i],
                                       full["w_o"][i], f"mixer_fwd_{i}")
        st["h2"], st["u"], st["a"] = _ff1_fwd(st["x1"], g_mlp[i].reshape(1, d), full["w_ff1"][i], f"ff1_fwd_{i}")
        st["x2"] = _ff2_fwd(st["a"], st["x1"], full["w_ff2"][i], f"ff2_fwd_{i}")
        st["pb"], st["h3"], st["pe"], st["gt"], xi = _ple_fwd(p[i, 0], st["x2"], g_pe[i].reshape(1, d),
                                                            full["w_pe"][i], full["w_pg"][i], f"ple_fwd_{i}")
        saved.append(st)

    dx, dg_final, loss_part = _loss_bwd(xi, target, g_final.reshape(1, d), "loss_bwd")
    gw = {n: [None] * depth for n in BIG}
    reduced = {}

    chip_sums = {}

    def by_halves(keys):
        tensors = []
        for n, l in keys:
            g = gw[n][l]
            if n in ROW_SHARDED:
                ka, nb = g.shape[1:]
                g = g.reshape(4, ka // 4, nb)
            _, k, nn = g.shape
            tensors.append(g.reshape(4, 2, k // 2, nn))
        return tensors

    def add_halves(keys, tensors, landed):
        for (n, l), g, r in zip(keys, tensors, landed):
            chip_sums[n, l] = _add_half(g, r, cr, f"add_half_{n}_{l}")

    def to_sibling(keys, run):
        tensors = by_halves(keys)
        add_halves(keys, tensors, run(_rs_to_sibling(tensors)))

    def to_chips(keys):
        return _rs_to_chips([chip_sums[k][0] for k in keys], [chip_sums[k][1] for k in keys])

    halves = {}

    def sum_chips(keys, parts):
        for (n, l), pc in zip(keys, parts):
            halves[n, l] = _sum4(pc, cr, f"sum4_{n}_{l}")

    def swap_halves(keys):
        def store(filled):
            for key, both in zip(keys, filled):
                reduced[key] = both
        return _exchange_halves([halves[k] for k in keys]), store

    outs = {}

    def adamw(n, ex=None):
        g0, g1 = (reduced[n, l].reshape(wl[n].shape[1:]) for l in range(depth))
        if ex is None:
            outs[n] = _adamw(wl[n], ml[n], vl[n], g0, g1, "adamw_" + n)
            return None
        outs[n], filled = _adamw(wl[n], ml[n], vl[n], g0, g1, "adamw_" + n, ex)
        return filled

    dg_mix, dg_mlp, dg_pe, dsinks = [None] * depth, [None] * depth, [None] * depth, [None] * depth
    drel = jnp.zeros((SW_HEADS, LANES), F32)
    for i in reversed(range(depth)):
        st = saved[i]
        dpe, dgt, dx2, dg_pe[i] = _ple_bwd(dx, st["pe"], st["gt"], st["x2"], g_pe[i].reshape(1, d),
                                           full["w_pg"][i], f"ple_bwd_{i}")
        gw["w_pe"][i] = _mm_tn(st["pb"], dpe, f"dw_pe_{i}", 4)
        gw["w_pg"][i] = _mm_tn(st["h3"], dgt, f"dw_pg_{i}")
        du, dx2b = _ff2_bwd(dx2, st["u"], full["w_ff2"][i], f"ff2_bwd_{i}")
        gw["w_ff2"][i] = _mm_tn(st["a"], dx2b, f"dw_ff2_{i}")
        gw["w_ff1"][i] = _mm_tn(st["h2"], du, f"dw_ff1_{i}", 4)
        if i == 0:
            (dx1, dx1b, dg_mlp[i]), parts = _ff1_bwd(du, dx2, st["x1"], g_mlp[i].reshape(1, d), full["w_ff1"][i],
                                                    f"ff1_bwd_{i}", to_chips([("w_in", 1)]))
            sum_chips([("w_in", 1)], parts)
        else:
            dx1, dx1b, dg_mlp[i] = _ff1_bwd(du, dx2, st["x1"], g_mlp[i].reshape(1, d), full["w_ff1"][i],
                                            f"ff1_bwd_{i}")
        gw["w_o"][i] = _mm_tn(st["m"], dx1b, f"dw_o_{i}")
        early = [(n, i) for n in ("w_pe", "w_pg", "w_ff2", "w_ff1", "w_o")]

        def mixer(ex):
            (dya, dyb, dgates, doa, dob), landed = _mixer_bwd(
                dx1b, st["gates"], st["oa"], st["ob"], full["w_o"][i], full["w_up_a"][i], full["w_up_b"][i],
                f"mixer_bwd_{i}", ex)
            st.update(dya=dya, dyb=dyb, dgates=dgates, doa=doa, dob=dob)
            return landed

        to_sibling(early, mixer)
        dgates = st["dgates"]
        gw["w_up_a"][i] = _mm_tn(st["oa"], st["dya"], f"dw_up_a_{i}", 4)
        gw["w_up_b"][i] = _mm_tn(st["ob"], st["dyb"], f"dw_up_b_{i}", 4)
        late = [("w_up_a", i), ("w_up_b", i)]
        late_halves = by_halves(late)
        keys = early + late
        with_sb = [(n, i) for n in ("w_ff1", "w_o", "w_pg", "w_pe")]
        with_swa = [(n, i) for n in ("w_ff2", "w_up_a", "w_up_b")]
        dqa, dka, dva, filled = _sb_bwd(st["qkv"], st["lt"], st["nb"], st["a_wide"], st["sg_wide"], st["doa"],
                                        f"sb_bwd_{i}", _both(_rs_to_sibling(late_halves), to_chips(with_sb)))
        add_halves(late, late_halves, filled[:len(late)])
        sum_chips(with_sb, filled[len(late):])
        (dqb, dkvb, dsk, drl), parts = _swa_bwd(st["qkv"], st["ob"], st["dob"], st["probs"], f"swa_bwd_{i}",
                                                to_chips(with_swa))
        sum_chips(with_swa, parts)
        dsinks[i] = dsk[:, 0]
        drel = drel + drl
        dproj = [dqa, dka, dva, dqb, dkvb, dgates]
        swap, store = swap_halves(keys + ([("w_in", 1)] if i == 0 else []))
        dw_in_t, filled = _mm_tn_pieces(dproj, st["h1"], f"dw_in_{i}", swap)
        store(filled)
        gw["w_in"][i] = dw_in_t.reshape(4, dw_in_t.shape[0] // 4, d)
        if i == 1:
            def inproj(ex):
                (dx, dg_mix[i]), landed = _inproj_bwd(dproj, dx1, st["x0"], g_mix[i].reshape(1, d),
                                                      full["w_in"][i], f"inproj_bwd_{i}", ex)
                st["dx"] = dx
                return landed

            to_sibling([("w_in", 1)], inproj)
            dx = st["dx"]
        else:
            keys = [("w_in", 0)]
            to_sibling(keys, lambda ex: adamw("w_ff1", ex))
            (dx, dg_mix[i]), parts = _inproj_bwd(dproj, dx1, st["x0"], g_mix[i].reshape(1, d),
                                                 full["w_in"][i], f"inproj_bwd_{i}", to_chips(keys))
            sum_chips(keys, parts)
            swap, store = swap_halves(keys)
            store(adamw("w_ff2", swap))
    grad_x = dx[None]

    for n in BIG:
        if n not in outs:
            adamw(n)
    outs["w_in"] = [jnp.swapaxes(a, 1, 2) for a in outs["w_in"]]

    drel_bias = drel[:, :N_BUCKETS].T
    gsmall = _pack_small(jnp.concatenate(dg_mix, 0), jnp.concatenate(dg_mlp, 0), jnp.concatenate(dg_pe, 0),
                         dg_final[0], jnp.stack(dsinks), drel_bias, loss_part[0, :1])
    wsmall = _pack_small(g_mix, g_mlp, g_pe, g_final, sinks, rel_bias)
    msmall = _pack_small(m_g_mix, m_g_mlp, m_g_pe, m_g_final, m_sinks, m_rel_bias)
    vsmall = _pack_small(v_g_mix, v_g_mlp, v_g_pe, v_g_final, v_sinks, v_rel_bias)
    small = _small_allreduce_adamw(gsmall, wsmall, msmall, vsmall)
    loss = small[0][7, 0]
    small = [_unpack_small(a, sinks.shape, rel_bias.shape) for a in small]

    result = [loss, grad_x]
    for kind in range(4):
        result += [outs[n][kind] for n in BIG]
        result += list(small[kind])
    return tuple(result)
```

```python
import functools
import math
from typing import Callable, NamedTuple

import numpy as np
import jax
import jax.numpy as jnp
from jax import lax
from jax.experimental import pallas as pl
from jax.experimental.pallas import tpu as pltpu

F32 = jnp.float32
BF16 = jnp.bfloat16
MESH = pl.DeviceIdType.MESH

HEAD_DIM = 64
SB_HEADS = 8
SW_HEADS = 8
SW_KV_HEADS = 2
WINDOW = 128
N_BUCKETS = 32
MAX_DISTANCE = 128
EPS = 1e-6
SB_W = SB_HEADS * HEAD_DIM
SW_QW = SW_HEADS * HEAD_DIM
SW_KVW = SW_KV_HEADS * HEAD_DIM
QKV_W = 3 * SB_W + SW_QW + 2 * SW_KVW
SCALE = HEAD_DIM ** -0.5
assert SCALE == 0.125
LANES = 128
TQ = 128
BK = 128
NEG = -1e30
SB_EXHAUSTED = -106.0

ADAM_LR = 0.001
ADAM_B1 = 0.9
ADAM_B2 = 0.999
ADAM_EPS = 1e-08
ADAM_WD = 0.01
ADAM_STEP = 10

VMEM_LIMIT = 56 * 1024 * 1024


def _dot(a, b):
    return jnp.dot(a, b, preferred_element_type=F32)


def _dot_nt(a, b):
    return lax.dot_general(a, b, (((1,), (1,)), ((), ())), preferred_element_type=F32)


def _dot_tn(a, b):
    return lax.dot_general(a, b, (((0,), (0,)), ((), ())), preferred_element_type=F32)


def _sum_all(x):
    return jnp.sum(jnp.sum(x, axis=1, keepdims=True), axis=0, keepdims=True)


def _sigmoid(x):
    return 1.0 / (1.0 + jnp.exp(-x))


def _rms(x, g):
    r = lax.rsqrt(jnp.mean(x * x, axis=-1, keepdims=True) + EPS)
    return (x * r) * g


def _rms_bwd(dy, x, g):
    r = lax.rsqrt(jnp.mean(x * x, axis=-1, keepdims=True) + EPS)
    n = x * r
    dg = jnp.sum(dy * n, axis=0, keepdims=True)
    dn = dy * g
    dx = r * (dn - n * jnp.mean(dn * n, axis=-1, keepdims=True))
    return dx, dg


def _params(n_axes):
    return pltpu.CompilerParams(dimension_semantics=("arbitrary",) * n_axes, vmem_limit_bytes=VMEM_LIMIT)


def _rowcall(name, body, row_ins, const_ins, row_outs, acc_outs=(), tm=512, side=None):
    s = row_ins[0].shape[0]
    assert s % tm == 0
    in_specs = [pl.BlockSpec((tm, a.shape[1]), lambda i: (i, 0)) for a in row_ins]
    in_specs += [pl.BlockSpec(a.shape, functools.partial(lambda i, nd: (0,) * nd, nd=a.ndim)) for a in const_ins]
    out_shape = [jax.ShapeDtypeStruct((s, c), dt) for c, dt in row_outs]
    out_specs = [pl.BlockSpec((tm, c), lambda i: (i, 0)) for c, _ in row_outs]
    out_shape += [jax.ShapeDtypeStruct(sh, dt) for sh, dt in acc_outs]
    out_specs += [pl.BlockSpec(sh, functools.partial(lambda i, nd: (0,) * nd, nd=len(sh))) for sh, _ in acc_outs]
    if side is None:
        return pl.pallas_call(body, out_shape=out_shape, grid=(s // tm,), in_specs=in_specs, out_specs=out_specs,
                              compiler_params=_params(1), name=name)(*row_ins, *const_ins)
    n_in, n_out = len(in_specs), len(out_specs)

    def with_side(*refs):
        side_in, outs, side_out, sems = _side_refs(side, refs[n_in:], n_out)

        @pl.when(pl.program_id(0) == 0)
        def _():
            side.start(side_in, side_out, *sems)

        body(*refs[:n_in], *outs)

        @pl.when(pl.program_id(0) == s // tm - 1)
        def _():
            side.finish(side_in, side_out, *sems)

    s_in, s_shape, s_out, s_alias, s_sems = _side_specs(side, n_in, n_out)
    outs = pl.pallas_call(with_side, out_shape=out_shape + s_shape, grid=(s // tm,), in_specs=in_specs + s_in,
                          out_specs=out_specs + s_out, input_output_aliases=s_alias, scratch_shapes=s_sems,
                          compiler_params=_params(1), name=name)(*row_ins, *const_ins, *side.arrays)
    return outs[:n_out], outs[n_out:]


def _dot_cols(a, w_ref):
    return jnp.concatenate([_dot(a, w_ref[r]) for r in range(w_ref.shape[0])], axis=1)


def _dot_cols_t(a, w_ref):
    n = w_ref.shape[2]
    out = _dot_nt(a[:, :n], w_ref[0])
    for r in range(1, w_ref.shape[0]):
        out = out + _dot_nt(a[:, r * n:(r + 1) * n], w_ref[r])
    return out


def _inproj_fwd(x, g, wt, name):
    d = x.shape[1]

    def body(x_ref, g_ref, w_ref, h_ref, qkv_ref, gate_ref):
        hb = _rms(x_ref[...], g_ref[...]).astype(BF16)
        h_ref[...] = hb
        qkv_ref[...] = _dot_nt(hb, w_ref[:QKV_W, :]).astype(BF16)
        gate_ref[...] = _dot_nt(hb, w_ref[QKV_W:, :])

    return _rowcall(name, body, [x], [g, wt], [(d, BF16), (QKV_W, BF16), (2 * d, F32)])


def _mixer_fwd(oa, ob, gates, x, wua, wub, wo, name):
    d = x.shape[1]

    def body(oa_ref, ob_ref, gate_ref, x_ref, wua_ref, wub_ref, wo_ref, m_ref, x1_ref):
        ya = _dot_cols(oa_ref[...], wua_ref)
        yb = _dot_cols(ob_ref[...], wub_ref)
        m = _sigmoid(gate_ref[:, :d]) * ya + _sigmoid(gate_ref[:, d:]) * yb
        mb = m.astype(BF16)
        m_ref[...] = mb
        x1_ref[...] = x_ref[...] + _dot(mb, wo_ref[...])

    return _rowcall(name, body, [oa, ob, gates, x], [wua, wub, wo], [(d, BF16), (d, F32)])


def _ff_fwd(x1, g, w1, w2, name):
    _, d, nq = w1.shape
    dff = 4 * nq

    def body(x_ref, g_ref, w1_ref, w2_ref, h_ref, r_ref, a_ref, o_ref):
        x = x_ref[...]
        hb = _rms(x, g_ref[...]).astype(BF16)
        h_ref[...] = hb
        r = jnp.maximum(_dot_cols(hb, w1_ref), 0.0)
        r_ref[...] = r.astype(BF16)
        a = jnp.square(r).astype(BF16)
        a_ref[...] = a
        o_ref[...] = x + _dot(a, w2_ref[...])

    return _rowcall(name, body, [x1], [g, w1, w2], [(d, BF16), (dff, BF16), (dff, BF16), (d, F32)], tm=256)


def _ple_fwd(p, x2, g, wpe, wpg, name):
    d = x2.shape[1]

    def body(p_ref, x_ref, g_ref, wpe_ref, wpg_ref, pb_ref, h_ref, pe_ref, gt_ref, x3_ref):
        pb = p_ref[...].astype(BF16)
        pb_ref[...] = pb
        pe = _dot_cols(pb, wpe_ref)
        x = x_ref[...]
        hb = _rms(x, g_ref[...]).astype(BF16)
        h_ref[...] = hb
        gt = _dot(hb, wpg_ref[...])
        pe_ref[...] = pe
        gt_ref[...] = gt
        x3_ref[...] = x + pe * _sigmoid(gt)

    return _rowcall(name, body, [p, x2], [g, wpe, wpg],
                    [(p.shape[1], BF16), (d, BF16), (d, F32), (d, F32), (d, F32)])


def _pair_stack(t, lane):
    zero = jnp.zeros_like(t)
    return jnp.concatenate([jnp.where(lane < HEAD_DIM, t, zero), jnp.where(lane >= HEAD_DIM, t, zero)], axis=0)


def _sb_rel():
    row = lax.broadcasted_iota(jnp.int32, (2 * TQ, BK), 0)
    row = jnp.where(row >= TQ, row - TQ, row)
    col = lax.broadcasted_iota(jnp.int32, (2 * TQ, BK), 1)
    return col - row


def _split_dot(x, m01, two_pass=True):
    hi = x.astype(BF16)
    if not two_pass:
        return _dot(hi, m01)
    lo = (x - hi.astype(F32)).astype(BF16)
    return _dot(hi, m01) + _dot(lo, m01)


def _sb_scores(qs, k, mask):
    z = _dot_nt(qs, k)
    lb = jnp.minimum(z, 0.0) - jnp.log(1.0 + jnp.exp(-jnp.abs(z)))
    lm = lb - z
    return lb, lm if mask is None else jnp.where(mask, lm, 0.0)


SB_STRAIGHT = 3
SB_WIDE = SB_STRAIGHT * BK
SB_QB = 2
SWA_QB = 4


def _sb_wide_consts():
    j = np.arange(BK)[:, None]
    s = np.arange(BK)[None, :]
    ones = np.ones((BK, BK), np.float32)
    as_bf16 = lambda m: jnp.asarray(np.concatenate([m, ones], axis=1).astype(np.float32), dtype=BF16)
    return as_bf16(j > s), as_bf16(j <= s), as_bf16(j < s)


def _wide_sums(x, m01, suffix, two_pass=True):
    parts = [_split_dot(x[:, b * BK:(b + 1) * BK], m01, two_pass) for b in range(SB_STRAIGHT)]
    order = range(SB_STRAIGHT - 1, -1, -1) if suffix else range(SB_STRAIGHT)
    out = [None] * SB_STRAIGHT
    carry = None
    for b in order:
        out[b] = parts[b][:, :BK] if carry is None else parts[b][:, :BK] + carry
        carry = parts[b][:, BK:] if carry is None else carry + parts[b][:, BK:]
    return jnp.concatenate(out, axis=1), carry


def _side_refs(ex, rest, n_out):
    n_in = len(ex.arrays) if ex else 0
    n_alias = len(ex.aliased) if ex else 0
    ins, rest = rest[:n_in], rest[n_in:]
    outs, rest = rest[:n_out], rest[n_out:]
    return ins, outs, rest[:n_alias], rest[n_alias:]


def _side_specs(ex, n_in, n_out):
    if ex is None:
        return [], [], [], {}, []
    any_spec = pl.BlockSpec(memory_space=pl.ANY)
    return ([any_spec] * len(ex.arrays),
            [jax.ShapeDtypeStruct(ex.arrays[a].shape, ex.arrays[a].dtype) for a in ex.aliased],
            [any_spec] * len(ex.aliased), {n_in + a: n_out + o for o, a in enumerate(ex.aliased)},
            [pltpu.SemaphoreType.DMA(shape) for shape in ex.sems for _ in range(2)])


def _sb_fwd(qkv, name, side=None):
    s = qkv.shape[0]
    nq = s // TQ
    npair = SB_W // LANES
    sufw = _sb_wide_consts()[0]

    def body(q_ref, k_ref, v_ref, sufw_ref, *rest):
        side_in, (o_ref, lt_ref, nb_ref, a_ref, sg_ref), side_out, scratch = _side_refs(side, rest, 5)
        cf_ref, acc_ref = scratch[:2]
        step_id = pl.program_id(1)
        if side is not None:
            @pl.when((pl.program_id(0) == 0) & (step_id == 0))
            def _():
                side.start(side_in, side_out, *scratch[2:])
        lane = lax.broadcasted_iota(jnp.int32, (TQ, LANES), 1)
        rel = _sb_rel()
        blocks = [step_id * SB_QB + b for b in range(SB_QB)]
        qs = [_pair_stack(q_ref[b * TQ:(b + 1) * TQ, :] * SCALE, lane) for b in range(SB_QB)]

        straight = blocks[0] >= SB_STRAIGHT - 1

        @pl.when(straight)
        def _():
            for b, i in enumerate(blocks):
                w0 = pl.multiple_of((i - (SB_STRAIGHT - 1)) * BK, BK)
                kw = k_ref[pl.ds(w0, SB_WIDE), :]
                lb, lm = _sb_scores(qs[b], kw, None)
                own = rel < 0
                past = SB_WIDE - BK
                lm = jnp.concatenate([lm[:, :past], jnp.where(own, lm[:, past:], 0.0)], axis=1)
                after, total = _wide_sums(lm, sufw_ref[...], True)
                on_past_keys = lambda t: jnp.concatenate([t[:, :past], jnp.where(own, t[:, past:], 0.0)], axis=1)
                a = on_past_keys(jnp.exp(lb + after)).astype(BF16)
                acc_ref[b] = _dot(a, v_ref[pl.ds(w0, SB_WIDE), :])
                cf_ref[b] = total
                a_ref[b] = a
                sg_ref[b] = on_past_keys(jnp.exp(lb)).astype(BF16)

        @pl.when(jnp.logical_not(straight))
        def _():
            cf_ref[...] = jnp.zeros_like(cf_ref)
            acc_ref[...] = jnp.zeros_like(acc_ref)

        for b, i in enumerate(blocks):
            q0 = i * TQ

            def more(c, i=i):
                return (c[0] <= i) & (c[1] > SB_EXHAUSTED)

            def step(c, b=b, i=i, q0=q0):
                k0 = pl.multiple_of((i - c[0]) * BK, BK)
                k = k_ref[pl.ds(k0, BK), :]
                v = v_ref[pl.ds(k0, BK), :]
                mask = rel < (q0 - k0)
                lb, lm = _sb_scores(qs[b], k, mask)
                cs = _split_dot(lm, sufw_ref[...])
                a = jnp.where(mask, jnp.exp(lb + cs[:, :BK] + cf_ref[b]), 0.0)
                acc_ref[b] += _dot(a.astype(BF16), v)
                cf = cf_ref[b] + cs[:, BK:]
                cf_ref[b] = cf
                return c[0] + 1, jnp.max(cf)

            n_blocks, _ = lax.while_loop(
                more, step, (jnp.where(straight, SB_STRAIGHT, 0).astype(jnp.int32), jnp.max(cf_ref[b])))
            o_ref[b * TQ:(b + 1) * TQ, :] = jnp.where(lane < HEAD_DIM, acc_ref[b, :TQ, :],
                                                     acc_ref[b, TQ:, :]).astype(BF16)
            lt_ref[b] = cf_ref[b]
            nb_ref[b] = jnp.full(nb_ref.shape[1:], n_blocks, F32)
        if side is not None:
            @pl.when((pl.program_id(0) == npair - 1) & (step_id == nq // SB_QB - 1))
            def _():
                side.finish(side_in, side_out, *scratch[2:])

    s_in, s_shape, s_out, s_alias, s_sems = _side_specs(side, 4, 5)
    wide = jax.ShapeDtypeStruct((npair, nq, 2 * TQ, SB_WIDE), BF16)
    wide_spec = pl.BlockSpec((None, SB_QB, 2 * TQ, SB_WIDE), lambda j, i: (j, i, 0, 0))
    outs = pl.pallas_call(
        body,
        out_shape=[jax.ShapeDtypeStruct((s, SB_W), BF16), jax.ShapeDtypeStruct((npair, nq, 2 * TQ, BK), F32),
                   jax.ShapeDtypeStruct((npair, nq, 8, LANES), F32), wide, wide] + s_shape,
        grid=(npair, nq // SB_QB),
        in_specs=[pl.BlockSpec((SB_QB * TQ, LANES), lambda j, i: (i, j)),
                  pl.BlockSpec((s, LANES), lambda j, i: (0, npair + j)),
                  pl.BlockSpec((s, LANES), lambda j, i: (0, 2 * npair + j)),
                  pl.BlockSpec(sufw.shape, lambda j, i: (0, 0))] + s_in,
        out_specs=[pl.BlockSpec((SB_QB * TQ, LANES), lambda j, i: (i, j)),
                   pl.BlockSpec((None, SB_QB, 2 * TQ, BK), lambda j, i: (j, i, 0, 0)),
                   pl.BlockSpec((None, SB_QB, 8, LANES), lambda j, i: (j, i, 0, 0)), wide_spec, wide_spec] + s_out,
        input_output_aliases=s_alias,
        scratch_shapes=[pltpu.VMEM((SB_QB, 2 * TQ, BK), F32), pltpu.VMEM((SB_QB, 2 * TQ, LANES), F32)] + s_sems,
        compiler_params=_params(2), name=name)(qkv, qkv, qkv, sufw, *(side.arrays if side else ()))
    return outs[:5], outs[5:]


def _sb_bwd(qkv, lt, nb, a_wide, sg_wide, doa, name, side=None):
    s = qkv.shape[0]
    nq = s // TQ
    npair = SB_W // LANES
    _, prew, prexw = _sb_wide_consts()

    def body(q_ref, k_ref, v_ref, lt_ref, nb_ref, do_ref, prew_ref, prexw_ref, a_ref, sg_ref, *rest):
        side_in, (dq_ref, dk_out, dv_out), side_out, scratch = _side_refs(side, rest, 3)
        cp_ref, ce_ref, dqa_ref, dk_ref, dv_ref = scratch[:5]
        sems = scratch[5:]
        step_id = pl.program_id(1)
        if side is not None:
            @pl.when((pl.program_id(0) == 0) & (step_id == 0))
            def _():
                side.start(side_in, side_out, *sems)
        lane = lax.broadcasted_iota(jnp.int32, (TQ, LANES), 1)
        rel = _sb_rel()
        blocks = [step_id * SB_QB + b for b in range(SB_QB)]
        rows = [slice(b * TQ, (b + 1) * TQ) for b in range(SB_QB)]
        qs = [_pair_stack(q_ref[rows[b], :] * SCALE, lane) for b in range(SB_QB)]
        dos = [_pair_stack(do_ref[rows[b], :], lane) for b in range(SB_QB)]
        n_blocks = [jnp.clip(jnp.max(nb_ref[b]).astype(jnp.int32), 1, i + 1) for b, i in enumerate(blocks)]
        first = [i + 1 - n for i, n in zip(blocks, n_blocks)]

        @pl.when(step_id == 0)
        def _():
            dk_ref[...] = jnp.zeros_like(dk_ref)
            dv_ref[...] = jnp.zeros_like(dv_ref)

        straight = n_blocks[0] == SB_STRAIGHT
        for n in n_blocks[1:]:
            straight = straight & (n == SB_STRAIGHT)

        @pl.when(straight)
        def _():
            for b in range(SB_QB):
                w0 = pl.multiple_of(first[b] * BK, BK)
                kw = k_ref[pl.ds(w0, SB_WIDE), :]
                vw = v_ref[pl.ds(w0, SB_WIDE), :]
                a = a_ref[b]
                e = a.astype(F32) * _dot_nt(dos[b], vw)
                big_e, _ = _wide_sums(e, prexw_ref[...], False, two_pass=False)
                dz = (e - sg_ref[b].astype(F32) * (e + big_e)).astype(BF16)
                dk_ref[pl.ds(w0, SB_WIDE), :] += _dot_tn(dz, qs[b])
                dv_ref[pl.ds(w0, SB_WIDE), :] += _dot_tn(a, dos[b])
                dqa_ref[b] = _dot(dz, kw)

        @pl.when(jnp.logical_not(straight))
        def _():
            cp_ref[...] = jnp.zeros_like(cp_ref)
            ce_ref[...] = jnp.zeros_like(ce_ref)
            dqa_ref[...] = jnp.zeros_like(dqa_ref)
            for b, i in enumerate(blocks):
                q0 = i * TQ

                def step(it, carry, b=b, q0=q0):
                    k0 = pl.multiple_of((first[b] + it) * BK, BK)
                    k = k_ref[pl.ds(k0, BK), :]
                    v = v_ref[pl.ds(k0, BK), :]
                    mask = rel < (q0 - k0)
                    lb, lm = _sb_scores(qs[b], k, mask)
                    cs = _split_dot(lm, prew_ref[...])
                    a = jnp.where(mask, jnp.exp(lb + (lt_ref[b] - (cs[:, :BK] + cp_ref[b]))), 0.0)
                    e = a * _dot_nt(dos[b], v)
                    ce = _split_dot(e, prexw_ref[...], two_pass=False)
                    big_e = ce[:, :BK] + ce_ref[b]
                    dz = jnp.where(mask, e - jnp.exp(lb) * (e + big_e), 0.0).astype(BF16)
                    dk_ref[pl.ds(k0, BK), :] += _dot_tn(dz, qs[b])
                    dv_ref[pl.ds(k0, BK), :] += _dot_tn(a.astype(BF16), dos[b])
                    dqa_ref[b] += _dot(dz, k)
                    cp_ref[b] += cs[:, BK:]
                    ce_ref[b] += ce[:, BK:]
                    return carry

                lax.fori_loop(0, n_blocks[b], step, 0)

        for b in range(SB_QB):
            dq = jnp.where(lane < HEAD_DIM, dqa_ref[b, :TQ, :], dqa_ref[b, TQ:, :])
            dq_ref[rows[b], :] = (dq * SCALE).astype(BF16)

        @pl.when(step_id == nq // SB_QB - 1)
        def _():
            dk_out[...] = dk_ref[...].astype(BF16)
            dv_out[...] = dv_ref[...].astype(BF16)

        if side is not None:
            @pl.when((pl.program_id(0) == npair - 1) & (step_id == nq // SB_QB - 1))
            def _():
                side.finish(side_in, side_out, *sems)

    s_in, s_shape, s_out, s_alias, s_sems = _side_specs(side, 10, 3)
    wide_spec = pl.BlockSpec((None, SB_QB, 2 * TQ, SB_WIDE), lambda j, i: (j, i, 0, 0))
    outs = pl.pallas_call(
        body,
        out_shape=[jax.ShapeDtypeStruct((s, SB_W), BF16)] * 3 + s_shape,
        grid=(npair, nq // SB_QB),
        in_specs=[pl.BlockSpec((SB_QB * TQ, LANES), lambda j, i: (i, j)),
                  pl.BlockSpec((s, LANES), lambda j, i: (0, npair + j)),
                  pl.BlockSpec((s, LANES), lambda j, i: (0, 2 * npair + j)),
                  pl.BlockSpec((None, SB_QB, 2 * TQ, BK), lambda j, i: (j, i, 0, 0)),
                  pl.BlockSpec((None, SB_QB, 8, LANES), lambda j, i: (j, i, 0, 0)),
                  pl.BlockSpec((SB_QB * TQ, LANES), lambda j, i: (i, j)),
                  pl.BlockSpec(prew.shape, lambda j, i: (0, 0)),
                  pl.BlockSpec(prexw.shape, lambda j, i: (0, 0)), wide_spec, wide_spec] + s_in,
        out_specs=[pl.BlockSpec((SB_QB * TQ, LANES), lambda j, i: (i, j)),
                   pl.BlockSpec((s, LANES), lambda j, i: (0, j)),
                   pl.BlockSpec((s, LANES), lambda j, i: (0, j))] + s_out,
        input_output_aliases=s_alias,
        scratch_shapes=[pltpu.VMEM((SB_QB, 2 * TQ, BK), F32), pltpu.VMEM((SB_QB, 2 * TQ, BK), F32),
                        pltpu.VMEM((SB_QB, 2 * TQ, LANES), F32), pltpu.VMEM((s, LANES), F32),
                        pltpu.VMEM((s, LANES), F32)] + s_sems,
        compiler_params=_params(2), name=name)(qkv, qkv, qkv, lt, nb, doa, prew, prexw, a_wide, sg_wide,
                                               *(side.arrays if side else ()))
    return outs[0], outs[1], outs[2], outs[3:]


def _bucket_table():
    i = np.arange(TQ)[:, None]
    j = np.arange(2 * BK)[None, :]
    dist = np.maximum(TQ + i - j, 0)
    max_exact = N_BUCKETS // 2
    df = np.maximum(dist, 1).astype(np.float32)
    large = max_exact + (np.log(df / np.float32(max_exact)) / np.float32(math.log(MAX_DISTANCE / max_exact))
                         * np.float32(N_BUCKETS - max_exact)).astype(np.int32)
    large = np.minimum(large, N_BUCKETS - 1)
    return np.where(dist < max_exact, dist, large).astype(np.int32)


def _swa_align_in(t, lane, g):
    tf = t.astype(F32)
    tr = pltpu.roll(tf, HEAD_DIM, 1)
    gmask = (lane >= HEAD_DIM) == (g == 1)
    top = jnp.where(gmask, jnp.where(g == 0, tf, tr), 0.0)
    bot = jnp.where(gmask, jnp.where(g == 1, tf, tr), 0.0)
    return jnp.concatenate([top, bot], axis=0).astype(BF16)


def _swa_align_out(t, lane, g):
    top, bot = t[:TQ, :], t[TQ:, :]
    top = jnp.where(g == 0, top, pltpu.roll(top, HEAD_DIM, 1))
    bot = jnp.where(g == 1, bot, pltpu.roll(bot, HEAD_DIM, 1))
    return jnp.where(lane < HEAD_DIM, top, bot)


def _swa_bias(bias_ref, bucket_ref, rb_ref, j):
    dist = TQ + lax.broadcasted_iota(jnp.int32, (TQ, 2 * BK), 0) - lax.broadcasted_iota(jnp.int32, (TQ, 2 * BK), 1)
    window = (dist >= 0) & (dist < WINDOW)
    for hh in range(2):
        def add(b, acc):
            return acc + jnp.where(bucket_ref[...] == b, rb_ref[b, 2 * j + hh], 0.0)
        bias = lax.fori_loop(0, N_BUCKETS, add, jnp.zeros((TQ, 2 * BK), F32))
        bias_ref[hh * TQ:(hh + 1) * TQ, :] = jnp.where(window, bias, NEG)


def _swa_probs(qs, k2, bias, own_block, sink_ref, i, j):
    s = _dot_nt(qs, k2) + bias
    s = jnp.where(own_block | (i > 0), s, NEG)
    row1 = lax.broadcasted_iota(jnp.int32, (2 * TQ, 1), 0)
    sink = jnp.where(row1 < TQ, sink_ref[2 * j], sink_ref[2 * j + 1])
    m = jnp.maximum(jnp.max(s, axis=1, keepdims=True), sink)
    e = jnp.exp(s - m)
    es = jnp.exp(sink - m)
    inv = 1.0 / (jnp.sum(e, axis=1, keepdims=True) + es)
    return e * inv, es * inv


def _swa_kv(ref, i):
    prev = pl.multiple_of(jnp.maximum(i - 1, 0) * BK, BK)
    cur = pl.multiple_of(i * BK, BK)
    return jnp.concatenate([ref[pl.ds(prev, BK), :], ref[pl.ds(cur, BK), :]], axis=0), prev, cur


def _swa_fwd(qkv, sinks, rel_bias, name, side=None):
    s = qkv.shape[0]
    nq = s // TQ
    npair = SW_QW // LANES
    qcol = 3 * SB_W // LANES
    bucket = jnp.asarray(_bucket_table())

    def body(q_ref, k_ref, v_ref, bucket_ref, sink_ref, rb_ref, *rest):
        side_in, (o_ref, p_ref), side_out, scratch = _side_refs(side, rest, 2)
        bias_ref, sems = scratch[0], scratch[1:]
        j = pl.program_id(0)
        step = pl.program_id(1)
        if side is not None:
            @pl.when((j == 0) & (step == 0))
            def _():
                side.start(side_in, side_out, *sems)
        g = j // 2
        lane = lax.broadcasted_iota(jnp.int32, (TQ, LANES), 1)

        @pl.when(step == 0)
        def _():
            _swa_bias(bias_ref, bucket_ref, rb_ref, j)

        own_block = lax.broadcasted_iota(jnp.int32, (2 * TQ, 2 * BK), 1) >= BK
        for b in range(SWA_QB):
            i = step * SWA_QB + b
            rows = slice(b * TQ, (b + 1) * TQ)
            qs = _swa_align_in(q_ref[rows, :] * SCALE, lane, g)
            k2, _, _ = _swa_kv(k_ref, i)
            v2, _, _ = _swa_kv(v_ref, i)
            pr, psink = _swa_probs(qs, k2, bias_ref[...], own_block, sink_ref, i, j)
            prb = pr.astype(BF16)
            o_ref[rows, :] = _swa_align_out(_dot(prb, v2), lane, g).astype(BF16)
            p_ref[b, :, :2 * BK] = prb
            p_ref[b, :, 2 * BK:] = jnp.broadcast_to(psink, (2 * TQ, BK)).astype(BF16)
        if side is not None:
            @pl.when((j == npair - 1) & (step == nq // SWA_QB - 1))
            def _():
                side.finish(side_in, side_out, *sems)

    assert nq % SWA_QB == 0
    s_in, s_shape, s_out, s_alias, s_sems = _side_specs(side, 6, 2)
    outs = pl.pallas_call(
        body,
        out_shape=[jax.ShapeDtypeStruct((s, SW_QW), BF16),
                   jax.ShapeDtypeStruct((npair, nq, 2 * TQ, 3 * BK), BF16)] + s_shape,
        grid=(npair, nq // SWA_QB),
        in_specs=[pl.BlockSpec((SWA_QB * TQ, LANES), lambda j, i: (i, qcol + j)),
                  pl.BlockSpec((s, LANES), lambda j, i: (0, qcol + npair)),
                  pl.BlockSpec((s, LANES), lambda j, i: (0, qcol + npair + 1)),
                  pl.BlockSpec((TQ, 2 * BK), lambda j, i: (0, 0)),
                  pl.BlockSpec(memory_space=pltpu.SMEM),
                  pl.BlockSpec(memory_space=pltpu.SMEM)] + s_in,
        out_specs=[pl.BlockSpec((SWA_QB * TQ, LANES), lambda j, i: (i, j)),
                   pl.BlockSpec((None, SWA_QB, 2 * TQ, 3 * BK), lambda j, i: (j, i, 0, 0))] + s_out,
        input_output_aliases=s_alias,
        scratch_shapes=[pltpu.VMEM((2 * TQ, 2 * BK), F32)] + s_sems,
        compiler_params=_params(2), name=name)(qkv, qkv, qkv, bucket, sinks, rel_bias,
                                               *(side.arrays if side else ()))
    return outs[:2], outs[2:]


def _swa_bwd(qkv, ob, dob, probs, name, side=None):
    s = qkv.shape[0]
    nq = s // TQ
    npair = SW_QW // LANES
    qcol = 3 * SB_W // LANES
    bucket = jnp.asarray(_bucket_table())

    def body(q_ref, k_ref, v_ref, o_ref, do_ref, bucket_ref, p_ref, *rest):
        side_in, (dq_ref, dkv_ref, dsink_ref, drel_ref), side_out, scratch = _side_refs(side, rest, 4)
        dsacc_ref, dk_ref, dv_ref = scratch[:3]
        sems = scratch[3:]
        j = pl.program_id(0)
        step = pl.program_id(1)
        if side is not None:
            @pl.when((j == 0) & (step == 0))
            def _():
                side.start(side_in, side_out, *sems)
        g = j // 2
        lane = lax.broadcasted_iota(jnp.int32, (TQ, LANES), 1)
        row8 = lax.broadcasted_iota(jnp.int32, (SW_HEADS, LANES), 0)
        lane8 = lax.broadcasted_iota(jnp.int32, (SW_HEADS, LANES), 1)

        @pl.when((step == 0) & (j == 0))
        def _():
            dk_ref[...] = jnp.zeros_like(dk_ref)
            dv_ref[...] = jnp.zeros_like(dv_ref)
            dsink_ref[...] = jnp.zeros_like(dsink_ref)
            drel_ref[...] = jnp.zeros_like(drel_ref)

        @pl.when(step == 0)
        def _():
            dsacc_ref[...] = jnp.zeros_like(dsacc_ref)

        ds_sum = jnp.zeros(dsacc_ref.shape, F32)
        dsink = jnp.zeros((SW_HEADS, LANES), F32)
        for b in range(SWA_QB):
            i = step * SWA_QB + b
            rows = slice(b * TQ, (b + 1) * TQ)
            qs = _swa_align_in(q_ref[rows, :] * SCALE, lane, g)
            do = do_ref[rows, :]
            dos = _swa_align_in(do, lane, g)
            dof = do.astype(F32) * o_ref[rows, :].astype(F32)
            d0 = jnp.sum(jnp.where(lane < HEAD_DIM, dof, 0.0), axis=1, keepdims=True)
            d1 = jnp.sum(jnp.where(lane >= HEAD_DIM, dof, 0.0), axis=1, keepdims=True)
            delta = jnp.concatenate([d0, d1], axis=0)
            k2, prev, cur = _swa_kv(k_ref, i)
            v2, _, _ = _swa_kv(v_ref, i)
            prb = p_ref[b, :, :2 * BK]
            ds = prb.astype(F32) * (_dot_nt(dos, v2) - delta)
            ds_sum = ds_sum + ds
            sd = p_ref[b, :, 2 * BK:].astype(F32) * delta
            ds0 = -jnp.sum(sd[:TQ, :], axis=0, keepdims=True)
            ds1 = -jnp.sum(sd[TQ:, :], axis=0, keepdims=True)
            dsink = dsink + jnp.where(row8 == 2 * j, ds0, jnp.where(row8 == 2 * j + 1, ds1, 0.0))
            dsb = ds.astype(BF16)
            dq_ref[rows, :] = _swa_align_out(_dot(dsb, k2) * SCALE, lane, g).astype(BF16)
            dk2 = _dot_tn(dsb, qs)
            dv2 = _dot_tn(prb, dos)
            dk_ref[pl.ds(prev, BK), :] += dk2[:BK, :]
            dk_ref[pl.ds(cur, BK), :] += dk2[BK:, :]
            dv_ref[pl.ds(prev, BK), :] += dv2[:BK, :]
            dv_ref[pl.ds(cur, BK), :] += dv2[BK:, :]
        dsacc_ref[...] += ds_sum
        dsink_ref[...] += dsink

        @pl.when(step == nq // SWA_QB - 1)
        def _():
            for hh in range(2):
                def red(b, acc):
                    val = _sum_all(jnp.where(bucket_ref[...] == b, dsacc_ref[hh * TQ:(hh + 1) * TQ, :], 0.0))
                    return jnp.where((row8 == 2 * j + hh) & (lane8 == b), val, acc)
                drel_ref[...] += lax.fori_loop(0, N_BUCKETS, red, jnp.zeros((SW_HEADS, LANES), F32))

        @pl.when((step == nq // SWA_QB - 1) & (j == npair - 1))
        def _():
            dkv_ref[:, :LANES] = dk_ref[...].astype(BF16)
            dkv_ref[:, LANES:] = dv_ref[...].astype(BF16)
            if side is not None:
                side.finish(side_in, side_out, *sems)

    whole = lambda j, i: (0, 0)
    s_in, s_shape, s_out, s_alias, s_sems = _side_specs(side, 7, 4)
    outs = pl.pallas_call(
        body,
        out_shape=[jax.ShapeDtypeStruct((s, SW_QW), BF16), jax.ShapeDtypeStruct((s, 2 * LANES), BF16),
                   jax.ShapeDtypeStruct((SW_HEADS, LANES), F32), jax.ShapeDtypeStruct((SW_HEADS, LANES), F32)] + s_shape,
        grid=(npair, nq // SWA_QB),
        in_specs=[pl.BlockSpec((SWA_QB * TQ, LANES), lambda j, i: (i, qcol + j)),
                  pl.BlockSpec((s, LANES), lambda j, i: (0, qcol + npair)),
                  pl.BlockSpec((s, LANES), lambda j, i: (0, qcol + npair + 1)),
                  pl.BlockSpec((SWA_QB * TQ, LANES), lambda j, i: (i, j)),
                  pl.BlockSpec((SWA_QB * TQ, LANES), lambda j, i: (i, j)),
                  pl.BlockSpec((TQ, 2 * BK), whole),
                  pl.BlockSpec((None, SWA_QB, 2 * TQ, 3 * BK), lambda j, i: (j, i, 0, 0))] + s_in,
        out_specs=[pl.BlockSpec((SWA_QB * TQ, LANES), lambda j, i: (i, j)),
                   pl.BlockSpec((s, 2 * LANES), whole),
                   pl.BlockSpec((SW_HEADS, LANES), whole), pl.BlockSpec((SW_HEADS, LANES), whole)] + s_out,
        input_output_aliases=s_alias,
        scratch_shapes=[pltpu.VMEM((2 * TQ, 2 * BK), F32), pltpu.VMEM((s, LANES), F32),
                        pltpu.VMEM((s, LANES), F32)] + s_sems,
        compiler_params=_params(2), name=name)(qkv, qkv, qkv, ob, dob, bucket, probs,
                                               *(side.arrays if side else ()))
    return outs[:4], outs[4:]


def _acc_init(i, *refs):
    @pl.when(i == 0)
    def _():
        for r in refs:
            r[...] = jnp.zeros_like(r)


def _loss_bwd(x3, target, g, name):
    d = x3.shape[1]

    def body(x_ref, t_ref, g_ref, dx_ref, dg_ref, loss_ref):
        _acc_init(pl.program_id(0), dg_ref, loss_ref)
        x = x_ref[...]
        gv = g_ref[...]
        diff = _rms(x, gv) - t_ref[...]
        loss_ref[...] += 0.5 * jnp.sum(jnp.mean(jnp.square(diff), axis=-1, keepdims=True), axis=0, keepdims=True)
        dx, dg = _rms_bwd(diff * (1.0 / d), x, gv)
        dx_ref[...] = dx
        dg_ref[...] += dg

    return _rowcall(name, body, [x3, target], [g], [(d, F32)], [((1, d), F32), ((1, LANES), F32)])


def _ple_bwd(dx3, pe, gt, x2, g, wpg, name):
    d = x2.shape[1]

    def body(dx3_ref, pe_ref, gt_ref, x_ref, g_ref, w_ref, dpe_ref, dgt_ref, dx2_ref, dg_ref):
        _acc_init(pl.program_id(0), dg_ref)
        dx3 = dx3_ref[...]
        sg = _sigmoid(gt_ref[...])
        dpe_ref[...] = (dx3 * sg).astype(BF16)
        dgt = (dx3 * pe_ref[...] * sg * (1.0 - sg)).astype(BF16)
        dgt_ref[...] = dgt
        dx, dg = _rms_bwd(_dot_nt(dgt, w_ref[...]), x_ref[...], g_ref[...])
        dx2_ref[...] = dx3 + dx
        dg_ref[...] += dg

    return _rowcall(name, body, [dx3, pe, gt, x2], [g, wpg], [(d, BF16), (d, BF16), (d, F32)], [((1, d), F32)])


def _ff2_bwd(dx2, r, w2, name):
    d = dx2.shape[1]
    dff = r.shape[1]

    def body(dx_ref, r_ref, w_ref, du_ref, dxb_ref):
        dxb = dx_ref[...].astype(BF16)
        dxb_ref[...] = dxb
        du_ref[...] = (_dot_nt(dxb, w_ref[...]) * (2.0 * r_ref[...].astype(F32))).astype(BF16)

    return _rowcall(name, body, [dx2, r], [w2], [(dff, BF16), (d, BF16)])


def _ff1_bwd(du, dx2, x1, g, w1, name, side=None):
    d = x1.shape[1]

    def body(du_ref, dx2_ref, x_ref, g_ref, w_ref, dx1_ref, dx1b_ref, dg_ref):
        _acc_init(pl.program_id(0), dg_ref)
        dx, dg = _rms_bwd(_dot_cols_t(du_ref[...], w_ref), x_ref[...], g_ref[...])
        dx1 = dx2_ref[...] + dx
        dx1_ref[...] = dx1
        dx1b_ref[...] = dx1.astype(BF16)
        dg_ref[...] += dg

    return _rowcall(name, body, [du, dx2, x1], [g, w1], [(d, F32), (d, BF16)], [((1, d), F32)], side=side)


def _mixer_bwd(dx1b, gates, oa, ob, wo, wua, wub, name, side=None):
    d = dx1b.shape[1]

    def body(dx_ref, gate_ref, oa_ref, ob_ref, wo_ref, wua_ref, wub_ref,
             dya_ref, dyb_ref, dgate_ref, doa_ref, dob_ref):
        dm = _dot_nt(dx_ref[...], wo_ref[...])
        sa = _sigmoid(gate_ref[:, :d])
        sb = _sigmoid(gate_ref[:, d:])
        ya = _dot_cols(oa_ref[...], wua_ref)
        yb = _dot_cols(ob_ref[...], wub_ref)
        dya = (dm * sa).astype(BF16)
        dyb = (dm * sb).astype(BF16)
        dya_ref[...] = dya
        dyb_ref[...] = dyb
        dgate_ref[:, :d] = (dm * ya * sa * (1.0 - sa)).astype(BF16)
        dgate_ref[:, d:] = (dm * yb * sb * (1.0 - sb)).astype(BF16)
        doa_ref[...] = _dot_cols_t(dya, wua_ref).astype(BF16)
        dob_ref[...] = _dot_cols_t(dyb, wub_ref).astype(BF16)

    return _rowcall(name, body, [dx1b, gates, oa, ob], [wo, wua, wub],
                    [(d, BF16), (d, BF16), (2 * d, BF16), (SB_W, BF16), (SW_QW, BF16)], side=side)


def _inproj_bwd(pieces, dx1, x, g, wt, name, side=None):
    d = x.shape[1]
    n = len(pieces)
    offsets = [sum(pc.shape[1] for pc in pieces[:p]) for p in range(n + 1)]

    def body(*refs):
        dx1_ref, x_ref, g_ref, w_ref, dx_ref, dg_ref = refs[n:]
        _acc_init(pl.program_id(0), dg_ref)
        dh = _dot(refs[0][...], w_ref[:offsets[1], :])
        for p in range(1, n):
            dh = dh + _dot(refs[p][...], w_ref[offsets[p]:offsets[p + 1], :])
        dx, dg = _rms_bwd(dh, x_ref[...], g_ref[...])
        dx_ref[...] = dx1_ref[...] + dx
        dg_ref[...] += dg

    return _rowcall(name, body, list(pieces) + [dx1, x], [g, wt], [(d, F32)], [((1, d), F32)], side=side)


def _tile(n, cap):
    assert n % LANES == 0
    return max(t for t in range(LANES, min(n, cap) + 1, LANES) if n % t == 0)


def _mm_tn(a, b, name, nshard=1):
    s, ka = a.shape
    nb = b.shape[1]
    n = nb // nshard
    ta = _tile(ka, 512)
    tb = _tile(n, 1024)
    per = n // tb

    def body(a_ref, b_ref, o_ref):
        o_ref[...] = _dot_tn(a_ref[...].astype(BF16), b_ref[...].astype(BF16))

    return pl.pallas_call(
        body, out_shape=jax.ShapeDtypeStruct((nshard, ka, n), F32), grid=(nb // tb, ka // ta),
        in_specs=[pl.BlockSpec((s, ta), lambda jb, ia: (0, ia)), pl.BlockSpec((s, tb), lambda jb, ia: (0, jb))],
        out_specs=pl.BlockSpec((None, ta, tb), lambda jb, ia: (jb // per, ia, jb % per)),
        compiler_params=_params(2), name=name)(a, b)


def _mm_tn_pieces(pieces, b, name, side=None):
    s, nb = b.shape
    ta = 256
    n_in = len(pieces) + 1
    tiles = [pc.shape[1] // ta for pc in pieces]
    assert all(pc.shape[1] % ta == 0 for pc in pieces)
    starts = [sum(tiles[:p]) for p in range(len(pieces))]
    tb = _tile(nb, 1024)
    grid = (nb // tb, sum(tiles))

    def body(*refs):
        a_refs, b_ref = refs[:n_in - 1], refs[n_in - 1]
        side_in, (o_ref,), side_out, sems = _side_refs(side, refs[n_in:], 1)
        jb, ia = pl.program_id(0), pl.program_id(1)
        if side is not None:
            @pl.when((jb == 0) & (ia == 0))
            def _():
                side.start(side_in, side_out, *sems)
        for p in range(len(pieces)):
            @pl.when((ia >= starts[p]) & (ia < starts[p] + tiles[p]))
            def _(p=p):
                o_ref[...] = _dot_tn(a_refs[p][...], b_ref[...])
        if side is not None:
            @pl.when((jb == grid[0] - 1) & (ia == grid[1] - 1))
            def _():
                side.finish(side_in, side_out, *sems)

    def piece_spec(p):
        return pl.BlockSpec((s, ta), lambda jb, ia: (0, jnp.clip(ia - starts[p], 0, tiles[p] - 1)))

    s_in, s_shape, s_out, s_alias, s_sems = _side_specs(side, n_in, 1)
    outs = pl.pallas_call(
        body, out_shape=[jax.ShapeDtypeStruct((sum(tiles) * ta, nb), F32)] + s_shape, grid=grid,
        in_specs=[piece_spec(p) for p in range(len(pieces))] + [pl.BlockSpec((s, tb), lambda jb, ia: (0, jb))] + s_in,
        out_specs=[pl.BlockSpec((ta, tb), lambda jb, ia: (ia, jb))] + s_out,
        input_output_aliases=s_alias, scratch_shapes=s_sems,
        compiler_params=_params(2), name=name)(*pieces, b, *(side.arrays if side else ()))
    return outs[0] if side is None else (outs[0], outs[1:])


def _place():
    return lax.axis_index("x"), lax.axis_index("y"), lax.axis_index("c")


def _chip_peer(x, y, k):
    return (x ^ (k >> 1), y ^ (k & 1))


def _row_tile(k, cap=544):
    return max(t for t in range(32, min(k, cap) + 1, 32) if k % t == 0)


def _cast_bf16(w, r, name):
    l, k, n = w.shape
    assert l == 2
    tk = _row_tile(k)

    def body(r_ref, w_ref, o0_ref, o1_ref):
        o0_ref[...] = w_ref[0].astype(BF16)
        o1_ref[...] = w_ref[1].astype(BF16)

    out_spec = pl.BlockSpec((None, tk, n), lambda i, r_ref: (r_ref[0], i, 0))
    return pl.pallas_call(
        body, out_shape=[jax.ShapeDtypeStruct((4, k, n), BF16)] * 2,
        grid_spec=pltpu.PrefetchScalarGridSpec(
            num_scalar_prefetch=1, grid=(k // tk,),
            in_specs=[pl.BlockSpec((l, tk, n), lambda i, r_ref: (0, i, 0))],
            out_specs=[out_spec, out_spec]),
        compiler_params=_params(1), name=name)(r, w)


class _Exchange(NamedTuple):
    arrays: tuple
    aliased: tuple
    sems: tuple
    start: Callable
    finish: Callable


def _both(a, b):
    na, ma, ka = len(a.arrays), len(a.aliased), 2 * len(a.sems)

    def start(ins, outs, *sems):
        a.start(ins[:na], outs[:ma], *sems[:ka])
        b.start(ins[na:], outs[ma:], *sems[ka:])

    def finish(ins, outs, *sems):
        a.finish(ins[:na], outs[:ma], *sems[:ka])
        b.finish(ins[na:], outs[ma:], *sems[ka:])

    return _Exchange(a.arrays + b.arrays, a.aliased + tuple(na + i for i in b.aliased), a.sems + b.sems, start, finish)


def _all_gather(bufs):
    nt = len(bufs)

    def ici(t, ins, outs, send_sems, recv_sems, x, y, c, r, k):
        return pltpu.make_async_remote_copy(
            src_ref=ins[t].at[r, c], dst_ref=outs[t].at[r, c], send_sem=send_sems.at[t, k - 1],
            recv_sem=recv_sems.at[t, k - 1], device_id=(*_chip_peer(x, y, k), c), device_id_type=MESH)

    def d2d(t, outs, send_sems, recv_sems, x, y, c, r, k, half):
        slab = outs[t].at[r ^ k, half]
        return pltpu.make_async_remote_copy(
            src_ref=slab, dst_ref=slab, send_sem=send_sems.at[t, 2 + k], recv_sem=recv_sems.at[t, 2 + k],
            device_id=(x, y, 1 - c), device_id_type=MESH)

    def start(ins, outs, send_sems, recv_sems):
        x, y, c = _place()
        r = 2 * x + y
        for t in range(nt):
            for k in (1, 2, 3):
                ici(t, ins, outs, send_sems, recv_sems, x, y, c, r, k).start()

    def finish(ins, outs, send_sems, recv_sems):
        x, y, c = _place()
        r = 2 * x + y
        for t in range(nt):
            for k in (1, 2, 3):
                slab = outs[t].at[r ^ k, c]
                pltpu.make_async_remote_copy(
                    src_ref=slab, dst_ref=slab, send_sem=send_sems.at[t, k - 1], recv_sem=recv_sems.at[t, k - 1],
                    device_id=(x, y, 1 - c), device_id_type=MESH).wait_recv()
                d2d(t, outs, send_sems, recv_sems, x, y, c, r, k, c).start()
        for t in range(nt):
            for k in (1, 2, 3):
                d2d(t, outs, send_sems, recv_sems, x, y, c, r, k, 1 - c).wait_recv()
        for t in range(nt):
            for k in (1, 2, 3):
                ici(t, ins, outs, send_sems, recv_sems, x, y, c, r, k).wait_send()
                d2d(t, outs, send_sems, recv_sems, x, y, c, r, k, c).wait_send()

    return _Exchange(tuple(bufs), tuple(range(nt)), ((nt, 6),), start, finish)


def _run_exchange(name, ex):
    n_in, n_out = len(ex.arrays), len(ex.aliased)

    def body(*refs):
        ins, outs = refs[:n_in], refs[n_in:n_in + n_out]
        send_sems, recv_sems = refs[n_in + n_out:]
        ex.start(ins, outs, send_sems, recv_sems)
        ex.finish(ins, outs, send_sems, recv_sems)

    any_spec = pl.BlockSpec(memory_space=pl.ANY)
    return pl.pallas_call(
        body, out_shape=[jax.ShapeDtypeStruct(ex.arrays[a].shape, ex.arrays[a].dtype) for a in ex.aliased],
        in_specs=[any_spec] * n_in, out_specs=[any_spec] * n_out,
        input_output_aliases={a: o for o, a in enumerate(ex.aliased)},
        scratch_shapes=[pltpu.SemaphoreType.DMA(shape) for shape in ex.sems for _ in range(2)],
        name=name)(*ex.arrays)


def _rs_to_sibling(grads):
    nt = len(grads)
    landing = [lax.empty((4,) + g.shape[2:], F32) for g in grads]

    def copies(ins, outs, send_sems, recv_sems):
        x, y, c = _place()
        return [pltpu.make_async_remote_copy(
            src_ref=ins[t].at[:, 1 - c], dst_ref=outs[t], send_sem=send_sems.at[t], recv_sem=recv_sems.at[t],
            device_id=(x, y, 1 - c), device_id_type=MESH) for t in range(nt)]

    def start(ins, outs, send_sems, recv_sems):
        for cp in copies(ins, outs, send_sems, recv_sems):
            cp.start()

    def finish(ins, outs, send_sems, recv_sems):
        for cp in copies(ins, outs, send_sems, recv_sems):
            cp.wait()

    return _Exchange(tuple(grads) + tuple(landing), tuple(range(nt, 2 * nt)), ((nt,),), start, finish)


def _add_half(g, recv, cr, name):
    _, _, k2, n = g.shape
    tk = _row_tile(k2)

    def body(cr_ref, g_ref, r_ref, sums_ref, mine_ref):
        val = (g_ref[...] + r_ref[...]).astype(BF16)
        sums_ref[...] = val

        @pl.when(pl.program_id(1) == cr_ref[1])
        def _():
            mine_ref[...] = val

    return pl.pallas_call(
        body, out_shape=[jax.ShapeDtypeStruct((4, k2, n), BF16)] * 2,
        grid_spec=pltpu.PrefetchScalarGridSpec(
            num_scalar_prefetch=1, grid=(k2 // tk, 4),
            in_specs=[pl.BlockSpec((None, None, tk, n), lambda i, q, cr_ref: (q, cr_ref[0], i, 0)),
                      pl.BlockSpec((None, tk, n), lambda i, q, cr_ref: (q, i, 0))],
            out_specs=[pl.BlockSpec((None, tk, n), lambda i, q, cr_ref: (q, i, 0)),
                       pl.BlockSpec((None, tk, n), lambda i, q, cr_ref: (cr_ref[1], i, 0))]),
        compiler_params=_params(2), name=name)(cr, g, recv)


def _rs_to_chips(sums, parts):
    nt = len(sums)

    def copies(ins, outs, send_sems, recv_sems):
        x, y, c = _place()
        r = 2 * x + y
        return [pltpu.make_async_remote_copy(
            src_ref=ins[t].at[r ^ k], dst_ref=outs[t].at[r], send_sem=send_sems.at[t, k - 1],
            recv_sem=recv_sems.at[t, k - 1], device_id=(*_chip_peer(x, y, k), c), device_id_type=MESH)
            for t in range(nt) for k in (1, 2, 3)]

    def start(ins, outs, send_sems, recv_sems):
        for cp in copies(ins, outs, send_sems, recv_sems):
            cp.start()

    def finish(ins, outs, send_sems, recv_sems):
        for cp in copies(ins, outs, send_sems, recv_sems):
            cp.wait()

    return _Exchange(tuple(sums) + tuple(parts), tuple(range(nt, 2 * nt)), ((nt, 3),), start, finish)


def _sum4(parts, cr, name):
    _, k2, n = parts.shape
    tk = _row_tile(k2)

    def body(cr_ref, p_ref, o_ref):
        p = p_ref[...].astype(F32)
        o_ref[...] = ((p[0] + p[1]) + p[2]) + p[3]

    return pl.pallas_call(
        body, out_shape=jax.ShapeDtypeStruct((2, k2, n), F32),
        grid_spec=pltpu.PrefetchScalarGridSpec(
            num_scalar_prefetch=1, grid=(k2 // tk,),
            in_specs=[pl.BlockSpec((4, tk, n), lambda i, cr_ref: (0, i, 0))],
            out_specs=pl.BlockSpec((None, tk, n), lambda i, cr_ref: (cr_ref[0], i, 0))),
        compiler_params=_params(1), name=name)(cr, parts)


def _exchange_halves(both):
    nt = len(both)

    def copies(ins, outs, send_sems, recv_sems):
        x, y, c = _place()
        return [pltpu.make_async_remote_copy(
            src_ref=ins[t].at[c], dst_ref=outs[t].at[c], send_sem=send_sems.at[t], recv_sem=recv_sems.at[t],
            device_id=(x, y, 1 - c), device_id_type=MESH) for t in range(nt)]

    def start(ins, outs, send_sems, recv_sems):
        for cp in copies(ins, outs, send_sems, recv_sems):
            cp.start()

    def finish(ins, outs, send_sems, recv_sems):
        for cp in copies(ins, outs, send_sems, recv_sems):
            cp.wait()

    return _Exchange(tuple(both), tuple(range(nt)), ((nt,),), start, finish)


def _adamw_math(w, g, m, v):
    m = ADAM_B1 * m + (1.0 - ADAM_B1) * g
    v = ADAM_B2 * v + (1.0 - ADAM_B2) * jnp.square(g)
    m_hat = m / (1.0 - ADAM_B1 ** ADAM_STEP)
    v_hat = v / (1.0 - ADAM_B2 ** ADAM_STEP)
    delta = -ADAM_LR * (m_hat / (jnp.sqrt(v_hat) + ADAM_EPS) + ADAM_WD * w)
    return delta, m, v


def _adamw(w, m, v, g0, g1, name):
    _, k, n = w.shape
    tk = _row_tile(k)
    nk = k // tk

    def body(w_ref, m_ref, v_ref, g0_ref, g1_ref, grad_ref, delta_ref, nm_ref, nv_ref):
        g = jnp.where(pl.program_id(0) == 0, g0_ref[...], g1_ref[...])
        delta, nm, nv = _adamw_math(w_ref[...], g, m_ref[...], v_ref[...])
        grad_ref[...] = g
        delta_ref[...] = delta
        nm_ref[...] = nm
        nv_ref[...] = nv

    lay = pl.BlockSpec((None, tk, n), lambda a, i: (a, i, 0))
    g0_spec = pl.BlockSpec((tk, n), lambda a, i: (jnp.where(a == 0, i, nk - 1), 0))
    g1_spec = pl.BlockSpec((tk, n), lambda a, i: (jnp.where(a == 1, i, 0), 0))
    return pl.pallas_call(
        body, out_shape=[jax.ShapeDtypeStruct(w.shape, F32)] * 4, grid=(2, nk),
        in_specs=[lay, lay, lay, g0_spec, g1_spec], out_specs=[lay] * 4,
        compiler_params=_params(2), name=name)(w, m, v, g0, g1)


def _small_allreduce_adamw(gpart, w, m, v):
    shape = gpart.shape

    def body(g_ref, w_ref, m_ref, v_ref, gsum_ref, delta_ref, nm_ref, nv_ref, recv_ref, send_sems, recv_sems):
        x, y, c = _place()
        me = 4 * x + 2 * y + c
        recv_ref[me] = g_ref[...]
        cps = []
        for k in range(1, 8):
            peer = (x ^ (k >> 2), y ^ ((k >> 1) & 1), c ^ (k & 1))
            cp = pltpu.make_async_remote_copy(
                src_ref=g_ref, dst_ref=recv_ref.at[me], send_sem=send_sems.at[k - 1], recv_sem=recv_sems.at[k - 1],
                device_id=peer, device_id_type=MESH)
            cp.start()
            cps.append(cp)
        for cp in cps:
            cp.wait()
        g = recv_ref[0]
        for dev in range(1, 8):
            g = g + recv_ref[dev]
        delta, nm, nv = _adamw_math(w_ref[...], g, m_ref[...], v_ref[...])
        gsum_ref[...] = g
        delta_ref[...] = delta
        nm_ref[...] = nm
        nv_ref[...] = nv

    vm = pl.BlockSpec(memory_space=pltpu.VMEM)
    return pl.pallas_call(
        body, out_shape=[jax.ShapeDtypeStruct(shape, F32)] * 4, in_specs=[vm] * 4, out_specs=[vm] * 4,
        scratch_shapes=[pltpu.VMEM((8,) + shape, F32), pltpu.SemaphoreType.DMA((7,)), pltpu.SemaphoreType.DMA((7,))],
        name="small_allreduce_adamw")(gpart, w, m, v)


BIG = ("w_in", "w_up_a", "w_up_b", "w_o", "w_ff1", "w_ff2", "w_pe", "w_pg")
COL_SHARDED = ("w_in", "w_up_a", "w_up_b", "w_ff1", "w_pe")
ROW_SHARDED = ("w_o", "w_ff2", "w_pg")
SMALL_ROWS = 16


def _pack_small(g_mix, g_mlp, g_pe, g_final, sinks, rel_bias, loss=None):
    d = g_final.shape[0]
    row = lambda v: jnp.pad(v.reshape(1, -1), ((0, 0), (0, d - v.size)))
    rows = [g_mix, g_mlp, g_pe, g_final.reshape(1, d),
            jnp.zeros((1, d), F32) if loss is None else row(loss), row(sinks), row(rel_bias)]
    out = jnp.concatenate(rows, axis=0)
    return jnp.pad(out, ((0, SMALL_ROWS - out.shape[0]), (0, 0)))


def _unpack_small(a, sinks_shape, rel_shape):
    return (a[0:2], a[2:4], a[4:6], a[6], a[8, :sinks_shape[0] * sinks_shape[1]].reshape(sinks_shape),
            a[9, :rel_shape[0] * rel_shape[1]].reshape(rel_shape))


def kernel(x, p, w_in, w_up_a, w_up_b, w_o, w_ff1, w_ff2, w_pe, w_pg, g_mix, g_mlp, g_pe, g_final, sinks, rel_bias, loss_target, m_w_in, m_w_up_a, m_w_up_b, m_w_o, m_w_ff1, m_w_ff2, m_w_pe, m_w_pg, m_g_mix, m_g_mlp, m_g_pe, m_g_final, m_sinks, m_rel_bias, v_w_in, v_w_up_a, v_w_up_b, v_w_o, v_w_ff1, v_w_ff2, v_w_pe, v_w_pg, v_g_mix, v_g_mlp, v_g_pe, v_g_final, v_sinks, v_rel_bias):
    depth = w_in.shape[0]
    assert depth == 2
    x0 = x[0]
    target = loss_target[0]
    d = x0.shape[1]
    wl = dict(w_in=w_in, w_up_a=w_up_a, w_up_b=w_up_b, w_o=w_o, w_ff1=w_ff1, w_ff2=w_ff2, w_pe=w_pe, w_pg=w_pg)
    ml = dict(w_in=m_w_in, w_up_a=m_w_up_a, w_up_b=m_w_up_b, w_o=m_w_o, w_ff1=m_w_ff1, w_ff2=m_w_ff2, w_pe=m_w_pe, w_pg=m_w_pg)
    vl = dict(w_in=v_w_in, w_up_a=v_w_up_a, w_up_b=v_w_up_b, w_o=v_w_o, w_ff1=v_w_ff1, w_ff2=v_w_ff2, w_pe=v_w_pe, w_pg=v_w_pg)
    c_idx = lax.axis_index("c").astype(jnp.int32)
    r_idx = (2 * lax.axis_index("x") + lax.axis_index("y")).astype(jnp.int32)
    cr = jnp.stack([c_idx, r_idx])

    wl["w_in"], ml["w_in"], vl["w_in"] = (jnp.swapaxes(a, 1, 2) for a in (w_in, m_w_in, v_w_in))

    bufs = {}
    for n in BIG:
        k, nn = wl[n].shape[1:]
        for l, b in enumerate(_cast_bf16(wl[n], r_idx.reshape(1), "cast_" + n)):
            bufs[n, l] = b.reshape(4, 2, k // 2, nn)

    def gather(keys, run):
        for key, b in zip(keys, run(_all_gather([bufs[key] for key in keys]))):
            bufs[key] = b

    def gathered(n, l):
        _, _, k2, nn = bufs[n, l].shape
        if n in ROW_SHARDED or n == "w_in":
            return bufs[n, l].reshape(8 * k2, nn)
        return bufs[n, l].reshape(4, 2 * k2, nn)

    gather([("w_in", 0)], lambda ex: _run_exchange("all_gather_first", ex))

    full = {n: [None] * depth for n in BIG}
    saved = []
    xi = x0
    for i in range(depth):
        st = dict(x0=xi)
        gm = g_mix[i].reshape(1, d)
        full["w_in"][i] = gathered("w_in", i)
        st["h1"], st["qkv"], st["gates"] = _inproj_fwd(xi, gm, full["w_in"][i], f"inproj_fwd_{i}")

        def attend(ex):
            (st["oa"], st["lt"], st["nb"], st["a_wide"], st["sg_wide"]), filled = _sb_fwd(st["qkv"], f"sb_fwd_{i}", ex)
            return filled

        with_window = [("w_in", 1)] if i == 0 else [(n, i) for n in ("w_up_a", "w_up_b", "w_o")]
        gather([(n, i) for n in BIG if n != "w_in" and (n, i) not in with_window], attend)

        def window(ex):
            (st["ob"], st["probs"]), filled = _swa_fwd(st["qkv"], sinks[i], rel_bias, f"swa_fwd_{i}", ex)
            return filled

        gather(with_window, window)
        for n in BIG:
            if n != "w_in":
                full[n][i] = gathered(n, i)
        st["m"], st["x1"] = _mixer_fwd(st["oa"], st["ob"], st["gates"], xi, full["w_up_a"][i], full["w_up_b"][i],
                                       full["w_o"][i], f"mixer_fwd_{i}")
        st["h2"], st["u"], st["a"], st["x2"] = _ff_fwd(st["x1"], g_mlp[i].reshape(1, d), full["w_ff1"][i],
                                                       full["w_ff2"][i], f"ff_fwd_{i}")
        st["pb"], st["h3"], st["pe"], st["gt"], xi = _ple_fwd(p[i, 0], st["x2"], g_pe[i].reshape(1, d),
                                                            full["w_pe"][i], full["w_pg"][i], f"ple_fwd_{i}")
        saved.append(st)

    dx, dg_final, loss_part = _loss_bwd(xi, target, g_final.reshape(1, d), "loss_bwd")
    gw = {n: [None] * depth for n in BIG}
    reduced = {}

    chip_sums = {}

    def by_halves(keys):
        tensors = []
        for n, l in keys:
            g = gw[n][l]
            if n in ROW_SHARDED:
                ka, nb = g.shape[1:]
                g = g.reshape(4, ka // 4, nb)
            _, k, nn = g.shape
            tensors.append(g.reshape(4, 2, k // 2, nn))
        return tensors

    def add_halves(keys, tensors, landed):
        for (n, l), g, r in zip(keys, tensors, landed):
            chip_sums[n, l] = _add_half(g, r, cr, f"add_half_{n}_{l}")

    def to_sibling(keys, run):
        tensors = by_halves(keys)
        add_halves(keys, tensors, run(_rs_to_sibling(tensors)))

    def to_chips(keys):
        return _rs_to_chips([chip_sums[k][0] for k in keys], [chip_sums[k][1] for k in keys])

    halves = {}

    def sum_chips(keys, parts):
        for (n, l), pc in zip(keys, parts):
            halves[n, l] = _sum4(pc, cr, f"sum4_{n}_{l}")

    def swap_halves(keys):
        def store(filled):
            for key, both in zip(keys, filled):
                reduced[key] = both
        return _exchange_halves([halves[k] for k in keys]), store

    dg_mix, dg_mlp, dg_pe, dsinks = [None] * depth, [None] * depth, [None] * depth, [None] * depth
    drel = jnp.zeros((SW_HEADS, LANES), F32)
    for i in reversed(range(depth)):
        st = saved[i]
        dpe, dgt, dx2, dg_pe[i] = _ple_bwd(dx, st["pe"], st["gt"], st["x2"], g_pe[i].reshape(1, d),
                                           full["w_pg"][i], f"ple_bwd_{i}")
        gw["w_pe"][i] = _mm_tn(st["pb"], dpe, f"dw_pe_{i}", 4)
        gw["w_pg"][i] = _mm_tn(st["h3"], dgt, f"dw_pg_{i}")
        du, dx2b = _ff2_bwd(dx2, st["u"], full["w_ff2"][i], f"ff2_bwd_{i}")
        gw["w_ff2"][i] = _mm_tn(st["a"], dx2b, f"dw_ff2_{i}")
        gw["w_ff1"][i] = _mm_tn(st["h2"], du, f"dw_ff1_{i}", 4)
        if i == 0:
            (dx1, dx1b, dg_mlp[i]), parts = _ff1_bwd(du, dx2, st["x1"], g_mlp[i].reshape(1, d), full["w_ff1"][i],
                                                    f"ff1_bwd_{i}", to_chips([("w_in", 1)]))
            sum_chips([("w_in", 1)], parts)
        else:
            dx1, dx1b, dg_mlp[i] = _ff1_bwd(du, dx2, st["x1"], g_mlp[i].reshape(1, d), full["w_ff1"][i],
                                            f"ff1_bwd_{i}")
        gw["w_o"][i] = _mm_tn(st["m"], dx1b, f"dw_o_{i}")
        early = [(n, i) for n in ("w_pe", "w_pg", "w_ff2", "w_ff1", "w_o")]

        def mixer(ex):
            (dya, dyb, dgates, doa, dob), landed = _mixer_bwd(
                dx1b, st["gates"], st["oa"], st["ob"], full["w_o"][i], full["w_up_a"][i], full["w_up_b"][i],
                f"mixer_bwd_{i}", ex)
            st.update(dya=dya, dyb=dyb, dgates=dgates, doa=doa, dob=dob)
            return landed

        to_sibling(early, mixer)
        dgates = st["dgates"]
        gw["w_up_a"][i] = _mm_tn(st["oa"], st["dya"], f"dw_up_a_{i}", 4)
        gw["w_up_b"][i] = _mm_tn(st["ob"], st["dyb"], f"dw_up_b_{i}", 4)
        late = [("w_up_a", i), ("w_up_b", i)]
        late_halves = by_halves(late)
        keys = early + late
        with_sb = [(n, i) for n in ("w_ff1", "w_o", "w_pg", "w_pe")]
        with_swa = [(n, i) for n in ("w_ff2", "w_up_a", "w_up_b")]
        dqa, dka, dva, filled = _sb_bwd(st["qkv"], st["lt"], st["nb"], st["a_wide"], st["sg_wide"], st["doa"],
                                        f"sb_bwd_{i}", _both(_rs_to_sibling(late_halves), to_chips(with_sb)))
        add_halves(late, late_halves, filled[:len(late)])
        sum_chips(with_sb, filled[len(late):])
        (dqb, dkvb, dsk, drl), parts = _swa_bwd(st["qkv"], st["ob"], st["dob"], st["probs"], f"swa_bwd_{i}",
                                                to_chips(with_swa))
        sum_chips(with_swa, parts)
        dsinks[i] = dsk[:, 0]
        drel = drel + drl
        dproj = [dqa, dka, dva, dqb, dkvb, dgates]
        swap, store = swap_halves(keys + ([("w_in", 1)] if i == 0 else []))
        dw_in_t, filled = _mm_tn_pieces(dproj, st["h1"], f"dw_in_{i}", swap)
        store(filled)
        gw["w_in"][i] = dw_in_t.reshape(4, dw_in_t.shape[0] // 4, d)
        if i == 1:
            def inproj(ex):
                (dx, dg_mix[i]), landed = _inproj_bwd(dproj, dx1, st["x0"], g_mix[i].reshape(1, d),
                                                      full["w_in"][i], f"inproj_bwd_{i}", ex)
                st["dx"] = dx
                return landed

            to_sibling([("w_in", 1)], inproj)
            dx = st["dx"]
        else:
            keys = [("w_in", 0)]
            to_sibling(keys, lambda ex: _run_exchange("rs_to_sibling_last", ex))
            (dx, dg_mix[i]), parts = _inproj_bwd(dproj, dx1, st["x0"], g_mix[i].reshape(1, d),
                                                 full["w_in"][i], f"inproj_bwd_{i}", to_chips(keys))
            sum_chips(keys, parts)
            swap, store = swap_halves(keys)
            store(_run_exchange("exchange_halves_last", swap))
    grad_x = dx[None]

    outs = {}
    for n in BIG:
        g0, g1 = (reduced[n, l].reshape(wl[n].shape[1:]) for l in range(depth))
        outs[n] = _adamw(wl[n], ml[n], vl[n], g0, g1, "adamw_" + n)
    outs["w_in"] = [jnp.swapaxes(a, 1, 2) for a in outs["w_in"]]

    drel_bias = drel[:, :N_BUCKETS].T
    gsmall = _pack_small(jnp.concatenate(dg_mix, 0), jnp.concatenate(dg_mlp, 0), jnp.concatenate(dg_pe, 0),
                         dg_final[0], jnp.stack(dsinks), drel_bias, loss_part[0, :1])
    wsmall = _pack_small(g_mix, g_mlp, g_pe, g_final, sinks, rel_bias)
    msmall = _pack_small(m_g_mix, m_g_mlp, m_g_pe, m_g_final, m_sinks, m_rel_bias)
    vsmall = _pack_small(v_g_mix, v_g_mlp, v_g_pe, v_g_final, v_sinks, v_rel_bias)
    small = _small_allreduce_adamw(gsmall, wsmall, msmall, vsmall)
    loss = small[0][7, 0]
    small = [_unpack_small(a, sinks.shape, rel_bias.shape) for a in small]

    result = [loss, grad_x]
    for kind in range(4):
        result += [outs[n][kind] for n in BIG]
        result += list(small[kind])
    return tuple(result)
```

```python
import functools
import math
from typing import Callable, NamedTuple

import numpy as np
import jax
import jax.numpy as jnp
from jax import lax
from jax.experimental import pallas as pl
from jax.experimental.pallas import tpu as pltpu

F32 = jnp.float32
BF16 = jnp.bfloat16
MESH = pl.DeviceIdType.MESH

HEAD_DIM = 64
SB_HEADS = 8
SW_HEADS = 8
SW_KV_HEADS = 2
WINDOW = 128
N_BUCKETS = 32
MAX_DISTANCE = 128
EPS = 1e-6
SB_W = SB_HEADS * HEAD_DIM
SW_QW = SW_HEADS * HEAD_DIM
SW_KVW = SW_KV_HEADS * HEAD_DIM
QKV_W = 3 * SB_W + SW_QW + 2 * SW_KVW
SCALE = HEAD_DIM ** -0.5
assert SCALE == 0.125
LANES = 128
TQ = 128
BK = 128
NEG = -1e30
SB_EXHAUSTED = -106.0

ADAM_LR = 0.001
ADAM_B1 = 0.9
ADAM_B2 = 0.999
ADAM_EPS = 1e-08
ADAM_WD = 0.01
ADAM_STEP = 10

VMEM_LIMIT = 56 * 1024 * 1024


def _dot(a, b):
    return jnp.dot(a, b, preferred_element_type=F32)


def _dot_nt(a, b):
    return lax.dot_general(a, b, (((1,), (1,)), ((), ())), preferred_element_type=F32)


def _dot_tn(a, b):
    return lax.dot_general(a, b, (((0,), (0,)), ((), ())), preferred_element_type=F32)


def _sum_all(x):
    return jnp.sum(jnp.sum(x, axis=1, keepdims=True), axis=0, keepdims=True)


def _sigmoid(x):
    return 1.0 / (1.0 + jnp.exp(-x))


def _rms(x, g):
    r = lax.rsqrt(jnp.mean(x * x, axis=-1, keepdims=True) + EPS)
    return (x * r) * g


def _rms_bwd(dy, x, g):
    r = lax.rsqrt(jnp.mean(x * x, axis=-1, keepdims=True) + EPS)
    n = x * r
    dg = jnp.sum(dy * n, axis=0, keepdims=True)
    dn = dy * g
    dx = r * (dn - n * jnp.mean(dn * n, axis=-1, keepdims=True))
    return dx, dg


def _params(n_axes):
    return pltpu.CompilerParams(dimension_semantics=("arbitrary",) * n_axes, vmem_limit_bytes=VMEM_LIMIT)


def _rowcall(name, body, row_ins, const_ins, row_outs, acc_outs=(), tm=512, side=None):
    s = row_ins[0].shape[0]
    assert s % tm == 0
    in_specs = [pl.BlockSpec((tm, a.shape[1]), lambda i: (i, 0)) for a in row_ins]
    in_specs += [pl.BlockSpec(a.shape, functools.partial(lambda i, nd: (0,) * nd, nd=a.ndim)) for a in const_ins]
    out_shape = [jax.ShapeDtypeStruct((s, c), dt) for c, dt in row_outs]
    out_specs = [pl.BlockSpec((tm, c), lambda i: (i, 0)) for c, _ in row_outs]
    out_shape += [jax.ShapeDtypeStruct(sh, dt) for sh, dt in acc_outs]
    out_specs += [pl.BlockSpec(sh, functools.partial(lambda i, nd: (0,) * nd, nd=len(sh))) for sh, _ in acc_outs]
    if side is None:
        return pl.pallas_call(body, out_shape=out_shape, grid=(s // tm,), in_specs=in_specs, out_specs=out_specs,
                              compiler_params=_params(1), name=name)(*row_ins, *const_ins)
    n_in, n_out = len(in_specs), len(out_specs)

    def with_side(*refs):
        side_in, outs, side_out, sems = _side_refs(side, refs[n_in:], n_out)

        @pl.when(pl.program_id(0) == 0)
        def _():
            side.start(side_in, side_out, *sems)

        body(*refs[:n_in], *outs)

        @pl.when(pl.program_id(0) == s // tm - 1)
        def _():
            side.finish(side_in, side_out, *sems)

    s_in, s_shape, s_out, s_alias, s_sems = _side_specs(side, n_in, n_out)
    outs = pl.pallas_call(with_side, out_shape=out_shape + s_shape, grid=(s // tm,), in_specs=in_specs + s_in,
                          out_specs=out_specs + s_out, input_output_aliases=s_alias, scratch_shapes=s_sems,
                          compiler_params=_params(1), name=name)(*row_ins, *const_ins, *side.arrays)
    return outs[:n_out], outs[n_out:]


def _dot_cols(a, w_ref):
    return jnp.concatenate([_dot(a, w_ref[r]) for r in range(w_ref.shape[0])], axis=1)


def _dot_cols_t(a, w_ref):
    n = w_ref.shape[2]
    out = _dot_nt(a[:, :n], w_ref[0])
    for r in range(1, w_ref.shape[0]):
        out = out + _dot_nt(a[:, r * n:(r + 1) * n], w_ref[r])
    return out


def _inproj_fwd(x, g, wt, name):
    d = x.shape[1]

    def body(x_ref, g_ref, w_ref, h_ref, qkv_ref, gate_ref):
        hb = _rms(x_ref[...], g_ref[...]).astype(BF16)
        h_ref[...] = hb
        qkv_ref[...] = _dot_nt(hb, w_ref[:QKV_W, :]).astype(BF16)
        gate_ref[...] = _dot_nt(hb, w_ref[QKV_W:, :])

    return _rowcall(name, body, [x], [g, wt], [(d, BF16), (QKV_W, BF16), (2 * d, F32)])


def _mixer_fwd(oa, ob, gates, x, wua, wub, wo, name):
    d = x.shape[1]

    def body(oa_ref, ob_ref, gate_ref, x_ref, wua_ref, wub_ref, wo_ref, m_ref, x1_ref):
        ya = _dot_cols(oa_ref[...], wua_ref)
        yb = _dot_cols(ob_ref[...], wub_ref)
        m = _sigmoid(gate_ref[:, :d]) * ya + _sigmoid(gate_ref[:, d:]) * yb
        mb = m.astype(BF16)
        m_ref[...] = mb
        x1_ref[...] = x_ref[...] + _dot(mb, wo_ref[...])

    return _rowcall(name, body, [oa, ob, gates, x], [wua, wub, wo], [(d, BF16), (d, F32)])


def _ff_fwd(x1, g, w1, w2, name):
    _, d, nq = w1.shape
    dff = 4 * nq

    def body(x_ref, g_ref, w1_ref, w2_ref, h_ref, r_ref, a_ref, o_ref):
        x = x_ref[...]
        hb = _rms(x, g_ref[...]).astype(BF16)
        h_ref[...] = hb
        r = jnp.maximum(_dot_cols(hb, w1_ref), 0.0)
        r_ref[...] = r.astype(BF16)
        a = jnp.square(r).astype(BF16)
        a_ref[...] = a
        o_ref[...] = x + _dot(a, w2_ref[...])

    return _rowcall(name, body, [x1], [g, w1, w2], [(d, BF16), (dff, BF16), (dff, BF16), (d, F32)], tm=256)


def _ple_fwd(p, x2, g, wpe, wpg, name):
    d = x2.shape[1]

    def body(p_ref, x_ref, g_ref, wpe_ref, wpg_ref, pb_ref, h_ref, pe_ref, gt_ref, x3_ref):
        pb = p_ref[...].astype(BF16)
        pb_ref[...] = pb
        pe = _dot_cols(pb, wpe_ref)
        x = x_ref[...]
        hb = _rms(x, g_ref[...]).astype(BF16)
        h_ref[...] = hb
        gt = _dot(hb, wpg_ref[...])
        pe_ref[...] = pe
        gt_ref[...] = gt
        x3_ref[...] = x + pe * _sigmoid(gt)

    return _rowcall(name, body, [p, x2], [g, wpe, wpg],
                    [(p.shape[1], BF16), (d, BF16), (d, F32), (d, F32), (d, F32)])


def _pair_stack(t, lane):
    zero = jnp.zeros_like(t)
    return jnp.concatenate([jnp.where(lane < HEAD_DIM, t, zero), jnp.where(lane >= HEAD_DIM, t, zero)], axis=0)


def _sb_rel():
    row = lax.broadcasted_iota(jnp.int32, (2 * TQ, BK), 0)
    row = jnp.where(row >= TQ, row - TQ, row)
    col = lax.broadcasted_iota(jnp.int32, (2 * TQ, BK), 1)
    return col - row


def _split_dot(x, m01, two_pass=True):
    hi = x.astype(BF16)
    if not two_pass:
        return _dot(hi, m01)
    lo = (x - hi.astype(F32)).astype(BF16)
    return _dot(hi, m01) + _dot(lo, m01)


def _sb_scores(qs, k, mask):
    z = _dot_nt(qs, k)
    lb = jnp.minimum(z, 0.0) - jnp.log(1.0 + jnp.exp(-jnp.abs(z)))
    lm = lb - z
    return lb, lm if mask is None else jnp.where(mask, lm, 0.0)


SB_STRAIGHT = 3
SB_WIDE = SB_STRAIGHT * BK
SB_QB = 2
SWA_QB = 4


def _sb_wide_consts():
    j = np.arange(BK)[:, None]
    s = np.arange(BK)[None, :]
    ones = np.ones((BK, BK), np.float32)
    as_bf16 = lambda m: jnp.asarray(np.concatenate([m, ones], axis=1).astype(np.float32), dtype=BF16)
    return as_bf16(j > s), as_bf16(j <= s), as_bf16(j < s)


def _wide_sums(x, m01, suffix, two_pass=True):
    parts = [_split_dot(x[:, b * BK:(b + 1) * BK], m01, two_pass) for b in range(SB_STRAIGHT)]
    order = range(SB_STRAIGHT - 1, -1, -1) if suffix else range(SB_STRAIGHT)
    out = [None] * SB_STRAIGHT
    carry = None
    for b in order:
        out[b] = parts[b][:, :BK] if carry is None else parts[b][:, :BK] + carry
        carry = parts[b][:, BK:] if carry is None else carry + parts[b][:, BK:]
    return jnp.concatenate(out, axis=1), carry


def _side_refs(ex, rest, n_out):
    n_in = len(ex.arrays) if ex else 0
    n_alias = len(ex.aliased) if ex else 0
    ins, rest = rest[:n_in], rest[n_in:]
    outs, rest = rest[:n_out], rest[n_out:]
    return ins, outs, rest[:n_alias], rest[n_alias:]


def _side_specs(ex, n_in, n_out):
    if ex is None:
        return [], [], [], {}, []
    any_spec = pl.BlockSpec(memory_space=pl.ANY)
    return ([any_spec] * len(ex.arrays),
            [jax.ShapeDtypeStruct(ex.arrays[a].shape, ex.arrays[a].dtype) for a in ex.aliased],
            [any_spec] * len(ex.aliased), {n_in + a: n_out + o for o, a in enumerate(ex.aliased)},
            [pltpu.SemaphoreType.DMA(shape) for shape in ex.sems for _ in range(2)])


def _sb_fwd(qkv, name, side=None):
    s = qkv.shape[0]
    nq = s // TQ
    npair = SB_W // LANES
    sufw = _sb_wide_consts()[0]

    def body(q_ref, k_ref, v_ref, sufw_ref, *rest):
        side_in, (o_ref, lt_ref, nb_ref, a_ref, sg_ref), side_out, scratch = _side_refs(side, rest, 5)
        cf_ref, acc_ref = scratch[:2]
        step_id = pl.program_id(1)
        if side is not None:
            @pl.when((pl.program_id(0) == 0) & (step_id == 0))
            def _():
                side.start(side_in, side_out, *scratch[2:])
        lane = lax.broadcasted_iota(jnp.int32, (TQ, LANES), 1)
        rel = _sb_rel()
        blocks = [step_id * SB_QB + b for b in range(SB_QB)]
        qs = [_pair_stack(q_ref[b * TQ:(b + 1) * TQ, :] * SCALE, lane) for b in range(SB_QB)]

        straight = blocks[0] >= SB_STRAIGHT - 1

        @pl.when(straight)
        def _():
            for b, i in enumerate(blocks):
                w0 = pl.multiple_of((i - (SB_STRAIGHT - 1)) * BK, BK)
                kw = k_ref[pl.ds(w0, SB_WIDE), :]
                lb, lm = _sb_scores(qs[b], kw, None)
                own = rel < 0
                past = SB_WIDE - BK
                lm = jnp.concatenate([lm[:, :past], jnp.where(own, lm[:, past:], 0.0)], axis=1)
                after, total = _wide_sums(lm, sufw_ref[...], True)
                on_past_keys = lambda t: jnp.concatenate([t[:, :past], jnp.where(own, t[:, past:], 0.0)], axis=1)
                a = on_past_keys(jnp.exp(lb + after)).astype(BF16)
                acc_ref[b] = _dot(a, v_ref[pl.ds(w0, SB_WIDE), :])
                cf_ref[b] = total
                a_ref[b] = a
                sg_ref[b] = on_past_keys(jnp.exp(lb)).astype(BF16)

        @pl.when(jnp.logical_not(straight))
        def _():
            cf_ref[...] = jnp.zeros_like(cf_ref)
            acc_ref[...] = jnp.zeros_like(acc_ref)

        for b, i in enumerate(blocks):
            q0 = i * TQ

            def more(c, i=i):
                return (c[0] <= i) & (c[1] > SB_EXHAUSTED)

            def step(c, b=b, i=i, q0=q0):
                k0 = pl.multiple_of((i - c[0]) * BK, BK)
                k = k_ref[pl.ds(k0, BK), :]
                v = v_ref[pl.ds(k0, BK), :]
                mask = rel < (q0 - k0)
                lb, lm = _sb_scores(qs[b], k, mask)
                cs = _split_dot(lm, sufw_ref[...])
                a = jnp.where(mask, jnp.exp(lb + cs[:, :BK] + cf_ref[b]), 0.0)
                acc_ref[b] += _dot(a.astype(BF16), v)
                cf = cf_ref[b] + cs[:, BK:]
                cf_ref[b] = cf
                return c[0] + 1, jnp.max(cf)

            n_blocks, _ = lax.while_loop(
                more, step, (jnp.where(straight, SB_STRAIGHT, 0).astype(jnp.int32), jnp.max(cf_ref[b])))
            o_ref[b * TQ:(b + 1) * TQ, :] = jnp.where(lane < HEAD_DIM, acc_ref[b, :TQ, :],
                                                     acc_ref[b, TQ:, :]).astype(BF16)
            lt_ref[b] = cf_ref[b]
            nb_ref[b] = jnp.full(nb_ref.shape[1:], n_blocks, F32)
        if side is not None:
            @pl.when((pl.program_id(0) == npair - 1) & (step_id == nq // SB_QB - 1))
            def _():
                side.finish(side_in, side_out, *scratch[2:])

    s_in, s_shape, s_out, s_alias, s_sems = _side_specs(side, 4, 5)
    wide = jax.ShapeDtypeStruct((npair, nq, 2 * TQ, SB_WIDE), BF16)
    wide_spec = pl.BlockSpec((None, SB_QB, 2 * TQ, SB_WIDE), lambda j, i: (j, i, 0, 0))
    outs = pl.pallas_call(
        body,
        out_shape=[jax.ShapeDtypeStruct((s, SB_W), BF16), jax.ShapeDtypeStruct((npair, nq, 2 * TQ, BK), F32),
                   jax.ShapeDtypeStruct((npair, nq, 8, LANES), F32), wide, wide] + s_shape,
        grid=(npair, nq // SB_QB),
        in_specs=[pl.BlockSpec((SB_QB * TQ, LANES), lambda j, i: (i, j)),
                  pl.BlockSpec((s, LANES), lambda j, i: (0, npair + j)),
                  pl.BlockSpec((s, LANES), lambda j, i: (0, 2 * npair + j)),
                  pl.BlockSpec(sufw.shape, lambda j, i: (0, 0))] + s_in,
        out_specs=[pl.BlockSpec((SB_QB * TQ, LANES), lambda j, i: (i, j)),
                   pl.BlockSpec((None, SB_QB, 2 * TQ, BK), lambda j, i: (j, i, 0, 0)),
                   pl.BlockSpec((None, SB_QB, 8, LANES), lambda j, i: (j, i, 0, 0)), wide_spec, wide_spec] + s_out,
        input_output_aliases=s_alias,
        scratch_shapes=[pltpu.VMEM((SB_QB, 2 * TQ, BK), F32), pltpu.VMEM((SB_QB, 2 * TQ, LANES), F32)] + s_sems,
        compiler_params=_params(2), name=name)(qkv, qkv, qkv, sufw, *(side.arrays if side else ()))
    return outs[:5], outs[5:]


def _sb_bwd(qkv, lt, nb, a_wide, sg_wide, doa, name, side=None):
    s = qkv.shape[0]
    nq = s // TQ
    npair = SB_W // LANES
    _, prew, prexw = _sb_wide_consts()

    def body(q_ref, k_ref, v_ref, lt_ref, nb_ref, do_ref, prew_ref, prexw_ref, a_ref, sg_ref, *rest):
        side_in, (dq_ref, dk_out, dv_out), side_out, scratch = _side_refs(side, rest, 3)
        cp_ref, ce_ref, dqa_ref, dk_ref, dv_ref = scratch[:5]
        sems = scratch[5:]
        step_id = pl.program_id(1)
        if side is not None:
            @pl.when((pl.program_id(0) == 0) & (step_id == 0))
            def _():
                side.start(side_in, side_out, *sems)
        lane = lax.broadcasted_iota(jnp.int32, (TQ, LANES), 1)
        rel = _sb_rel()
        blocks = [step_id * SB_QB + b for b in range(SB_QB)]
        rows = [slice(b * TQ, (b + 1) * TQ) for b in range(SB_QB)]
        qs = [_pair_stack(q_ref[rows[b], :] * SCALE, lane) for b in range(SB_QB)]
        dos = [_pair_stack(do_ref[rows[b], :], lane) for b in range(SB_QB)]
        n_blocks = [jnp.clip(jnp.max(nb_ref[b]).astype(jnp.int32), 1, i + 1) for b, i in enumerate(blocks)]
        first = [i + 1 - n for i, n in zip(blocks, n_blocks)]

        @pl.when(step_id == 0)
        def _():
            dk_ref[...] = jnp.zeros_like(dk_ref)
            dv_ref[...] = jnp.zeros_like(dv_ref)

        straight = n_blocks[0] == SB_STRAIGHT
        for n in n_blocks[1:]:
            straight = straight & (n == SB_STRAIGHT)

        @pl.when(straight)
        def _():
            for b in range(SB_QB):
                w0 = pl.multiple_of(first[b] * BK, BK)
                kw = k_ref[pl.ds(w0, SB_WIDE), :]
                vw = v_ref[pl.ds(w0, SB_WIDE), :]
                a = a_ref[b]
                e = a.astype(F32) * _dot_nt(dos[b], vw)
                big_e, _ = _wide_sums(e, prexw_ref[...], False, two_pass=False)
                dz = (e - sg_ref[b].astype(F32) * (e + big_e)).astype(BF16)
                dk_ref[pl.ds(w0, SB_WIDE), :] += _dot_tn(dz, qs[b])
                dv_ref[pl.ds(w0, SB_WIDE), :] += _dot_tn(a, dos[b])
                dqa_ref[b] = _dot(dz, kw)

        @pl.when(jnp.logical_not(straight))
        def _():
            cp_ref[...] = jnp.zeros_like(cp_ref)
            ce_ref[...] = jnp.zeros_like(ce_ref)
            dqa_ref[...] = jnp.zeros_like(dqa_ref)
            for b, i in enumerate(blocks):
                q0 = i * TQ

                def step(it, carry, b=b, q0=q0):
                    k0 = pl.multiple_of((first[b] + it) * BK, BK)
                    k = k_ref[pl.ds(k0, BK), :]
                    v = v_ref[pl.ds(k0, BK), :]
                    mask = rel < (q0 - k0)
                    lb, lm = _sb_scores(qs[b], k, mask)
                    cs = _split_dot(lm, prew_ref[...])
                    a = jnp.where(mask, jnp.exp(lb + (lt_ref[b] - (cs[:, :BK] + cp_ref[b]))), 0.0)
                    e = a * _dot_nt(dos[b], v)
                    ce = _split_dot(e, prexw_ref[...], two_pass=False)
                    big_e = ce[:, :BK] + ce_ref[b]
                    dz = jnp.where(mask, e - jnp.exp(lb) * (e + big_e), 0.0).astype(BF16)
                    dk_ref[pl.ds(k0, BK), :] += _dot_tn(dz, qs[b])
                    dv_ref[pl.ds(k0, BK), :] += _dot_tn(a.astype(BF16), dos[b])
                    dqa_ref[b] += _dot(dz, k)
                    cp_ref[b] += cs[:, BK:]
                    ce_ref[b] += ce[:, BK:]
                    return carry

                lax.fori_loop(0, n_blocks[b], step, 0)

        for b in range(SB_QB):
            dq = jnp.where(lane < HEAD_DIM, dqa_ref[b, :TQ, :], dqa_ref[b, TQ:, :])
            dq_ref[rows[b], :] = (dq * SCALE).astype(BF16)

        @pl.when(step_id == nq // SB_QB - 1)
        def _():
            dk_out[...] = dk_ref[...].astype(BF16)
            dv_out[...] = dv_ref[...].astype(BF16)

        if side is not None:
            @pl.when((pl.program_id(0) == npair - 1) & (step_id == nq // SB_QB - 1))
            def _():
                side.finish(side_in, side_out, *sems)

    s_in, s_shape, s_out, s_alias, s_sems = _side_specs(side, 10, 3)
    wide_spec = pl.BlockSpec((None, SB_QB, 2 * TQ, SB_WIDE), lambda j, i: (j, i, 0, 0))
    outs = pl.pallas_call(
        body,
        out_shape=[jax.ShapeDtypeStruct((s, SB_W), BF16)] * 3 + s_shape,
        grid=(npair, nq // SB_QB),
        in_specs=[pl.BlockSpec((SB_QB * TQ, LANES), lambda j, i: (i, j)),
                  pl.BlockSpec((s, LANES), lambda j, i: (0, npair + j)),
                  pl.BlockSpec((s, LANES), lambda j, i: (0, 2 * npair + j)),
                  pl.BlockSpec((None, SB_QB, 2 * TQ, BK), lambda j, i: (j, i, 0, 0)),
                  pl.BlockSpec((None, SB_QB, 8, LANES), lambda j, i: (j, i, 0, 0)),
                  pl.BlockSpec((SB_QB * TQ, LANES), lambda j, i: (i, j)),
                  pl.BlockSpec(prew.shape, lambda j, i: (0, 0)),
                  pl.BlockSpec(prexw.shape, lambda j, i: (0, 0)), wide_spec, wide_spec] + s_in,
        out_specs=[pl.BlockSpec((SB_QB * TQ, LANES), lambda j, i: (i, j)),
                   pl.BlockSpec((s, LANES), lambda j, i: (0, j)),
                   pl.BlockSpec((s, LANES), lambda j, i: (0, j))] + s_out,
        input_output_aliases=s_alias,
        scratch_shapes=[pltpu.VMEM((SB_QB, 2 * TQ, BK), F32), pltpu.VMEM((SB_QB, 2 * TQ, BK), F32),
                        pltpu.VMEM((SB_QB, 2 * TQ, LANES), F32), pltpu.VMEM((s, LANES), F32),
                        pltpu.VMEM((s, LANES), F32)] + s_sems,
        compiler_params=_params(2), name=name)(qkv, qkv, qkv, lt, nb, doa, prew, prexw, a_wide, sg_wide,
                                               *(side.arrays if side else ()))
    return outs[0], outs[1], outs[2], outs[3:]


def _bucket_table():
    i = np.arange(TQ)[:, None]
    j = np.arange(2 * BK)[None, :]
    dist = np.maximum(TQ + i - j, 0)
    max_exact = N_BUCKETS // 2
    df = np.maximum(dist, 1).astype(np.float32)
    large = max_exact + (np.log(df / np.float32(max_exact)) / np.float32(math.log(MAX_DISTANCE / max_exact))
                         * np.float32(N_BUCKETS - max_exact)).astype(np.int32)
    large = np.minimum(large, N_BUCKETS - 1)
    return np.where(dist < max_exact, dist, large).astype(np.int32)


def _swa_align_in(t, lane, g):
    tf = t.astype(F32)
    tr = pltpu.roll(tf, HEAD_DIM, 1)
    gmask = (lane >= HEAD_DIM) == (g == 1)
    top = jnp.where(gmask, jnp.where(g == 0, tf, tr), 0.0)
    bot = jnp.where(gmask, jnp.where(g == 1, tf, tr), 0.0)
    return jnp.concatenate([top, bot], axis=0).astype(BF16)


def _swa_align_out(t, lane, g):
    top, bot = t[:TQ, :], t[TQ:, :]
    top = jnp.where(g == 0, top, pltpu.roll(top, HEAD_DIM, 1))
    bot = jnp.where(g == 1, bot, pltpu.roll(bot, HEAD_DIM, 1))
    return jnp.where(lane < HEAD_DIM, top, bot)


def _swa_bias(bias_ref, bucket_ref, rb_ref, j):
    dist = TQ + lax.broadcasted_iota(jnp.int32, (TQ, 2 * BK), 0) - lax.broadcasted_iota(jnp.int32, (TQ, 2 * BK), 1)
    window = (dist >= 0) & (dist < WINDOW)
    for hh in range(2):
        def add(b, acc):
            return acc + jnp.where(bucket_ref[...] == b, rb_ref[b, 2 * j + hh], 0.0)
        bias = lax.fori_loop(0, N_BUCKETS, add, jnp.zeros((TQ, 2 * BK), F32))
        bias_ref[hh * TQ:(hh + 1) * TQ, :] = jnp.where(window, bias, NEG)


def _swa_probs(qs, k2, bias, own_block, sink_ref, i, j):
    s = _dot_nt(qs, k2) + bias
    s = jnp.where(own_block | (i > 0), s, NEG)
    row1 = lax.broadcasted_iota(jnp.int32, (2 * TQ, 1), 0)
    sink = jnp.where(row1 < TQ, sink_ref[2 * j], sink_ref[2 * j + 1])
    m = jnp.maximum(jnp.max(s, axis=1, keepdims=True), sink)
    e = jnp.exp(s - m)
    es = jnp.exp(sink - m)
    inv = 1.0 / (jnp.sum(e, axis=1, keepdims=True) + es)
    return e * inv, es * inv


def _swa_kv(ref, i):
    prev = pl.multiple_of(jnp.maximum(i - 1, 0) * BK, BK)
    cur = pl.multiple_of(i * BK, BK)
    return jnp.concatenate([ref[pl.ds(prev, BK), :], ref[pl.ds(cur, BK), :]], axis=0), prev, cur


def _swa_fwd(qkv, sinks, rel_bias, name, side=None):
    s = qkv.shape[0]
    nq = s // TQ
    npair = SW_QW // LANES
    qcol = 3 * SB_W // LANES
    bucket = jnp.asarray(_bucket_table())

    def body(q_ref, k_ref, v_ref, bucket_ref, sink_ref, rb_ref, *rest):
        side_in, (o_ref, p_ref), side_out, scratch = _side_refs(side, rest, 2)
        bias_ref, sems = scratch[0], scratch[1:]
        j = pl.program_id(0)
        step = pl.program_id(1)
        if side is not None:
            @pl.when((j == 0) & (step == 0))
            def _():
                side.start(side_in, side_out, *sems)
        g = j // 2
        lane = lax.broadcasted_iota(jnp.int32, (TQ, LANES), 1)

        @pl.when(step == 0)
        def _():
            _swa_bias(bias_ref, bucket_ref, rb_ref, j)

        own_block = lax.broadcasted_iota(jnp.int32, (2 * TQ, 2 * BK), 1) >= BK
        for b in range(SWA_QB):
            i = step * SWA_QB + b
            rows = slice(b * TQ, (b + 1) * TQ)
            qs = _swa_align_in(q_ref[rows, :] * SCALE, lane, g)
            k2, _, _ = _swa_kv(k_ref, i)
            v2, _, _ = _swa_kv(v_ref, i)
            pr, psink = _swa_probs(qs, k2, bias_ref[...], own_block, sink_ref, i, j)
            prb = pr.astype(BF16)
            o_ref[rows, :] = _swa_align_out(_dot(prb, v2), lane, g).astype(BF16)
            p_ref[b, :, :2 * BK] = prb
            p_ref[b, :, 2 * BK:] = jnp.broadcast_to(psink, (2 * TQ, BK)).astype(BF16)
        if side is not None:
            @pl.when((j == npair - 1) & (step == nq // SWA_QB - 1))
            def _():
                side.finish(side_in, side_out, *sems)

    assert nq % SWA_QB == 0
    s_in, s_shape, s_out, s_alias, s_sems = _side_specs(side, 6, 2)
    outs = pl.pallas_call(
        body,
        out_shape=[jax.ShapeDtypeStruct((s, SW_QW), BF16),
                   jax.ShapeDtypeStruct((npair, nq, 2 * TQ, 3 * BK), BF16)] + s_shape,
        grid=(npair, nq // SWA_QB),
        in_specs=[pl.BlockSpec((SWA_QB * TQ, LANES), lambda j, i: (i, qcol + j)),
                  pl.BlockSpec((s, LANES), lambda j, i: (0, qcol + npair)),
                  pl.BlockSpec((s, LANES), lambda j, i: (0, qcol + npair + 1)),
                  pl.BlockSpec((TQ, 2 * BK), lambda j, i: (0, 0)),
                  pl.BlockSpec(memory_space=pltpu.SMEM),
                  pl.BlockSpec(memory_space=pltpu.SMEM)] + s_in,
        out_specs=[pl.BlockSpec((SWA_QB * TQ, LANES), lambda j, i: (i, j)),
                   pl.BlockSpec((None, SWA_QB, 2 * TQ, 3 * BK), lambda j, i: (j, i, 0, 0))] + s_out,
        input_output_aliases=s_alias,
        scratch_shapes=[pltpu.VMEM((2 * TQ, 2 * BK), F32)] + s_sems,
        compiler_params=_params(2), name=name)(qkv, qkv, qkv, bucket, sinks, rel_bias,
                                               *(side.arrays if side else ()))
    return outs[:2], outs[2:]


def _swa_bwd(qkv, ob, dob, probs, name, side=None):
    s = qkv.shape[0]
    nq = s // TQ
    npair = SW_QW // LANES
    qcol = 3 * SB_W // LANES
    bucket = jnp.asarray(_bucket_table())

    def body(q_ref, k_ref, v_ref, o_ref, do_ref, bucket_ref, p_ref, *rest):
        side_in, (dq_ref, dkv_ref, dsink_ref, drel_ref), side_out, scratch = _side_refs(side, rest, 4)
        dsacc_ref, dk_ref, dv_ref = scratch[:3]
        sems = scratch[3:]
        j = pl.program_id(0)
        step = pl.program_id(1)
        if side is not None:
            @pl.when((j == 0) & (step == 0))
            def _():
                side.start(side_in, side_out, *sems)
        g = j // 2
        lane = lax.broadcasted_iota(jnp.int32, (TQ, LANES), 1)
        row8 = lax.broadcasted_iota(jnp.int32, (SW_HEADS, LANES), 0)
        lane8 = lax.broadcasted_iota(jnp.int32, (SW_HEADS, LANES), 1)

        @pl.when((step == 0) & (j == 0))
        def _():
            dk_ref[...] = jnp.zeros_like(dk_ref)
            dv_ref[...] = jnp.zeros_like(dv_ref)
            dsink_ref[...] = jnp.zeros_like(dsink_ref)
            drel_ref[...] = jnp.zeros_like(drel_ref)

        @pl.when(step == 0)
        def _():
            dsacc_ref[...] = jnp.zeros_like(dsacc_ref)

        ds_sum = jnp.zeros(dsacc_ref.shape, F32)
        dsink = jnp.zeros((SW_HEADS, LANES), F32)
        for b in range(SWA_QB):
            i = step * SWA_QB + b
            rows = slice(b * TQ, (b + 1) * TQ)
            qs = _swa_align_in(q_ref[rows, :] * SCALE, lane, g)
            do = do_ref[rows, :]
            dos = _swa_align_in(do, lane, g)
            dof = do.astype(F32) * o_ref[rows, :].astype(F32)
            d0 = jnp.sum(jnp.where(lane < HEAD_DIM, dof, 0.0), axis=1, keepdims=True)
            d1 = jnp.sum(jnp.where(lane >= HEAD_DIM, dof, 0.0), axis=1, keepdims=True)
            delta = jnp.concatenate([d0, d1], axis=0)
            k2, prev, cur = _swa_kv(k_ref, i)
            v2, _, _ = _swa_kv(v_ref, i)
            prb = p_ref[b, :, :2 * BK]
            ds = prb.astype(F32) * (_dot_nt(dos, v2) - delta)
            ds_sum = ds_sum + ds
            sd = p_ref[b, :, 2 * BK:].astype(F32) * delta
            ds0 = -jnp.sum(sd[:TQ, :], axis=0, keepdims=True)
            ds1 = -jnp.sum(sd[TQ:, :], axis=0, keepdims=True)
            dsink = dsink + jnp.where(row8 == 2 * j, ds0, jnp.where(row8 == 2 * j + 1, ds1, 0.0))
            dsb = ds.astype(BF16)
            dq_ref[rows, :] = _swa_align_out(_dot(dsb, k2) * SCALE, lane, g).astype(BF16)
            dk2 = _dot_tn(dsb, qs)
            dv2 = _dot_tn(prb, dos)
            dk_ref[pl.ds(prev, BK), :] += dk2[:BK, :]
            dk_ref[pl.ds(cur, BK), :] += dk2[BK:, :]
            dv_ref[pl.ds(prev, BK), :] += dv2[:BK, :]
            dv_ref[pl.ds(cur, BK), :] += dv2[BK:, :]
        dsacc_ref[...] += ds_sum
        dsink_ref[...] += dsink

        @pl.when(step == nq // SWA_QB - 1)
        def _():
            for hh in range(2):
                def red(b, acc):
                    val = _sum_all(jnp.where(bucket_ref[...] == b, dsacc_ref[hh * TQ:(hh + 1) * TQ, :], 0.0))
                    return jnp.where((row8 == 2 * j + hh) & (lane8 == b), val, acc)
                drel_ref[...] += lax.fori_loop(0, N_BUCKETS, red, jnp.zeros((SW_HEADS, LANES), F32))

        @pl.when((step == nq // SWA_QB - 1) & (j == npair - 1))
        def _():
            dkv_ref[:, :LANES] = dk_ref[...].astype(BF16)
            dkv_ref[:, LANES:] = dv_ref[...].astype(BF16)
            if side is not None:
                side.finish(side_in, side_out, *sems)

    whole = lambda j, i: (0, 0)
    s_in, s_shape, s_out, s_alias, s_sems = _side_specs(side, 7, 4)
    outs = pl.pallas_call(
        body,
        out_shape=[jax.ShapeDtypeStruct((s, SW_QW), BF16), jax.ShapeDtypeStruct((s, 2 * LANES), BF16),
                   jax.ShapeDtypeStruct((SW_HEADS, LANES), F32), jax.ShapeDtypeStruct((SW_HEADS, LANES), F32)] + s_shape,
        grid=(npair, nq // SWA_QB),
        in_specs=[pl.BlockSpec((SWA_QB * TQ, LANES), lambda j, i: (i, qcol + j)),
                  pl.BlockSpec((s, LANES), lambda j, i: (0, qcol + npair)),
                  pl.BlockSpec((s, LANES), lambda j, i: (0, qcol + npair + 1)),
                  pl.BlockSpec((SWA_QB * TQ, LANES), lambda j, i: (i, j)),
                  pl.BlockSpec((SWA_QB * TQ, LANES), lambda j, i: (i, j)),
                  pl.BlockSpec((TQ, 2 * BK), whole),
                  pl.BlockSpec((None, SWA_QB, 2 * TQ, 3 * BK), lambda j, i: (j, i, 0, 0))] + s_in,
        out_specs=[pl.BlockSpec((SWA_QB * TQ, LANES), lambda j, i: (i, j)),
                   pl.BlockSpec((s, 2 * LANES), whole),
                   pl.BlockSpec((SW_HEADS, LANES), whole), pl.BlockSpec((SW_HEADS, LANES), whole)] + s_out,
        input_output_aliases=s_alias,
        scratch_shapes=[pltpu.VMEM((2 * TQ, 2 * BK), F32), pltpu.VMEM((s, LANES), F32),
                        pltpu.VMEM((s, LANES), F32)] + s_sems,
        compiler_params=_params(2), name=name)(qkv, qkv, qkv, ob, dob, bucket, probs,
                                               *(side.arrays if side else ()))
    return outs[:4], outs[4:]


def _acc_init(i, *refs):
    @pl.when(i == 0)
    def _():
        for r in refs:
            r[...] = jnp.zeros_like(r)


def _loss_bwd(x3, target, g, name):
    d = x3.shape[1]

    def body(x_ref, t_ref, g_ref, dx_ref, dg_ref, loss_ref):
        _acc_init(pl.program_id(0), dg_ref, loss_ref)
        x = x_ref[...]
        gv = g_ref[...]
        diff = _rms(x, gv) - t_ref[...]
        loss_ref[...] += 0.5 * jnp.sum(jnp.mean(jnp.square(diff), axis=-1, keepdims=True), axis=0, keepdims=True)
        dx, dg = _rms_bwd(diff * (1.0 / d), x, gv)
        dx_ref[...] = dx
        dg_ref[...] += dg

    return _rowcall(name, body, [x3, target], [g], [(d, F32)], [((1, d), F32), ((1, LANES), F32)])


def _ple_bwd(dx3, pe, gt, x2, g, wpg, name):
    d = x2.shape[1]

    def body(dx3_ref, pe_ref, gt_ref, x_ref, g_ref, w_ref, dpe_ref, dgt_ref, dx2_ref, dg_ref):
        _acc_init(pl.program_id(0), dg_ref)
        dx3 = dx3_ref[...]
        sg = _sigmoid(gt_ref[...])
        dpe_ref[...] = (dx3 * sg).astype(BF16)
        dgt = (dx3 * pe_ref[...] * sg * (1.0 - sg)).astype(BF16)
        dgt_ref[...] = dgt
        dx, dg = _rms_bwd(_dot_nt(dgt, w_ref[...]), x_ref[...], g_ref[...])
        dx2_ref[...] = dx3 + dx
        dg_ref[...] += dg

    return _rowcall(name, body, [dx3, pe, gt, x2], [g, wpg], [(d, BF16), (d, BF16), (d, F32)], [((1, d), F32)])


def _ff_bwd(dx2, r, x1, g, w1, w2, name, side=None):
    d = x1.shape[1]
    dff = r.shape[1]

    def body(dx2_ref, r_ref, x_ref, g_ref, w1_ref, w2_ref, du_ref, dxb_ref, dx1_ref, dx1b_ref, dg_ref):
        _acc_init(pl.program_id(0), dg_ref)
        dx2 = dx2_ref[...]
        dxb = dx2.astype(BF16)
        dxb_ref[...] = dxb
        du = (_dot_nt(dxb, w2_ref[...]) * (2.0 * r_ref[...].astype(F32))).astype(BF16)
        du_ref[...] = du
        dx, dg = _rms_bwd(_dot_cols_t(du, w1_ref), x_ref[...], g_ref[...])
        dx1 = dx2 + dx
        dx1_ref[...] = dx1
        dx1b_ref[...] = dx1.astype(BF16)
        dg_ref[...] += dg

    return _rowcall(name, body, [dx2, r, x1], [g, w1, w2], [(dff, BF16), (d, BF16), (d, F32), (d, BF16)],
                    [((1, d), F32)], tm=256, side=side)


def _mixer_bwd(dx1b, gates, oa, ob, wo, wua, wub, name, side=None):
    d = dx1b.shape[1]

    def body(dx_ref, gate_ref, oa_ref, ob_ref, wo_ref, wua_ref, wub_ref,
             dya_ref, dyb_ref, dgate_ref, doa_ref, dob_ref):
        dm = _dot_nt(dx_ref[...], wo_ref[...])
        sa = _sigmoid(gate_ref[:, :d])
        sb = _sigmoid(gate_ref[:, d:])
        ya = _dot_cols(oa_ref[...], wua_ref)
        yb = _dot_cols(ob_ref[...], wub_ref)
        dya = (dm * sa).astype(BF16)
        dyb = (dm * sb).astype(BF16)
        dya_ref[...] = dya
        dyb_ref[...] = dyb
        dgate_ref[:, :d] = (dm * ya * sa * (1.0 - sa)).astype(BF16)
        dgate_ref[:, d:] = (dm * yb * sb * (1.0 - sb)).astype(BF16)
        doa_ref[...] = _dot_cols_t(dya, wua_ref).astype(BF16)
        dob_ref[...] = _dot_cols_t(dyb, wub_ref).astype(BF16)

    return _rowcall(name, body, [dx1b, gates, oa, ob], [wo, wua, wub],
                    [(d, BF16), (d, BF16), (2 * d, BF16), (SB_W, BF16), (SW_QW, BF16)], side=side)


def _inproj_bwd(pieces, dx1, x, g, wt, name, side=None):
    d = x.shape[1]
    n = len(pieces)
    offsets = [sum(pc.shape[1] for pc in pieces[:p]) for p in range(n + 1)]

    def body(*refs):
        dx1_ref, x_ref, g_ref, w_ref, dx_ref, dg_ref = refs[n:]
        _acc_init(pl.program_id(0), dg_ref)
        dh = _dot(refs[0][...], w_ref[:offsets[1], :])
        for p in range(1, n):
            dh = dh + _dot(refs[p][...], w_ref[offsets[p]:offsets[p + 1], :])
        dx, dg = _rms_bwd(dh, x_ref[...], g_ref[...])
        dx_ref[...] = dx1_ref[...] + dx
        dg_ref[...] += dg

    return _rowcall(name, body, list(pieces) + [dx1, x], [g, wt], [(d, F32)], [((1, d), F32)], side=side)


def _tile(n, cap):
    assert n % LANES == 0
    return max(t for t in range(LANES, min(n, cap) + 1, LANES) if n % t == 0)


def _mm_tn(a, b, name, nshard=1):
    s, ka = a.shape
    nb = b.shape[1]
    n = nb // nshard
    ta = _tile(ka, 512)
    tb = _tile(n, 1024)
    per = n // tb

    def body(a_ref, b_ref, o_ref):
        o_ref[...] = _dot_tn(a_ref[...].astype(BF16), b_ref[...].astype(BF16))

    return pl.pallas_call(
        body, out_shape=jax.ShapeDtypeStruct((nshard, ka, n), F32), grid=(nb // tb, ka // ta),
        in_specs=[pl.BlockSpec((s, ta), lambda jb, ia: (0, ia)), pl.BlockSpec((s, tb), lambda jb, ia: (0, jb))],
        out_specs=pl.BlockSpec((None, ta, tb), lambda jb, ia: (jb // per, ia, jb % per)),
        compiler_params=_params(2), name=name)(a, b)


def _mm_tn_pieces(pieces, b, name, side=None):
    s, nb = b.shape
    ta = 256
    n_in = len(pieces) + 1
    tiles = [pc.shape[1] // ta for pc in pieces]
    assert all(pc.shape[1] % ta == 0 for pc in pieces)
    starts = [sum(tiles[:p]) for p in range(len(pieces))]
    tb = _tile(nb, 1024)
    grid = (nb // tb, sum(tiles))

    def body(*refs):
        a_refs, b_ref = refs[:n_in - 1], refs[n_in - 1]
        side_in, (o_ref,), side_out, sems = _side_refs(side, refs[n_in:], 1)
        jb, ia = pl.program_id(0), pl.program_id(1)
        if side is not None:
            @pl.when((jb == 0) & (ia == 0))
            def _():
                side.start(side_in, side_out, *sems)
        for p in range(len(pieces)):
            @pl.when((ia >= starts[p]) & (ia < starts[p] + tiles[p]))
            def _(p=p):
                o_ref[...] = _dot_tn(a_refs[p][...], b_ref[...])
        if side is not None:
            @pl.when((jb == grid[0] - 1) & (ia == grid[1] - 1))
            def _():
                side.finish(side_in, side_out, *sems)

    def piece_spec(p):
        return pl.BlockSpec((s, ta), lambda jb, ia: (0, jnp.clip(ia - starts[p], 0, tiles[p] - 1)))

    s_in, s_shape, s_out, s_alias, s_sems = _side_specs(side, n_in, 1)
    outs = pl.pallas_call(
        body, out_shape=[jax.ShapeDtypeStruct((sum(tiles) * ta, nb), F32)] + s_shape, grid=grid,
        in_specs=[piece_spec(p) for p in range(len(pieces))] + [pl.BlockSpec((s, tb), lambda jb, ia: (0, jb))] + s_in,
        out_specs=[pl.BlockSpec((ta, tb), lambda jb, ia: (ia, jb))] + s_out,
        input_output_aliases=s_alias, scratch_shapes=s_sems,
        compiler_params=_params(2), name=name)(*pieces, b, *(side.arrays if side else ()))
    return outs[0] if side is None else (outs[0], outs[1:])


def _place():
    return lax.axis_index("x"), lax.axis_index("y"), lax.axis_index("c")


def _chip_peer(x, y, k):
    return (x ^ (k >> 1), y ^ (k & 1))


def _row_tile(k, cap=544):
    return max(t for t in range(32, min(k, cap) + 1, 32) if k % t == 0)


def _cast_bf16(w, r, name):
    l, k, n = w.shape
    assert l == 2
    tk = _row_tile(k)

    def body(r_ref, w_ref, o0_ref, o1_ref):
        o0_ref[...] = w_ref[0].astype(BF16)
        o1_ref[...] = w_ref[1].astype(BF16)

    out_spec = pl.BlockSpec((None, tk, n), lambda i, r_ref: (r_ref[0], i, 0))
    return pl.pallas_call(
        body, out_shape=[jax.ShapeDtypeStruct((4, k, n), BF16)] * 2,
        grid_spec=pltpu.PrefetchScalarGridSpec(
            num_scalar_prefetch=1, grid=(k // tk,),
            in_specs=[pl.BlockSpec((l, tk, n), lambda i, r_ref: (0, i, 0))],
            out_specs=[out_spec, out_spec]),
        compiler_params=_params(1), name=name)(r, w)


class _Exchange(NamedTuple):
    arrays: tuple
    aliased: tuple
    sems: tuple
    start: Callable
    finish: Callable


def _both(a, b):
    na, ma, ka = len(a.arrays), len(a.aliased), 2 * len(a.sems)

    def start(ins, outs, *sems):
        a.start(ins[:na], outs[:ma], *sems[:ka])
        b.start(ins[na:], outs[ma:], *sems[ka:])

    def finish(ins, outs, *sems):
        a.finish(ins[:na], outs[:ma], *sems[:ka])
        b.finish(ins[na:], outs[ma:], *sems[ka:])

    return _Exchange(a.arrays + b.arrays, a.aliased + tuple(na + i for i in b.aliased), a.sems + b.sems, start, finish)


def _all_gather(bufs):
    nt = len(bufs)

    def ici(t, ins, outs, send_sems, recv_sems, x, y, c, r, k):
        return pltpu.make_async_remote_copy(
            src_ref=ins[t].at[r, c], dst_ref=outs[t].at[r, c], send_sem=send_sems.at[t, k - 1],
            recv_sem=recv_sems.at[t, k - 1], device_id=(*_chip_peer(x, y, k), c), device_id_type=MESH)

    def d2d(t, outs, send_sems, recv_sems, x, y, c, r, k, half):
        slab = outs[t].at[r ^ k, half]
        return pltpu.make_async_remote_copy(
            src_ref=slab, dst_ref=slab, send_sem=send_sems.at[t, 2 + k], recv_sem=recv_sems.at[t, 2 + k],
            device_id=(x, y, 1 - c), device_id_type=MESH)

    def start(ins, outs, send_sems, recv_sems):
        x, y, c = _place()
        r = 2 * x + y
        for t in range(nt):
            for k in (1, 2, 3):
                ici(t, ins, outs, send_sems, recv_sems, x, y, c, r, k).start()

    def finish(ins, outs, send_sems, recv_sems):
        x, y, c = _place()
        r = 2 * x + y
        for t in range(nt):
            for k in (1, 2, 3):
                slab = outs[t].at[r ^ k, c]
                pltpu.make_async_remote_copy(
                    src_ref=slab, dst_ref=slab, send_sem=send_sems.at[t, k - 1], recv_sem=recv_sems.at[t, k - 1],
                    device_id=(x, y, 1 - c), device_id_type=MESH).wait_recv()
                d2d(t, outs, send_sems, recv_sems, x, y, c, r, k, c).start()
        for t in range(nt):
            for k in (1, 2, 3):
                d2d(t, outs, send_sems, recv_sems, x, y, c, r, k, 1 - c).wait_recv()
        for t in range(nt):
            for k in (1, 2, 3):
                ici(t, ins, outs, send_sems, recv_sems, x, y, c, r, k).wait_send()
                d2d(t, outs, send_sems, recv_sems, x, y, c, r, k, c).wait_send()

    return _Exchange(tuple(bufs), tuple(range(nt)), ((nt, 6),), start, finish)


def _run_exchange(name, ex):
    n_in, n_out = len(ex.arrays), len(ex.aliased)

    def body(*refs):
        ins, outs = refs[:n_in], refs[n_in:n_in + n_out]
        send_sems, recv_sems = refs[n_in + n_out:]
        ex.start(ins, outs, send_sems, recv_sems)
        ex.finish(ins, outs, send_sems, recv_sems)

    any_spec = pl.BlockSpec(memory_space=pl.ANY)
    return pl.pallas_call(
        body, out_shape=[jax.ShapeDtypeStruct(ex.arrays[a].shape, ex.arrays[a].dtype) for a in ex.aliased],
        in_specs=[any_spec] * n_in, out_specs=[any_spec] * n_out,
        input_output_aliases={a: o for o, a in enumerate(ex.aliased)},
        scratch_shapes=[pltpu.SemaphoreType.DMA(shape) for shape in ex.sems for _ in range(2)],
        name=name)(*ex.arrays)


def _rs_to_sibling(grads):
    nt = len(grads)
    landing = [lax.empty((4,) + g.shape[2:], F32) for g in grads]

    def copies(ins, outs, send_sems, recv_sems):
        x, y, c = _place()
        return [pltpu.make_async_remote_copy(
            src_ref=ins[t].at[:, 1 - c], dst_ref=outs[t], send_sem=send_sems.at[t], recv_sem=recv_sems.at[t],
            device_id=(x, y, 1 - c), device_id_type=MESH) for t in range(nt)]

    def start(ins, outs, send_sems, recv_sems):
        for cp in copies(ins, outs, send_sems, recv_sems):
            cp.start()

    def finish(ins, outs, send_sems, recv_sems):
        for cp in copies(ins, outs, send_sems, recv_sems):
            cp.wait()

    return _Exchange(tuple(grads) + tuple(landing), tuple(range(nt, 2 * nt)), ((nt,),), start, finish)


def _add_half(g, recv, cr, name):
    _, _, k2, n = g.shape
    tk = _row_tile(k2)

    def body(cr_ref, g_ref, r_ref, sums_ref, mine_ref):
        val = (g_ref[...] + r_ref[...]).astype(BF16)
        sums_ref[...] = val

        @pl.when(pl.program_id(1) == cr_ref[1])
        def _():
            mine_ref[...] = val

    return pl.pallas_call(
        body, out_shape=[jax.ShapeDtypeStruct((4, k2, n), BF16)] * 2,
        grid_spec=pltpu.PrefetchScalarGridSpec(
            num_scalar_prefetch=1, grid=(k2 // tk, 4),
            in_specs=[pl.BlockSpec((None, None, tk, n), lambda i, q, cr_ref: (q, cr_ref[0], i, 0)),
                      pl.BlockSpec((None, tk, n), lambda i, q, cr_ref: (q, i, 0))],
            out_specs=[pl.BlockSpec((None, tk, n), lambda i, q, cr_ref: (q, i, 0)),
                       pl.BlockSpec((None, tk, n), lambda i, q, cr_ref: (cr_ref[1], i, 0))]),
        compiler_params=_params(2), name=name)(cr, g, recv)


def _rs_to_chips(sums, parts):
    nt = len(sums)

    def copies(ins, outs, send_sems, recv_sems):
        x, y, c = _place()
        r = 2 * x + y
        return [pltpu.make_async_remote_copy(
            src_ref=ins[t].at[r ^ k], dst_ref=outs[t].at[r], send_sem=send_sems.at[t, k - 1],
            recv_sem=recv_sems.at[t, k - 1], device_id=(*_chip_peer(x, y, k), c), device_id_type=MESH)
            for t in range(nt) for k in (1, 2, 3)]

    def start(ins, outs, send_sems, recv_sems):
        for cp in copies(ins, outs, send_sems, recv_sems):
            cp.start()

    def finish(ins, outs, send_sems, recv_sems):
        for cp in copies(ins, outs, send_sems, recv_sems):
            cp.wait()

    return _Exchange(tuple(sums) + tuple(parts), tuple(range(nt, 2 * nt)), ((nt, 3),), start, finish)


def _sum4(parts, cr, name):
    _, k2, n = parts.shape
    tk = _row_tile(k2)

    def body(cr_ref, p_ref, o_ref):
        p = p_ref[...].astype(F32)
        o_ref[...] = ((p[0] + p[1]) + p[2]) + p[3]

    return pl.pallas_call(
        body, out_shape=jax.ShapeDtypeStruct((2, k2, n), F32),
        grid_spec=pltpu.PrefetchScalarGridSpec(
            num_scalar_prefetch=1, grid=(k2 // tk,),
            in_specs=[pl.BlockSpec((4, tk, n), lambda i, cr_ref: (0, i, 0))],
            out_specs=pl.BlockSpec((None, tk, n), lambda i, cr_ref: (cr_ref[0], i, 0))),
        compiler_params=_params(1), name=name)(cr, parts)


def _exchange_halves(both):
    nt = len(both)

    def copies(ins, outs, send_sems, recv_sems):
        x, y, c = _place()
        return [pltpu.make_async_remote_copy(
            src_ref=ins[t].at[c], dst_ref=outs[t].at[c], send_sem=send_sems.at[t], recv_sem=recv_sems.at[t],
            device_id=(x, y, 1 - c), device_id_type=MESH) for t in range(nt)]

    def start(ins, outs, send_sems, recv_sems):
        for cp in copies(ins, outs, send_sems, recv_sems):
            cp.start()

    def finish(ins, outs, send_sems, recv_sems):
        for cp in copies(ins, outs, send_sems, recv_sems):
            cp.wait()

    return _Exchange(tuple(both), tuple(range(nt)), ((nt,),), start, finish)


def _adamw_math(w, g, m, v):
    m = ADAM_B1 * m + (1.0 - ADAM_B1) * g
    v = ADAM_B2 * v + (1.0 - ADAM_B2) * jnp.square(g)
    m_hat = m / (1.0 - ADAM_B1 ** ADAM_STEP)
    v_hat = v / (1.0 - ADAM_B2 ** ADAM_STEP)
    delta = -ADAM_LR * (m_hat / (jnp.sqrt(v_hat) + ADAM_EPS) + ADAM_WD * w)
    return delta, m, v


def _adamw(w, m, v, g0, g1, name):
    _, k, n = w.shape
    tk = _row_tile(k)
    nk = k // tk

    def body(w_ref, m_ref, v_ref, g0_ref, g1_ref, grad_ref, delta_ref, nm_ref, nv_ref):
        g = jnp.where(pl.program_id(0) == 0, g0_ref[...], g1_ref[...])
        delta, nm, nv = _adamw_math(w_ref[...], g, m_ref[...], v_ref[...])
        grad_ref[...] = g
        delta_ref[...] = delta
        nm_ref[...] = nm
        nv_ref[...] = nv

    lay = pl.BlockSpec((None, tk, n), lambda a, i: (a, i, 0))
    g0_spec = pl.BlockSpec((tk, n), lambda a, i: (jnp.where(a == 0, i, nk - 1), 0))
    g1_spec = pl.BlockSpec((tk, n), lambda a, i: (jnp.where(a == 1, i, 0), 0))
    return pl.pallas_call(
        body, out_shape=[jax.ShapeDtypeStruct(w.shape, F32)] * 4, grid=(2, nk),
        in_specs=[lay, lay, lay, g0_spec, g1_spec], out_specs=[lay] * 4,
        compiler_params=_params(2), name=name)(w, m, v, g0, g1)


def _small_allreduce_adamw(gpart, w, m, v):
    shape = gpart.shape

    def body(g_ref, w_ref, m_ref, v_ref, gsum_ref, delta_ref, nm_ref, nv_ref, recv_ref, send_sems, recv_sems):
        x, y, c = _place()
        me = 4 * x + 2 * y + c
        recv_ref[me] = g_ref[...]
        cps = []
        for k in range(1, 8):
            peer = (x ^ (k >> 2), y ^ ((k >> 1) & 1), c ^ (k & 1))
            cp = pltpu.make_async_remote_copy(
                src_ref=g_ref, dst_ref=recv_ref.at[me], send_sem=send_sems.at[k - 1], recv_sem=recv_sems.at[k - 1],
                device_id=peer, device_id_type=MESH)
            cp.start()
            cps.append(cp)
        for cp in cps:
            cp.wait()
        g = recv_ref[0]
        for dev in range(1, 8):
            g = g + recv_ref[dev]
        delta, nm, nv = _adamw_math(w_ref[...], g, m_ref[...], v_ref[...])
        gsum_ref[...] = g
        delta_ref[...] = delta
        nm_ref[...] = nm
        nv_ref[...] = nv

    vm = pl.BlockSpec(memory_space=pltpu.VMEM)
    return pl.pallas_call(
        body, out_shape=[jax.ShapeDtypeStruct(shape, F32)] * 4, in_specs=[vm] * 4, out_specs=[vm] * 4,
        scratch_shapes=[pltpu.VMEM((8,) + shape, F32), pltpu.SemaphoreType.DMA((7,)), pltpu.SemaphoreType.DMA((7,))],
        name="small_allreduce_adamw")(gpart, w, m, v)


BIG = ("w_in", "w_up_a", "w_up_b", "w_o", "w_ff1", "w_ff2", "w_pe", "w_pg")
COL_SHARDED = ("w_in", "w_up_a", "w_up_b", "w_ff1", "w_pe")
ROW_SHARDED = ("w_o", "w_ff2", "w_pg")
SMALL_ROWS = 16


def _pack_small(g_mix, g_mlp, g_pe, g_final, sinks, rel_bias, loss=None):
    d = g_final.shape[0]
    row = lambda v: jnp.pad(v.reshape(1, -1), ((0, 0), (0, d - v.size)))
    rows = [g_mix, g_mlp, g_pe, g_final.reshape(1, d),
            jnp.zeros((1, d), F32) if loss is None else row(loss), row(sinks), row(rel_bias)]
    out = jnp.concatenate(rows, axis=0)
    return jnp.pad(out, ((0, SMALL_ROWS - out.shape[0]), (0, 0)))


def _unpack_small(a, sinks_shape, rel_shape):
    return (a[0:2], a[2:4], a[4:6], a[6], a[8, :sinks_shape[0] * sinks_shape[1]].reshape(sinks_shape),
            a[9, :rel_shape[0] * rel_shape[1]].reshape(rel_shape))


def kernel(x, p, w_in, w_up_a, w_up_b, w_o, w_ff1, w_ff2, w_pe, w_pg, g_mix, g_mlp, g_pe, g_final, sinks, rel_bias, loss_target, m_w_in, m_w_up_a, m_w_up_b, m_w_o, m_w_ff1, m_w_ff2, m_w_pe, m_w_pg, m_g_mix, m_g_mlp, m_g_pe, m_g_final, m_sinks, m_rel_bias, v_w_in, v_w_up_a, v_w_up_b, v_w_o, v_w_ff1, v_w_ff2, v_w_pe, v_w_pg, v_g_mix, v_g_mlp, v_g_pe, v_g_final, v_sinks, v_rel_bias):
    depth = w_in.shape[0]
    assert depth == 2
    x0 = x[0]
    target = loss_target[0]
    d = x0.shape[1]
    wl = dict(w_in=w_in, w_up_a=w_up_a, w_up_b=w_up_b, w_o=w_o, w_ff1=w_ff1, w_ff2=w_ff2, w_pe=w_pe, w_pg=w_pg)
    ml = dict(w_in=m_w_in, w_up_a=m_w_up_a, w_up_b=m_w_up_b, w_o=m_w_o, w_ff1=m_w_ff1, w_ff2=m_w_ff2, w_pe=m_w_pe, w_pg=m_w_pg)
    vl = dict(w_in=v_w_in, w_up_a=v_w_up_a, w_up_b=v_w_up_b, w_o=v_w_o, w_ff1=v_w_ff1, w_ff2=v_w_ff2, w_pe=v_w_pe, w_pg=v_w_pg)
    c_idx = lax.axis_index("c").astype(jnp.int32)
    r_idx = (2 * lax.axis_index("x") + lax.axis_index("y")).astype(jnp.int32)
    cr = jnp.stack([c_idx, r_idx])

    wl["w_in"], ml["w_in"], vl["w_in"] = (jnp.swapaxes(a, 1, 2) for a in (w_in, m_w_in, v_w_in))

    bufs = {}
    for n in BIG:
        k, nn = wl[n].shape[1:]
        for l, b in enumerate(_cast_bf16(wl[n], r_idx.reshape(1), "cast_" + n)):
            bufs[n, l] = b.reshape(4, 2, k // 2, nn)

    def gather(keys, run):
        for key, b in zip(keys, run(_all_gather([bufs[key] for key in keys]))):
            bufs[key] = b

    def gathered(n, l):
        _, _, k2, nn = bufs[n, l].shape
        if n in ROW_SHARDED or n == "w_in":
            return bufs[n, l].reshape(8 * k2, nn)
        return bufs[n, l].reshape(4, 2 * k2, nn)

    gather([("w_in", 0)], lambda ex: _run_exchange("all_gather_first", ex))

    full = {n: [None] * depth for n in BIG}
    saved = []
    xi = x0
    for i in range(depth):
        st = dict(x0=xi)
        gm = g_mix[i].reshape(1, d)
        full["w_in"][i] = gathered("w_in", i)
        st["h1"], st["qkv"], st["gates"] = _inproj_fwd(xi, gm, full["w_in"][i], f"inproj_fwd_{i}")

        def attend(ex):
            (st["oa"], st["lt"], st["nb"], st["a_wide"], st["sg_wide"]), filled = _sb_fwd(st["qkv"], f"sb_fwd_{i}", ex)
            return filled

        with_window = [("w_in", 1)] if i == 0 else [(n, i) for n in ("w_up_a", "w_up_b", "w_o")]
        gather([(n, i) for n in BIG if n != "w_in" and (n, i) not in with_window], attend)

        def window(ex):
            (st["ob"], st["probs"]), filled = _swa_fwd(st["qkv"], sinks[i], rel_bias, f"swa_fwd_{i}", ex)
            return filled

        gather(with_window, window)
        for n in BIG:
            if n != "w_in":
                full[n][i] = gathered(n, i)
        st["m"], st["x1"] = _mixer_fwd(st["oa"], st["ob"], st["gates"], xi, full["w_up_a"][i], full["w_up_b"][i],
                                       full["w_o"][i], f"mixer_fwd_{i}")
        st["h2"], st["u"], st["a"], st["x2"] = _ff_fwd(st["x1"], g_mlp[i].reshape(1, d), full["w_ff1"][i],
                                                       full["w_ff2"][i], f"ff_fwd_{i}")
        st["pb"], st["h3"], st["pe"], st["gt"], xi = _ple_fwd(p[i, 0], st["x2"], g_pe[i].reshape(1, d),
                                                            full["w_pe"][i], full["w_pg"][i], f"ple_fwd_{i}")
        saved.append(st)

    dx, dg_final, loss_part = _loss_bwd(xi, target, g_final.reshape(1, d), "loss_bwd")
    gw = {n: [None] * depth for n in BIG}
    reduced = {}

    chip_sums = {}

    def by_halves(keys):
        tensors = []
        for n, l in keys:
            g = gw[n][l]
            if n in ROW_SHARDED:
                ka, nb = g.shape[1:]
                g = g.reshape(4, ka // 4, nb)
            _, k, nn = g.shape
            tensors.append(g.reshape(4, 2, k // 2, nn))
        return tensors

    def add_halves(keys, tensors, landed):
        for (n, l), g, r in zip(keys, tensors, landed):
            chip_sums[n, l] = _add_half(g, r, cr, f"add_half_{n}_{l}")

    def to_sibling(keys, run):
        tensors = by_halves(keys)
        add_halves(keys, tensors, run(_rs_to_sibling(tensors)))

    def to_chips(keys):
        return _rs_to_chips([chip_sums[k][0] for k in keys], [chip_sums[k][1] for k in keys])

    halves = {}

    def sum_chips(keys, parts):
        for (n, l), pc in zip(keys, parts):
            halves[n, l] = _sum4(pc, cr, f"sum4_{n}_{l}")

    def swap_halves(keys):
        def store(filled):
            for key, both in zip(keys, filled):
                reduced[key] = both
        return _exchange_halves([halves[k] for k in keys]), store

    dg_mix, dg_mlp, dg_pe, dsinks = [None] * depth, [None] * depth, [None] * depth, [None] * depth
    drel = jnp.zeros((SW_HEADS, LANES), F32)
    for i in reversed(range(depth)):
        st = saved[i]
        dpe, dgt, dx2, dg_pe[i] = _ple_bwd(dx, st["pe"], st["gt"], st["x2"], g_pe[i].reshape(1, d),
                                           full["w_pg"][i], f"ple_bwd_{i}")
        gw["w_pe"][i] = _mm_tn(st["pb"], dpe, f"dw_pe_{i}", 4)
        gw["w_pg"][i] = _mm_tn(st["h3"], dgt, f"dw_pg_{i}")
        ff_args = (dx2, st["u"], st["x1"], g_mlp[i].reshape(1, d), full["w_ff1"][i], full["w_ff2"][i], f"ff_bwd_{i}")
        if i == 0:
            (du, dx2b, dx1, dx1b, dg_mlp[i]), parts = _ff_bwd(*ff_args, to_chips([("w_in", 1)]))
            sum_chips([("w_in", 1)], parts)
        else:
            du, dx2b, dx1, dx1b, dg_mlp[i] = _ff_bwd(*ff_args)
        gw["w_ff2"][i] = _mm_tn(st["a"], dx2b, f"dw_ff2_{i}")
        gw["w_ff1"][i] = _mm_tn(st["h2"], du, f"dw_ff1_{i}", 4)
        gw["w_o"][i] = _mm_tn(st["m"], dx1b, f"dw_o_{i}")
        early = [(n, i) for n in ("w_pe", "w_pg", "w_ff2", "w_ff1", "w_o")]

        def mixer(ex):
            (dya, dyb, dgates, doa, dob), landed = _mixer_bwd(
                dx1b, st["gates"], st["oa"], st["ob"], full["w_o"][i], full["w_up_a"][i], full["w_up_b"][i],
                f"mixer_bwd_{i}", ex)
            st.update(dya=dya, dyb=dyb, dgates=dgates, doa=doa, dob=dob)
            return landed

        to_sibling(early, mixer)
        dgates = st["dgates"]
        gw["w_up_a"][i] = _mm_tn(st["oa"], st["dya"], f"dw_up_a_{i}", 4)
        gw["w_up_b"][i] = _mm_tn(st["ob"], st["dyb"], f"dw_up_b_{i}", 4)
        late = [("w_up_a", i), ("w_up_b", i)]
        late_halves = by_halves(late)
        keys = early + late
        with_sb = [(n, i) for n in ("w_ff1", "w_o", "w_pg", "w_pe")]
        with_swa = [(n, i) for n in ("w_ff2", "w_up_a", "w_up_b")]
        dqa, dka, dva, filled = _sb_bwd(st["qkv"], st["lt"], st["nb"], st["a_wide"], st["sg_wide"], st["doa"],
                                        f"sb_bwd_{i}", _both(_rs_to_sibling(late_halves), to_chips(with_sb)))
        add_halves(late, late_halves, filled[:len(late)])
        sum_chips(with_sb, filled[len(late):])
        (dqb, dkvb, dsk, drl), parts = _swa_bwd(st["qkv"], st["ob"], st["dob"], st["probs"], f"swa_bwd_{i}",
                                                to_chips(with_swa))
        sum_chips(with_swa, parts)
        dsinks[i] = dsk[:, 0]
        drel = drel + drl
        dproj = [dqa, dka, dva, dqb, dkvb, dgates]
        swap, store = swap_halves(keys + ([("w_in", 1)] if i == 0 else []))
        dw_in_t, filled = _mm_tn_pieces(dproj, st["h1"], f"dw_in_{i}", swap)
        store(filled)
        gw["w_in"][i] = dw_in_t.reshape(4, dw_in_t.shape[0] // 4, d)
        if i == 1:
            def inproj(ex):
                (dx, dg_mix[i]), landed = _inproj_bwd(dproj, dx1, st["x0"], g_mix[i].reshape(1, d),
                                                      full["w_in"][i], f"inproj_bwd_{i}", ex)
                st["dx"] = dx
                return landed

            to_sibling([("w_in", 1)], inproj)
            dx = st["dx"]
        else:
            keys = [("w_in", 0)]
            to_sibling(keys, lambda ex: _run_exchange("rs_to_sibling_last", ex))
            (dx, dg_mix[i]), parts = _inproj_bwd(dproj, dx1, st["x0"], g_mix[i].reshape(1, d),
                                                 full["w_in"][i], f"inproj_bwd_{i}", to_chips(keys))
            sum_chips(keys, parts)
            swap, store = swap_halves(keys)
            store(_run_exchange("exchange_halves_last", swap))
    grad_x = dx[None]

    outs = {}
    for n in BIG:
        g0, g1 = (reduced[n, l].reshape(wl[n].shape[1:]) for l in range(depth))
        outs[n] = _adamw(wl[n], ml[n], vl[n], g0, g1, "adamw_" + n)
    outs["w_in"] = [jnp.swapaxes(a, 1, 2) for a in outs["w_in"]]

    drel_bias = drel[:, :N_BUCKETS].T
    gsmall = _pack_small(jnp.concatenate(dg_mix, 0), jnp.concatenate(dg_mlp, 0), jnp.concatenate(dg_pe, 0),
                         dg_final[0], jnp.stack(dsinks), drel_bias, loss_part[0, :1])
    wsmall = _pack_small(g_mix, g_mlp, g_pe, g_final, sinks, rel_bias)
    msmall = _pack_small(m_g_mix, m_g_mlp, m_g_pe, m_g_final, m_sinks, m_rel_bias)
    vsmall = _pack_small(v_g_mix, v_g_mlp, v_g_pe, v_g_final, v_sinks, v_rel_bias)
    small = _small_allreduce_adamw(gsmall, wsmall, msmall, vsmall)
    loss = small[0][7, 0]
    small = [_unpack_small(a, sinks.shape, rel_bias.shape) for a in small]

    result = [loss, grad_x]
    for kind in range(4):
        result += [outs[n][kind] for n in BIG]
        result += list(small[kind])
    return tuple(result)
```

```python
import functools
import math
from typing import Callable, NamedTuple

import numpy as np
import jax
import jax.numpy as jnp
from jax import lax
from jax.experimental import pallas as pl
from jax.experimental.pallas import tpu as pltpu

F32 = jnp.float32
BF16 = jnp.bfloat16
MESH = pl.DeviceIdType.MESH

HEAD_DIM = 64
SB_HEADS = 8
SW_HEADS = 8
SW_KV_HEADS = 2
WINDOW = 128
N_BUCKETS = 32
MAX_DISTANCE = 128
EPS = 1e-6
SB_W = SB_HEADS * HEAD_DIM
SW_QW = SW_HEADS * HEAD_DIM
SW_KVW = SW_KV_HEADS * HEAD_DIM
QKV_W = 3 * SB_W + SW_QW + 2 * SW_KVW
SCALE = HEAD_DIM ** -0.5
assert SCALE == 0.125
LANES = 128
TQ = 128
BK = 128
NEG = -1e30
SB_EXHAUSTED = -106.0

ADAM_LR = 0.001
ADAM_B1 = 0.9
ADAM_B2 = 0.999
ADAM_EPS = 1e-08
ADAM_WD = 0.01
ADAM_STEP = 10

VMEM_LIMIT = 56 * 1024 * 1024


def _dot(a, b):
    return jnp.dot(a, b, preferred_element_type=F32)


def _dot_nt(a, b):
    return lax.dot_general(a, b, (((1,), (1,)), ((), ())), preferred_element_type=F32)


def _dot_tn(a, b):
    return lax.dot_general(a, b, (((0,), (0,)), ((), ())), preferred_element_type=F32)


def _sum_all(x):
    return jnp.sum(jnp.sum(x, axis=1, keepdims=True), axis=0, keepdims=True)


def _sigmoid(x):
    return 1.0 / (1.0 + jnp.exp(-x))


def _rms(x, g):
    r = lax.rsqrt(jnp.mean(x * x, axis=-1, keepdims=True) + EPS)
    return (x * r) * g


def _rms_bwd(dy, x, g):
    r = lax.rsqrt(jnp.mean(x * x, axis=-1, keepdims=True) + EPS)
    n = x * r
    dg = jnp.sum(dy * n, axis=0, keepdims=True)
    dn = dy * g
    dx = r * (dn - n * jnp.mean(dn * n, axis=-1, keepdims=True))
    return dx, dg


def _params(n_axes):
    return pltpu.CompilerParams(dimension_semantics=("arbitrary",) * n_axes, vmem_limit_bytes=VMEM_LIMIT)


def _rowcall(name, body, row_ins, const_ins, row_outs, acc_outs=(), tm=512, side=None):
    s = row_ins[0].shape[0]
    assert s % tm == 0
    in_specs = [pl.BlockSpec((tm, a.shape[1]), lambda i: (i, 0)) for a in row_ins]
    in_specs += [pl.BlockSpec(a.shape, functools.partial(lambda i, nd: (0,) * nd, nd=a.ndim)) for a in const_ins]
    out_shape = [jax.ShapeDtypeStruct((s, c), dt) for c, dt in row_outs]
    out_specs = [pl.BlockSpec((tm, c), lambda i: (i, 0)) for c, _ in row_outs]
    out_shape += [jax.ShapeDtypeStruct(sh, dt) for sh, dt in acc_outs]
    out_specs += [pl.BlockSpec(sh, functools.partial(lambda i, nd: (0,) * nd, nd=len(sh))) for sh, _ in acc_outs]
    if side is None:
        return pl.pallas_call(body, out_shape=out_shape, grid=(s // tm,), in_specs=in_specs, out_specs=out_specs,
                              compiler_params=_params(1), name=name)(*row_ins, *const_ins)
    n_in, n_out = len(in_specs), len(out_specs)

    def with_side(*refs):
        side_in, outs, side_out, sems = _side_refs(side, refs[n_in:], n_out)

        @pl.when(pl.program_id(0) == 0)
        def _():
            side.start(side_in, side_out, *sems)

        body(*refs[:n_in], *outs)

        @pl.when(pl.program_id(0) == s // tm - 1)
        def _():
            side.finish(side_in, side_out, *sems)

    s_in, s_shape, s_out, s_alias, s_sems = _side_specs(side, n_in, n_out)
    outs = pl.pallas_call(with_side, out_shape=out_shape + s_shape, grid=(s // tm,), in_specs=in_specs + s_in,
                          out_specs=out_specs + s_out, input_output_aliases=s_alias, scratch_shapes=s_sems,
                          compiler_params=_params(1), name=name)(*row_ins, *const_ins, *side.arrays)
    return outs[:n_out], outs[n_out:]


def _dot_cols(a, w_ref):
    return jnp.concatenate([_dot(a, w_ref[r]) for r in range(w_ref.shape[0])], axis=1)


def _dot_cols_t(a, w_ref):
    n = w_ref.shape[2]
    out = _dot_nt(a[:, :n], w_ref[0])
    for r in range(1, w_ref.shape[0]):
        out = out + _dot_nt(a[:, r * n:(r + 1) * n], w_ref[r])
    return out


def _inproj_fwd(x, g, wt, name):
    d = x.shape[1]

    def body(x_ref, g_ref, w_ref, h_ref, qkv_ref, gate_ref):
        hb = _rms(x_ref[...], g_ref[...]).astype(BF16)
        h_ref[...] = hb
        qkv_ref[...] = _dot_nt(hb, w_ref[:QKV_W, :]).astype(BF16)
        gate_ref[...] = _dot_nt(hb, w_ref[QKV_W:, :])

    return _rowcall(name, body, [x], [g, wt], [(d, BF16), (QKV_W, BF16), (2 * d, F32)])


def _mixer_fwd(oa, ob, gates, x, wua, wub, wo, name):
    d = x.shape[1]

    def body(oa_ref, ob_ref, gate_ref, x_ref, wua_ref, wub_ref, wo_ref, m_ref, x1_ref):
        ya = _dot_cols(oa_ref[...], wua_ref)
        yb = _dot_cols(ob_ref[...], wub_ref)
        m = _sigmoid(gate_ref[:, :d]) * ya + _sigmoid(gate_ref[:, d:]) * yb
        mb = m.astype(BF16)
        m_ref[...] = mb
        x1_ref[...] = x_ref[...] + _dot(mb, wo_ref[...])

    return _rowcall(name, body, [oa, ob, gates, x], [wua, wub, wo], [(d, BF16), (d, F32)])


def _ff_fwd(x1, g, w1, w2, name):
    _, d, nq = w1.shape
    dff = 4 * nq

    def body(x_ref, g_ref, w1_ref, w2_ref, h_ref, r_ref, a_ref, o_ref):
        x = x_ref[...]
        hb = _rms(x, g_ref[...]).astype(BF16)
        h_ref[...] = hb
        r = jnp.maximum(_dot_cols(hb, w1_ref), 0.0)
        r_ref[...] = r.astype(BF16)
        a = jnp.square(r).astype(BF16)
        a_ref[...] = a
        o_ref[...] = x + _dot(a, w2_ref[...])

    return _rowcall(name, body, [x1], [g, w1, w2], [(d, BF16), (dff, BF16), (dff, BF16), (d, F32)], tm=256)


def _ple_fwd(p, x2, g, wpe, wpg, name):
    d = x2.shape[1]

    def body(p_ref, x_ref, g_ref, wpe_ref, wpg_ref, pb_ref, h_ref, pe_ref, gt_ref, x3_ref):
        pb = p_ref[...].astype(BF16)
        pb_ref[...] = pb
        pe = _dot_cols(pb, wpe_ref)
        x = x_ref[...]
        hb = _rms(x, g_ref[...]).astype(BF16)
        h_ref[...] = hb
        gt = _dot(hb, wpg_ref[...])
        pe_ref[...] = pe
        gt_ref[...] = gt
        x3_ref[...] = x + pe * _sigmoid(gt)

    return _rowcall(name, body, [p, x2], [g, wpe, wpg],
                    [(p.shape[1], BF16), (d, BF16), (d, F32), (d, F32), (d, F32)])


def _pair_stack(t, lane):
    zero = jnp.zeros_like(t)
    return jnp.concatenate([jnp.where(lane < HEAD_DIM, t, zero), jnp.where(lane >= HEAD_DIM, t, zero)], axis=0)


def _sb_rel():
    row = lax.broadcasted_iota(jnp.int32, (2 * TQ, BK), 0)
    row = jnp.where(row >= TQ, row - TQ, row)
    col = lax.broadcasted_iota(jnp.int32, (2 * TQ, BK), 1)
    return col - row


def _split_dot(x, m01, two_pass=True):
    hi = x.astype(BF16)
    if not two_pass:
        return _dot(hi, m01)
    lo = (x - hi.astype(F32)).astype(BF16)
    return _dot(hi, m01) + _dot(lo, m01)


def _sb_scores(qs, k, mask):
    z = _dot_nt(qs, k)
    lb = jnp.minimum(z, 0.0) - jnp.log(1.0 + jnp.exp(-jnp.abs(z)))
    lm = lb - z
    return lb, lm if mask is None else jnp.where(mask, lm, 0.0)


SB_STRAIGHT = 3
SB_WIDE = SB_STRAIGHT * BK
SB_QB = 2
SWA_QB = 4


def _sb_wide_consts():
    j = np.arange(BK)[:, None]
    s = np.arange(BK)[None, :]
    ones = np.ones((BK, BK), np.float32)
    as_bf16 = lambda m: jnp.asarray(np.concatenate([m, ones], axis=1).astype(np.float32), dtype=BF16)
    return as_bf16(j > s), as_bf16(j <= s), as_bf16(j < s)


def _wide_sums(x, m01, suffix, two_pass=True):
    parts = [_split_dot(x[:, b * BK:(b + 1) * BK], m01, two_pass) for b in range(SB_STRAIGHT)]
    order = range(SB_STRAIGHT - 1, -1, -1) if suffix else range(SB_STRAIGHT)
    out = [None] * SB_STRAIGHT
    carry = None
    for b in order:
        out[b] = parts[b][:, :BK] if carry is None else parts[b][:, :BK] + carry
        carry = parts[b][:, BK:] if carry is None else carry + parts[b][:, BK:]
    return jnp.concatenate(out, axis=1), carry


def _side_refs(ex, rest, n_out):
    n_in = len(ex.arrays) if ex else 0
    n_alias = len(ex.aliased) if ex else 0
    ins, rest = rest[:n_in], rest[n_in:]
    outs, rest = rest[:n_out], rest[n_out:]
    return ins, outs, rest[:n_alias], rest[n_alias:]


def _side_specs(ex, n_in, n_out):
    if ex is None:
        return [], [], [], {}, []
    any_spec = pl.BlockSpec(memory_space=pl.ANY)
    return ([any_spec] * len(ex.arrays),
            [jax.ShapeDtypeStruct(ex.arrays[a].shape, ex.arrays[a].dtype) for a in ex.aliased],
            [any_spec] * len(ex.aliased), {n_in + a: n_out + o for o, a in enumerate(ex.aliased)},
            [pltpu.SemaphoreType.DMA(shape) for shape in ex.sems for _ in range(2)])


def _sb_fwd(qkv, name, side=None):
    s = qkv.shape[0]
    nq = s // TQ
    npair = SB_W // LANES
    sufw = _sb_wide_consts()[0]

    def body(q_ref, k_ref, v_ref, sufw_ref, *rest):
        side_in, (o_ref, lt_ref, nb_ref, a_ref, sg_ref), side_out, scratch = _side_refs(side, rest, 5)
        cf_ref, acc_ref = scratch[:2]
        step_id = pl.program_id(1)
        if side is not None:
            @pl.when((pl.program_id(0) == 0) & (step_id == 0))
            def _():
                side.start(side_in, side_out, *scratch[2:])
        lane = lax.broadcasted_iota(jnp.int32, (TQ, LANES), 1)
        rel = _sb_rel()
        blocks = [step_id * SB_QB + b for b in range(SB_QB)]
        qs = [_pair_stack(q_ref[b * TQ:(b + 1) * TQ, :] * SCALE, lane) for b in range(SB_QB)]

        straight = blocks[0] >= SB_STRAIGHT - 1

        @pl.when(straight)
        def _():
            for b, i in enumerate(blocks):
                w0 = pl.multiple_of((i - (SB_STRAIGHT - 1)) * BK, BK)
                kw = k_ref[pl.ds(w0, SB_WIDE), :]
                lb, lm = _sb_scores(qs[b], kw, None)
                own = rel < 0
                past = SB_WIDE - BK
                lm = jnp.concatenate([lm[:, :past], jnp.where(own, lm[:, past:], 0.0)], axis=1)
                after, total = _wide_sums(lm, sufw_ref[...], True)
                on_past_keys = lambda t: jnp.concatenate([t[:, :past], jnp.where(own, t[:, past:], 0.0)], axis=1)
                a = on_past_keys(jnp.exp(lb + after)).astype(BF16)
                acc_ref[b] = _dot(a, v_ref[pl.ds(w0, SB_WIDE), :])
                cf_ref[b] = total
                a_ref[b] = a
                sg_ref[b] = on_past_keys(jnp.exp(lb)).astype(BF16)

        @pl.when(jnp.logical_not(straight))
        def _():
            cf_ref[...] = jnp.zeros_like(cf_ref)
            acc_ref[...] = jnp.zeros_like(acc_ref)

        for b, i in enumerate(blocks):
            q0 = i * TQ

            def more(c, i=i):
                return (c[0] <= i) & (c[1] > SB_EXHAUSTED)

            def step(c, b=b, i=i, q0=q0):
                k0 = pl.multiple_of((i - c[0]) * BK, BK)
                k = k_ref[pl.ds(k0, BK), :]
                v = v_ref[pl.ds(k0, BK), :]
                mask = rel < (q0 - k0)
                lb, lm = _sb_scores(qs[b], k, mask)
                cs = _split_dot(lm, sufw_ref[...])
                a = jnp.where(mask, jnp.exp(lb + cs[:, :BK] + cf_ref[b]), 0.0)
                acc_ref[b] += _dot(a.astype(BF16), v)
                cf = cf_ref[b] + cs[:, BK:]
                cf_ref[b] = cf
                return c[0] + 1, jnp.max(cf)

            n_blocks, _ = lax.while_loop(
                more, step, (jnp.where(straight, SB_STRAIGHT, 0).astype(jnp.int32), jnp.max(cf_ref[b])))
            o_ref[b * TQ:(b + 1) * TQ, :] = jnp.where(lane < HEAD_DIM, acc_ref[b, :TQ, :],
                                                     acc_ref[b, TQ:, :]).astype(BF16)
            lt_ref[b] = cf_ref[b]
            nb_ref[b] = jnp.full(nb_ref.shape[1:], n_blocks, F32)
        if side is not None:
            @pl.when((pl.program_id(0) == npair - 1) & (step_id == nq // SB_QB - 1))
            def _():
                side.finish(side_in, side_out, *scratch[2:])

    s_in, s_shape, s_out, s_alias, s_sems = _side_specs(side, 4, 5)
    wide = jax.ShapeDtypeStruct((npair, nq, 2 * TQ, SB_WIDE), BF16)
    wide_spec = pl.BlockSpec((None, SB_QB, 2 * TQ, SB_WIDE), lambda j, i: (j, i, 0, 0))
    outs = pl.pallas_call(
        body,
        out_shape=[jax.ShapeDtypeStruct((s, SB_W), BF16), jax.ShapeDtypeStruct((npair, nq, 2 * TQ, BK), F32),
                   jax.ShapeDtypeStruct((npair, nq, 8, LANES), F32), wide, wide] + s_shape,
        grid=(npair, nq // SB_QB),
        in_specs=[pl.BlockSpec((SB_QB * TQ, LANES), lambda j, i: (i, j)),
                  pl.BlockSpec((s, LANES), lambda j, i: (0, npair + j)),
                  pl.BlockSpec((s, LANES), lambda j, i: (0, 2 * npair + j)),
                  pl.BlockSpec(sufw.shape, lambda j, i: (0, 0))] + s_in,
        out_specs=[pl.BlockSpec((SB_QB * TQ, LANES), lambda j, i: (i, j)),
                   pl.BlockSpec((None, SB_QB, 2 * TQ, BK), lambda j, i: (j, i, 0, 0)),
                   pl.BlockSpec((None, SB_QB, 8, LANES), lambda j, i: (j, i, 0, 0)), wide_spec, wide_spec] + s_out,
        input_output_aliases=s_alias,
        scratch_shapes=[pltpu.VMEM((SB_QB, 2 * TQ, BK), F32), pltpu.VMEM((SB_QB, 2 * TQ, LANES), F32)] + s_sems,
        compiler_params=_params(2), name=name)(qkv, qkv, qkv, sufw, *(side.arrays if side else ()))
    return outs[:5], outs[5:]


def _sb_bwd(qkv, lt, nb, a_wide, sg_wide, doa, name, side=None):
    s = qkv.shape[0]
    nq = s // TQ
    npair = SB_W // LANES
    _, prew, prexw = _sb_wide_consts()

    def body(q_ref, k_ref, v_ref, lt_ref, nb_ref, do_ref, prew_ref, prexw_ref, a_ref, sg_ref, *rest):
        side_in, (dq_ref, dk_out, dv_out), side_out, scratch = _side_refs(side, rest, 3)
        cp_ref, ce_ref, dqa_ref, dk_ref, dv_ref = scratch[:5]
        sems = scratch[5:]
        step_id = pl.program_id(1)
        if side is not None:
            @pl.when((pl.program_id(0) == 0) & (step_id == 0))
            def _():
                side.start(side_in, side_out, *sems)
        lane = lax.broadcasted_iota(jnp.int32, (TQ, LANES), 1)
        rel = _sb_rel()
        blocks = [step_id * SB_QB + b for b in range(SB_QB)]
        rows = [slice(b * TQ, (b + 1) * TQ) for b in range(SB_QB)]
        qs = [_pair_stack(q_ref[rows[b], :] * SCALE, lane) for b in range(SB_QB)]
        dos = [_pair_stack(do_ref[rows[b], :], lane) for b in range(SB_QB)]
        n_blocks = [jnp.clip(jnp.max(nb_ref[b]).astype(jnp.int32), 1, i + 1) for b, i in enumerate(blocks)]
        first = [i + 1 - n for i, n in zip(blocks, n_blocks)]

        @pl.when(step_id == 0)
        def _():
            dk_ref[...] = jnp.zeros_like(dk_ref)
            dv_ref[...] = jnp.zeros_like(dv_ref)

        straight = n_blocks[0] == SB_STRAIGHT
        for n in n_blocks[1:]:
            straight = straight & (n == SB_STRAIGHT)

        @pl.when(straight)
        def _():
            for b in range(SB_QB):
                w0 = pl.multiple_of(first[b] * BK, BK)
                kw = k_ref[pl.ds(w0, SB_WIDE), :]
                vw = v_ref[pl.ds(w0, SB_WIDE), :]
                a = a_ref[b]
                e = a.astype(F32) * _dot_nt(dos[b], vw)
                big_e, _ = _wide_sums(e, prexw_ref[...], False, two_pass=False)
                dz = (e - sg_ref[b].astype(F32) * (e + big_e)).astype(BF16)
                dk_ref[pl.ds(w0, SB_WIDE), :] += _dot_tn(dz, qs[b])
                dv_ref[pl.ds(w0, SB_WIDE), :] += _dot_tn(a, dos[b])
                dqa_ref[b] = _dot(dz, kw)

        @pl.when(jnp.logical_not(straight))
        def _():
            cp_ref[...] = jnp.zeros_like(cp_ref)
            ce_ref[...] = jnp.zeros_like(ce_ref)
            dqa_ref[...] = jnp.zeros_like(dqa_ref)
            for b, i in enumerate(blocks):
                q0 = i * TQ

                def step(it, carry, b=b, q0=q0):
                    k0 = pl.multiple_of((first[b] + it) * BK, BK)
                    k = k_ref[pl.ds(k0, BK), :]
                    v = v_ref[pl.ds(k0, BK), :]
                    mask = rel < (q0 - k0)
                    lb, lm = _sb_scores(qs[b], k, mask)
                    cs = _split_dot(lm, prew_ref[...])
                    a = jnp.where(mask, jnp.exp(lb + (lt_ref[b] - (cs[:, :BK] + cp_ref[b]))), 0.0)
                    e = a * _dot_nt(dos[b], v)
                    ce = _split_dot(e, prexw_ref[...], two_pass=False)
                    big_e = ce[:, :BK] + ce_ref[b]
                    dz = jnp.where(mask, e - jnp.exp(lb) * (e + big_e), 0.0).astype(BF16)
                    dk_ref[pl.ds(k0, BK), :] += _dot_tn(dz, qs[b])
                    dv_ref[pl.ds(k0, BK), :] += _dot_tn(a.astype(BF16), dos[b])
                    dqa_ref[b] += _dot(dz, k)
                    cp_ref[b] += cs[:, BK:]
                    ce_ref[b] += ce[:, BK:]
                    return carry

                lax.fori_loop(0, n_blocks[b], step, 0)

        for b in range(SB_QB):
            dq = jnp.where(lane < HEAD_DIM, dqa_ref[b, :TQ, :], dqa_ref[b, TQ:, :])
            dq_ref[rows[b], :] = (dq * SCALE).astype(BF16)

        @pl.when(step_id == nq // SB_QB - 1)
        def _():
            dk_out[...] = dk_ref[...].astype(BF16)
            dv_out[...] = dv_ref[...].astype(BF16)

        if side is not None:
            @pl.when((pl.program_id(0) == npair - 1) & (step_id == nq // SB_QB - 1))
            def _():
                side.finish(side_in, side_out, *sems)

    s_in, s_shape, s_out, s_alias, s_sems = _side_specs(side, 10, 3)
    wide_spec = pl.BlockSpec((None, SB_QB, 2 * TQ, SB_WIDE), lambda j, i: (j, i, 0, 0))
    outs = pl.pallas_call(
        body,
        out_shape=[jax.ShapeDtypeStruct((s, SB_W), BF16)] * 3 + s_shape,
        grid=(npair, nq // SB_QB),
        in_specs=[pl.BlockSpec((SB_QB * TQ, LANES), lambda j, i: (i, j)),
                  pl.BlockSpec((s, LANES), lambda j, i: (0, npair + j)),
                  pl.BlockSpec((s, LANES), lambda j, i: (0, 2 * npair + j)),
                  pl.BlockSpec((None, SB_QB, 2 * TQ, BK), lambda j, i: (j, i, 0, 0)),
                  pl.BlockSpec((None, SB_QB, 8, LANES), lambda j, i: (j, i, 0, 0)),
                  pl.BlockSpec((SB_QB * TQ, LANES), lambda j, i: (i, j)),
                  pl.BlockSpec(prew.shape, lambda j, i: (0, 0)),
                  pl.BlockSpec(prexw.shape, lambda j, i: (0, 0)), wide_spec, wide_spec] + s_in,
        out_specs=[pl.BlockSpec((SB_QB * TQ, LANES), lambda j, i: (i, j)),
                   pl.BlockSpec((s, LANES), lambda j, i: (0, j)),
                   pl.BlockSpec((s, LANES), lambda j, i: (0, j))] + s_out,
        input_output_aliases=s_alias,
        scratch_shapes=[pltpu.VMEM((SB_QB, 2 * TQ, BK), F32), pltpu.VMEM((SB_QB, 2 * TQ, BK), F32),
                        pltpu.VMEM((SB_QB, 2 * TQ, LANES), F32), pltpu.VMEM((s, LANES), F32),
                        pltpu.VMEM((s, LANES), F32)] + s_sems,
        compiler_params=_params(2), name=name)(qkv, qkv, qkv, lt, nb, doa, prew, prexw, a_wide, sg_wide,
                                               *(side.arrays if side else ()))
    return outs[0], outs[1], outs[2], outs[3:]


def _bucket_table():
    i = np.arange(TQ)[:, None]
    j = np.arange(2 * BK)[None, :]
    dist = np.maximum(TQ + i - j, 0)
    max_exact = N_BUCKETS // 2
    df = np.maximum(dist, 1).astype(np.float32)
    large = max_exact + (np.log(df / np.float32(max_exact)) / np.float32(math.log(MAX_DISTANCE / max_exact))
                         * np.float32(N_BUCKETS - max_exact)).astype(np.int32)
    large = np.minimum(large, N_BUCKETS - 1)
    return np.where(dist < max_exact, dist, large).astype(np.int32)


def _swa_align_in(t, lane, g):
    tf = t.astype(F32)
    tr = pltpu.roll(tf, HEAD_DIM, 1)
    gmask = (lane >= HEAD_DIM) == (g == 1)
    top = jnp.where(gmask, jnp.where(g == 0, tf, tr), 0.0)
    bot = jnp.where(gmask, jnp.where(g == 1, tf, tr), 0.0)
    return jnp.concatenate([top, bot], axis=0).astype(BF16)


def _swa_align_out(t, lane, g):
    top, bot = t[:TQ, :], t[TQ:, :]
    top = jnp.where(g == 0, top, pltpu.roll(top, HEAD_DIM, 1))
    bot = jnp.where(g == 1, bot, pltpu.roll(bot, HEAD_DIM, 1))
    return jnp.where(lane < HEAD_DIM, top, bot)


def _swa_bias(bias_ref, bucket_ref, rb_ref, j):
    dist = TQ + lax.broadcasted_iota(jnp.int32, (TQ, 2 * BK), 0) - lax.broadcasted_iota(jnp.int32, (TQ, 2 * BK), 1)
    window = (dist >= 0) & (dist < WINDOW)
    for hh in range(2):
        def add(b, acc):
            return acc + jnp.where(bucket_ref[...] == b, rb_ref[b, 2 * j + hh], 0.0)
        bias = lax.fori_loop(0, N_BUCKETS, add, jnp.zeros((TQ, 2 * BK), F32))
        bias_ref[hh * TQ:(hh + 1) * TQ, :] = jnp.where(window, bias, NEG)


def _swa_probs(qs, k2, bias, own_block, sink_ref, i, j):
    s = _dot_nt(qs, k2) + bias
    s = jnp.where(own_block | (i > 0), s, NEG)
    row1 = lax.broadcasted_iota(jnp.int32, (2 * TQ, 1), 0)
    sink = jnp.where(row1 < TQ, sink_ref[2 * j], sink_ref[2 * j + 1])
    m = jnp.maximum(jnp.max(s, axis=1, keepdims=True), sink)
    e = jnp.exp(s - m)
    es = jnp.exp(sink - m)
    inv = 1.0 / (jnp.sum(e, axis=1, keepdims=True) + es)
    return e * inv, es * inv


def _swa_kv(ref, i):
    prev = pl.multiple_of(jnp.maximum(i - 1, 0) * BK, BK)
    cur = pl.multiple_of(i * BK, BK)
    return jnp.concatenate([ref[pl.ds(prev, BK), :], ref[pl.ds(cur, BK), :]], axis=0), prev, cur


def _swa_fwd(qkv, sinks, rel_bias, name, side=None):
    s = qkv.shape[0]
    nq = s // TQ
    npair = SW_QW // LANES
    qcol = 3 * SB_W // LANES
    bucket = jnp.asarray(_bucket_table())

    def body(q_ref, k_ref, v_ref, bucket_ref, sink_ref, rb_ref, *rest):
        side_in, (o_ref, p_ref), side_out, scratch = _side_refs(side, rest, 2)
        bias_ref, sems = scratch[0], scratch[1:]
        j = pl.program_id(0)
        step = pl.program_id(1)
        if side is not None:
            @pl.when((j == 0) & (step == 0))
            def _():
                side.start(side_in, side_out, *sems)
        g = j // 2
        lane = lax.broadcasted_iota(jnp.int32, (TQ, LANES), 1)

        @pl.when(step == 0)
        def _():
            _swa_bias(bias_ref, bucket_ref, rb_ref, j)

        own_block = lax.broadcasted_iota(jnp.int32, (2 * TQ, 2 * BK), 1) >= BK
        for b in range(SWA_QB):
            i = step * SWA_QB + b
            rows = slice(b * TQ, (b + 1) * TQ)
            qs = _swa_align_in(q_ref[rows, :] * SCALE, lane, g)
            k2, _, _ = _swa_kv(k_ref, i)
            v2, _, _ = _swa_kv(v_ref, i)
            pr, psink = _swa_probs(qs, k2, bias_ref[...], own_block, sink_ref, i, j)
            prb = pr.astype(BF16)
            o_ref[rows, :] = _swa_align_out(_dot(prb, v2), lane, g).astype(BF16)
            p_ref[b, :, :2 * BK] = prb
            p_ref[b, :, 2 * BK:] = jnp.broadcast_to(psink, (2 * TQ, BK)).astype(BF16)
        if side is not None:
            @pl.when((j == npair - 1) & (step == nq // SWA_QB - 1))
            def _():
                side.finish(side_in, side_out, *sems)

    assert nq % SWA_QB == 0
    s_in, s_shape, s_out, s_alias, s_sems = _side_specs(side, 6, 2)
    outs = pl.pallas_call(
        body,
        out_shape=[jax.ShapeDtypeStruct((s, SW_QW), BF16),
                   jax.ShapeDtypeStruct((npair, nq, 2 * TQ, 3 * BK), BF16)] + s_shape,
        grid=(npair, nq // SWA_QB),
        in_specs=[pl.BlockSpec((SWA_QB * TQ, LANES), lambda j, i: (i, qcol + j)),
                  pl.BlockSpec((s, LANES), lambda j, i: (0, qcol + npair)),
                  pl.BlockSpec((s, LANES), lambda j, i: (0, qcol + npair + 1)),
                  pl.BlockSpec((TQ, 2 * BK), lambda j, i: (0, 0)),
                  pl.BlockSpec(memory_space=pltpu.SMEM),
                  pl.BlockSpec(memory_space=pltpu.SMEM)] + s_in,
        out_specs=[pl.BlockSpec((SWA_QB * TQ, LANES), lambda j, i: (i, j)),
                   pl.BlockSpec((None, SWA_QB, 2 * TQ, 3 * BK), lambda j, i: (j, i, 0, 0))] + s_out,
        input_output_aliases=s_alias,
        scratch_shapes=[pltpu.VMEM((2 * TQ, 2 * BK), F32)] + s_sems,
        compiler_params=_params(2), name=name)(qkv, qkv, qkv, bucket, sinks, rel_bias,
                                               *(side.arrays if side else ()))
    return outs[:2], outs[2:]


def _swa_bwd(qkv, ob, dob, probs, name, side=None):
    s = qkv.shape[0]
    nq = s // TQ
    npair = SW_QW // LANES
    qcol = 3 * SB_W // LANES
    bucket = jnp.asarray(_bucket_table())

    def body(q_ref, k_ref, v_ref, o_ref, do_ref, bucket_ref, p_ref, *rest):
        side_in, (dq_ref, dkv_ref, dsink_ref, drel_ref), side_out, scratch = _side_refs(side, rest, 4)
        dsacc_ref, dk_ref, dv_ref = scratch[:3]
        sems = scratch[3:]
        j = pl.program_id(0)
        step = pl.program_id(1)
        if side is not None:
            @pl.when((j == 0) & (step == 0))
            def _():
                side.start(side_in, side_out, *sems)
        g = j // 2
        lane = lax.broadcasted_iota(jnp.int32, (TQ, LANES), 1)
        row8 = lax.broadcasted_iota(jnp.int32, (SW_HEADS, LANES), 0)
        lane8 = lax.broadcasted_iota(jnp.int32, (SW_HEADS, LANES), 1)

        @pl.when((step == 0) & (j == 0))
        def _():
            dk_ref[...] = jnp.zeros_like(dk_ref)
            dv_ref[...] = jnp.zeros_like(dv_ref)
            dsink_ref[...] = jnp.zeros_like(dsink_ref)
            drel_ref[...] = jnp.zeros_like(drel_ref)

        @pl.when(step == 0)
        def _():
            dsacc_ref[...] = jnp.zeros_like(dsacc_ref)

        ds_sum = jnp.zeros(dsacc_ref.shape, F32)
        dsink = jnp.zeros((SW_HEADS, LANES), F32)
        for b in range(SWA_QB):
            i = step * SWA_QB + b
            rows = slice(b * TQ, (b + 1) * TQ)
            qs = _swa_align_in(q_ref[rows, :] * SCALE, lane, g)
            do = do_ref[rows, :]
            dos = _swa_align_in(do, lane, g)
            dof = do.astype(F32) * o_ref[rows, :].astype(F32)
            d0 = jnp.sum(jnp.where(lane < HEAD_DIM, dof, 0.0), axis=1, keepdims=True)
            d1 = jnp.sum(jnp.where(lane >= HEAD_DIM, dof, 0.0), axis=1, keepdims=True)
            delta = jnp.concatenate([d0, d1], axis=0)
            k2, prev, cur = _swa_kv(k_ref, i)
            v2, _, _ = _swa_kv(v_ref, i)
            prb = p_ref[b, :, :2 * BK]
            ds = prb.astype(F32) * (_dot_nt(dos, v2) - delta)
            ds_sum = ds_sum + ds
            sd = p_ref[b, :, 2 * BK:].astype(F32) * delta
            ds0 = -jnp.sum(sd[:TQ, :], axis=0, keepdims=True)
            ds1 = -jnp.sum(sd[TQ:, :], axis=0, keepdims=True)
            dsink = dsink + jnp.where(row8 == 2 * j, ds0, jnp.where(row8 == 2 * j + 1, ds1, 0.0))
            dsb = ds.astype(BF16)
            dq_ref[rows, :] = _swa_align_out(_dot(dsb, k2) * SCALE, lane, g).astype(BF16)
            dk2 = _dot_tn(dsb, qs)
            dv2 = _dot_tn(prb, dos)
            dk_ref[pl.ds(prev, BK), :] += dk2[:BK, :]
            dk_ref[pl.ds(cur, BK), :] += dk2[BK:, :]
            dv_ref[pl.ds(prev, BK), :] += dv2[:BK, :]
            dv_ref[pl.ds(cur, BK), :] += dv2[BK:, :]
        dsacc_ref[...] += ds_sum
        dsink_ref[...] += dsink

        @pl.when(step == nq // SWA_QB - 1)
        def _():
            for hh in range(2):
                def red(b, acc):
                    val = _sum_all(jnp.where(bucket_ref[...] == b, dsacc_ref[hh * TQ:(hh + 1) * TQ, :], 0.0))
                    return jnp.where((row8 == 2 * j + hh) & (lane8 == b), val, acc)
                drel_ref[...] += lax.fori_loop(0, N_BUCKETS, red, jnp.zeros((SW_HEADS, LANES), F32))

        @pl.when((step == nq // SWA_QB - 1) & (j == npair - 1))
        def _():
            dkv_ref[:, :LANES] = dk_ref[...].astype(BF16)
            dkv_ref[:, LANES:] = dv_ref[...].astype(BF16)
            if side is not None:
                side.finish(side_in, side_out, *sems)

    whole = lambda j, i: (0, 0)
    s_in, s_shape, s_out, s_alias, s_sems = _side_specs(side, 7, 4)
    outs = pl.pallas_call(
        body,
        out_shape=[jax.ShapeDtypeStruct((s, SW_QW), BF16), jax.ShapeDtypeStruct((s, 2 * LANES), BF16),
                   jax.ShapeDtypeStruct((SW_HEADS, LANES), F32), jax.ShapeDtypeStruct((SW_HEADS, LANES), F32)] + s_shape,
        grid=(npair, nq // SWA_QB),
        in_specs=[pl.BlockSpec((SWA_QB * TQ, LANES), lambda j, i: (i, qcol + j)),
                  pl.BlockSpec((s, LANES), lambda j, i: (0, qcol + npair)),
                  pl.BlockSpec((s, LANES), lambda j, i: (0, qcol + npair + 1)),
                  pl.BlockSpec((SWA_QB * TQ, LANES), lambda j, i: (i, j)),
                  pl.BlockSpec((SWA_QB * TQ, LANES), lambda j, i: (i, j)),
                  pl.BlockSpec((TQ, 2 * BK), whole),
                  pl.BlockSpec((None, SWA_QB, 2 * TQ, 3 * BK), lambda j, i: (j, i, 0, 0))] + s_in,
        out_specs=[pl.BlockSpec((SWA_QB * TQ, LANES), lambda j, i: (i, j)),
                   pl.BlockSpec((s, 2 * LANES), whole),
                   pl.BlockSpec((SW_HEADS, LANES), whole), pl.BlockSpec((SW_HEADS, LANES), whole)] + s_out,
        input_output_aliases=s_alias,
        scratch_shapes=[pltpu.VMEM((2 * TQ, 2 * BK), F32), pltpu.VMEM((s, LANES), F32),
                        pltpu.VMEM((s, LANES), F32)] + s_sems,
        compiler_params=_params(2), name=name)(qkv, qkv, qkv, ob, dob, bucket, probs,
                                               *(side.arrays if side else ()))
    return outs[:4], outs[4:]


def _acc_init(i, *refs):
    @pl.when(i == 0)
    def _():
        for r in refs:
            r[...] = jnp.zeros_like(r)


def _ple_fwd_loss(p, x2, g, wpe, wpg, target, g_final, name):
    d = x2.shape[1]

    def body(p_ref, x_ref, t_ref, g_ref, wpe_ref, wpg_ref, gf_ref,
             pb_ref, h_ref, pe_ref, gt_ref, dx_ref, dg_ref, loss_ref):
        _acc_init(pl.program_id(0), dg_ref, loss_ref)
        pb = p_ref[...].astype(BF16)
        pb_ref[...] = pb
        pe = _dot_cols(pb, wpe_ref)
        x = x_ref[...]
        hb = _rms(x, g_ref[...]).astype(BF16)
        h_ref[...] = hb
        gt = _dot(hb, wpg_ref[...])
        pe_ref[...] = pe
        gt_ref[...] = gt
        x3 = x + pe * _sigmoid(gt)
        gf = gf_ref[...]
        diff = _rms(x3, gf) - t_ref[...]
        loss_ref[...] += 0.5 * jnp.sum(jnp.mean(jnp.square(diff), axis=-1, keepdims=True), axis=0, keepdims=True)
        dx, dg = _rms_bwd(diff * (1.0 / d), x3, gf)
        dx_ref[...] = dx
        dg_ref[...] += dg

    return _rowcall(name, body, [p, x2, target], [g, wpe, wpg, g_final],
                    [(p.shape[1], BF16), (d, BF16), (d, F32), (d, F32), (d, F32)],
                    [((1, d), F32), ((1, LANES), F32)])


def _ple_bwd(dx3, pe, gt, x2, g, wpg, name):
    d = x2.shape[1]

    def body(dx3_ref, pe_ref, gt_ref, x_ref, g_ref, w_ref, dpe_ref, dgt_ref, dx2_ref, dg_ref):
        _acc_init(pl.program_id(0), dg_ref)
        dx3 = dx3_ref[...]
        sg = _sigmoid(gt_ref[...])
        dpe_ref[...] = (dx3 * sg).astype(BF16)
        dgt = (dx3 * pe_ref[...] * sg * (1.0 - sg)).astype(BF16)
        dgt_ref[...] = dgt
        dx, dg = _rms_bwd(_dot_nt(dgt, w_ref[...]), x_ref[...], g_ref[...])
        dx2_ref[...] = dx3 + dx
        dg_ref[...] += dg

    return _rowcall(name, body, [dx3, pe, gt, x2], [g, wpg], [(d, BF16), (d, BF16), (d, F32)], [((1, d), F32)])


def _ff_bwd(dx2, r, x1, g, w1, w2, name, side=None):
    d = x1.shape[1]
    dff = r.shape[1]

    def body(dx2_ref, r_ref, x_ref, g_ref, w1_ref, w2_ref, du_ref, dxb_ref, dx1_ref, dx1b_ref, dg_ref):
        _acc_init(pl.program_id(0), dg_ref)
        dx2 = dx2_ref[...]
        dxb = dx2.astype(BF16)
        dxb_ref[...] = dxb
        du = (_dot_nt(dxb, w2_ref[...]) * (2.0 * r_ref[...].astype(F32))).astype(BF16)
        du_ref[...] = du
        dx, dg = _rms_bwd(_dot_cols_t(du, w1_ref), x_ref[...], g_ref[...])
        dx1 = dx2 + dx
        dx1_ref[...] = dx1
        dx1b_ref[...] = dx1.astype(BF16)
        dg_ref[...] += dg

    return _rowcall(name, body, [dx2, r, x1], [g, w1, w2], [(dff, BF16), (d, BF16), (d, F32), (d, BF16)],
                    [((1, d), F32)], tm=256, side=side)


def _mixer_bwd(dx1b, gates, oa, ob, wo, wua, wub, name, side=None):
    d = dx1b.shape[1]

    def body(dx_ref, gate_ref, oa_ref, ob_ref, wo_ref, wua_ref, wub_ref,
             dya_ref, dyb_ref, dgate_ref, doa_ref, dob_ref):
        dm = _dot_nt(dx_ref[...], wo_ref[...])
        sa = _sigmoid(gate_ref[:, :d])
        sb = _sigmoid(gate_ref[:, d:])
        ya = _dot_cols(oa_ref[...], wua_ref)
        yb = _dot_cols(ob_ref[...], wub_ref)
        dya = (dm * sa).astype(BF16)
        dyb = (dm * sb).astype(BF16)
        dya_ref[...] = dya
        dyb_ref[...] = dyb
        dgate_ref[:, :d] = (dm * ya * sa * (1.0 - sa)).astype(BF16)
        dgate_ref[:, d:] = (dm * yb * sb * (1.0 - sb)).astype(BF16)
        doa_ref[...] = _dot_cols_t(dya, wua_ref).astype(BF16)
        dob_ref[...] = _dot_cols_t(dyb, wub_ref).astype(BF16)

    return _rowcall(name, body, [dx1b, gates, oa, ob], [wo, wua, wub],
                    [(d, BF16), (d, BF16), (2 * d, BF16), (SB_W, BF16), (SW_QW, BF16)], side=side)


def _inproj_bwd(pieces, dx1, x, g, wt, name, side=None):
    d = x.shape[1]
    n = len(pieces)
    offsets = [sum(pc.shape[1] for pc in pieces[:p]) for p in range(n + 1)]

    def body(*refs):
        dx1_ref, x_ref, g_ref, w_ref, dx_ref, dg_ref = refs[n:]
        _acc_init(pl.program_id(0), dg_ref)
        dh = _dot(refs[0][...], w_ref[:offsets[1], :])
        for p in range(1, n):
            dh = dh + _dot(refs[p][...], w_ref[offsets[p]:offsets[p + 1], :])
        dx, dg = _rms_bwd(dh, x_ref[...], g_ref[...])
        dx_ref[...] = dx1_ref[...] + dx
        dg_ref[...] += dg

    return _rowcall(name, body, list(pieces) + [dx1, x], [g, wt], [(d, F32)], [((1, d), F32)], side=side)


def _tile(n, cap):
    assert n % LANES == 0
    return max(t for t in range(LANES, min(n, cap) + 1, LANES) if n % t == 0)


def _mm_tn(a, b, name, nshard=1):
    s, ka = a.shape
    nb = b.shape[1]
    n = nb // nshard
    ta = _tile(ka, 512)
    tb = _tile(n, 1024)
    per = n // tb

    def body(a_ref, b_ref, o_ref):
        o_ref[...] = _dot_tn(a_ref[...].astype(BF16), b_ref[...].astype(BF16))

    return pl.pallas_call(
        body, out_shape=jax.ShapeDtypeStruct((nshard, ka, n), F32), grid=(nb // tb, ka // ta),
        in_specs=[pl.BlockSpec((s, ta), lambda jb, ia: (0, ia)), pl.BlockSpec((s, tb), lambda jb, ia: (0, jb))],
        out_specs=pl.BlockSpec((None, ta, tb), lambda jb, ia: (jb // per, ia, jb % per)),
        compiler_params=_params(2), name=name)(a, b)


def _mm_tn_pieces(pieces, b, name, side=None):
    s, nb = b.shape
    ta = 256
    n_in = len(pieces) + 1
    tiles = [pc.shape[1] // ta for pc in pieces]
    assert all(pc.shape[1] % ta == 0 for pc in pieces)
    starts = [sum(tiles[:p]) for p in range(len(pieces))]
    tb = _tile(nb, 1024)
    grid = (nb // tb, sum(tiles))

    def body(*refs):
        a_refs, b_ref = refs[:n_in - 1], refs[n_in - 1]
        side_in, (o_ref,), side_out, sems = _side_refs(side, refs[n_in:], 1)
        jb, ia = pl.program_id(0), pl.program_id(1)
        if side is not None:
            @pl.when((jb == 0) & (ia == 0))
            def _():
                side.start(side_in, side_out, *sems)
        for p in range(len(pieces)):
            @pl.when((ia >= starts[p]) & (ia < starts[p] + tiles[p]))
            def _(p=p):
                o_ref[...] = _dot_tn(a_refs[p][...], b_ref[...])
        if side is not None:
            @pl.when((jb == grid[0] - 1) & (ia == grid[1] - 1))
            def _():
                side.finish(side_in, side_out, *sems)

    def piece_spec(p):
        return pl.BlockSpec((s, ta), lambda jb, ia: (0, jnp.clip(ia - starts[p], 0, tiles[p] - 1)))

    s_in, s_shape, s_out, s_alias, s_sems = _side_specs(side, n_in, 1)
    outs = pl.pallas_call(
        body, out_shape=[jax.ShapeDtypeStruct((sum(tiles) * ta, nb), F32)] + s_shape, grid=grid,
        in_specs=[piece_spec(p) for p in range(len(pieces))] + [pl.BlockSpec((s, tb), lambda jb, ia: (0, jb))] + s_in,
        out_specs=[pl.BlockSpec((ta, tb), lambda jb, ia: (ia, jb))] + s_out,
        input_output_aliases=s_alias, scratch_shapes=s_sems,
        compiler_params=_params(2), name=name)(*pieces, b, *(side.arrays if side else ()))
    return outs[0] if side is None else (outs[0], outs[1:])


def _place():
    return lax.axis_index("x"), lax.axis_index("y"), lax.axis_index("c")


def _chip_peer(x, y, k):
    return (x ^ (k >> 1), y ^ (k & 1))


def _row_tile(k, cap=544):
    return max(t for t in range(32, min(k, cap) + 1, 32) if k % t == 0)


def _cast_bf16(w, r, name):
    l, k, n = w.shape
    assert l == 2
    tk = _row_tile(k)

    def body(r_ref, w_ref, o0_ref, o1_ref):
        o0_ref[...] = w_ref[0].astype(BF16)
        o1_ref[...] = w_ref[1].astype(BF16)

    out_spec = pl.BlockSpec((None, tk, n), lambda i, r_ref: (r_ref[0], i, 0))
    return pl.pallas_call(
        body, out_shape=[jax.ShapeDtypeStruct((4, k, n), BF16)] * 2,
        grid_spec=pltpu.PrefetchScalarGridSpec(
            num_scalar_prefetch=1, grid=(k // tk,),
            in_specs=[pl.BlockSpec((l, tk, n), lambda i, r_ref: (0, i, 0))],
            out_specs=[out_spec, out_spec]),
        compiler_params=_params(1), name=name)(r, w)


class _Exchange(NamedTuple):
    arrays: tuple
    aliased: tuple
    sems: tuple
    start: Callable
    finish: Callable


def _both(a, b):
    na, ma, ka = len(a.arrays), len(a.aliased), 2 * len(a.sems)

    def start(ins, outs, *sems):
        a.start(ins[:na], outs[:ma], *sems[:ka])
        b.start(ins[na:], outs[ma:], *sems[ka:])

    def finish(ins, outs, *sems):
        a.finish(ins[:na], outs[:ma], *sems[:ka])
        b.finish(ins[na:], outs[ma:], *sems[ka:])

    return _Exchange(a.arrays + b.arrays, a.aliased + tuple(na + i for i in b.aliased), a.sems + b.sems, start, finish)


def _all_gather(bufs):
    nt = len(bufs)

    def ici(t, ins, outs, send_sems, recv_sems, x, y, c, r, k):
        return pltpu.make_async_remote_copy(
            src_ref=ins[t].at[r, c], dst_ref=outs[t].at[r, c], send_sem=send_sems.at[t, k - 1],
            recv_sem=recv_sems.at[t, k - 1], device_id=(*_chip_peer(x, y, k), c), device_id_type=MESH)

    def d2d(t, outs, send_sems, recv_sems, x, y, c, r, k, half):
        slab = outs[t].at[r ^ k, half]
        return pltpu.make_async_remote_copy(
            src_ref=slab, dst_ref=slab, send_sem=send_sems.at[t, 2 + k], recv_sem=recv_sems.at[t, 2 + k],
            device_id=(x, y, 1 - c), device_id_type=MESH)

    def start(ins, outs, send_sems, recv_sems):
        x, y, c = _place()
        r = 2 * x + y
        for t in range(nt):
            for k in (1, 2, 3):
                ici(t, ins, outs, send_sems, recv_sems, x, y, c, r, k).start()

    def finish(ins, outs, send_sems, recv_sems):
        x, y, c = _place()
        r = 2 * x + y
        for t in range(nt):
            for k in (1, 2, 3):
                slab = outs[t].at[r ^ k, c]
                pltpu.make_async_remote_copy(
                    src_ref=slab, dst_ref=slab, send_sem=send_sems.at[t, k - 1], recv_sem=recv_sems.at[t, k - 1],
                    device_id=(x, y, 1 - c), device_id_type=MESH).wait_recv()
                d2d(t, outs, send_sems, recv_sems, x, y, c, r, k, c).start()
        for t in range(nt):
            for k in (1, 2, 3):
                d2d(t, outs, send_sems, recv_sems, x, y, c, r, k, 1 - c).wait_recv()
        for t in range(nt):
            for k in (1, 2, 3):
                ici(t, ins, outs, send_sems, recv_sems, x, y, c, r, k).wait_send()
                d2d(t, outs, send_sems, recv_sems, x, y, c, r, k, c).wait_send()

    return _Exchange(tuple(bufs), tuple(range(nt)), ((nt, 6),), start, finish)


def _run_exchange(name, ex):
    n_in, n_out = len(ex.arrays), len(ex.aliased)

    def body(*refs):
        ins, outs = refs[:n_in], refs[n_in:n_in + n_out]
        send_sems, recv_sems = refs[n_in + n_out:]
        ex.start(ins, outs, send_sems, recv_sems)
        ex.finish(ins, outs, send_sems, recv_sems)

    any_spec = pl.BlockSpec(memory_space=pl.ANY)
    return pl.pallas_call(
        body, out_shape=[jax.ShapeDtypeStruct(ex.arrays[a].shape, ex.arrays[a].dtype) for a in ex.aliased],
        in_specs=[any_spec] * n_in, out_specs=[any_spec] * n_out,
        input_output_aliases={a: o for o, a in enumerate(ex.aliased)},
        scratch_shapes=[pltpu.SemaphoreType.DMA(shape) for shape in ex.sems for _ in range(2)],
        name=name)(*ex.arrays)


def _rs_to_sibling(grads):
    nt = len(grads)
    landing = [lax.empty((4,) + g.shape[2:], F32) for g in grads]

    def copies(ins, outs, send_sems, recv_sems):
        x, y, c = _place()
        return [pltpu.make_async_remote_copy(
            src_ref=ins[t].at[:, 1 - c], dst_ref=outs[t], send_sem=send_sems.at[t], recv_sem=recv_sems.at[t],
            device_id=(x, y, 1 - c), device_id_type=MESH) for t in range(nt)]

    def start(ins, outs, send_sems, recv_sems):
        for cp in copies(ins, outs, send_sems, recv_sems):
            cp.start()

    def finish(ins, outs, send_sems, recv_sems):
        for cp in copies(ins, outs, send_sems, recv_sems):
            cp.wait()

    return _Exchange(tuple(grads) + tuple(landing), tuple(range(nt, 2 * nt)), ((nt,),), start, finish)


def _add_half(g, recv, cr, name):
    _, _, k2, n = g.shape
    tk = _row_tile(k2)

    def body(cr_ref, g_ref, r_ref, sums_ref, mine_ref):
        val = (g_ref[...] + r_ref[...]).astype(BF16)
        sums_ref[...] = val

        @pl.when(pl.program_id(1) == cr_ref[1])
        def _():
            mine_ref[...] = val

    return pl.pallas_call(
        body, out_shape=[jax.ShapeDtypeStruct((4, k2, n), BF16)] * 2,
        grid_spec=pltpu.PrefetchScalarGridSpec(
            num_scalar_prefetch=1, grid=(k2 // tk, 4),
            in_specs=[pl.BlockSpec((None, None, tk, n), lambda i, q, cr_ref: (q, cr_ref[0], i, 0)),
                      pl.BlockSpec((None, tk, n), lambda i, q, cr_ref: (q, i, 0))],
            out_specs=[pl.BlockSpec((None, tk, n), lambda i, q, cr_ref: (q, i, 0)),
                       pl.BlockSpec((None, tk, n), lambda i, q, cr_ref: (cr_ref[1], i, 0))]),
        compiler_params=_params(2), name=name)(cr, g, recv)


def _rs_to_chips(sums, parts):
    nt = len(sums)

    def copies(ins, outs, send_sems, recv_sems):
        x, y, c = _place()
        r = 2 * x + y
        return [pltpu.make_async_remote_copy(
            src_ref=ins[t].at[r ^ k], dst_ref=outs[t].at[r], send_sem=send_sems.at[t, k - 1],
            recv_sem=recv_sems.at[t, k - 1], device_id=(*_chip_peer(x, y, k), c), device_id_type=MESH)
            for t in range(nt) for k in (1, 2, 3)]

    def start(ins, outs, send_sems, recv_sems):
        for cp in copies(ins, outs, send_sems, recv_sems):
            cp.start()

    def finish(ins, outs, send_sems, recv_sems):
        for cp in copies(ins, outs, send_sems, recv_sems):
            cp.wait()

    return _Exchange(tuple(sums) + tuple(parts), tuple(range(nt, 2 * nt)), ((nt, 3),), start, finish)


def _sum4(parts, cr, name):
    _, k2, n = parts.shape
    tk = _row_tile(k2)

    def body(cr_ref, p_ref, o_ref):
        p = p_ref[...].astype(F32)
        o_ref[...] = ((p[0] + p[1]) + p[2]) + p[3]

    return pl.pallas_call(
        body, out_shape=jax.ShapeDtypeStruct((2, k2, n), F32),
        grid_spec=pltpu.PrefetchScalarGridSpec(
            num_scalar_prefetch=1, grid=(k2 // tk,),
            in_specs=[pl.BlockSpec((4, tk, n), lambda i, cr_ref: (0, i, 0))],
            out_specs=pl.BlockSpec((None, tk, n), lambda i, cr_ref: (cr_ref[0], i, 0))),
        compiler_params=_params(1), name=name)(cr, parts)


def _exchange_halves(both):
    nt = len(both)

    def copies(ins, outs, send_sems, recv_sems):
        x, y, c = _place()
        return [pltpu.make_async_remote_copy(
            src_ref=ins[t].at[c], dst_ref=outs[t].at[c], send_sem=send_sems.at[t], recv_sem=recv_sems.at[t],
            device_id=(x, y, 1 - c), device_id_type=MESH) for t in range(nt)]

    def start(ins, outs, send_sems, recv_sems):
        for cp in copies(ins, outs, send_sems, recv_sems):
            cp.start()

    def finish(ins, outs, send_sems, recv_sems):
        for cp in copies(ins, outs, send_sems, recv_sems):
            cp.wait()

    return _Exchange(tuple(both), tuple(range(nt)), ((nt,),), start, finish)


def _adamw_math(w, g, m, v):
    m = ADAM_B1 * m + (1.0 - ADAM_B1) * g
    v = ADAM_B2 * v + (1.0 - ADAM_B2) * jnp.square(g)
    m_hat = m / (1.0 - ADAM_B1 ** ADAM_STEP)
    v_hat = v / (1.0 - ADAM_B2 ** ADAM_STEP)
    delta = -ADAM_LR * (m_hat / (jnp.sqrt(v_hat) + ADAM_EPS) + ADAM_WD * w)
    return delta, m, v


def _adamw(w, m, v, g0, g1, name):
    _, k, n = w.shape
    tk = _row_tile(k)
    nk = k // tk

    def body(w_ref, m_ref, v_ref, g0_ref, g1_ref, grad_ref, delta_ref, nm_ref, nv_ref):
        g = jnp.where(pl.program_id(0) == 0, g0_ref[...], g1_ref[...])
        delta, nm, nv = _adamw_math(w_ref[...], g, m_ref[...], v_ref[...])
        grad_ref[...] = g
        delta_ref[...] = delta
        nm_ref[...] = nm
        nv_ref[...] = nv

    lay = pl.BlockSpec((None, tk, n), lambda a, i: (a, i, 0))
    g0_spec = pl.BlockSpec((tk, n), lambda a, i: (jnp.where(a == 0, i, nk - 1), 0))
    g1_spec = pl.BlockSpec((tk, n), lambda a, i: (jnp.where(a == 1, i, 0), 0))
    return pl.pallas_call(
        body, out_shape=[jax.ShapeDtypeStruct(w.shape, F32)] * 4, grid=(2, nk),
        in_specs=[lay, lay, lay, g0_spec, g1_spec], out_specs=[lay] * 4,
        compiler_params=_params(2), name=name)(w, m, v, g0, g1)


def _small_allreduce_adamw(gpart, w, m, v):
    shape = gpart.shape

    def body(g_ref, w_ref, m_ref, v_ref, gsum_ref, delta_ref, nm_ref, nv_ref, recv_ref, send_sems, recv_sems):
        x, y, c = _place()
        me = 4 * x + 2 * y + c
        recv_ref[me] = g_ref[...]
        cps = []
        for k in range(1, 8):
            peer = (x ^ (k >> 2), y ^ ((k >> 1) & 1), c ^ (k & 1))
            cp = pltpu.make_async_remote_copy(
                src_ref=g_ref, dst_ref=recv_ref.at[me], send_sem=send_sems.at[k - 1], recv_sem=recv_sems.at[k - 1],
                device_id=peer, device_id_type=MESH)
            cp.start()
            cps.append(cp)
        for cp in cps:
            cp.wait()
        g = recv_ref[0]
        for dev in range(1, 8):
            g = g + recv_ref[dev]
        delta, nm, nv = _adamw_math(w_ref[...], g, m_ref[...], v_ref[...])
        gsum_ref[...] = g
        delta_ref[...] = delta
        nm_ref[...] = nm
        nv_ref[...] = nv

    vm = pl.BlockSpec(memory_space=pltpu.VMEM)
    return pl.pallas_call(
        body, out_shape=[jax.ShapeDtypeStruct(shape, F32)] * 4, in_specs=[vm] * 4, out_specs=[vm] * 4,
        scratch_shapes=[pltpu.VMEM((8,) + shape, F32), pltpu.SemaphoreType.DMA((7,)), pltpu.SemaphoreType.DMA((7,))],
        name="small_allreduce_adamw")(gpart, w, m, v)


BIG = ("w_in", "w_up_a", "w_up_b", "w_o", "w_ff1", "w_ff2", "w_pe", "w_pg")
COL_SHARDED = ("w_in", "w_up_a", "w_up_b", "w_ff1", "w_pe")
ROW_SHARDED = ("w_o", "w_ff2", "w_pg")
SMALL_ROWS = 16


def _pack_small(g_mix, g_mlp, g_pe, g_final, sinks, rel_bias, loss=None):
    d = g_final.shape[0]
    row = lambda v: jnp.pad(v.reshape(1, -1), ((0, 0), (0, d - v.size)))
    rows = [g_mix, g_mlp, g_pe, g_final.reshape(1, d),
            jnp.zeros((1, d), F32) if loss is None else row(loss), row(sinks), row(rel_bias)]
    out = jnp.concatenate(rows, axis=0)
    return jnp.pad(out, ((0, SMALL_ROWS - out.shape[0]), (0, 0)))


def _unpack_small(a, sinks_shape, rel_shape):
    return (a[0:2], a[2:4], a[4:6], a[6], a[8, :sinks_shape[0] * sinks_shape[1]].reshape(sinks_shape),
            a[9, :rel_shape[0] * rel_shape[1]].reshape(rel_shape))


def kernel(x, p, w_in, w_up_a, w_up_b, w_o, w_ff1, w_ff2, w_pe, w_pg, g_mix, g_mlp, g_pe, g_final, sinks, rel_bias, loss_target, m_w_in, m_w_up_a, m_w_up_b, m_w_o, m_w_ff1, m_w_ff2, m_w_pe, m_w_pg, m_g_mix, m_g_mlp, m_g_pe, m_g_final, m_sinks, m_rel_bias, v_w_in, v_w_up_a, v_w_up_b, v_w_o, v_w_ff1, v_w_ff2, v_w_pe, v_w_pg, v_g_mix, v_g_mlp, v_g_pe, v_g_final, v_sinks, v_rel_bias):
    depth = w_in.shape[0]
    assert depth == 2
    x0 = x[0]
    target = loss_target[0]
    d = x0.shape[1]
    wl = dict(w_in=w_in, w_up_a=w_up_a, w_up_b=w_up_b, w_o=w_o, w_ff1=w_ff1, w_ff2=w_ff2, w_pe=w_pe, w_pg=w_pg)
    ml = dict(w_in=m_w_in, w_up_a=m_w_up_a, w_up_b=m_w_up_b, w_o=m_w_o, w_ff1=m_w_ff1, w_ff2=m_w_ff2, w_pe=m_w_pe, w_pg=m_w_pg)
    vl = dict(w_in=v_w_in, w_up_a=v_w_up_a, w_up_b=v_w_up_b, w_o=v_w_o, w_ff1=v_w_ff1, w_ff2=v_w_ff2, w_pe=v_w_pe, w_pg=v_w_pg)
    c_idx = lax.axis_index("c").astype(jnp.int32)
    r_idx = (2 * lax.axis_index("x") + lax.axis_index("y")).astype(jnp.int32)
    cr = jnp.stack([c_idx, r_idx])

    wl["w_in"], ml["w_in"], vl["w_in"] = (jnp.swapaxes(a, 1, 2) for a in (w_in, m_w_in, v_w_in))

    bufs = {}
    for n in BIG:
        k, nn = wl[n].shape[1:]
        for l, b in enumerate(_cast_bf16(wl[n], r_idx.reshape(1), "cast_" + n)):
            bufs[n, l] = b.reshape(4, 2, k // 2, nn)

    def gather(keys, run):
        for key, b in zip(keys, run(_all_gather([bufs[key] for key in keys]))):
            bufs[key] = b

    def gathered(n, l):
        _, _, k2, nn = bufs[n, l].shape
        if n in ROW_SHARDED or n == "w_in":
            return bufs[n, l].reshape(8 * k2, nn)
        return bufs[n, l].reshape(4, 2 * k2, nn)

    gather([("w_in", 0)], lambda ex: _run_exchange("all_gather_first", ex))

    full = {n: [None] * depth for n in BIG}
    saved = []
    xi = x0
    for i in range(depth):
        st = dict(x0=xi)
        gm = g_mix[i].reshape(1, d)
        full["w_in"][i] = gathered("w_in", i)
        st["h1"], st["qkv"], st["gates"] = _inproj_fwd(xi, gm, full["w_in"][i], f"inproj_fwd_{i}")

        def attend(ex):
            (st["oa"], st["lt"], st["nb"], st["a_wide"], st["sg_wide"]), filled = _sb_fwd(st["qkv"], f"sb_fwd_{i}", ex)
            return filled

        with_window = [("w_in", 1)] if i == 0 else [(n, i) for n in ("w_up_a", "w_up_b", "w_o")]
        gather([(n, i) for n in BIG if n != "w_in" and (n, i) not in with_window], attend)

        def window(ex):
            (st["ob"], st["probs"]), filled = _swa_fwd(st["qkv"], sinks[i], rel_bias, f"swa_fwd_{i}", ex)
            return filled

        gather(with_window, window)
        for n in BIG:
            if n != "w_in":
                full[n][i] = gathered(n, i)
        st["m"], st["x1"] = _mixer_fwd(st["oa"], st["ob"], st["gates"], xi, full["w_up_a"][i], full["w_up_b"][i],
                                       full["w_o"][i], f"mixer_fwd_{i}")
        st["h2"], st["u"], st["a"], st["x2"] = _ff_fwd(st["x1"], g_mlp[i].reshape(1, d), full["w_ff1"][i],
                                                       full["w_ff2"][i], f"ff_fwd_{i}")
        ple_args = (p[i, 0], st["x2"], g_pe[i].reshape(1, d), full["w_pe"][i], full["w_pg"][i])
        if i < depth - 1:
            st["pb"], st["h3"], st["pe"], st["gt"], xi = _ple_fwd(*ple_args, f"ple_fwd_{i}")
        else:
            st["pb"], st["h3"], st["pe"], st["gt"], dx, dg_final, loss_part = _ple_fwd_loss(
                *ple_args, target, g_final.reshape(1, d), f"ple_fwd_loss_{i}")
        saved.append(st)

    gw = {n: [None] * depth for n in BIG}
    reduced = {}

    chip_sums = {}

    def by_halves(keys):
        tensors = []
        for n, l in keys:
            g = gw[n][l]
            if n in ROW_SHARDED:
                ka, nb = g.shape[1:]
                g = g.reshape(4, ka // 4, nb)
            _, k, nn = g.shape
            tensors.append(g.reshape(4, 2, k // 2, nn))
        return tensors

    def add_halves(keys, tensors, landed):
        for (n, l), g, r in zip(keys, tensors, landed):
            chip_sums[n, l] = _add_half(g, r, cr, f"add_half_{n}_{l}")

    def to_sibling(keys, run):
        tensors = by_halves(keys)
        add_halves(keys, tensors, run(_rs_to_sibling(tensors)))

    def to_chips(keys):
        return _rs_to_chips([chip_sums[k][0] for k in keys], [chip_sums[k][1] for k in keys])

    halves = {}

    def sum_chips(keys, parts):
        for (n, l), pc in zip(keys, parts):
            halves[n, l] = _sum4(pc, cr, f"sum4_{n}_{l}")

    def swap_halves(keys):
        def store(filled):
            for key, both in zip(keys, filled):
                reduced[key] = both
        return _exchange_halves([halves[k] for k in keys]), store

    dg_mix, dg_mlp, dg_pe, dsinks = [None] * depth, [None] * depth, [None] * depth, [None] * depth
    drel = jnp.zeros((SW_HEADS, LANES), F32)
    for i in reversed(range(depth)):
        st = saved[i]
        dpe, dgt, dx2, dg_pe[i] = _ple_bwd(dx, st["pe"], st["gt"], st["x2"], g_pe[i].reshape(1, d),
                                           full["w_pg"][i], f"ple_bwd_{i}")
        gw["w_pe"][i] = _mm_tn(st["pb"], dpe, f"dw_pe_{i}", 4)
        gw["w_pg"][i] = _mm_tn(st["h3"], dgt, f"dw_pg_{i}")
        ff_args = (dx2, st["u"], st["x1"], g_mlp[i].reshape(1, d), full["w_ff1"][i], full["w_ff2"][i], f"ff_bwd_{i}")
        if i == 0:
            (du, dx2b, dx1, dx1b, dg_mlp[i]), parts = _ff_bwd(*ff_args, to_chips([("w_in", 1)]))
            sum_chips([("w_in", 1)], parts)
        else:
            du, dx2b, dx1, dx1b, dg_mlp[i] = _ff_bwd(*ff_args)
        gw["w_ff2"][i] = _mm_tn(st["a"], dx2b, f"dw_ff2_{i}")
        gw["w_ff1"][i] = _mm_tn(st["h2"], du, f"dw_ff1_{i}", 4)
        gw["w_o"][i] = _mm_tn(st["m"], dx1b, f"dw_o_{i}")
        early = [(n, i) for n in ("w_pe", "w_pg", "w_ff2", "w_ff1", "w_o")]

        def mixer(ex):
            (dya, dyb, dgates, doa, dob), landed = _mixer_bwd(
                dx1b, st["gates"], st["oa"], st["ob"], full["w_o"][i], full["w_up_a"][i], full["w_up_b"][i],
                f"mixer_bwd_{i}", ex)
            st.update(dya=dya, dyb=dyb, dgates=dgates, doa=doa, dob=dob)
            return landed

        to_sibling(early, mixer)
        dgates = st["dgates"]
        gw["w_up_a"][i] = _mm_tn(st["oa"], st["dya"], f"dw_up_a_{i}", 4)
        gw["w_up_b"][i] = _mm_tn(st["ob"], st["dyb"], f"dw_up_b_{i}", 4)
        late = [("w_up_a", i), ("w_up_b", i)]
        late_halves = by_halves(late)
        keys = early + late
        with_sb = [(n, i) for n in ("w_ff1", "w_o", "w_pg", "w_pe")]
        with_swa = [(n, i) for n in ("w_ff2", "w_up_a", "w_up_b")]
        dqa, dka, dva, filled = _sb_bwd(st["qkv"], st["lt"], st["nb"], st["a_wide"], st["sg_wide"], st["doa"],
                                        f"sb_bwd_{i}", _both(_rs_to_sibling(late_halves), to_chips(with_sb)))
        add_halves(late, late_halves, filled[:len(late)])
        sum_chips(with_sb, filled[len(late):])
        (dqb, dkvb, dsk, drl), parts = _swa_bwd(st["qkv"], st["ob"], st["dob"], st["probs"], f"swa_bwd_{i}",
                                                to_chips(with_swa))
        sum_chips(with_swa, parts)
        dsinks[i] = dsk[:, 0]
        drel = drel + drl
        dproj = [dqa, dka, dva, dqb, dkvb, dgates]
        swap, store = swap_halves(keys + ([("w_in", 1)] if i == 0 else []))
        dw_in_t, filled = _mm_tn_pieces(dproj, st["h1"], f"dw_in_{i}", swap)
        store(filled)
        gw["w_in"][i] = dw_in_t.reshape(4, dw_in_t.shape[0] // 4, d)
        if i == 1:
            def inproj(ex):
                (dx, dg_mix[i]), landed = _inproj_bwd(dproj, dx1, st["x0"], g_mix[i].reshape(1, d),
                                                      full["w_in"][i], f"inproj_bwd_{i}", ex)
                st["dx"] = dx
                return landed

            to_sibling([("w_in", 1)], inproj)
            dx = st["dx"]
        else:
            keys = [("w_in", 0)]
            to_sibling(keys, lambda ex: _run_exchange("rs_to_sibling_last", ex))
            (dx, dg_mix[i]), parts = _inproj_bwd(dproj, dx1, st["x0"], g_mix[i].reshape(1, d),
                                                 full["w_in"][i], f"inproj_bwd_{i}", to_chips(keys))
            sum_chips(keys, parts)
            swap, store = swap_halves(keys)
            store(_run_exchange("exchange_halves_last", swap))
    grad_x = dx[None]

    outs = {}
    for n in BIG:
        g0, g1 = (reduced[n, l].reshape(wl[n].shape[1:]) for l in range(depth))
        outs[n] = _adamw(wl[n], ml[n], vl[n], g0, g1, "adamw_" + n)
    outs["w_in"] = [jnp.swapaxes(a, 1, 2) for a in outs["w_in"]]

    drel_bias = drel[:, :N_BUCKETS].T
    gsmall = _pack_small(jnp.concatenate(dg_mix, 0), jnp.concatenate(dg_mlp, 0), jnp.concatenate(dg_pe, 0),
                         dg_final[0], jnp.stack(dsinks), drel_bias, loss_part[0, :1])
    wsmall = _pack_small(g_mix, g_mlp, g_pe, g_final, sinks, rel_bias)
    msmall = _pack_small(m_g_mix, m_g_mlp, m_g_pe, m_g_final, m_sinks, m_rel_bias)
    vsmall = _pack_small(v_g_mix, v_g_mlp, v_g_pe, v_g_final, v_sinks, v_rel_bias)
    small = _small_allreduce_adamw(gsmall, wsmall, msmall, vsmall)
    loss = small[0][7, 0]
    small = [_unpack_small(a, sinks.shape, rel_bias.shape) for a in small]

    result = [loss, grad_x]
    for kind in range(4):
        result += [outs[n][kind] for n in BIG]
        result += list(small[kind])
    return tuple(result)
```

```python
import functools
import math
from typing import Callable, NamedTuple

import numpy as np
import jax
import jax.numpy as jnp
from jax import lax
from jax.experimental import pallas as pl
from jax.experimental.pallas import tpu as pltpu

F32 = jnp.float32
BF16 = jnp.bfloat16
MESH = pl.DeviceIdType.MESH

HEAD_DIM = 64
SB_HEADS = 8
SW_HEADS = 8
SW_KV_HEADS = 2
WINDOW = 128
N_BUCKETS = 32
MAX_DISTANCE = 128
EPS = 1e-6
SB_W = SB_HEADS * HEAD_DIM
SW_QW = SW_HEADS * HEAD_DIM
SW_KVW = SW_KV_HEADS * HEAD_DIM
QKV_W = 3 * SB_W + SW_QW + 2 * SW_KVW
SCALE = HEAD_DIM ** -0.5
assert SCALE == 0.125
LANES = 128
TQ = 128
BK = 128
NEG = -1e30
SB_EXHAUSTED = -106.0

ADAM_LR = 0.001
ADAM_B1 = 0.9
ADAM_B2 = 0.999
ADAM_EPS = 1e-08
ADAM_WD = 0.01
ADAM_STEP = 10

VMEM_LIMIT = 56 * 1024 * 1024


def _dot(a, b):
    return jnp.dot(a, b, preferred_element_type=F32)


def _dot_nt(a, b):
    return lax.dot_general(a, b, (((1,), (1,)), ((), ())), preferred_element_type=F32)


def _dot_tn(a, b):
    return lax.dot_general(a, b, (((0,), (0,)), ((), ())), preferred_element_type=F32)


def _sum_all(x):
    return jnp.sum(jnp.sum(x, axis=1, keepdims=True), axis=0, keepdims=True)


def _sigmoid(x):
    return 1.0 / (1.0 + jnp.exp(-x))


def _rms(x, g):
    r = lax.rsqrt(jnp.mean(x * x, axis=-1, keepdims=True) + EPS)
    return (x * r) * g


def _rms_bwd(dy, x, g):
    r = lax.rsqrt(jnp.mean(x * x, axis=-1, keepdims=True) + EPS)
    n = x * r
    dg = jnp.sum(dy * n, axis=0, keepdims=True)
    dn = dy * g
    dx = r * (dn - n * jnp.mean(dn * n, axis=-1, keepdims=True))
    return dx, dg


def _params(n_axes):
    return pltpu.CompilerParams(dimension_semantics=("arbitrary",) * n_axes, vmem_limit_bytes=VMEM_LIMIT)


def _rowcall(name, body, row_ins, const_ins, row_outs, acc_outs=(), tm=512, side=None):
    s = row_ins[0].shape[0]
    assert s % tm == 0
    in_specs = [pl.BlockSpec((tm, a.shape[1]), lambda i: (i, 0)) for a in row_ins]
    in_specs += [pl.BlockSpec(a.shape, functools.partial(lambda i, nd: (0,) * nd, nd=a.ndim)) for a in const_ins]
    out_shape = [jax.ShapeDtypeStruct((s, c), dt) for c, dt in row_outs]
    out_specs = [pl.BlockSpec((tm, c), lambda i: (i, 0)) for c, _ in row_outs]
    out_shape += [jax.ShapeDtypeStruct(sh, dt) for sh, dt in acc_outs]
    out_specs += [pl.BlockSpec(sh, functools.partial(lambda i, nd: (0,) * nd, nd=len(sh))) for sh, _ in acc_outs]
    if side is None:
        return pl.pallas_call(body, out_shape=out_shape, grid=(s // tm,), in_specs=in_specs, out_specs=out_specs,
                              compiler_params=_params(1), name=name)(*row_ins, *const_ins)
    n_in, n_out = len(in_specs), len(out_specs)

    def with_side(*refs):
        side_in, outs, side_out, sems = _side_refs(side, refs[n_in:], n_out)

        @pl.when(pl.program_id(0) == 0)
        def _():
            side.start(side_in, side_out, *sems)

        body(*refs[:n_in], *outs)

        @pl.when(pl.program_id(0) == s // tm - 1)
        def _():
            side.finish(side_in, side_out, *sems)

    s_in, s_shape, s_out, s_alias, s_sems = _side_specs(side, n_in, n_out)
    outs = pl.pallas_call(with_side, out_shape=out_shape + s_shape, grid=(s // tm,), in_specs=in_specs + s_in,
                          out_specs=out_specs + s_out, input_output_aliases=s_alias, scratch_shapes=s_sems,
                          compiler_params=_params(1), name=name)(*row_ins, *const_ins, *side.arrays)
    return outs[:n_out], outs[n_out:]


def _dot_cols(a, w_ref):
    return jnp.concatenate([_dot(a, w_ref[r]) for r in range(w_ref.shape[0])], axis=1)


def _dot_cols_t(a, w_ref):
    n = w_ref.shape[2]
    out = _dot_nt(a[:, :n], w_ref[0])
    for r in range(1, w_ref.shape[0]):
        out = out + _dot_nt(a[:, r * n:(r + 1) * n], w_ref[r])
    return out


def _inproj_fwd(x, g, wt, name):
    d = x.shape[1]

    def body(x_ref, g_ref, w_ref, h_ref, qkv_ref, gate_ref):
        hb = _rms(x_ref[...], g_ref[...]).astype(BF16)
        h_ref[...] = hb
        qkv_ref[...] = _dot_nt(hb, w_ref[:QKV_W, :]).astype(BF16)
        gate_ref[...] = _dot_nt(hb, w_ref[QKV_W:, :])

    return _rowcall(name, body, [x], [g, wt], [(d, BF16), (QKV_W, BF16), (2 * d, F32)])


def _mixer_fwd(oa, ob, gates, x, wua, wub, wo, name):
    d = x.shape[1]

    def body(oa_ref, ob_ref, gate_ref, x_ref, wua_ref, wub_ref, wo_ref, m_ref, x1_ref):
        ya = _dot_cols(oa_ref[...], wua_ref)
        yb = _dot_cols(ob_ref[...], wub_ref)
        m = _sigmoid(gate_ref[:, :d]) * ya + _sigmoid(gate_ref[:, d:]) * yb
        mb = m.astype(BF16)
        m_ref[...] = mb
        x1_ref[...] = x_ref[...] + _dot(mb, wo_ref[...])

    return _rowcall(name, body, [oa, ob, gates, x], [wua, wub, wo], [(d, BF16), (d, F32)])


def _ff_fwd(x1, g, w1, w2, name):
    _, d, nq = w1.shape
    dff = 4 * nq

    def body(x_ref, g_ref, w1_ref, w2_ref, h_ref, r_ref, a_ref, o_ref):
        x = x_ref[...]
        hb = _rms(x, g_ref[...]).astype(BF16)
        h_ref[...] = hb
        r = jnp.maximum(_dot_cols(hb, w1_ref), 0.0)
        r_ref[...] = r.astype(BF16)
        a = jnp.square(r).astype(BF16)
        a_ref[...] = a
        o_ref[...] = x + _dot(a, w2_ref[...])

    return _rowcall(name, body, [x1], [g, w1, w2], [(d, BF16), (dff, BF16), (dff, BF16), (d, F32)], tm=256)


def _ple_fwd(p, x2, g, wpe, wpg, name):
    d = x2.shape[1]

    def body(p_ref, x_ref, g_ref, wpe_ref, wpg_ref, pb_ref, h_ref, pe_ref, gt_ref, x3_ref):
        pb = p_ref[...].astype(BF16)
        pb_ref[...] = pb
        pe = _dot_cols(pb, wpe_ref)
        x = x_ref[...]
        hb = _rms(x, g_ref[...]).astype(BF16)
        h_ref[...] = hb
        gt = _dot(hb, wpg_ref[...])
        pe_ref[...] = pe
        gt_ref[...] = gt
        x3_ref[...] = x + pe * _sigmoid(gt)

    return _rowcall(name, body, [p, x2], [g, wpe, wpg],
                    [(p.shape[1], BF16), (d, BF16), (d, F32), (d, F32), (d, F32)])


def _pair_stack(t, lane):
    zero = jnp.zeros_like(t)
    return jnp.concatenate([jnp.where(lane < HEAD_DIM, t, zero), jnp.where(lane >= HEAD_DIM, t, zero)], axis=0)


def _sb_rel():
    row = lax.broadcasted_iota(jnp.int32, (2 * TQ, BK), 0)
    row = jnp.where(row >= TQ, row - TQ, row)
    col = lax.broadcasted_iota(jnp.int32, (2 * TQ, BK), 1)
    return col - row


def _split_dot(x, m01, two_pass=True):
    hi = x.astype(BF16)
    if not two_pass:
        return _dot(hi, m01)
    lo = (x - hi.astype(F32)).astype(BF16)
    return _dot(hi, m01) + _dot(lo, m01)


def _sb_scores(qs, k, mask):
    z = _dot_nt(qs, k)
    lb = jnp.minimum(z, 0.0) - jnp.log(1.0 + jnp.exp(-jnp.abs(z)))
    lm = lb - z
    return lb, lm if mask is None else jnp.where(mask, lm, 0.0)


SB_STRAIGHT = 3
SB_WIDE = SB_STRAIGHT * BK
SB_QB = 2
SWA_QB = 4


def _sb_wide_consts():
    j = np.arange(BK)[:, None]
    s = np.arange(BK)[None, :]
    ones = np.ones((BK, BK), np.float32)
    as_bf16 = lambda m: jnp.asarray(np.concatenate([m, ones], axis=1).astype(np.float32), dtype=BF16)
    return as_bf16(j > s), as_bf16(j <= s), as_bf16(j < s)


def _wide_sums(x, m01, suffix, two_pass=True):
    parts = [_split_dot(x[:, b * BK:(b + 1) * BK], m01, two_pass) for b in range(SB_STRAIGHT)]
    order = range(SB_STRAIGHT - 1, -1, -1) if suffix else range(SB_STRAIGHT)
    out = [None] * SB_STRAIGHT
    carry = None
    for b in order:
        out[b] = parts[b][:, :BK] if carry is None else parts[b][:, :BK] + carry
        carry = parts[b][:, BK:] if carry is None else carry + parts[b][:, BK:]
    return jnp.concatenate(out, axis=1), carry


def _side_refs(ex, rest, n_out):
    n_in = len(ex.arrays) if ex else 0
    n_alias = len(ex.aliased) if ex else 0
    ins, rest = rest[:n_in], rest[n_in:]
    outs, rest = rest[:n_out], rest[n_out:]
    return ins, outs, rest[:n_alias], rest[n_alias:]


def _side_specs(ex, n_in, n_out):
    if ex is None:
        return [], [], [], {}, []
    any_spec = pl.BlockSpec(memory_space=pl.ANY)
    return ([any_spec] * len(ex.arrays),
            [jax.ShapeDtypeStruct(ex.arrays[a].shape, ex.arrays[a].dtype) for a in ex.aliased],
            [any_spec] * len(ex.aliased), {n_in + a: n_out + o for o, a in enumerate(ex.aliased)},
            [pltpu.SemaphoreType.DMA(shape) for shape in ex.sems for _ in range(2)])


def _sb_fwd(qkv, name, side=None):
    s = qkv.shape[0]
    nq = s // TQ
    npair = SB_W // LANES
    sufw = _sb_wide_consts()[0]

    def body(q_ref, k_ref, v_ref, sufw_ref, *rest):
        side_in, (o_ref, lt_ref, nb_ref, a_ref, sg_ref), side_out, scratch = _side_refs(side, rest, 5)
        cf_ref, acc_ref = scratch[:2]
        step_id = pl.program_id(1)
        if side is not None:
            @pl.when((pl.program_id(0) == 0) & (step_id == 0))
            def _():
                side.start(side_in, side_out, *scratch[2:])
        lane = lax.broadcasted_iota(jnp.int32, (TQ, LANES), 1)
        rel = _sb_rel()
        blocks = [step_id * SB_QB + b for b in range(SB_QB)]
        qs = [_pair_stack(q_ref[b * TQ:(b + 1) * TQ, :] * SCALE, lane) for b in range(SB_QB)]

        straight = blocks[0] >= SB_STRAIGHT - 1

        @pl.when(straight)
        def _():
            for b, i in enumerate(blocks):
                w0 = pl.multiple_of((i - (SB_STRAIGHT - 1)) * BK, BK)
                kw = k_ref[pl.ds(w0, SB_WIDE), :]
                lb, lm = _sb_scores(qs[b], kw, None)
                own = rel < 0
                past = SB_WIDE - BK
                lm = jnp.concatenate([lm[:, :past], jnp.where(own, lm[:, past:], 0.0)], axis=1)
                after, total = _wide_sums(lm, sufw_ref[...], True)
                on_past_keys = lambda t: jnp.concatenate([t[:, :past], jnp.where(own, t[:, past:], 0.0)], axis=1)
                a = on_past_keys(jnp.exp(lb + after)).astype(BF16)
                acc_ref[b] = _dot(a, v_ref[pl.ds(w0, SB_WIDE), :])
                cf_ref[b] = total
                a_ref[b] = a
                sg_ref[b] = on_past_keys(jnp.exp(lb)).astype(BF16)

        @pl.when(jnp.logical_not(straight))
        def _():
            cf_ref[...] = jnp.zeros_like(cf_ref)
            acc_ref[...] = jnp.zeros_like(acc_ref)

        for b, i in enumerate(blocks):
            q0 = i * TQ

            def more(c, i=i):
                return (c[0] <= i) & (c[1] > SB_EXHAUSTED)

            def step(c, b=b, i=i, q0=q0):
                k0 = pl.multiple_of((i - c[0]) * BK, BK)
                k = k_ref[pl.ds(k0, BK), :]
                v = v_ref[pl.ds(k0, BK), :]
                mask = rel < (q0 - k0)
                lb, lm = _sb_scores(qs[b], k, mask)
                cs = _split_dot(lm, sufw_ref[...])
                a = jnp.where(mask, jnp.exp(lb + cs[:, :BK] + cf_ref[b]), 0.0)
                acc_ref[b] += _dot(a.astype(BF16), v)
                cf = cf_ref[b] + cs[:, BK:]
                cf_ref[b] = cf
                return c[0] + 1, jnp.max(cf)

            n_blocks, _ = lax.while_loop(
                more, step, (jnp.where(straight, SB_STRAIGHT, 0).astype(jnp.int32), jnp.max(cf_ref[b])))
            o_ref[b * TQ:(b + 1) * TQ, :] = jnp.where(lane < HEAD_DIM, acc_ref[b, :TQ, :],
                                                     acc_ref[b, TQ:, :]).astype(BF16)
            lt_ref[b] = cf_ref[b]
            nb_ref[b] = jnp.full(nb_ref.shape[1:], n_blocks, F32)
        if side is not None:
            @pl.when((pl.program_id(0) == npair - 1) & (step_id == nq // SB_QB - 1))
            def _():
                side.finish(side_in, side_out, *scratch[2:])

    s_in, s_shape, s_out, s_alias, s_sems = _side_specs(side, 4, 5)
    wide = jax.ShapeDtypeStruct((npair, nq, 2 * TQ, SB_WIDE), BF16)
    wide_spec = pl.BlockSpec((None, SB_QB, 2 * TQ, SB_WIDE), lambda j, i: (j, i, 0, 0))
    outs = pl.pallas_call(
        body,
        out_shape=[jax.ShapeDtypeStruct((s, SB_W), BF16), jax.ShapeDtypeStruct((npair, nq, 2 * TQ, BK), F32),
                   jax.ShapeDtypeStruct((npair, nq, 8, LANES), F32), wide, wide] + s_shape,
        grid=(npair, nq // SB_QB),
        in_specs=[pl.BlockSpec((SB_QB * TQ, LANES), lambda j, i: (i, j)),
                  pl.BlockSpec((s, LANES), lambda j, i: (0, npair + j)),
                  pl.BlockSpec((s, LANES), lambda j, i: (0, 2 * npair + j)),
                  pl.BlockSpec(sufw.shape, lambda j, i: (0, 0))] + s_in,
        out_specs=[pl.BlockSpec((SB_QB * TQ, LANES), lambda j, i: (i, j)),
                   pl.BlockSpec((None, SB_QB, 2 * TQ, BK), lambda j, i: (j, i, 0, 0)),
                   pl.BlockSpec((None, SB_QB, 8, LANES), lambda j, i: (j, i, 0, 0)), wide_spec, wide_spec] + s_out,
        input_output_aliases=s_alias,
        scratch_shapes=[pltpu.VMEM((SB_QB, 2 * TQ, BK), F32), pltpu.VMEM((SB_QB, 2 * TQ, LANES), F32)] + s_sems,
        compiler_params=_params(2), name=name)(qkv, qkv, qkv, sufw, *(side.arrays if side else ()))
    return outs[:5], outs[5:]


def _sb_bwd(qkv, lt, nb, a_wide, sg_wide, doa, name, side=None):
    s = qkv.shape[0]
    nq = s // TQ
    npair = SB_W // LANES
    _, prew, prexw = _sb_wide_consts()

    def body(q_ref, k_ref, v_ref, lt_ref, nb_ref, do_ref, prew_ref, prexw_ref, a_ref, sg_ref, *rest):
        side_in, (dq_ref, dk_out, dv_out), side_out, scratch = _side_refs(side, rest, 3)
        cp_ref, ce_ref, dqa_ref, dk_ref, dv_ref = scratch[:5]
        sems = scratch[5:]
        step_id = pl.program_id(1)
        if side is not None:
            @pl.when((pl.program_id(0) == 0) & (step_id == 0))
            def _():
                side.start(side_in, side_out, *sems)
        lane = lax.broadcasted_iota(jnp.int32, (TQ, LANES), 1)
        rel = _sb_rel()
        blocks = [step_id * SB_QB + b for b in range(SB_QB)]
        rows = [slice(b * TQ, (b + 1) * TQ) for b in range(SB_QB)]
        qs = [_pair_stack(q_ref[rows[b], :] * SCALE, lane) for b in range(SB_QB)]
        dos = [_pair_stack(do_ref[rows[b], :], lane) for b in range(SB_QB)]
        n_blocks = [jnp.clip(jnp.max(nb_ref[b]).astype(jnp.int32), 1, i + 1) for b, i in enumerate(blocks)]
        first = [i + 1 - n for i, n in zip(blocks, n_blocks)]

        @pl.when(step_id == 0)
        def _():
            dk_ref[...] = jnp.zeros_like(dk_ref)
            dv_ref[...] = jnp.zeros_like(dv_ref)

        straight = n_blocks[0] == SB_STRAIGHT
        for n in n_blocks[1:]:
            straight = straight & (n == SB_STRAIGHT)

        @pl.when(straight)
        def _():
            for b in range(SB_QB):
                w0 = pl.multiple_of(first[b] * BK, BK)
                kw = k_ref[pl.ds(w0, SB_WIDE), :]
                vw = v_ref[pl.ds(w0, SB_WIDE), :]
                a = a_ref[b]
                e = a.astype(F32) * _dot_nt(dos[b], vw)
                big_e, _ = _wide_sums(e, prexw_ref[...], False, two_pass=False)
                dz = (e - sg_ref[b].astype(F32) * (e + big_e)).astype(BF16)
                dk_ref[pl.ds(w0, SB_WIDE), :] += _dot_tn(dz, qs[b])
                dv_ref[pl.ds(w0, SB_WIDE), :] += _dot_tn(a, dos[b])
                dqa_ref[b] = _dot(dz, kw)

        @pl.when(jnp.logical_not(straight))
        def _():
            cp_ref[...] = jnp.zeros_like(cp_ref)
            ce_ref[...] = jnp.zeros_like(ce_ref)
            dqa_ref[...] = jnp.zeros_like(dqa_ref)
            for b, i in enumerate(blocks):
                q0 = i * TQ

                def step(it, carry, b=b, q0=q0):
                    k0 = pl.multiple_of((first[b] + it) * BK, BK)
                    k = k_ref[pl.ds(k0, BK), :]
                    v = v_ref[pl.ds(k0, BK), :]
                    mask = rel < (q0 - k0)
                    lb, lm = _sb_scores(qs[b], k, mask)
                    cs = _split_dot(lm, prew_ref[...])
                    a = jnp.where(mask, jnp.exp(lb + (lt_ref[b] - (cs[:, :BK] + cp_ref[b]))), 0.0)
                    e = a * _dot_nt(dos[b], v)
                    ce = _split_dot(e, prexw_ref[...], two_pass=False)
                    big_e = ce[:, :BK] + ce_ref[b]
                    dz = jnp.where(mask, e - jnp.exp(lb) * (e + big_e), 0.0).astype(BF16)
                    dk_ref[pl.ds(k0, BK), :] += _dot_tn(dz, qs[b])
                    dv_ref[pl.ds(k0, BK), :] += _dot_tn(a.astype(BF16), dos[b])
                    dqa_ref[b] += _dot(dz, k)
                    cp_ref[b] += cs[:, BK:]
                    ce_ref[b] += ce[:, BK:]
                    return carry

                lax.fori_loop(0, n_blocks[b], step, 0)

        for b in range(SB_QB):
            dq = jnp.where(lane < HEAD_DIM, dqa_ref[b, :TQ, :], dqa_ref[b, TQ:, :])
            dq_ref[rows[b], :] = (dq * SCALE).astype(BF16)

        @pl.when(step_id == nq // SB_QB - 1)
        def _():
            dk_out[...] = dk_ref[...].astype(BF16)
            dv_out[...] = dv_ref[...].astype(BF16)

        if side is not None:
            @pl.when((pl.program_id(0) == npair - 1) & (step_id == nq // SB_QB - 1))
            def _():
                side.finish(side_in, side_out, *sems)

    s_in, s_shape, s_out, s_alias, s_sems = _side_specs(side, 10, 3)
    wide_spec = pl.BlockSpec((None, SB_QB, 2 * TQ, SB_WIDE), lambda j, i: (j, i, 0, 0))
    outs = pl.pallas_call(
        body,
        out_shape=[jax.ShapeDtypeStruct((s, SB_W), BF16)] * 3 + s_shape,
        grid=(npair, nq // SB_QB),
        in_specs=[pl.BlockSpec((SB_QB * TQ, LANES), lambda j, i: (i, j)),
                  pl.BlockSpec((s, LANES), lambda j, i: (0, npair + j)),
                  pl.BlockSpec((s, LANES), lambda j, i: (0, 2 * npair + j)),
                  pl.BlockSpec((None, SB_QB, 2 * TQ, BK), lambda j, i: (j, i, 0, 0)),
                  pl.BlockSpec((None, SB_QB, 8, LANES), lambda j, i: (j, i, 0, 0)),
                  pl.BlockSpec((SB_QB * TQ, LANES), lambda j, i: (i, j)),
                  pl.BlockSpec(prew.shape, lambda j, i: (0, 0)),
                  pl.BlockSpec(prexw.shape, lambda j, i: (0, 0)), wide_spec, wide_spec] + s_in,
        out_specs=[pl.BlockSpec((SB_QB * TQ, LANES), lambda j, i: (i, j)),
                   pl.BlockSpec((s, LANES), lambda j, i: (0, j)),
                   pl.BlockSpec((s, LANES), lambda j, i: (0, j))] + s_out,
        input_output_aliases=s_alias,
        scratch_shapes=[pltpu.VMEM((SB_QB, 2 * TQ, BK), F32), pltpu.VMEM((SB_QB, 2 * TQ, BK), F32),
                        pltpu.VMEM((SB_QB, 2 * TQ, LANES), F32), pltpu.VMEM((s, LANES), F32),
                        pltpu.VMEM((s, LANES), F32)] + s_sems,
        compiler_params=_params(2), name=name)(qkv, qkv, qkv, lt, nb, doa, prew, prexw, a_wide, sg_wide,
                                               *(side.arrays if side else ()))
    return outs[0], outs[1], outs[2], outs[3:]


def _bucket_table():
    i = np.arange(TQ)[:, None]
    j = np.arange(2 * BK)[None, :]
    dist = np.maximum(TQ + i - j, 0)
    max_exact = N_BUCKETS // 2
    df = np.maximum(dist, 1).astype(np.float32)
    large = max_exact + (np.log(df / np.float32(max_exact)) / np.float32(math.log(MAX_DISTANCE / max_exact))
                         * np.float32(N_BUCKETS - max_exact)).astype(np.int32)
    large = np.minimum(large, N_BUCKETS - 1)
    return np.where(dist < max_exact, dist, large).astype(np.int32)


def _swa_align_in(t, lane, g):
    tf = t.astype(F32)
    tr = pltpu.roll(tf, HEAD_DIM, 1)
    gmask = (lane >= HEAD_DIM) == (g == 1)
    top = jnp.where(gmask, jnp.where(g == 0, tf, tr), 0.0)
    bot = jnp.where(gmask, jnp.where(g == 1, tf, tr), 0.0)
    return jnp.concatenate([top, bot], axis=0).astype(BF16)


def _swa_align_out(t, lane, g):
    top, bot = t[:TQ, :], t[TQ:, :]
    top = jnp.where(g == 0, top, pltpu.roll(top, HEAD_DIM, 1))
    bot = jnp.where(g == 1, bot, pltpu.roll(bot, HEAD_DIM, 1))
    return jnp.where(lane < HEAD_DIM, top, bot)


def _swa_bias(bias_ref, bucket_ref, rb_ref, j):
    dist = TQ + lax.broadcasted_iota(jnp.int32, (TQ, 2 * BK), 0) - lax.broadcasted_iota(jnp.int32, (TQ, 2 * BK), 1)
    window = (dist >= 0) & (dist < WINDOW)
    for hh in range(2):
        def add(b, acc):
            return acc + jnp.where(bucket_ref[...] == b, rb_ref[b, 2 * j + hh], 0.0)
        bias = lax.fori_loop(0, N_BUCKETS, add, jnp.zeros((TQ, 2 * BK), F32))
        bias_ref[hh * TQ:(hh + 1) * TQ, :] = jnp.where(window, bias, NEG)


def _swa_probs(qs, k2, bias, own_block, sink_ref, i, j):
    s = _dot_nt(qs, k2) + bias
    s = jnp.where(own_block | (i > 0), s, NEG)
    row1 = lax.broadcasted_iota(jnp.int32, (2 * TQ, 1), 0)
    sink = jnp.where(row1 < TQ, sink_ref[2 * j], sink_ref[2 * j + 1])
    m = jnp.maximum(jnp.max(s, axis=1, keepdims=True), sink)
    e = jnp.exp(s - m)
    es = jnp.exp(sink - m)
    inv = 1.0 / (jnp.sum(e, axis=1, keepdims=True) + es)
    return e * inv, es * inv


def _swa_kv(ref, i):
    prev = pl.multiple_of(jnp.maximum(i - 1, 0) * BK, BK)
    cur = pl.multiple_of(i * BK, BK)
    return jnp.concatenate([ref[pl.ds(prev, BK), :], ref[pl.ds(cur, BK), :]], axis=0), prev, cur


def _swa_fwd(qkv, sinks, rel_bias, name, side=None):
    s = qkv.shape[0]
    nq = s // TQ
    npair = SW_QW // LANES
    qcol = 3 * SB_W // LANES
    bucket = jnp.asarray(_bucket_table())

    def body(q_ref, k_ref, v_ref, bucket_ref, sink_ref, rb_ref, *rest):
        side_in, (o_ref, p_ref), side_out, scratch = _side_refs(side, rest, 2)
        bias_ref, sems = scratch[0], scratch[1:]
        j = pl.program_id(0)
        step = pl.program_id(1)
        if side is not None:
            @pl.when((j == 0) & (step == 0))
            def _():
                side.start(side_in, side_out, *sems)
        g = j // 2
        lane = lax.broadcasted_iota(jnp.int32, (TQ, LANES), 1)

        @pl.when(step == 0)
        def _():
            _swa_bias(bias_ref, bucket_ref, rb_ref, j)

        own_block = lax.broadcasted_iota(jnp.int32, (2 * TQ, 2 * BK), 1) >= BK
        for b in range(SWA_QB):
            i = step * SWA_QB + b
            rows = slice(b * TQ, (b + 1) * TQ)
            qs = _swa_align_in(q_ref[rows, :] * SCALE, lane, g)
            k2, _, _ = _swa_kv(k_ref, i)
            v2, _, _ = _swa_kv(v_ref, i)
            pr, psink = _swa_probs(qs, k2, bias_ref[...], own_block, sink_ref, i, j)
            prb = pr.astype(BF16)
            o_ref[rows, :] = _swa_align_out(_dot(prb, v2), lane, g).astype(BF16)
            p_ref[b, :, :2 * BK] = prb
            p_ref[b, :, 2 * BK:] = jnp.broadcast_to(psink, (2 * TQ, BK)).astype(BF16)
        if side is not None:
            @pl.when((j == npair - 1) & (step == nq // SWA_QB - 1))
            def _():
                side.finish(side_in, side_out, *sems)

    assert nq % SWA_QB == 0
    s_in, s_shape, s_out, s_alias, s_sems = _side_specs(side, 6, 2)
    outs = pl.pallas_call(
        body,
        out_shape=[jax.ShapeDtypeStruct((s, SW_QW), BF16),
                   jax.ShapeDtypeStruct((npair, nq, 2 * TQ, 3 * BK), BF16)] + s_shape,
        grid=(npair, nq // SWA_QB),
        in_specs=[pl.BlockSpec((SWA_QB * TQ, LANES), lambda j, i: (i, qcol + j)),
                  pl.BlockSpec((s, LANES), lambda j, i: (0, qcol + npair)),
                  pl.BlockSpec((s, LANES), lambda j, i: (0, qcol + npair + 1)),
                  pl.BlockSpec((TQ, 2 * BK), lambda j, i: (0, 0)),
                  pl.BlockSpec(memory_space=pltpu.SMEM),
                  pl.BlockSpec(memory_space=pltpu.SMEM)] + s_in,
        out_specs=[pl.BlockSpec((SWA_QB * TQ, LANES), lambda j, i: (i, j)),
                   pl.BlockSpec((None, SWA_QB, 2 * TQ, 3 * BK), lambda j, i: (j, i, 0, 0))] + s_out,
        input_output_aliases=s_alias,
        scratch_shapes=[pltpu.VMEM((2 * TQ, 2 * BK), F32)] + s_sems,
        compiler_params=_params(2), name=name)(qkv, qkv, qkv, bucket, sinks, rel_bias,
                                               *(side.arrays if side else ()))
    return outs[:2], outs[2:]


def _swa_bwd(qkv, ob, dob, probs, name, side=None):
    s = qkv.shape[0]
    nq = s // TQ
    npair = SW_QW // LANES
    qcol = 3 * SB_W // LANES
    bucket = jnp.asarray(_bucket_table())

    def body(q_ref, k_ref, v_ref, o_ref, do_ref, bucket_ref, p_ref, *rest):
        side_in, (dq_ref, dkv_ref, dsink_ref, drel_ref), side_out, scratch = _side_refs(side, rest, 4)
        dsacc_ref, dk_ref, dv_ref = scratch[:3]
        sems = scratch[3:]
        j = pl.program_id(0)
        step = pl.program_id(1)
        if side is not None:
            @pl.when((j == 0) & (step == 0))
            def _():
                side.start(side_in, side_out, *sems)
        g = j // 2
        lane = lax.broadcasted_iota(jnp.int32, (TQ, LANES), 1)
        row8 = lax.broadcasted_iota(jnp.int32, (SW_HEADS, LANES), 0)
        lane8 = lax.broadcasted_iota(jnp.int32, (SW_HEADS, LANES), 1)

        @pl.when((step == 0) & (j == 0))
        def _():
            dk_ref[...] = jnp.zeros_like(dk_ref)
            dv_ref[...] = jnp.zeros_like(dv_ref)
            dsink_ref[...] = jnp.zeros_like(dsink_ref)
            drel_ref[...] = jnp.zeros_like(drel_ref)

        @pl.when(step == 0)
        def _():
            dsacc_ref[...] = jnp.zeros_like(dsacc_ref)

        ds_sum = jnp.zeros(dsacc_ref.shape, F32)
        dsink = jnp.zeros((SW_HEADS, LANES), F32)
        for b in range(SWA_QB):
            i = step * SWA_QB + b
            rows = slice(b * TQ, (b + 1) * TQ)
            qs = _swa_align_in(q_ref[rows, :] * SCALE, lane, g)
            do = do_ref[rows, :]
            dos = _swa_align_in(do, lane, g)
            dof = do.astype(F32) * o_ref[rows, :].astype(F32)
            d0 = jnp.sum(jnp.where(lane < HEAD_DIM, dof, 0.0), axis=1, keepdims=True)
            d1 = jnp.sum(jnp.where(lane >= HEAD_DIM, dof, 0.0), axis=1, keepdims=True)
            delta = jnp.concatenate([d0, d1], axis=0)
            k2, prev, cur = _swa_kv(k_ref, i)
            v2, _, _ = _swa_kv(v_ref, i)
            prb = p_ref[b, :, :2 * BK]
            ds = prb.astype(F32) * (_dot_nt(dos, v2) - delta)
            ds_sum = ds_sum + ds
            sd = p_ref[b, :, 2 * BK:].astype(F32) * delta
            ds0 = -jnp.sum(sd[:TQ, :], axis=0, keepdims=True)
            ds1 = -jnp.sum(sd[TQ:, :], axis=0, keepdims=True)
            dsink = dsink + jnp.where(row8 == 2 * j, ds0, jnp.where(row8 == 2 * j + 1, ds1, 0.0))
            dsb = ds.astype(BF16)
            dq_ref[rows, :] = _swa_align_out(_dot(dsb, k2) * SCALE, lane, g).astype(BF16)
            dk2 = _dot_tn(dsb, qs)
            dv2 = _dot_tn(prb, dos)
            dk_ref[pl.ds(prev, BK), :] += dk2[:BK, :]
            dk_ref[pl.ds(cur, BK), :] += dk2[BK:, :]
            dv_ref[pl.ds(prev, BK), :] += dv2[:BK, :]
            dv_ref[pl.ds(cur, BK), :] += dv2[BK:, :]
        dsacc_ref[...] += ds_sum
        dsink_ref[...] += dsink

        @pl.when(step == nq // SWA_QB - 1)
        def _():
            for hh in range(2):
                def red(b, acc):
                    val = _sum_all(jnp.where(bucket_ref[...] == b, dsacc_ref[hh * TQ:(hh + 1) * TQ, :], 0.0))
                    return jnp.where((row8 == 2 * j + hh) & (lane8 == b), val, acc)
                drel_ref[...] += lax.fori_loop(0, N_BUCKETS, red, jnp.zeros((SW_HEADS, LANES), F32))

        @pl.when((step == nq // SWA_QB - 1) & (j == npair - 1))
        def _():
            dkv_ref[:, :LANES] = dk_ref[...].astype(BF16)
            dkv_ref[:, LANES:] = dv_ref[...].astype(BF16)
            if side is not None:
                side.finish(side_in, side_out, *sems)

    whole = lambda j, i: (0, 0)
    s_in, s_shape, s_out, s_alias, s_sems = _side_specs(side, 7, 4)
    outs = pl.pallas_call(
        body,
        out_shape=[jax.ShapeDtypeStruct((s, SW_QW), BF16), jax.ShapeDtypeStruct((s, 2 * LANES), BF16),
                   jax.ShapeDtypeStruct((SW_HEADS, LANES), F32), jax.ShapeDtypeStruct((SW_HEADS, LANES), F32)] + s_shape,
        grid=(npair, nq // SWA_QB),
        in_specs=[pl.BlockSpec((SWA_QB * TQ, LANES), lambda j, i: (i, qcol + j)),
                  pl.BlockSpec((s, LANES), lambda j, i: (0, qcol + npair)),
                  pl.BlockSpec((s, LANES), lambda j, i: (0, qcol + npair + 1)),
                  pl.BlockSpec((SWA_QB * TQ, LANES), lambda j, i: (i, j)),
                  pl.BlockSpec((SWA_QB * TQ, LANES), lambda j, i: (i, j)),
                  pl.BlockSpec((TQ, 2 * BK), whole),
                  pl.BlockSpec((None, SWA_QB, 2 * TQ, 3 * BK), lambda j, i: (j, i, 0, 0))] + s_in,
        out_specs=[pl.BlockSpec((SWA_QB * TQ, LANES), lambda j, i: (i, j)),
                   pl.BlockSpec((s, 2 * LANES), whole),
                   pl.BlockSpec((SW_HEADS, LANES), whole), pl.BlockSpec((SW_HEADS, LANES), whole)] + s_out,
        input_output_aliases=s_alias,
        scratch_shapes=[pltpu.VMEM((2 * TQ, 2 * BK), F32), pltpu.VMEM((s, LANES), F32),
                        pltpu.VMEM((s, LANES), F32)] + s_sems,
        compiler_params=_params(2), name=name)(qkv, qkv, qkv, ob, dob, bucket, probs,
                                               *(side.arrays if side else ()))
    return outs[:4], outs[4:]


def _acc_init(i, *refs):
    @pl.when(i == 0)
    def _():
        for r in refs:
            r[...] = jnp.zeros_like(r)


def _ple_fwd_loss(p, x2, g, wpe, wpg, target, g_final, name):
    d = x2.shape[1]

    def body(p_ref, x_ref, t_ref, g_ref, wpe_ref, wpg_ref, gf_ref,
             pb_ref, h_ref, dpe_ref, dgt_ref, dx2_ref, dgf_ref, loss_ref, dg_ref):
        _acc_init(pl.program_id(0), dgf_ref, loss_ref, dg_ref)
        pb = p_ref[...].astype(BF16)
        pb_ref[...] = pb
        pe = _dot_cols(pb, wpe_ref)
        x = x_ref[...]
        gv = g_ref[...]
        hb = _rms(x, gv).astype(BF16)
        h_ref[...] = hb
        sg = _sigmoid(_dot(hb, wpg_ref[...]))
        x3 = x + pe * sg
        gf = gf_ref[...]
        diff = _rms(x3, gf) - t_ref[...]
        loss_ref[...] += 0.5 * jnp.sum(jnp.mean(jnp.square(diff), axis=-1, keepdims=True), axis=0, keepdims=True)
        dx3, dgf = _rms_bwd(diff * (1.0 / d), x3, gf)
        dgf_ref[...] += dgf
        dpe_ref[...] = (dx3 * sg).astype(BF16)
        dgt = (dx3 * pe * sg * (1.0 - sg)).astype(BF16)
        dgt_ref[...] = dgt
        dx, dg = _rms_bwd(_dot_nt(dgt, wpg_ref[...]), x, gv)
        dx2_ref[...] = dx3 + dx
        dg_ref[...] += dg

    return _rowcall(name, body, [p, x2, target], [g, wpe, wpg, g_final],
                    [(p.shape[1], BF16), (d, BF16), (d, BF16), (d, BF16), (d, F32)],
                    [((1, d), F32), ((1, LANES), F32), ((1, d), F32)])


def _ple_bwd(dx3, pe, gt, x2, g, wpg, name):
    d = x2.shape[1]

    def body(dx3_ref, pe_ref, gt_ref, x_ref, g_ref, w_ref, dpe_ref, dgt_ref, dx2_ref, dg_ref):
        _acc_init(pl.program_id(0), dg_ref)
        dx3 = dx3_ref[...]
        sg = _sigmoid(gt_ref[...])
        dpe_ref[...] = (dx3 * sg).astype(BF16)
        dgt = (dx3 * pe_ref[...] * sg * (1.0 - sg)).astype(BF16)
        dgt_ref[...] = dgt
        dx, dg = _rms_bwd(_dot_nt(dgt, w_ref[...]), x_ref[...], g_ref[...])
        dx2_ref[...] = dx3 + dx
        dg_ref[...] += dg

    return _rowcall(name, body, [dx3, pe, gt, x2], [g, wpg], [(d, BF16), (d, BF16), (d, F32)], [((1, d), F32)])


def _ff_bwd(dx2, r, x1, g, w1, w2, name, side=None):
    d = x1.shape[1]
    dff = r.shape[1]

    def body(dx2_ref, r_ref, x_ref, g_ref, w1_ref, w2_ref, du_ref, dxb_ref, dx1_ref, dx1b_ref, dg_ref):
        _acc_init(pl.program_id(0), dg_ref)
        dx2 = dx2_ref[...]
        dxb = dx2.astype(BF16)
        dxb_ref[...] = dxb
        du = (_dot_nt(dxb, w2_ref[...]) * (2.0 * r_ref[...].astype(F32))).astype(BF16)
        du_ref[...] = du
        dx, dg = _rms_bwd(_dot_cols_t(du, w1_ref), x_ref[...], g_ref[...])
        dx1 = dx2 + dx
        dx1_ref[...] = dx1
        dx1b_ref[...] = dx1.astype(BF16)
        dg_ref[...] += dg

    return _rowcall(name, body, [dx2, r, x1], [g, w1, w2], [(dff, BF16), (d, BF16), (d, F32), (d, BF16)],
                    [((1, d), F32)], tm=256, side=side)


def _mixer_bwd(dx1b, gates, oa, ob, wo, wua, wub, name, side=None):
    d = dx1b.shape[1]

    def body(dx_ref, gate_ref, oa_ref, ob_ref, wo_ref, wua_ref, wub_ref,
             dya_ref, dyb_ref, dgate_ref, doa_ref, dob_ref):
        dm = _dot_nt(dx_ref[...], wo_ref[...])
        sa = _sigmoid(gate_ref[:, :d])
        sb = _sigmoid(gate_ref[:, d:])
        ya = _dot_cols(oa_ref[...], wua_ref)
        yb = _dot_cols(ob_ref[...], wub_ref)
        dya = (dm * sa).astype(BF16)
        dyb = (dm * sb).astype(BF16)
        dya_ref[...] = dya
        dyb_ref[...] = dyb
        dgate_ref[:, :d] = (dm * ya * sa * (1.0 - sa)).astype(BF16)
        dgate_ref[:, d:] = (dm * yb * sb * (1.0 - sb)).astype(BF16)
        doa_ref[...] = _dot_cols_t(dya, wua_ref).astype(BF16)
        dob_ref[...] = _dot_cols_t(dyb, wub_ref).astype(BF16)

    return _rowcall(name, body, [dx1b, gates, oa, ob], [wo, wua, wub],
                    [(d, BF16), (d, BF16), (2 * d, BF16), (SB_W, BF16), (SW_QW, BF16)], side=side)


def _inproj_bwd(pieces, dx1, x, g, wt, name, side=None):
    d = x.shape[1]
    n = len(pieces)
    offsets = [sum(pc.shape[1] for pc in pieces[:p]) for p in range(n + 1)]

    def body(*refs):
        dx1_ref, x_ref, g_ref, w_ref, dx_ref, dg_ref = refs[n:]
        _acc_init(pl.program_id(0), dg_ref)
        dh = _dot(refs[0][...], w_ref[:offsets[1], :])
        for p in range(1, n):
            dh = dh + _dot(refs[p][...], w_ref[offsets[p]:offsets[p + 1], :])
        dx, dg = _rms_bwd(dh, x_ref[...], g_ref[...])
        dx_ref[...] = dx1_ref[...] + dx
        dg_ref[...] += dg

    return _rowcall(name, body, list(pieces) + [dx1, x], [g, wt], [(d, F32)], [((1, d), F32)], side=side)


def _tile(n, cap):
    assert n % LANES == 0
    return max(t for t in range(LANES, min(n, cap) + 1, LANES) if n % t == 0)


def _mm_tn(a, b, name, nshard=1):
    s, ka = a.shape
    nb = b.shape[1]
    n = nb // nshard
    ta = _tile(ka, 512)
    tb = _tile(n, 1024)
    per = n // tb

    def body(a_ref, b_ref, o_ref):
        o_ref[...] = _dot_tn(a_ref[...].astype(BF16), b_ref[...].astype(BF16))

    return pl.pallas_call(
        body, out_shape=jax.ShapeDtypeStruct((nshard, ka, n), F32), grid=(nb // tb, ka // ta),
        in_specs=[pl.BlockSpec((s, ta), lambda jb, ia: (0, ia)), pl.BlockSpec((s, tb), lambda jb, ia: (0, jb))],
        out_specs=pl.BlockSpec((None, ta, tb), lambda jb, ia: (jb // per, ia, jb % per)),
        compiler_params=_params(2), name=name)(a, b)


def _mm_tn_pieces(pieces, b, name, side=None):
    s, nb = b.shape
    ta = 256
    n_in = len(pieces) + 1
    tiles = [pc.shape[1] // ta for pc in pieces]
    assert all(pc.shape[1] % ta == 0 for pc in pieces)
    starts = [sum(tiles[:p]) for p in range(len(pieces))]
    tb = _tile(nb, 1024)
    grid = (nb // tb, sum(tiles))

    def body(*refs):
        a_refs, b_ref = refs[:n_in - 1], refs[n_in - 1]
        side_in, (o_ref,), side_out, sems = _side_refs(side, refs[n_in:], 1)
        jb, ia = pl.program_id(0), pl.program_id(1)
        if side is not None:
            @pl.when((jb == 0) & (ia == 0))
            def _():
                side.start(side_in, side_out, *sems)
        for p in range(len(pieces)):
            @pl.when((ia >= starts[p]) & (ia < starts[p] + tiles[p]))
            def _(p=p):
                o_ref[...] = _dot_tn(a_refs[p][...], b_ref[...])
        if side is not None:
            @pl.when((jb == grid[0] - 1) & (ia == grid[1] - 1))
            def _():
                side.finish(side_in, side_out, *sems)

    def piece_spec(p):
        return pl.BlockSpec((s, ta), lambda jb, ia: (0, jnp.clip(ia - starts[p], 0, tiles[p] - 1)))

    s_in, s_shape, s_out, s_alias, s_sems = _side_specs(side, n_in, 1)
    outs = pl.pallas_call(
        body, out_shape=[jax.ShapeDtypeStruct((sum(tiles) * ta, nb), F32)] + s_shape, grid=grid,
        in_specs=[piece_spec(p) for p in range(len(pieces))] + [pl.BlockSpec((s, tb), lambda jb, ia: (0, jb))] + s_in,
        out_specs=[pl.BlockSpec((ta, tb), lambda jb, ia: (ia, jb))] + s_out,
        input_output_aliases=s_alias, scratch_shapes=s_sems,
        compiler_params=_params(2), name=name)(*pieces, b, *(side.arrays if side else ()))
    return outs[0] if side is None else (outs[0], outs[1:])


def _place():
    return lax.axis_index("x"), lax.axis_index("y"), lax.axis_index("c")


def _chip_peer(x, y, k):
    return (x ^ (k >> 1), y ^ (k & 1))


def _row_tile(k, cap=544):
    return max(t for t in range(32, min(k, cap) + 1, 32) if k % t == 0)


def _cast_bf16(w, r, name):
    l, k, n = w.shape
    assert l == 2
    tk = _row_tile(k)

    def body(r_ref, w_ref, o0_ref, o1_ref):
        o0_ref[...] = w_ref[0].astype(BF16)
        o1_ref[...] = w_ref[1].astype(BF16)

    out_spec = pl.BlockSpec((None, tk, n), lambda i, r_ref: (r_ref[0], i, 0))
    return pl.pallas_call(
        body, out_shape=[jax.ShapeDtypeStruct((4, k, n), BF16)] * 2,
        grid_spec=pltpu.PrefetchScalarGridSpec(
            num_scalar_prefetch=1, grid=(k // tk,),
            in_specs=[pl.BlockSpec((l, tk, n), lambda i, r_ref: (0, i, 0))],
            out_specs=[out_spec, out_spec]),
        compiler_params=_params(1), name=name)(r, w)


class _Exchange(NamedTuple):
    arrays: tuple
    aliased: tuple
    sems: tuple
    start: Callable
    finish: Callable


def _both(a, b):
    na, ma, ka = len(a.arrays), len(a.aliased), 2 * len(a.sems)

    def start(ins, outs, *sems):
        a.start(ins[:na], outs[:ma], *sems[:ka])
        b.start(ins[na:], outs[ma:], *sems[ka:])

    def finish(ins, outs, *sems):
        a.finish(ins[:na], outs[:ma], *sems[:ka])
        b.finish(ins[na:], outs[ma:], *sems[ka:])

    return _Exchange(a.arrays + b.arrays, a.aliased + tuple(na + i for i in b.aliased), a.sems + b.sems, start, finish)


def _all_gather(bufs):
    nt = len(bufs)

    def ici(t, ins, outs, send_sems, recv_sems, x, y, c, r, k):
        return pltpu.make_async_remote_copy(
            src_ref=ins[t].at[r, c], dst_ref=outs[t].at[r, c], send_sem=send_sems.at[t, k - 1],
            recv_sem=recv_sems.at[t, k - 1], device_id=(*_chip_peer(x, y, k), c), device_id_type=MESH)

    def d2d(t, outs, send_sems, recv_sems, x, y, c, r, k, half):
        slab = outs[t].at[r ^ k, half]
        return pltpu.make_async_remote_copy(
            src_ref=slab, dst_ref=slab, send_sem=send_sems.at[t, 2 + k], recv_sem=recv_sems.at[t, 2 + k],
            device_id=(x, y, 1 - c), device_id_type=MESH)

    def start(ins, outs, send_sems, recv_sems):
        x, y, c = _place()
        r = 2 * x + y
        for t in range(nt):
            for k in (1, 2, 3):
                ici(t, ins, outs, send_sems, recv_sems, x, y, c, r, k).start()

    def finish(ins, outs, send_sems, recv_sems):
        x, y, c = _place()
        r = 2 * x + y
        for t in range(nt):
            for k in (1, 2, 3):
                slab = outs[t].at[r ^ k, c]
                pltpu.make_async_remote_copy(
                    src_ref=slab, dst_ref=slab, send_sem=send_sems.at[t, k - 1], recv_sem=recv_sems.at[t, k - 1],
                    device_id=(x, y, 1 - c), device_id_type=MESH).wait_recv()
                d2d(t, outs, send_sems, recv_sems, x, y, c, r, k, c).start()
        for t in range(nt):
            for k in (1, 2, 3):
                d2d(t, outs, send_sems, recv_sems, x, y, c, r, k, 1 - c).wait_recv()
        for t in range(nt):
            for k in (1, 2, 3):
                ici(t, ins, outs, send_sems, recv_sems, x, y, c, r, k).wait_send()
                d2d(t, outs, send_sems, recv_sems, x, y, c, r, k, c).wait_send()

    return _Exchange(tuple(bufs), tuple(range(nt)), ((nt, 6),), start, finish)


def _run_exchange(name, ex):
    n_in, n_out = len(ex.arrays), len(ex.aliased)

    def body(*refs):
        ins, outs = refs[:n_in], refs[n_in:n_in + n_out]
        send_sems, recv_sems = refs[n_in + n_out:]
        ex.start(ins, outs, send_sems, recv_sems)
        ex.finish(ins, outs, send_sems, recv_sems)

    any_spec = pl.BlockSpec(memory_space=pl.ANY)
    return pl.pallas_call(
        body, out_shape=[jax.ShapeDtypeStruct(ex.arrays[a].shape, ex.arrays[a].dtype) for a in ex.aliased],
        in_specs=[any_spec] * n_in, out_specs=[any_spec] * n_out,
        input_output_aliases={a: o for o, a in enumerate(ex.aliased)},
        scratch_shapes=[pltpu.SemaphoreType.DMA(shape) for shape in ex.sems for _ in range(2)],
        name=name)(*ex.arrays)


def _rs_to_sibling(grads):
    nt = len(grads)
    landing = [lax.empty((4,) + g.shape[2:], F32) for g in grads]

    def copies(ins, outs, send_sems, recv_sems):
        x, y, c = _place()
        return [pltpu.make_async_remote_copy(
            src_ref=ins[t].at[:, 1 - c], dst_ref=outs[t], send_sem=send_sems.at[t], recv_sem=recv_sems.at[t],
            device_id=(x, y, 1 - c), device_id_type=MESH) for t in range(nt)]

    def start(ins, outs, send_sems, recv_sems):
        for cp in copies(ins, outs, send_sems, recv_sems):
            cp.start()

    def finish(ins, outs, send_sems, recv_sems):
        for cp in copies(ins, outs, send_sems, recv_sems):
            cp.wait()

    return _Exchange(tuple(grads) + tuple(landing), tuple(range(nt, 2 * nt)), ((nt,),), start, finish)


def _add_half(g, recv, cr, name):
    _, _, k2, n = g.shape
    tk = _row_tile(k2)

    def body(cr_ref, g_ref, r_ref, sums_ref, mine_ref):
        val = (g_ref[...] + r_ref[...]).astype(BF16)
        sums_ref[...] = val

        @pl.when(pl.program_id(1) == cr_ref[1])
        def _():
            mine_ref[...] = val

    return pl.pallas_call(
        body, out_shape=[jax.ShapeDtypeStruct((4, k2, n), BF16)] * 2,
        grid_spec=pltpu.PrefetchScalarGridSpec(
            num_scalar_prefetch=1, grid=(k2 // tk, 4),
            in_specs=[pl.BlockSpec((None, None, tk, n), lambda i, q, cr_ref: (q, cr_ref[0], i, 0)),
                      pl.BlockSpec((None, tk, n), lambda i, q, cr_ref: (q, i, 0))],
            out_specs=[pl.BlockSpec((None, tk, n), lambda i, q, cr_ref: (q, i, 0)),
                       pl.BlockSpec((None, tk, n), lambda i, q, cr_ref: (cr_ref[1], i, 0))]),
        compiler_params=_params(2), name=name)(cr, g, recv)


def _rs_to_chips(sums, parts):
    nt = len(sums)

    def copies(ins, outs, send_sems, recv_sems):
        x, y, c = _place()
        r = 2 * x + y
        return [pltpu.make_async_remote_copy(
            src_ref=ins[t].at[r ^ k], dst_ref=outs[t].at[r], send_sem=send_sems.at[t, k - 1],
            recv_sem=recv_sems.at[t, k - 1], device_id=(*_chip_peer(x, y, k), c), device_id_type=MESH)
            for t in range(nt) for k in (1, 2, 3)]

    def start(ins, outs, send_sems, recv_sems):
        for cp in copies(ins, outs, send_sems, recv_sems):
            cp.start()

    def finish(ins, outs, send_sems, recv_sems):
        for cp in copies(ins, outs, send_sems, recv_sems):
            cp.wait()

    return _Exchange(tuple(sums) + tuple(parts), tuple(range(nt, 2 * nt)), ((nt, 3),), start, finish)


def _sum4(parts, cr, name):
    _, k2, n = parts.shape
    tk = _row_tile(k2)

    def body(cr_ref, p_ref, o_ref):
        p = p_ref[...].astype(F32)
        o_ref[...] = ((p[0] + p[1]) + p[2]) + p[3]

    return pl.pallas_call(
        body, out_shape=jax.ShapeDtypeStruct((2, k2, n), F32),
        grid_spec=pltpu.PrefetchScalarGridSpec(
            num_scalar_prefetch=1, grid=(k2 // tk,),
            in_specs=[pl.BlockSpec((4, tk, n), lambda i, cr_ref: (0, i, 0))],
            out_specs=pl.BlockSpec((None, tk, n), lambda i, cr_ref: (cr_ref[0], i, 0))),
        compiler_params=_params(1), name=name)(cr, parts)


def _exchange_halves(both):
    nt = len(both)

    def copies(ins, outs, send_sems, recv_sems):
        x, y, c = _place()
        return [pltpu.make_async_remote_copy(
            src_ref=ins[t].at[c], dst_ref=outs[t].at[c], send_sem=send_sems.at[t], recv_sem=recv_sems.at[t],
            device_id=(x, y, 1 - c), device_id_type=MESH) for t in range(nt)]

    def start(ins, outs, send_sems, recv_sems):
        for cp in copies(ins, outs, send_sems, recv_sems):
            cp.start()

    def finish(ins, outs, send_sems, recv_sems):
        for cp in copies(ins, outs, send_sems, recv_sems):
            cp.wait()

    return _Exchange(tuple(both), tuple(range(nt)), ((nt,),), start, finish)


def _adamw_math(w, g, m, v):
    m = ADAM_B1 * m + (1.0 - ADAM_B1) * g
    v = ADAM_B2 * v + (1.0 - ADAM_B2) * jnp.square(g)
    m_hat = m / (1.0 - ADAM_B1 ** ADAM_STEP)
    v_hat = v / (1.0 - ADAM_B2 ** ADAM_STEP)
    delta = -ADAM_LR * (m_hat / (jnp.sqrt(v_hat) + ADAM_EPS) + ADAM_WD * w)
    return delta, m, v


def _adamw(w, m, v, g0, g1, name):
    _, k, n = w.shape
    tk = _row_tile(k)
    nk = k // tk

    def body(w_ref, m_ref, v_ref, g0_ref, g1_ref, grad_ref, delta_ref, nm_ref, nv_ref):
        g = jnp.where(pl.program_id(0) == 0, g0_ref[...], g1_ref[...])
        delta, nm, nv = _adamw_math(w_ref[...], g, m_ref[...], v_ref[...])
        grad_ref[...] = g
        delta_ref[...] = delta
        nm_ref[...] = nm
        nv_ref[...] = nv

    lay = pl.BlockSpec((None, tk, n), lambda a, i: (a, i, 0))
    g0_spec = pl.BlockSpec((tk, n), lambda a, i: (jnp.where(a == 0, i, nk - 1), 0))
    g1_spec = pl.BlockSpec((tk, n), lambda a, i: (jnp.where(a == 1, i, 0), 0))
    return pl.pallas_call(
        body, out_shape=[jax.ShapeDtypeStruct(w.shape, F32)] * 4, grid=(2, nk),
        in_specs=[lay, lay, lay, g0_spec, g1_spec], out_specs=[lay] * 4,
        compiler_params=_params(2), name=name)(w, m, v, g0, g1)


def _small_allreduce_adamw(gpart, w, m, v):
    shape = gpart.shape

    def body(g_ref, w_ref, m_ref, v_ref, gsum_ref, delta_ref, nm_ref, nv_ref, recv_ref, send_sems, recv_sems):
        x, y, c = _place()
        me = 4 * x + 2 * y + c
        recv_ref[me] = g_ref[...]
        cps = []
        for k in range(1, 8):
            peer = (x ^ (k >> 2), y ^ ((k >> 1) & 1), c ^ (k & 1))
            cp = pltpu.make_async_remote_copy(
                src_ref=g_ref, dst_ref=recv_ref.at[me], send_sem=send_sems.at[k - 1], recv_sem=recv_sems.at[k - 1],
                device_id=peer, device_id_type=MESH)
            cp.start()
            cps.append(cp)
        for cp in cps:
            cp.wait()
        g = recv_ref[0]
        for dev in range(1, 8):
            g = g + recv_ref[dev]
        delta, nm, nv = _adamw_math(w_ref[...], g, m_ref[...], v_ref[...])
        gsum_ref[...] = g
        delta_ref[...] = delta
        nm_ref[...] = nm
        nv_ref[...] = nv

    vm = pl.BlockSpec(memory_space=pltpu.VMEM)
    return pl.pallas_call(
        body, out_shape=[jax.ShapeDtypeStruct(shape, F32)] * 4, in_specs=[vm] * 4, out_specs=[vm] * 4,
        scratch_shapes=[pltpu.VMEM((8,) + shape, F32), pltpu.SemaphoreType.DMA((7,)), pltpu.SemaphoreType.DMA((7,))],
        name="small_allreduce_adamw")(gpart, w, m, v)


BIG = ("w_in", "w_up_a", "w_up_b", "w_o", "w_ff1", "w_ff2", "w_pe", "w_pg")
COL_SHARDED = ("w_in", "w_up_a", "w_up_b", "w_ff1", "w_pe")
ROW_SHARDED = ("w_o", "w_ff2", "w_pg")
SMALL_ROWS = 16


def _pack_small(g_mix, g_mlp, g_pe, g_final, sinks, rel_bias, loss=None):
    d = g_final.shape[0]
    row = lambda v: jnp.pad(v.reshape(1, -1), ((0, 0), (0, d - v.size)))
    rows = [g_mix, g_mlp, g_pe, g_final.reshape(1, d),
            jnp.zeros((1, d), F32) if loss is None else row(loss), row(sinks), row(rel_bias)]
    out = jnp.concatenate(rows, axis=0)
    return jnp.pad(out, ((0, SMALL_ROWS - out.shape[0]), (0, 0)))


def _unpack_small(a, sinks_shape, rel_shape):
    return (a[0:2], a[2:4], a[4:6], a[6], a[8, :sinks_shape[0] * sinks_shape[1]].reshape(sinks_shape),
            a[9, :rel_shape[0] * rel_shape[1]].reshape(rel_shape))


def kernel(x, p, w_in, w_up_a, w_up_b, w_o, w_ff1, w_ff2, w_pe, w_pg, g_mix, g_mlp, g_pe, g_final, sinks, rel_bias, loss_target, m_w_in, m_w_up_a, m_w_up_b, m_w_o, m_w_ff1, m_w_ff2, m_w_pe, m_w_pg, m_g_mix, m_g_mlp, m_g_pe, m_g_final, m_sinks, m_rel_bias, v_w_in, v_w_up_a, v_w_up_b, v_w_o, v_w_ff1, v_w_ff2, v_w_pe, v_w_pg, v_g_mix, v_g_mlp, v_g_pe, v_g_final, v_sinks, v_rel_bias):
    depth = w_in.shape[0]
    assert depth == 2
    x0 = x[0]
    target = loss_target[0]
    d = x0.shape[1]
    wl = dict(w_in=w_in, w_up_a=w_up_a, w_up_b=w_up_b, w_o=w_o, w_ff1=w_ff1, w_ff2=w_ff2, w_pe=w_pe, w_pg=w_pg)
    ml = dict(w_in=m_w_in, w_up_a=m_w_up_a, w_up_b=m_w_up_b, w_o=m_w_o, w_ff1=m_w_ff1, w_ff2=m_w_ff2, w_pe=m_w_pe, w_pg=m_w_pg)
    vl = dict(w_in=v_w_in, w_up_a=v_w_up_a, w_up_b=v_w_up_b, w_o=v_w_o, w_ff1=v_w_ff1, w_ff2=v_w_ff2, w_pe=v_w_pe, w_pg=v_w_pg)
    c_idx = lax.axis_index("c").astype(jnp.int32)
    r_idx = (2 * lax.axis_index("x") + lax.axis_index("y")).astype(jnp.int32)
    cr = jnp.stack([c_idx, r_idx])

    wl["w_in"], ml["w_in"], vl["w_in"] = (jnp.swapaxes(a, 1, 2) for a in (w_in, m_w_in, v_w_in))

    bufs = {}
    for n in BIG:
        k, nn = wl[n].shape[1:]
        for l, b in enumerate(_cast_bf16(wl[n], r_idx.reshape(1), "cast_" + n)):
            bufs[n, l] = b.reshape(4, 2, k // 2, nn)

    def gather(keys, run):
        for key, b in zip(keys, run(_all_gather([bufs[key] for key in keys]))):
            bufs[key] = b

    def gathered(n, l):
        _, _, k2, nn = bufs[n, l].shape
        if n in ROW_SHARDED or n == "w_in":
            return bufs[n, l].reshape(8 * k2, nn)
        return bufs[n, l].reshape(4, 2 * k2, nn)

    gather([("w_in", 0)], lambda ex: _run_exchange("all_gather_first", ex))

    full = {n: [None] * depth for n in BIG}
    saved = []
    xi = x0
    for i in range(depth):
        st = dict(x0=xi)
        gm = g_mix[i].reshape(1, d)
        full["w_in"][i] = gathered("w_in", i)
        st["h1"], st["qkv"], st["gates"] = _inproj_fwd(xi, gm, full["w_in"][i], f"inproj_fwd_{i}")

        def attend(ex):
            (st["oa"], st["lt"], st["nb"], st["a_wide"], st["sg_wide"]), filled = _sb_fwd(st["qkv"], f"sb_fwd_{i}", ex)
            return filled

        with_window = [("w_in", 1)] if i == 0 else [(n, i) for n in ("w_up_a", "w_up_b", "w_o")]
        gather([(n, i) for n in BIG if n != "w_in" and (n, i) not in with_window], attend)

        def window(ex):
            (st["ob"], st["probs"]), filled = _swa_fwd(st["qkv"], sinks[i], rel_bias, f"swa_fwd_{i}", ex)
            return filled

        gather(with_window, window)
        for n in BIG:
            if n != "w_in":
                full[n][i] = gathered(n, i)
        st["m"], st["x1"] = _mixer_fwd(st["oa"], st["ob"], st["gates"], xi, full["w_up_a"][i], full["w_up_b"][i],
                                       full["w_o"][i], f"mixer_fwd_{i}")
        st["h2"], st["u"], st["a"], st["x2"] = _ff_fwd(st["x1"], g_mlp[i].reshape(1, d), full["w_ff1"][i],
                                                       full["w_ff2"][i], f"ff_fwd_{i}")
        ple_args = (p[i, 0], st["x2"], g_pe[i].reshape(1, d), full["w_pe"][i], full["w_pg"][i])
        if i < depth - 1:
            st["pb"], st["h3"], st["pe"], st["gt"], xi = _ple_fwd(*ple_args, f"ple_fwd_{i}")
        else:
            st["pb"], st["h3"], st["dpe"], st["dgt"], st["dx2"], dg_final, loss_part, st["dg_pe"] = _ple_fwd_loss(
                *ple_args, target, g_final.reshape(1, d), f"ple_fwd_loss_{i}")
        saved.append(st)

    gw = {n: [None] * depth for n in BIG}
    reduced = {}

    chip_sums = {}

    def by_halves(keys):
        tensors = []
        for n, l in keys:
            g = gw[n][l]
            if n in ROW_SHARDED:
                ka, nb = g.shape[1:]
                g = g.reshape(4, ka // 4, nb)
            _, k, nn = g.shape
            tensors.append(g.reshape(4, 2, k // 2, nn))
        return tensors

    def add_halves(keys, tensors, landed):
        for (n, l), g, r in zip(keys, tensors, landed):
            chip_sums[n, l] = _add_half(g, r, cr, f"add_half_{n}_{l}")

    def to_sibling(keys, run):
        tensors = by_halves(keys)
        add_halves(keys, tensors, run(_rs_to_sibling(tensors)))

    def to_chips(keys):
        return _rs_to_chips([chip_sums[k][0] for k in keys], [chip_sums[k][1] for k in keys])

    halves = {}

    def sum_chips(keys, parts):
        for (n, l), pc in zip(keys, parts):
            halves[n, l] = _sum4(pc, cr, f"sum4_{n}_{l}")

    def swap_halves(keys):
        def store(filled):
            for key, both in zip(keys, filled):
                reduced[key] = both
        return _exchange_halves([halves[k] for k in keys]), store

    dg_mix, dg_mlp, dg_pe, dsinks = [None] * depth, [None] * depth, [None] * depth, [None] * depth
    drel = jnp.zeros((SW_HEADS, LANES), F32)
    for i in reversed(range(depth)):
        st = saved[i]
        if i == depth - 1:
            dpe, dgt, dx2, dg_pe[i] = st["dpe"], st["dgt"], st["dx2"], st["dg_pe"]
        else:
            dpe, dgt, dx2, dg_pe[i] = _ple_bwd(dx, st["pe"], st["gt"], st["x2"], g_pe[i].reshape(1, d),
                                               full["w_pg"][i], f"ple_bwd_{i}")
        gw["w_pe"][i] = _mm_tn(st["pb"], dpe, f"dw_pe_{i}", 4)
        gw["w_pg"][i] = _mm_tn(st["h3"], dgt, f"dw_pg_{i}")
        ff_args = (dx2, st["u"], st["x1"], g_mlp[i].reshape(1, d), full["w_ff1"][i], full["w_ff2"][i], f"ff_bwd_{i}")
        if i == 0:
            (du, dx2b, dx1, dx1b, dg_mlp[i]), parts = _ff_bwd(*ff_args, to_chips([("w_in", 1)]))
            sum_chips([("w_in", 1)], parts)
        else:
            du, dx2b, dx1, dx1b, dg_mlp[i] = _ff_bwd(*ff_args)
        gw["w_ff2"][i] = _mm_tn(st["a"], dx2b, f"dw_ff2_{i}")
        gw["w_ff1"][i] = _mm_tn(st["h2"], du, f"dw_ff1_{i}", 4)
        gw["w_o"][i] = _mm_tn(st["m"], dx1b, f"dw_o_{i}")
        early = [(n, i) for n in ("w_pe", "w_pg", "w_ff2", "w_ff1", "w_o")]

        def mixer(ex):
            (dya, dyb, dgates, doa, dob), landed = _mixer_bwd(
                dx1b, st["gates"], st["oa"], st["ob"], full["w_o"][i], full["w_up_a"][i], full["w_up_b"][i],
                f"mixer_bwd_{i}", ex)
            st.update(dya=dya, dyb=dyb, dgates=dgates, doa=doa, dob=dob)
            return landed

        to_sibling(early, mixer)
        dgates = st["dgates"]
        gw["w_up_a"][i] = _mm_tn(st["oa"], st["dya"], f"dw_up_a_{i}", 4)
        gw["w_up_b"][i] = _mm_tn(st["ob"], st["dyb"], f"dw_up_b_{i}", 4)
        late = [("w_up_a", i), ("w_up_b", i)]
        late_halves = by_halves(late)
        keys = early + late
        with_sb = [(n, i) for n in ("w_ff1", "w_o", "w_pg", "w_pe")]
        with_swa = [(n, i) for n in ("w_ff2", "w_up_a", "w_up_b")]
        dqa, dka, dva, filled = _sb_bwd(st["qkv"], st["lt"], st["nb"], st["a_wide"], st["sg_wide"], st["doa"],
                                        f"sb_bwd_{i}", _both(_rs_to_sibling(late_halves), to_chips(with_sb)))
        add_halves(late, late_halves, filled[:len(late)])
        sum_chips(with_sb, filled[len(late):])
        (dqb, dkvb, dsk, drl), parts = _swa_bwd(st["qkv"], st["ob"], st["dob"], st["probs"], f"swa_bwd_{i}",
                                                to_chips(with_swa))
        sum_chips(with_swa, parts)
        dsinks[i] = dsk[:, 0]
        drel = drel + drl
        dproj = [dqa, dka, dva, dqb, dkvb, dgates]
        swap, store = swap_halves(keys + ([("w_in", 1)] if i == 0 else []))
        dw_in_t, filled = _mm_tn_pieces(dproj, st["h1"], f"dw_in_{i}", swap)
        store(filled)
        gw["w_in"][i] = dw_in_t.reshape(4, dw_in_t.shape[0] // 4, d)
        if i == 1:
            def inproj(ex):
                (dx, dg_mix[i]), landed = _inproj_bwd(dproj, dx1, st["x0"], g_mix[i].reshape(1, d),
                                                      full["w_in"][i], f"inproj_bwd_{i}", ex)
                st["dx"] = dx
                return landed

            to_sibling([("w_in", 1)], inproj)
            dx = st["dx"]
        else:
            keys = [("w_in", 0)]
            to_sibling(keys, lambda ex: _run_exchange("rs_to_sibling_last", ex))
            (dx, dg_mix[i]), parts = _inproj_bwd(dproj, dx1, st["x0"], g_mix[i].reshape(1, d),
                                                 full["w_in"][i], f"inproj_bwd_{i}", to_chips(keys))
            sum_chips(keys, parts)
            swap, store = swap_halves(keys)
            store(_run_exchange("exchange_halves_last", swap))
    grad_x = dx[None]

    outs = {}
    for n in BIG:
        g0, g1 = (reduced[n, l].reshape(wl[n].shape[1:]) for l in range(depth))
        outs[n] = _adamw(wl[n], ml[n], vl[n], g0, g1, "adamw_" + n)
    outs["w_in"] = [jnp.swapaxes(a, 1, 2) for a in outs["w_in"]]

    drel_bias = drel[:, :N_BUCKETS].T
    gsmall = _pack_small(jnp.concatenate(dg_mix, 0), jnp.concatenate(dg_mlp, 0), jnp.concatenate(dg_pe, 0),
                         dg_final[0], jnp.stack(dsinks), drel_bias, loss_part[0, :1])
    wsmall = _pack_small(g_mix, g_mlp, g_pe, g_final, sinks, rel_bias)
    msmall = _pack_small(m_g_mix, m_g_mlp, m_g_pe, m_g_final, m_sinks, m_rel_bias)
    vsmall = _pack_small(v_g_mix, v_g_mlp, v_g_pe, v_g_final, v_sinks, v_rel_bias)
    small = _small_allreduce_adamw(gsmall, wsmall, msmall, vsmall)
    loss = small[0][7, 0]
    small = [_unpack_small(a, sinks.shape, rel_bias.shape) for a in small]

    result = [loss, grad_x]
    for kind in range(4):
        result += [outs[n][kind] for n in BIG]
        result += list(small[kind])
    return tuple(result)
```
